```python
import math
import jax, jax.numpy as jnp
from jax import lax
import numpy as np

D_MODEL = 1024
BATCH = 32
SEQ = 256
DEPTH = 1
DEC_BATCH = 8
DEC_SEQ = 1024
PAST_LEN = 512

GRID_W = 64
GLA_HEADS = 4
DK_HEAD = 128
DV_HEAD = 256
GLA_DK = GLA_HEADS * DK_HEAD
GLA_DV = GLA_HEADS * DV_HEAD
DECAY_RANK = 16
GATE_NORMALIZER = 16.0
CHUNK = 16
FNET_GROUPS = 4
FNET_GROUP_DIM = 128
FNET_DIM = FNET_GROUPS * FNET_GROUP_DIM
N_BRANCHES = 2
IN_SPLITS = (GLA_DK, 2 * GLA_DK, 2 * GLA_DK + GLA_DV, 2 * GLA_DK + 2 * GLA_DV,
             2 * GLA_DK + 2 * GLA_DV + DECAY_RANK,
             2 * GLA_DK + 2 * GLA_DV + DECAY_RANK + FNET_DIM)
IN_COLS = 2 * GLA_DK + 2 * GLA_DV + DECAY_RANK + FNET_DIM + N_BRANCHES * D_MODEL
N_EXPERTS = 32
TOP_K = 4
D_EXPERT = 1024
SWIGLU_LIMIT = 7.0
SWIGLU_ALPHA = 1.702
MOE_BLOCK = 128
DEEPNORM_ALPHA = (2.0 * DEPTH) ** 0.25
DEEPNORM_BETA = (8.0 * DEPTH) ** -0.25
LN_EPS = 1e-6
N_MOD = 6

kernel_name = "hybrid_gla_fnet_moe_diffusion_step"


def layer_norm(x):
    xf = x.astype(jnp.float32)
    mu = jnp.mean(xf, axis=-1, keepdims=True)
    var = jnp.mean(jnp.square(xf - mu), axis=-1, keepdims=True)
    return (xf - mu) * lax.rsqrt(var + LN_EPS)


def layer_norm_affine(x, g, b):
    return (layer_norm(x) * g + b).astype(x.dtype)


def pos_embed_2d(n_tokens):
    rows = n_tokens // GRID_W
    r = jnp.repeat(jnp.arange(rows), GRID_W).astype(jnp.float32)
    col = jnp.tile(jnp.arange(GRID_W), rows).astype(jnp.float32)
    quarter = D_MODEL // 4
    omega = 1.0 / (10000.0 ** (jnp.arange(quarter, dtype=jnp.float32) / quarter))
    er = r[:, None] * omega
    ec = col[:, None] * omega
    return jnp.concatenate([jnp.sin(er), jnp.cos(er), jnp.sin(ec), jnp.cos(ec)], axis=-1)


def gla_chunked_scan(q, k, v, log_a, s0):
    B, L, H, DK = q.shape
    DV = v.shape[-1]
    n_chunks = L // CHUNK

    def chunks(t):
        return t.astype(jnp.float32).reshape(B, n_chunks, CHUNK, H, t.shape[-1]).transpose(1, 0, 3, 2, 4)

    qc, kc, vc = chunks(q), chunks(k), chunks(v)
    bc = jnp.cumsum(chunks(log_a), axis=3)
    tri = jnp.tril(jnp.ones((CHUNK, CHUNK), dtype=bool))[:, :, None]
    diff = bc[:, :, :, :, None, :] - bc[:, :, :, None, :, :]
    decay = jnp.where(tri, jnp.exp(jnp.where(tri, diff, 0.0)), 0.0)
    scores = jnp.einsum('nbhtd,nbhsd,nbhtsd->nbhts', qc, kc, decay)
    o_intra = jnp.einsum('nbhts,nbhsv->nbhtv', scores, vc)

    b_last = bc[:, :, :, -1:, :]
    q_in = qc * jnp.exp(bc)
    k_out = kc * jnp.exp(b_last - bc)
    chunk_decay = jnp.exp(b_last[:, :, :, 0, :])

    def step(s, inp):
        qi, ko, vi, dcy = inp
        o = jnp.einsum('bhtd,bhdv->bhtv', qi, s)
        s = dcy[..., None] * s + jnp.einsum('bhsd,bhsv->bhdv', ko, vi)
        return s, o

    s_final, o_inter = lax.scan(step, s0.astype(jnp.float32), (q_in, k_out, vc, chunk_decay))
    o = (o_intra + o_inter).transpose(1, 0, 3, 2, 4).reshape(B, L, H, DV)
    return o, s_final


def hybrid_mixer(h, s0_fwd, s0_bwd, w_in, w_dec_fwd, b_dec_fwd, w_dec_bwd, b_dec_bwd,
                 gla_norm_g, w_br_gla, w_br_fnet, w_out):
    B, L, _ = h.shape
    proj = h @ w_in
    q, k, v, g, r, f, gates = jnp.split(proj, IN_SPLITS, axis=-1)
    q = q.reshape(B, L, GLA_HEADS, DK_HEAD) * (DK_HEAD ** -0.5)
    k = k.reshape(B, L, GLA_HEADS, DK_HEAD)
    v = v.reshape(B, L, GLA_HEADS, DV_HEAD)

    def log_decay(w, b):
        z = (r @ w + b).astype(jnp.float32)
        return (jax.nn.log_sigmoid(z) / GATE_NORMALIZER).reshape(B, L, GLA_HEADS, DK_HEAD)

    o_f, s_f = gla_chunked_scan(q, k, v, log_decay(w_dec_fwd, b_dec_fwd), s0_fwd)
    o_b, s_b = gla_chunked_scan(q[:, ::-1], k[:, ::-1], v[:, ::-1],
                                log_decay(w_dec_bwd, b_dec_bwd)[:, ::-1], s0_bwd)
    o = o_f + o_b[:, ::-1]
    o = o * lax.rsqrt(jnp.mean(jnp.square(o), axis=-1, keepdims=True) + LN_EPS) * gla_norm_g
    gla_out = (o.reshape(B, L, GLA_DV).astype(h.dtype) * jax.nn.silu(g)) @ w_br_gla

    u = f.reshape(B, L, FNET_GROUPS, FNET_GROUP_DIM).astype(jnp.float32)
    mixed = jnp.fft.fft2(u, axes=(1, 3), norm='ortho').real.reshape(B, L, FNET_DIM).astype(h.dtype)
    fnet_out = mixed @ w_br_fnet

    s = jax.nn.sigmoid(gates).reshape(B, L, N_BRANCHES, D_MODEL)
    merged = s[:, :, 0] * gla_out + s[:, :, 1] * fnet_out
    return merged @ w_out, s_f, s_b


def moe_ffn(h, w_router, b_router, w_gate_up, b_gate_up, w_down, b_down):
    B, L, D = h.shape
    T = B * L
    xt = h.reshape(T, D)
    logits = (xt @ w_router + b_router).astype(jnp.float32)
    top_val, top_idx = lax.top_k(logits, TOP_K)
    probs = jax.nn.softmax(top_val, axis=-1)
    n_assign = T * TOP_K
    e_flat = top_idx.reshape(-1)
    tok_flat = jnp.arange(n_assign, dtype=jnp.int32) // TOP_K
    w_flat = probs.reshape(-1)
    order = jnp.argsort(e_flat, stable=True)
    se, stok, sw = e_flat[order], tok_flat[order], w_flat[order]
    counts = jnp.zeros((N_EXPERTS,), jnp.int32).at[e_flat].add(1)
    starts = jnp.cumsum(counts) - counts
    padded = ((counts + MOE_BLOCK - 1) // MOE_BLOCK) * MOE_BLOCK
    pends = jnp.cumsum(padded)
    pstarts = pends - padded
    dest = pstarts[se] + (jnp.arange(n_assign, dtype=jnp.int32) - starts[se])
    n_blocks = -(-n_assign // MOE_BLOCK) + N_EXPERTS
    P = n_blocks * MOE_BLOCK
    buf_tok = jnp.full((P,), T, jnp.int32).at[dest].set(stok)
    buf_w = jnp.zeros((P,), jnp.float32).at[dest].set(sw)
    block_expert = jnp.clip(jnp.searchsorted(pends, jnp.arange(n_blocks) * MOE_BLOCK, side='right'),
                            0, N_EXPERTS - 1)
    x_pad = jnp.concatenate([xt, jnp.zeros((1, D), xt.dtype)], axis=0)
    xb = x_pad[buf_tok].reshape(n_blocks, MOE_BLOCK, D)

    def expert_block(args):
        xblk, e = args
        gu = xblk @ w_gate_up[e] + b_gate_up[e]
        gate, up = gu[:, :D_EXPERT], gu[:, D_EXPERT:]
        gate = jnp.minimum(gate, SWIGLU_LIMIT)
        up = jnp.clip(up, -SWIGLU_LIMIT, SWIGLU_LIMIT)
        glu = gate * jax.nn.sigmoid(gate * SWIGLU_ALPHA)
        return ((up + 1.0) * glu) @ w_down[e] + b_down[e]

    yb = lax.map(expert_block, (xb, block_expert)).reshape(P, D)
    y = jax.ops.segment_sum(yb * buf_w[:, None].astype(yb.dtype), buf_tok, num_segments=T + 1)[:T]
    return y.reshape(B, L, D).astype(h.dtype)


def trunk_layer(x, mod, s0_fwd, s0_bwd, w_in, w_dec_fwd, b_dec_fwd, w_dec_bwd, b_dec_bwd,
                gla_norm_g, w_br_gla, w_br_fnet, w_out, ln1_g, ln1_b,
                w_router, b_router, w_gate_up, b_gate_up, w_down, b_down, ln2_g, ln2_b):
    shift1, scale1, gate1, shift2, scale2, gate2 = jnp.split(mod[:, None, :], N_MOD, axis=-1)
    h = (layer_norm(x) * (1.0 + scale1) + shift1).astype(x.dtype)
    mix, s_f, s_b = hybrid_mixer(h, s0_fwd, s0_bwd, w_in, w_dec_fwd, b_dec_fwd, w_dec_bwd, b_dec_bwd,
                                 gla_norm_g, w_br_gla, w_br_fnet, w_out)
    x = layer_norm_affine(DEEPNORM_ALPHA * x + gate1 * mix, ln1_g, ln1_b)
    h = (layer_norm(x) * (1.0 + scale2) + shift2).astype(x.dtype)
    ff = moe_ffn(h, w_router, b_router, w_gate_up, b_gate_up, w_down, b_down)
    x = layer_norm_affine(DEEPNORM_ALPHA * x + gate2 * ff, ln2_g, ln2_b)
    return x, s_f, s_b


def setup_inputs(seed: int = 0) -> dict:
    key = jax.random.key(seed)
    ks = jax.random.split(key, 32)
    nrm = lambda k, shape, s: jax.random.normal(k, shape, jnp.float32) * s
    D, E, F = D_MODEL, N_EXPERTS, D_EXPERT
    return {
        "x_prompt": nrm(ks[0], (BATCH, SEQ, D), 1.0),
        "x_sample": nrm(ks[1], (DEC_BATCH, DEC_SEQ, D), 1.0),
        "state_gla": nrm(ks[2], (DEC_BATCH, DEPTH, 2, GLA_HEADS, DK_HEAD, DV_HEAD), 0.5),
        "c": nrm(ks[3], (DEC_BATCH, D), 1.0),
        "c_ctx": nrm(ks[4], (D,), 1.0),
        "w_ada": nrm(ks[5], (DEPTH, D, N_MOD * D), 0.5 * D ** -0.5),
        "b_ada": nrm(ks[6], (DEPTH, N_MOD * D), 0.02),
        "w_in": nrm(ks[7], (DEPTH, D, IN_COLS), D ** -0.5),
        "w_dec_fwd": nrm(ks[8], (DEPTH, DECAY_RANK, GLA_DK), DECAY_RANK ** -0.5),
        "b_dec_fwd": nrm(ks[9], (DEPTH, GLA_DK), 0.1),
        "w_dec_bwd": nrm(ks[10], (DEPTH, DECAY_RANK, GLA_DK), DECAY_RANK ** -0.5),
        "b_dec_bwd": nrm(ks[11], (DEPTH, GLA_DK), 0.1),
        "gla_norm_g": 1.0 + nrm(ks[12], (DEPTH, DV_HEAD), 0.02),
        "w_br_gla": nrm(ks[13], (DEPTH, GLA_DV, D), GLA_DV ** -0.5),
        "w_br_fnet": nrm(ks[14], (DEPTH, FNET_DIM, D), FNET_DIM ** -0.5),
        "w_out": nrm(ks[15], (DEPTH, D, D), DEEPNORM_BETA * D ** -0.5),
        "ln1_g": 1.0 + nrm(ks[16], (DEPTH, D), 0.02),
        "ln1_b": nrm(ks[17], (DEPTH, D), 0.02),
        "w_router": nrm(ks[18], (DEPTH, D, E), D ** -0.5),
        "b_router": nrm(ks[19], (DEPTH, E), 0.01),
        "w_gate_up": nrm(ks[20], (DEPTH, E, D, 2 * F), D ** -0.5),
        "b_gate_up": nrm(ks[21], (DEPTH, E, 2 * F), 0.01),
        "w_down": nrm(ks[22], (DEPTH, E, F, D), DEEPNORM_BETA * F ** -0.5),
        "b_down": nrm(ks[23], (DEPTH, E, D), 0.01),
        "ln2_g": 1.0 + nrm(ks[24], (DEPTH, D), 0.02),
        "ln2_b": nrm(ks[25], (DEPTH, D), 0.02),
    }


def reference(x_prompt, x_sample, state_gla, c, c_ctx, w_ada, b_ada, w_in,
              w_dec_fwd, b_dec_fwd, w_dec_bwd, b_dec_bwd, gla_norm_g, w_br_gla, w_br_fnet,
              w_out, ln1_g, ln1_b, w_router, b_router, w_gate_up, b_gate_up, w_down, b_down,
              ln2_g, ln2_b):
    x = x_prompt
    n_req = x_prompt.shape[0]
    zero_state = jnp.zeros((n_req, GLA_HEADS, DK_HEAD, DV_HEAD), jnp.float32)
    layer_states = []
    for l in range(DEPTH):
        mod = jax.nn.silu(c_ctx)[None, :] @ w_ada[l] + b_ada[l]
        x, s_f, s_b = trunk_layer(x, mod, zero_state, zero_state, w_in[l], w_dec_fwd[l], b_dec_fwd[l],
                                  w_dec_bwd[l], b_dec_bwd[l], gla_norm_g[l], w_br_gla[l], w_br_fnet[l],
                                  w_out[l], ln1_g[l], ln1_b[l], w_router[l], b_router[l],
                                  w_gate_up[l], b_gate_up[l], w_down[l], b_down[l], ln2_g[l], ln2_b[l])
        layer_states.append(jnp.stack([s_f, s_b], axis=1))
    y_prompt = x
    new_state_gla = jnp.stack(layer_states, axis=1).astype(x_prompt.dtype)

    x = x_sample + pos_embed_2d(x_sample.shape[1]).astype(x_sample.dtype)[None]
    for l in range(DEPTH):
        mod = jax.nn.silu(c) @ w_ada[l] + b_ada[l]
        x, _, _ = trunk_layer(x, mod, state_gla[:, l, 0], state_gla[:, l, 1], w_in[l], w_dec_fwd[l],
                              b_dec_fwd[l], w_dec_bwd[l], b_dec_bwd[l], gla_norm_g[l], w_br_gla[l],
                              w_br_fnet[l], w_out[l], ln1_g[l], ln1_b[l], w_router[l], b_router[l],
                              w_gate_up[l], b_gate_up[l], w_down[l], b_down[l], ln2_g[l], ln2_b[l])
    y_sample = x
    return (y_prompt, y_sample, new_state_gla)
```

```python
import functools
import math

import numpy as np
import jax
import jax.numpy as jnp
from jax import lax
from jax.experimental import pallas as pl
from jax.experimental.pallas import tpu as pltpu

F32 = jnp.float32
BF16 = jnp.bfloat16

D_MODEL = 1024
GRID_W = 64
GLA_HEADS = 4
DK_HEAD = 128
DV_HEAD = 256
GLA_DK = GLA_HEADS * DK_HEAD
GLA_DV = GLA_HEADS * DV_HEAD
DECAY_RANK = 16
GATE_NORMALIZER = 16.0
FNET_GROUPS = 4
FNET_GROUP_DIM = 128
FNET_DIM = FNET_GROUPS * FNET_GROUP_DIM
N_EXPERTS = 32
TOP_K = 4
D_EXPERT = 1024
SWIGLU_LIMIT = 7.0
SWIGLU_ALPHA = 1.702
LN_EPS = 1e-6
N_MOD = 6

LANES = 128
COL_Q = 0
COL_K = GLA_DK
COL_V = 2 * GLA_DK
COL_G = COL_V + GLA_DV
COL_GATE_A = COL_G + GLA_DV
COL_GATE_B = COL_GATE_A + D_MODEL
COL_F = COL_GATE_B + D_MODEL
COL_R = COL_F + FNET_DIM
PROJ_COLS = COL_R + LANES

GLA_CHUNK = 64
TM_PROJ = 512
TN_PROJ = 1152
TM_MIX = 256
MOE_ROWS = 256
VMEM_LIMIT = 56 * 1024 * 1024

HIGHEST = lax.Precision.HIGHEST


def _layer_norm(x):
    mu = jnp.mean(x, axis=-1, keepdims=True)
    xc = x - mu
    var = jnp.mean(xc * xc, axis=-1, keepdims=True)
    return xc * lax.rsqrt(var + LN_EPS)


def _sigmoid(x):
    return 1.0 / (1.0 + jnp.exp(-x))


def _log_sigmoid(z):
    return jnp.minimum(z, 0.0) - jnp.log1p(jnp.exp(-jnp.abs(z)))


def _dot(a, b):
    return jnp.dot(a, b, preferred_element_type=F32)


def _dot_nt(a, b):
    return lax.dot_general(a, b, (((1,), (1,)), ((), ())), preferred_element_type=F32)


def _dot_tn(a, b):
    return lax.dot_general(a, b, (((0,), (0,)), ((), ())), preferred_element_type=F32)


def _ada_kernel(c_ref, w_ref, b_ref, o_ref):
    c = c_ref[...]
    s = c * _sigmoid(c)
    o_ref[...] = _dot(s.astype(BF16), w_ref[...].astype(BF16)) + b_ref[...]


def _ada(cond, w_ada, b_ada):
    rows = cond.shape[0]
    n = w_ada.shape[1]
    tn = 1536
    return pl.pallas_call(
        _ada_kernel,
        grid=(n // tn,),
        in_specs=[pl.BlockSpec((rows, D_MODEL), lambda j: (0, 0)),
                  pl.BlockSpec((D_MODEL, tn), lambda j: (0, j)),
                  pl.BlockSpec((1, tn), lambda j: (0, j))],
        out_specs=pl.BlockSpec((rows, tn), lambda j: (0, j)),
        out_shape=jax.ShapeDtypeStruct((rows, n), F32),
        compiler_params=pltpu.CompilerParams(vmem_limit_bytes=VMEM_LIMIT),
        name="ada_mod",
    )(cond, w_ada, b_ada.reshape(1, n))


def _token_specs(tm, n_ctx_tiles, tiles_per_latent_seq, ctx_mod_row, extra_axes=0):
    def ctx_map(i, *_):
        return (jnp.minimum(i, n_ctx_tiles - 1), 0)

    def lat_map(i, *_):
        return (jnp.maximum(i - n_ctx_tiles, 0), 0)

    def pos_map(i, *_):
        return (jnp.maximum(i - n_ctx_tiles, 0) % tiles_per_latent_seq, 0)

    def mod_map(i, *_):
        return (jnp.where(i < n_ctx_tiles, ctx_mod_row,
                          jnp.maximum(i - n_ctx_tiles, 0) // tiles_per_latent_seq), 0, 0)

    return [pl.BlockSpec((tm, D_MODEL), ctx_map),
            pl.BlockSpec((tm, D_MODEL), lat_map),
            pl.BlockSpec((tm, D_MODEL), pos_map),
            pl.BlockSpec((1, N_MOD, D_MODEL), mod_map)]


def _inproj_kernel(xc_ref, xl_ref, pos_ref, mod_ref, w_ref, o_ref, h_ref, *, n_ctx_tiles):
    i = pl.program_id(0)
    j = pl.program_id(1)

    def modulate(x):
        h = _layer_norm(x) * (1.0 + mod_ref[0, 1:2, :]) + mod_ref[0, 0:1, :]
        h_ref[...] = h.astype(BF16)

    @pl.when((j == 0) & (i < n_ctx_tiles))
    def _():
        modulate(xc_ref[...])

    @pl.when((j == 0) & (i >= n_ctx_tiles))
    def _():
        modulate(xl_ref[...] + pos_ref[...])

    o_ref[...] = _dot(h_ref[...], w_ref[...])


def _inproj(x_ctx, x_lat, pos, mod, w_in_bf, lat_len):
    t_ctx, t_lat = x_ctx.shape[0], x_lat.shape[0]
    n_ctx_tiles = t_ctx // TM_PROJ
    n_tiles = (t_ctx + t_lat) // TM_PROJ
    kern = functools.partial(_inproj_kernel, n_ctx_tiles=n_ctx_tiles)
    specs = _token_specs(TM_PROJ, n_ctx_tiles, lat_len // TM_PROJ, mod.shape[0] - 1)
    return pl.pallas_call(
        kern,
        grid=(n_tiles, PROJ_COLS // TN_PROJ),
        in_specs=specs + [pl.BlockSpec((D_MODEL, TN_PROJ), lambda i, j: (0, j))],
        out_specs=pl.BlockSpec((TM_PROJ, TN_PROJ), lambda i, j: (i, j)),
        out_shape=jax.ShapeDtypeStruct((t_ctx + t_lat, PROJ_COLS), F32),
        scratch_shapes=[pltpu.VMEM((TM_PROJ, D_MODEL), BF16)],
        compiler_params=pltpu.CompilerParams(
            dimension_semantics=("arbitrary", "arbitrary"), vmem_limit_bytes=VMEM_LIMIT),
        name="ln_inproj",
    )(x_ctx, x_lat, pos, mod, w_in_bf)


def _gla_kernel(*refs, seq_len, has_s0, emit_state):
    it = iter(refs)
    q_ref, k_ref, v_ref, r_ref = next(it), next(it), next(it), next(it)
    wdf_ref, bdf_ref, wdb_ref, bdb_ref, g_ref = next(it), next(it), next(it), next(it), next(it)
    s0_ref = next(it) if has_s0 else None
    o_ref = next(it)
    sout_ref = next(it) if emit_state else None
    of_ref, laf_ref, lab_ref, st_ref = next(it), next(it), next(it), next(it)

    C = GLA_CHUNK
    n_chunks = seq_len // C
    r = r_ref[...]
    zf = jnp.dot(r, wdf_ref[...], precision=HIGHEST, preferred_element_type=F32) + bdf_ref[...]
    zb = jnp.dot(r, wdb_ref[...], precision=HIGHEST, preferred_element_type=F32) + bdb_ref[...]
    laf_ref[...] = _log_sigmoid(zf) * (1.0 / GATE_NORMALIZER)
    lab_ref[...] = _log_sigmoid(zb) * (1.0 / GATE_NORMALIZER)

    row = lax.broadcasted_iota(jnp.int32, (C, C), 0)
    col = lax.broadcasted_iota(jnp.int32, (C, C), 1)
    lower = row >= col
    upper = col >= row
    tri_lower = lower.astype(F32)
    tri_upper = upper.astype(F32)
    q_scale = DK_HEAD ** -0.5

    def chunk(n, la_ref, tri, mask, ref_row, end_row):
        r0 = pl.multiple_of(n * C, C)
        la = la_ref[pl.ds(r0, C), :]
        cum = jnp.dot(tri, la, precision=HIGHEST, preferred_element_type=F32)
        ref = cum[ref_row:ref_row + 1, :]
        end = cum[end_row:end_row + 1, :]
        q = q_ref[pl.ds(r0, C), :] * q_scale
        k = k_ref[pl.ds(r0, C), :]
        v = v_ref[pl.ds(r0, C), :].astype(BF16)
        qs = (q * jnp.exp(cum - ref)).astype(BF16)
        ks = (k * jnp.exp(ref - cum)).astype(BF16)
        a = jnp.where(mask, _dot_nt(qs, ks), 0.0)
        o = _dot(a.astype(BF16), v)
        st = st_ref[...]
        qi = (q * jnp.exp(cum)).astype(BF16)
        o = o + _dot_nt(qi, st.astype(BF16))
        ko = (k * jnp.exp(end - cum)).astype(BF16)
        st_ref[...] = st * jnp.exp(end) + _dot_tn(v, ko)
        return r0, o

    def init_state(d):
        if has_s0:
            st_ref[...] = s0_ref[0, d, 0].T
        else:
            st_ref[...] = jnp.zeros_like(st_ref)

    init_state(0)

    def fwd_body(n, carry):
        r0, o = chunk(n, laf_ref, tri_lower, lower, C // 2 - 1, C - 1)
        of_ref[pl.ds(r0, C), :] = o
        return carry

    lax.fori_loop(0, n_chunks, fwd_body, 0)
    if emit_state:
        sout_ref[0, 0, 0] = st_ref[...].T

    init_state(1)
    g = g_ref[...]

    def bwd_body(m, carry):
        n = n_chunks - 1 - m
        r0, o = chunk(n, lab_ref, tri_upper, upper, C // 2, 0)
        o = o + of_ref[pl.ds(r0, C), :]
        ms = jnp.mean(o * o, axis=-1, keepdims=True)
        o_ref[pl.ds(r0, C), :] = o * lax.rsqrt(ms + LN_EPS) * g
        return carry

    lax.fori_loop(0, n_chunks, bwd_body, 0)
    if emit_state:
        sout_ref[0, 1, 0] = st_ref[...].T


def _gla(proj, o_prev, wdf, bdf, wdb, bdb, g, s0, *, n_seq, seq_len, row0, emit_state):
    has_s0 = s0 is not None
    blk0 = row0 // seq_len
    kern = functools.partial(_gla_kernel, seq_len=seq_len, has_s0=has_s0, emit_state=emit_state)
    in_specs = [
        pl.BlockSpec((seq_len, DK_HEAD), lambda b, h: (blk0 + b, COL_Q // DK_HEAD + h)),
        pl.BlockSpec((seq_len, DK_HEAD), lambda b, h: (blk0 + b, COL_K // DK_HEAD + h)),
        pl.BlockSpec((seq_len, DV_HEAD), lambda b, h: (blk0 + b, COL_V // DV_HEAD + h)),
        pl.BlockSpec((seq_len, LANES), lambda b, h: (blk0 + b, COL_R // LANES)),
        pl.BlockSpec((LANES, DK_HEAD), lambda b, h: (0, h)),
        pl.BlockSpec((1, DK_HEAD), lambda b, h: (0, h)),
        pl.BlockSpec((LANES, DK_HEAD), lambda b, h: (0, h)),
        pl.BlockSpec((1, DK_HEAD), lambda b, h: (0, h)),
        pl.BlockSpec((1, DV_HEAD), lambda b, h: (0, 0)),
    ]
    args = [proj, proj, proj, proj, wdf, bdf, wdb, bdb, g]
    if has_s0:
        in_specs.append(pl.BlockSpec((1, 2, 1, DK_HEAD, DV_HEAD), lambda b, h: (b, 0, h, 0, 0)))
        args.append(s0)
    in_specs.append(pl.BlockSpec(memory_space=pl.ANY))
    args.append(o_prev)
    out_specs = [pl.BlockSpec((seq_len, DV_HEAD), lambda b, h: (blk0 + b, h))]
    out_shape = [jax.ShapeDtypeStruct(o_prev.shape, F32)]
    if emit_state:
        out_specs.append(pl.BlockSpec((1, 2, 1, DK_HEAD, DV_HEAD), lambda b, h: (b, 0, h, 0, 0)))
        out_shape.append(jax.ShapeDtypeStruct((n_seq, 2, GLA_HEADS, DK_HEAD, DV_HEAD), F32))

    def body(*refs):
        n_in = len(args)
        kern(*refs[:n_in - 1], *refs[n_in:])

    res = pl.pallas_call(
        body,
        grid=(n_seq, GLA_HEADS),
        in_specs=in_specs,
        out_specs=out_specs,
        out_shape=out_shape,
        scratch_shapes=[pltpu.VMEM((seq_len, DV_HEAD), F32),
                        pltpu.VMEM((seq_len, DK_HEAD), F32),
                        pltpu.VMEM((seq_len, DK_HEAD), F32),
                        pltpu.VMEM((DV_HEAD, DK_HEAD), F32)],
        input_output_aliases={len(args) - 1: 0},
        compiler_params=pltpu.CompilerParams(
            dimension_semantics=("arbitrary", "arbitrary"), vmem_limit_bytes=VMEM_LIMIT),
        name="gla_seq%d" % seq_len,
    )(*args)
    return res


def _fnet_kernel(f_ref, cl_ref, sl_ref, cg_ref, sg_ref, prev_ref, o_ref, uc_ref, us_ref, *, seq_len):
    del prev_ref
    cg = cg_ref[...]
    sg = sg_ref[...]
    for grp in range(FNET_GROUPS):
        lo = grp * FNET_GROUP_DIM
        u = f_ref[:, lo:lo + FNET_GROUP_DIM].astype(BF16)
        uc_ref[:, lo:lo + FNET_GROUP_DIM] = _dot(u, cg).astype(BF16)
        us_ref[:, lo:lo + FNET_GROUP_DIM] = _dot(u, sg).astype(BF16)
    mixed = _dot(cl_ref[...], uc_ref[...]) - _dot(sl_ref[...], us_ref[...])
    o_ref[...] = mixed * (1.0 / math.sqrt(seq_len * FNET_GROUP_DIM))


def _dft_mats(n):
    j = np.arange(n, dtype=np.int64)
    ang = (2.0 * np.pi / n) * ((j[:, None] * j[None, :]) % n).astype(np.float64)
    return (jnp.asarray(np.cos(ang), dtype=F32).astype(BF16),
            jnp.asarray(np.sin(ang), dtype=F32).astype(BF16))


def _fnet(proj, mixed_prev, *, n_seq, seq_len, row0):
    blk0 = row0 // seq_len
    cl, sl = _dft_mats(seq_len)
    cg, sg = _dft_mats(FNET_GROUP_DIM)
    kern = functools.partial(_fnet_kernel, seq_len=seq_len)
    return pl.pallas_call(
        kern,
        grid=(n_seq,),
        in_specs=[pl.BlockSpec((seq_len, FNET_DIM), lambda b: (blk0 + b, COL_F // FNET_DIM)),
                  pl.BlockSpec((seq_len, seq_len), lambda b: (0, 0)),
                  pl.BlockSpec((seq_len, seq_len), lambda b: (0, 0)),
                  pl.BlockSpec((FNET_GROUP_DIM, FNET_GROUP_DIM), lambda b: (0, 0)),
                  pl.BlockSpec((FNET_GROUP_DIM, FNET_GROUP_DIM), lambda b: (0, 0)),
                  pl.BlockSpec(memory_space=pl.ANY)],
        out_specs=pl.BlockSpec((seq_len, FNET_DIM), lambda b: (blk0 + b, 0)),
        out_shape=jax.ShapeDtypeStruct(mixed_prev.shape, F32),
        scratch_shapes=[pltpu.VMEM((seq_len, FNET_DIM), BF16),
                        pltpu.VMEM((seq_len, FNET_DIM), BF16)],
        input_output_aliases={5: 0},
        compiler_params=pltpu.CompilerParams(
            dimension_semantics=("arbitrary",), vmem_limit_bytes=VMEM_LIMIT),
        name="fnet_seq%d" % seq_len,
    )(proj, cl, sl, cg, sg, mixed_prev)


def _merge_kernel(xc_ref, xl_ref, pos_ref, mod_ref, o_ref, g_ref, ga_ref, gb_ref, mx_ref,
                  wbg_ref, wbf_ref, wo_ref, l1g_ref, l1b_ref, wr_ref, br_ref,
                  x1_ref, h2_ref, ridx_ref, rw_ref, x_sc, *, n_ctx_tiles, alpha):
    i = pl.program_id(0)

    @pl.when(i < n_ctx_tiles)
    def _():
        x_sc[...] = xc_ref[...]

    @pl.when(i >= n_ctx_tiles)
    def _():
        x_sc[...] = xl_ref[...] + pos_ref[...]

    g = g_ref[...]
    a = (o_ref[...] * (g * _sigmoid(g))).astype(BF16)
    gla_out = _dot(a, wbg_ref[...])
    fnet_out = _dot(mx_ref[...].astype(BF16), wbf_ref[...])
    merged = _sigmoid(ga_ref[...]) * gla_out + _sigmoid(gb_ref[...]) * fnet_out
    mix = _dot(merged.astype(BF16), wo_ref[...])
    y = alpha * x_sc[...] + mod_ref[0, 2:3, :] * mix
    x1 = _layer_norm(y) * l1g_ref[...] + l1b_ref[...]
    x1_ref[...] = x1
    h2 = _layer_norm(x1) * (1.0 + mod_ref[0, 4:5, :]) + mod_ref[0, 3:4, :]
    h2_ref[...] = h2.astype(BF16)

    logits = jnp.dot(h2, wr_ref[...], precision=HIGHEST, preferred_element_type=F32) + br_ref[...]
    lane_i = lax.broadcasted_iota(jnp.int32, logits.shape, 1)
    lane = lane_i.astype(F32)
    idx_out = jnp.zeros(logits.shape, F32)
    val_out = jnp.zeros(logits.shape, F32)
    top0 = None
    denom = None
    for kk in range(TOP_K):
        m = jnp.max(logits, axis=-1, keepdims=True)
        sel = jnp.min(jnp.where(logits == m, lane, float(LANES)), axis=-1, keepdims=True)
        if kk == 0:
            top0 = m
            p = jnp.ones_like(m)
            denom = p
        else:
            p = jnp.exp(m - top0)
            denom = denom + p
        idx_out = jnp.where(lane_i == kk, sel, idx_out)
        val_out = jnp.where(lane_i == kk, p, val_out)
        logits = jnp.where(lane == sel, -jnp.inf, logits)
    ridx_ref[...] = idx_out.astype(jnp.int32)
    rw_ref[...] = val_out / denom


def _merge(x_ctx, x_lat, pos, mod, o_all, proj, mixed, wbg, wbf, wo, l1g, l1b, wr, br, lat_len, alpha):
    t_ctx, t_lat = x_ctx.shape[0], x_lat.shape[0]
    t_all = t_ctx + t_lat
    tm = TM_MIX
    n_ctx_tiles = t_ctx // tm
    kern = functools.partial(_merge_kernel, n_ctx_tiles=n_ctx_tiles, alpha=alpha)
    specs = _token_specs(tm, n_ctx_tiles, lat_len // tm, mod.shape[0] - 1)

    def const(shape):
        return pl.BlockSpec(shape, lambda i: (0,) * len(shape))

    in_specs = specs + [
        pl.BlockSpec((tm, GLA_DV), lambda i: (i, 0)),
        pl.BlockSpec((tm, GLA_DV), lambda i: (i, COL_G // GLA_DV)),
        pl.BlockSpec((tm, D_MODEL), lambda i: (i, COL_GATE_A // D_MODEL)),
        pl.BlockSpec((tm, D_MODEL), lambda i: (i, COL_GATE_B // D_MODEL)),
        pl.BlockSpec((tm, FNET_DIM), lambda i: (i, 0)),
        const((GLA_DV, D_MODEL)), const((FNET_DIM, D_MODEL)), const((D_MODEL, D_MODEL)),
        const((1, D_MODEL)), const((1, D_MODEL)),
        const((D_MODEL, LANES)), const((1, LANES)),
    ]
    out_specs = [pl.BlockSpec((tm, D_MODEL), lambda i: (i, 0)),
                 pl.BlockSpec((tm, D_MODEL), lambda i: (i, 0)),
                 pl.BlockSpec((tm, LANES), lambda i: (i, 0)),
                 pl.BlockSpec((tm, LANES), lambda i: (i, 0))]
    out_shape = [jax.ShapeDtypeStruct((t_all, D_MODEL), F32),
                 jax.ShapeDtypeStruct((t_all, D_MODEL), BF16),
                 jax.ShapeDtypeStruct((t_all, LANES), jnp.int32),
                 jax.ShapeDtypeStruct((t_all, LANES), F32)]
    return pl.pallas_call(
        kern,
        grid=(t_all // tm,),
        in_specs=in_specs,
        out_specs=out_specs,
        out_shape=out_shape,
        scratch_shapes=[pltpu.VMEM((tm, D_MODEL), F32)],
        compiler_params=pltpu.CompilerParams(
            dimension_semantics=("arbitrary",), vmem_limit_bytes=VMEM_LIMIT),
        name="merge_ln1_router",
    )(x_ctx, x_lat, pos, mod, o_all, proj, proj, proj, mixed, wbg, wbf, wo, l1g, l1b, wr, br)


def _moe_kernel(be_ref, nu_ref, x_ref, wgu_ref, bgu_ref, wd_ref, bd_ref, o_ref, wgu_bf, wd_bf):
    b = pl.program_id(0)
    e = be_ref[b]
    prev = be_ref[jnp.maximum(b - 1, 0)]
    active = b < nu_ref[0]
    changed = (b == 0) | (e != prev)

    @pl.when(active & changed)
    def _():
        wgu_bf[...] = wgu_ref[0].astype(BF16)
        wd_bf[...] = wd_ref[0].astype(BF16)

    @pl.when(active)
    def _():
        gu = _dot(x_ref[...], wgu_bf[...]) + bgu_ref[0]
        gate = jnp.minimum(gu[:, :D_EXPERT], SWIGLU_LIMIT)
        up = jnp.clip(gu[:, D_EXPERT:], -SWIGLU_LIMIT, SWIGLU_LIMIT)
        glu = gate * _sigmoid(gate * SWIGLU_ALPHA)
        act = ((up + 1.0) * glu).astype(BF16)
        o_ref[...] = _dot(act, wd_bf[...]) + bd_ref[0]


def _moe(block_expert, n_used, xs, w_gate_up, b_gate_up, w_down, b_down):
    p_rows = xs.shape[0]
    n_blocks = p_rows // MOE_ROWS

    def blk(b, be, nu):
        return jnp.minimum(b, nu[0] - 1)

    grid_spec = pltpu.PrefetchScalarGridSpec(
        num_scalar_prefetch=2,
        grid=(n_blocks,),
        in_specs=[
            pl.BlockSpec((MOE_ROWS, D_MODEL), lambda b, be, nu: (blk(b, be, nu), 0)),
            pl.BlockSpec((1, D_MODEL, 2 * D_EXPERT), lambda b, be, nu: (be[blk(b, be, nu)], 0, 0)),
            pl.BlockSpec((1, 1, 2 * D_EXPERT), lambda b, be, nu: (be[blk(b, be, nu)], 0, 0)),
            pl.BlockSpec((1, D_EXPERT, D_MODEL), lambda b, be, nu: (be[blk(b, be, nu)], 0, 0)),
            pl.BlockSpec((1, 1, D_MODEL), lambda b, be, nu: (be[blk(b, be, nu)], 0, 0)),
        ],
        out_specs=pl.BlockSpec((MOE_ROWS, D_MODEL), lambda b, be, nu: (blk(b, be, nu), 0)),
        scratch_shapes=[pltpu.VMEM((D_MODEL, 2 * D_EXPERT), BF16),
                        pltpu.VMEM((D_EXPERT, D_MODEL), BF16)],
    )
    return pl.pallas_call(
        _moe_kernel,
        grid_spec=grid_spec,
        out_shape=jax.ShapeDtypeStruct((p_rows, D_MODEL), F32),
        compiler_params=pltpu.CompilerParams(
            dimension_semantics=("arbitrary",), vmem_limit_bytes=VMEM_LIMIT),
        name="moe_grouped_mlp",
    )(block_expert, n_used, xs, w_gate_up, b_gate_up.reshape(N_EXPERTS, 1, 2 * D_EXPERT),
      w_down, b_down.reshape(N_EXPERTS, 1, D_MODEL))


def _combine_kernel(x1_ref, yg_ref, rw_ref, mod_ref, g_ref, b_ref, o_ref, *, alpha):
    rw = rw_ref[...]
    ff = rw[:, 0:1] * yg_ref[:, 0:D_MODEL]
    for kk in range(1, TOP_K):
        ff = ff + rw[:, kk:kk + 1] * yg_ref[:, kk * D_MODEL:(kk + 1) * D_MODEL]
    y = alpha * x1_ref[...] + mod_ref[0, 5:6, :] * ff
    o_ref[...] = _layer_norm(y) * g_ref[...] + b_ref[...]


def _combine(x1, yg, rw, mod, l2g, l2b, *, row0, n_rows, mod_map, alpha):
    tm = TM_MIX
    t0 = row0 // tm
    kern = functools.partial(_combine_kernel, alpha=alpha)
    return pl.pallas_call(
        kern,
        grid=(n_rows // tm,),
        in_specs=[pl.BlockSpec((tm, D_MODEL), lambda i: (t0 + i, 0)),
                  pl.BlockSpec((tm, TOP_K * D_MODEL), lambda i: (t0 + i, 0)),
                  pl.BlockSpec((tm, LANES), lambda i: (t0 + i, 0)),
                  pl.BlockSpec((1, N_MOD, D_MODEL), mod_map),
                  pl.BlockSpec((1, D_MODEL), lambda i: (0, 0)),
                  pl.BlockSpec((1, D_MODEL), lambda i: (0, 0))],
        out_specs=pl.BlockSpec((tm, D_MODEL), lambda i: (i, 0)),
        out_shape=jax.ShapeDtypeStruct((n_rows, D_MODEL), F32),
        compiler_params=pltpu.CompilerParams(
            dimension_semantics=("arbitrary",), vmem_limit_bytes=VMEM_LIMIT),
        name="combine_ln2",
    )(x1, yg, rw, mod, l2g, l2b)


def _routing_tables(ridx, n_blocks):
    t_all = ridx.shape[0]
    idx = ridx[:, :TOP_K]
    onehot = jnp.any(idx[:, :, None] == jnp.arange(N_EXPERTS, dtype=jnp.int32)[None, None, :], axis=1)
    onehot = onehot.astype(jnp.int32)
    incl = jnp.cumsum(onehot, axis=0)
    rank = incl - onehot
    counts = incl[-1]
    padded = ((counts + MOE_ROWS - 1) // MOE_ROWS) * MOE_ROWS
    pends = jnp.cumsum(padded)
    pstarts = pends - padded
    dest = pstarts[idx] + jnp.take_along_axis(rank, idx, axis=1)
    tok = jnp.broadcast_to(jnp.arange(t_all, dtype=jnp.int32)[:, None], dest.shape)
    buf_tok = jnp.zeros((n_blocks * MOE_ROWS,), jnp.int32).at[dest.reshape(-1)].set(tok.reshape(-1))
    block_expert = jnp.clip(
        jnp.searchsorted(pends, jnp.arange(n_blocks, dtype=jnp.int32) * MOE_ROWS, side='right'),
        0, N_EXPERTS - 1).astype(jnp.int32)
    n_used = (pends[-1:] // MOE_ROWS).astype(jnp.int32)
    return dest, buf_tok, block_expert, n_used


def _pos_embed_2d(n_tokens):
    rows = n_tokens // GRID_W
    r = jnp.repeat(jnp.arange(rows), GRID_W).astype(F32)
    col = jnp.tile(jnp.arange(GRID_W), rows).astype(F32)
    quarter = D_MODEL // 4
    omega = 1.0 / (10000.0 ** (jnp.arange(quarter, dtype=F32) / quarter))
    er = r[:, None] * omega
    ec = col[:, None] * omega
    return jnp.concatenate([jnp.sin(er), jnp.cos(er), jnp.sin(ec), jnp.cos(ec)], axis=-1)


def _reorder_w_in(w):
    o_r = 2 * GLA_DK + 2 * GLA_DV
    o_f = o_r + DECAY_RANK
    o_gate = o_f + FNET_DIM
    pad = jnp.zeros((w.shape[0], LANES - DECAY_RANK), w.dtype)
    return jnp.concatenate([w[:, :o_r], w[:, o_gate:], w[:, o_f:o_gate], w[:, o_r:o_f], pad], axis=1)


def kernel(x_prompt, x_sample, state_gla, c, c_ctx, w_ada, b_ada, w_in, w_dec_fwd, b_dec_fwd,
           w_dec_bwd, b_dec_bwd, gla_norm_g, w_br_gla, w_br_fnet, w_out, ln1_g, ln1_b, w_router,
           b_router, w_gate_up, b_gate_up, w_down, b_down, ln2_g, ln2_b):
    n_req, ctx_len, _ = x_prompt.shape
    n_lat, lat_len, _ = x_sample.shape
    depth = w_in.shape[0]
    alpha = (2.0 * depth) ** 0.25
    t_ctx = n_req * ctx_len
    t_lat = n_lat * lat_len
    t_all = t_ctx + t_lat

    x_ctx = x_prompt.reshape(t_ctx, D_MODEL)
    x_lat = x_sample.reshape(t_lat, D_MODEL)
    pos = _pos_embed_2d(lat_len)
    zero_pos = jnp.zeros_like(pos)

    cond_rows = -(-(n_lat + 1) // 8) * 8
    cond = jnp.zeros((cond_rows, D_MODEL), F32).at[:n_lat].set(c).at[cond_rows - 1].set(c_ctx)

    n_moe_blocks = (t_all * TOP_K) // MOE_ROWS + N_EXPERTS
    states = []
    for l in range(depth):
        mod = _ada(cond, w_ada[l], b_ada[l]).reshape(cond_rows, N_MOD, D_MODEL)
        layer_pos = pos if l == 0 else zero_pos
        proj = _inproj(x_ctx, x_lat, layer_pos, mod, _reorder_w_in(w_in[l]).astype(BF16), lat_len)

        def pad_dec(w):
            return jnp.zeros((LANES, GLA_DK), F32).at[:DECAY_RANK].set(w)

        dec = (pad_dec(w_dec_fwd[l]), b_dec_fwd[l].reshape(1, GLA_DK),
               pad_dec(w_dec_bwd[l]), b_dec_bwd[l].reshape(1, GLA_DK),
               gla_norm_g[l].reshape(1, DV_HEAD))
        o_all = jnp.zeros((t_all, GLA_DV), F32)
        o_all, s_new = _gla(proj, o_all, *dec, None, n_seq=n_req, seq_len=ctx_len, row0=0,
                            emit_state=True)
        (o_all,) = _gla(proj, o_all, *dec, state_gla[:, l], n_seq=n_lat, seq_len=lat_len,
                        row0=t_ctx, emit_state=False)
        states.append(s_new)

        mixed = jnp.zeros((t_all, FNET_DIM), F32)
        mixed = _fnet(proj, mixed, n_seq=n_req, seq_len=ctx_len, row0=0)
        mixed = _fnet(proj, mixed, n_seq=n_lat, seq_len=lat_len, row0=t_ctx)

        wr = jnp.zeros((D_MODEL, LANES), F32).at[:, :N_EXPERTS].set(w_router[l])
        br = jnp.full((1, LANES), -1e30, F32).at[0, :N_EXPERTS].set(b_router[l])
        x1, h2, ridx, rw = _merge(
            x_ctx, x_lat, layer_pos, mod, o_all, proj, mixed,
            w_br_gla[l].astype(BF16), w_br_fnet[l].astype(BF16), w_out[l].astype(BF16),
            ln1_g[l].reshape(1, D_MODEL), ln1_b[l].reshape(1, D_MODEL), wr, br, lat_len, alpha)

        dest, buf_tok, block_expert, n_used = _routing_tables(ridx, n_moe_blocks)
        xs = jnp.take(h2, buf_tok, axis=0)
        yb = _moe(block_expert, n_used, xs, w_gate_up[l], b_gate_up[l], w_down[l], b_down[l])
        yg = jnp.take(yb, dest.reshape(-1), axis=0).reshape(t_all, TOP_K * D_MODEL)

        l2g = ln2_g[l].reshape(1, D_MODEL)
        l2b = ln2_b[l].reshape(1, D_MODEL)
        tiles_per_seq = lat_len // TM_MIX
        x_ctx = _combine(x1, yg, rw, mod, l2g, l2b, row0=0, n_rows=t_ctx,
                         mod_map=lambda i: (cond_rows - 1, 0, 0), alpha=alpha)
        x_lat = _combine(x1, yg, rw, mod, l2g, l2b, row0=t_ctx, n_rows=t_lat,
                         mod_map=lambda i: (i // tiles_per_seq, 0, 0), alpha=alpha)

    y_prompt = x_ctx.reshape(x_prompt.shape)
    y_sample = x_lat.reshape(x_sample.shape)
    new_state = jnp.stack(states, axis=1).astype(x_prompt.dtype)
    return (y_prompt, y_sample, new_state)
```

```python
import functools
import math

import numpy as np
import jax
import jax.numpy as jnp
from jax import lax
from jax.experimental import pallas as pl
from jax.experimental.pallas import tpu as pltpu
from jax.experimental.pallas import tpu_sc as plsc

F32 = jnp.float32
BF16 = jnp.bfloat16

D_MODEL = 1024
GRID_W = 64
GLA_HEADS = 4
DK_HEAD = 128
DV_HEAD = 256
GLA_DK = GLA_HEADS * DK_HEAD
GLA_DV = GLA_HEADS * DV_HEAD
DECAY_RANK = 16
GATE_NORMALIZER = 16.0
FNET_GROUPS = 4
FNET_GROUP_DIM = 128
FNET_DIM = FNET_GROUPS * FNET_GROUP_DIM
N_EXPERTS = 32
TOP_K = 4
D_EXPERT = 1024
SWIGLU_LIMIT = 7.0
SWIGLU_ALPHA = 1.702
LN_EPS = 1e-6
N_MOD = 6

LANES = 128
SUBLANES = 8
ROW_TILES = D_MODEL // LANES
COL_Q = 0
COL_K = GLA_DK
COL_V = 2 * GLA_DK
COL_G = COL_V + GLA_DV
COL_GATE_A = COL_G + GLA_DV
COL_GATE_B = COL_GATE_A + D_MODEL
COL_F = COL_GATE_B + D_MODEL
COL_R = COL_F + FNET_DIM
PROJ_COLS = COL_R + LANES

GLA_CHUNK = 64
TM_PROJ = 512
TN_PROJ = 1152
TM_MIX = 256
MOE_ROWS = 256
VMEM_LIMIT = 56 * 1024 * 1024

SC_CORES = 2
SC_SUBCORES = 16
SC_WORKERS = SC_CORES * SC_SUBCORES
SC_WINDOW = 32

HIGHEST = lax.Precision.HIGHEST


def _layer_norm(x):
    mu = jnp.mean(x, axis=-1, keepdims=True)
    xc = x - mu
    var = jnp.mean(xc * xc, axis=-1, keepdims=True)
    return xc * lax.rsqrt(var + LN_EPS)


def _sigmoid(x):
    return 1.0 / (1.0 + jnp.exp(-x))


def _log_sigmoid(z):
    return jnp.minimum(z, 0.0) - jnp.log1p(jnp.exp(-jnp.abs(z)))


def _dot(a, b):
    return jnp.dot(a, b, preferred_element_type=F32)


def _dot_nt(a, b):
    return lax.dot_general(a, b, (((1,), (1,)), ((), ())), preferred_element_type=F32)


def _dot_tn(a, b):
    return lax.dot_general(a, b, (((0,), (0,)), ((), ())), preferred_element_type=F32)


def _store_row_tiles(ref, val):
    for j in range(ROW_TILES):
        ref[:, j, :] = val[:, j * LANES:(j + 1) * LANES]


def _ada_kernel(c_ref, w_ref, b_ref, o_ref):
    c = c_ref[...]
    s = c * _sigmoid(c)
    o_ref[...] = _dot(s.astype(BF16), w_ref[...].astype(BF16)) + b_ref[...]


def _ada(cond, w_ada, b_ada):
    rows = cond.shape[0]
    n = w_ada.shape[1]
    tn = 1536
    return pl.pallas_call(
        _ada_kernel,
        grid=(n // tn,),
        in_specs=[pl.BlockSpec((rows, D_MODEL), lambda j: (0, 0)),
                  pl.BlockSpec((D_MODEL, tn), lambda j: (0, j)),
                  pl.BlockSpec((1, tn), lambda j: (0, j))],
        out_specs=pl.BlockSpec((rows, tn), lambda j: (0, j)),
        out_shape=jax.ShapeDtypeStruct((rows, n), F32),
        compiler_params=pltpu.CompilerParams(vmem_limit_bytes=VMEM_LIMIT),
        name="ada_mod",
    )(cond, w_ada, b_ada.reshape(1, n))


def _group_maps(n_ctx_tiles):
    def ctx_map(i, *_):
        return (jnp.minimum(i, n_ctx_tiles - 1), 0)

    def lat_map(i, *_):
        return (jnp.maximum(i - n_ctx_tiles, 0), 0)

    return ctx_map, lat_map


def _token_specs(tm, n_ctx_tiles, tiles_per_latent_seq, ctx_mod_row):
    ctx_map, lat_map = _group_maps(n_ctx_tiles)

    def pos_map(i, *_):
        return (jnp.maximum(i - n_ctx_tiles, 0) % tiles_per_latent_seq, 0)

    def mod_map(i, *_):
        return (jnp.where(i < n_ctx_tiles, ctx_mod_row,
                          jnp.maximum(i - n_ctx_tiles, 0) // tiles_per_latent_seq), 0, 0)

    return [pl.BlockSpec((tm, D_MODEL), ctx_map),
            pl.BlockSpec((tm, D_MODEL), lat_map),
            pl.BlockSpec((tm, D_MODEL), pos_map),
            pl.BlockSpec((1, N_MOD, D_MODEL), mod_map)]


def _inproj_kernel(xc_ref, xl_ref, pos_ref, mod_ref, w_ref, o_ref, h_ref, *, n_ctx_tiles):
    i = pl.program_id(0)
    j = pl.program_id(1)

    def modulate(x):
        h = _layer_norm(x) * (1.0 + mod_ref[0, 1:2, :]) + mod_ref[0, 0:1, :]
        h_ref[...] = h.astype(BF16)

    @pl.when((j == 0) & (i < n_ctx_tiles))
    def _():
        modulate(xc_ref[...])

    @pl.when((j == 0) & (i >= n_ctx_tiles))
    def _():
        modulate(xl_ref[...] + pos_ref[...])

    o_ref[...] = _dot(h_ref[...], w_ref[...])


def _inproj(x_ctx, x_lat, pos, mod, w_in_bf, lat_len):
    t_ctx, t_lat = x_ctx.shape[0], x_lat.shape[0]
    n_ctx_tiles = t_ctx // TM_PROJ
    n_tiles = (t_ctx + t_lat) // TM_PROJ
    kern = functools.partial(_inproj_kernel, n_ctx_tiles=n_ctx_tiles)
    specs = _token_specs(TM_PROJ, n_ctx_tiles, lat_len // TM_PROJ, mod.shape[0] - 1)
    return pl.pallas_call(
        kern,
        grid=(n_tiles, PROJ_COLS // TN_PROJ),
        in_specs=specs + [pl.BlockSpec((D_MODEL, TN_PROJ), lambda i, j: (0, j))],
        out_specs=pl.BlockSpec((TM_PROJ, TN_PROJ), lambda i, j: (i, j)),
        out_shape=jax.ShapeDtypeStruct((t_ctx + t_lat, PROJ_COLS), F32),
        scratch_shapes=[pltpu.VMEM((TM_PROJ, D_MODEL), BF16)],
        compiler_params=pltpu.CompilerParams(
            dimension_semantics=("arbitrary", "arbitrary"), vmem_limit_bytes=VMEM_LIMIT),
        name="ln_inproj",
    )(x_ctx, x_lat, pos, mod, w_in_bf)


def _gla_kernel(*refs, seq_len, has_s0, emit_state):
    it = iter(refs)
    q_ref, k_ref, v_ref, r_ref = next(it), next(it), next(it), next(it)
    wdf_ref, bdf_ref, wdb_ref, bdb_ref, g_ref = next(it), next(it), next(it), next(it), next(it)
    s0_ref = next(it) if has_s0 else None
    o_ref = next(it)
    sout_ref = next(it) if emit_state else None
    of_ref, laf_ref, lab_ref, st_ref = next(it), next(it), next(it), next(it)

    C = GLA_CHUNK
    n_chunks = seq_len // C
    r = r_ref[...]
    zf = jnp.dot(r, wdf_ref[...], precision=HIGHEST, preferred_element_type=F32) + bdf_ref[...]
    zb = jnp.dot(r, wdb_ref[...], precision=HIGHEST, preferred_element_type=F32) + bdb_ref[...]
    laf_ref[...] = _log_sigmoid(zf) * (1.0 / GATE_NORMALIZER)
    lab_ref[...] = _log_sigmoid(zb) * (1.0 / GATE_NORMALIZER)

    row = lax.broadcasted_iota(jnp.int32, (C, C), 0)
    col = lax.broadcasted_iota(jnp.int32, (C, C), 1)
    lower = row >= col
    upper = col >= row
    tri_lower = lower.astype(F32)
    tri_upper = upper.astype(F32)
    q_scale = DK_HEAD ** -0.5

    def chunk(n, la_ref, tri, mask, ref_row, end_row):
        r0 = pl.multiple_of(n * C, C)
        la = la_ref[pl.ds(r0, C), :]
        cum = jnp.dot(tri, la, precision=HIGHEST, preferred_element_type=F32)
        ref = cum[ref_row:ref_row + 1, :]
        end = cum[end_row:end_row + 1, :]
        q = q_ref[pl.ds(r0, C), :] * q_scale
        k = k_ref[pl.ds(r0, C), :]
        v = v_ref[pl.ds(r0, C), :].astype(BF16)
        qs = (q * jnp.exp(cum - ref)).astype(BF16)
        ks = (k * jnp.exp(ref - cum)).astype(BF16)
        a = jnp.where(mask, _dot_nt(qs, ks), 0.0)
        o = _dot(a.astype(BF16), v)
        st = st_ref[...]
        qi = (q * jnp.exp(cum)).astype(BF16)
        o = o + _dot_nt(qi, st.astype(BF16))
        ko = (k * jnp.exp(end - cum)).astype(BF16)
        st_ref[...] = st * jnp.exp(end) + _dot_tn(v, ko)
        return r0, o

    def init_state(d):
        if has_s0:
            st_ref[...] = s0_ref[0, d, 0].T
        else:
            st_ref[...] = jnp.zeros_like(st_ref)

    init_state(0)

    def fwd_body(n, carry):
        r0, o = chunk(n, laf_ref, tri_lower, lower, C // 2 - 1, C - 1)
        of_ref[pl.ds(r0, C), :] = o
        return carry

    lax.fori_loop(0, n_chunks, fwd_body, 0)
    if emit_state:
        sout_ref[0, 0, 0] = st_ref[...].T

    init_state(1)
    g = g_ref[...]

    def bwd_body(m, carry):
        n = n_chunks - 1 - m
        r0, o = chunk(n, lab_ref, tri_upper, upper, C // 2, 0)
        o = o + of_ref[pl.ds(r0, C), :]
        ms = jnp.mean(o * o, axis=-1, keepdims=True)
        o_ref[pl.ds(r0, C), :] = o * lax.rsqrt(ms + LN_EPS) * g
        return carry

    lax.fori_loop(0, n_chunks, bwd_body, 0)
    if emit_state:
        sout_ref[0, 1, 0] = st_ref[...].T


def _gla(proj, wdf, bdf, wdb, bdb, g, s0, *, n_seq, seq_len, row0, emit_state):
    has_s0 = s0 is not None
    blk0 = row0 // seq_len
    kern = functools.partial(_gla_kernel, seq_len=seq_len, has_s0=has_s0, emit_state=emit_state)
    in_specs = [
        pl.BlockSpec((seq_len, DK_HEAD), lambda b, h: (blk0 + b, COL_Q // DK_HEAD + h)),
        pl.BlockSpec((seq_len, DK_HEAD), lambda b, h: (blk0 + b, COL_K // DK_HEAD + h)),
        pl.BlockSpec((seq_len, DV_HEAD), lambda b, h: (blk0 + b, COL_V // DV_HEAD + h)),
        pl.BlockSpec((seq_len, LANES), lambda b, h: (blk0 + b, COL_R // LANES)),
        pl.BlockSpec((LANES, DK_HEAD), lambda b, h: (0, h)),
        pl.BlockSpec((1, DK_HEAD), lambda b, h: (0, h)),
        pl.BlockSpec((LANES, DK_HEAD), lambda b, h: (0, h)),
        pl.BlockSpec((1, DK_HEAD), lambda b, h: (0, h)),
        pl.BlockSpec((1, DV_HEAD), lambda b, h: (0, 0)),
    ]
    args = [proj, proj, proj, proj, wdf, bdf, wdb, bdb, g]
    if has_s0:
        in_specs.append(pl.BlockSpec((1, 2, 1, DK_HEAD, DV_HEAD), lambda b, h: (b, 0, h, 0, 0)))
        args.append(s0)
    out_specs = [pl.BlockSpec((seq_len, DV_HEAD), lambda b, h: (b, h))]
    out_shape = [jax.ShapeDtypeStruct((n_seq * seq_len, GLA_DV), F32)]
    if emit_state:
        out_specs.append(pl.BlockSpec((1, 2, 1, DK_HEAD, DV_HEAD), lambda b, h: (b, 0, h, 0, 0)))
        out_shape.append(jax.ShapeDtypeStruct((n_seq, 2, GLA_HEADS, DK_HEAD, DV_HEAD), F32))

    res = pl.pallas_call(
        kern,
        grid=(n_seq, GLA_HEADS),
        in_specs=in_specs,
        out_specs=out_specs,
        out_shape=out_shape,
        scratch_shapes=[pltpu.VMEM((seq_len, DV_HEAD), F32),
                        pltpu.VMEM((seq_len, DK_HEAD), F32),
                        pltpu.VMEM((seq_len, DK_HEAD), F32),
                        pltpu.VMEM((DV_HEAD, DK_HEAD), F32)],
        compiler_params=pltpu.CompilerParams(
            dimension_semantics=("arbitrary", "arbitrary"), vmem_limit_bytes=VMEM_LIMIT),
        name="gla_seq%d" % seq_len,
    )(*args)
    return res


def _fnet_kernel(f_ref, cl_ref, sl_ref, cg_ref, sg_ref, o_ref, uc_ref, us_ref, *, seq_len):
    cg = cg_ref[...]
    sg = sg_ref[...]
    for grp in range(FNET_GROUPS):
        lo = grp * FNET_GROUP_DIM
        u = f_ref[:, lo:lo + FNET_GROUP_DIM].astype(BF16)
        uc_ref[:, lo:lo + FNET_GROUP_DIM] = _dot(u, cg).astype(BF16)
        us_ref[:, lo:lo + FNET_GROUP_DIM] = _dot(u, sg).astype(BF16)
    mixed = _dot(cl_ref[...], uc_ref[...]) - _dot(sl_ref[...], us_ref[...])
    o_ref[...] = mixed * (1.0 / math.sqrt(seq_len * FNET_GROUP_DIM))


def _dft_mats(n):
    j = np.arange(n, dtype=np.int64)
    ang = (2.0 * np.pi / n) * ((j[:, None] * j[None, :]) % n).astype(np.float64)
    return (jnp.asarray(np.cos(ang), dtype=F32).astype(BF16),
            jnp.asarray(np.sin(ang), dtype=F32).astype(BF16))


def _fnet(proj, *, n_seq, seq_len, row0):
    blk0 = row0 // seq_len
    cl, sl = _dft_mats(seq_len)
    cg, sg = _dft_mats(FNET_GROUP_DIM)
    kern = functools.partial(_fnet_kernel, seq_len=seq_len)
    return pl.pallas_call(
        kern,
        grid=(n_seq,),
        in_specs=[pl.BlockSpec((seq_len, FNET_DIM), lambda b: (blk0 + b, COL_F // FNET_DIM)),
                  pl.BlockSpec((seq_len, seq_len), lambda b: (0, 0)),
                  pl.BlockSpec((seq_len, seq_len), lambda b: (0, 0)),
                  pl.BlockSpec((FNET_GROUP_DIM, FNET_GROUP_DIM), lambda b: (0, 0)),
                  pl.BlockSpec((FNET_GROUP_DIM, FNET_GROUP_DIM), lambda b: (0, 0))],
        out_specs=pl.BlockSpec((seq_len, FNET_DIM), lambda b: (b, 0)),
        out_shape=jax.ShapeDtypeStruct((n_seq * seq_len, FNET_DIM), F32),
        scratch_shapes=[pltpu.VMEM((seq_len, FNET_DIM), BF16),
                        pltpu.VMEM((seq_len, FNET_DIM), BF16)],
        compiler_params=pltpu.CompilerParams(
            dimension_semantics=("arbitrary",), vmem_limit_bytes=VMEM_LIMIT),
        name="fnet_seq%d" % seq_len,
    )(proj, cl, sl, cg, sg)


def _merge_kernel(xc_ref, xl_ref, pos_ref, mod_ref, oc_ref, ol_ref, mc_ref, ml_ref,
                  g_ref, ga_ref, gb_ref, wbg_ref, wbf_ref, wo_ref, l1g_ref, l1b_ref, wr_ref, br_ref,
                  x1_ref, h2_ref, ridx_ref, rw_ref, *, n_ctx_tiles, alpha):
    i = pl.program_id(0)

    def compute(x, o, mx):
        g = g_ref[...]
        a = (o * (g * _sigmoid(g))).astype(BF16)
        gla_out = _dot(a, wbg_ref[...])
        fnet_out = _dot(mx.astype(BF16), wbf_ref[...])
        merged = _sigmoid(ga_ref[...]) * gla_out + _sigmoid(gb_ref[...]) * fnet_out
        mix = _dot(merged.astype(BF16), wo_ref[...])
        y = alpha * x + mod_ref[0, 2:3, :] * mix
        x1 = _layer_norm(y) * l1g_ref[...] + l1b_ref[...]
        x1_ref[...] = x1
        h2 = _layer_norm(x1) * (1.0 + mod_ref[0, 4:5, :]) + mod_ref[0, 3:4, :]
        _store_row_tiles(h2_ref, h2)

        logits = jnp.dot(h2, wr_ref[...], precision=HIGHEST, preferred_element_type=F32) + br_ref[...]
        lane_i = lax.broadcasted_iota(jnp.int32, logits.shape, 1)
        lane = lane_i.astype(F32)
        idx_out = jnp.zeros(logits.shape, F32)
        val_out = jnp.zeros(logits.shape, F32)
        top0 = None
        denom = None
        for kk in range(TOP_K):
            m = jnp.max(logits, axis=-1, keepdims=True)
            sel = jnp.min(jnp.where(logits == m, lane, float(LANES)), axis=-1, keepdims=True)
            if kk == 0:
                top0 = m
                p = jnp.ones_like(m)
                denom = p
            else:
                p = jnp.exp(m - top0)
                denom = denom + p
            idx_out = jnp.where(lane_i == kk, sel, idx_out)
            val_out = jnp.where(lane_i == kk, p, val_out)
            logits = jnp.where(lane == sel, -jnp.inf, logits)
        ridx_ref[...] = idx_out.astype(jnp.int32)
        rw_ref[...] = val_out / denom

    @pl.when(i < n_ctx_tiles)
    def _():
        compute(xc_ref[...], oc_ref[...], mc_ref[...])

    @pl.when(i >= n_ctx_tiles)
    def _():
        compute(xl_ref[...] + pos_ref[...], ol_ref[...], ml_ref[...])


def _merge(x_ctx, x_lat, pos, mod, o_ctx, o_lat, mixed_ctx, mixed_lat, proj,
           wbg, wbf, wo, l1g, l1b, wr, br, lat_len, alpha):
    t_ctx, t_lat = x_ctx.shape[0], x_lat.shape[0]
    t_all = t_ctx + t_lat
    tm = TM_MIX
    n_ctx_tiles = t_ctx // tm
    kern = functools.partial(_merge_kernel, n_ctx_tiles=n_ctx_tiles, alpha=alpha)
    specs = _token_specs(tm, n_ctx_tiles, lat_len // tm, mod.shape[0] - 1)
    ctx_map, lat_map = _group_maps(n_ctx_tiles)

    def const(shape):
        return pl.BlockSpec(shape, lambda i: (0,) * len(shape))

    in_specs = specs + [
        pl.BlockSpec((tm, GLA_DV), ctx_map),
        pl.BlockSpec((tm, GLA_DV), lat_map),
        pl.BlockSpec((tm, FNET_DIM), ctx_map),
        pl.BlockSpec((tm, FNET_DIM), lat_map),
        pl.BlockSpec((tm, GLA_DV), lambda i: (i, COL_G // GLA_DV)),
        pl.BlockSpec((tm, D_MODEL), lambda i: (i, COL_GATE_A // D_MODEL)),
        pl.BlockSpec((tm, D_MODEL), lambda i: (i, COL_GATE_B // D_MODEL)),
        const((GLA_DV, D_MODEL)), const((FNET_DIM, D_MODEL)), const((D_MODEL, D_MODEL)),
        const((1, D_MODEL)), const((1, D_MODEL)),
        const((D_MODEL, LANES)), const((1, LANES)),
    ]
    out_specs = [pl.BlockSpec((tm, D_MODEL), lambda i: (i, 0)),
                 pl.BlockSpec((tm, ROW_TILES, LANES), lambda i: (i, 0, 0)),
                 pl.BlockSpec((tm, LANES), lambda i: (i, 0)),
                 pl.BlockSpec((tm, LANES), lambda i: (i, 0))]
    out_shape = [jax.ShapeDtypeStruct((t_all, D_MODEL), F32),
                 jax.ShapeDtypeStruct((t_all, ROW_TILES, LANES), F32),
                 jax.ShapeDtypeStruct((t_all, LANES), jnp.int32),
                 jax.ShapeDtypeStruct((t_all, LANES), F32)]
    return pl.pallas_call(
        kern,
        grid=(t_all // tm,),
        in_specs=in_specs,
        out_specs=out_specs,
        out_shape=out_shape,
        compiler_params=pltpu.CompilerParams(
            dimension_semantics=("arbitrary",), vmem_limit_bytes=VMEM_LIMIT),
        name="merge_ln1_router",
    )(x_ctx, x_lat, pos, mod, o_ctx, o_lat, mixed_ctx, mixed_lat, proj, proj, proj,
      wbg, wbf, wo, l1g, l1b, wr, br)


def _sc_mesh():
    return plsc.VectorSubcoreMesh(core_axis_name="c", subcore_axis_name="s")


def _sc_worker_id():
    return lax.axis_index("s") * SC_CORES + lax.axis_index("c")


def _sc_scatter_rows(src, idx, n_out):
    n_src = src.shape[0]
    w = SC_WINDOW
    n_chunks = n_src // (SC_WORKERS * w)
    copies = idx.shape[1] // n_chunks
    assert n_chunks % 2 == 0 and idx.shape == (SC_WORKERS, copies * n_chunks, w)

    @functools.partial(
        pl.kernel, mesh=_sc_mesh(),
        out_type=jax.ShapeDtypeStruct((n_out, ROW_TILES, LANES), F32),
        scratch_types=[pltpu.VMEM((copies * n_chunks, w), jnp.int32),
                       pltpu.VMEM((2, w, ROW_TILES, LANES), F32),
                       pltpu.SemaphoreType.DMA((2,)),
                       pltpu.SemaphoreType.DMA((2,))],
        name="moe_dispatch_scatter")
    def k(src_hbm, idx_hbm, out_hbm, idx_v, rows_v, rsem, wsem):
        wid = _sc_worker_id()
        base = wid * (n_chunks * w)
        pltpu.sync_copy(idx_hbm.at[wid], idx_v)

        def read(j, slot):
            return pltpu.make_async_copy(src_hbm.at[pl.ds(base + j * w, w)], rows_v.at[slot],
                                         rsem.at[slot])

        def scatter(j, kk, slot):
            return pltpu.make_async_copy(rows_v.at[slot], out_hbm.at[idx_v.at[kk * n_chunks + j]],
                                         wsem.at[slot])

        read(0, 0).start()

        @pl.loop(0, n_chunks, step=2)
        def _(jj):
            read(jj, 0).wait()

            @pl.when(jj > 0)
            def _():
                for kk in range(copies):
                    scatter(jj - 1, kk, 1).wait()

            read(jj + 1, 1).start()
            for kk in range(copies):
                scatter(jj, kk, 0).start()
            read(jj + 1, 1).wait()
            for kk in range(copies):
                scatter(jj, kk, 0).wait()

            @pl.when(jj + 2 < n_chunks)
            def _():
                read(jj + 2, 0).start()

            for kk in range(copies):
                scatter(jj + 1, kk, 1).start()

        for kk in range(copies):
            scatter(n_chunks - 1, kk, 1).wait()

    return k(src, idx)


def _sc_gather_rows(table, idx):
    _, n_chunks, w = idx.shape
    assert n_chunks % 2 == 0 and idx.shape[0] == SC_WORKERS and w == SC_WINDOW
    n_out = SC_WORKERS * n_chunks * w

    @functools.partial(
        pl.kernel, mesh=_sc_mesh(),
        out_type=jax.ShapeDtypeStruct((n_out, ROW_TILES, LANES), F32),
        scratch_types=[pltpu.VMEM((n_chunks, w), jnp.int32),
                       pltpu.VMEM((2, w, ROW_TILES, LANES), F32),
                       pltpu.SemaphoreType.DMA((2,)),
                       pltpu.SemaphoreType.DMA((2,))],
        name="moe_combine_gather")
    def k(table_hbm, idx_hbm, out_hbm, idx_v, rows_v, gsem, wsem):
        wid = _sc_worker_id()
        base = wid * (n_chunks * w)
        pltpu.sync_copy(idx_hbm.at[wid], idx_v)

        def gather(j, slot):
            return pltpu.make_async_copy(table_hbm.at[idx_v.at[j]], rows_v.at[slot], gsem.at[slot])

        def write(j, slot):
            return pltpu.make_async_copy(rows_v.at[slot], out_hbm.at[pl.ds(base + j * w, w)],
                                         wsem.at[slot])

        gather(0, 0).start()

        @pl.loop(0, n_chunks, step=2)
        def _(jj):
            gather(jj, 0).wait()

            @pl.when(jj > 0)
            def _():
                write(jj - 1, 1).wait()

            gather(jj + 1, 1).start()
            write(jj, 0).start()
            gather(jj + 1, 1).wait()
            write(jj, 0).wait()

            @pl.when(jj + 2 < n_chunks)
            def _():
                gather(jj + 2, 0).start()

            write(jj + 1, 1).start()

        write(n_chunks - 1, 1).wait()

    return k(table, idx)


def _moe_kernel(be_ref, nu_ref, nv_ref, x_ref, wgu_ref, bgu_ref, wd_ref, bd_ref, o_ref,
                wgu_bf, wd_bf, xb_ref):
    b = pl.program_id(0)
    e = be_ref[b]
    prev = be_ref[jnp.maximum(b - 1, 0)]
    active = b < nu_ref[0]
    changed = (b == 0) | (e != prev)

    @pl.when(active & changed)
    def _():
        wgu_bf[...] = wgu_ref[0].astype(BF16)
        wd_bf[...] = wd_ref[0].astype(BF16)

    @pl.when(active)
    def _():
        valid = lax.broadcasted_iota(jnp.int32, (MOE_ROWS, LANES), 0) < nv_ref[b]
        for j in range(ROW_TILES):
            xb_ref[:, j * LANES:(j + 1) * LANES] = jnp.where(valid, x_ref[:, j, :], 0.0).astype(BF16)
        gu = _dot(xb_ref[...], wgu_bf[...]) + bgu_ref[0]
        gate = jnp.minimum(gu[:, :D_EXPERT], SWIGLU_LIMIT)
        up = jnp.clip(gu[:, D_EXPERT:], -SWIGLU_LIMIT, SWIGLU_LIMIT)
        glu = gate * _sigmoid(gate * SWIGLU_ALPHA)
        act = ((up + 1.0) * glu).astype(BF16)
        _store_row_tiles(o_ref, _dot(act, wd_bf[...]) + bd_ref[0])


def _moe(block_expert, n_used, n_valid, xs, w_gate_up, b_gate_up, w_down, b_down):
    p_rows = xs.shape[0]
    n_blocks = p_rows // MOE_ROWS

    def blk(b, be, nu, nv):
        return jnp.minimum(b, nu[0] - 1)

    def expert(b, be, nu, nv):
        return (be[blk(b, be, nu, nv)], 0, 0)

    def rows(b, be, nu, nv):
        return (blk(b, be, nu, nv), 0, 0)

    grid_spec = pltpu.PrefetchScalarGridSpec(
        num_scalar_prefetch=3,
        grid=(n_blocks,),
        in_specs=[
            pl.BlockSpec((MOE_ROWS, ROW_TILES, LANES), rows),
            pl.BlockSpec((1, D_MODEL, 2 * D_EXPERT), expert),
            pl.BlockSpec((1, 1, 2 * D_EXPERT), expert),
            pl.BlockSpec((1, D_EXPERT, D_MODEL), expert),
            pl.BlockSpec((1, 1, D_MODEL), expert),
        ],
        out_specs=pl.BlockSpec((MOE_ROWS, ROW_TILES, LANES), rows),
        scratch_shapes=[pltpu.VMEM((D_MODEL, 2 * D_EXPERT), BF16),
                        pltpu.VMEM((D_EXPERT, D_MODEL), BF16),
                        pltpu.VMEM((MOE_ROWS, D_MODEL), BF16)],
    )
    return pl.pallas_call(
        _moe_kernel,
        grid_spec=grid_spec,
        out_shape=jax.ShapeDtypeStruct((p_rows, ROW_TILES, LANES), F32),
        compiler_params=pltpu.CompilerParams(
            dimension_semantics=("arbitrary",), vmem_limit_bytes=VMEM_LIMIT),
        name="moe_grouped_mlp",
    )(block_expert, n_used, n_valid, xs, w_gate_up,
      b_gate_up.reshape(N_EXPERTS, 1, 2 * D_EXPERT), w_down, b_down.reshape(N_EXPERTS, 1, D_MODEL))


def _combine_kernel(x1_ref, y0_ref, y1_ref, y2_ref, y3_ref, rw_ref, mod_ref, g_ref, b_ref, o_ref,
                    *, alpha):
    rw = rw_ref[...]
    y_refs = (y0_ref, y1_ref, y2_ref, y3_ref)
    pieces = []
    for j in range(ROW_TILES):
        acc = rw[:, 0:1] * y_refs[0][:, j, :]
        for kk in range(1, TOP_K):
            acc = acc + rw[:, kk:kk + 1] * y_refs[kk][:, j, :]
        pieces.append(acc)
    ff = jnp.concatenate(pieces, axis=-1)
    y = alpha * x1_ref[...] + mod_ref[0, 5:6, :] * ff
    o_ref[...] = _layer_norm(y) * g_ref[...] + b_ref[...]


def _combine(x1, yg, rw, mod, l2g, l2b, *, row0, n_rows, mod_map, alpha):
    tm = TM_MIX
    t0 = row0 // tm
    tiles_all = x1.shape[0] // tm
    kern = functools.partial(_combine_kernel, alpha=alpha)

    def y_spec(kk):
        return pl.BlockSpec((tm, ROW_TILES, LANES), lambda i: (kk * tiles_all + t0 + i, 0, 0))

    return pl.pallas_call(
        kern,
        grid=(n_rows // tm,),
        in_specs=[pl.BlockSpec((tm, D_MODEL), lambda i: (t0 + i, 0))]
        + [y_spec(kk) for kk in range(TOP_K)]
        + [pl.BlockSpec((tm, LANES), lambda i: (t0 + i, 0)),
           pl.BlockSpec((1, N_MOD, D_MODEL), mod_map),
           pl.BlockSpec((1, D_MODEL), lambda i: (0, 0)),
           pl.BlockSpec((1, D_MODEL), lambda i: (0, 0))],
        out_specs=pl.BlockSpec((tm, D_MODEL), lambda i: (i, 0)),
        out_shape=jax.ShapeDtypeStruct((n_rows, D_MODEL), F32),
        compiler_params=pltpu.CompilerParams(
            dimension_semantics=("arbitrary",), vmem_limit_bytes=VMEM_LIMIT),
        name="combine_ln2",
    )(x1, yg, yg, yg, yg, rw, mod, l2g, l2b)


def _routing_tables(ridx, n_blocks):
    idx = ridx[:, :TOP_K]
    experts = jnp.arange(N_EXPERTS, dtype=jnp.int32)
    onehot = jnp.any(idx[:, :, None] == experts[None, None, :], axis=1).astype(jnp.int32)
    incl = jnp.cumsum(onehot, axis=0)
    rank = incl - onehot
    counts = incl[-1]
    blocks_per = (counts + MOE_ROWS - 1) // MOE_ROWS
    bends = jnp.cumsum(blocks_per)
    bstarts = bends - blocks_per
    dest = bstarts[idx] * MOE_ROWS + jnp.take_along_axis(rank, idx, axis=1)
    blocks = jnp.arange(n_blocks, dtype=jnp.int32)
    block_expert = jnp.minimum(
        jnp.sum((bends[None, :] <= blocks[:, None]).astype(jnp.int32), axis=1), N_EXPERTS - 1)
    n_used = bends[-1:].astype(jnp.int32)
    n_valid = jnp.clip(counts[block_expert] - (blocks - bstarts[block_expert]) * MOE_ROWS,
                       0, MOE_ROWS).astype(jnp.int32)
    return dest.astype(jnp.int32), block_expert.astype(jnp.int32), n_used, n_valid


def _pos_embed_2d(n_tokens):
    rows = n_tokens // GRID_W
    r = jnp.repeat(jnp.arange(rows), GRID_W).astype(F32)
    col = jnp.tile(jnp.arange(GRID_W), rows).astype(F32)
    quarter = D_MODEL // 4
    omega = 1.0 / (10000.0 ** (jnp.arange(quarter, dtype=F32) / quarter))
    er = r[:, None] * omega
    ec = col[:, None] * omega
    return jnp.concatenate([jnp.sin(er), jnp.cos(er), jnp.sin(ec), jnp.cos(ec)], axis=-1)


def _reorder_w_in(w):
    o_r = 2 * GLA_DK + 2 * GLA_DV
    o_f = o_r + DECAY_RANK
    o_gate = o_f + FNET_DIM
    pad = jnp.zeros((w.shape[0], LANES - DECAY_RANK), w.dtype)
    return jnp.concatenate([w[:, :o_r], w[:, o_gate:], w[:, o_f:o_gate], w[:, o_r:o_f], pad], axis=1)


def kernel(x_prompt, x_sample, state_gla, c, c_ctx, w_ada, b_ada, w_in, w_dec_fwd, b_dec_fwd,
           w_dec_bwd, b_dec_bwd, gla_norm_g, w_br_gla, w_br_fnet, w_out, ln1_g, ln1_b, w_router,
           b_router, w_gate_up, b_gate_up, w_down, b_down, ln2_g, ln2_b):
    n_req, ctx_len, _ = x_prompt.shape
    n_lat, lat_len, _ = x_sample.shape
    depth = w_in.shape[0]
    alpha = (2.0 * depth) ** 0.25
    t_ctx = n_req * ctx_len
    t_lat = n_lat * lat_len
    t_all = t_ctx + t_lat

    x_ctx = x_prompt.reshape(t_ctx, D_MODEL)
    x_lat = x_sample.reshape(t_lat, D_MODEL)
    pos = _pos_embed_2d(lat_len)
    zero_pos = jnp.zeros_like(pos)

    cond_rows = -(-(n_lat + 1) // SUBLANES) * SUBLANES
    cond = jnp.zeros((cond_rows, D_MODEL), F32).at[:n_lat].set(c).at[cond_rows - 1].set(c_ctx)

    n_moe_blocks = (t_all * TOP_K) // MOE_ROWS + N_EXPERTS
    tok_chunks = t_all // (SC_WORKERS * SC_WINDOW)
    states = []
    for l in range(depth):
        mod = _ada(cond, w_ada[l], b_ada[l]).reshape(cond_rows, N_MOD, D_MODEL)
        layer_pos = pos if l == 0 else zero_pos
        proj = _inproj(x_ctx, x_lat, layer_pos, mod, _reorder_w_in(w_in[l]).astype(BF16), lat_len)

        def pad_dec(w):
            return jnp.zeros((LANES, GLA_DK), F32).at[:DECAY_RANK].set(w)

        dec = (pad_dec(w_dec_fwd[l]), b_dec_fwd[l].reshape(1, GLA_DK),
               pad_dec(w_dec_bwd[l]), b_dec_bwd[l].reshape(1, GLA_DK),
               gla_norm_g[l].reshape(1, DV_HEAD))
        o_ctx, s_new = _gla(proj, *dec, None, n_seq=n_req, seq_len=ctx_len, row0=0, emit_state=True)
        (o_lat,) = _gla(proj, *dec, state_gla[:, l], n_seq=n_lat, seq_len=lat_len, row0=t_ctx,
                        emit_state=False)
        states.append(s_new)

        mixed_ctx = _fnet(proj, n_seq=n_req, seq_len=ctx_len, row0=0)
        mixed_lat = _fnet(proj, n_seq=n_lat, seq_len=lat_len, row0=t_ctx)

        wr = jnp.zeros((D_MODEL, LANES), F32).at[:, :N_EXPERTS].set(w_router[l])
        br = jnp.full((1, LANES), -1e30, F32).at[0, :N_EXPERTS].set(b_router[l])
        x1, h2, ridx, rw = _merge(
            x_ctx, x_lat, layer_pos, mod, o_ctx, o_lat, mixed_ctx, mixed_lat, proj,
            w_br_gla[l].astype(BF16), w_br_fnet[l].astype(BF16), w_out[l].astype(BF16),
            ln1_g[l].reshape(1, D_MODEL), ln1_b[l].reshape(1, D_MODEL), wr, br, lat_len, alpha)

        dest, block_expert, n_used, n_valid = _routing_tables(ridx, n_moe_blocks)
        dest_k = dest.T.reshape(TOP_K, SC_WORKERS, tok_chunks, SC_WINDOW)
        scatter_idx = dest_k.transpose(1, 0, 2, 3).reshape(SC_WORKERS, TOP_K * tok_chunks, SC_WINDOW)
        xs = _sc_scatter_rows(h2, scatter_idx, n_moe_blocks * MOE_ROWS)
        yb = _moe(block_expert, n_used, n_valid, xs, w_gate_up[l], b_gate_up[l], w_down[l], b_down[l])
        gather_idx = dest.T.reshape(SC_WORKERS, TOP_K * tok_chunks, SC_WINDOW)
        yg = _sc_gather_rows(yb, gather_idx)

        l2g = ln2_g[l].reshape(1, D_MODEL)
        l2b = ln2_b[l].reshape(1, D_MODEL)
        tiles_per_seq = lat_len // TM_MIX
        x_ctx = _combine(x1, yg, rw, mod, l2g, l2b, row0=0, n_rows=t_ctx,
                         mod_map=lambda i: (cond_rows - 1, 0, 0), alpha=alpha)
        x_lat = _combine(x1, yg, rw, mod, l2g, l2b, row0=t_ctx, n_rows=t_lat,
                         mod_map=lambda i: (i // tiles_per_seq, 0, 0), alpha=alpha)

    y_prompt = x_ctx.reshape(x_prompt.shape)
    y_sample = x_lat.reshape(x_sample.shape)
    new_state = jnp.stack(states, axis=1).astype(x_prompt.dtype)
    return (y_prompt, y_sample, new_state)
```

```python
import functools
import math

import numpy as np
import jax
import jax.numpy as jnp
from jax import lax
from jax.experimental import pallas as pl
from jax.experimental.pallas import tpu as pltpu
from jax.experimental.pallas import tpu_sc as plsc

F32 = jnp.float32
BF16 = jnp.bfloat16

D_MODEL = 1024
GRID_W = 64
GLA_HEADS = 4
DK_HEAD = 128
DV_HEAD = 256
GLA_DK = GLA_HEADS * DK_HEAD
GLA_DV = GLA_HEADS * DV_HEAD
DECAY_RANK = 16
GATE_NORMALIZER = 16.0
FNET_GROUPS = 4
FNET_GROUP_DIM = 128
FNET_DIM = FNET_GROUPS * FNET_GROUP_DIM
N_EXPERTS = 32
TOP_K = 4
D_EXPERT = 1024
SWIGLU_LIMIT = 7.0
SWIGLU_ALPHA = 1.702
LN_EPS = 1e-6
N_MOD = 6

LANES = 128
SUBLANES = 8
ROW_TILES = D_MODEL // LANES
COL_Q = 0
COL_K = GLA_DK
COL_V = 2 * GLA_DK
COL_G = COL_V + GLA_DV
COL_GATE_A = COL_G + GLA_DV
COL_GATE_B = COL_GATE_A + D_MODEL
COL_F = COL_GATE_B + D_MODEL
COL_R = COL_F + FNET_DIM
PROJ_COLS = COL_R + LANES

GLA_CHUNK = 128
GLA_LEAF = 16
TM_PROJ = 512
TN_PROJ = 1152
TM_MIX = 256
MOE_ROWS = 256
VMEM_LIMIT = 56 * 1024 * 1024

SC_CORES = 2
SC_SUBCORES = 16
SC_WORKERS = SC_CORES * SC_SUBCORES
SC_WINDOW = 32

HIGHEST = lax.Precision.HIGHEST


def _layer_norm(x):
    mu = jnp.mean(x, axis=-1, keepdims=True)
    xc = x - mu
    var = jnp.mean(xc * xc, axis=-1, keepdims=True)
    return xc * lax.rsqrt(var + LN_EPS)


def _sigmoid(x):
    return 1.0 / (1.0 + jnp.exp(-x))


def _log_sigmoid(z):
    return jnp.minimum(z, 0.0) - jnp.log1p(jnp.exp(-jnp.abs(z)))


def _dot(a, b):
    return jnp.dot(a, b, preferred_element_type=F32)


def _dot_nt(a, b):
    return lax.dot_general(a, b, (((1,), (1,)), ((), ())), preferred_element_type=F32)


def _dot_tn(a, b):
    return lax.dot_general(a, b, (((0,), (0,)), ((), ())), preferred_element_type=F32)


def _row_tile_slice(j, n_rows):
    return pl.ds(j, n_rows, stride=ROW_TILES)


def _store_row_tiles(ref, val):
    for j in range(ROW_TILES):
        ref[_row_tile_slice(j, val.shape[0]), :] = val[:, j * LANES:(j + 1) * LANES]


def _ada_kernel(c_ref, w_ref, b_ref, o_ref):
    c = c_ref[...]
    s = c * _sigmoid(c)
    o_ref[...] = _dot(s.astype(BF16), w_ref[...].astype(BF16)) + b_ref[...]


def _ada(cond, w_ada, b_ada):
    rows = cond.shape[0]
    n = w_ada.shape[1]
    tn = 1536
    return pl.pallas_call(
        _ada_kernel,
        grid=(n // tn,),
        in_specs=[pl.BlockSpec((rows, D_MODEL), lambda j: (0, 0)),
                  pl.BlockSpec((D_MODEL, tn), lambda j: (0, j)),
                  pl.BlockSpec((1, tn), lambda j: (0, j))],
        out_specs=pl.BlockSpec((rows, tn), lambda j: (0, j)),
        out_shape=jax.ShapeDtypeStruct((rows, n), F32),
        compiler_params=pltpu.CompilerParams(vmem_limit_bytes=VMEM_LIMIT),
        name="ada_mod",
    )(cond, w_ada, b_ada.reshape(1, n))


def _group_maps(n_ctx_tiles):
    def ctx_map(i, *_):
        return (jnp.minimum(i, n_ctx_tiles - 1), 0)

    def lat_map(i, *_):
        return (jnp.maximum(i - n_ctx_tiles, 0), 0)

    return ctx_map, lat_map


def _token_specs(tm, n_ctx_tiles, tiles_per_latent_seq, ctx_mod_row):
    ctx_map, lat_map = _group_maps(n_ctx_tiles)

    def pos_map(i, *_):
        return (jnp.maximum(i - n_ctx_tiles, 0) % tiles_per_latent_seq, 0)

    def mod_map(i, *_):
        return (jnp.where(i < n_ctx_tiles, ctx_mod_row,
                          jnp.maximum(i - n_ctx_tiles, 0) // tiles_per_latent_seq), 0, 0)

    return [pl.BlockSpec((tm, D_MODEL), ctx_map),
            pl.BlockSpec((tm, D_MODEL), lat_map),
            pl.BlockSpec((tm, D_MODEL), pos_map),
            pl.BlockSpec((1, N_MOD, D_MODEL), mod_map)]


def _inproj_kernel(xc_ref, xl_ref, pos_ref, mod_ref, w_ref, o_ref, h_ref, *, n_ctx_tiles):
    i = pl.program_id(0)
    j = pl.program_id(1)

    def modulate(x):
        h = _layer_norm(x) * (1.0 + mod_ref[0, 1:2, :]) + mod_ref[0, 0:1, :]
        h_ref[...] = h.astype(BF16)

    @pl.when((j == 0) & (i < n_ctx_tiles))
    def _():
        modulate(xc_ref[...])

    @pl.when((j == 0) & (i >= n_ctx_tiles))
    def _():
        modulate(xl_ref[...] + pos_ref[...])

    o_ref[...] = _dot(h_ref[...], w_ref[...])


def _inproj(x_ctx, x_lat, pos, mod, w_in_bf, lat_len):
    t_ctx, t_lat = x_ctx.shape[0], x_lat.shape[0]
    n_ctx_tiles = t_ctx // TM_PROJ
    n_tiles = (t_ctx + t_lat) // TM_PROJ
    kern = functools.partial(_inproj_kernel, n_ctx_tiles=n_ctx_tiles)
    specs = _token_specs(TM_PROJ, n_ctx_tiles, lat_len // TM_PROJ, mod.shape[0] - 1)
    return pl.pallas_call(
        kern,
        grid=(n_tiles, PROJ_COLS // TN_PROJ),
        in_specs=specs + [pl.BlockSpec((D_MODEL, TN_PROJ), lambda i, j: (0, j))],
        out_specs=pl.BlockSpec((TM_PROJ, TN_PROJ), lambda i, j: (i, j)),
        out_shape=jax.ShapeDtypeStruct((t_ctx + t_lat, PROJ_COLS), F32),
        scratch_shapes=[pltpu.VMEM((TM_PROJ, D_MODEL), BF16)],
        compiler_params=pltpu.CompilerParams(
            dimension_semantics=("arbitrary", "arbitrary"), vmem_limit_bytes=VMEM_LIMIT),
        name="ln_inproj",
    )(x_ctx, x_lat, pos, mod, w_in_bf)


def _gla_kernel(*refs, seq_len, has_s0, emit_state):
    it = iter(refs)
    q_ref, k_ref, v_ref, r_ref = next(it), next(it), next(it), next(it)
    wdf_ref, bdf_ref, wdb_ref, bdb_ref, g_ref = next(it), next(it), next(it), next(it), next(it)
    s0_ref = next(it) if has_s0 else None
    o_ref = next(it)
    sout_ref = next(it) if emit_state else None
    cum_ref, a_ref, qi_ref, ko_ref, dec_ref, op_ref, st_ref = (next(it) for _ in range(7))

    C = GLA_CHUNK
    n_chunks = seq_len // C
    q_scale = DK_HEAD ** -0.5

    def rows(n):
        if isinstance(n, int):
            return pl.ds(n * C, C)
        return pl.ds(pl.multiple_of(n * C, C), C)

    def dec_rows(n):
        if isinstance(n, int):
            return pl.ds(n * SUBLANES, SUBLANES)
        return pl.ds(pl.multiple_of(n * SUBLANES, SUBLANES), SUBLANES)

    def loop(body):
        if n_chunks <= 2:
            for n in range(n_chunks):
                body(n)
        else:
            def step(n, carry):
                body(n)
                return carry
            lax.fori_loop(0, n_chunks, step, 0)

    rt = lax.broadcasted_iota(jnp.int32, (C, C), 0)
    ct = lax.broadcasted_iota(jnp.int32, (C, C), 1)
    tri = ((rt >= ct).astype(F32), (ct >= rt).astype(F32))
    row_id = lax.broadcasted_iota(jnp.int32, (C, DK_HEAD), 0)

    r = r_ref[...]
    for d, (w_ref, b_ref) in enumerate(((wdf_ref, bdf_ref), (wdb_ref, bdb_ref))):
        z = jnp.dot(r, w_ref[...], precision=HIGHEST, preferred_element_type=F32) + b_ref[...]
        cum_ref[d] = _log_sigmoid(z) * (1.0 / GATE_NORMALIZER)

    def cumsum_chunk(n):
        for d in range(2):
            la = cum_ref[d, rows(n), :]
            cum_ref[d, rows(n), :] = jnp.dot(tri[d], la, precision=HIGHEST,
                                             preferred_element_type=F32)

    loop(cumsum_chunk)

    def block_rows(cum, first, step, count):
        span = C // count
        parts = [jnp.broadcast_to(cum[first + p * step:first + p * step + 1, :], (span, DK_HEAD))
                 for p in range(count)]
        return parts[0] if count == 1 else jnp.concatenate(parts, axis=0)

    def scores(n, d):
        cum = cum_ref[d, rows(n), :]
        q = q_ref[rows(n), :] * q_scale
        k = k_ref[rows(n), :]
        acc = None
        blk = C // 2
        while blk >= GLA_LEAF:
            pairs = C // (2 * blk)
            bnd = blk - 1 if d == 0 else blk
            w = jnp.exp(-jnp.abs(cum - block_rows(cum, bnd, 2 * blk, pairs)))
            is_q = ((row_id // blk) % 2) == (1 if d == 0 else 0)
            qb = jnp.where(is_q, q * w, 0.0).astype(BF16)
            kb = jnp.where(is_q, 0.0, k * w).astype(BF16)
            s = _dot_nt(qb, kb)
            if pairs > 1:
                s = jnp.where((rt // (2 * blk)) == (ct // (2 * blk)), s, 0.0)
            acc = s if acc is None else acc + s
            blk //= 2
        leaf = GLA_LEAF
        mid = leaf // 2 - 1 if d == 0 else leaf // 2
        e = cum - block_rows(cum, mid, leaf, C // leaf)
        s = _dot_nt((q * jnp.exp(e)).astype(BF16), (k * jnp.exp(-e)).astype(BF16))
        order = (rt >= ct) if d == 0 else (ct >= rt)
        acc = acc + jnp.where(((rt // leaf) == (ct // leaf)) & order, s, 0.0)
        a_ref[d, rows(n), :] = acc.astype(BF16)
        end = cum[C - 1:C, :] if d == 0 else cum[0:1, :]
        qi_ref[d, rows(n), :] = (q * jnp.exp(cum)).astype(BF16)
        ko_ref[d, rows(n), :] = (k * jnp.exp(end - cum)).astype(BF16)
        dec_ref[d, dec_rows(n), :] = jnp.broadcast_to(jnp.exp(end), (SUBLANES, DK_HEAD))

    def scores_chunk(n):
        scores(n, 0)
        scores(n, 1)

    loop(scores_chunk)

    for d in range(2):
        if has_s0:
            st_ref[d] = s0_ref[0, d, 0].T
        else:
            st_ref[d] = jnp.zeros((DV_HEAD, DK_HEAD), F32)

    def scan(n, d):
        v = v_ref[rows(n), :].astype(BF16)
        st = st_ref[d]
        o = _dot(a_ref[d, rows(n), :], v) + _dot_nt(qi_ref[d, rows(n), :], st.astype(BF16))
        st_ref[d] = st * dec_ref[d, dec_rows(n), :][0:1, :] + _dot_tn(v, ko_ref[d, rows(n), :])
        op_ref[d, rows(n), :] = o

    def scan_chunk(m):
        scan(m, 0)
        scan(n_chunks - 1 - m, 1)

    loop(scan_chunk)
    if emit_state:
        sout_ref[0, 0, 0] = st_ref[0].T
        sout_ref[0, 1, 0] = st_ref[1].T

    g = g_ref[...]

    def finish_chunk(n):
        o = op_ref[0, rows(n), :] + op_ref[1, rows(n), :]
        ms = jnp.mean(o * o, axis=-1, keepdims=True)
        o_ref[rows(n), :] = o * lax.rsqrt(ms + LN_EPS) * g

    loop(finish_chunk)


def _gla(proj, wdf, bdf, wdb, bdb, g, s0, *, n_seq, seq_len, row0, emit_state):
    has_s0 = s0 is not None
    blk0 = row0 // seq_len
    kern = functools.partial(_gla_kernel, seq_len=seq_len, has_s0=has_s0, emit_state=emit_state)
    in_specs = [
        pl.BlockSpec((seq_len, DK_HEAD), lambda b, h: (blk0 + b, COL_Q // DK_HEAD + h)),
        pl.BlockSpec((seq_len, DK_HEAD), lambda b, h: (blk0 + b, COL_K // DK_HEAD + h)),
        pl.BlockSpec((seq_len, DV_HEAD), lambda b, h: (blk0 + b, COL_V // DV_HEAD + h)),
        pl.BlockSpec((seq_len, LANES), lambda b, h: (blk0 + b, COL_R // LANES)),
        pl.BlockSpec((LANES, DK_HEAD), lambda b, h: (0, h)),
        pl.BlockSpec((1, DK_HEAD), lambda b, h: (0, h)),
        pl.BlockSpec((LANES, DK_HEAD), lambda b, h: (0, h)),
        pl.BlockSpec((1, DK_HEAD), lambda b, h: (0, h)),
        pl.BlockSpec((1, DV_HEAD), lambda b, h: (0, 0)),
    ]
    args = [proj, proj, proj, proj, wdf, bdf, wdb, bdb, g]
    if has_s0:
        in_specs.append(pl.BlockSpec((1, 2, 1, DK_HEAD, DV_HEAD), lambda b, h: (b, 0, h, 0, 0)))
        args.append(s0)
    out_specs = [pl.BlockSpec((seq_len, DV_HEAD), lambda b, h: (b, h))]
    out_shape = [jax.ShapeDtypeStruct((n_seq * seq_len, GLA_DV), F32)]
    if emit_state:
        out_specs.append(pl.BlockSpec((1, 2, 1, DK_HEAD, DV_HEAD), lambda b, h: (b, 0, h, 0, 0)))
        out_shape.append(jax.ShapeDtypeStruct((n_seq, 2, GLA_HEADS, DK_HEAD, DV_HEAD), F32))

    res = pl.pallas_call(
        kern,
        grid=(n_seq, GLA_HEADS),
        in_specs=in_specs,
        out_specs=out_specs,
        out_shape=out_shape,
        scratch_shapes=[pltpu.VMEM((2, seq_len, DK_HEAD), F32),
                        pltpu.VMEM((2, seq_len, GLA_CHUNK), BF16),
                        pltpu.VMEM((2, seq_len, DK_HEAD), BF16),
                        pltpu.VMEM((2, seq_len, DK_HEAD), BF16),
                        pltpu.VMEM((2, seq_len // GLA_CHUNK * SUBLANES, DK_HEAD), F32),
                        pltpu.VMEM((2, seq_len, DV_HEAD), F32),
                        pltpu.VMEM((2, DV_HEAD, DK_HEAD), F32)],
        compiler_params=pltpu.CompilerParams(
            dimension_semantics=("arbitrary", "arbitrary"), vmem_limit_bytes=VMEM_LIMIT),
        name="gla_seq%d" % seq_len,
    )(*args)
    return res


def _fnet_kernel(f_ref, cl_ref, sl_ref, cg_ref, sg_ref, o_ref, uc_ref, us_ref, *, seq_len):
    cg = cg_ref[...]
    sg = sg_ref[...]
    for grp in range(FNET_GROUPS):
        lo = grp * FNET_GROUP_DIM
        u = f_ref[:, lo:lo + FNET_GROUP_DIM].astype(BF16)
        uc_ref[:, lo:lo + FNET_GROUP_DIM] = _dot(u, cg).astype(BF16)
        us_ref[:, lo:lo + FNET_GROUP_DIM] = _dot(u, sg).astype(BF16)
    mixed = _dot(cl_ref[...], uc_ref[...]) - _dot(sl_ref[...], us_ref[...])
    o_ref[...] = mixed * (1.0 / math.sqrt(seq_len * FNET_GROUP_DIM))


def _dft_mats(n):
    j = np.arange(n, dtype=np.int64)
    ang = (2.0 * np.pi / n) * ((j[:, None] * j[None, :]) % n).astype(np.float64)
    return (jnp.asarray(np.cos(ang), dtype=F32).astype(BF16),
            jnp.asarray(np.sin(ang), dtype=F32).astype(BF16))


def _fnet(proj, *, n_seq, seq_len, row0):
    blk0 = row0 // seq_len
    cl, sl = _dft_mats(seq_len)
    cg, sg = _dft_mats(FNET_GROUP_DIM)
    kern = functools.partial(_fnet_kernel, seq_len=seq_len)
    return pl.pallas_call(
        kern,
        grid=(n_seq,),
        in_specs=[pl.BlockSpec((seq_len, FNET_DIM), lambda b: (blk0 + b, COL_F // FNET_DIM)),
                  pl.BlockSpec((seq_len, seq_len), lambda b: (0, 0)),
                  pl.BlockSpec((seq_len, seq_len), lambda b: (0, 0)),
                  pl.BlockSpec((FNET_GROUP_DIM, FNET_GROUP_DIM), lambda b: (0, 0)),
                  pl.BlockSpec((FNET_GROUP_DIM, FNET_GROUP_DIM), lambda b: (0, 0))],
        out_specs=pl.BlockSpec((seq_len, FNET_DIM), lambda b: (b, 0)),
        out_shape=jax.ShapeDtypeStruct((n_seq * seq_len, FNET_DIM), F32),
        scratch_shapes=[pltpu.VMEM((seq_len, FNET_DIM), BF16),
                        pltpu.VMEM((seq_len, FNET_DIM), BF16)],
        compiler_params=pltpu.CompilerParams(
            dimension_semantics=("arbitrary",), vmem_limit_bytes=VMEM_LIMIT),
        name="fnet_seq%d" % seq_len,
    )(proj, cl, sl, cg, sg)


def _merge_kernel(xc_ref, xl_ref, pos_ref, mod_ref, oc_ref, ol_ref, mc_ref, ml_ref,
                  g_ref, ga_ref, gb_ref, wbg_ref, wbf_ref, wo_ref, l1g_ref, l1b_ref, wr_ref, br_ref,
                  x1_ref, h2_ref, ridx_ref, rw_ref, *, n_ctx_tiles, alpha):
    i = pl.program_id(0)

    def compute(x, o, mx):
        g = g_ref[...]
        a = (o * (g * _sigmoid(g))).astype(BF16)
        gla_out = _dot(a, wbg_ref[...])
        fnet_out = _dot(mx.astype(BF16), wbf_ref[...])
        merged = _sigmoid(ga_ref[...]) * gla_out + _sigmoid(gb_ref[...]) * fnet_out
        mix = _dot(merged.astype(BF16), wo_ref[...])
        y = alpha * x + mod_ref[0, 2:3, :] * mix
        x1 = _layer_norm(y) * l1g_ref[...] + l1b_ref[...]
        x1_ref[...] = x1
        h2 = _layer_norm(x1) * (1.0 + mod_ref[0, 4:5, :]) + mod_ref[0, 3:4, :]
        _store_row_tiles(h2_ref, h2)

        logits = jnp.dot(h2, wr_ref[...], precision=HIGHEST, preferred_element_type=F32) + br_ref[...]
        lane_i = lax.broadcasted_iota(jnp.int32, logits.shape, 1)
        lane = lane_i.astype(F32)
        idx_out = jnp.zeros(logits.shape, F32)
        val_out = jnp.zeros(logits.shape, F32)
        top0 = None
        denom = None
        for kk in range(TOP_K):
            m = jnp.max(logits, axis=-1, keepdims=True)
            sel = jnp.min(jnp.where(logits == m, lane, float(LANES)), axis=-1, keepdims=True)
            if kk == 0:
                top0 = m
                p = jnp.ones_like(m)
                denom = p
            else:
                p = jnp.exp(m - top0)
                denom = denom + p
            idx_out = jnp.where(lane_i == kk, sel, idx_out)
            val_out = jnp.where(lane_i == kk, p, val_out)
            logits = jnp.where(lane == sel, -jnp.inf, logits)
        ridx_ref[...] = idx_out.astype(jnp.int32)
        rw_ref[...] = val_out / denom

    @pl.when(i < n_ctx_tiles)
    def _():
        compute(xc_ref[...], oc_ref[...], mc_ref[...])

    @pl.when(i >= n_ctx_tiles)
    def _():
        compute(xl_ref[...] + pos_ref[...], ol_ref[...], ml_ref[...])


def _merge(x_ctx, x_lat, pos, mod, o_ctx, o_lat, mixed_ctx, mixed_lat, proj,
           wbg, wbf, wo, l1g, l1b, wr, br, lat_len, alpha):
    t_ctx, t_lat = x_ctx.shape[0], x_lat.shape[0]
    t_all = t_ctx + t_lat
    tm = TM_MIX
    n_ctx_tiles = t_ctx // tm
    kern = functools.partial(_merge_kernel, n_ctx_tiles=n_ctx_tiles, alpha=alpha)
    specs = _token_specs(tm, n_ctx_tiles, lat_len // tm, mod.shape[0] - 1)
    ctx_map, lat_map = _group_maps(n_ctx_tiles)

    def const(shape):
        return pl.BlockSpec(shape, lambda i: (0,) * len(shape))

    in_specs = specs + [
        pl.BlockSpec((tm, GLA_DV), ctx_map),
        pl.BlockSpec((tm, GLA_DV), lat_map),
        pl.BlockSpec((tm, FNET_DIM), ctx_map),
        pl.BlockSpec((tm, FNET_DIM), lat_map),
        pl.BlockSpec((tm, GLA_DV), lambda i: (i, COL_G // GLA_DV)),
        pl.BlockSpec((tm, D_MODEL), lambda i: (i, COL_GATE_A // D_MODEL)),
        pl.BlockSpec((tm, D_MODEL), lambda i: (i, COL_GATE_B // D_MODEL)),
        const((GLA_DV, D_MODEL)), const((FNET_DIM, D_MODEL)), const((D_MODEL, D_MODEL)),
        const((1, D_MODEL)), const((1, D_MODEL)),
        const((D_MODEL, LANES)), const((1, LANES)),
    ]
    out_specs = [pl.BlockSpec((tm, D_MODEL), lambda i: (i, 0)),
                 pl.BlockSpec((tm * ROW_TILES, LANES), lambda i: (i, 0)),
                 pl.BlockSpec((tm, LANES), lambda i: (i, 0)),
                 pl.BlockSpec((tm, LANES), lambda i: (i, 0))]
    out_shape = [jax.ShapeDtypeStruct((t_all, D_MODEL), F32),
                 jax.ShapeDtypeStruct((t_all * ROW_TILES, LANES), F32),
                 jax.ShapeDtypeStruct((t_all, LANES), jnp.int32),
                 jax.ShapeDtypeStruct((t_all, LANES), F32)]
    return pl.pallas_call(
        kern,
        grid=(t_all // tm,),
        in_specs=in_specs,
        out_specs=out_specs,
        out_shape=out_shape,
        compiler_params=pltpu.CompilerParams(
            dimension_semantics=("arbitrary",), vmem_limit_bytes=VMEM_LIMIT),
        name="merge_ln1_router",
    )(x_ctx, x_lat, pos, mod, o_ctx, o_lat, mixed_ctx, mixed_lat, proj, proj, proj,
      wbg, wbf, wo, l1g, l1b, wr, br)


def _sc_mesh():
    return plsc.VectorSubcoreMesh(core_axis_name="c", subcore_axis_name="s")


def _sc_worker_id():
    return lax.axis_index("s") * SC_CORES + lax.axis_index("c")


def _sc_scatter_rows(src, idx, n_out):
    n_src = src.shape[0]
    w = SC_WINDOW
    n_chunks = n_src // (SC_WORKERS * w)
    copies = idx.shape[1] // n_chunks
    assert n_chunks % 2 == 0 and idx.shape == (SC_WORKERS, copies * n_chunks, w)

    @functools.partial(
        pl.kernel, mesh=_sc_mesh(),
        out_type=jax.ShapeDtypeStruct((n_out, ROW_TILES, LANES), F32),
        scratch_types=[pltpu.VMEM((copies * n_chunks, w), jnp.int32),
                       pltpu.VMEM((2, w, ROW_TILES, LANES), F32),
                       pltpu.SemaphoreType.DMA((2,)),
                       pltpu.SemaphoreType.DMA((2,))],
        name="moe_dispatch_scatter")
    def k(src_hbm, idx_hbm, out_hbm, idx_v, rows_v, rsem, wsem):
        wid = _sc_worker_id()
        base = wid * (n_chunks * w)
        pltpu.sync_copy(idx_hbm.at[wid], idx_v)

        def read(j, slot):
            return pltpu.make_async_copy(src_hbm.at[pl.ds(base + j * w, w)], rows_v.at[slot],
                                         rsem.at[slot])

        def scatter(j, kk, slot):
            return pltpu.make_async_copy(rows_v.at[slot], out_hbm.at[idx_v.at[kk * n_chunks + j]],
                                         wsem.at[slot])

        read(0, 0).start()

        @pl.loop(0, n_chunks, step=2)
        def _(jj):
            read(jj, 0).wait()

            @pl.when(jj > 0)
            def _():
                for kk in range(copies):
                    scatter(jj - 1, kk, 1).wait()

            read(jj + 1, 1).start()
            for kk in range(copies):
                scatter(jj, kk, 0).start()
            read(jj + 1, 1).wait()
            for kk in range(copies):
                scatter(jj, kk, 0).wait()

            @pl.when(jj + 2 < n_chunks)
            def _():
                read(jj + 2, 0).start()

            for kk in range(copies):
                scatter(jj + 1, kk, 1).start()

        for kk in range(copies):
            scatter(n_chunks - 1, kk, 1).wait()

    return k(src, idx)


def _sc_gather_rows(table, idx):
    _, n_chunks, w = idx.shape
    assert n_chunks % 2 == 0 and idx.shape[0] == SC_WORKERS and w == SC_WINDOW
    n_out = SC_WORKERS * n_chunks * w

    @functools.partial(
        pl.kernel, mesh=_sc_mesh(),
        out_type=jax.ShapeDtypeStruct((n_out, ROW_TILES, LANES), F32),
        scratch_types=[pltpu.VMEM((n_chunks, w), jnp.int32),
                       pltpu.VMEM((2, w, ROW_TILES, LANES), F32),
                       pltpu.SemaphoreType.DMA((2,)),
                       pltpu.SemaphoreType.DMA((2,))],
        name="moe_combine_gather")
    def k(table_hbm, idx_hbm, out_hbm, idx_v, rows_v, gsem, wsem):
        wid = _sc_worker_id()
        base = wid * (n_chunks * w)
        pltpu.sync_copy(idx_hbm.at[wid], idx_v)

        def gather(j, slot):
            return pltpu.make_async_copy(table_hbm.at[idx_v.at[j]], rows_v.at[slot], gsem.at[slot])

        def write(j, slot):
            return pltpu.make_async_copy(rows_v.at[slot], out_hbm.at[pl.ds(base + j * w, w)],
                                         wsem.at[slot])

        gather(0, 0).start()

        @pl.loop(0, n_chunks, step=2)
        def _(jj):
            gather(jj, 0).wait()

            @pl.when(jj > 0)
            def _():
                write(jj - 1, 1).wait()

            gather(jj + 1, 1).start()
            write(jj, 0).start()
            gather(jj + 1, 1).wait()
            write(jj, 0).wait()

            @pl.when(jj + 2 < n_chunks)
            def _():
                gather(jj + 2, 0).start()

            write(jj + 1, 1).start()

        write(n_chunks - 1, 1).wait()

    return k(table, idx)


def _moe_kernel(be_ref, nu_ref, nv_ref, x_ref, wgu_ref, bgu_ref, wd_ref, bd_ref, o_ref,
                wgu_bf, wd_bf, xb_ref):
    b = pl.program_id(0)
    e = be_ref[b]
    prev = be_ref[jnp.maximum(b - 1, 0)]
    active = b < nu_ref[0]
    changed = (b == 0) | (e != prev)

    @pl.when(active & changed)
    def _():
        wgu_bf[...] = wgu_ref[0].astype(BF16)
        wd_bf[...] = wd_ref[0].astype(BF16)

    @pl.when(active)
    def _():
        valid = lax.broadcasted_iota(jnp.int32, (MOE_ROWS, LANES), 0) < nv_ref[b]
        for j in range(ROW_TILES):
            xj = x_ref[_row_tile_slice(j, MOE_ROWS), :]
            xb_ref[:, j * LANES:(j + 1) * LANES] = jnp.where(valid, xj, 0.0).astype(BF16)
        gu = _dot(xb_ref[...], wgu_bf[...]) + bgu_ref[0]
        gate = jnp.minimum(gu[:, :D_EXPERT], SWIGLU_LIMIT)
        up = jnp.clip(gu[:, D_EXPERT:], -SWIGLU_LIMIT, SWIGLU_LIMIT)
        glu = gate * _sigmoid(gate * SWIGLU_ALPHA)
        act = ((up + 1.0) * glu).astype(BF16)
        _store_row_tiles(o_ref, _dot(act, wd_bf[...]) + bd_ref[0])


def _moe(block_expert, n_used, n_valid, xs, w_gate_up, b_gate_up, w_down, b_down):
    p_rows = xs.shape[0] // ROW_TILES
    n_blocks = p_rows // MOE_ROWS

    def blk(b, be, nu, nv):
        return jnp.minimum(b, nu[0] - 1)

    def expert(b, be, nu, nv):
        return (be[blk(b, be, nu, nv)], 0, 0)

    def rows(b, be, nu, nv):
        return (blk(b, be, nu, nv), 0)

    grid_spec = pltpu.PrefetchScalarGridSpec(
        num_scalar_prefetch=3,
        grid=(n_blocks,),
        in_specs=[
            pl.BlockSpec((MOE_ROWS * ROW_TILES, LANES), rows),
            pl.BlockSpec((1, D_MODEL, 2 * D_EXPERT), expert),
            pl.BlockSpec((1, 1, 2 * D_EXPERT), expert),
            pl.BlockSpec((1, D_EXPERT, D_MODEL), expert),
            pl.BlockSpec((1, 1, D_MODEL), expert),
        ],
        out_specs=pl.BlockSpec((MOE_ROWS * ROW_TILES, LANES), rows),
        scratch_shapes=[pltpu.VMEM((D_MODEL, 2 * D_EXPERT), BF16),
                        pltpu.VMEM((D_EXPERT, D_MODEL), BF16),
                        pltpu.VMEM((MOE_ROWS, D_MODEL), BF16)],
    )
    return pl.pallas_call(
        _moe_kernel,
        grid_spec=grid_spec,
        out_shape=jax.ShapeDtypeStruct((p_rows * ROW_TILES, LANES), F32),
        compiler_params=pltpu.CompilerParams(
            dimension_semantics=("arbitrary",), vmem_limit_bytes=VMEM_LIMIT),
        name="moe_grouped_mlp",
    )(block_expert, n_used, n_valid, xs, w_gate_up,
      b_gate_up.reshape(N_EXPERTS, 1, 2 * D_EXPERT), w_down, b_down.reshape(N_EXPERTS, 1, D_MODEL))


def _combine_kernel(x1_ref, y0_ref, y1_ref, y2_ref, y3_ref, rw_ref, mod_ref, g_ref, b_ref, o_ref,
                    *, alpha):
    rw = rw_ref[...]
    y_refs = (y0_ref, y1_ref, y2_ref, y3_ref)
    pieces = []
    for j in range(ROW_TILES):
        tile = _row_tile_slice(j, rw.shape[0])
        acc = rw[:, 0:1] * y_refs[0][tile, :]
        for kk in range(1, TOP_K):
            acc = acc + rw[:, kk:kk + 1] * y_refs[kk][tile, :]
        pieces.append(acc)
    ff = jnp.concatenate(pieces, axis=-1)
    y = alpha * x1_ref[...] + mod_ref[0, 5:6, :] * ff
    o_ref[...] = _layer_norm(y) * g_ref[...] + b_ref[...]


def _combine(x1, yg, rw, mod, l2g, l2b, *, row0, n_rows, mod_map, alpha):
    tm = TM_MIX
    t0 = row0 // tm
    tiles_all = x1.shape[0] // tm
    kern = functools.partial(_combine_kernel, alpha=alpha)

    def y_spec(kk):
        return pl.BlockSpec((tm * ROW_TILES, LANES), lambda i: (kk * tiles_all + t0 + i, 0))

    return pl.pallas_call(
        kern,
        grid=(n_rows // tm,),
        in_specs=[pl.BlockSpec((tm, D_MODEL), lambda i: (t0 + i, 0))]
        + [y_spec(kk) for kk in range(TOP_K)]
        + [pl.BlockSpec((tm, LANES), lambda i: (t0 + i, 0)),
           pl.BlockSpec((1, N_MOD, D_MODEL), mod_map),
           pl.BlockSpec((1, D_MODEL), lambda i: (0, 0)),
           pl.BlockSpec((1, D_MODEL), lambda i: (0, 0))],
        out_specs=pl.BlockSpec((tm, D_MODEL), lambda i: (i, 0)),
        out_shape=jax.ShapeDtypeStruct((n_rows, D_MODEL), F32),
        compiler_params=pltpu.CompilerParams(
            dimension_semantics=("arbitrary",), vmem_limit_bytes=VMEM_LIMIT),
        name="combine_ln2",
    )(x1, yg, yg, yg, yg, rw, mod, l2g, l2b)


def _routing_tables(ridx, n_blocks):
    idx = ridx[:, :TOP_K]
    experts = jnp.arange(N_EXPERTS, dtype=jnp.int32)
    onehot = jnp.any(idx[:, :, None] == experts[None, None, :], axis=1).astype(jnp.int32)
    incl = jnp.cumsum(onehot, axis=0)
    rank = incl - onehot
    counts = incl[-1]
    blocks_per = (counts + MOE_ROWS - 1) // MOE_ROWS
    bends = jnp.cumsum(blocks_per)
    bstarts = bends - blocks_per
    dest = bstarts[idx] * MOE_ROWS + jnp.take_along_axis(rank, idx, axis=1)
    blocks = jnp.arange(n_blocks, dtype=jnp.int32)
    block_expert = jnp.minimum(
        jnp.sum((bends[None, :] <= blocks[:, None]).astype(jnp.int32), axis=1), N_EXPERTS - 1)
    n_used = bends[-1:].astype(jnp.int32)
    n_valid = jnp.clip(counts[block_expert] - (blocks - bstarts[block_expert]) * MOE_ROWS,
                       0, MOE_ROWS).astype(jnp.int32)
    return dest.astype(jnp.int32), block_expert.astype(jnp.int32), n_used, n_valid


def _pos_embed_2d(n_tokens):
    rows = n_tokens // GRID_W
    r = jnp.repeat(jnp.arange(rows), GRID_W).astype(F32)
    col = jnp.tile(jnp.arange(GRID_W), rows).astype(F32)
    quarter = D_MODEL // 4
    omega = 1.0 / (10000.0 ** (jnp.arange(quarter, dtype=F32) / quarter))
    er = r[:, None] * omega
    ec = col[:, None] * omega
    return jnp.concatenate([jnp.sin(er), jnp.cos(er), jnp.sin(ec), jnp.cos(ec)], axis=-1)


def _reorder_w_in(w):
    o_r = 2 * GLA_DK + 2 * GLA_DV
    o_f = o_r + DECAY_RANK
    o_gate = o_f + FNET_DIM
    pad = jnp.zeros((w.shape[0], LANES - DECAY_RANK), w.dtype)
    return jnp.concatenate([w[:, :o_r], w[:, o_gate:], w[:, o_f:o_gate], w[:, o_r:o_f], pad], axis=1)


def kernel(x_prompt, x_sample, state_gla, c, c_ctx, w_ada, b_ada, w_in, w_dec_fwd, b_dec_fwd,
           w_dec_bwd, b_dec_bwd, gla_norm_g, w_br_gla, w_br_fnet, w_out, ln1_g, ln1_b, w_router,
           b_router, w_gate_up, b_gate_up, w_down, b_down, ln2_g, ln2_b):
    n_req, ctx_len, _ = x_prompt.shape
    n_lat, lat_len, _ = x_sample.shape
    depth = w_in.shape[0]
    alpha = (2.0 * depth) ** 0.25
    t_ctx = n_req * ctx_len
    t_lat = n_lat * lat_len
    t_all = t_ctx + t_lat

    x_ctx = x_prompt.reshape(t_ctx, D_MODEL)
    x_lat = x_sample.reshape(t_lat, D_MODEL)
    pos = _pos_embed_2d(lat_len)
    zero_pos = jnp.zeros_like(pos)

    cond_rows = -(-(n_lat + 1) // SUBLANES) * SUBLANES
    cond = jnp.zeros((cond_rows, D_MODEL), F32).at[:n_lat].set(c).at[cond_rows - 1].set(c_ctx)

    n_moe_blocks = (t_all * TOP_K) // MOE_ROWS + N_EXPERTS
    tok_chunks = t_all // (SC_WORKERS * SC_WINDOW)
    states = []
    for l in range(depth):
        mod = _ada(cond, w_ada[l], b_ada[l]).reshape(cond_rows, N_MOD, D_MODEL)
        layer_pos = pos if l == 0 else zero_pos
        proj = _inproj(x_ctx, x_lat, layer_pos, mod, _reorder_w_in(w_in[l]).astype(BF16), lat_len)

        def pad_dec(w):
            return jnp.zeros((LANES, GLA_DK), F32).at[:DECAY_RANK].set(w)

        dec = (pad_dec(w_dec_fwd[l]), b_dec_fwd[l].reshape(1, GLA_DK),
               pad_dec(w_dec_bwd[l]), b_dec_bwd[l].reshape(1, GLA_DK),
               gla_norm_g[l].reshape(1, DV_HEAD))
        o_ctx, s_new = _gla(proj, *dec, None, n_seq=n_req, seq_len=ctx_len, row0=0, emit_state=True)
        (o_lat,) = _gla(proj, *dec, state_gla[:, l], n_seq=n_lat, seq_len=lat_len, row0=t_ctx,
                        emit_state=False)
        states.append(s_new)

        mixed_ctx = _fnet(proj, n_seq=n_req, seq_len=ctx_len, row0=0)
        mixed_lat = _fnet(proj, n_seq=n_lat, seq_len=lat_len, row0=t_ctx)

        wr = jnp.zeros((D_MODEL, LANES), F32).at[:, :N_EXPERTS].set(w_router[l])
        br = jnp.full((1, LANES), -1e30, F32).at[0, :N_EXPERTS].set(b_router[l])
        x1, h2, ridx, rw = _merge(
            x_ctx, x_lat, layer_pos, mod, o_ctx, o_lat, mixed_ctx, mixed_lat, proj,
            w_br_gla[l].astype(BF16), w_br_fnet[l].astype(BF16), w_out[l].astype(BF16),
            ln1_g[l].reshape(1, D_MODEL), ln1_b[l].reshape(1, D_MODEL), wr, br, lat_len, alpha)

        dest, block_expert, n_used, n_valid = _routing_tables(ridx, n_moe_blocks)
        dest_k = dest.T.reshape(TOP_K, SC_WORKERS, tok_chunks, SC_WINDOW)
        scatter_idx = dest_k.transpose(1, 0, 2, 3).reshape(SC_WORKERS, TOP_K * tok_chunks, SC_WINDOW)
        p_rows = n_moe_blocks * MOE_ROWS
        xs = _sc_scatter_rows(h2.reshape(t_all, ROW_TILES, LANES), scatter_idx, p_rows)
        yb = _moe(block_expert, n_used, n_valid, xs.reshape(p_rows * ROW_TILES, LANES),
                  w_gate_up[l], b_gate_up[l], w_down[l], b_down[l])
        gather_idx = dest.T.reshape(SC_WORKERS, TOP_K * tok_chunks, SC_WINDOW)
        yg = _sc_gather_rows(yb.reshape(p_rows, ROW_TILES, LANES), gather_idx)
        yg = yg.reshape(TOP_K * t_all * ROW_TILES, LANES)

        l2g = ln2_g[l].reshape(1, D_MODEL)
        l2b = ln2_b[l].reshape(1, D_MODEL)
        tiles_per_seq = lat_len // TM_MIX
        x_ctx = _combine(x1, yg, rw, mod, l2g, l2b, row0=0, n_rows=t_ctx,
                         mod_map=lambda i: (cond_rows - 1, 0, 0), alpha=alpha)
        x_lat = _combine(x1, yg, rw, mod, l2g, l2b, row0=t_ctx, n_rows=t_lat,
                         mod_map=lambda i: (i // tiles_per_seq, 0, 0), alpha=alpha)

    y_prompt = x_ctx.reshape(x_prompt.shape)
    y_sample = x_lat.reshape(x_sample.shape)
    new_state = jnp.stack(states, axis=1).astype(x_prompt.dtype)
    return (y_prompt, y_sample, new_state)
```

```python
import functools
import math

import numpy as np
import jax
import jax.numpy as jnp
from jax import lax
from jax.experimental import pallas as pl
from jax.experimental.pallas import tpu as pltpu
from jax.experimental.pallas import tpu_sc as plsc

F32 = jnp.float32
BF16 = jnp.bfloat16

D_MODEL = 1024
GRID_W = 64
GLA_HEADS = 4
DK_HEAD = 128
DV_HEAD = 256
GLA_DK = GLA_HEADS * DK_HEAD
GLA_DV = GLA_HEADS * DV_HEAD
DECAY_RANK = 16
GATE_NORMALIZER = 16.0
FNET_GROUPS = 4
FNET_GROUP_DIM = 128
FNET_DIM = FNET_GROUPS * FNET_GROUP_DIM
N_EXPERTS = 32
TOP_K = 4
D_EXPERT = 1024
SWIGLU_LIMIT = 7.0
SWIGLU_ALPHA = 1.702
LN_EPS = 1e-6
N_MOD = 6

LANES = 128
SUBLANES = 8
ROW_TILES = D_MODEL // LANES
COL_Q = 0
COL_K = GLA_DK
COL_V = 2 * GLA_DK
COL_G = COL_V + GLA_DV
COL_GATE_A = COL_G + GLA_DV
COL_GATE_B = COL_GATE_A + D_MODEL
COL_F = COL_GATE_B + D_MODEL
COL_R = COL_F + FNET_DIM
PROJ_COLS = COL_R + LANES

GLA_CHUNK = 128
GLA_LEAF = 16
TM_PROJ = 256
TM_MIX = 256
MOE_ROWS = 256
VMEM_LIMIT = 56 * 1024 * 1024

SC_CORES = 2
SC_SUBCORES = 16
SC_WORKERS = SC_CORES * SC_SUBCORES
SC_WINDOW = 32

HIGHEST = lax.Precision.HIGHEST


def _layer_norm(x):
    mu = jnp.mean(x, axis=-1, keepdims=True)
    xc = x - mu
    var = jnp.mean(xc * xc, axis=-1, keepdims=True)
    return xc * lax.rsqrt(var + LN_EPS)


def _sigmoid(x):
    return 1.0 / (1.0 + jnp.exp(-x))


def _log_sigmoid(z):
    return jnp.minimum(z, 0.0) - jnp.log1p(jnp.exp(-jnp.abs(z)))


def _dot(a, b):
    return jnp.dot(a, b, preferred_element_type=F32)


def _dot_nt(a, b):
    return lax.dot_general(a, b, (((1,), (1,)), ((), ())), preferred_element_type=F32)


def _dot_tn(a, b):
    return lax.dot_general(a, b, (((0,), (0,)), ((), ())), preferred_element_type=F32)


def _row_tile_slice(j, n_rows):
    return pl.ds(j, n_rows, stride=ROW_TILES)


def _store_row_tiles(ref, val):
    for j in range(ROW_TILES):
        ref[_row_tile_slice(j, val.shape[0]), :] = val[:, j * LANES:(j + 1) * LANES]


def _ada_kernel(c_ref, w_ref, b_ref, o_ref):
    c = c_ref[...]
    s = c * _sigmoid(c)
    o_ref[...] = _dot(s.astype(BF16), w_ref[...].astype(BF16)) + b_ref[...]


def _ada(cond, w_ada, b_ada):
    rows = cond.shape[0]
    n = w_ada.shape[1]
    tn = 1536
    return pl.pallas_call(
        _ada_kernel,
        grid=(n // tn,),
        in_specs=[pl.BlockSpec((rows, D_MODEL), lambda j: (0, 0)),
                  pl.BlockSpec((D_MODEL, tn), lambda j: (0, j)),
                  pl.BlockSpec((1, tn), lambda j: (0, j))],
        out_specs=pl.BlockSpec((rows, tn), lambda j: (0, j)),
        out_shape=jax.ShapeDtypeStruct((rows, n), F32),
        compiler_params=pltpu.CompilerParams(vmem_limit_bytes=VMEM_LIMIT),
        name="ada_mod",
    )(cond, w_ada, b_ada.reshape(1, n))


def _group_maps(n_ctx_tiles):
    def ctx_map(i, *_):
        return (jnp.minimum(i, n_ctx_tiles - 1), 0)

    def lat_map(i, *_):
        return (jnp.maximum(i - n_ctx_tiles, 0), 0)

    return ctx_map, lat_map


def _token_specs(tm, n_ctx_tiles, tiles_per_latent_seq, ctx_mod_row):
    ctx_map, lat_map = _group_maps(n_ctx_tiles)

    def pos_map(i, *_):
        return (jnp.maximum(i - n_ctx_tiles, 0) % tiles_per_latent_seq, 0)

    def mod_map(i, *_):
        return (jnp.where(i < n_ctx_tiles, ctx_mod_row,
                          jnp.maximum(i - n_ctx_tiles, 0) // tiles_per_latent_seq), 0, 0)

    return [pl.BlockSpec((tm, D_MODEL), ctx_map),
            pl.BlockSpec((tm, D_MODEL), lat_map),
            pl.BlockSpec((tm, D_MODEL), pos_map),
            pl.BlockSpec((1, N_MOD, D_MODEL), mod_map)]


def _inproj_kernel(xc_ref, xl_ref, pos_ref, mod_ref, w_ref, o_ref, *, n_ctx_tiles):
    i = pl.program_id(0)

    def project(x):
        h = _layer_norm(x) * (1.0 + mod_ref[0, 1:2, :]) + mod_ref[0, 0:1, :]
        o_ref[...] = _dot(h.astype(BF16), w_ref[...])

    @pl.when(i < n_ctx_tiles)
    def _():
        project(xc_ref[...])

    @pl.when(i >= n_ctx_tiles)
    def _():
        project(xl_ref[...] + pos_ref[...])


def _inproj(x_ctx, x_lat, pos, mod, w_in_bf, lat_len):
    t_ctx, t_lat = x_ctx.shape[0], x_lat.shape[0]
    n_ctx_tiles = t_ctx // TM_PROJ
    n_tiles = (t_ctx + t_lat) // TM_PROJ
    kern = functools.partial(_inproj_kernel, n_ctx_tiles=n_ctx_tiles)
    specs = _token_specs(TM_PROJ, n_ctx_tiles, lat_len // TM_PROJ, mod.shape[0] - 1)
    return pl.pallas_call(
        kern,
        grid=(n_tiles,),
        in_specs=specs + [pl.BlockSpec((D_MODEL, PROJ_COLS), lambda i: (0, 0),
                                       pipeline_mode=pl.Buffered(1))],
        out_specs=pl.BlockSpec((TM_PROJ, PROJ_COLS), lambda i: (i, 0)),
        out_shape=jax.ShapeDtypeStruct((t_ctx + t_lat, PROJ_COLS), F32),
        compiler_params=pltpu.CompilerParams(
            dimension_semantics=("arbitrary",), vmem_limit_bytes=VMEM_LIMIT),
        name="ln_inproj",
    )(x_ctx, x_lat, pos, mod, w_in_bf)


def _gla_kernel(*refs, seq_len, has_s0, emit_state):
    it = iter(refs)
    q_ref, k_ref, v_ref, r_ref = next(it), next(it), next(it), next(it)
    wdf_ref, bdf_ref, wdb_ref, bdb_ref, g_ref = next(it), next(it), next(it), next(it), next(it)
    s0_ref = next(it) if has_s0 else None
    o_ref = next(it)
    sout_ref = next(it) if emit_state else None
    cum_ref, a_ref, qi_ref, ko_ref, dec_ref, op_ref, st_ref = (next(it) for _ in range(7))

    C = GLA_CHUNK
    n_chunks = seq_len // C
    q_scale = DK_HEAD ** -0.5

    def rows(n):
        if isinstance(n, int):
            return pl.ds(n * C, C)
        return pl.ds(pl.multiple_of(n * C, C), C)

    def dec_rows(n):
        if isinstance(n, int):
            return pl.ds(n * SUBLANES, SUBLANES)
        return pl.ds(pl.multiple_of(n * SUBLANES, SUBLANES), SUBLANES)

    def loop(body):
        if n_chunks <= 2:
            for n in range(n_chunks):
                body(n)
        else:
            def step(n, carry):
                body(n)
                return carry
            lax.fori_loop(0, n_chunks, step, 0)

    rt = lax.broadcasted_iota(jnp.int32, (C, C), 0)
    ct = lax.broadcasted_iota(jnp.int32, (C, C), 1)
    tri = ((rt >= ct).astype(F32), (ct >= rt).astype(F32))
    row_id = lax.broadcasted_iota(jnp.int32, (C, DK_HEAD), 0)

    r = r_ref[...]
    for d, (w_ref, b_ref) in enumerate(((wdf_ref, bdf_ref), (wdb_ref, bdb_ref))):
        z = jnp.dot(r, w_ref[...], precision=HIGHEST, preferred_element_type=F32) + b_ref[...]
        cum_ref[d] = _log_sigmoid(z) * (1.0 / GATE_NORMALIZER)

    def cumsum_chunk(n):
        for d in range(2):
            la = cum_ref[d, rows(n), :]
            cum_ref[d, rows(n), :] = jnp.dot(tri[d], la, precision=HIGHEST,
                                             preferred_element_type=F32)

    loop(cumsum_chunk)

    def block_rows(cum, first, step, count):
        span = C // count
        parts = [jnp.broadcast_to(cum[first + p * step:first + p * step + 1, :], (span, DK_HEAD))
                 for p in range(count)]
        return parts[0] if count == 1 else jnp.concatenate(parts, axis=0)

    def scores(n, d):
        cum = cum_ref[d, rows(n), :]
        q = q_ref[rows(n), :] * q_scale
        k = k_ref[rows(n), :]
        acc = None
        blk = C // 2
        while blk >= GLA_LEAF:
            pairs = C // (2 * blk)
            bnd = blk - 1 if d == 0 else blk
            w = jnp.exp(-jnp.abs(cum - block_rows(cum, bnd, 2 * blk, pairs)))
            is_q = ((row_id // blk) % 2) == (1 if d == 0 else 0)
            qb = jnp.where(is_q, q * w, 0.0).astype(BF16)
            kb = jnp.where(is_q, 0.0, k * w).astype(BF16)
            s = _dot_nt(qb, kb)
            if pairs > 1:
                s = jnp.where((rt // (2 * blk)) == (ct // (2 * blk)), s, 0.0)
            acc = s if acc is None else acc + s
            blk //= 2
        leaf = GLA_LEAF
        mid = leaf // 2 - 1 if d == 0 else leaf // 2
        e = cum - block_rows(cum, mid, leaf, C // leaf)
        s = _dot_nt((q * jnp.exp(e)).astype(BF16), (k * jnp.exp(-e)).astype(BF16))
        order = (rt >= ct) if d == 0 else (ct >= rt)
        acc = acc + jnp.where(((rt // leaf) == (ct // leaf)) & order, s, 0.0)
        a_ref[d, rows(n), :] = acc.astype(BF16)
        end = cum[C - 1:C, :] if d == 0 else cum[0:1, :]
        qi_ref[d, rows(n), :] = (q * jnp.exp(cum)).astype(BF16)
        ko_ref[d, rows(n), :] = (k * jnp.exp(end - cum)).astype(BF16)
        dec_ref[d, dec_rows(n), :] = jnp.broadcast_to(jnp.exp(end), (SUBLANES, DK_HEAD))

    def scores_chunk(n):
        scores(n, 0)
        scores(n, 1)

    loop(scores_chunk)

    for d in range(2):
        if has_s0:
            st_ref[d] = s0_ref[0, d, 0].T
        else:
            st_ref[d] = jnp.zeros((DV_HEAD, DK_HEAD), F32)

    def scan(n, d):
        v = v_ref[rows(n), :].astype(BF16)
        st = st_ref[d]
        o = _dot(a_ref[d, rows(n), :], v) + _dot_nt(qi_ref[d, rows(n), :], st.astype(BF16))
        st_ref[d] = st * dec_ref[d, dec_rows(n), :][0:1, :] + _dot_tn(v, ko_ref[d, rows(n), :])
        op_ref[d, rows(n), :] = o

    def scan_chunk(m):
        scan(m, 0)
        scan(n_chunks - 1 - m, 1)

    loop(scan_chunk)
    if emit_state:
        sout_ref[0, 0, 0] = st_ref[0].T
        sout_ref[0, 1, 0] = st_ref[1].T

    g = g_ref[...]

    def finish_chunk(n):
        o = op_ref[0, rows(n), :] + op_ref[1, rows(n), :]
        ms = jnp.mean(o * o, axis=-1, keepdims=True)
        o_ref[rows(n), :] = o * lax.rsqrt(ms + LN_EPS) * g

    loop(finish_chunk)


def _gla(proj, wdf, bdf, wdb, bdb, g, s0, *, n_seq, seq_len, row0, emit_state):
    has_s0 = s0 is not None
    blk0 = row0 // seq_len
    kern = functools.partial(_gla_kernel, seq_len=seq_len, has_s0=has_s0, emit_state=emit_state)
    in_specs = [
        pl.BlockSpec((seq_len, DK_HEAD), lambda b, h: (blk0 + b, COL_Q // DK_HEAD + h)),
        pl.BlockSpec((seq_len, DK_HEAD), lambda b, h: (blk0 + b, COL_K // DK_HEAD + h)),
        pl.BlockSpec((seq_len, DV_HEAD), lambda b, h: (blk0 + b, COL_V // DV_HEAD + h)),
        pl.BlockSpec((seq_len, LANES), lambda b, h: (blk0 + b, COL_R // LANES)),
        pl.BlockSpec((LANES, DK_HEAD), lambda b, h: (0, h)),
        pl.BlockSpec((1, DK_HEAD), lambda b, h: (0, h)),
        pl.BlockSpec((LANES, DK_HEAD), lambda b, h: (0, h)),
        pl.BlockSpec((1, DK_HEAD), lambda b, h: (0, h)),
        pl.BlockSpec((1, DV_HEAD), lambda b, h: (0, 0)),
    ]
    args = [proj, proj, proj, proj, wdf, bdf, wdb, bdb, g]
    if has_s0:
        in_specs.append(pl.BlockSpec((1, 2, 1, DK_HEAD, DV_HEAD), lambda b, h: (b, 0, h, 0, 0)))
        args.append(s0)
    out_specs = [pl.BlockSpec((seq_len, DV_HEAD), lambda b, h: (b, h))]
    out_shape = [jax.ShapeDtypeStruct((n_seq * seq_len, GLA_DV), F32)]
    if emit_state:
        out_specs.append(pl.BlockSpec((1, 2, 1, DK_HEAD, DV_HEAD), lambda b, h: (b, 0, h, 0, 0)))
        out_shape.append(jax.ShapeDtypeStruct((n_seq, 2, GLA_HEADS, DK_HEAD, DV_HEAD), F32))

    res = pl.pallas_call(
        kern,
        grid=(n_seq, GLA_HEADS),
        in_specs=in_specs,
        out_specs=out_specs,
        out_shape=out_shape,
        scratch_shapes=[pltpu.VMEM((2, seq_len, DK_HEAD), F32),
                        pltpu.VMEM((2, seq_len, GLA_CHUNK), BF16),
                        pltpu.VMEM((2, seq_len, DK_HEAD), BF16),
                        pltpu.VMEM((2, seq_len, DK_HEAD), BF16),
                        pltpu.VMEM((2, seq_len // GLA_CHUNK * SUBLANES, DK_HEAD), F32),
                        pltpu.VMEM((2, seq_len, DV_HEAD), F32),
                        pltpu.VMEM((2, DV_HEAD, DK_HEAD), F32)],
        compiler_params=pltpu.CompilerParams(
            dimension_semantics=("arbitrary", "arbitrary"), vmem_limit_bytes=VMEM_LIMIT),
        name="gla_seq%d" % seq_len,
    )(*args)
    return res


def _fnet_kernel(f_ref, cl_ref, sl_ref, cg_ref, sg_ref, o_ref, uc_ref, us_ref, *, seq_len):
    cg = cg_ref[...]
    sg = sg_ref[...]
    for grp in range(FNET_GROUPS):
        lo = grp * FNET_GROUP_DIM
        u = f_ref[:, lo:lo + FNET_GROUP_DIM].astype(BF16)
        uc_ref[:, lo:lo + FNET_GROUP_DIM] = _dot(u, cg).astype(BF16)
        us_ref[:, lo:lo + FNET_GROUP_DIM] = _dot(u, sg).astype(BF16)
    mixed = _dot(cl_ref[...], uc_ref[...]) - _dot(sl_ref[...], us_ref[...])
    o_ref[...] = mixed * (1.0 / math.sqrt(seq_len * FNET_GROUP_DIM))


def _dft_mats(n):
    j = np.arange(n, dtype=np.int64)
    ang = (2.0 * np.pi / n) * ((j[:, None] * j[None, :]) % n).astype(np.float64)
    return (jnp.asarray(np.cos(ang), dtype=F32).astype(BF16),
            jnp.asarray(np.sin(ang), dtype=F32).astype(BF16))


def _fnet(proj, *, n_seq, seq_len, row0):
    blk0 = row0 // seq_len
    cl, sl = _dft_mats(seq_len)
    cg, sg = _dft_mats(FNET_GROUP_DIM)
    kern = functools.partial(_fnet_kernel, seq_len=seq_len)
    return pl.pallas_call(
        kern,
        grid=(n_seq,),
        in_specs=[pl.BlockSpec((seq_len, FNET_DIM), lambda b: (blk0 + b, COL_F // FNET_DIM)),
                  pl.BlockSpec((seq_len, seq_len), lambda b: (0, 0)),
                  pl.BlockSpec((seq_len, seq_len), lambda b: (0, 0)),
                  pl.BlockSpec((FNET_GROUP_DIM, FNET_GROUP_DIM), lambda b: (0, 0)),
                  pl.BlockSpec((FNET_GROUP_DIM, FNET_GROUP_DIM), lambda b: (0, 0))],
        out_specs=pl.BlockSpec((seq_len, FNET_DIM), lambda b: (b, 0)),
        out_shape=jax.ShapeDtypeStruct((n_seq * seq_len, FNET_DIM), F32),
        scratch_shapes=[pltpu.VMEM((seq_len, FNET_DIM), BF16),
                        pltpu.VMEM((seq_len, FNET_DIM), BF16)],
        compiler_params=pltpu.CompilerParams(
            dimension_semantics=("arbitrary",), vmem_limit_bytes=VMEM_LIMIT),
        name="fnet_seq%d" % seq_len,
    )(proj, cl, sl, cg, sg)


def _merge_kernel(xc_ref, xl_ref, pos_ref, mod_ref, oc_ref, ol_ref, mc_ref, ml_ref,
                  g_ref, ga_ref, gb_ref, wbg_ref, wbf_ref, wo_ref, l1g_ref, l1b_ref, wr_ref, br_ref,
                  x1_ref, h2_ref, ridx_ref, rw_ref, *, n_ctx_tiles, alpha):
    i = pl.program_id(0)

    def compute(x, o, mx):
        g = g_ref[...]
        a = (o * (g * _sigmoid(g))).astype(BF16)
        gla_out = _dot(a, wbg_ref[...])
        fnet_out = _dot(mx.astype(BF16), wbf_ref[...])
        merged = _sigmoid(ga_ref[...]) * gla_out + _sigmoid(gb_ref[...]) * fnet_out
        mix = _dot(merged.astype(BF16), wo_ref[...])
        y = alpha * x + mod_ref[0, 2:3, :] * mix
        x1 = _layer_norm(y) * l1g_ref[...] + l1b_ref[...]
        x1_ref[...] = x1
        h2 = _layer_norm(x1) * (1.0 + mod_ref[0, 4:5, :]) + mod_ref[0, 3:4, :]
        _store_row_tiles(h2_ref, h2)

        logits = jnp.dot(h2, wr_ref[...], precision=HIGHEST, preferred_element_type=F32) + br_ref[...]
        lane_i = lax.broadcasted_iota(jnp.int32, logits.shape, 1)
        lane = lane_i.astype(F32)
        idx_out = jnp.zeros(logits.shape, F32)
        val_out = jnp.zeros(logits.shape, F32)
        top0 = None
        denom = None
        for kk in range(TOP_K):
            m = jnp.max(logits, axis=-1, keepdims=True)
            sel = jnp.min(jnp.where(logits == m, lane, float(LANES)), axis=-1, keepdims=True)
            if kk == 0:
                top0 = m
                p = jnp.ones_like(m)
                denom = p
            else:
                p = jnp.exp(m - top0)
                denom = denom + p
            idx_out = jnp.where(lane_i == kk, sel, idx_out)
            val_out = jnp.where(lane_i == kk, p, val_out)
            logits = jnp.where(lane == sel, -jnp.inf, logits)
        ridx_ref[...] = idx_out.astype(jnp.int32)
        rw_ref[...] = val_out / denom

    @pl.when(i < n_ctx_tiles)
    def _():
        compute(xc_ref[...], oc_ref[...], mc_ref[...])

    @pl.when(i >= n_ctx_tiles)
    def _():
        compute(xl_ref[...] + pos_ref[...], ol_ref[...], ml_ref[...])


def _merge(x_ctx, x_lat, pos, mod, o_ctx, o_lat, mixed_ctx, mixed_lat, proj,
           wbg, wbf, wo, l1g, l1b, wr, br, lat_len, alpha):
    t_ctx, t_lat = x_ctx.shape[0], x_lat.shape[0]
    t_all = t_ctx + t_lat
    tm = TM_MIX
    n_ctx_tiles = t_ctx // tm
    kern = functools.partial(_merge_kernel, n_ctx_tiles=n_ctx_tiles, alpha=alpha)
    specs = _token_specs(tm, n_ctx_tiles, lat_len // tm, mod.shape[0] - 1)
    ctx_map, lat_map = _group_maps(n_ctx_tiles)

    def const(shape):
        return pl.BlockSpec(shape, lambda i: (0,) * len(shape))

    in_specs = specs + [
        pl.BlockSpec((tm, GLA_DV), ctx_map),
        pl.BlockSpec((tm, GLA_DV), lat_map),
        pl.BlockSpec((tm, FNET_DIM), ctx_map),
        pl.BlockSpec((tm, FNET_DIM), lat_map),
        pl.BlockSpec((tm, GLA_DV), lambda i: (i, COL_G // GLA_DV)),
        pl.BlockSpec((tm, D_MODEL), lambda i: (i, COL_GATE_A // D_MODEL)),
        pl.BlockSpec((tm, D_MODEL), lambda i: (i, COL_GATE_B // D_MODEL)),
        const((GLA_DV, D_MODEL)), const((FNET_DIM, D_MODEL)), const((D_MODEL, D_MODEL)),
        const((1, D_MODEL)), const((1, D_MODEL)),
        const((D_MODEL, LANES)), const((1, LANES)),
    ]
    out_specs = [pl.BlockSpec((tm, D_MODEL), lambda i: (i, 0)),
                 pl.BlockSpec((tm * ROW_TILES, LANES), lambda i: (i, 0)),
                 pl.BlockSpec((tm, LANES), lambda i: (i, 0)),
                 pl.BlockSpec((tm, LANES), lambda i: (i, 0))]
    out_shape = [jax.ShapeDtypeStruct((t_all, D_MODEL), F32),
                 jax.ShapeDtypeStruct((t_all * ROW_TILES, LANES), F32),
                 jax.ShapeDtypeStruct((t_all, LANES), jnp.int32),
                 jax.ShapeDtypeStruct((t_all, LANES), F32)]
    return pl.pallas_call(
        kern,
        grid=(t_all // tm,),
        in_specs=in_specs,
        out_specs=out_specs,
        out_shape=out_shape,
        compiler_params=pltpu.CompilerParams(
            dimension_semantics=("arbitrary",), vmem_limit_bytes=VMEM_LIMIT),
        name="merge_ln1_router",
    )(x_ctx, x_lat, pos, mod, o_ctx, o_lat, mixed_ctx, mixed_lat, proj, proj, proj,
      wbg, wbf, wo, l1g, l1b, wr, br)


def _sc_mesh():
    return plsc.VectorSubcoreMesh(core_axis_name="c", subcore_axis_name="s")


def _sc_worker_id():
    return lax.axis_index("s") * SC_CORES + lax.axis_index("c")


def _sc_scatter_rows(src, idx, n_out):
    n_src = src.shape[0]
    w = SC_WINDOW
    n_chunks = n_src // (SC_WORKERS * w)
    copies = idx.shape[1] // n_chunks
    assert n_chunks % 2 == 0 and idx.shape == (SC_WORKERS, copies * n_chunks, w)

    @functools.partial(
        pl.kernel, mesh=_sc_mesh(),
        out_type=jax.ShapeDtypeStruct((n_out, ROW_TILES, LANES), F32),
        scratch_types=[pltpu.VMEM((copies * n_chunks, w), jnp.int32),
                       pltpu.VMEM((2, w, ROW_TILES, LANES), F32),
                       pltpu.SemaphoreType.DMA((2,)),
                       pltpu.SemaphoreType.DMA((2,))],
        name="moe_dispatch_scatter")
    def k(src_hbm, idx_hbm, out_hbm, idx_v, rows_v, rsem, wsem):
        wid = _sc_worker_id()
        base = wid * (n_chunks * w)
        pltpu.sync_copy(idx_hbm.at[wid], idx_v)

        def read(j, slot):
            return pltpu.make_async_copy(src_hbm.at[pl.ds(base + j * w, w)], rows_v.at[slot],
                                         rsem.at[slot])

        def scatter(j, kk, slot):
            return pltpu.make_async_copy(rows_v.at[slot], out_hbm.at[idx_v.at[kk * n_chunks + j]],
                                         wsem.at[slot])

        read(0, 0).start()

        @pl.loop(0, n_chunks, step=2)
        def _(jj):
            read(jj, 0).wait()

            @pl.when(jj > 0)
            def _():
                for kk in range(copies):
                    scatter(jj - 1, kk, 1).wait()

            read(jj + 1, 1).start()
            for kk in range(copies):
                scatter(jj, kk, 0).start()
            read(jj + 1, 1).wait()
            for kk in range(copies):
                scatter(jj, kk, 0).wait()

            @pl.when(jj + 2 < n_chunks)
            def _():
                read(jj + 2, 0).start()

            for kk in range(copies):
                scatter(jj + 1, kk, 1).start()

        for kk in range(copies):
            scatter(n_chunks - 1, kk, 1).wait()

    return k(src, idx)


def _sc_gather_rows(table, idx):
    _, n_chunks, w = idx.shape
    assert n_chunks % 2 == 0 and idx.shape[0] == SC_WORKERS and w == SC_WINDOW
    n_out = SC_WORKERS * n_chunks * w

    @functools.partial(
        pl.kernel, mesh=_sc_mesh(),
        out_type=jax.ShapeDtypeStruct((n_out, ROW_TILES, LANES), F32),
        scratch_types=[pltpu.VMEM((n_chunks, w), jnp.int32),
                       pltpu.VMEM((2, w, ROW_TILES, LANES), F32),
                       pltpu.SemaphoreType.DMA((2,)),
                       pltpu.SemaphoreType.DMA((2,))],
        name="moe_combine_gather")
    def k(table_hbm, idx_hbm, out_hbm, idx_v, rows_v, gsem, wsem):
        wid = _sc_worker_id()
        base = wid * (n_chunks * w)
        pltpu.sync_copy(idx_hbm.at[wid], idx_v)

        def gather(j, slot):
            return pltpu.make_async_copy(table_hbm.at[idx_v.at[j]], rows_v.at[slot], gsem.at[slot])

        def write(j, slot):
            return pltpu.make_async_copy(rows_v.at[slot], out_hbm.at[pl.ds(base + j * w, w)],
                                         wsem.at[slot])

        gather(0, 0).start()

        @pl.loop(0, n_chunks, step=2)
        def _(jj):
            gather(jj, 0).wait()

            @pl.when(jj > 0)
            def _():
                write(jj - 1, 1).wait()

            gather(jj + 1, 1).start()
            write(jj, 0).start()
            gather(jj + 1, 1).wait()
            write(jj, 0).wait()

            @pl.when(jj + 2 < n_chunks)
            def _():
                gather(jj + 2, 0).start()

            write(jj + 1, 1).start()

        write(n_chunks - 1, 1).wait()

    return k(table, idx)


def _moe_kernel(be_ref, nu_ref, nv_ref, slot_ref, nxt_ref, x_ref, wgu_hbm, bgu_ref, wd_hbm, bd_ref,
                o_ref, wgu_st, wd_st, wgu_bf, wd_bf, xb_ref, sem):
    b = pl.program_id(0)
    e = be_ref[b]
    prev = be_ref[jnp.maximum(b - 1, 0)]
    active = b < nu_ref[0]
    changed = (b == 0) | (e != prev)

    def weight_copies(expert, s):
        return (pltpu.make_async_copy(wgu_hbm.at[expert], wgu_st.at[s], sem.at[0, s]),
                pltpu.make_async_copy(wd_hbm.at[expert], wd_st.at[s], sem.at[1, s]))

    @pl.when(active & changed)
    def _():
        s = slot_ref[b]

        @pl.when(b == 0)
        def _():
            for cp in weight_copies(e, s):
                cp.start()

        for cp in weight_copies(e, s):
            cp.wait()
        wgu_bf[...] = wgu_st[s].astype(BF16)
        wd_bf[...] = wd_st[s].astype(BF16)
        nxt = nxt_ref[b]

        @pl.when(nxt >= 0)
        def _():
            for cp in weight_copies(nxt, 1 - s):
                cp.start()

    @pl.when(active)
    def _():
        valid = lax.broadcasted_iota(jnp.int32, (MOE_ROWS, LANES), 0) < nv_ref[b]
        for j in range(ROW_TILES):
            xj = x_ref[_row_tile_slice(j, MOE_ROWS), :]
            xb_ref[:, j * LANES:(j + 1) * LANES] = jnp.where(valid, xj, 0.0).astype(BF16)
        gu = _dot(xb_ref[...], wgu_bf[...]) + bgu_ref[0]
        gate = jnp.minimum(gu[:, :D_EXPERT], SWIGLU_LIMIT)
        up = jnp.clip(gu[:, D_EXPERT:], -SWIGLU_LIMIT, SWIGLU_LIMIT)
        glu = gate * _sigmoid(gate * SWIGLU_ALPHA)
        act = ((up + 1.0) * glu).astype(BF16)
        _store_row_tiles(o_ref, _dot(act, wd_bf[...]) + bd_ref[0])


def _moe(tables, xs, w_gate_up, b_gate_up, w_down, b_down):
    p_rows = xs.shape[0] // ROW_TILES
    n_blocks = p_rows // MOE_ROWS

    def blk(b, be, nu, *_):
        return jnp.minimum(b, nu[0] - 1)

    def expert(b, be, nu, *_):
        return (be[blk(b, be, nu)], 0, 0)

    def rows(b, be, nu, *_):
        return (blk(b, be, nu), 0)

    grid_spec = pltpu.PrefetchScalarGridSpec(
        num_scalar_prefetch=len(tables),
        grid=(n_blocks,),
        in_specs=[
            pl.BlockSpec((MOE_ROWS * ROW_TILES, LANES), rows),
            pl.BlockSpec(memory_space=pl.ANY),
            pl.BlockSpec((1, 1, 2 * D_EXPERT), expert),
            pl.BlockSpec(memory_space=pl.ANY),
            pl.BlockSpec((1, 1, D_MODEL), expert),
        ],
        out_specs=pl.BlockSpec((MOE_ROWS * ROW_TILES, LANES), rows),
        scratch_shapes=[pltpu.VMEM((2, D_MODEL, 2 * D_EXPERT), F32),
                        pltpu.VMEM((2, D_EXPERT, D_MODEL), F32),
                        pltpu.VMEM((D_MODEL, 2 * D_EXPERT), BF16),
                        pltpu.VMEM((D_EXPERT, D_MODEL), BF16),
                        pltpu.VMEM((MOE_ROWS, D_MODEL), BF16),
                        pltpu.SemaphoreType.DMA((2, 2))],
    )
    return pl.pallas_call(
        _moe_kernel,
        grid_spec=grid_spec,
        out_shape=jax.ShapeDtypeStruct((p_rows * ROW_TILES, LANES), F32),
        compiler_params=pltpu.CompilerParams(
            dimension_semantics=("arbitrary",), vmem_limit_bytes=VMEM_LIMIT),
        name="moe_grouped_mlp",
    )(*tables, xs, w_gate_up, b_gate_up.reshape(N_EXPERTS, 1, 2 * D_EXPERT), w_down,
      b_down.reshape(N_EXPERTS, 1, D_MODEL))


def _combine_kernel(x1_ref, y0_ref, y1_ref, y2_ref, y3_ref, rw_ref, mod_ref, g_ref, b_ref, o_ref,
                    *, alpha):
    rw = rw_ref[...]
    y_refs = (y0_ref, y1_ref, y2_ref, y3_ref)
    pieces = []
    for j in range(ROW_TILES):
        tile = _row_tile_slice(j, rw.shape[0])
        acc = rw[:, 0:1] * y_refs[0][tile, :]
        for kk in range(1, TOP_K):
            acc = acc + rw[:, kk:kk + 1] * y_refs[kk][tile, :]
        pieces.append(acc)
    ff = jnp.concatenate(pieces, axis=-1)
    y = alpha * x1_ref[...] + mod_ref[0, 5:6, :] * ff
    o_ref[...] = _layer_norm(y) * g_ref[...] + b_ref[...]


def _combine(x1, yg, rw, mod, l2g, l2b, *, row0, n_rows, mod_map, alpha):
    tm = TM_MIX
    t0 = row0 // tm
    tiles_all = x1.shape[0] // tm
    kern = functools.partial(_combine_kernel, alpha=alpha)

    def y_spec(kk):
        return pl.BlockSpec((tm * ROW_TILES, LANES), lambda i: (kk * tiles_all + t0 + i, 0))

    return pl.pallas_call(
        kern,
        grid=(n_rows // tm,),
        in_specs=[pl.BlockSpec((tm, D_MODEL), lambda i: (t0 + i, 0))]
        + [y_spec(kk) for kk in range(TOP_K)]
        + [pl.BlockSpec((tm, LANES), lambda i: (t0 + i, 0)),
           pl.BlockSpec((1, N_MOD, D_MODEL), mod_map),
           pl.BlockSpec((1, D_MODEL), lambda i: (0, 0)),
           pl.BlockSpec((1, D_MODEL), lambda i: (0, 0))],
        out_specs=pl.BlockSpec((tm, D_MODEL), lambda i: (i, 0)),
        out_shape=jax.ShapeDtypeStruct((n_rows, D_MODEL), F32),
        compiler_params=pltpu.CompilerParams(
            dimension_semantics=("arbitrary",), vmem_limit_bytes=VMEM_LIMIT),
        name="combine_ln2",
    )(x1, yg, yg, yg, yg, rw, mod, l2g, l2b)


def _routing_tables(ridx, n_blocks):
    idx = ridx[:, :TOP_K]
    experts = jnp.arange(N_EXPERTS, dtype=jnp.int32)
    onehot = jnp.any(idx[:, :, None] == experts[None, None, :], axis=1).astype(jnp.int32)
    incl = jnp.cumsum(onehot, axis=0)
    rank = incl - onehot
    counts = incl[-1]
    blocks_per = (counts + MOE_ROWS - 1) // MOE_ROWS
    bends = jnp.cumsum(blocks_per)
    bstarts = bends - blocks_per
    dest = bstarts[idx] * MOE_ROWS + jnp.take_along_axis(rank, idx, axis=1)
    blocks = jnp.arange(n_blocks, dtype=jnp.int32)
    block_expert = jnp.minimum(
        jnp.sum((bends[None, :] <= blocks[:, None]).astype(jnp.int32), axis=1), N_EXPERTS - 1)
    n_used = bends[-1:].astype(jnp.int32)
    n_valid = jnp.clip(counts[block_expert] - (blocks - bstarts[block_expert]) * MOE_ROWS,
                       0, MOE_ROWS).astype(jnp.int32)
    present = blocks_per > 0
    ordinal = jnp.cumsum(present.astype(jnp.int32)) - 1
    later = lax.cummin(jnp.where(present, experts, N_EXPERTS), reverse=True)
    succ = jnp.concatenate([later[1:], jnp.full((1,), N_EXPERTS, jnp.int32)])
    succ = jnp.where(succ >= N_EXPERTS, -1, succ)
    stage_slot = (ordinal[block_expert] % 2).astype(jnp.int32)
    next_expert = succ[block_expert].astype(jnp.int32)
    tables = (block_expert.astype(jnp.int32), n_used, n_valid, stage_slot, next_expert)
    return dest.astype(jnp.int32), tables


def _pos_embed_2d(n_tokens):
    rows = n_tokens // GRID_W
    r = jnp.repeat(jnp.arange(rows), GRID_W).astype(F32)
    col = jnp.tile(jnp.arange(GRID_W), rows).astype(F32)
    quarter = D_MODEL // 4
    omega = 1.0 / (10000.0 ** (jnp.arange(quarter, dtype=F32) / quarter))
    er = r[:, None] * omega
    ec = col[:, None] * omega
    return jnp.concatenate([jnp.sin(er), jnp.cos(er), jnp.sin(ec), jnp.cos(ec)], axis=-1)


def _reorder_w_in(w):
    o_r = 2 * GLA_DK + 2 * GLA_DV
    o_f = o_r + DECAY_RANK
    o_gate = o_f + FNET_DIM
    pad = jnp.zeros((w.shape[0], LANES - DECAY_RANK), w.dtype)
    return jnp.concatenate([w[:, :o_r], w[:, o_gate:], w[:, o_f:o_gate], w[:, o_r:o_f], pad], axis=1)


def kernel(x_prompt, x_sample, state_gla, c, c_ctx, w_ada, b_ada, w_in, w_dec_fwd, b_dec_fwd,
           w_dec_bwd, b_dec_bwd, gla_norm_g, w_br_gla, w_br_fnet, w_out, ln1_g, ln1_b, w_router,
           b_router, w_gate_up, b_gate_up, w_down, b_down, ln2_g, ln2_b):
    n_req, ctx_len, _ = x_prompt.shape
    n_lat, lat_len, _ = x_sample.shape
    depth = w_in.shape[0]
    alpha = (2.0 * depth) ** 0.25
    t_ctx = n_req * ctx_len
    t_lat = n_lat * lat_len
    t_all = t_ctx + t_lat

    x_ctx = x_prompt.reshape(t_ctx, D_MODEL)
    x_lat = x_sample.reshape(t_lat, D_MODEL)
    pos = _pos_embed_2d(lat_len)
    zero_pos = jnp.zeros_like(pos)

    cond_rows = -(-(n_lat + 1) // SUBLANES) * SUBLANES
    cond = jnp.zeros((cond_rows, D_MODEL), F32).at[:n_lat].set(c).at[cond_rows - 1].set(c_ctx)

    n_moe_blocks = (t_all * TOP_K) // MOE_ROWS + N_EXPERTS
    tok_chunks = t_all // (SC_WORKERS * SC_WINDOW)
    states = []
    for l in range(depth):
        mod = _ada(cond, w_ada[l], b_ada[l]).reshape(cond_rows, N_MOD, D_MODEL)
        layer_pos = pos if l == 0 else zero_pos
        proj = _inproj(x_ctx, x_lat, layer_pos, mod, _reorder_w_in(w_in[l]).astype(BF16), lat_len)

        def pad_dec(w):
            return jnp.zeros((LANES, GLA_DK), F32).at[:DECAY_RANK].set(w)

        dec = (pad_dec(w_dec_fwd[l]), b_dec_fwd[l].reshape(1, GLA_DK),
               pad_dec(w_dec_bwd[l]), b_dec_bwd[l].reshape(1, GLA_DK),
               gla_norm_g[l].reshape(1, DV_HEAD))
        o_ctx, s_new = _gla(proj, *dec, None, n_seq=n_req, seq_len=ctx_len, row0=0, emit_state=True)
        (o_lat,) = _gla(proj, *dec, state_gla[:, l], n_seq=n_lat, seq_len=lat_len, row0=t_ctx,
                        emit_state=False)
        states.append(s_new)

        mixed_ctx = _fnet(proj, n_seq=n_req, seq_len=ctx_len, row0=0)
        mixed_lat = _fnet(proj, n_seq=n_lat, seq_len=lat_len, row0=t_ctx)

        wr = jnp.zeros((D_MODEL, LANES), F32).at[:, :N_EXPERTS].set(w_router[l])
        br = jnp.full((1, LANES), -1e30, F32).at[0, :N_EXPERTS].set(b_router[l])
        x1, h2, ridx, rw = _merge(
            x_ctx, x_lat, layer_pos, mod, o_ctx, o_lat, mixed_ctx, mixed_lat, proj,
            w_br_gla[l].astype(BF16), w_br_fnet[l].astype(BF16), w_out[l].astype(BF16),
            ln1_g[l].reshape(1, D_MODEL), ln1_b[l].reshape(1, D_MODEL), wr, br, lat_len, alpha)

        dest, moe_tables = _routing_tables(ridx, n_moe_blocks)
        dest_k = dest.T.reshape(TOP_K, SC_WORKERS, tok_chunks, SC_WINDOW)
        scatter_idx = dest_k.transpose(1, 0, 2, 3).reshape(SC_WORKERS, TOP_K * tok_chunks, SC_WINDOW)
        p_rows = n_moe_blocks * MOE_ROWS
        xs = _sc_scatter_rows(h2.reshape(t_all, ROW_TILES, LANES), scatter_idx, p_rows)
        yb = _moe(moe_tables, xs.reshape(p_rows * ROW_TILES, LANES),
                  w_gate_up[l], b_gate_up[l], w_down[l], b_down[l])
        gather_idx = dest.T.reshape(SC_WORKERS, TOP_K * tok_chunks, SC_WINDOW)
        yg = _sc_gather_rows(yb.reshape(p_rows, ROW_TILES, LANES), gather_idx)
        yg = yg.reshape(TOP_K * t_all * ROW_TILES, LANES)

        l2g = ln2_g[l].reshape(1, D_MODEL)
        l2b = ln2_b[l].reshape(1, D_MODEL)
        tiles_per_seq = lat_len // TM_MIX
        x_ctx = _combine(x1, yg, rw, mod, l2g, l2b, row0=0, n_rows=t_ctx,
                         mod_map=lambda i: (cond_rows - 1, 0, 0), alpha=alpha)
        x_lat = _combine(x1, yg, rw, mod, l2g, l2b, row0=t_ctx, n_rows=t_lat,
                         mod_map=lambda i: (i // tiles_per_seq, 0, 0), alpha=alpha)

    y_prompt = x_ctx.reshape(x_prompt.shape)
    y_sample = x_lat.reshape(x_sample.shape)
    new_state = jnp.stack(states, axis=1).astype(x_prompt.dtype)
    return (y_prompt, y_sample, new_state)
```

```python
import functools
import math

import numpy as np
import jax
import jax.numpy as jnp
from jax import lax
from jax.experimental import pallas as pl
from jax.experimental.pallas import tpu as pltpu
from jax.experimental.pallas import tpu_sc as plsc

F32 = jnp.float32
BF16 = jnp.bfloat16

D_MODEL = 1024
GRID_W = 64
GLA_HEADS = 4
DK_HEAD = 128
DV_HEAD = 256
GLA_DK = GLA_HEADS * DK_HEAD
GLA_DV = GLA_HEADS * DV_HEAD
DECAY_RANK = 16
GATE_NORMALIZER = 16.0
FNET_GROUPS = 4
FNET_GROUP_DIM = 128
FNET_DIM = FNET_GROUPS * FNET_GROUP_DIM
N_EXPERTS = 32
TOP_K = 4
D_EXPERT = 1024
SWIGLU_LIMIT = 7.0
SWIGLU_ALPHA = 1.702
LN_EPS = 1e-6
N_MOD = 6

LANES = 128
SUBLANES = 8
ROW_TILES = D_MODEL // LANES
COL_Q = 0
COL_K = GLA_DK
COL_V = 2 * GLA_DK
COL_G = COL_V + GLA_DV
COL_GATE_A = COL_G + GLA_DV
COL_GATE_B = COL_GATE_A + D_MODEL
COL_F = COL_GATE_B + D_MODEL
COL_R = COL_F + FNET_DIM
PROJ_COLS = COL_R + LANES

GLA_CHUNK = 128
GLA_LEAF = 16
TM_PROJ = 256
TM_MIX = 256
TM_ROUTE = 1024
MOE_ROWS = 256
VMEM_LIMIT = 56 * 1024 * 1024

SC_CORES = 2
SC_SUBCORES = 16
SC_WORKERS = SC_CORES * SC_SUBCORES
SC_WINDOW = 32

HIGHEST = lax.Precision.HIGHEST


def _layer_norm(x):
    mu = jnp.mean(x, axis=-1, keepdims=True)
    xc = x - mu
    var = jnp.mean(xc * xc, axis=-1, keepdims=True)
    return xc * lax.rsqrt(var + LN_EPS)


def _sigmoid(x):
    return 0.5 * jnp.tanh(0.5 * x) + 0.5


def _log_sigmoid(z):
    return jnp.minimum(z, 0.0) - jnp.log1p(jnp.exp(-jnp.abs(z)))


def _dot(a, b):
    return jnp.dot(a, b, preferred_element_type=F32)


def _dot_nt(a, b):
    return lax.dot_general(a, b, (((1,), (1,)), ((), ())), preferred_element_type=F32)


def _dot_tn(a, b):
    return lax.dot_general(a, b, (((0,), (0,)), ((), ())), preferred_element_type=F32)


def _row_tile_slice(j, n_rows):
    return pl.ds(j, n_rows, stride=ROW_TILES)


def _store_row_tiles(ref, val):
    for j in range(ROW_TILES):
        ref[_row_tile_slice(j, val.shape[0]), :] = val[:, j * LANES:(j + 1) * LANES]


def _ada_kernel(c_ref, w_ref, b_ref, o_ref):
    c = c_ref[...]
    s = c * _sigmoid(c)
    o_ref[...] = _dot(s.astype(BF16), w_ref[...].astype(BF16)) + b_ref[...]


def _ada(cond, w_ada, b_ada):
    rows = cond.shape[0]
    n = w_ada.shape[1]
    tn = 1536
    return pl.pallas_call(
        _ada_kernel,
        grid=(n // tn,),
        in_specs=[pl.BlockSpec((rows, D_MODEL), lambda j: (0, 0)),
                  pl.BlockSpec((D_MODEL, tn), lambda j: (0, j)),
                  pl.BlockSpec((1, tn), lambda j: (0, j))],
        out_specs=pl.BlockSpec((rows, tn), lambda j: (0, j)),
        out_shape=jax.ShapeDtypeStruct((rows, n), F32),
        compiler_params=pltpu.CompilerParams(vmem_limit_bytes=VMEM_LIMIT),
        name="ada_mod",
    )(cond, w_ada, b_ada.reshape(1, n))


def _group_maps(n_ctx_tiles):
    def ctx_map(i, *_):
        return (jnp.minimum(i, n_ctx_tiles - 1), 0)

    def lat_map(i, *_):
        return (jnp.maximum(i - n_ctx_tiles, 0), 0)

    return ctx_map, lat_map


def _token_specs(tm, n_ctx_tiles, tiles_per_latent_seq, ctx_mod_row):
    ctx_map, lat_map = _group_maps(n_ctx_tiles)

    def pos_map(i, *_):
        return (jnp.maximum(i - n_ctx_tiles, 0) % tiles_per_latent_seq, 0)

    def mod_map(i, *_):
        return (jnp.where(i < n_ctx_tiles, ctx_mod_row,
                          jnp.maximum(i - n_ctx_tiles, 0) // tiles_per_latent_seq), 0, 0)

    return [pl.BlockSpec((tm, D_MODEL), ctx_map),
            pl.BlockSpec((tm, D_MODEL), lat_map),
            pl.BlockSpec((tm, D_MODEL), pos_map),
            pl.BlockSpec((1, N_MOD, D_MODEL), mod_map)]


def _inproj_kernel(xc_ref, xl_ref, pos_ref, mod_ref, w_ref, o_ref, *, n_ctx_tiles):
    i = pl.program_id(0)

    def project(x):
        h = _layer_norm(x) * (1.0 + mod_ref[0, 1:2, :]) + mod_ref[0, 0:1, :]
        o_ref[...] = _dot(h.astype(BF16), w_ref[...])

    @pl.when(i < n_ctx_tiles)
    def _():
        project(xc_ref[...])

    @pl.when(i >= n_ctx_tiles)
    def _():
        project(xl_ref[...] + pos_ref[...])


def _inproj(x_ctx, x_lat, pos, mod, w_in_bf, lat_len):
    t_ctx, t_lat = x_ctx.shape[0], x_lat.shape[0]
    n_ctx_tiles = t_ctx // TM_PROJ
    n_tiles = (t_ctx + t_lat) // TM_PROJ
    kern = functools.partial(_inproj_kernel, n_ctx_tiles=n_ctx_tiles)
    specs = _token_specs(TM_PROJ, n_ctx_tiles, lat_len // TM_PROJ, mod.shape[0] - 1)
    return pl.pallas_call(
        kern,
        grid=(n_tiles,),
        in_specs=specs + [pl.BlockSpec((D_MODEL, PROJ_COLS), lambda i: (0, 0),
                                       pipeline_mode=pl.Buffered(1))],
        out_specs=pl.BlockSpec((TM_PROJ, PROJ_COLS), lambda i: (i, 0)),
        out_shape=jax.ShapeDtypeStruct((t_ctx + t_lat, PROJ_COLS), F32),
        compiler_params=pltpu.CompilerParams(
            dimension_semantics=("arbitrary",), vmem_limit_bytes=VMEM_LIMIT),
        name="ln_inproj",
    )(x_ctx, x_lat, pos, mod, w_in_bf)


def _gla_kernel(*refs, seq_len, has_s0, emit_state):
    it = iter(refs)
    q_ref, k_ref, v_ref, r_ref = next(it), next(it), next(it), next(it)
    wdf_ref, bdf_ref, wdb_ref, bdb_ref, g_ref = next(it), next(it), next(it), next(it), next(it)
    s0_ref = next(it) if has_s0 else None
    o_ref = next(it)
    sout_ref = next(it) if emit_state else None
    cum_ref, a_ref, qi_ref, ko_ref, dec_ref, op_ref, st_ref = (next(it) for _ in range(7))

    C = GLA_CHUNK
    n_chunks = seq_len // C
    q_scale = DK_HEAD ** -0.5

    def rows(n):
        if isinstance(n, int):
            return pl.ds(n * C, C)
        return pl.ds(pl.multiple_of(n * C, C), C)

    def dec_rows(n):
        if isinstance(n, int):
            return pl.ds(n * SUBLANES, SUBLANES)
        return pl.ds(pl.multiple_of(n * SUBLANES, SUBLANES), SUBLANES)

    def loop(body):
        if n_chunks <= 2:
            for n in range(n_chunks):
                body(n)
        else:
            def step(n, carry):
                body(n)
                return carry
            lax.fori_loop(0, n_chunks, step, 0)

    rt = lax.broadcasted_iota(jnp.int32, (C, C), 0)
    ct = lax.broadcasted_iota(jnp.int32, (C, C), 1)
    tri = ((rt >= ct).astype(F32), (ct >= rt).astype(F32))
    row_id = lax.broadcasted_iota(jnp.int32, (C, DK_HEAD), 0)

    r = r_ref[...]
    for d, (w_ref, b_ref) in enumerate(((wdf_ref, bdf_ref), (wdb_ref, bdb_ref))):
        z = jnp.dot(r, w_ref[...], precision=HIGHEST, preferred_element_type=F32) + b_ref[...]
        cum_ref[d] = _log_sigmoid(z) * (1.0 / GATE_NORMALIZER)

    def cumsum_chunk(n):
        for d in range(2):
            la = cum_ref[d, rows(n), :]
            cum_ref[d, rows(n), :] = jnp.dot(tri[d], la, precision=HIGHEST,
                                             preferred_element_type=F32)

    loop(cumsum_chunk)

    def block_rows(cum, first, step, count):
        span = C // count
        parts = [jnp.broadcast_to(cum[first + p * step:first + p * step + 1, :], (span, DK_HEAD))
                 for p in range(count)]
        return parts[0] if count == 1 else jnp.concatenate(parts, axis=0)

    def scores(n, d):
        cum = cum_ref[d, rows(n), :]
        q = q_ref[rows(n), :] * q_scale
        k = k_ref[rows(n), :]
        acc = None
        blk = C // 2
        while blk >= GLA_LEAF:
            pairs = C // (2 * blk)
            bnd = blk - 1 if d == 0 else blk
            w = jnp.exp(-jnp.abs(cum - block_rows(cum, bnd, 2 * blk, pairs)))
            is_q = ((row_id // blk) % 2) == (1 if d == 0 else 0)
            qb = jnp.where(is_q, q * w, 0.0).astype(BF16)
            kb = jnp.where(is_q, 0.0, k * w).astype(BF16)
            s = _dot_nt(qb, kb)
            if pairs > 1:
                s = jnp.where((rt // (2 * blk)) == (ct // (2 * blk)), s, 0.0)
            acc = s if acc is None else acc + s
            blk //= 2
        leaf = GLA_LEAF
        mid = leaf // 2 - 1 if d == 0 else leaf // 2
        e = cum - block_rows(cum, mid, leaf, C // leaf)
        s = _dot_nt((q * jnp.exp(e)).astype(BF16), (k * jnp.exp(-e)).astype(BF16))
        order = (rt >= ct) if d == 0 else (ct >= rt)
        acc = acc + jnp.where(((rt // leaf) == (ct // leaf)) & order, s, 0.0)
        a_ref[d, rows(n), :] = acc.astype(BF16)
        end = cum[C - 1:C, :] if d == 0 else cum[0:1, :]
        qi_ref[d, rows(n), :] = (q * jnp.exp(cum)).astype(BF16)
        ko_ref[d, rows(n), :] = (k * jnp.exp(end - cum)).astype(BF16)
        dec_ref[d, dec_rows(n), :] = jnp.broadcast_to(jnp.exp(end), (SUBLANES, DK_HEAD))

    def scores_chunk(n):
        scores(n, 0)
        scores(n, 1)

    loop(scores_chunk)

    for d in range(2):
        if has_s0:
            st_ref[d] = s0_ref[0, d, 0].T
        else:
            st_ref[d] = jnp.zeros((DV_HEAD, DK_HEAD), F32)

    def scan(n, d):
        v = v_ref[rows(n), :].astype(BF16)
        st = st_ref[d]
        o = _dot(a_ref[d, rows(n), :], v) + _dot_nt(qi_ref[d, rows(n), :], st.astype(BF16))
        st_ref[d] = st * dec_ref[d, dec_rows(n), :][0:1, :] + _dot_tn(v, ko_ref[d, rows(n), :])
        op_ref[d, rows(n), :] = o

    def scan_chunk(m):
        scan(m, 0)
        scan(n_chunks - 1 - m, 1)

    loop(scan_chunk)
    if emit_state:
        sout_ref[0, 0, 0] = st_ref[0].T
        sout_ref[0, 1, 0] = st_ref[1].T

    g = g_ref[...]

    def finish_chunk(n):
        o = op_ref[0, rows(n), :] + op_ref[1, rows(n), :]
        ms = jnp.mean(o * o, axis=-1, keepdims=True)
        o_ref[rows(n), :] = o * lax.rsqrt(ms + LN_EPS) * g

    loop(finish_chunk)


def _gla(proj, wdf, bdf, wdb, bdb, g, s0, *, n_seq, seq_len, row0, emit_state):
    has_s0 = s0 is not None
    blk0 = row0 // seq_len
    kern = functools.partial(_gla_kernel, seq_len=seq_len, has_s0=has_s0, emit_state=emit_state)
    in_specs = [
        pl.BlockSpec((seq_len, DK_HEAD), lambda b, h: (blk0 + b, COL_Q // DK_HEAD + h)),
        pl.BlockSpec((seq_len, DK_HEAD), lambda b, h: (blk0 + b, COL_K // DK_HEAD + h)),
        pl.BlockSpec((seq_len, DV_HEAD), lambda b, h: (blk0 + b, COL_V // DV_HEAD + h)),
        pl.BlockSpec((seq_len, LANES), lambda b, h: (blk0 + b, COL_R // LANES)),
        pl.BlockSpec((LANES, DK_HEAD), lambda b, h: (0, h)),
        pl.BlockSpec((1, DK_HEAD), lambda b, h: (0, h)),
        pl.BlockSpec((LANES, DK_HEAD), lambda b, h: (0, h)),
        pl.BlockSpec((1, DK_HEAD), lambda b, h: (0, h)),
        pl.BlockSpec((1, DV_HEAD), lambda b, h: (0, 0)),
    ]
    args = [proj, proj, proj, proj, wdf, bdf, wdb, bdb, g]
    if has_s0:
        in_specs.append(pl.BlockSpec((1, 2, 1, DK_HEAD, DV_HEAD), lambda b, h: (b, 0, h, 0, 0)))
        args.append(s0)
    out_specs = [pl.BlockSpec((seq_len, DV_HEAD), lambda b, h: (b, h))]
    out_shape = [jax.ShapeDtypeStruct((n_seq * seq_len, GLA_DV), F32)]
    if emit_state:
        out_specs.append(pl.BlockSpec((1, 2, 1, DK_HEAD, DV_HEAD), lambda b, h: (b, 0, h, 0, 0)))
        out_shape.append(jax.ShapeDtypeStruct((n_seq, 2, GLA_HEADS, DK_HEAD, DV_HEAD), F32))

    res = pl.pallas_call(
        kern,
        grid=(n_seq, GLA_HEADS),
        in_specs=in_specs,
        out_specs=out_specs,
        out_shape=out_shape,
        scratch_shapes=[pltpu.VMEM((2, seq_len, DK_HEAD), F32),
                        pltpu.VMEM((2, seq_len, GLA_CHUNK), BF16),
                        pltpu.VMEM((2, seq_len, DK_HEAD), BF16),
                        pltpu.VMEM((2, seq_len, DK_HEAD), BF16),
                        pltpu.VMEM((2, seq_len // GLA_CHUNK * SUBLANES, DK_HEAD), F32),
                        pltpu.VMEM((2, seq_len, DV_HEAD), F32),
                        pltpu.VMEM((2, DV_HEAD, DK_HEAD), F32)],
        compiler_params=pltpu.CompilerParams(
            dimension_semantics=("arbitrary", "arbitrary"), vmem_limit_bytes=VMEM_LIMIT),
        name="gla_seq%d" % seq_len,
    )(*args)
    return res


def _fnet_kernel(f_ref, cl_ref, sl_ref, cg_ref, sg_ref, o_ref, uc_ref, us_ref, *, seq_len):
    cg = cg_ref[...]
    sg = sg_ref[...]
    for grp in range(FNET_GROUPS):
        lo = grp * FNET_GROUP_DIM
        u = f_ref[:, lo:lo + FNET_GROUP_DIM].astype(BF16)
        uc_ref[:, lo:lo + FNET_GROUP_DIM] = _dot(u, cg).astype(BF16)
        us_ref[:, lo:lo + FNET_GROUP_DIM] = _dot(u, sg).astype(BF16)
    mixed = _dot(cl_ref[...], uc_ref[...]) - _dot(sl_ref[...], us_ref[...])
    o_ref[...] = mixed * (1.0 / math.sqrt(seq_len * FNET_GROUP_DIM))


def _dft_mats(n):
    j = np.arange(n, dtype=np.int64)
    ang = (2.0 * np.pi / n) * ((j[:, None] * j[None, :]) % n).astype(np.float64)
    return (jnp.asarray(np.cos(ang), dtype=F32).astype(BF16),
            jnp.asarray(np.sin(ang), dtype=F32).astype(BF16))


def _fnet(proj, *, n_seq, seq_len, row0):
    blk0 = row0 // seq_len
    cl, sl = _dft_mats(seq_len)
    cg, sg = _dft_mats(FNET_GROUP_DIM)
    kern = functools.partial(_fnet_kernel, seq_len=seq_len)
    return pl.pallas_call(
        kern,
        grid=(n_seq,),
        in_specs=[pl.BlockSpec((seq_len, FNET_DIM), lambda b: (blk0 + b, COL_F // FNET_DIM)),
                  pl.BlockSpec((seq_len, seq_len), lambda b: (0, 0)),
                  pl.BlockSpec((seq_len, seq_len), lambda b: (0, 0)),
                  pl.BlockSpec((FNET_GROUP_DIM, FNET_GROUP_DIM), lambda b: (0, 0)),
                  pl.BlockSpec((FNET_GROUP_DIM, FNET_GROUP_DIM), lambda b: (0, 0))],
        out_specs=pl.BlockSpec((seq_len, FNET_DIM), lambda b: (b, 0)),
        out_shape=jax.ShapeDtypeStruct((n_seq * seq_len, FNET_DIM), F32),
        scratch_shapes=[pltpu.VMEM((seq_len, FNET_DIM), BF16),
                        pltpu.VMEM((seq_len, FNET_DIM), BF16)],
        compiler_params=pltpu.CompilerParams(
            dimension_semantics=("arbitrary",), vmem_limit_bytes=VMEM_LIMIT),
        name="fnet_seq%d" % seq_len,
    )(proj, cl, sl, cg, sg)


def _merge_kernel(xc_ref, xl_ref, pos_ref, mod_ref, oc_ref, ol_ref, mc_ref, ml_ref,
                  g_ref, ga_ref, gb_ref, wbg_ref, wbf_ref, wo_ref, l1g_ref, l1b_ref, wr_ref, br_ref,
                  x1_ref, h2_ref, ridx_ref, rw_ref, *, n_ctx_tiles, alpha):
    i = pl.program_id(0)

    def compute(x, o, mx):
        g = g_ref[...]
        a = (o * (g * _sigmoid(g))).astype(BF16)
        gla_out = _dot(a, wbg_ref[...])
        fnet_out = _dot(mx.astype(BF16), wbf_ref[...])
        merged = _sigmoid(ga_ref[...]) * gla_out + _sigmoid(gb_ref[...]) * fnet_out
        mix = _dot(merged.astype(BF16), wo_ref[...])
        y = alpha * x + mod_ref[0, 2:3, :] * mix
        x1 = _layer_norm(y) * l1g_ref[...] + l1b_ref[...]
        x1_ref[...] = x1
        h2 = _layer_norm(x1) * (1.0 + mod_ref[0, 4:5, :]) + mod_ref[0, 3:4, :]
        _store_row_tiles(h2_ref, h2)

        logits = jnp.dot(h2, wr_ref[...], precision=HIGHEST, preferred_element_type=F32) + br_ref[...]
        lane_i = lax.broadcasted_iota(jnp.int32, logits.shape, 1)
        lane = lane_i.astype(F32)
        idx_out = jnp.zeros(logits.shape, F32)
        val_out = jnp.zeros(logits.shape, F32)
        top0 = None
        denom = None
        for kk in range(TOP_K):
            m = jnp.max(logits, axis=-1, keepdims=True)
            sel = jnp.min(jnp.where(logits == m, lane, float(LANES)), axis=-1, keepdims=True)
            if kk == 0:
                top0 = m
                p = jnp.ones_like(m)
                denom = p
            else:
                p = jnp.exp(m - top0)
                denom = denom + p
            idx_out = jnp.where(lane_i == kk, sel, idx_out)
            val_out = jnp.where(lane_i == kk, p, val_out)
            logits = jnp.where(lane == sel, -jnp.inf, logits)
        ridx_ref[...] = idx_out.astype(jnp.int32)
        rw_ref[...] = val_out / denom

    @pl.when(i < n_ctx_tiles)
    def _():
        compute(xc_ref[...], oc_ref[...], mc_ref[...])

    @pl.when(i >= n_ctx_tiles)
    def _():
        compute(xl_ref[...] + pos_ref[...], ol_ref[...], ml_ref[...])


def _merge(x_ctx, x_lat, pos, mod, o_ctx, o_lat, mixed_ctx, mixed_lat, proj,
           wbg, wbf, wo, l1g, l1b, wr, br, lat_len, alpha):
    t_ctx, t_lat = x_ctx.shape[0], x_lat.shape[0]
    t_all = t_ctx + t_lat
    tm = TM_MIX
    n_ctx_tiles = t_ctx // tm
    kern = functools.partial(_merge_kernel, n_ctx_tiles=n_ctx_tiles, alpha=alpha)
    specs = _token_specs(tm, n_ctx_tiles, lat_len // tm, mod.shape[0] - 1)
    ctx_map, lat_map = _group_maps(n_ctx_tiles)

    def const(shape):
        return pl.BlockSpec(shape, lambda i: (0,) * len(shape))

    in_specs = specs + [
        pl.BlockSpec((tm, GLA_DV), ctx_map),
        pl.BlockSpec((tm, GLA_DV), lat_map),
        pl.BlockSpec((tm, FNET_DIM), ctx_map),
        pl.BlockSpec((tm, FNET_DIM), lat_map),
        pl.BlockSpec((tm, GLA_DV), lambda i: (i, COL_G // GLA_DV)),
        pl.BlockSpec((tm, D_MODEL), lambda i: (i, COL_GATE_A // D_MODEL)),
        pl.BlockSpec((tm, D_MODEL), lambda i: (i, COL_GATE_B // D_MODEL)),
        const((GLA_DV, D_MODEL)), const((FNET_DIM, D_MODEL)), const((D_MODEL, D_MODEL)),
        const((1, D_MODEL)), const((1, D_MODEL)),
        const((D_MODEL, LANES)), const((1, LANES)),
    ]
    out_specs = [pl.BlockSpec((tm, D_MODEL), lambda i: (i, 0)),
                 pl.BlockSpec((tm * ROW_TILES, LANES), lambda i: (i, 0)),
                 pl.BlockSpec((tm, LANES), lambda i: (i, 0)),
                 pl.BlockSpec((tm, LANES), lambda i: (i, 0))]
    out_shape = [jax.ShapeDtypeStruct((t_all, D_MODEL), F32),
                 jax.ShapeDtypeStruct((t_all * ROW_TILES, LANES), F32),
                 jax.ShapeDtypeStruct((t_all, LANES), jnp.int32),
                 jax.ShapeDtypeStruct((t_all, LANES), F32)]
    return pl.pallas_call(
        kern,
        grid=(t_all // tm,),
        in_specs=in_specs,
        out_specs=out_specs,
        out_shape=out_shape,
        compiler_params=pltpu.CompilerParams(
            dimension_semantics=("arbitrary",), vmem_limit_bytes=VMEM_LIMIT),
        name="merge_ln1_router",
    )(x_ctx, x_lat, pos, mod, o_ctx, o_lat, mixed_ctx, mixed_lat, proj, proj, proj,
      wbg, wbf, wo, l1g, l1b, wr, br)


def _sc_mesh():
    return plsc.VectorSubcoreMesh(core_axis_name="c", subcore_axis_name="s")


def _sc_worker_id():
    return lax.axis_index("s") * SC_CORES + lax.axis_index("c")


def _sc_scatter_rows(src, idx, n_out):
    n_src = src.shape[0]
    w = SC_WINDOW
    n_chunks = n_src // (SC_WORKERS * w)
    copies = idx.shape[0]
    assert n_chunks % 2 == 0 and idx.shape == (copies, SC_WORKERS, n_chunks, w)

    @functools.partial(
        pl.kernel, mesh=_sc_mesh(),
        out_type=jax.ShapeDtypeStruct((n_out, ROW_TILES, LANES), F32),
        scratch_types=[pltpu.VMEM((copies * n_chunks, w), jnp.int32),
                       pltpu.VMEM((2, w, ROW_TILES, LANES), F32),
                       pltpu.SemaphoreType.DMA((2,)),
                       pltpu.SemaphoreType.DMA((2,))],
        name="moe_dispatch_scatter")
    def k(src_hbm, idx_hbm, out_hbm, idx_v, rows_v, rsem, wsem):
        wid = _sc_worker_id()
        base = wid * (n_chunks * w)
        for kk in range(copies):
            pltpu.sync_copy(idx_hbm.at[kk, wid], idx_v.at[pl.ds(kk * n_chunks, n_chunks)])

        def read(j, slot):
            return pltpu.make_async_copy(src_hbm.at[pl.ds(base + j * w, w)], rows_v.at[slot],
                                         rsem.at[slot])

        def scatter(j, kk, slot):
            return pltpu.make_async_copy(rows_v.at[slot], out_hbm.at[idx_v.at[kk * n_chunks + j]],
                                         wsem.at[slot])

        read(0, 0).start()

        @pl.loop(0, n_chunks, step=2)
        def _(jj):
            read(jj, 0).wait()

            @pl.when(jj > 0)
            def _():
                for kk in range(copies):
                    scatter(jj - 1, kk, 1).wait()

            read(jj + 1, 1).start()
            for kk in range(copies):
                scatter(jj, kk, 0).start()
            read(jj + 1, 1).wait()
            for kk in range(copies):
                scatter(jj, kk, 0).wait()

            @pl.when(jj + 2 < n_chunks)
            def _():
                read(jj + 2, 0).start()

            for kk in range(copies):
                scatter(jj + 1, kk, 1).start()

        for kk in range(copies):
            scatter(n_chunks - 1, kk, 1).wait()

    return k(src, idx)


def _sc_gather_rows(table, idx):
    _, n_chunks, w = idx.shape
    assert n_chunks % 2 == 0 and idx.shape[0] == SC_WORKERS and w == SC_WINDOW
    n_out = SC_WORKERS * n_chunks * w

    @functools.partial(
        pl.kernel, mesh=_sc_mesh(),
        out_type=jax.ShapeDtypeStruct((n_out, ROW_TILES, LANES), F32),
        scratch_types=[pltpu.VMEM((n_chunks, w), jnp.int32),
                       pltpu.VMEM((2, w, ROW_TILES, LANES), F32),
                       pltpu.SemaphoreType.DMA((2,)),
                       pltpu.SemaphoreType.DMA((2,))],
        name="moe_combine_gather")
    def k(table_hbm, idx_hbm, out_hbm, idx_v, rows_v, gsem, wsem):
        wid = _sc_worker_id()
        base = wid * (n_chunks * w)
        pltpu.sync_copy(idx_hbm.at[wid], idx_v)

        def gather(j, slot):
            return pltpu.make_async_copy(table_hbm.at[idx_v.at[j]], rows_v.at[slot], gsem.at[slot])

        def write(j, slot):
            return pltpu.make_async_copy(rows_v.at[slot], out_hbm.at[pl.ds(base + j * w, w)],
                                         wsem.at[slot])

        gather(0, 0).start()

        @pl.loop(0, n_chunks, step=2)
        def _(jj):
            gather(jj, 0).wait()

            @pl.when(jj > 0)
            def _():
                write(jj - 1, 1).wait()

            gather(jj + 1, 1).start()
            write(jj, 0).start()
            gather(jj + 1, 1).wait()
            write(jj, 0).wait()

            @pl.when(jj + 2 < n_chunks)
            def _():
                gather(jj + 2, 0).start()

            write(jj + 1, 1).start()

        write(n_chunks - 1, 1).wait()

    return k(table, idx)


def _moe_kernel(be_ref, nu_ref, nv_ref, slot_ref, nxt_ref, x_ref, wgu_hbm, bgu_ref, wd_hbm, bd_ref,
                o_ref, wgu_st, wd_st, wgu_bf, wd_bf, xb_ref, sem):
    b = pl.program_id(0)
    e = be_ref[b]
    prev = be_ref[jnp.maximum(b - 1, 0)]
    active = b < nu_ref[0]
    changed = (b == 0) | (e != prev)

    def weight_copies(expert, s):
        return (pltpu.make_async_copy(wgu_hbm.at[expert], wgu_st.at[s], sem.at[0, s]),
                pltpu.make_async_copy(wd_hbm.at[expert], wd_st.at[s], sem.at[1, s]))

    @pl.when(active & changed)
    def _():
        s = slot_ref[b]

        @pl.when(b == 0)
        def _():
            for cp in weight_copies(e, s):
                cp.start()

        for cp in weight_copies(e, s):
            cp.wait()
        wgu_bf[...] = wgu_st[s].astype(BF16)
        wd_bf[...] = wd_st[s].astype(BF16)
        nxt = nxt_ref[b]

        @pl.when(nxt >= 0)
        def _():
            for cp in weight_copies(nxt, 1 - s):
                cp.start()

    @pl.when(active)
    def _():
        valid = lax.broadcasted_iota(jnp.int32, (MOE_ROWS, LANES), 0) < nv_ref[b]
        for j in range(ROW_TILES):
            xj = x_ref[_row_tile_slice(j, MOE_ROWS), :]
            xb_ref[:, j * LANES:(j + 1) * LANES] = jnp.where(valid, xj, 0.0).astype(BF16)
        gu = _dot(xb_ref[...], wgu_bf[...]) + bgu_ref[0]
        gate = jnp.minimum(gu[:, :D_EXPERT], SWIGLU_LIMIT)
        up = jnp.clip(gu[:, D_EXPERT:], -SWIGLU_LIMIT, SWIGLU_LIMIT)
        glu = gate * _sigmoid(gate * SWIGLU_ALPHA)
        act = ((up + 1.0) * glu).astype(BF16)
        _store_row_tiles(o_ref, _dot(act, wd_bf[...]) + bd_ref[0])


def _moe(tables, xs, w_gate_up, b_gate_up, w_down, b_down):
    p_rows = xs.shape[0] // ROW_TILES
    n_blocks = p_rows // MOE_ROWS

    def blk(b, be, nu, *_):
        return jnp.minimum(b, nu[0] - 1)

    def expert(b, be, nu, *_):
        return (be[blk(b, be, nu)], 0, 0)

    def rows(b, be, nu, *_):
        return (blk(b, be, nu), 0)

    grid_spec = pltpu.PrefetchScalarGridSpec(
        num_scalar_prefetch=len(tables),
        grid=(n_blocks,),
        in_specs=[
            pl.BlockSpec((MOE_ROWS * ROW_TILES, LANES), rows),
            pl.BlockSpec(memory_space=pl.ANY),
            pl.BlockSpec((1, 1, 2 * D_EXPERT), expert),
            pl.BlockSpec(memory_space=pl.ANY),
            pl.BlockSpec((1, 1, D_MODEL), expert),
        ],
        out_specs=pl.BlockSpec((MOE_ROWS * ROW_TILES, LANES), rows),
        scratch_shapes=[pltpu.VMEM((2, D_MODEL, 2 * D_EXPERT), F32),
                        pltpu.VMEM((2, D_EXPERT, D_MODEL), F32),
                        pltpu.VMEM((D_MODEL, 2 * D_EXPERT), BF16),
                        pltpu.VMEM((D_EXPERT, D_MODEL), BF16),
                        pltpu.VMEM((MOE_ROWS, D_MODEL), BF16),
                        pltpu.SemaphoreType.DMA((2, 2))],
    )
    return pl.pallas_call(
        _moe_kernel,
        grid_spec=grid_spec,
        out_shape=jax.ShapeDtypeStruct((p_rows * ROW_TILES, LANES), F32),
        compiler_params=pltpu.CompilerParams(
            dimension_semantics=("arbitrary",), vmem_limit_bytes=VMEM_LIMIT),
        name="moe_grouped_mlp",
    )(*tables, xs, w_gate_up, b_gate_up.reshape(N_EXPERTS, 1, 2 * D_EXPERT), w_down,
      b_down.reshape(N_EXPERTS, 1, D_MODEL))


def _combine_kernel(x1_ref, y0_ref, y1_ref, y2_ref, y3_ref, rw_ref, mod_ref, g_ref, b_ref, o_ref,
                    *, alpha):
    rw = rw_ref[...]
    y_refs = (y0_ref, y1_ref, y2_ref, y3_ref)
    pieces = []
    for j in range(ROW_TILES):
        tile = _row_tile_slice(j, rw.shape[0])
        acc = rw[:, 0:1] * y_refs[0][tile, :]
        for kk in range(1, TOP_K):
            acc = acc + rw[:, kk:kk + 1] * y_refs[kk][tile, :]
        pieces.append(acc)
    ff = jnp.concatenate(pieces, axis=-1)
    y = alpha * x1_ref[...] + mod_ref[0, 5:6, :] * ff
    o_ref[...] = _layer_norm(y) * g_ref[...] + b_ref[...]


def _combine(x1, yg, rw, mod, l2g, l2b, *, row0, n_rows, mod_map, alpha):
    tm = TM_MIX
    t0 = row0 // tm
    tiles_all = x1.shape[0] // tm
    kern = functools.partial(_combine_kernel, alpha=alpha)

    def y_spec(kk):
        return pl.BlockSpec((tm * ROW_TILES, LANES), lambda i: (kk * tiles_all + t0 + i, 0))

    return pl.pallas_call(
        kern,
        grid=(n_rows // tm,),
        in_specs=[pl.BlockSpec((tm, D_MODEL), lambda i: (t0 + i, 0))]
        + [y_spec(kk) for kk in range(TOP_K)]
        + [pl.BlockSpec((tm, LANES), lambda i: (t0 + i, 0)),
           pl.BlockSpec((1, N_MOD, D_MODEL), mod_map),
           pl.BlockSpec((1, D_MODEL), lambda i: (0, 0)),
           pl.BlockSpec((1, D_MODEL), lambda i: (0, 0))],
        out_specs=pl.BlockSpec((tm, D_MODEL), lambda i: (i, 0)),
        out_shape=jax.ShapeDtypeStruct((n_rows, D_MODEL), F32),
        compiler_params=pltpu.CompilerParams(
            dimension_semantics=("arbitrary",), vmem_limit_bytes=VMEM_LIMIT),
        name="combine_ln2",
    )(x1, yg, yg, yg, yg, rw, mod, l2g, l2b)


def _route_kernel(ridx_ref, dest_ref, cnt_ref, run_ref, bst_ref):
    phase = pl.program_id(0)
    i = pl.program_id(1)
    tm = ridx_ref.shape[0]
    ridx = ridx_ref[...]
    lane = lax.broadcasted_iota(jnp.int32, (tm, LANES), 1)
    hits = [ridx[:, kk:kk + 1] == lane for kk in range(TOP_K)]
    chosen = jnp.where(hits[0], 1.0, 0.0)
    for kk in range(1, TOP_K):
        chosen = chosen + jnp.where(hits[kk], 1.0, 0.0)
    colsum = jnp.sum(chosen, axis=0, keepdims=True)

    @pl.when((phase == 0) & (i == 0))
    def _():
        run_ref[...] = jnp.zeros_like(run_ref)

    @pl.when(phase == 0)
    def _():
        run_ref[...] = run_ref[...] + colsum

    @pl.when((phase == 1) & (i == 0))
    def _():
        counts = run_ref[...]
        cnt_ref[...] = counts
        blocks = jnp.floor((counts + (MOE_ROWS - 1.0)) * (1.0 / MOE_ROWS))
        r = lax.broadcasted_iota(jnp.int32, (LANES, LANES), 0)
        c = lax.broadcasted_iota(jnp.int32, (LANES, LANES), 1)
        before = jnp.dot(blocks, (r < c).astype(F32), precision=HIGHEST, preferred_element_type=F32)
        bst_ref[...] = before * float(MOE_ROWS)
        run_ref[...] = jnp.zeros_like(run_ref)

    @pl.when(phase == 1)
    def _():
        rt = lax.broadcasted_iota(jnp.int32, (tm, tm), 0)
        ct = lax.broadcasted_iota(jnp.int32, (tm, tm), 1)
        earlier = _dot((ct < rt).astype(BF16), chosen.astype(BF16))
        row_of = bst_ref[0:1, :] + run_ref[0:1, :] + earlier
        out = jnp.zeros((tm, LANES), F32)
        for kk in range(TOP_K):
            dk = jnp.sum(jnp.where(hits[kk], row_of, 0.0), axis=-1, keepdims=True)
            out = jnp.where(lane == kk, dk, out)
        dest_ref[...] = out.T[0:SUBLANES, :].astype(jnp.int32)
        run_ref[...] = run_ref[...] + colsum


def _route(ridx):
    t_all = ridx.shape[0]
    tm = TM_ROUTE
    return pl.pallas_call(
        _route_kernel,
        grid=(2, t_all // tm),
        in_specs=[pl.BlockSpec((tm, LANES), lambda p, i: (i, 0))],
        out_specs=[pl.BlockSpec((SUBLANES, tm), lambda p, i: (0, i * p)),
                   pl.BlockSpec((SUBLANES, LANES), lambda p, i: (0, 0))],
        out_shape=[jax.ShapeDtypeStruct((SUBLANES, t_all), jnp.int32),
                   jax.ShapeDtypeStruct((SUBLANES, LANES), F32)],
        scratch_shapes=[pltpu.VMEM((SUBLANES, LANES), F32),
                        pltpu.VMEM((SUBLANES, LANES), F32)],
        compiler_params=pltpu.CompilerParams(
            dimension_semantics=("arbitrary", "arbitrary"), vmem_limit_bytes=VMEM_LIMIT),
        name="moe_route",
    )(ridx)


def _routing_tables(counts, n_blocks):
    experts = jnp.arange(N_EXPERTS, dtype=jnp.int32)
    blocks_per = (counts + MOE_ROWS - 1) // MOE_ROWS
    bends = jnp.cumsum(blocks_per)
    bstarts = bends - blocks_per
    blocks = jnp.arange(n_blocks, dtype=jnp.int32)
    block_expert = jnp.minimum(
        jnp.sum((bends[None, :] <= blocks[:, None]).astype(jnp.int32), axis=1), N_EXPERTS - 1)
    n_used = bends[-1:].astype(jnp.int32)
    n_valid = jnp.clip(counts[block_expert] - (blocks - bstarts[block_expert]) * MOE_ROWS,
                       0, MOE_ROWS).astype(jnp.int32)
    present = blocks_per > 0
    ordinal = jnp.cumsum(present.astype(jnp.int32)) - 1
    later = lax.cummin(jnp.where(present, experts, N_EXPERTS), reverse=True)
    succ = jnp.concatenate([later[1:], jnp.full((1,), N_EXPERTS, jnp.int32)])
    succ = jnp.where(succ >= N_EXPERTS, -1, succ)
    stage_slot = (ordinal[block_expert] % 2).astype(jnp.int32)
    next_expert = succ[block_expert].astype(jnp.int32)
    return (block_expert.astype(jnp.int32), n_used, n_valid, stage_slot, next_expert)


def _pos_embed_2d(n_tokens):
    rows = n_tokens // GRID_W
    r = jnp.repeat(jnp.arange(rows), GRID_W).astype(F32)
    col = jnp.tile(jnp.arange(GRID_W), rows).astype(F32)
    quarter = D_MODEL // 4
    omega = 1.0 / (10000.0 ** (jnp.arange(quarter, dtype=F32) / quarter))
    er = r[:, None] * omega
    ec = col[:, None] * omega
    return jnp.concatenate([jnp.sin(er), jnp.cos(er), jnp.sin(ec), jnp.cos(ec)], axis=-1)


def _reorder_w_in(w):
    o_r = 2 * GLA_DK + 2 * GLA_DV
    o_f = o_r + DECAY_RANK
    o_gate = o_f + FNET_DIM
    pad = jnp.zeros((w.shape[0], LANES - DECAY_RANK), w.dtype)
    return jnp.concatenate([w[:, :o_r], w[:, o_gate:], w[:, o_f:o_gate], w[:, o_r:o_f], pad], axis=1)


def kernel(x_prompt, x_sample, state_gla, c, c_ctx, w_ada, b_ada, w_in, w_dec_fwd, b_dec_fwd,
           w_dec_bwd, b_dec_bwd, gla_norm_g, w_br_gla, w_br_fnet, w_out, ln1_g, ln1_b, w_router,
           b_router, w_gate_up, b_gate_up, w_down, b_down, ln2_g, ln2_b):
    n_req, ctx_len, _ = x_prompt.shape
    n_lat, lat_len, _ = x_sample.shape
    depth = w_in.shape[0]
    alpha = (2.0 * depth) ** 0.25
    t_ctx = n_req * ctx_len
    t_lat = n_lat * lat_len
    t_all = t_ctx + t_lat

    x_ctx = x_prompt.reshape(t_ctx, D_MODEL)
    x_lat = x_sample.reshape(t_lat, D_MODEL)
    pos = _pos_embed_2d(lat_len)
    zero_pos = jnp.zeros_like(pos)

    cond_rows = -(-(n_lat + 1) // SUBLANES) * SUBLANES
    cond = jnp.zeros((cond_rows, D_MODEL), F32).at[:n_lat].set(c).at[cond_rows - 1].set(c_ctx)

    n_moe_blocks = (t_all * TOP_K) // MOE_ROWS + N_EXPERTS
    tok_chunks = t_all // (SC_WORKERS * SC_WINDOW)
    states = []
    for l in range(depth):
        mod = _ada(cond, w_ada[l], b_ada[l]).reshape(cond_rows, N_MOD, D_MODEL)
        layer_pos = pos if l == 0 else zero_pos
        proj = _inproj(x_ctx, x_lat, layer_pos, mod, _reorder_w_in(w_in[l]).astype(BF16), lat_len)

        def pad_dec(w):
            return jnp.zeros((LANES, GLA_DK), F32).at[:DECAY_RANK].set(w)

        dec = (pad_dec(w_dec_fwd[l]), b_dec_fwd[l].reshape(1, GLA_DK),
               pad_dec(w_dec_bwd[l]), b_dec_bwd[l].reshape(1, GLA_DK),
               gla_norm_g[l].reshape(1, DV_HEAD))
        o_ctx, s_new = _gla(proj, *dec, None, n_seq=n_req, seq_len=ctx_len, row0=0, emit_state=True)
        (o_lat,) = _gla(proj, *dec, state_gla[:, l], n_seq=n_lat, seq_len=lat_len, row0=t_ctx,
                        emit_state=False)
        states.append(s_new)

        mixed_ctx = _fnet(proj, n_seq=n_req, seq_len=ctx_len, row0=0)
        mixed_lat = _fnet(proj, n_seq=n_lat, seq_len=lat_len, row0=t_ctx)

        wr = jnp.zeros((D_MODEL, LANES), F32).at[:, :N_EXPERTS].set(w_router[l])
        br = jnp.full((1, LANES), -1e30, F32).at[0, :N_EXPERTS].set(b_router[l])
        x1, h2, ridx, rw = _merge(
            x_ctx, x_lat, layer_pos, mod, o_ctx, o_lat, mixed_ctx, mixed_lat, proj,
            w_br_gla[l].astype(BF16), w_br_fnet[l].astype(BF16), w_out[l].astype(BF16),
            ln1_g[l].reshape(1, D_MODEL), ln1_b[l].reshape(1, D_MODEL), wr, br, lat_len, alpha)

        dest, counts = _route(ridx)
        moe_tables = _routing_tables(counts[0, :N_EXPERTS].astype(jnp.int32), n_moe_blocks)
        dest = dest[:TOP_K]
        scatter_idx = dest.reshape(TOP_K, SC_WORKERS, tok_chunks, SC_WINDOW)
        p_rows = n_moe_blocks * MOE_ROWS
        xs = _sc_scatter_rows(h2.reshape(t_all, ROW_TILES, LANES), scatter_idx, p_rows)
        yb = _moe(moe_tables, xs.reshape(p_rows * ROW_TILES, LANES),
                  w_gate_up[l], b_gate_up[l], w_down[l], b_down[l])
        gather_idx = dest.reshape(SC_WORKERS, TOP_K * tok_chunks, SC_WINDOW)
        yg = _sc_gather_rows(yb.reshape(p_rows, ROW_TILES, LANES), gather_idx)
        yg = yg.reshape(TOP_K * t_all * ROW_TILES, LANES)

        l2g = ln2_g[l].reshape(1, D_MODEL)
        l2b = ln2_b[l].reshape(1, D_MODEL)
        tiles_per_seq = lat_len // TM_MIX
        x_ctx = _combine(x1, yg, rw, mod, l2g, l2b, row0=0, n_rows=t_ctx,
                         mod_map=lambda i: (cond_rows - 1, 0, 0), alpha=alpha)
        x_lat = _combine(x1, yg, rw, mod, l2g, l2b, row0=t_ctx, n_rows=t_lat,
                         mod_map=lambda i: (i // tiles_per_seq, 0, 0), alpha=alpha)

    y_prompt = x_ctx.reshape(x_prompt.shape)
    y_sample = x_lat.reshape(x_sample.shape)
    new_state = jnp.stack(states, axis=1).astype(x_prompt.dtype)
    return (y_prompt, y_sample, new_state)
```

```python
import functools
import math

import numpy as np
import jax
import jax.numpy as jnp
from jax import lax
from jax.experimental import pallas as pl
from jax.experimental.pallas import tpu as pltpu
from jax.experimental.pallas import tpu_sc as plsc

F32 = jnp.float32
BF16 = jnp.bfloat16

D_MODEL = 1024
GRID_W = 64
GLA_HEADS = 4
DK_HEAD = 128
DV_HEAD = 256
GLA_DK = GLA_HEADS * DK_HEAD
GLA_DV = GLA_HEADS * DV_HEAD
DECAY_RANK = 16
GATE_NORMALIZER = 16.0
FNET_GROUPS = 4
FNET_GROUP_DIM = 128
FNET_DIM = FNET_GROUPS * FNET_GROUP_DIM
N_EXPERTS = 32
TOP_K = 4
D_EXPERT = 1024
SWIGLU_LIMIT = 7.0
SWIGLU_ALPHA = 1.702
LN_EPS = 1e-6
N_MOD = 6

LANES = 128
SUBLANES = 8
ROW_TILES = D_MODEL // LANES
COL_Q = 0
COL_K = GLA_DK
COL_V = 2 * GLA_DK
COL_G = COL_V + GLA_DV
COL_GATE_A = COL_G + GLA_DV
COL_GATE_B = COL_GATE_A + D_MODEL
COL_F = COL_GATE_B + D_MODEL
COL_R = COL_F + FNET_DIM
PROJ_COLS = COL_R + LANES

GLA_CHUNK = 128
GLA_LEAF = 16
TM_PROJ = 256
TM_MIX = 256
TM_ROUTE = 1024
MOE_ROWS = 256
VMEM_LIMIT = 56 * 1024 * 1024

SC_CORES = 2
SC_SUBCORES = 16
SC_WORKERS = SC_CORES * SC_SUBCORES
SC_WINDOW = 32

HIGHEST = lax.Precision.HIGHEST


def _layer_norm(x):
    mu = jnp.mean(x, axis=-1, keepdims=True)
    xc = x - mu
    var = jnp.mean(xc * xc, axis=-1, keepdims=True)
    return xc * lax.rsqrt(var + LN_EPS)


def _sigmoid(x):
    return 0.5 * jnp.tanh(0.5 * x) + 0.5


def _log_sigmoid(z):
    return jnp.minimum(z, 0.0) - jnp.log1p(jnp.exp(-jnp.abs(z)))


def _dot(a, b):
    return jnp.dot(a, b, preferred_element_type=F32)


def _split_bf16(x, terms):
    parts = []
    for _ in range(terms):
        p = x.astype(BF16)
        parts.append(p)
        x = x - p.astype(F32)
    return parts


def _dot_nt(a, b):
    return lax.dot_general(a, b, (((1,), (1,)), ((), ())), preferred_element_type=F32)


def _dot_tn(a, b):
    return lax.dot_general(a, b, (((0,), (0,)), ((), ())), preferred_element_type=F32)


def _row_tile_slice(j, n_rows):
    return pl.ds(j, n_rows, stride=ROW_TILES)


def _store_row_tiles(ref, val):
    for j in range(ROW_TILES):
        ref[_row_tile_slice(j, val.shape[0]), :] = val[:, j * LANES:(j + 1) * LANES]


def _ada_kernel(c_ref, w_ref, b_ref, o_ref):
    c = c_ref[...]
    s = c * _sigmoid(c)
    o_ref[...] = _dot(s.astype(BF16), w_ref[...].astype(BF16)) + b_ref[...]


def _ada(cond, w_ada, b_ada):
    rows = cond.shape[0]
    n = w_ada.shape[1]
    tn = 1536
    return pl.pallas_call(
        _ada_kernel,
        grid=(n // tn,),
        in_specs=[pl.BlockSpec((rows, D_MODEL), lambda j: (0, 0)),
                  pl.BlockSpec((D_MODEL, tn), lambda j: (0, j)),
                  pl.BlockSpec((1, tn), lambda j: (0, j))],
        out_specs=pl.BlockSpec((rows, tn), lambda j: (0, j)),
        out_shape=jax.ShapeDtypeStruct((rows, n), F32),
        compiler_params=pltpu.CompilerParams(vmem_limit_bytes=VMEM_LIMIT),
        name="ada_mod",
    )(cond, w_ada, b_ada.reshape(1, n))


def _group_maps(n_ctx_tiles):
    def ctx_map(i, *_):
        return (jnp.minimum(i, n_ctx_tiles - 1), 0)

    def lat_map(i, *_):
        return (jnp.maximum(i - n_ctx_tiles, 0), 0)

    return ctx_map, lat_map


def _token_specs(tm, n_ctx_tiles, tiles_per_latent_seq, ctx_mod_row):
    ctx_map, lat_map = _group_maps(n_ctx_tiles)

    def pos_map(i, *_):
        return (jnp.maximum(i - n_ctx_tiles, 0) % tiles_per_latent_seq, 0)

    def mod_map(i, *_):
        return (jnp.where(i < n_ctx_tiles, ctx_mod_row,
                          jnp.maximum(i - n_ctx_tiles, 0) // tiles_per_latent_seq), 0, 0)

    return [pl.BlockSpec((tm, D_MODEL), ctx_map),
            pl.BlockSpec((tm, D_MODEL), lat_map),
            pl.BlockSpec((tm, D_MODEL), pos_map),
            pl.BlockSpec((1, N_MOD, D_MODEL), mod_map)]


def _inproj_kernel(xc_ref, xl_ref, pos_ref, mod_ref, w_ref, o_ref, *, n_ctx_tiles):
    i = pl.program_id(0)

    def project(x):
        h = _layer_norm(x) * (1.0 + mod_ref[0, 1:2, :]) + mod_ref[0, 0:1, :]
        o_ref[...] = _dot(h.astype(BF16), w_ref[...])

    @pl.when(i < n_ctx_tiles)
    def _():
        project(xc_ref[...])

    @pl.when(i >= n_ctx_tiles)
    def _():
        project(xl_ref[...] + pos_ref[...])


def _inproj(x_ctx, x_lat, pos, mod, w_in_bf, lat_len):
    t_ctx, t_lat = x_ctx.shape[0], x_lat.shape[0]
    n_ctx_tiles = t_ctx // TM_PROJ
    n_tiles = (t_ctx + t_lat) // TM_PROJ
    kern = functools.partial(_inproj_kernel, n_ctx_tiles=n_ctx_tiles)
    specs = _token_specs(TM_PROJ, n_ctx_tiles, lat_len // TM_PROJ, mod.shape[0] - 1)
    return pl.pallas_call(
        kern,
        grid=(n_tiles,),
        in_specs=specs + [pl.BlockSpec((D_MODEL, PROJ_COLS), lambda i: (0, 0),
                                       pipeline_mode=pl.Buffered(1))],
        out_specs=pl.BlockSpec((TM_PROJ, PROJ_COLS), lambda i: (i, 0)),
        out_shape=jax.ShapeDtypeStruct((t_ctx + t_lat, PROJ_COLS), F32),
        compiler_params=pltpu.CompilerParams(
            dimension_semantics=("arbitrary",), vmem_limit_bytes=VMEM_LIMIT),
        name="ln_inproj",
    )(x_ctx, x_lat, pos, mod, w_in_bf)


def _gla_kernel(*refs, seq_len, has_s0, emit_state):
    it = iter(refs)
    q_ref, k_ref, v_ref, r_ref = next(it), next(it), next(it), next(it)
    wdf_ref, bdf_ref, wdb_ref, bdb_ref, g_ref = next(it), next(it), next(it), next(it), next(it)
    s0_ref = next(it) if has_s0 else None
    o_ref = next(it)
    sout_ref = next(it) if emit_state else None
    cum_ref, a_ref, qi_ref, ko_ref, dec_ref, op_ref, st_ref = (next(it) for _ in range(7))

    C = GLA_CHUNK
    n_chunks = seq_len // C

    def rows(n):
        if isinstance(n, int):
            return pl.ds(n * C, C)
        return pl.ds(pl.multiple_of(n * C, C), C)

    def dec_rows(n):
        if isinstance(n, int):
            return pl.ds(n * SUBLANES, SUBLANES)
        return pl.ds(pl.multiple_of(n * SUBLANES, SUBLANES), SUBLANES)

    def loop(body):
        if n_chunks <= 2:
            for n in range(n_chunks):
                body(n)
        else:
            def step(m, carry):
                body(2 * m)
                body(2 * m + 1)
                return carry
            lax.fori_loop(0, n_chunks // 2, step, 0)

    rt = lax.broadcasted_iota(jnp.int32, (C, C), 0)
    ct = lax.broadcasted_iota(jnp.int32, (C, C), 1)
    tri = ((rt >= ct).astype(BF16), (ct >= rt).astype(BF16))
    row_id = lax.broadcasted_iota(jnp.int32, (C, DK_HEAD), 0)

    r_hi, r_lo = _split_bf16(r_ref[...], 2)
    for d, (w_ref, b_ref) in enumerate(((wdf_ref, bdf_ref), (wdb_ref, bdb_ref))):
        w_hi, w_lo = _split_bf16(w_ref[...], 2)
        z = _dot(r_hi, w_hi) + _dot(r_lo, w_hi) + _dot(r_hi, w_lo) + b_ref[...]
        cum_ref[d] = _log_sigmoid(z) * (1.0 / GATE_NORMALIZER)

    def cumsum_chunk(n):
        for d in range(2):
            la_hi, la_lo = _split_bf16(cum_ref[d, rows(n), :], 2)
            cum_ref[d, rows(n), :] = _dot(tri[d], la_hi) + _dot(tri[d], la_lo)

    loop(cumsum_chunk)

    query_rows = ({}, {})
    keep = ({}, {})
    for d in range(2):
        blk = C // 2
        while blk >= GLA_LEAF:
            q_parity = 1 if d == 0 else 0
            query_rows[d][blk] = ((row_id // blk) % 2) == q_parity
            qb, kb = rt // blk, ct // blk
            keep[d][blk] = ((qb % 2) == q_parity) & ((qb == kb + 1) if d == 0 else (kb == qb + 1))
            blk //= 2
        order = (rt >= ct) if d == 0 else (ct >= rt)
        keep[d][0] = ((rt // GLA_LEAF) == (ct // GLA_LEAF)) & order

    def block_rows(cum, first, step, count):
        span = C // count
        parts = [jnp.broadcast_to(cum[first + p * step:first + p * step + 1, :], (span, DK_HEAD))
                 for p in range(count)]
        return parts[0] if count == 1 else jnp.concatenate(parts, axis=0)

    def scores(n, d):
        cum = cum_ref[d, rows(n), :]
        q = q_ref[rows(n), :]
        k = k_ref[rows(n), :]
        acc = None
        blk = C // 2
        while blk >= GLA_LEAF:
            pairs = C // (2 * blk)
            bnd = blk - 1 if d == 0 else blk
            w = jnp.exp(-jnp.abs(cum - block_rows(cum, bnd, 2 * blk, pairs)))
            x = (jnp.where(query_rows[d][blk], q, k) * w).astype(BF16)
            s = jnp.where(keep[d][blk], _dot_nt(x, x), 0.0)
            acc = s if acc is None else acc + s
            blk //= 2
        mid = GLA_LEAF // 2 - 1 if d == 0 else GLA_LEAF // 2
        e = cum - block_rows(cum, mid, GLA_LEAF, C // GLA_LEAF)
        s = _dot_nt((q * jnp.exp(e)).astype(BF16), (k * jnp.exp(-e)).astype(BF16))
        acc = acc + jnp.where(keep[d][0], s, 0.0)
        a_ref[d, rows(n), :] = acc.astype(BF16)
        end = cum[C - 1:C, :] if d == 0 else cum[0:1, :]
        qi_ref[d, rows(n), :] = (q * jnp.exp(cum)).astype(BF16)
        ko_ref[d, rows(n), :] = (k * jnp.exp(end - cum)).astype(BF16)
        dec_ref[d, dec_rows(n), :] = jnp.broadcast_to(jnp.exp(end), (SUBLANES, DK_HEAD))

    def scores_chunk(n):
        scores(n, 0)
        scores(n, 1)

    loop(scores_chunk)

    for d in range(2):
        if has_s0:
            st_ref[d] = s0_ref[0, d, 0].T
        else:
            st_ref[d] = jnp.zeros((DV_HEAD, DK_HEAD), F32)

    def scan(n, d):
        v = v_ref[rows(n), :].astype(BF16)
        st = st_ref[d]
        o = _dot(a_ref[d, rows(n), :], v) + _dot_nt(qi_ref[d, rows(n), :], st.astype(BF16))
        st_ref[d] = st * dec_ref[d, dec_rows(n), :][0:1, :] + _dot_tn(v, ko_ref[d, rows(n), :])
        op_ref[d, rows(n), :] = o

    def scan_chunk(m):
        scan(m, 0)
        scan(n_chunks - 1 - m, 1)

    loop(scan_chunk)
    if emit_state:
        sout_ref[0, 0, 0] = st_ref[0].T
        sout_ref[0, 1, 0] = st_ref[1].T

    g = g_ref[...]

    def finish_chunk(n):
        o = op_ref[0, rows(n), :] + op_ref[1, rows(n), :]
        ms = jnp.mean(o * o, axis=-1, keepdims=True)
        o_ref[rows(n), :] = o * lax.rsqrt(ms + LN_EPS) * g

    loop(finish_chunk)


def _gla(proj, wdf, bdf, wdb, bdb, g, s0, *, n_seq, seq_len, row0, emit_state):
    has_s0 = s0 is not None
    blk0 = row0 // seq_len
    kern = functools.partial(_gla_kernel, seq_len=seq_len, has_s0=has_s0, emit_state=emit_state)
    in_specs = [
        pl.BlockSpec((seq_len, DK_HEAD), lambda b, h: (blk0 + b, COL_Q // DK_HEAD + h)),
        pl.BlockSpec((seq_len, DK_HEAD), lambda b, h: (blk0 + b, COL_K // DK_HEAD + h)),
        pl.BlockSpec((seq_len, DV_HEAD), lambda b, h: (blk0 + b, COL_V // DV_HEAD + h)),
        pl.BlockSpec((seq_len, LANES), lambda b, h: (blk0 + b, COL_R // LANES)),
        pl.BlockSpec((LANES, DK_HEAD), lambda b, h: (0, h)),
        pl.BlockSpec((1, DK_HEAD), lambda b, h: (0, h)),
        pl.BlockSpec((LANES, DK_HEAD), lambda b, h: (0, h)),
        pl.BlockSpec((1, DK_HEAD), lambda b, h: (0, h)),
        pl.BlockSpec((1, DV_HEAD), lambda b, h: (0, 0)),
    ]
    args = [proj, proj, proj, proj, wdf, bdf, wdb, bdb, g]
    if has_s0:
        in_specs.append(pl.BlockSpec((1, 2, 1, DK_HEAD, DV_HEAD), lambda b, h: (b, 0, h, 0, 0)))
        args.append(s0)
    out_specs = [pl.BlockSpec((seq_len, DV_HEAD), lambda b, h: (b, h))]
    out_shape = [jax.ShapeDtypeStruct((n_seq * seq_len, GLA_DV), F32)]
    if emit_state:
        out_specs.append(pl.BlockSpec((1, 2, 1, DK_HEAD, DV_HEAD), lambda b, h: (b, 0, h, 0, 0)))
        out_shape.append(jax.ShapeDtypeStruct((n_seq, 2, GLA_HEADS, DK_HEAD, DV_HEAD), F32))

    res = pl.pallas_call(
        kern,
        grid=(n_seq, GLA_HEADS),
        in_specs=in_specs,
        out_specs=out_specs,
        out_shape=out_shape,
        scratch_shapes=[pltpu.VMEM((2, seq_len, DK_HEAD), F32),
                        pltpu.VMEM((2, seq_len, GLA_CHUNK), BF16),
                        pltpu.VMEM((2, seq_len, DK_HEAD), BF16),
                        pltpu.VMEM((2, seq_len, DK_HEAD), BF16),
                        pltpu.VMEM((2, seq_len // GLA_CHUNK * SUBLANES, DK_HEAD), F32),
                        pltpu.VMEM((2, seq_len, DV_HEAD), F32),
                        pltpu.VMEM((2, DV_HEAD, DK_HEAD), F32)],
        compiler_params=pltpu.CompilerParams(
            dimension_semantics=("arbitrary", "arbitrary"), vmem_limit_bytes=VMEM_LIMIT),
        name="gla_seq%d" % seq_len,
    )(*args)
    return res


def _fnet_kernel(f_ref, cl_ref, sl_ref, cg_ref, sg_ref, o_ref, uc_ref, us_ref, *, seq_len):
    cg = cg_ref[...]
    sg = sg_ref[...]
    for grp in range(FNET_GROUPS):
        lo = grp * FNET_GROUP_DIM
        u = f_ref[:, lo:lo + FNET_GROUP_DIM].astype(BF16)
        uc_ref[:, lo:lo + FNET_GROUP_DIM] = _dot(u, cg).astype(BF16)
        us_ref[:, lo:lo + FNET_GROUP_DIM] = _dot(u, sg).astype(BF16)
    mixed = _dot(cl_ref[...], uc_ref[...]) - _dot(sl_ref[...], us_ref[...])
    o_ref[...] = mixed * (1.0 / math.sqrt(seq_len * FNET_GROUP_DIM))


def _dft_mats(n):
    j = np.arange(n, dtype=np.int64)
    ang = (2.0 * np.pi / n) * ((j[:, None] * j[None, :]) % n).astype(np.float64)
    return (jnp.asarray(np.cos(ang), dtype=F32).astype(BF16),
            jnp.asarray(np.sin(ang), dtype=F32).astype(BF16))


def _fnet(proj, *, n_seq, seq_len, row0):
    blk0 = row0 // seq_len
    cl, sl = _dft_mats(seq_len)
    cg, sg = _dft_mats(FNET_GROUP_DIM)
    kern = functools.partial(_fnet_kernel, seq_len=seq_len)
    return pl.pallas_call(
        kern,
        grid=(n_seq,),
        in_specs=[pl.BlockSpec((seq_len, FNET_DIM), lambda b: (blk0 + b, COL_F // FNET_DIM)),
                  pl.BlockSpec((seq_len, seq_len), lambda b: (0, 0)),
                  pl.BlockSpec((seq_len, seq_len), lambda b: (0, 0)),
                  pl.BlockSpec((FNET_GROUP_DIM, FNET_GROUP_DIM), lambda b: (0, 0)),
                  pl.BlockSpec((FNET_GROUP_DIM, FNET_GROUP_DIM), lambda b: (0, 0))],
        out_specs=pl.BlockSpec((seq_len, FNET_DIM), lambda b: (b, 0)),
        out_shape=jax.ShapeDtypeStruct((n_seq * seq_len, FNET_DIM), F32),
        scratch_shapes=[pltpu.VMEM((seq_len, FNET_DIM), BF16),
                        pltpu.VMEM((seq_len, FNET_DIM), BF16)],
        compiler_params=pltpu.CompilerParams(
            dimension_semantics=("arbitrary",), vmem_limit_bytes=VMEM_LIMIT),
        name="fnet_seq%d" % seq_len,
    )(proj, cl, sl, cg, sg)


def _merge_kernel(xc_ref, xl_ref, pos_ref, mod_ref, oc_ref, ol_ref, mc_ref, ml_ref,
                  g_ref, ga_ref, gb_ref, wbg_ref, wbf_ref, wo_ref, l1g_ref, l1b_ref, wr_ref, br_ref,
                  x1_ref, h2_ref, ridx_ref, rw_ref, *, n_ctx_tiles, alpha):
    i = pl.program_id(0)

    def compute(x, o, mx):
        g = g_ref[...]
        a = (o * (g * _sigmoid(g))).astype(BF16)
        gla_out = _dot(a, wbg_ref[...])
        fnet_out = _dot(mx.astype(BF16), wbf_ref[...])
        merged = _sigmoid(ga_ref[...]) * gla_out + _sigmoid(gb_ref[...]) * fnet_out
        mix = _dot(merged.astype(BF16), wo_ref[...])
        y = alpha * x + mod_ref[0, 2:3, :] * mix
        x1 = _layer_norm(y) * l1g_ref[...] + l1b_ref[...]
        x1_ref[...] = x1
        h2 = _layer_norm(x1) * (1.0 + mod_ref[0, 4:5, :]) + mod_ref[0, 3:4, :]
        _store_row_tiles(h2_ref, h2)

        logits = jnp.dot(h2, wr_ref[...], precision=HIGHEST, preferred_element_type=F32) + br_ref[...]
        lane_i = lax.broadcasted_iota(jnp.int32, logits.shape, 1)
        lane = lane_i.astype(F32)
        idx_out = jnp.zeros(logits.shape, F32)
        val_out = jnp.zeros(logits.shape, F32)
        top0 = None
        denom = None
        for kk in range(TOP_K):
            m = jnp.max(logits, axis=-1, keepdims=True)
            sel = jnp.min(jnp.where(logits == m, lane, float(LANES)), axis=-1, keepdims=True)
            if kk == 0:
                top0 = m
                p = jnp.ones_like(m)
                denom = p
            else:
                p = jnp.exp(m - top0)
                denom = denom + p
            idx_out = jnp.where(lane_i == kk, sel, idx_out)
            val_out = jnp.where(lane_i == kk, p, val_out)
            logits = jnp.where(lane == sel, -jnp.inf, logits)
        ridx_ref[...] = idx_out.astype(jnp.int32)
        rw_ref[...] = val_out / denom

    @pl.when(i < n_ctx_tiles)
    def _():
        compute(xc_ref[...], oc_ref[...], mc_ref[...])

    @pl.when(i >= n_ctx_tiles)
    def _():
        compute(xl_ref[...] + pos_ref[...], ol_ref[...], ml_ref[...])


def _merge(x_ctx, x_lat, pos, mod, o_ctx, o_lat, mixed_ctx, mixed_lat, proj,
           wbg, wbf, wo, l1g, l1b, wr, br, lat_len, alpha):
    t_ctx, t_lat = x_ctx.shape[0], x_lat.shape[0]
    t_all = t_ctx + t_lat
    tm = TM_MIX
    n_ctx_tiles = t_ctx // tm
    kern = functools.partial(_merge_kernel, n_ctx_tiles=n_ctx_tiles, alpha=alpha)
    specs = _token_specs(tm, n_ctx_tiles, lat_len // tm, mod.shape[0] - 1)
    ctx_map, lat_map = _group_maps(n_ctx_tiles)

    def const(shape):
        return pl.BlockSpec(shape, lambda i: (0,) * len(shape))

    in_specs = specs + [
        pl.BlockSpec((tm, GLA_DV), ctx_map),
        pl.BlockSpec((tm, GLA_DV), lat_map),
        pl.BlockSpec((tm, FNET_DIM), ctx_map),
        pl.BlockSpec((tm, FNET_DIM), lat_map),
        pl.BlockSpec((tm, GLA_DV), lambda i: (i, COL_G // GLA_DV)),
        pl.BlockSpec((tm, D_MODEL), lambda i: (i, COL_GATE_A // D_MODEL)),
        pl.BlockSpec((tm, D_MODEL), lambda i: (i, COL_GATE_B // D_MODEL)),
        const((GLA_DV, D_MODEL)), const((FNET_DIM, D_MODEL)), const((D_MODEL, D_MODEL)),
        const((1, D_MODEL)), const((1, D_MODEL)),
        const((D_MODEL, LANES)), const((1, LANES)),
    ]
    out_specs = [pl.BlockSpec((tm, D_MODEL), lambda i: (i, 0)),
                 pl.BlockSpec((tm * ROW_TILES, LANES), lambda i: (i, 0)),
                 pl.BlockSpec((tm, LANES), lambda i: (i, 0)),
                 pl.BlockSpec((tm, LANES), lambda i: (i, 0))]
    out_shape = [jax.ShapeDtypeStruct((t_all, D_MODEL), F32),
                 jax.ShapeDtypeStruct((t_all * ROW_TILES, LANES), F32),
                 jax.ShapeDtypeStruct((t_all, LANES), jnp.int32),
                 jax.ShapeDtypeStruct((t_all, LANES), F32)]
    return pl.pallas_call(
        kern,
        grid=(t_all // tm,),
        in_specs=in_specs,
        out_specs=out_specs,
        out_shape=out_shape,
        compiler_params=pltpu.CompilerParams(
            dimension_semantics=("arbitrary",), vmem_limit_bytes=VMEM_LIMIT),
        name="merge_ln1_router",
    )(x_ctx, x_lat, pos, mod, o_ctx, o_lat, mixed_ctx, mixed_lat, proj, proj, proj,
      wbg, wbf, wo, l1g, l1b, wr, br)


def _sc_mesh():
    return plsc.VectorSubcoreMesh(core_axis_name="c", subcore_axis_name="s")


def _sc_worker_id():
    return lax.axis_index("s") * SC_CORES + lax.axis_index("c")


def _sc_scatter_rows(src, idx, n_out):
    n_src = src.shape[0]
    w = SC_WINDOW
    n_chunks = n_src // (SC_WORKERS * w)
    copies = idx.shape[0]
    assert n_chunks % 2 == 0 and idx.shape == (copies, SC_WORKERS, n_chunks, w)

    @functools.partial(
        pl.kernel, mesh=_sc_mesh(),
        out_type=jax.ShapeDtypeStruct((n_out, ROW_TILES, LANES), F32),
        scratch_types=[pltpu.VMEM((copies * n_chunks, w), jnp.int32),
                       pltpu.VMEM((2, w, ROW_TILES, LANES), F32),
                       pltpu.SemaphoreType.DMA((2,)),
                       pltpu.SemaphoreType.DMA((2,))],
        name="moe_dispatch_scatter")
    def k(src_hbm, idx_hbm, out_hbm, idx_v, rows_v, rsem, wsem):
        wid = _sc_worker_id()
        base = wid * (n_chunks * w)
        for kk in range(copies):
            pltpu.sync_copy(idx_hbm.at[kk, wid], idx_v.at[pl.ds(kk * n_chunks, n_chunks)])

        def read(j, slot):
            return pltpu.make_async_copy(src_hbm.at[pl.ds(base + j * w, w)], rows_v.at[slot],
                                         rsem.at[slot])

        def scatter(j, kk, slot):
            return pltpu.make_async_copy(rows_v.at[slot], out_hbm.at[idx_v.at[kk * n_chunks + j]],
                                         wsem.at[slot])

        read(0, 0).start()

        @pl.loop(0, n_chunks, step=2)
        def _(jj):
            read(jj, 0).wait()

            @pl.when(jj > 0)
            def _():
                for kk in range(copies):
                    scatter(jj - 1, kk, 1).wait()

            read(jj + 1, 1).start()
            for kk in range(copies):
                scatter(jj, kk, 0).start()
            read(jj + 1, 1).wait()
            for kk in range(copies):
                scatter(jj, kk, 0).wait()

            @pl.when(jj + 2 < n_chunks)
            def _():
                read(jj + 2, 0).start()

            for kk in range(copies):
                scatter(jj + 1, kk, 1).start()

        for kk in range(copies):
            scatter(n_chunks - 1, kk, 1).wait()

    return k(src, idx)


def _sc_gather_rows(table, idx):
    _, n_chunks, w = idx.shape
    assert n_chunks % 2 == 0 and idx.shape[0] == SC_WORKERS and w == SC_WINDOW
    n_out = SC_WORKERS * n_chunks * w

    @functools.partial(
        pl.kernel, mesh=_sc_mesh(),
        out_type=jax.ShapeDtypeStruct((n_out, ROW_TILES, LANES), F32),
        scratch_types=[pltpu.VMEM((n_chunks, w), jnp.int32),
                       pltpu.VMEM((2, w, ROW_TILES, LANES), F32),
                       pltpu.SemaphoreType.DMA((2,)),
                       pltpu.SemaphoreType.DMA((2,))],
        name="moe_combine_gather")
    def k(table_hbm, idx_hbm, out_hbm, idx_v, rows_v, gsem, wsem):
        wid = _sc_worker_id()
        base = wid * (n_chunks * w)
        pltpu.sync_copy(idx_hbm.at[wid], idx_v)

        def gather(j, slot):
            return pltpu.make_async_copy(table_hbm.at[idx_v.at[j]], rows_v.at[slot], gsem.at[slot])

        def write(j, slot):
            return pltpu.make_async_copy(rows_v.at[slot], out_hbm.at[pl.ds(base + j * w, w)],
                                         wsem.at[slot])

        gather(0, 0).start()

        @pl.loop(0, n_chunks, step=2)
        def _(jj):
            gather(jj, 0).wait()

            @pl.when(jj > 0)
            def _():
                write(jj - 1, 1).wait()

            gather(jj + 1, 1).start()
            write(jj, 0).start()
            gather(jj + 1, 1).wait()
            write(jj, 0).wait()

            @pl.when(jj + 2 < n_chunks)
            def _():
                gather(jj + 2, 0).start()

            write(jj + 1, 1).start()

        write(n_chunks - 1, 1).wait()

    return k(table, idx)


def _moe_kernel(be_ref, nu_ref, nv_ref, slot_ref, nxt_ref, x_ref, wgu_hbm, bgu_ref, wd_hbm, bd_ref,
                o_ref, wgu_st, wd_st, wgu_bf, wd_bf, xb_ref, sem):
    b = pl.program_id(0)
    e = be_ref[b]
    prev = be_ref[jnp.maximum(b - 1, 0)]
    active = b < nu_ref[0]
    changed = (b == 0) | (e != prev)

    def weight_copies(expert, s):
        return (pltpu.make_async_copy(wgu_hbm.at[expert], wgu_st.at[s], sem.at[0, s]),
                pltpu.make_async_copy(wd_hbm.at[expert], wd_st.at[s], sem.at[1, s]))

    @pl.when(active & changed)
    def _():
        s = slot_ref[b]

        @pl.when(b == 0)
        def _():
            for cp in weight_copies(e, s):
                cp.start()

        for cp in weight_copies(e, s):
            cp.wait()
        wgu_bf[...] = wgu_st[s].astype(BF16)
        wd_bf[...] = wd_st[s].astype(BF16)
        nxt = nxt_ref[b]

        @pl.when(nxt >= 0)
        def _():
            for cp in weight_copies(nxt, 1 - s):
                cp.start()

    @pl.when(active)
    def _():
        valid = lax.broadcasted_iota(jnp.int32, (MOE_ROWS, LANES), 0) < nv_ref[b]
        for j in range(ROW_TILES):
            xj = x_ref[_row_tile_slice(j, MOE_ROWS), :]
            xb_ref[:, j * LANES:(j + 1) * LANES] = jnp.where(valid, xj, 0.0).astype(BF16)
        gu = _dot(xb_ref[...], wgu_bf[...]) + bgu_ref[0]
        gate = jnp.minimum(gu[:, :D_EXPERT], SWIGLU_LIMIT)
        up = jnp.clip(gu[:, D_EXPERT:], -SWIGLU_LIMIT, SWIGLU_LIMIT)
        glu = gate * _sigmoid(gate * SWIGLU_ALPHA)
        act = ((up + 1.0) * glu).astype(BF16)
        _store_row_tiles(o_ref, _dot(act, wd_bf[...]) + bd_ref[0])


def _moe(tables, xs, w_gate_up, b_gate_up, w_down, b_down):
    p_rows = xs.shape[0] // ROW_TILES
    n_blocks = p_rows // MOE_ROWS

    def blk(b, be, nu, *_):
        return jnp.minimum(b, nu[0] - 1)

    def expert(b, be, nu, *_):
        return (be[blk(b, be, nu)], 0, 0)

    def rows(b, be, nu, *_):
        return (blk(b, be, nu), 0)

    grid_spec = pltpu.PrefetchScalarGridSpec(
        num_scalar_prefetch=len(tables),
        grid=(n_blocks,),
        in_specs=[
            pl.BlockSpec((MOE_ROWS * ROW_TILES, LANES), rows),
            pl.BlockSpec(memory_space=pl.ANY),
            pl.BlockSpec((1, 1, 2 * D_EXPERT), expert),
            pl.BlockSpec(memory_space=pl.ANY),
            pl.BlockSpec((1, 1, D_MODEL), expert),
        ],
        out_specs=pl.BlockSpec((MOE_ROWS * ROW_TILES, LANES), rows),
        scratch_shapes=[pltpu.VMEM((2, D_MODEL, 2 * D_EXPERT), F32),
                        pltpu.VMEM((2, D_EXPERT, D_MODEL), F32),
                        pltpu.VMEM((D_MODEL, 2 * D_EXPERT), BF16),
                        pltpu.VMEM((D_EXPERT, D_MODEL), BF16),
                        pltpu.VMEM((MOE_ROWS, D_MODEL), BF16),
                        pltpu.SemaphoreType.DMA((2, 2))],
    )
    return pl.pallas_call(
        _moe_kernel,
        grid_spec=grid_spec,
        out_shape=jax.ShapeDtypeStruct((p_rows * ROW_TILES, LANES), F32),
        compiler_params=pltpu.CompilerParams(
            dimension_semantics=("arbitrary",), vmem_limit_bytes=VMEM_LIMIT),
        name="moe_grouped_mlp",
    )(*tables, xs, w_gate_up, b_gate_up.reshape(N_EXPERTS, 1, 2 * D_EXPERT), w_down,
      b_down.reshape(N_EXPERTS, 1, D_MODEL))


def _combine_kernel(x1_ref, y0_ref, y1_ref, y2_ref, y3_ref, rw_ref, mod_ref, g_ref, b_ref, o_ref,
                    *, alpha):
    rw = rw_ref[...]
    y_refs = (y0_ref, y1_ref, y2_ref, y3_ref)
    pieces = []
    for j in range(ROW_TILES):
        tile = _row_tile_slice(j, rw.shape[0])
        acc = rw[:, 0:1] * y_refs[0][tile, :]
        for kk in range(1, TOP_K):
            acc = acc + rw[:, kk:kk + 1] * y_refs[kk][tile, :]
        pieces.append(acc)
    ff = jnp.concatenate(pieces, axis=-1)
    y = alpha * x1_ref[...] + mod_ref[0, 5:6, :] * ff
    o_ref[...] = _layer_norm(y) * g_ref[...] + b_ref[...]


def _combine(x1, yg, rw, mod, l2g, l2b, *, row0, n_rows, mod_map, alpha):
    tm = TM_MIX
    t0 = row0 // tm
    tiles_all = x1.shape[0] // tm
    kern = functools.partial(_combine_kernel, alpha=alpha)

    def y_spec(kk):
        return pl.BlockSpec((tm * ROW_TILES, LANES), lambda i: (kk * tiles_all + t0 + i, 0))

    return pl.pallas_call(
        kern,
        grid=(n_rows // tm,),
        in_specs=[pl.BlockSpec((tm, D_MODEL), lambda i: (t0 + i, 0))]
        + [y_spec(kk) for kk in range(TOP_K)]
        + [pl.BlockSpec((tm, LANES), lambda i: (t0 + i, 0)),
           pl.BlockSpec((1, N_MOD, D_MODEL), mod_map),
           pl.BlockSpec((1, D_MODEL), lambda i: (0, 0)),
           pl.BlockSpec((1, D_MODEL), lambda i: (0, 0))],
        out_specs=pl.BlockSpec((tm, D_MODEL), lambda i: (i, 0)),
        out_shape=jax.ShapeDtypeStruct((n_rows, D_MODEL), F32),
        compiler_params=pltpu.CompilerParams(
            dimension_semantics=("arbitrary",), vmem_limit_bytes=VMEM_LIMIT),
        name="combine_ln2",
    )(x1, yg, yg, yg, yg, rw, mod, l2g, l2b)


def _route_kernel(ridx_ref, dest_ref, cnt_ref, run_ref, bst_ref):
    phase = pl.program_id(0)
    i = pl.program_id(1)
    tm = ridx_ref.shape[0]
    ridx = ridx_ref[...]
    lane = lax.broadcasted_iota(jnp.int32, (tm, LANES), 1)
    hits = [ridx[:, kk:kk + 1] == lane for kk in range(TOP_K)]
    chosen = jnp.where(hits[0], 1.0, 0.0)
    for kk in range(1, TOP_K):
        chosen = chosen + jnp.where(hits[kk], 1.0, 0.0)
    colsum = jnp.sum(chosen, axis=0, keepdims=True)

    @pl.when((phase == 0) & (i == 0))
    def _():
        run_ref[...] = jnp.zeros_like(run_ref)

    @pl.when(phase == 0)
    def _():
        run_ref[...] = run_ref[...] + colsum

    @pl.when((phase == 1) & (i == 0))
    def _():
        counts = run_ref[...]
        cnt_ref[...] = counts
        blocks = jnp.floor((counts + (MOE_ROWS - 1.0)) * (1.0 / MOE_ROWS))
        r = lax.broadcasted_iota(jnp.int32, (LANES, LANES), 0)
        c = lax.broadcasted_iota(jnp.int32, (LANES, LANES), 1)
        before = jnp.dot(blocks, (r < c).astype(F32), precision=HIGHEST, preferred_element_type=F32)
        bst_ref[...] = before * float(MOE_ROWS)
        run_ref[...] = jnp.zeros_like(run_ref)

    @pl.when(phase == 1)
    def _():
        rt = lax.broadcasted_iota(jnp.int32, (tm, tm), 0)
        ct = lax.broadcasted_iota(jnp.int32, (tm, tm), 1)
        earlier = _dot((ct < rt).astype(BF16), chosen.astype(BF16))
        row_of = bst_ref[0:1, :] + run_ref[0:1, :] + earlier
        out = jnp.zeros((tm, LANES), F32)
        for kk in range(TOP_K):
            dk = jnp.sum(jnp.where(hits[kk], row_of, 0.0), axis=-1, keepdims=True)
            out = jnp.where(lane == kk, dk, out)
        dest_ref[...] = out.T[0:SUBLANES, :].astype(jnp.int32)
        run_ref[...] = run_ref[...] + colsum


def _route(ridx):
    t_all = ridx.shape[0]
    tm = TM_ROUTE
    return pl.pallas_call(
        _route_kernel,
        grid=(2, t_all // tm),
        in_specs=[pl.BlockSpec((tm, LANES), lambda p, i: (i, 0))],
        out_specs=[pl.BlockSpec((SUBLANES, tm), lambda p, i: (0, i * p)),
                   pl.BlockSpec((SUBLANES, LANES), lambda p, i: (0, 0))],
        out_shape=[jax.ShapeDtypeStruct((SUBLANES, t_all), jnp.int32),
                   jax.ShapeDtypeStruct((SUBLANES, LANES), F32)],
        scratch_shapes=[pltpu.VMEM((SUBLANES, LANES), F32),
                        pltpu.VMEM((SUBLANES, LANES), F32)],
        compiler_params=pltpu.CompilerParams(
            dimension_semantics=("arbitrary", "arbitrary"), vmem_limit_bytes=VMEM_LIMIT),
        name="moe_route",
    )(ridx)


def _routing_tables(counts, n_blocks):
    experts = jnp.arange(N_EXPERTS, dtype=jnp.int32)
    blocks_per = (counts + MOE_ROWS - 1) // MOE_ROWS
    bends = jnp.cumsum(blocks_per)
    bstarts = bends - blocks_per
    blocks = jnp.arange(n_blocks, dtype=jnp.int32)
    block_expert = jnp.minimum(
        jnp.sum((bends[None, :] <= blocks[:, None]).astype(jnp.int32), axis=1), N_EXPERTS - 1)
    n_used = bends[-1:].astype(jnp.int32)
    owner = block_expert[:, None] == experts[None, :]

    def per_block(table):
        return jnp.sum(jnp.where(owner, table[None, :], 0), axis=1)

    n_valid = jnp.clip(per_block(counts) - (blocks - per_block(bstarts)) * MOE_ROWS,
                       0, MOE_ROWS).astype(jnp.int32)
    present = blocks_per > 0
    ordinal = jnp.cumsum(present.astype(jnp.int32)) - 1
    later = lax.cummin(jnp.where(present, experts, N_EXPERTS), reverse=True)
    succ = jnp.concatenate([later[1:], jnp.full((1,), N_EXPERTS, jnp.int32)])
    succ = jnp.where(succ >= N_EXPERTS, -1, succ)
    stage_slot = (per_block(ordinal) % 2).astype(jnp.int32)
    next_expert = per_block(succ).astype(jnp.int32)
    return (block_expert.astype(jnp.int32), n_used, n_valid, stage_slot, next_expert)


def _pos_embed_2d(n_tokens):
    rows = n_tokens // GRID_W
    r = jnp.repeat(jnp.arange(rows), GRID_W).astype(F32)
    col = jnp.tile(jnp.arange(GRID_W), rows).astype(F32)
    quarter = D_MODEL // 4
    omega = 1.0 / (10000.0 ** (jnp.arange(quarter, dtype=F32) / quarter))
    er = r[:, None] * omega
    ec = col[:, None] * omega
    return jnp.concatenate([jnp.sin(er), jnp.cos(er), jnp.sin(ec), jnp.cos(ec)], axis=-1)


def _reorder_w_in(w):
    o_r = 2 * GLA_DK + 2 * GLA_DV
    o_f = o_r + DECAY_RANK
    o_gate = o_f + FNET_DIM
    pad = jnp.zeros((w.shape[0], LANES - DECAY_RANK), w.dtype)
    w_q = w[:, :GLA_DK] * (DK_HEAD ** -0.5)
    return jnp.concatenate([w_q, w[:, GLA_DK:o_r], w[:, o_gate:], w[:, o_f:o_gate], w[:, o_r:o_f], pad],
                           axis=1)


def kernel(x_prompt, x_sample, state_gla, c, c_ctx, w_ada, b_ada, w_in, w_dec_fwd, b_dec_fwd,
           w_dec_bwd, b_dec_bwd, gla_norm_g, w_br_gla, w_br_fnet, w_out, ln1_g, ln1_b, w_router,
           b_router, w_gate_up, b_gate_up, w_down, b_down, ln2_g, ln2_b):
    n_req, ctx_len, _ = x_prompt.shape
    n_lat, lat_len, _ = x_sample.shape
    depth = w_in.shape[0]
    alpha = (2.0 * depth) ** 0.25
    t_ctx = n_req * ctx_len
    t_lat = n_lat * lat_len
    t_all = t_ctx + t_lat

    x_ctx = x_prompt.reshape(t_ctx, D_MODEL)
    x_lat = x_sample.reshape(t_lat, D_MODEL)
    pos = _pos_embed_2d(lat_len)
    zero_pos = jnp.zeros_like(pos)

    cond_rows = -(-(n_lat + 1) // SUBLANES) * SUBLANES
    cond = jnp.zeros((cond_rows, D_MODEL), F32).at[:n_lat].set(c).at[cond_rows - 1].set(c_ctx)

    n_moe_blocks = (t_all * TOP_K) // MOE_ROWS + N_EXPERTS
    tok_chunks = t_all // (SC_WORKERS * SC_WINDOW)
    states = []
    for l in range(depth):
        mod = _ada(cond, w_ada[l], b_ada[l]).reshape(cond_rows, N_MOD, D_MODEL)
        layer_pos = pos if l == 0 else zero_pos
        proj = _inproj(x_ctx, x_lat, layer_pos, mod, _reorder_w_in(w_in[l]).astype(BF16), lat_len)

        def pad_dec(w):
            return jnp.zeros((LANES, GLA_DK), F32).at[:DECAY_RANK].set(w)

        dec = (pad_dec(w_dec_fwd[l]), b_dec_fwd[l].reshape(1, GLA_DK),
               pad_dec(w_dec_bwd[l]), b_dec_bwd[l].reshape(1, GLA_DK),
               gla_norm_g[l].reshape(1, DV_HEAD))
        o_ctx, s_new = _gla(proj, *dec, None, n_seq=n_req, seq_len=ctx_len, row0=0, emit_state=True)
        (o_lat,) = _gla(proj, *dec, state_gla[:, l], n_seq=n_lat, seq_len=lat_len, row0=t_ctx,
                        emit_state=False)
        states.append(s_new)

        mixed_ctx = _fnet(proj, n_seq=n_req, seq_len=ctx_len, row0=0)
        mixed_lat = _fnet(proj, n_seq=n_lat, seq_len=lat_len, row0=t_ctx)

        wr = jnp.zeros((D_MODEL, LANES), F32).at[:, :N_EXPERTS].set(w_router[l])
        br = jnp.full((1, LANES), -1e30, F32).at[0, :N_EXPERTS].set(b_router[l])
        x1, h2, ridx, rw = _merge(
            x_ctx, x_lat, layer_pos, mod, o_ctx, o_lat, mixed_ctx, mixed_lat, proj,
            w_br_gla[l].astype(BF16), w_br_fnet[l].astype(BF16), w_out[l].astype(BF16),
            ln1_g[l].reshape(1, D_MODEL), ln1_b[l].reshape(1, D_MODEL), wr, br, lat_len, alpha)

        dest, counts = _route(ridx)
        moe_tables = _routing_tables(counts[0, :N_EXPERTS].astype(jnp.int32), n_moe_blocks)
        dest = dest[:TOP_K]
        scatter_idx = dest.reshape(TOP_K, SC_WORKERS, tok_chunks, SC_WINDOW)
        p_rows = n_moe_blocks * MOE_ROWS
        xs = _sc_scatter_rows(h2.reshape(t_all, ROW_TILES, LANES), scatter_idx, p_rows)
        yb = _moe(moe_tables, xs.reshape(p_rows * ROW_TILES, LANES),
                  w_gate_up[l], b_gate_up[l], w_down[l], b_down[l])
        gather_idx = dest.reshape(SC_WORKERS, TOP_K * tok_chunks, SC_WINDOW)
        yg = _sc_gather_rows(yb.reshape(p_rows, ROW_TILES, LANES), gather_idx)
        yg = yg.reshape(TOP_K * t_all * ROW_TILES, LANES)

        l2g = ln2_g[l].reshape(1, D_MODEL)
        l2b = ln2_b[l].reshape(1, D_MODEL)
        tiles_per_seq = lat_len // TM_MIX
        x_ctx = _combine(x1, yg, rw, mod, l2g, l2b, row0=0, n_rows=t_ctx,
                         mod_map=lambda i: (cond_rows - 1, 0, 0), alpha=alpha)
        x_lat = _combine(x1, yg, rw, mod, l2g, l2b, row0=t_ctx, n_rows=t_lat,
                         mod_map=lambda i: (i // tiles_per_seq, 0, 0), alpha=alpha)

    y_prompt = x_ctx.reshape(x_prompt.shape)
    y_sample = x_lat.reshape(x_sample.shape)
    new_state = jnp.stack(states, axis=1).astype(x_prompt.dtype)
    return (y_prompt, y_sample, new_state)
```

```python
import functools
import math

import numpy as np
import jax
import jax.numpy as jnp
from jax import lax
from jax.experimental import pallas as pl
from jax.experimental.pallas import tpu as pltpu
from jax.experimental.pallas import tpu_sc as plsc

F32 = jnp.float32
BF16 = jnp.bfloat16

D_MODEL = 1024
GRID_W = 64
GLA_HEADS = 4
DK_HEAD = 128
DV_HEAD = 256
GLA_DK = GLA_HEADS * DK_HEAD
GLA_DV = GLA_HEADS * DV_HEAD
DECAY_RANK = 16
GATE_NORMALIZER = 16.0
FNET_GROUPS = 4
FNET_GROUP_DIM = 128
FNET_DIM = FNET_GROUPS * FNET_GROUP_DIM
N_EXPERTS = 32
TOP_K = 4
D_EXPERT = 1024
SWIGLU_LIMIT = 7.0
SWIGLU_ALPHA = 1.702
LN_EPS = 1e-6
N_MOD = 6

LANES = 128
SUBLANES = 8
HALF_MODEL = D_MODEL // 2
ROW_TILES = HALF_MODEL // LANES
ROW_DTYPE = jnp.uint32
COL_Q = 0
COL_K = GLA_DK
COL_V = 2 * GLA_DK
COL_G = COL_V + GLA_DV
COL_GATE_A = COL_G + GLA_DV
COL_GATE_B = COL_GATE_A + D_MODEL
COL_F = COL_GATE_B + D_MODEL
COL_R = COL_F + FNET_DIM
PROJ_COLS = COL_R + LANES

GLA_CHUNK = 128
GLA_LEAF = 16
TM_PROJ = 256
TM_MIX = 256
TM_ROUTE = 1024
MOE_ROWS = 256
VMEM_LIMIT = 56 * 1024 * 1024

SC_CORES = 2
SC_SUBCORES = 16
SC_WORKERS = SC_CORES * SC_SUBCORES
SC_WINDOW = 32

HIGHEST = lax.Precision.HIGHEST


def _layer_norm(x):
    mu = jnp.mean(x, axis=-1, keepdims=True)
    xc = x - mu
    var = jnp.mean(xc * xc, axis=-1, keepdims=True)
    return xc * lax.rsqrt(var + LN_EPS)


def _sigmoid(x):
    return 0.5 * jnp.tanh(0.5 * x) + 0.5


def _log_sigmoid(z):
    return jnp.minimum(z, 0.0) - jnp.log1p(jnp.exp(-jnp.abs(z)))


def _dot(a, b):
    return jnp.dot(a, b, preferred_element_type=F32)


def _split_bf16(x, terms):
    parts = []
    for _ in range(terms):
        p = x.astype(BF16)
        parts.append(p)
        x = x - p.astype(F32)
    return parts


def _dot_nt(a, b):
    return lax.dot_general(a, b, (((1,), (1,)), ((), ())), preferred_element_type=F32)


def _dot_tn(a, b):
    return lax.dot_general(a, b, (((0,), (0,)), ((), ())), preferred_element_type=F32)


def _row_tile_slice(j, n_rows):
    return pl.ds(j, n_rows, stride=ROW_TILES)


def _store_row_tiles(ref, val):
    for j in range(ROW_TILES):
        lo = val[:, j * LANES:(j + 1) * LANES]
        hi = val[:, HALF_MODEL + j * LANES:HALF_MODEL + (j + 1) * LANES]
        ref[_row_tile_slice(j, val.shape[0]), :] = pltpu.pack_elementwise([lo, hi], packed_dtype=BF16)


def _load_row_tile(ref, j, n_rows):
    words = ref[_row_tile_slice(j, n_rows), :]
    return tuple(pltpu.unpack_elementwise(words, index=half, packed_dtype=BF16, unpacked_dtype=F32)
                 for half in range(2))


def _ada_kernel(c_ref, w_ref, b_ref, o_ref):
    c = c_ref[...]
    s = c * _sigmoid(c)
    o_ref[...] = _dot(s.astype(BF16), w_ref[...].astype(BF16)) + b_ref[...]


def _ada(cond, w_ada, b_ada):
    rows = cond.shape[0]
    n = w_ada.shape[1]
    tn = 1536
    return pl.pallas_call(
        _ada_kernel,
        grid=(n // tn,),
        in_specs=[pl.BlockSpec((rows, D_MODEL), lambda j: (0, 0)),
                  pl.BlockSpec((D_MODEL, tn), lambda j: (0, j)),
                  pl.BlockSpec((1, tn), lambda j: (0, j))],
        out_specs=pl.BlockSpec((rows, tn), lambda j: (0, j)),
        out_shape=jax.ShapeDtypeStruct((rows, n), F32),
        compiler_params=pltpu.CompilerParams(vmem_limit_bytes=VMEM_LIMIT),
        name="ada_mod",
    )(cond, w_ada, b_ada.reshape(1, n))


def _group_maps(n_ctx_tiles):
    def ctx_map(i, *_):
        return (jnp.minimum(i, n_ctx_tiles - 1), 0)

    def lat_map(i, *_):
        return (jnp.maximum(i - n_ctx_tiles, 0), 0)

    return ctx_map, lat_map


def _token_specs(tm, n_ctx_tiles, tiles_per_latent_seq, ctx_mod_row):
    ctx_map, lat_map = _group_maps(n_ctx_tiles)

    def pos_map(i, *_):
        return (jnp.maximum(i - n_ctx_tiles, 0) % tiles_per_latent_seq, 0)

    def mod_map(i, *_):
        return (jnp.where(i < n_ctx_tiles, ctx_mod_row,
                          jnp.maximum(i - n_ctx_tiles, 0) // tiles_per_latent_seq), 0, 0)

    return [pl.BlockSpec((tm, D_MODEL), ctx_map),
            pl.BlockSpec((tm, D_MODEL), lat_map),
            pl.BlockSpec((tm, D_MODEL), pos_map),
            pl.BlockSpec((1, N_MOD, D_MODEL), mod_map)]


def _inproj_kernel(xc_ref, xl_ref, pos_ref, mod_ref, w_ref, o_ref, *, n_ctx_tiles):
    i = pl.program_id(0)

    def project(x):
        h = _layer_norm(x) * (1.0 + mod_ref[0, 1:2, :]) + mod_ref[0, 0:1, :]
        o_ref[...] = _dot(h.astype(BF16), w_ref[...])

    @pl.when(i < n_ctx_tiles)
    def _():
        project(xc_ref[...])

    @pl.when(i >= n_ctx_tiles)
    def _():
        project(xl_ref[...] + pos_ref[...])


def _inproj(x_ctx, x_lat, pos, mod, w_in_bf, lat_len):
    t_ctx, t_lat = x_ctx.shape[0], x_lat.shape[0]
    n_ctx_tiles = t_ctx // TM_PROJ
    n_tiles = (t_ctx + t_lat) // TM_PROJ
    kern = functools.partial(_inproj_kernel, n_ctx_tiles=n_ctx_tiles)
    specs = _token_specs(TM_PROJ, n_ctx_tiles, lat_len // TM_PROJ, mod.shape[0] - 1)
    return pl.pallas_call(
        kern,
        grid=(n_tiles,),
        in_specs=specs + [pl.BlockSpec((D_MODEL, PROJ_COLS), lambda i: (0, 0),
                                       pipeline_mode=pl.Buffered(1))],
        out_specs=pl.BlockSpec((TM_PROJ, PROJ_COLS), lambda i: (i, 0)),
        out_shape=jax.ShapeDtypeStruct((t_ctx + t_lat, PROJ_COLS), F32),
        compiler_params=pltpu.CompilerParams(
            dimension_semantics=("arbitrary",), vmem_limit_bytes=VMEM_LIMIT),
        name="ln_inproj",
    )(x_ctx, x_lat, pos, mod, w_in_bf)


def _gla_kernel(*refs, seq_len, has_s0, emit_state):
    it = iter(refs)
    q_ref, k_ref, v_ref, r_ref = next(it), next(it), next(it), next(it)
    wdf_ref, bdf_ref, wdb_ref, bdb_ref, g_ref = next(it), next(it), next(it), next(it), next(it)
    s0_ref = next(it) if has_s0 else None
    o_ref = next(it)
    sout_ref = next(it) if emit_state else None
    cum_ref, a_ref, qi_ref, ko_ref, dec_ref, op_ref, st_ref = (next(it) for _ in range(7))

    C = GLA_CHUNK
    n_chunks = seq_len // C

    def rows(n):
        if isinstance(n, int):
            return pl.ds(n * C, C)
        return pl.ds(pl.multiple_of(n * C, C), C)

    def dec_rows(n):
        if isinstance(n, int):
            return pl.ds(n * SUBLANES, SUBLANES)
        return pl.ds(pl.multiple_of(n * SUBLANES, SUBLANES), SUBLANES)

    def loop(body):
        if n_chunks <= 2:
            for n in range(n_chunks):
                body(n)
        else:
            def step(m, carry):
                body(2 * m)
                body(2 * m + 1)
                return carry
            lax.fori_loop(0, n_chunks // 2, step, 0)

    rt = lax.broadcasted_iota(jnp.int32, (C, C), 0)
    ct = lax.broadcasted_iota(jnp.int32, (C, C), 1)
    tri = ((rt >= ct).astype(BF16), (ct >= rt).astype(BF16))
    row_id = lax.broadcasted_iota(jnp.int32, (C, DK_HEAD), 0)

    r_hi, r_lo = _split_bf16(r_ref[...], 2)
    for d, (w_ref, b_ref) in enumerate(((wdf_ref, bdf_ref), (wdb_ref, bdb_ref))):
        w_hi, w_lo = _split_bf16(w_ref[...], 2)
        z = _dot(r_hi, w_hi) + _dot(r_lo, w_hi) + _dot(r_hi, w_lo) + b_ref[...]
        cum_ref[d] = _log_sigmoid(z) * (1.0 / GATE_NORMALIZER)

    def cumsum_chunk(n):
        for d in range(2):
            la_hi, la_lo = _split_bf16(cum_ref[d, rows(n), :], 2)
            cum_ref[d, rows(n), :] = _dot(tri[d], la_hi) + _dot(tri[d], la_lo)

    loop(cumsum_chunk)

    query_rows = ({}, {})
    keep = ({}, {})
    for d in range(2):
        blk = C // 2
        while blk >= GLA_LEAF:
            q_parity = 1 if d == 0 else 0
            query_rows[d][blk] = ((row_id // blk) % 2) == q_parity
            qb, kb = rt // blk, ct // blk
            keep[d][blk] = ((qb % 2) == q_parity) & ((qb == kb + 1) if d == 0 else (kb == qb + 1))
            blk //= 2
        order = (rt >= ct) if d == 0 else (ct >= rt)
        keep[d][0] = ((rt // GLA_LEAF) == (ct // GLA_LEAF)) & order

    def block_rows(cum, first, step, count):
        span = C // count
        parts = [jnp.broadcast_to(cum[first + p * step:first + p * step + 1, :], (span, DK_HEAD))
                 for p in range(count)]
        return parts[0] if count == 1 else jnp.concatenate(parts, axis=0)

    def scores(n, d):
        cum = cum_ref[d, rows(n), :]
        q = q_ref[rows(n), :]
        k = k_ref[rows(n), :]
        acc = None
        blk = C // 2
        while blk >= GLA_LEAF:
            pairs = C // (2 * blk)
            bnd = blk - 1 if d == 0 else blk
            w = jnp.exp(-jnp.abs(cum - block_rows(cum, bnd, 2 * blk, pairs)))
            x = (jnp.where(query_rows[d][blk], q, k) * w).astype(BF16)
            s = jnp.where(keep[d][blk], _dot_nt(x, x), 0.0)
            acc = s if acc is None else acc + s
            blk //= 2
        mid = GLA_LEAF // 2 - 1 if d == 0 else GLA_LEAF // 2
        e = cum - block_rows(cum, mid, GLA_LEAF, C // GLA_LEAF)
        s = _dot_nt((q * jnp.exp(e)).astype(BF16), (k * jnp.exp(-e)).astype(BF16))
        acc = acc + jnp.where(keep[d][0], s, 0.0)
        a_ref[d, rows(n), :] = acc.astype(BF16)
        end = cum[C - 1:C, :] if d == 0 else cum[0:1, :]
        qi_ref[d, rows(n), :] = (q * jnp.exp(cum)).astype(BF16)
        ko_ref[d, rows(n), :] = (k * jnp.exp(end - cum)).astype(BF16)
        dec_ref[d, dec_rows(n), :] = jnp.broadcast_to(jnp.exp(end), (SUBLANES, DK_HEAD))

    def scores_chunk(n):
        scores(n, 0)
        scores(n, 1)

    loop(scores_chunk)

    for d in range(2):
        if has_s0:
            st_ref[d] = s0_ref[0, d, 0].T
        else:
            st_ref[d] = jnp.zeros((DV_HEAD, DK_HEAD), F32)

    def scan(n, d):
        v = v_ref[rows(n), :].astype(BF16)
        st = st_ref[d]
        o = _dot(a_ref[d, rows(n), :], v) + _dot_nt(qi_ref[d, rows(n), :], st.astype(BF16))
        st_ref[d] = st * dec_ref[d, dec_rows(n), :][0:1, :] + _dot_tn(v, ko_ref[d, rows(n), :])
        op_ref[d, rows(n), :] = o

    def scan_chunk(m):
        scan(m, 0)
        scan(n_chunks - 1 - m, 1)

    loop(scan_chunk)
    if emit_state:
        sout_ref[0, 0, 0] = st_ref[0].T
        sout_ref[0, 1, 0] = st_ref[1].T

    g = g_ref[...]

    def finish_chunk(n):
        o = op_ref[0, rows(n), :] + op_ref[1, rows(n), :]
        ms = jnp.mean(o * o, axis=-1, keepdims=True)
        o_ref[rows(n), :] = o * lax.rsqrt(ms + LN_EPS) * g

    loop(finish_chunk)


def _gla(proj, wdf, bdf, wdb, bdb, g, s0, *, n_seq, seq_len, row0, emit_state):
    has_s0 = s0 is not None
    blk0 = row0 // seq_len
    kern = functools.partial(_gla_kernel, seq_len=seq_len, has_s0=has_s0, emit_state=emit_state)
    in_specs = [
        pl.BlockSpec((seq_len, DK_HEAD), lambda b, h: (blk0 + b, COL_Q // DK_HEAD + h)),
        pl.BlockSpec((seq_len, DK_HEAD), lambda b, h: (blk0 + b, COL_K // DK_HEAD + h)),
        pl.BlockSpec((seq_len, DV_HEAD), lambda b, h: (blk0 + b, COL_V // DV_HEAD + h)),
        pl.BlockSpec((seq_len, LANES), lambda b, h: (blk0 + b, COL_R // LANES)),
        pl.BlockSpec((LANES, DK_HEAD), lambda b, h: (0, h)),
        pl.BlockSpec((1, DK_HEAD), lambda b, h: (0, h)),
        pl.BlockSpec((LANES, DK_HEAD), lambda b, h: (0, h)),
        pl.BlockSpec((1, DK_HEAD), lambda b, h: (0, h)),
        pl.BlockSpec((1, DV_HEAD), lambda b, h: (0, 0)),
    ]
    args = [proj, proj, proj, proj, wdf, bdf, wdb, bdb, g]
    if has_s0:
        in_specs.append(pl.BlockSpec((1, 2, 1, DK_HEAD, DV_HEAD), lambda b, h: (b, 0, h, 0, 0)))
        args.append(s0)
    out_specs = [pl.BlockSpec((seq_len, DV_HEAD), lambda b, h: (b, h))]
    out_shape = [jax.ShapeDtypeStruct((n_seq * seq_len, GLA_DV), F32)]
    if emit_state:
        out_specs.append(pl.BlockSpec((1, 2, 1, DK_HEAD, DV_HEAD), lambda b, h: (b, 0, h, 0, 0)))
        out_shape.append(jax.ShapeDtypeStruct((n_seq, 2, GLA_HEADS, DK_HEAD, DV_HEAD), F32))

    res = pl.pallas_call(
        kern,
        grid=(n_seq, GLA_HEADS),
        in_specs=in_specs,
        out_specs=out_specs,
        out_shape=out_shape,
        scratch_shapes=[pltpu.VMEM((2, seq_len, DK_HEAD), F32),
                        pltpu.VMEM((2, seq_len, GLA_CHUNK), BF16),
                        pltpu.VMEM((2, seq_len, DK_HEAD), BF16),
                        pltpu.VMEM((2, seq_len, DK_HEAD), BF16),
                        pltpu.VMEM((2, seq_len // GLA_CHUNK * SUBLANES, DK_HEAD), F32),
                        pltpu.VMEM((2, seq_len, DV_HEAD), F32),
                        pltpu.VMEM((2, DV_HEAD, DK_HEAD), F32)],
        compiler_params=pltpu.CompilerParams(
            dimension_semantics=("arbitrary", "arbitrary"), vmem_limit_bytes=VMEM_LIMIT),
        name="gla_seq%d" % seq_len,
    )(*args)
    return res


def _fnet_kernel(f_ref, cl_ref, sl_ref, cg_ref, sg_ref, o_ref, uc_ref, us_ref, *, seq_len):
    cg = cg_ref[...]
    sg = sg_ref[...]
    for grp in range(FNET_GROUPS):
        lo = grp * FNET_GROUP_DIM
        u = f_ref[:, lo:lo + FNET_GROUP_DIM].astype(BF16)
        uc_ref[:, lo:lo + FNET_GROUP_DIM] = _dot(u, cg).astype(BF16)
        us_ref[:, lo:lo + FNET_GROUP_DIM] = _dot(u, sg).astype(BF16)
    mixed = _dot(cl_ref[...], uc_ref[...]) - _dot(sl_ref[...], us_ref[...])
    o_ref[...] = mixed * (1.0 / math.sqrt(seq_len * FNET_GROUP_DIM))


def _dft_mats(n):
    j = np.arange(n, dtype=np.int64)
    ang = (2.0 * np.pi / n) * ((j[:, None] * j[None, :]) % n).astype(np.float64)
    return (jnp.asarray(np.cos(ang), dtype=F32).astype(BF16),
            jnp.asarray(np.sin(ang), dtype=F32).astype(BF16))


def _fnet(proj, *, n_seq, seq_len, row0):
    blk0 = row0 // seq_len
    cl, sl = _dft_mats(seq_len)
    cg, sg = _dft_mats(FNET_GROUP_DIM)
    kern = functools.partial(_fnet_kernel, seq_len=seq_len)
    return pl.pallas_call(
        kern,
        grid=(n_seq,),
        in_specs=[pl.BlockSpec((seq_len, FNET_DIM), lambda b: (blk0 + b, COL_F // FNET_DIM)),
                  pl.BlockSpec((seq_len, seq_len), lambda b: (0, 0)),
                  pl.BlockSpec((seq_len, seq_len), lambda b: (0, 0)),
                  pl.BlockSpec((FNET_GROUP_DIM, FNET_GROUP_DIM), lambda b: (0, 0)),
                  pl.BlockSpec((FNET_GROUP_DIM, FNET_GROUP_DIM), lambda b: (0, 0))],
        out_specs=pl.BlockSpec((seq_len, FNET_DIM), lambda b: (b, 0)),
        out_shape=jax.ShapeDtypeStruct((n_seq * seq_len, FNET_DIM), F32),
        scratch_shapes=[pltpu.VMEM((seq_len, FNET_DIM), BF16),
                        pltpu.VMEM((seq_len, FNET_DIM), BF16)],
        compiler_params=pltpu.CompilerParams(
            dimension_semantics=("arbitrary",), vmem_limit_bytes=VMEM_LIMIT),
        name="fnet_seq%d" % seq_len,
    )(proj, cl, sl, cg, sg)


def _merge_kernel(xc_ref, xl_ref, pos_ref, mod_ref, oc_ref, ol_ref, mc_ref, ml_ref,
                  g_ref, ga_ref, gb_ref, wbg_ref, wbf_ref, wo_ref, l1g_ref, l1b_ref, wr_ref, br_ref,
                  x1_ref, h2_ref, ridx_ref, rw_ref, *, n_ctx_tiles, alpha):
    i = pl.program_id(0)

    def compute(x, o, mx):
        g = g_ref[...]
        a = (o * (g * _sigmoid(g))).astype(BF16)
        gla_out = _dot(a, wbg_ref[...])
        fnet_out = _dot(mx.astype(BF16), wbf_ref[...])
        merged = _sigmoid(ga_ref[...]) * gla_out + _sigmoid(gb_ref[...]) * fnet_out
        mix = _dot(merged.astype(BF16), wo_ref[...])
        y = alpha * x + mod_ref[0, 2:3, :] * mix
        x1 = _layer_norm(y) * l1g_ref[...] + l1b_ref[...]
        x1_ref[...] = x1
        h2 = _layer_norm(x1) * (1.0 + mod_ref[0, 4:5, :]) + mod_ref[0, 3:4, :]
        _store_row_tiles(h2_ref, h2)

        logits = jnp.dot(h2, wr_ref[...], precision=HIGHEST, preferred_element_type=F32) + br_ref[...]
        lane_i = lax.broadcasted_iota(jnp.int32, logits.shape, 1)
        lane = lane_i.astype(F32)
        idx_out = jnp.zeros(logits.shape, F32)
        val_out = jnp.zeros(logits.shape, F32)
        top0 = None
        denom = None
        for kk in range(TOP_K):
            m = jnp.max(logits, axis=-1, keepdims=True)
            sel = jnp.min(jnp.where(logits == m, lane, float(LANES)), axis=-1, keepdims=True)
            if kk == 0:
                top0 = m
                p = jnp.ones_like(m)
                denom = p
            else:
                p = jnp.exp(m - top0)
                denom = denom + p
            idx_out = jnp.where(lane_i == kk, sel, idx_out)
            val_out = jnp.where(lane_i == kk, p, val_out)
            logits = jnp.where(lane == sel, -jnp.inf, logits)
        ridx_ref[...] = idx_out.astype(jnp.int32)
        rw_ref[...] = val_out / denom

    @pl.when(i < n_ctx_tiles)
    def _():
        compute(xc_ref[...], oc_ref[...], mc_ref[...])

    @pl.when(i >= n_ctx_tiles)
    def _():
        compute(xl_ref[...] + pos_ref[...], ol_ref[...], ml_ref[...])


def _merge(x_ctx, x_lat, pos, mod, o_ctx, o_lat, mixed_ctx, mixed_lat, proj,
           wbg, wbf, wo, l1g, l1b, wr, br, lat_len, alpha):
    t_ctx, t_lat = x_ctx.shape[0], x_lat.shape[0]
    t_all = t_ctx + t_lat
    tm = TM_MIX
    n_ctx_tiles = t_ctx // tm
    kern = functools.partial(_merge_kernel, n_ctx_tiles=n_ctx_tiles, alpha=alpha)
    specs = _token_specs(tm, n_ctx_tiles, lat_len // tm, mod.shape[0] - 1)
    ctx_map, lat_map = _group_maps(n_ctx_tiles)

    def const(shape):
        return pl.BlockSpec(shape, lambda i: (0,) * len(shape))

    in_specs = specs + [
        pl.BlockSpec((tm, GLA_DV), ctx_map),
        pl.BlockSpec((tm, GLA_DV), lat_map),
        pl.BlockSpec((tm, FNET_DIM), ctx_map),
        pl.BlockSpec((tm, FNET_DIM), lat_map),
        pl.BlockSpec((tm, GLA_DV), lambda i: (i, COL_G // GLA_DV)),
        pl.BlockSpec((tm, D_MODEL), lambda i: (i, COL_GATE_A // D_MODEL)),
        pl.BlockSpec((tm, D_MODEL), lambda i: (i, COL_GATE_B // D_MODEL)),
        const((GLA_DV, D_MODEL)), const((FNET_DIM, D_MODEL)), const((D_MODEL, D_MODEL)),
        const((1, D_MODEL)), const((1, D_MODEL)),
        const((D_MODEL, LANES)), const((1, LANES)),
    ]
    out_specs = [pl.BlockSpec((tm, D_MODEL), lambda i: (i, 0)),
                 pl.BlockSpec((tm * ROW_TILES, LANES), lambda i: (i, 0)),
                 pl.BlockSpec((tm, LANES), lambda i: (i, 0)),
                 pl.BlockSpec((tm, LANES), lambda i: (i, 0))]
    out_shape = [jax.ShapeDtypeStruct((t_all, D_MODEL), F32),
                 jax.ShapeDtypeStruct((t_all * ROW_TILES, LANES), ROW_DTYPE),
                 jax.ShapeDtypeStruct((t_all, LANES), jnp.int32),
                 jax.ShapeDtypeStruct((t_all, LANES), F32)]
    return pl.pallas_call(
        kern,
        grid=(t_all // tm,),
        in_specs=in_specs,
        out_specs=out_specs,
        out_shape=out_shape,
        compiler_params=pltpu.CompilerParams(
            dimension_semantics=("arbitrary",), vmem_limit_bytes=VMEM_LIMIT),
        name="merge_ln1_router",
    )(x_ctx, x_lat, pos, mod, o_ctx, o_lat, mixed_ctx, mixed_lat, proj, proj, proj,
      wbg, wbf, wo, l1g, l1b, wr, br)


def _sc_mesh():
    return plsc.VectorSubcoreMesh(core_axis_name="c", subcore_axis_name="s")


def _sc_worker_id():
    return lax.axis_index("s") * SC_CORES + lax.axis_index("c")


def _sc_scatter_rows(src, idx, n_out):
    n_src = src.shape[0]
    w = SC_WINDOW
    n_chunks = n_src // (SC_WORKERS * w)
    copies = idx.shape[0]
    assert n_chunks % 2 == 0 and idx.shape == (copies, SC_WORKERS, n_chunks, w)

    @functools.partial(
        pl.kernel, mesh=_sc_mesh(),
        out_type=jax.ShapeDtypeStruct((n_out, ROW_TILES, LANES), ROW_DTYPE),
        scratch_types=[pltpu.VMEM((copies * n_chunks, w), jnp.int32),
                       pltpu.VMEM((2, w, ROW_TILES, LANES), ROW_DTYPE),
                       pltpu.SemaphoreType.DMA((2,)),
                       pltpu.SemaphoreType.DMA((2,))],
        name="moe_dispatch_scatter")
    def k(src_hbm, idx_hbm, out_hbm, idx_v, rows_v, rsem, wsem):
        wid = _sc_worker_id()
        base = wid * (n_chunks * w)
        for kk in range(copies):
            pltpu.sync_copy(idx_hbm.at[kk, wid], idx_v.at[pl.ds(kk * n_chunks, n_chunks)])

        def read(j, slot):
            return pltpu.make_async_copy(src_hbm.at[pl.ds(base + j * w, w)], rows_v.at[slot],
                                         rsem.at[slot])

        def scatter(j, kk, slot):
            return pltpu.make_async_copy(rows_v.at[slot], out_hbm.at[idx_v.at[kk * n_chunks + j]],
                                         wsem.at[slot])

        read(0, 0).start()

        @pl.loop(0, n_chunks, step=2)
        def _(jj):
            read(jj, 0).wait()

            @pl.when(jj > 0)
            def _():
                for kk in range(copies):
                    scatter(jj - 1, kk, 1).wait()

            read(jj + 1, 1).start()
            for kk in range(copies):
                scatter(jj, kk, 0).start()
            read(jj + 1, 1).wait()
            for kk in range(copies):
                scatter(jj, kk, 0).wait()

            @pl.when(jj + 2 < n_chunks)
            def _():
                read(jj + 2, 0).start()

            for kk in range(copies):
                scatter(jj + 1, kk, 1).start()

        for kk in range(copies):
            scatter(n_chunks - 1, kk, 1).wait()

    return k(src, idx)


def _sc_gather_rows(table, idx):
    _, n_chunks, w = idx.shape
    assert n_chunks % 2 == 0 and idx.shape[0] == SC_WORKERS and w == SC_WINDOW
    n_out = SC_WORKERS * n_chunks * w

    @functools.partial(
        pl.kernel, mesh=_sc_mesh(),
        out_type=jax.ShapeDtypeStruct((n_out, ROW_TILES, LANES), ROW_DTYPE),
        scratch_types=[pltpu.VMEM((n_chunks, w), jnp.int32),
                       pltpu.VMEM((2, w, ROW_TILES, LANES), ROW_DTYPE),
                       pltpu.SemaphoreType.DMA((2,)),
                       pltpu.SemaphoreType.DMA((2,))],
        name="moe_combine_gather")
    def k(table_hbm, idx_hbm, out_hbm, idx_v, rows_v, gsem, wsem):
        wid = _sc_worker_id()
        base = wid * (n_chunks * w)
        pltpu.sync_copy(idx_hbm.at[wid], idx_v)

        def gather(j, slot):
            return pltpu.make_async_copy(table_hbm.at[idx_v.at[j]], rows_v.at[slot], gsem.at[slot])

        def write(j, slot):
            return pltpu.make_async_copy(rows_v.at[slot], out_hbm.at[pl.ds(base + j * w, w)],
                                         wsem.at[slot])

        gather(0, 0).start()

        @pl.loop(0, n_chunks, step=2)
        def _(jj):
            gather(jj, 0).wait()

            @pl.when(jj > 0)
            def _():
                write(jj - 1, 1).wait()

            gather(jj + 1, 1).start()
            write(jj, 0).start()
            gather(jj + 1, 1).wait()
            write(jj, 0).wait()

            @pl.when(jj + 2 < n_chunks)
            def _():
                gather(jj + 2, 0).start()

            write(jj + 1, 1).start()

        write(n_chunks - 1, 1).wait()

    return k(table, idx)


def _moe_kernel(be_ref, nu_ref, nv_ref, slot_ref, nxt_ref, x_ref, wgu_hbm, bgu_ref, wd_hbm, bd_ref,
                o_ref, wgu_st, wd_st, wgu_bf, wd_bf, xb_ref, sem):
    b = pl.program_id(0)
    e = be_ref[b]
    prev = be_ref[jnp.maximum(b - 1, 0)]
    active = b < nu_ref[0]
    changed = (b == 0) | (e != prev)

    def weight_copies(expert, s):
        return (pltpu.make_async_copy(wgu_hbm.at[expert], wgu_st.at[s], sem.at[0, s]),
                pltpu.make_async_copy(wd_hbm.at[expert], wd_st.at[s], sem.at[1, s]))

    @pl.when(active & changed)
    def _():
        s = slot_ref[b]

        @pl.when(b == 0)
        def _():
            for cp in weight_copies(e, s):
                cp.start()

        for cp in weight_copies(e, s):
            cp.wait()
        wgu_bf[...] = wgu_st[s].astype(BF16)
        wd_bf[...] = wd_st[s].astype(BF16)
        nxt = nxt_ref[b]

        @pl.when(nxt >= 0)
        def _():
            for cp in weight_copies(nxt, 1 - s):
                cp.start()

    @pl.when(active)
    def _():
        valid = lax.broadcasted_iota(jnp.int32, (MOE_ROWS, LANES), 0) < nv_ref[b]
        for j in range(ROW_TILES):
            for half, xj in enumerate(_load_row_tile(x_ref, j, MOE_ROWS)):
                c0 = half * HALF_MODEL + j * LANES
                xb_ref[:, c0:c0 + LANES] = jnp.where(valid, xj, 0.0).astype(BF16)
        gu = _dot(xb_ref[...], wgu_bf[...]) + bgu_ref[0]
        gate = jnp.minimum(gu[:, :D_EXPERT], SWIGLU_LIMIT)
        up = jnp.clip(gu[:, D_EXPERT:], -SWIGLU_LIMIT, SWIGLU_LIMIT)
        glu = gate * _sigmoid(gate * SWIGLU_ALPHA)
        act = ((up + 1.0) * glu).astype(BF16)
        _store_row_tiles(o_ref, _dot(act, wd_bf[...]) + bd_ref[0])


def _moe(tables, xs, w_gate_up, b_gate_up, w_down, b_down):
    p_rows = xs.shape[0] // ROW_TILES
    n_blocks = p_rows // MOE_ROWS

    def blk(b, be, nu, *_):
        return jnp.minimum(b, nu[0] - 1)

    def expert(b, be, nu, *_):
        return (be[blk(b, be, nu)], 0, 0)

    def rows(b, be, nu, *_):
        return (blk(b, be, nu), 0)

    grid_spec = pltpu.PrefetchScalarGridSpec(
        num_scalar_prefetch=len(tables),
        grid=(n_blocks,),
        in_specs=[
            pl.BlockSpec((MOE_ROWS * ROW_TILES, LANES), rows),
            pl.BlockSpec(memory_space=pl.ANY),
            pl.BlockSpec((1, 1, 2 * D_EXPERT), expert),
            pl.BlockSpec(memory_space=pl.ANY),
            pl.BlockSpec((1, 1, D_MODEL), expert),
        ],
        out_specs=pl.BlockSpec((MOE_ROWS * ROW_TILES, LANES), rows),
        scratch_shapes=[pltpu.VMEM((2, D_MODEL, 2 * D_EXPERT), F32),
                        pltpu.VMEM((2, D_EXPERT, D_MODEL), F32),
                        pltpu.VMEM((D_MODEL, 2 * D_EXPERT), BF16),
                        pltpu.VMEM((D_EXPERT, D_MODEL), BF16),
                        pltpu.VMEM((MOE_ROWS, D_MODEL), BF16),
                        pltpu.SemaphoreType.DMA((2, 2))],
    )
    return pl.pallas_call(
        _moe_kernel,
        grid_spec=grid_spec,
        out_shape=jax.ShapeDtypeStruct((p_rows * ROW_TILES, LANES), ROW_DTYPE),
        compiler_params=pltpu.CompilerParams(
            dimension_semantics=("arbitrary",), vmem_limit_bytes=VMEM_LIMIT),
        name="moe_grouped_mlp",
    )(*tables, xs, w_gate_up, b_gate_up.reshape(N_EXPERTS, 1, 2 * D_EXPERT), w_down,
      b_down.reshape(N_EXPERTS, 1, D_MODEL))


def _combine_kernel(x1_ref, y0_ref, y1_ref, y2_ref, y3_ref, rw_ref, mod_ref, g_ref, b_ref, o_ref,
                    *, alpha):
    rw = rw_ref[...]
    y_refs = (y0_ref, y1_ref, y2_ref, y3_ref)
    pieces = [None] * (2 * ROW_TILES)
    for j in range(ROW_TILES):
        for kk in range(TOP_K):
            for half, yj in enumerate(_load_row_tile(y_refs[kk], j, rw.shape[0])):
                term = rw[:, kk:kk + 1] * yj
                slot = half * ROW_TILES + j
                pieces[slot] = term if kk == 0 else pieces[slot] + term
    ff = jnp.concatenate(pieces, axis=-1)
    y = alpha * x1_ref[...] + mod_ref[0, 5:6, :] * ff
    o_ref[...] = _layer_norm(y) * g_ref[...] + b_ref[...]


def _combine(x1, yg, rw, mod, l2g, l2b, *, row0, n_rows, mod_map, alpha):
    tm = TM_MIX
    t0 = row0 // tm
    tiles_all = x1.shape[0] // tm
    kern = functools.partial(_combine_kernel, alpha=alpha)

    def y_spec(kk):
        return pl.BlockSpec((tm * ROW_TILES, LANES), lambda i: (kk * tiles_all + t0 + i, 0))

    return pl.pallas_call(
        kern,
        grid=(n_rows // tm,),
        in_specs=[pl.BlockSpec((tm, D_MODEL), lambda i: (t0 + i, 0))]
        + [y_spec(kk) for kk in range(TOP_K)]
        + [pl.BlockSpec((tm, LANES), lambda i: (t0 + i, 0)),
           pl.BlockSpec((1, N_MOD, D_MODEL), mod_map),
           pl.BlockSpec((1, D_MODEL), lambda i: (0, 0)),
           pl.BlockSpec((1, D_MODEL), lambda i: (0, 0))],
        out_specs=pl.BlockSpec((tm, D_MODEL), lambda i: (i, 0)),
        out_shape=jax.ShapeDtypeStruct((n_rows, D_MODEL), F32),
        compiler_params=pltpu.CompilerParams(
            dimension_semantics=("arbitrary",), vmem_limit_bytes=VMEM_LIMIT),
        name="combine_ln2",
    )(x1, yg, yg, yg, yg, rw, mod, l2g, l2b)


def _route_kernel(ridx_ref, dest_ref, cnt_ref, run_ref, bst_ref):
    phase = pl.program_id(0)
    i = pl.program_id(1)
    tm = ridx_ref.shape[0]
    ridx = ridx_ref[...]
    lane = lax.broadcasted_iota(jnp.int32, (tm, LANES), 1)
    hits = [ridx[:, kk:kk + 1] == lane for kk in range(TOP_K)]
    chosen = jnp.where(hits[0], 1.0, 0.0)
    for kk in range(1, TOP_K):
        chosen = chosen + jnp.where(hits[kk], 1.0, 0.0)
    colsum = jnp.sum(chosen, axis=0, keepdims=True)

    @pl.when((phase == 0) & (i == 0))
    def _():
        run_ref[...] = jnp.zeros_like(run_ref)

    @pl.when(phase == 0)
    def _():
        run_ref[...] = run_ref[...] + colsum

    @pl.when((phase == 1) & (i == 0))
    def _():
        counts = run_ref[...]
        cnt_ref[...] = counts
        blocks = jnp.floor((counts + (MOE_ROWS - 1.0)) * (1.0 / MOE_ROWS))
        r = lax.broadcasted_iota(jnp.int32, (LANES, LANES), 0)
        c = lax.broadcasted_iota(jnp.int32, (LANES, LANES), 1)
        before = jnp.dot(blocks, (r < c).astype(F32), precision=HIGHEST, preferred_element_type=F32)
        bst_ref[...] = before * float(MOE_ROWS)
        run_ref[...] = jnp.zeros_like(run_ref)

    @pl.when(phase == 1)
    def _():
        rt = lax.broadcasted_iota(jnp.int32, (tm, tm), 0)
        ct = lax.broadcasted_iota(jnp.int32, (tm, tm), 1)
        earlier = _dot((ct < rt).astype(BF16), chosen.astype(BF16))
        row_of = bst_ref[0:1, :] + run_ref[0:1, :] + earlier
        out = jnp.zeros((tm, LANES), F32)
        for kk in range(TOP_K):
            dk = jnp.sum(jnp.where(hits[kk], row_of, 0.0), axis=-1, keepdims=True)
            out = jnp.where(lane == kk, dk, out)
        dest_ref[...] = out.T[0:SUBLANES, :].astype(jnp.int32)
        run_ref[...] = run_ref[...] + colsum


def _route(ridx):
    t_all = ridx.shape[0]
    tm = TM_ROUTE
    return pl.pallas_call(
        _route_kernel,
        grid=(2, t_all // tm),
        in_specs=[pl.BlockSpec((tm, LANES), lambda p, i: (i, 0))],
        out_specs=[pl.BlockSpec((SUBLANES, tm), lambda p, i: (0, i * p)),
                   pl.BlockSpec((SUBLANES, LANES), lambda p, i: (0, 0))],
        out_shape=[jax.ShapeDtypeStruct((SUBLANES, t_all), jnp.int32),
                   jax.ShapeDtypeStruct((SUBLANES, LANES), F32)],
        scratch_shapes=[pltpu.VMEM((SUBLANES, LANES), F32),
                        pltpu.VMEM((SUBLANES, LANES), F32)],
        compiler_params=pltpu.CompilerParams(
            dimension_semantics=("arbitrary", "arbitrary"), vmem_limit_bytes=VMEM_LIMIT),
        name="moe_route",
    )(ridx)


def _routing_tables(counts, n_blocks):
    experts = jnp.arange(N_EXPERTS, dtype=jnp.int32)
    blocks_per = (counts + MOE_ROWS - 1) // MOE_ROWS
    bends = jnp.cumsum(blocks_per)
    bstarts = bends - blocks_per
    blocks = jnp.arange(n_blocks, dtype=jnp.int32)
    block_expert = jnp.minimum(
        jnp.sum((bends[None, :] <= blocks[:, None]).astype(jnp.int32), axis=1), N_EXPERTS - 1)
    n_used = bends[-1:].astype(jnp.int32)
    owner = block_expert[:, None] == experts[None, :]

    def per_block(table):
        return jnp.sum(jnp.where(owner, table[None, :], 0), axis=1)

    n_valid = jnp.clip(per_block(counts) - (blocks - per_block(bstarts)) * MOE_ROWS,
                       0, MOE_ROWS).astype(jnp.int32)
    present = blocks_per > 0
    ordinal = jnp.cumsum(present.astype(jnp.int32)) - 1
    later = lax.cummin(jnp.where(present, experts, N_EXPERTS), reverse=True)
    succ = jnp.concatenate([later[1:], jnp.full((1,), N_EXPERTS, jnp.int32)])
    succ = jnp.where(succ >= N_EXPERTS, -1, succ)
    stage_slot = (per_block(ordinal) % 2).astype(jnp.int32)
    next_expert = per_block(succ).astype(jnp.int32)
    return (block_expert.astype(jnp.int32), n_used, n_valid, stage_slot, next_expert)


def _pos_embed_2d(n_tokens):
    rows = n_tokens // GRID_W
    r = jnp.repeat(jnp.arange(rows), GRID_W).astype(F32)
    col = jnp.tile(jnp.arange(GRID_W), rows).astype(F32)
    quarter = D_MODEL // 4
    omega = 1.0 / (10000.0 ** (jnp.arange(quarter, dtype=F32) / quarter))
    er = r[:, None] * omega
    ec = col[:, None] * omega
    return jnp.concatenate([jnp.sin(er), jnp.cos(er), jnp.sin(ec), jnp.cos(ec)], axis=-1)


def _reorder_w_in(w):
    o_r = 2 * GLA_DK + 2 * GLA_DV
    o_f = o_r + DECAY_RANK
    o_gate = o_f + FNET_DIM
    pad = jnp.zeros((w.shape[0], LANES - DECAY_RANK), w.dtype)
    w_q = w[:, :GLA_DK] * (DK_HEAD ** -0.5)
    return jnp.concatenate([w_q, w[:, GLA_DK:o_r], w[:, o_gate:], w[:, o_f:o_gate], w[:, o_r:o_f], pad],
                           axis=1)


def kernel(x_prompt, x_sample, state_gla, c, c_ctx, w_ada, b_ada, w_in, w_dec_fwd, b_dec_fwd,
           w_dec_bwd, b_dec_bwd, gla_norm_g, w_br_gla, w_br_fnet, w_out, ln1_g, ln1_b, w_router,
           b_router, w_gate_up, b_gate_up, w_down, b_down, ln2_g, ln2_b):
    n_req, ctx_len, _ = x_prompt.shape
    n_lat, lat_len, _ = x_sample.shape
    depth = w_in.shape[0]
    alpha = (2.0 * depth) ** 0.25
    t_ctx = n_req * ctx_len
    t_lat = n_lat * lat_len
    t_all = t_ctx + t_lat

    x_ctx = x_prompt.reshape(t_ctx, D_MODEL)
    x_lat = x_sample.reshape(t_lat, D_MODEL)
    pos = _pos_embed_2d(lat_len)
    zero_pos = jnp.zeros_like(pos)

    cond_rows = -(-(n_lat + 1) // SUBLANES) * SUBLANES
    cond = jnp.zeros((cond_rows, D_MODEL), F32).at[:n_lat].set(c).at[cond_rows - 1].set(c_ctx)

    n_moe_blocks = (t_all * TOP_K) // MOE_ROWS + N_EXPERTS
    tok_chunks = t_all // (SC_WORKERS * SC_WINDOW)
    states = []
    for l in range(depth):
        mod = _ada(cond, w_ada[l], b_ada[l]).reshape(cond_rows, N_MOD, D_MODEL)
        layer_pos = pos if l == 0 else zero_pos
        proj = _inproj(x_ctx, x_lat, layer_pos, mod, _reorder_w_in(w_in[l]).astype(BF16), lat_len)

        def pad_dec(w):
            return jnp.zeros((LANES, GLA_DK), F32).at[:DECAY_RANK].set(w)

        dec = (pad_dec(w_dec_fwd[l]), b_dec_fwd[l].reshape(1, GLA_DK),
               pad_dec(w_dec_bwd[l]), b_dec_bwd[l].reshape(1, GLA_DK),
               gla_norm_g[l].reshape(1, DV_HEAD))
        o_ctx, s_new = _gla(proj, *dec, None, n_seq=n_req, seq_len=ctx_len, row0=0, emit_state=True)
        (o_lat,) = _gla(proj, *dec, state_gla[:, l], n_seq=n_lat, seq_len=lat_len, row0=t_ctx,
                        emit_state=False)
        states.append(s_new)

        mixed_ctx = _fnet(proj, n_seq=n_req, seq_len=ctx_len, row0=0)
        mixed_lat = _fnet(proj, n_seq=n_lat, seq_len=lat_len, row0=t_ctx)

        wr = jnp.zeros((D_MODEL, LANES), F32).at[:, :N_EXPERTS].set(w_router[l])
        br = jnp.full((1, LANES), -1e30, F32).at[0, :N_EXPERTS].set(b_router[l])
        x1, h2, ridx, rw = _merge(
            x_ctx, x_lat, layer_pos, mod, o_ctx, o_lat, mixed_ctx, mixed_lat, proj,
            w_br_gla[l].astype(BF16), w_br_fnet[l].astype(BF16), w_out[l].astype(BF16),
            ln1_g[l].reshape(1, D_MODEL), ln1_b[l].reshape(1, D_MODEL), wr, br, lat_len, alpha)

        dest, counts = _route(ridx)
        moe_tables = _routing_tables(counts[0, :N_EXPERTS].astype(jnp.int32), n_moe_blocks)
        dest = dest[:TOP_K]
        scatter_idx = dest.reshape(TOP_K, SC_WORKERS, tok_chunks, SC_WINDOW)
        p_rows = n_moe_blocks * MOE_ROWS
        xs = _sc_scatter_rows(h2.reshape(t_all, ROW_TILES, LANES), scatter_idx, p_rows)
        yb = _moe(moe_tables, xs.reshape(p_rows * ROW_TILES, LANES),
                  w_gate_up[l], b_gate_up[l], w_down[l], b_down[l])
        gather_idx = dest.reshape(SC_WORKERS, TOP_K * tok_chunks, SC_WINDOW)
        yg = _sc_gather_rows(yb.reshape(p_rows, ROW_TILES, LANES), gather_idx)
        yg = yg.reshape(TOP_K * t_all * ROW_TILES, LANES)

        l2g = ln2_g[l].reshape(1, D_MODEL)
        l2b = ln2_b[l].reshape(1, D_MODEL)
        tiles_per_seq = lat_len // TM_MIX
        x_ctx = _combine(x1, yg, rw, mod, l2g, l2b, row0=0, n_rows=t_ctx,
                         mod_map=lambda i: (cond_rows - 1, 0, 0), alpha=alpha)
        x_lat = _combine(x1, yg, rw, mod, l2g, l2b, row0=t_ctx, n_rows=t_lat,
                         mod_map=lambda i: (i // tiles_per_seq, 0, 0), alpha=alpha)

    y_prompt = x_ctx.reshape(x_prompt.shape)
    y_sample = x_lat.reshape(x_sample.shape)
    new_state = jnp.stack(states, axis=1).astype(x_prompt.dtype)
    return (y_prompt, y_sample, new_state)
```

```python
import functools
import math

import numpy as np
import jax
import jax.numpy as jnp
from jax import lax
from jax.experimental import pallas as pl
from jax.experimental.pallas import tpu as pltpu
from jax.experimental.pallas import tpu_sc as plsc

F32 = jnp.float32
BF16 = jnp.bfloat16

D_MODEL = 1024
GRID_W = 64
GLA_HEADS = 4
DK_HEAD = 128
DV_HEAD = 256
GLA_DK = GLA_HEADS * DK_HEAD
GLA_DV = GLA_HEADS * DV_HEAD
DECAY_RANK = 16
GATE_NORMALIZER = 16.0
FNET_GROUPS = 4
FNET_GROUP_DIM = 128
FNET_DIM = FNET_GROUPS * FNET_GROUP_DIM
N_EXPERTS = 32
TOP_K = 4
D_EXPERT = 1024
SWIGLU_LIMIT = 7.0
SWIGLU_ALPHA = 1.702
LN_EPS = 1e-6
N_MOD = 6

LANES = 128
SUBLANES = 8
HALF_MODEL = D_MODEL // 2
ROW_TILES = HALF_MODEL // LANES
ROW_DTYPE = jnp.uint32
COL_Q = 0
COL_K = GLA_DK
COL_V = 2 * GLA_DK
COL_G = COL_V + GLA_DV
COL_GATE_A = COL_G + GLA_DV
COL_GATE_B = COL_GATE_A + D_MODEL
COL_F = COL_GATE_B + D_MODEL
COL_R = COL_F + FNET_DIM
PROJ_COLS = COL_R + LANES

GLA_CHUNK = 128
GLA_LEAF = 16
TM_PROJ = 256
TM_MIX = 256
TM_MERGE = 512
TM_MERGE_GROUP = 128
TM_ROUTE = 1024
MOE_ROWS = 256
VMEM_LIMIT = 56 * 1024 * 1024

SC_CORES = 2
SC_SUBCORES = 16
SC_WORKERS = SC_CORES * SC_SUBCORES
SC_WINDOW = 32

HIGHEST = lax.Precision.HIGHEST


def _layer_norm(x):
    mu = jnp.mean(x, axis=-1, keepdims=True)
    xc = x - mu
    var = jnp.mean(xc * xc, axis=-1, keepdims=True)
    return xc * lax.rsqrt(var + LN_EPS)


def _sigmoid(x):
    return 0.5 * jnp.tanh(0.5 * x) + 0.5


def _log_sigmoid(z):
    return jnp.minimum(z, 0.0) - jnp.log1p(jnp.exp(-jnp.abs(z)))


def _dot(a, b):
    return jnp.dot(a, b, preferred_element_type=F32)


def _split_bf16(x, terms):
    parts = []
    for _ in range(terms):
        p = x.astype(BF16)
        parts.append(p)
        x = x - p.astype(F32)
    return parts


def _dot_nt(a, b):
    return lax.dot_general(a, b, (((1,), (1,)), ((), ())), preferred_element_type=F32)


def _dot_tn(a, b):
    return lax.dot_general(a, b, (((0,), (0,)), ((), ())), preferred_element_type=F32)


def _row_tile_slice(j, n_rows, first_row=0):
    return pl.ds(first_row * ROW_TILES + j, n_rows, stride=ROW_TILES)


def _store_row_tiles(ref, val, first_row=0):
    for j in range(ROW_TILES):
        lo = val[:, j * LANES:(j + 1) * LANES]
        hi = val[:, HALF_MODEL + j * LANES:HALF_MODEL + (j + 1) * LANES]
        ref[_row_tile_slice(j, val.shape[0], first_row), :] = pltpu.pack_elementwise(
            [lo, hi], packed_dtype=BF16)


def _load_row_tile(ref, j, n_rows):
    words = ref[_row_tile_slice(j, n_rows), :]
    return tuple(pltpu.unpack_elementwise(words, index=half, packed_dtype=BF16, unpacked_dtype=F32)
                 for half in range(2))


def _ada_kernel(c_ref, w_ref, b_ref, o_ref):
    c = c_ref[...]
    s = c * _sigmoid(c)
    o_ref[...] = _dot(s.astype(BF16), w_ref[...].astype(BF16)) + b_ref[...]


def _ada(cond, w_ada, b_ada):
    rows = cond.shape[0]
    n = w_ada.shape[1]
    tn = 1536
    return pl.pallas_call(
        _ada_kernel,
        grid=(n // tn,),
        in_specs=[pl.BlockSpec((rows, D_MODEL), lambda j: (0, 0)),
                  pl.BlockSpec((D_MODEL, tn), lambda j: (0, j)),
                  pl.BlockSpec((1, tn), lambda j: (0, j))],
        out_specs=pl.BlockSpec((rows, tn), lambda j: (0, j)),
        out_shape=jax.ShapeDtypeStruct((rows, n), F32),
        compiler_params=pltpu.CompilerParams(vmem_limit_bytes=VMEM_LIMIT),
        name="ada_mod",
    )(cond, w_ada, b_ada.reshape(1, n))


def _group_maps(n_ctx_tiles):
    def ctx_map(i, *_):
        return (jnp.minimum(i, n_ctx_tiles - 1), 0)

    def lat_map(i, *_):
        return (jnp.maximum(i - n_ctx_tiles, 0), 0)

    return ctx_map, lat_map


def _token_specs(tm, n_ctx_tiles, tiles_per_latent_seq, ctx_mod_row):
    ctx_map, lat_map = _group_maps(n_ctx_tiles)

    def pos_map(i, *_):
        return (jnp.maximum(i - n_ctx_tiles, 0) % tiles_per_latent_seq, 0)

    def mod_map(i, *_):
        return (jnp.where(i < n_ctx_tiles, ctx_mod_row,
                          jnp.maximum(i - n_ctx_tiles, 0) // tiles_per_latent_seq), 0, 0)

    return [pl.BlockSpec((tm, D_MODEL), ctx_map),
            pl.BlockSpec((tm, D_MODEL), lat_map),
            pl.BlockSpec((tm, D_MODEL), pos_map),
            pl.BlockSpec((1, N_MOD, D_MODEL), mod_map)]


def _inproj_kernel(xc_ref, xl_ref, pos_ref, mod_ref, w_ref, o_ref, *, n_ctx_tiles):
    i = pl.program_id(0)

    def project(x):
        h = _layer_norm(x) * (1.0 + mod_ref[0, 1:2, :]) + mod_ref[0, 0:1, :]
        o_ref[...] = _dot(h.astype(BF16), w_ref[...])

    @pl.when(i < n_ctx_tiles)
    def _():
        project(xc_ref[...])

    @pl.when(i >= n_ctx_tiles)
    def _():
        project(xl_ref[...] + pos_ref[...])


def _inproj(x_ctx, x_lat, pos, mod, w_in_bf, lat_len):
    t_ctx, t_lat = x_ctx.shape[0], x_lat.shape[0]
    n_ctx_tiles = t_ctx // TM_PROJ
    n_tiles = (t_ctx + t_lat) // TM_PROJ
    kern = functools.partial(_inproj_kernel, n_ctx_tiles=n_ctx_tiles)
    specs = _token_specs(TM_PROJ, n_ctx_tiles, lat_len // TM_PROJ, mod.shape[0] - 1)
    return pl.pallas_call(
        kern,
        grid=(n_tiles,),
        in_specs=specs + [pl.BlockSpec((D_MODEL, PROJ_COLS), lambda i: (0, 0),
                                       pipeline_mode=pl.Buffered(1))],
        out_specs=pl.BlockSpec((TM_PROJ, PROJ_COLS), lambda i: (i, 0)),
        out_shape=jax.ShapeDtypeStruct((t_ctx + t_lat, PROJ_COLS), F32),
        compiler_params=pltpu.CompilerParams(
            dimension_semantics=("arbitrary",), vmem_limit_bytes=VMEM_LIMIT),
        name="ln_inproj",
    )(x_ctx, x_lat, pos, mod, w_in_bf)


def _gla_kernel(*refs, seq_len, has_s0, emit_state):
    it = iter(refs)
    q_ref, k_ref, v_ref, r_ref = next(it), next(it), next(it), next(it)
    wdf_ref, bdf_ref, wdb_ref, bdb_ref, g_ref = next(it), next(it), next(it), next(it), next(it)
    s0_ref = next(it) if has_s0 else None
    o_ref = next(it)
    sout_ref = next(it) if emit_state else None
    cum_ref, a_ref, qi_ref, ko_ref, dec_ref, op_ref, st_ref = (next(it) for _ in range(7))

    C = GLA_CHUNK
    n_chunks = seq_len // C

    def rows(n):
        if isinstance(n, int):
            return pl.ds(n * C, C)
        return pl.ds(pl.multiple_of(n * C, C), C)

    def dec_rows(n):
        if isinstance(n, int):
            return pl.ds(n * SUBLANES, SUBLANES)
        return pl.ds(pl.multiple_of(n * SUBLANES, SUBLANES), SUBLANES)

    def loop(body):
        if n_chunks <= 2:
            for n in range(n_chunks):
                body(n)
        else:
            def step(m, carry):
                body(2 * m)
                body(2 * m + 1)
                return carry
            lax.fori_loop(0, n_chunks // 2, step, 0)

    rt = lax.broadcasted_iota(jnp.int32, (C, C), 0)
    ct = lax.broadcasted_iota(jnp.int32, (C, C), 1)
    tri = ((rt >= ct).astype(BF16), (ct >= rt).astype(BF16))
    row_id = lax.broadcasted_iota(jnp.int32, (C, DK_HEAD), 0)

    r_hi, r_lo = _split_bf16(r_ref[...], 2)
    for d, (w_ref, b_ref) in enumerate(((wdf_ref, bdf_ref), (wdb_ref, bdb_ref))):
        w_hi, w_lo = _split_bf16(w_ref[...], 2)
        z = _dot(r_hi, w_hi) + _dot(r_lo, w_hi) + _dot(r_hi, w_lo) + b_ref[...]
        cum_ref[d] = _log_sigmoid(z) * (1.0 / GATE_NORMALIZER)

    def cumsum_chunk(n):
        for d in range(2):
            la_hi, la_lo = _split_bf16(cum_ref[d, rows(n), :], 2)
            cum_ref[d, rows(n), :] = _dot(tri[d], la_hi) + _dot(tri[d], la_lo)

    loop(cumsum_chunk)

    query_rows = ({}, {})
    keep = ({}, {})
    for d in range(2):
        blk = C // 2
        while blk >= GLA_LEAF:
            q_parity = 1 if d == 0 else 0
            query_rows[d][blk] = ((row_id // blk) % 2) == q_parity
            qb, kb = rt // blk, ct // blk
            keep[d][blk] = ((qb % 2) == q_parity) & ((qb == kb + 1) if d == 0 else (kb == qb + 1))
            blk //= 2
        order = (rt >= ct) if d == 0 else (ct >= rt)
        keep[d][0] = ((rt // GLA_LEAF) == (ct // GLA_LEAF)) & order

    def block_rows(cum, first, step, count):
        span = C // count
        parts = [jnp.broadcast_to(cum[first + p * step:first + p * step + 1, :], (span, DK_HEAD))
                 for p in range(count)]
        return parts[0] if count == 1 else jnp.concatenate(parts, axis=0)

    def scores(n, d):
        cum = cum_ref[d, rows(n), :]
        q = q_ref[rows(n), :]
        k = k_ref[rows(n), :]
        acc = None
        blk = C // 2
        while blk >= GLA_LEAF:
            pairs = C // (2 * blk)
            bnd = blk - 1 if d == 0 else blk
            w = jnp.exp(-jnp.abs(cum - block_rows(cum, bnd, 2 * blk, pairs)))
            x = (jnp.where(query_rows[d][blk], q, k) * w).astype(BF16)
            s = jnp.where(keep[d][blk], _dot_nt(x, x), 0.0)
            acc = s if acc is None else acc + s
            blk //= 2
        mid = GLA_LEAF // 2 - 1 if d == 0 else GLA_LEAF // 2
        e = cum - block_rows(cum, mid, GLA_LEAF, C // GLA_LEAF)
        s = _dot_nt((q * jnp.exp(e)).astype(BF16), (k * jnp.exp(-e)).astype(BF16))
        acc = acc + jnp.where(keep[d][0], s, 0.0)
        a_ref[d, rows(n), :] = acc.astype(BF16)
        end = cum[C - 1:C, :] if d == 0 else cum[0:1, :]
        qi_ref[d, rows(n), :] = (q * jnp.exp(cum)).astype(BF16)
        ko_ref[d, rows(n), :] = (k * jnp.exp(end - cum)).astype(BF16)
        dec_ref[d, dec_rows(n), :] = jnp.broadcast_to(jnp.exp(end), (SUBLANES, DK_HEAD))

    def scores_chunk(n):
        scores(n, 0)
        scores(n, 1)

    loop(scores_chunk)

    for d in range(2):
        if has_s0:
            st_ref[d] = s0_ref[0, d, 0].T
        else:
            st_ref[d] = jnp.zeros((DV_HEAD, DK_HEAD), F32)

    def scan(n, d):
        v = v_ref[rows(n), :].astype(BF16)
        st = st_ref[d]
        o = _dot(a_ref[d, rows(n), :], v) + _dot_nt(qi_ref[d, rows(n), :], st.astype(BF16))
        st_ref[d] = st * dec_ref[d, dec_rows(n), :][0:1, :] + _dot_tn(v, ko_ref[d, rows(n), :])
        op_ref[d, rows(n), :] = o

    def scan_chunk(m):
        scan(m, 0)
        scan(n_chunks - 1 - m, 1)

    loop(scan_chunk)
    if emit_state:
        sout_ref[0, 0, 0] = st_ref[0].T
        sout_ref[0, 1, 0] = st_ref[1].T

    g = g_ref[...]

    def finish_chunk(n):
        o = op_ref[0, rows(n), :] + op_ref[1, rows(n), :]
        ms = jnp.mean(o * o, axis=-1, keepdims=True)
        o_ref[rows(n), :] = o * lax.rsqrt(ms + LN_EPS) * g

    loop(finish_chunk)


def _gla(proj, wdf, bdf, wdb, bdb, g, s0, *, n_seq, seq_len, row0, emit_state):
    has_s0 = s0 is not None
    blk0 = row0 // seq_len
    kern = functools.partial(_gla_kernel, seq_len=seq_len, has_s0=has_s0, emit_state=emit_state)
    in_specs = [
        pl.BlockSpec((seq_len, DK_HEAD), lambda b, h: (blk0 + b, COL_Q // DK_HEAD + h)),
        pl.BlockSpec((seq_len, DK_HEAD), lambda b, h: (blk0 + b, COL_K // DK_HEAD + h)),
        pl.BlockSpec((seq_len, DV_HEAD), lambda b, h: (blk0 + b, COL_V // DV_HEAD + h)),
        pl.BlockSpec((seq_len, LANES), lambda b, h: (blk0 + b, COL_R // LANES)),
        pl.BlockSpec((LANES, DK_HEAD), lambda b, h: (0, h)),
        pl.BlockSpec((1, DK_HEAD), lambda b, h: (0, h)),
        pl.BlockSpec((LANES, DK_HEAD), lambda b, h: (0, h)),
        pl.BlockSpec((1, DK_HEAD), lambda b, h: (0, h)),
        pl.BlockSpec((1, DV_HEAD), lambda b, h: (0, 0)),
    ]
    args = [proj, proj, proj, proj, wdf, bdf, wdb, bdb, g]
    if has_s0:
        in_specs.append(pl.BlockSpec((1, 2, 1, DK_HEAD, DV_HEAD), lambda b, h: (b, 0, h, 0, 0)))
        args.append(s0)
    out_specs = [pl.BlockSpec((seq_len, DV_HEAD), lambda b, h: (b, h))]
    out_shape = [jax.ShapeDtypeStruct((n_seq * seq_len, GLA_DV), F32)]
    if emit_state:
        out_specs.append(pl.BlockSpec((1, 2, 1, DK_HEAD, DV_HEAD), lambda b, h: (b, 0, h, 0, 0)))
        out_shape.append(jax.ShapeDtypeStruct((n_seq, 2, GLA_HEADS, DK_HEAD, DV_HEAD), F32))

    res = pl.pallas_call(
        kern,
        grid=(n_seq, GLA_HEADS),
        in_specs=in_specs,
        out_specs=out_specs,
        out_shape=out_shape,
        scratch_shapes=[pltpu.VMEM((2, seq_len, DK_HEAD), F32),
                        pltpu.VMEM((2, seq_len, GLA_CHUNK), BF16),
                        pltpu.VMEM((2, seq_len, DK_HEAD), BF16),
                        pltpu.VMEM((2, seq_len, DK_HEAD), BF16),
                        pltpu.VMEM((2, seq_len // GLA_CHUNK * SUBLANES, DK_HEAD), F32),
                        pltpu.VMEM((2, seq_len, DV_HEAD), F32),
                        pltpu.VMEM((2, DV_HEAD, DK_HEAD), F32)],
        compiler_params=pltpu.CompilerParams(
            dimension_semantics=("arbitrary", "arbitrary"), vmem_limit_bytes=VMEM_LIMIT),
        name="gla_seq%d" % seq_len,
    )(*args)
    return res


def _fnet_kernel(f_ref, cl_ref, sl_ref, cg_ref, sg_ref, o_ref, uc_ref, us_ref, *, seq_len):
    cg = cg_ref[...]
    sg = sg_ref[...]
    for grp in range(FNET_GROUPS):
        lo = grp * FNET_GROUP_DIM
        u = f_ref[:, lo:lo + FNET_GROUP_DIM].astype(BF16)
        uc_ref[:, lo:lo + FNET_GROUP_DIM] = _dot(u, cg).astype(BF16)
        us_ref[:, lo:lo + FNET_GROUP_DIM] = _dot(u, sg).astype(BF16)
    mixed = _dot(cl_ref[...], uc_ref[...]) - _dot(sl_ref[...], us_ref[...])
    o_ref[...] = mixed * (1.0 / math.sqrt(seq_len * FNET_GROUP_DIM))


def _dft_mats(n):
    j = np.arange(n, dtype=np.int64)
    ang = (2.0 * np.pi / n) * ((j[:, None] * j[None, :]) % n).astype(np.float64)
    return (jnp.asarray(np.cos(ang), dtype=F32).astype(BF16),
            jnp.asarray(np.sin(ang), dtype=F32).astype(BF16))


def _fnet(proj, *, n_seq, seq_len, row0):
    blk0 = row0 // seq_len
    cl, sl = _dft_mats(seq_len)
    cg, sg = _dft_mats(FNET_GROUP_DIM)
    kern = functools.partial(_fnet_kernel, seq_len=seq_len)
    return pl.pallas_call(
        kern,
        grid=(n_seq,),
        in_specs=[pl.BlockSpec((seq_len, FNET_DIM), lambda b: (blk0 + b, COL_F // FNET_DIM)),
                  pl.BlockSpec((seq_len, seq_len), lambda b: (0, 0)),
                  pl.BlockSpec((seq_len, seq_len), lambda b: (0, 0)),
                  pl.BlockSpec((FNET_GROUP_DIM, FNET_GROUP_DIM), lambda b: (0, 0)),
                  pl.BlockSpec((FNET_GROUP_DIM, FNET_GROUP_DIM), lambda b: (0, 0))],
        out_specs=pl.BlockSpec((seq_len, FNET_DIM), lambda b: (b, 0)),
        out_shape=jax.ShapeDtypeStruct((n_seq * seq_len, FNET_DIM), F32),
        scratch_shapes=[pltpu.VMEM((seq_len, FNET_DIM), BF16),
                        pltpu.VMEM((seq_len, FNET_DIM), BF16)],
        compiler_params=pltpu.CompilerParams(
            dimension_semantics=("arbitrary",), vmem_limit_bytes=VMEM_LIMIT),
        name="fnet_seq%d" % seq_len,
    )(proj, cl, sl, cg, sg)


def _merge_kernel(xc_ref, xl_ref, pos_ref, mod_ref, oc_ref, ol_ref, mc_ref, ml_ref,
                  g_ref, ga_ref, gb_ref, wbg_ref, wbf_ref, wo_ref, l1g_ref, l1b_ref, wrh_ref, wrl_ref,
                  br_ref, x1_ref, h2_ref, ridx_ref, rw_ref, *, n_ctx_tiles, alpha):
    i = pl.program_id(0)

    def compute(rows, x, o, mx):
        g = g_ref[rows, :]
        a = (o * (g * _sigmoid(g))).astype(BF16)
        gla_out = _dot(a, wbg_ref[...])
        fnet_out = _dot(mx.astype(BF16), wbf_ref[...])
        merged = _sigmoid(ga_ref[rows, :]) * gla_out + _sigmoid(gb_ref[rows, :]) * fnet_out
        mix = _dot(merged.astype(BF16), wo_ref[...])
        y = alpha * x + mod_ref[0, 2:3, :] * mix
        x1 = _layer_norm(y) * l1g_ref[...] + l1b_ref[...]
        x1_ref[rows, :] = x1
        h2 = _layer_norm(x1) * (1.0 + mod_ref[0, 4:5, :]) + mod_ref[0, 3:4, :]
        _store_row_tiles(h2_ref, h2, rows.start)

        h_hi, h_lo = _split_bf16(h2, 2)
        logits = (_dot(h_hi, wrh_ref[...]) + _dot(h_lo, wrh_ref[...]) + _dot(h_hi, wrl_ref[...])
                  + br_ref[...])
        lane_i = lax.broadcasted_iota(jnp.int32, logits.shape, 1)
        lane = lane_i.astype(F32)
        idx_out = jnp.zeros(logits.shape, F32)
        val_out = jnp.zeros(logits.shape, F32)
        top0 = None
        denom = None
        for kk in range(TOP_K):
            m = jnp.max(logits, axis=-1, keepdims=True)
            sel = jnp.min(jnp.where(logits == m, lane, float(LANES)), axis=-1, keepdims=True)
            if kk == 0:
                top0 = m
                p = jnp.ones_like(m)
                denom = p
            else:
                p = jnp.exp(m - top0)
                denom = denom + p
            idx_out = jnp.where(lane_i == kk, sel, idx_out)
            val_out = jnp.where(lane_i == kk, p, val_out)
            logits = jnp.where(lane == sel, -jnp.inf, logits)
        ridx_ref[rows, :] = idx_out.astype(jnp.int32)
        rw_ref[rows, :] = val_out / denom

    tm = x1_ref.shape[0]
    groups = [slice(r0, r0 + TM_MERGE_GROUP) for r0 in range(0, tm, TM_MERGE_GROUP)]

    @pl.when(i < n_ctx_tiles)
    def _():
        for rows in groups:
            compute(rows, xc_ref[rows, :], oc_ref[rows, :], mc_ref[rows, :])

    @pl.when(i >= n_ctx_tiles)
    def _():
        for rows in groups:
            compute(rows, xl_ref[rows, :] + pos_ref[rows, :], ol_ref[rows, :], ml_ref[rows, :])


def _merge(x_ctx, x_lat, pos, mod, o_ctx, o_lat, mixed_ctx, mixed_lat, proj,
           wbg, wbf, wo, l1g, l1b, wr_hi, wr_lo, br, lat_len, alpha):
    t_ctx, t_lat = x_ctx.shape[0], x_lat.shape[0]
    t_all = t_ctx + t_lat
    tm = TM_MERGE
    n_ctx_tiles = t_ctx // tm
    kern = functools.partial(_merge_kernel, n_ctx_tiles=n_ctx_tiles, alpha=alpha)
    specs = _token_specs(tm, n_ctx_tiles, lat_len // tm, mod.shape[0] - 1)
    ctx_map, lat_map = _group_maps(n_ctx_tiles)

    def const(shape):
        return pl.BlockSpec(shape, lambda i: (0,) * len(shape))

    in_specs = specs + [
        pl.BlockSpec((tm, GLA_DV), ctx_map),
        pl.BlockSpec((tm, GLA_DV), lat_map),
        pl.BlockSpec((tm, FNET_DIM), ctx_map),
        pl.BlockSpec((tm, FNET_DIM), lat_map),
        pl.BlockSpec((tm, GLA_DV), lambda i: (i, COL_G // GLA_DV)),
        pl.BlockSpec((tm, D_MODEL), lambda i: (i, COL_GATE_A // D_MODEL)),
        pl.BlockSpec((tm, D_MODEL), lambda i: (i, COL_GATE_B // D_MODEL)),
        const((GLA_DV, D_MODEL)), const((FNET_DIM, D_MODEL)), const((D_MODEL, D_MODEL)),
        const((1, D_MODEL)), const((1, D_MODEL)),
        const((D_MODEL, LANES)), const((D_MODEL, LANES)), const((1, LANES)),
    ]
    out_specs = [pl.BlockSpec((tm, D_MODEL), lambda i: (i, 0)),
                 pl.BlockSpec((tm * ROW_TILES, LANES), lambda i: (i, 0)),
                 pl.BlockSpec((tm, LANES), lambda i: (i, 0)),
                 pl.BlockSpec((tm, LANES), lambda i: (i, 0))]
    out_shape = [jax.ShapeDtypeStruct((t_all, D_MODEL), F32),
                 jax.ShapeDtypeStruct((t_all * ROW_TILES, LANES), ROW_DTYPE),
                 jax.ShapeDtypeStruct((t_all, LANES), jnp.int32),
                 jax.ShapeDtypeStruct((t_all, LANES), F32)]
    return pl.pallas_call(
        kern,
        grid=(t_all // tm,),
        in_specs=in_specs,
        out_specs=out_specs,
        out_shape=out_shape,
        compiler_params=pltpu.CompilerParams(
            dimension_semantics=("arbitrary",), vmem_limit_bytes=VMEM_LIMIT),
        name="merge_ln1_router",
    )(x_ctx, x_lat, pos, mod, o_ctx, o_lat, mixed_ctx, mixed_lat, proj, proj, proj,
      wbg, wbf, wo, l1g, l1b, wr_hi, wr_lo, br)


def _sc_mesh():
    return plsc.VectorSubcoreMesh(core_axis_name="c", subcore_axis_name="s")


def _sc_worker_id():
    return lax.axis_index("s") * SC_CORES + lax.axis_index("c")


def _sc_scatter_rows(src, idx, n_out):
    n_src = src.shape[0]
    w = SC_WINDOW
    n_chunks = n_src // (SC_WORKERS * w)
    copies = idx.shape[0]
    assert n_chunks % 2 == 0 and idx.shape == (copies, SC_WORKERS, n_chunks, w)

    @functools.partial(
        pl.kernel, mesh=_sc_mesh(),
        out_type=jax.ShapeDtypeStruct((n_out, ROW_TILES, LANES), ROW_DTYPE),
        scratch_types=[pltpu.VMEM((copies * n_chunks, w), jnp.int32),
                       pltpu.VMEM((2, w, ROW_TILES, LANES), ROW_DTYPE),
                       pltpu.SemaphoreType.DMA((2,)),
                       pltpu.SemaphoreType.DMA((2,))],
        name="moe_dispatch_scatter")
    def k(src_hbm, idx_hbm, out_hbm, idx_v, rows_v, rsem, wsem):
        wid = _sc_worker_id()
        base = wid * (n_chunks * w)
        for kk in range(copies):
            pltpu.sync_copy(idx_hbm.at[kk, wid], idx_v.at[pl.ds(kk * n_chunks, n_chunks)])

        def read(j, slot):
            return pltpu.make_async_copy(src_hbm.at[pl.ds(base + j * w, w)], rows_v.at[slot],
                                         rsem.at[slot])

        def scatter(j, kk, slot):
            return pltpu.make_async_copy(rows_v.at[slot], out_hbm.at[idx_v.at[kk * n_chunks + j]],
                                         wsem.at[slot])

        read(0, 0).start()

        @pl.loop(0, n_chunks, step=2)
        def _(jj):
            read(jj, 0).wait()

            @pl.when(jj > 0)
            def _():
                for kk in range(copies):
                    scatter(jj - 1, kk, 1).wait()

            read(jj + 1, 1).start()
            for kk in range(copies):
                scatter(jj, kk, 0).start()
            read(jj + 1, 1).wait()
            for kk in range(copies):
                scatter(jj, kk, 0).wait()

            @pl.when(jj + 2 < n_chunks)
            def _():
                read(jj + 2, 0).start()

            for kk in range(copies):
                scatter(jj + 1, kk, 1).start()

        for kk in range(copies):
            scatter(n_chunks - 1, kk, 1).wait()

    return k(src, idx)


def _sc_gather_rows(table, idx):
    _, n_chunks, w = idx.shape
    assert n_chunks % 2 == 0 and idx.shape[0] == SC_WORKERS and w == SC_WINDOW
    n_out = SC_WORKERS * n_chunks * w

    @functools.partial(
        pl.kernel, mesh=_sc_mesh(),
        out_type=jax.ShapeDtypeStruct((n_out, ROW_TILES, LANES), ROW_DTYPE),
        scratch_types=[pltpu.VMEM((n_chunks, w), jnp.int32),
                       pltpu.VMEM((2, w, ROW_TILES, LANES), ROW_DTYPE),
                       pltpu.SemaphoreType.DMA((2,)),
                       pltpu.SemaphoreType.DMA((2,))],
        name="moe_combine_gather")
    def k(table_hbm, idx_hbm, out_hbm, idx_v, rows_v, gsem, wsem):
        wid = _sc_worker_id()
        base = wid * (n_chunks * w)
        pltpu.sync_copy(idx_hbm.at[wid], idx_v)

        def gather(j, slot):
            return pltpu.make_async_copy(table_hbm.at[idx_v.at[j]], rows_v.at[slot], gsem.at[slot])

        def write(j, slot):
            return pltpu.make_async_copy(rows_v.at[slot], out_hbm.at[pl.ds(base + j * w, w)],
                                         wsem.at[slot])

        gather(0, 0).start()

        @pl.loop(0, n_chunks, step=2)
        def _(jj):
            gather(jj, 0).wait()

            @pl.when(jj > 0)
            def _():
                write(jj - 1, 1).wait()

            gather(jj + 1, 1).start()
            write(jj, 0).start()
            gather(jj + 1, 1).wait()
            write(jj, 0).wait()

            @pl.when(jj + 2 < n_chunks)
            def _():
                gather(jj + 2, 0).start()

            write(jj + 1, 1).start()

        write(n_chunks - 1, 1).wait()

    return k(table, idx)


def _moe_kernel(be_ref, nu_ref, nv_ref, slot_ref, nxt_ref, x_ref, wgu_hbm, bgu_ref, wd_hbm, bd_ref,
                o_ref, wgu_st, wd_st, wgu_bf, wd_bf, xb_ref, sem):
    b = pl.program_id(0)
    e = be_ref[b]
    prev = be_ref[jnp.maximum(b - 1, 0)]
    active = b < nu_ref[0]
    changed = (b == 0) | (e != prev)

    def weight_copies(expert, s):
        return (pltpu.make_async_copy(wgu_hbm.at[expert], wgu_st.at[s], sem.at[0, s]),
                pltpu.make_async_copy(wd_hbm.at[expert], wd_st.at[s], sem.at[1, s]))

    @pl.when(active & changed)
    def _():
        s = slot_ref[b]

        @pl.when(b == 0)
        def _():
            for cp in weight_copies(e, s):
                cp.start()

        for cp in weight_copies(e, s):
            cp.wait()
        wgu_bf[...] = wgu_st[s].astype(BF16)
        wd_bf[...] = wd_st[s].astype(BF16)
        nxt = nxt_ref[b]

        @pl.when(nxt >= 0)
        def _():
            for cp in weight_copies(nxt, 1 - s):
                cp.start()

    @pl.when(active)
    def _():
        valid = lax.broadcasted_iota(jnp.int32, (MOE_ROWS, LANES), 0) < nv_ref[b]
        for j in range(ROW_TILES):
            for half, xj in enumerate(_load_row_tile(x_ref, j, MOE_ROWS)):
                c0 = half * HALF_MODEL + j * LANES
                xb_ref[:, c0:c0 + LANES] = jnp.where(valid, xj, 0.0).astype(BF16)
        gu = _dot(xb_ref[...], wgu_bf[...]) + bgu_ref[0]
        gate = jnp.minimum(gu[:, :D_EXPERT], SWIGLU_LIMIT)
        up = jnp.clip(gu[:, D_EXPERT:], -SWIGLU_LIMIT, SWIGLU_LIMIT)
        glu = gate * _sigmoid(gate * SWIGLU_ALPHA)
        act = ((up + 1.0) * glu).astype(BF16)
        _store_row_tiles(o_ref, _dot(act, wd_bf[...]) + bd_ref[0])


def _moe(tables, xs, w_gate_up, b_gate_up, w_down, b_down):
    p_rows = xs.shape[0] // ROW_TILES
    n_blocks = p_rows // MOE_ROWS

    def blk(b, be, nu, *_):
        return jnp.minimum(b, nu[0] - 1)

    def expert(b, be, nu, *_):
        return (be[blk(b, be, nu)], 0, 0)

    def rows(b, be, nu, *_):
        return (blk(b, be, nu), 0)

    grid_spec = pltpu.PrefetchScalarGridSpec(
        num_scalar_prefetch=len(tables),
        grid=(n_blocks,),
        in_specs=[
            pl.BlockSpec((MOE_ROWS * ROW_TILES, LANES), rows),
            pl.BlockSpec(memory_space=pl.ANY),
            pl.BlockSpec((1, 1, 2 * D_EXPERT), expert),
            pl.BlockSpec(memory_space=pl.ANY),
            pl.BlockSpec((1, 1, D_MODEL), expert),
        ],
        out_specs=pl.BlockSpec((MOE_ROWS * ROW_TILES, LANES), rows),
        scratch_shapes=[pltpu.VMEM((2, D_MODEL, 2 * D_EXPERT), F32),
                        pltpu.VMEM((2, D_EXPERT, D_MODEL), F32),
                        pltpu.VMEM((D_MODEL, 2 * D_EXPERT), BF16),
                        pltpu.VMEM((D_EXPERT, D_MODEL), BF16),
                        pltpu.VMEM((MOE_ROWS, D_MODEL), BF16),
                        pltpu.SemaphoreType.DMA((2, 2))],
    )
    return pl.pallas_call(
        _moe_kernel,
        grid_spec=grid_spec,
        out_shape=jax.ShapeDtypeStruct((p_rows * ROW_TILES, LANES), ROW_DTYPE),
        compiler_params=pltpu.CompilerParams(
            dimension_semantics=("arbitrary",), vmem_limit_bytes=VMEM_LIMIT),
        name="moe_grouped_mlp",
    )(*tables, xs, w_gate_up, b_gate_up.reshape(N_EXPERTS, 1, 2 * D_EXPERT), w_down,
      b_down.reshape(N_EXPERTS, 1, D_MODEL))


def _combine_kernel(x1_ref, y0_ref, y1_ref, y2_ref, y3_ref, rw_ref, mod_ref, g_ref, b_ref, o_ref,
                    *, alpha):
    rw = rw_ref[...]
    y_refs = (y0_ref, y1_ref, y2_ref, y3_ref)
    pieces = [None] * (2 * ROW_TILES)
    for j in range(ROW_TILES):
        for kk in range(TOP_K):
            for half, yj in enumerate(_load_row_tile(y_refs[kk], j, rw.shape[0])):
                term = rw[:, kk:kk + 1] * yj
                slot = half * ROW_TILES + j
                pieces[slot] = term if kk == 0 else pieces[slot] + term
    ff = jnp.concatenate(pieces, axis=-1)
    y = alpha * x1_ref[...] + mod_ref[0, 5:6, :] * ff
    o_ref[...] = _layer_norm(y) * g_ref[...] + b_ref[...]


def _combine(x1, yg, rw, mod, l2g, l2b, *, row0, n_rows, mod_map, alpha):
    tm = TM_MIX
    t0 = row0 // tm
    tiles_all = x1.shape[0] // tm
    kern = functools.partial(_combine_kernel, alpha=alpha)

    def y_spec(kk):
        return pl.BlockSpec((tm * ROW_TILES, LANES), lambda i: (kk * tiles_all + t0 + i, 0))

    return pl.pallas_call(
        kern,
        grid=(n_rows // tm,),
        in_specs=[pl.BlockSpec((tm, D_MODEL), lambda i: (t0 + i, 0))]
        + [y_spec(kk) for kk in range(TOP_K)]
        + [pl.BlockSpec((tm, LANES), lambda i: (t0 + i, 0)),
           pl.BlockSpec((1, N_MOD, D_MODEL), mod_map),
           pl.BlockSpec((1, D_MODEL), lambda i: (0, 0)),
           pl.BlockSpec((1, D_MODEL), lambda i: (0, 0))],
        out_specs=pl.BlockSpec((tm, D_MODEL), lambda i: (i, 0)),
        out_shape=jax.ShapeDtypeStruct((n_rows, D_MODEL), F32),
        compiler_params=pltpu.CompilerParams(
            dimension_semantics=("arbitrary",), vmem_limit_bytes=VMEM_LIMIT),
        name="combine_ln2",
    )(x1, yg, yg, yg, yg, rw, mod, l2g, l2b)


def _route_kernel(ridx_ref, dest_ref, cnt_ref, run_ref, bst_ref):
    phase = pl.program_id(0)
    i = pl.program_id(1)
    tm = ridx_ref.shape[0]
    ridx = ridx_ref[...]
    lane = lax.broadcasted_iota(jnp.int32, (tm, LANES), 1)
    hits = [ridx[:, kk:kk + 1] == lane for kk in range(TOP_K)]
    chosen = jnp.where(hits[0], 1.0, 0.0)
    for kk in range(1, TOP_K):
        chosen = chosen + jnp.where(hits[kk], 1.0, 0.0)
    colsum = jnp.sum(chosen, axis=0, keepdims=True)

    @pl.when((phase == 0) & (i == 0))
    def _():
        run_ref[...] = jnp.zeros_like(run_ref)

    @pl.when(phase == 0)
    def _():
        run_ref[...] = run_ref[...] + colsum

    @pl.when((phase == 1) & (i == 0))
    def _():
        counts = run_ref[...]
        cnt_ref[...] = counts
        blocks = jnp.floor((counts + (MOE_ROWS - 1.0)) * (1.0 / MOE_ROWS))
        r = lax.broadcasted_iota(jnp.int32, (LANES, LANES), 0)
        c = lax.broadcasted_iota(jnp.int32, (LANES, LANES), 1)
        before = jnp.dot(blocks, (r < c).astype(F32), precision=HIGHEST, preferred_element_type=F32)
        bst_ref[...] = before * float(MOE_ROWS)
        run_ref[...] = jnp.zeros_like(run_ref)

    @pl.when(phase == 1)
    def _():
        rt = lax.broadcasted_iota(jnp.int32, (tm, tm), 0)
        ct = lax.broadcasted_iota(jnp.int32, (tm, tm), 1)
        earlier = _dot((ct < rt).astype(BF16), chosen.astype(BF16))
        row_of = bst_ref[0:1, :] + run_ref[0:1, :] + earlier
        out = jnp.zeros((tm, LANES), F32)
        for kk in range(TOP_K):
            dk = jnp.sum(jnp.where(hits[kk], row_of, 0.0), axis=-1, keepdims=True)
            out = jnp.where(lane == kk, dk, out)
        dest_ref[...] = out.T[0:SUBLANES, :].astype(jnp.int32)
        run_ref[...] = run_ref[...] + colsum


def _route(ridx):
    t_all = ridx.shape[0]
    tm = TM_ROUTE
    return pl.pallas_call(
        _route_kernel,
        grid=(2, t_all // tm),
        in_specs=[pl.BlockSpec((tm, LANES), lambda p, i: (i, 0))],
        out_specs=[pl.BlockSpec((SUBLANES, tm), lambda p, i: (0, i * p)),
                   pl.BlockSpec((SUBLANES, LANES), lambda p, i: (0, 0))],
        out_shape=[jax.ShapeDtypeStruct((SUBLANES, t_all), jnp.int32),
                   jax.ShapeDtypeStruct((SUBLANES, LANES), F32)],
        scratch_shapes=[pltpu.VMEM((SUBLANES, LANES), F32),
                        pltpu.VMEM((SUBLANES, LANES), F32)],
        compiler_params=pltpu.CompilerParams(
            dimension_semantics=("arbitrary", "arbitrary"), vmem_limit_bytes=VMEM_LIMIT),
        name="moe_route",
    )(ridx)


def _routing_tables(counts, n_blocks):
    experts = jnp.arange(N_EXPERTS, dtype=jnp.int32)
    blocks_per = (counts + MOE_ROWS - 1) // MOE_ROWS
    bends = jnp.cumsum(blocks_per)
    bstarts = bends - blocks_per
    blocks = jnp.arange(n_blocks, dtype=jnp.int32)
    block_expert = jnp.minimum(
        jnp.sum((bends[None, :] <= blocks[:, None]).astype(jnp.int32), axis=1), N_EXPERTS - 1)
    n_used = bends[-1:].astype(jnp.int32)
    owner = block_expert[:, None] == experts[None, :]

    def per_block(table):
        return jnp.sum(jnp.where(owner, table[None, :], 0), axis=1)

    n_valid = jnp.clip(per_block(counts) - (blocks - per_block(bstarts)) * MOE_ROWS,
                       0, MOE_ROWS).astype(jnp.int32)
    present = blocks_per > 0
    ordinal = jnp.cumsum(present.astype(jnp.int32)) - 1
    later = lax.cummin(jnp.where(present, experts, N_EXPERTS), reverse=True)
    succ = jnp.concatenate([later[1:], jnp.full((1,), N_EXPERTS, jnp.int32)])
    succ = jnp.where(succ >= N_EXPERTS, -1, succ)
    stage_slot = (per_block(ordinal) % 2).astype(jnp.int32)
    next_expert = per_block(succ).astype(jnp.int32)
    return (block_expert.astype(jnp.int32), n_used, n_valid, stage_slot, next_expert)


def _pos_embed_2d(n_tokens):
    rows = n_tokens // GRID_W
    r = jnp.repeat(jnp.arange(rows), GRID_W).astype(F32)
    col = jnp.tile(jnp.arange(GRID_W), rows).astype(F32)
    quarter = D_MODEL // 4
    omega = 1.0 / (10000.0 ** (jnp.arange(quarter, dtype=F32) / quarter))
    er = r[:, None] * omega
    ec = col[:, None] * omega
    return jnp.concatenate([jnp.sin(er), jnp.cos(er), jnp.sin(ec), jnp.cos(ec)], axis=-1)


def _reorder_w_in(w):
    o_r = 2 * GLA_DK + 2 * GLA_DV
    o_f = o_r + DECAY_RANK
    o_gate = o_f + FNET_DIM
    pad = jnp.zeros((w.shape[0], LANES - DECAY_RANK), w.dtype)
    w_q = w[:, :GLA_DK] * (DK_HEAD ** -0.5)
    return jnp.concatenate([w_q, w[:, GLA_DK:o_r], w[:, o_gate:], w[:, o_f:o_gate], w[:, o_r:o_f], pad],
                           axis=1)


def kernel(x_prompt, x_sample, state_gla, c, c_ctx, w_ada, b_ada, w_in, w_dec_fwd, b_dec_fwd,
           w_dec_bwd, b_dec_bwd, gla_norm_g, w_br_gla, w_br_fnet, w_out, ln1_g, ln1_b, w_router,
           b_router, w_gate_up, b_gate_up, w_down, b_down, ln2_g, ln2_b):
    n_req, ctx_len, _ = x_prompt.shape
    n_lat, lat_len, _ = x_sample.shape
    depth = w_in.shape[0]
    alpha = (2.0 * depth) ** 0.25
    t_ctx = n_req * ctx_len
    t_lat = n_lat * lat_len
    t_all = t_ctx + t_lat

    x_ctx = x_prompt.reshape(t_ctx, D_MODEL)
    x_lat = x_sample.reshape(t_lat, D_MODEL)
    pos = _pos_embed_2d(lat_len)
    zero_pos = jnp.zeros_like(pos)

    cond_rows = -(-(n_lat + 1) // SUBLANES) * SUBLANES
    cond = jnp.zeros((cond_rows, D_MODEL), F32).at[:n_lat].set(c).at[cond_rows - 1].set(c_ctx)

    n_moe_blocks = (t_all * TOP_K) // MOE_ROWS + N_EXPERTS
    tok_chunks = t_all // (SC_WORKERS * SC_WINDOW)
    states = []
    for l in range(depth):
        mod = _ada(cond, w_ada[l], b_ada[l]).reshape(cond_rows, N_MOD, D_MODEL)
        layer_pos = pos if l == 0 else zero_pos
        proj = _inproj(x_ctx, x_lat, layer_pos, mod, _reorder_w_in(w_in[l]).astype(BF16), lat_len)

        def pad_dec(w):
            return jnp.zeros((LANES, GLA_DK), F32).at[:DECAY_RANK].set(w)

        dec = (pad_dec(w_dec_fwd[l]), b_dec_fwd[l].reshape(1, GLA_DK),
               pad_dec(w_dec_bwd[l]), b_dec_bwd[l].reshape(1, GLA_DK),
               gla_norm_g[l].reshape(1, DV_HEAD))
        o_ctx, s_new = _gla(proj, *dec, None, n_seq=n_req, seq_len=ctx_len, row0=0, emit_state=True)
        (o_lat,) = _gla(proj, *dec, state_gla[:, l], n_seq=n_lat, seq_len=lat_len, row0=t_ctx,
                        emit_state=False)
        states.append(s_new)

        mixed_ctx = _fnet(proj, n_seq=n_req, seq_len=ctx_len, row0=0)
        mixed_lat = _fnet(proj, n_seq=n_lat, seq_len=lat_len, row0=t_ctx)

        wr = jnp.zeros((D_MODEL, LANES), F32).at[:, :N_EXPERTS].set(w_router[l])
        br = jnp.full((1, LANES), -1e30, F32).at[0, :N_EXPERTS].set(b_router[l])
        wr_hi = wr.astype(BF16)
        wr_lo = (wr - wr_hi.astype(F32)).astype(BF16)
        x1, h2, ridx, rw = _merge(
            x_ctx, x_lat, layer_pos, mod, o_ctx, o_lat, mixed_ctx, mixed_lat, proj,
            w_br_gla[l].astype(BF16), w_br_fnet[l].astype(BF16), w_out[l].astype(BF16),
            ln1_g[l].reshape(1, D_MODEL), ln1_b[l].reshape(1, D_MODEL), wr_hi, wr_lo, br, lat_len,
            alpha)

        dest, counts = _route(ridx)
        moe_tables = _routing_tables(counts[0, :N_EXPERTS].astype(jnp.int32), n_moe_blocks)
        dest = dest[:TOP_K]
        scatter_idx = dest.reshape(TOP_K, SC_WORKERS, tok_chunks, SC_WINDOW)
        p_rows = n_moe_blocks * MOE_ROWS
        xs = _sc_scatter_rows(h2.reshape(t_all, ROW_TILES, LANES), scatter_idx, p_rows)
        yb = _moe(moe_tables, xs.reshape(p_rows * ROW_TILES, LANES),
                  w_gate_up[l], b_gate_up[l], w_down[l], b_down[l])
        gather_idx = dest.reshape(SC_WORKERS, TOP_K * tok_chunks, SC_WINDOW)
        yg = _sc_gather_rows(yb.reshape(p_rows, ROW_TILES, LANES), gather_idx)
        yg = yg.reshape(TOP_K * t_all * ROW_TILES, LANES)

        l2g = ln2_g[l].reshape(1, D_MODEL)
        l2b = ln2_b[l].reshape(1, D_MODEL)
        tiles_per_seq = lat_len // TM_MIX
        x_ctx = _combine(x1, yg, rw, mod, l2g, l2b, row0=0, n_rows=t_ctx,
                         mod_map=lambda i: (cond_rows - 1, 0, 0), alpha=alpha)
        x_lat = _combine(x1, yg, rw, mod, l2g, l2b, row0=t_ctx, n_rows=t_lat,
                         mod_map=lambda i: (i // tiles_per_seq, 0, 0), alpha=alpha)

    y_prompt = x_ctx.reshape(x_prompt.shape)
    y_sample = x_lat.reshape(x_sample.shape)
    new_state = jnp.stack(states, axis=1).astype(x_prompt.dtype)
    return (y_prompt, y_sample, new_state)
```

```python
import functools
import math

import numpy as np
import jax
import jax.numpy as jnp
from jax import lax
from jax.experimental import pallas as pl
from jax.experimental.pallas import tpu as pltpu
from jax.experimental.pallas import tpu_sc as plsc

F32 = jnp.float32
BF16 = jnp.bfloat16

D_MODEL = 1024
GRID_W = 64
GLA_HEADS = 4
DK_HEAD = 128
DV_HEAD = 256
GLA_DK = GLA_HEADS * DK_HEAD
GLA_DV = GLA_HEADS * DV_HEAD
DECAY_RANK = 16
GATE_NORMALIZER = 16.0
FNET_GROUPS = 4
FNET_GROUP_DIM = 128
FNET_DIM = FNET_GROUPS * FNET_GROUP_DIM
N_EXPERTS = 32
TOP_K = 4
D_EXPERT = 1024
SWIGLU_LIMIT = 7.0
SWIGLU_ALPHA = 1.702
LN_EPS = 1e-6
N_MOD = 6

LANES = 128
SUBLANES = 8
HALF_MODEL = D_MODEL // 2
ROW_TILES = HALF_MODEL // LANES
ROW_DTYPE = jnp.uint32
COL_Q = 0
COL_K = GLA_DK
COL_V = 2 * GLA_DK
COL_G = COL_V + GLA_DV
COL_GATE_A = COL_G + GLA_DV
COL_GATE_B = COL_GATE_A + D_MODEL
COL_F = COL_GATE_B + D_MODEL
COL_R = COL_F + FNET_DIM
PROJ_COLS = COL_R + LANES

GLA_CHUNK = 128
GLA_LEAF = 16
TM_PROJ = 256
TM_MIX = 256
TM_MERGE = 512
TM_MERGE_GROUP = 128
TM_ROUTE = 1024
MOE_ROWS = 256
VMEM_LIMIT = 56 * 1024 * 1024

SC_CORES = 2
SC_SUBCORES = 16
SC_WORKERS = SC_CORES * SC_SUBCORES
SC_WINDOW = 64

HIGHEST = lax.Precision.HIGHEST


def _layer_norm(x):
    mu = jnp.mean(x, axis=-1, keepdims=True)
    xc = x - mu
    var = jnp.mean(xc * xc, axis=-1, keepdims=True)
    return xc * lax.rsqrt(var + LN_EPS)


def _sigmoid(x):
    return 0.5 * jnp.tanh(0.5 * x) + 0.5


def _log_sigmoid(z):
    return jnp.minimum(z, 0.0) - jnp.log1p(jnp.exp(-jnp.abs(z)))


def _dot(a, b):
    return jnp.dot(a, b, preferred_element_type=F32)


def _split_bf16(x, terms):
    parts = []
    for _ in range(terms):
        p = x.astype(BF16)
        parts.append(p)
        x = x - p.astype(F32)
    return parts


def _dot_nt(a, b):
    return lax.dot_general(a, b, (((1,), (1,)), ((), ())), preferred_element_type=F32)


def _dot_tn(a, b):
    return lax.dot_general(a, b, (((0,), (0,)), ((), ())), preferred_element_type=F32)


def _row_tile_slice(j, n_rows, first_row=0):
    return pl.ds(first_row * ROW_TILES + j, n_rows, stride=ROW_TILES)


def _store_row_tiles(ref, val, first_row=0):
    for j in range(ROW_TILES):
        lo = val[:, j * LANES:(j + 1) * LANES]
        hi = val[:, HALF_MODEL + j * LANES:HALF_MODEL + (j + 1) * LANES]
        ref[_row_tile_slice(j, val.shape[0], first_row), :] = pltpu.pack_elementwise(
            [lo, hi], packed_dtype=BF16)


def _load_row_tile(ref, j, n_rows):
    words = ref[_row_tile_slice(j, n_rows), :]
    return tuple(pltpu.unpack_elementwise(words, index=half, packed_dtype=BF16, unpacked_dtype=F32)
                 for half in range(2))


def _ada_kernel(c_ref, w_ref, b_ref, o_ref):
    c = c_ref[...]
    s = c * _sigmoid(c)
    o_ref[...] = _dot(s.astype(BF16), w_ref[...].astype(BF16)) + b_ref[...]


def _ada(cond, w_ada, b_ada):
    rows = cond.shape[0]
    n = w_ada.shape[1]
    tn = 1536
    return pl.pallas_call(
        _ada_kernel,
        grid=(n // tn,),
        in_specs=[pl.BlockSpec((rows, D_MODEL), lambda j: (0, 0)),
                  pl.BlockSpec((D_MODEL, tn), lambda j: (0, j)),
                  pl.BlockSpec((1, tn), lambda j: (0, j))],
        out_specs=pl.BlockSpec((rows, tn), lambda j: (0, j)),
        out_shape=jax.ShapeDtypeStruct((rows, n), F32),
        compiler_params=pltpu.CompilerParams(vmem_limit_bytes=VMEM_LIMIT),
        name="ada_mod",
    )(cond, w_ada, b_ada.reshape(1, n))


def _group_maps(n_ctx_tiles):
    def ctx_map(i, *_):
        return (jnp.minimum(i, n_ctx_tiles - 1), 0)

    def lat_map(i, *_):
        return (jnp.maximum(i - n_ctx_tiles, 0), 0)

    return ctx_map, lat_map


def _token_specs(tm, n_ctx_tiles, tiles_per_latent_seq, ctx_mod_row):
    ctx_map, lat_map = _group_maps(n_ctx_tiles)

    def pos_map(i, *_):
        return (jnp.maximum(i - n_ctx_tiles, 0) % tiles_per_latent_seq, 0)

    def mod_map(i, *_):
        return (jnp.where(i < n_ctx_tiles, ctx_mod_row,
                          jnp.maximum(i - n_ctx_tiles, 0) // tiles_per_latent_seq), 0, 0)

    return [pl.BlockSpec((tm, D_MODEL), ctx_map),
            pl.BlockSpec((tm, D_MODEL), lat_map),
            pl.BlockSpec((tm, D_MODEL), pos_map),
            pl.BlockSpec((1, N_MOD, D_MODEL), mod_map)]


def _inproj_kernel(xc_ref, xl_ref, pos_ref, mod_ref, w_ref, o_ref, *, n_ctx_tiles):
    i = pl.program_id(0)

    def project(x):
        h = _layer_norm(x) * (1.0 + mod_ref[0, 1:2, :]) + mod_ref[0, 0:1, :]
        o_ref[...] = _dot(h.astype(BF16), w_ref[...])

    @pl.when(i < n_ctx_tiles)
    def _():
        project(xc_ref[...])

    @pl.when(i >= n_ctx_tiles)
    def _():
        project(xl_ref[...] + pos_ref[...])


def _inproj(x_ctx, x_lat, pos, mod, w_in_bf, lat_len):
    t_ctx, t_lat = x_ctx.shape[0], x_lat.shape[0]
    n_ctx_tiles = t_ctx // TM_PROJ
    n_tiles = (t_ctx + t_lat) // TM_PROJ
    kern = functools.partial(_inproj_kernel, n_ctx_tiles=n_ctx_tiles)
    specs = _token_specs(TM_PROJ, n_ctx_tiles, lat_len // TM_PROJ, mod.shape[0] - 1)
    return pl.pallas_call(
        kern,
        grid=(n_tiles,),
        in_specs=specs + [pl.BlockSpec((D_MODEL, PROJ_COLS), lambda i: (0, 0),
                                       pipeline_mode=pl.Buffered(1))],
        out_specs=pl.BlockSpec((TM_PROJ, PROJ_COLS), lambda i: (i, 0)),
        out_shape=jax.ShapeDtypeStruct((t_ctx + t_lat, PROJ_COLS), F32),
        compiler_params=pltpu.CompilerParams(
            dimension_semantics=("arbitrary",), vmem_limit_bytes=VMEM_LIMIT),
        name="ln_inproj",
    )(x_ctx, x_lat, pos, mod, w_in_bf)


def _gla_kernel(*refs, seq_len, has_s0, emit_state):
    it = iter(refs)
    q_ref, k_ref, v_ref, r_ref = next(it), next(it), next(it), next(it)
    wdf_ref, bdf_ref, wdb_ref, bdb_ref, g_ref = next(it), next(it), next(it), next(it), next(it)
    s0_ref = next(it) if has_s0 else None
    o_ref = next(it)
    sout_ref = next(it) if emit_state else None
    cum_ref, a_ref, qi_ref, ko_ref, dec_ref, op_ref, st_ref = (next(it) for _ in range(7))

    C = GLA_CHUNK
    n_chunks = seq_len // C

    def rows(n):
        if isinstance(n, int):
            return pl.ds(n * C, C)
        return pl.ds(pl.multiple_of(n * C, C), C)

    def dec_rows(n):
        if isinstance(n, int):
            return pl.ds(n * SUBLANES, SUBLANES)
        return pl.ds(pl.multiple_of(n * SUBLANES, SUBLANES), SUBLANES)

    def loop(body):
        if n_chunks <= 2:
            for n in range(n_chunks):
                body(n)
        else:
            def step(m, carry):
                body(2 * m)
                body(2 * m + 1)
                return carry
            lax.fori_loop(0, n_chunks // 2, step, 0)

    rt = lax.broadcasted_iota(jnp.int32, (C, C), 0)
    ct = lax.broadcasted_iota(jnp.int32, (C, C), 1)
    tri = ((rt >= ct).astype(BF16), (ct >= rt).astype(BF16))
    row_id = lax.broadcasted_iota(jnp.int32, (C, DK_HEAD), 0)

    r_hi, r_lo = _split_bf16(r_ref[...], 2)
    for d, (w_ref, b_ref) in enumerate(((wdf_ref, bdf_ref), (wdb_ref, bdb_ref))):
        w_hi, w_lo = _split_bf16(w_ref[...], 2)
        z = _dot(r_hi, w_hi) + _dot(r_lo, w_hi) + _dot(r_hi, w_lo) + b_ref[...]
        cum_ref[d] = _log_sigmoid(z) * (1.0 / GATE_NORMALIZER)

    def cumsum_chunk(n):
        for d in range(2):
            la_hi, la_lo = _split_bf16(cum_ref[d, rows(n), :], 2)
            cum_ref[d, rows(n), :] = _dot(tri[d], la_hi) + _dot(tri[d], la_lo)

    loop(cumsum_chunk)

    query_rows = ({}, {})
    keep = ({}, {})
    for d in range(2):
        blk = C // 2
        while blk >= GLA_LEAF:
            q_parity = 1 if d == 0 else 0
            query_rows[d][blk] = ((row_id // blk) % 2) == q_parity
            qb, kb = rt // blk, ct // blk
            keep[d][blk] = ((qb % 2) == q_parity) & ((qb == kb + 1) if d == 0 else (kb == qb + 1))
            blk //= 2
        order = (rt >= ct) if d == 0 else (ct >= rt)
        keep[d][0] = ((rt // GLA_LEAF) == (ct // GLA_LEAF)) & order

    def block_rows(cum, first, step, count):
        span = C // count
        parts = [jnp.broadcast_to(cum[first + p * step:first + p * step + 1, :], (span, DK_HEAD))
                 for p in range(count)]
        return parts[0] if count == 1 else jnp.concatenate(parts, axis=0)

    def scores(n, d):
        cum = cum_ref[d, rows(n), :]
        q = q_ref[rows(n), :]
        k = k_ref[rows(n), :]
        acc = None
        blk = C // 2
        while blk >= GLA_LEAF:
            pairs = C // (2 * blk)
            bnd = blk - 1 if d == 0 else blk
            w = jnp.exp(-jnp.abs(cum - block_rows(cum, bnd, 2 * blk, pairs)))
            x = (jnp.where(query_rows[d][blk], q, k) * w).astype(BF16)
            s = jnp.where(keep[d][blk], _dot_nt(x, x), 0.0)
            acc = s if acc is None else acc + s
            blk //= 2
        mid = GLA_LEAF // 2 - 1 if d == 0 else GLA_LEAF // 2
        e = cum - block_rows(cum, mid, GLA_LEAF, C // GLA_LEAF)
        s = _dot_nt((q * jnp.exp(e)).astype(BF16), (k * jnp.exp(-e)).astype(BF16))
        acc = acc + jnp.where(keep[d][0], s, 0.0)
        a_ref[d, rows(n), :] = acc.astype(BF16)
        end = cum[C - 1:C, :] if d == 0 else cum[0:1, :]
        qi_ref[d, rows(n), :] = (q * jnp.exp(cum)).astype(BF16)
        ko_ref[d, rows(n), :] = (k * jnp.exp(end - cum)).astype(BF16)
        dec_ref[d, dec_rows(n), :] = jnp.broadcast_to(jnp.exp(end), (SUBLANES, DK_HEAD))

    def scores_chunk(n):
        scores(n, 0)
        scores(n, 1)

    loop(scores_chunk)

    for d in range(2):
        if has_s0:
            st_ref[d] = s0_ref[0, d, 0].T
        else:
            st_ref[d] = jnp.zeros((DV_HEAD, DK_HEAD), F32)

    def scan(n, d):
        v = v_ref[rows(n), :].astype(BF16)
        st = st_ref[d]
        o = _dot(a_ref[d, rows(n), :], v) + _dot_nt(qi_ref[d, rows(n), :], st.astype(BF16))
        st_ref[d] = st * dec_ref[d, dec_rows(n), :][0:1, :] + _dot_tn(v, ko_ref[d, rows(n), :])
        op_ref[d, rows(n), :] = o

    def scan_chunk(m):
        scan(m, 0)
        scan(n_chunks - 1 - m, 1)

    loop(scan_chunk)
    if emit_state:
        sout_ref[0, 0, 0] = st_ref[0].T
        sout_ref[0, 1, 0] = st_ref[1].T

    g = g_ref[...]

    def finish_chunk(n):
        o = op_ref[0, rows(n), :] + op_ref[1, rows(n), :]
        ms = jnp.mean(o * o, axis=-1, keepdims=True)
        o_ref[rows(n), :] = o * lax.rsqrt(ms + LN_EPS) * g

    loop(finish_chunk)


def _gla(proj, wdf, bdf, wdb, bdb, g, s0, *, n_seq, seq_len, row0, emit_state):
    has_s0 = s0 is not None
    blk0 = row0 // seq_len
    kern = functools.partial(_gla_kernel, seq_len=seq_len, has_s0=has_s0, emit_state=emit_state)
    in_specs = [
        pl.BlockSpec((seq_len, DK_HEAD), lambda b, h: (blk0 + b, COL_Q // DK_HEAD + h)),
        pl.BlockSpec((seq_len, DK_HEAD), lambda b, h: (blk0 + b, COL_K // DK_HEAD + h)),
        pl.BlockSpec((seq_len, DV_HEAD), lambda b, h: (blk0 + b, COL_V // DV_HEAD + h)),
        pl.BlockSpec((seq_len, LANES), lambda b, h: (blk0 + b, COL_R // LANES)),
        pl.BlockSpec((LANES, DK_HEAD), lambda b, h: (0, h)),
        pl.BlockSpec((1, DK_HEAD), lambda b, h: (0, h)),
        pl.BlockSpec((LANES, DK_HEAD), lambda b, h: (0, h)),
        pl.BlockSpec((1, DK_HEAD), lambda b, h: (0, h)),
        pl.BlockSpec((1, DV_HEAD), lambda b, h: (0, 0)),
    ]
    args = [proj, proj, proj, proj, wdf, bdf, wdb, bdb, g]
    if has_s0:
        in_specs.append(pl.BlockSpec((1, 2, 1, DK_HEAD, DV_HEAD), lambda b, h: (b, 0, h, 0, 0)))
        args.append(s0)
    out_specs = [pl.BlockSpec((seq_len, DV_HEAD), lambda b, h: (b, h))]
    out_shape = [jax.ShapeDtypeStruct((n_seq * seq_len, GLA_DV), F32)]
    if emit_state:
        out_specs.append(pl.BlockSpec((1, 2, 1, DK_HEAD, DV_HEAD), lambda b, h: (b, 0, h, 0, 0)))
        out_shape.append(jax.ShapeDtypeStruct((n_seq, 2, GLA_HEADS, DK_HEAD, DV_HEAD), F32))

    res = pl.pallas_call(
        kern,
        grid=(n_seq, GLA_HEADS),
        in_specs=in_specs,
        out_specs=out_specs,
        out_shape=out_shape,
        scratch_shapes=[pltpu.VMEM((2, seq_len, DK_HEAD), F32),
                        pltpu.VMEM((2, seq_len, GLA_CHUNK), BF16),
                        pltpu.VMEM((2, seq_len, DK_HEAD), BF16),
                        pltpu.VMEM((2, seq_len, DK_HEAD), BF16),
                        pltpu.VMEM((2, seq_len // GLA_CHUNK * SUBLANES, DK_HEAD), F32),
                        pltpu.VMEM((2, seq_len, DV_HEAD), F32),
                        pltpu.VMEM((2, DV_HEAD, DK_HEAD), F32)],
        compiler_params=pltpu.CompilerParams(
            dimension_semantics=("arbitrary", "arbitrary"), vmem_limit_bytes=VMEM_LIMIT),
        name="gla_seq%d" % seq_len,
    )(*args)
    return res


def _fnet_kernel(f_ref, cl_ref, sl_ref, cg_ref, sg_ref, o_ref, uc_ref, us_ref, *, seq_len):
    cg = cg_ref[...]
    sg = sg_ref[...]
    for grp in range(FNET_GROUPS):
        lo = grp * FNET_GROUP_DIM
        u = f_ref[:, lo:lo + FNET_GROUP_DIM].astype(BF16)
        uc_ref[:, lo:lo + FNET_GROUP_DIM] = _dot(u, cg).astype(BF16)
        us_ref[:, lo:lo + FNET_GROUP_DIM] = _dot(u, sg).astype(BF16)
    mixed = _dot(cl_ref[...], uc_ref[...]) - _dot(sl_ref[...], us_ref[...])
    o_ref[...] = mixed * (1.0 / math.sqrt(seq_len * FNET_GROUP_DIM))


def _dft_mats(n):
    j = np.arange(n, dtype=np.int64)
    ang = (2.0 * np.pi / n) * ((j[:, None] * j[None, :]) % n).astype(np.float64)
    return (jnp.asarray(np.cos(ang), dtype=F32).astype(BF16),
            jnp.asarray(np.sin(ang), dtype=F32).astype(BF16))


def _fnet(proj, *, n_seq, seq_len, row0):
    blk0 = row0 // seq_len
    cl, sl = _dft_mats(seq_len)
    cg, sg = _dft_mats(FNET_GROUP_DIM)
    kern = functools.partial(_fnet_kernel, seq_len=seq_len)
    return pl.pallas_call(
        kern,
        grid=(n_seq,),
        in_specs=[pl.BlockSpec((seq_len, FNET_DIM), lambda b: (blk0 + b, COL_F // FNET_DIM)),
                  pl.BlockSpec((seq_len, seq_len), lambda b: (0, 0)),
                  pl.BlockSpec((seq_len, seq_len), lambda b: (0, 0)),
                  pl.BlockSpec((FNET_GROUP_DIM, FNET_GROUP_DIM), lambda b: (0, 0)),
                  pl.BlockSpec((FNET_GROUP_DIM, FNET_GROUP_DIM), lambda b: (0, 0))],
        out_specs=pl.BlockSpec((seq_len, FNET_DIM), lambda b: (b, 0)),
        out_shape=jax.ShapeDtypeStruct((n_seq * seq_len, FNET_DIM), F32),
        scratch_shapes=[pltpu.VMEM((seq_len, FNET_DIM), BF16),
                        pltpu.VMEM((seq_len, FNET_DIM), BF16)],
        compiler_params=pltpu.CompilerParams(
            dimension_semantics=("arbitrary",), vmem_limit_bytes=VMEM_LIMIT),
        name="fnet_seq%d" % seq_len,
    )(proj, cl, sl, cg, sg)


def _merge_kernel(xc_ref, xl_ref, pos_ref, mod_ref, oc_ref, ol_ref, mc_ref, ml_ref,
                  g_ref, ga_ref, gb_ref, wbg_ref, wbf_ref, wo_ref, l1g_ref, l1b_ref, wrh_ref, wrl_ref,
                  br_ref, x1_ref, h2_ref, ridx_ref, rw_ref, *, n_ctx_tiles, alpha):
    i = pl.program_id(0)

    def compute(rows, x, o, mx):
        g = g_ref[rows, :]
        a = (o * (g * _sigmoid(g))).astype(BF16)
        gla_out = _dot(a, wbg_ref[...])
        fnet_out = _dot(mx.astype(BF16), wbf_ref[...])
        merged = _sigmoid(ga_ref[rows, :]) * gla_out + _sigmoid(gb_ref[rows, :]) * fnet_out
        mix = _dot(merged.astype(BF16), wo_ref[...])
        y = alpha * x + mod_ref[0, 2:3, :] * mix
        x1 = _layer_norm(y) * l1g_ref[...] + l1b_ref[...]
        x1_ref[rows, :] = x1
        h2 = _layer_norm(x1) * (1.0 + mod_ref[0, 4:5, :]) + mod_ref[0, 3:4, :]
        _store_row_tiles(h2_ref, h2, rows.start)

        h_hi, h_lo = _split_bf16(h2, 2)
        logits = (_dot(h_hi, wrh_ref[...]) + _dot(h_lo, wrh_ref[...]) + _dot(h_hi, wrl_ref[...])
                  + br_ref[...])
        lane_i = lax.broadcasted_iota(jnp.int32, logits.shape, 1)
        lane = lane_i.astype(F32)
        idx_out = jnp.zeros(logits.shape, F32)
        val_out = jnp.zeros(logits.shape, F32)
        top0 = None
        denom = None
        for kk in range(TOP_K):
            m = jnp.max(logits, axis=-1, keepdims=True)
            sel = jnp.min(jnp.where(logits == m, lane, float(LANES)), axis=-1, keepdims=True)
            if kk == 0:
                top0 = m
                p = jnp.ones_like(m)
                denom = p
            else:
                p = jnp.exp(m - top0)
                denom = denom + p
            idx_out = jnp.where(lane_i == kk, sel, idx_out)
            val_out = jnp.where(lane_i == kk, p, val_out)
            logits = jnp.where(lane == sel, -jnp.inf, logits)
        ridx_ref[rows, :] = idx_out.astype(jnp.int32)
        rw_ref[rows, :] = val_out / denom

    tm = x1_ref.shape[0]
    groups = [slice(r0, r0 + TM_MERGE_GROUP) for r0 in range(0, tm, TM_MERGE_GROUP)]

    @pl.when(i < n_ctx_tiles)
    def _():
        for rows in groups:
            compute(rows, xc_ref[rows, :], oc_ref[rows, :], mc_ref[rows, :])

    @pl.when(i >= n_ctx_tiles)
    def _():
        for rows in groups:
            compute(rows, xl_ref[rows, :] + pos_ref[rows, :], ol_ref[rows, :], ml_ref[rows, :])


def _merge(x_ctx, x_lat, pos, mod, o_ctx, o_lat, mixed_ctx, mixed_lat, proj,
           wbg, wbf, wo, l1g, l1b, wr_hi, wr_lo, br, lat_len, alpha):
    t_ctx, t_lat = x_ctx.shape[0], x_lat.shape[0]
    t_all = t_ctx + t_lat
    tm = TM_MERGE
    n_ctx_tiles = t_ctx // tm
    kern = functools.partial(_merge_kernel, n_ctx_tiles=n_ctx_tiles, alpha=alpha)
    specs = _token_specs(tm, n_ctx_tiles, lat_len // tm, mod.shape[0] - 1)
    ctx_map, lat_map = _group_maps(n_ctx_tiles)

    def const(shape):
        return pl.BlockSpec(shape, lambda i: (0,) * len(shape))

    in_specs = specs + [
        pl.BlockSpec((tm, GLA_DV), ctx_map),
        pl.BlockSpec((tm, GLA_DV), lat_map),
        pl.BlockSpec((tm, FNET_DIM), ctx_map),
        pl.BlockSpec((tm, FNET_DIM), lat_map),
        pl.BlockSpec((tm, GLA_DV), lambda i: (i, COL_G // GLA_DV)),
        pl.BlockSpec((tm, D_MODEL), lambda i: (i, COL_GATE_A // D_MODEL)),
        pl.BlockSpec((tm, D_MODEL), lambda i: (i, COL_GATE_B // D_MODEL)),
        const((GLA_DV, D_MODEL)), const((FNET_DIM, D_MODEL)), const((D_MODEL, D_MODEL)),
        const((1, D_MODEL)), const((1, D_MODEL)),
        const((D_MODEL, LANES)), const((D_MODEL, LANES)), const((1, LANES)),
    ]
    out_specs = [pl.BlockSpec((tm, D_MODEL), lambda i: (i, 0)),
                 pl.BlockSpec((tm * ROW_TILES, LANES), lambda i: (i, 0)),
                 pl.BlockSpec((tm, LANES), lambda i: (i, 0)),
                 pl.BlockSpec((tm, LANES), lambda i: (i, 0))]
    out_shape = [jax.ShapeDtypeStruct((t_all, D_MODEL), F32),
                 jax.ShapeDtypeStruct((t_all * ROW_TILES, LANES), ROW_DTYPE),
                 jax.ShapeDtypeStruct((t_all, LANES), jnp.int32),
                 jax.ShapeDtypeStruct((t_all, LANES), F32)]
    return pl.pallas_call(
        kern,
        grid=(t_all // tm,),
        in_specs=in_specs,
        out_specs=out_specs,
        out_shape=out_shape,
        compiler_params=pltpu.CompilerParams(
            dimension_semantics=("arbitrary",), vmem_limit_bytes=VMEM_LIMIT),
        name="merge_ln1_router",
    )(x_ctx, x_lat, pos, mod, o_ctx, o_lat, mixed_ctx, mixed_lat, proj, proj, proj,
      wbg, wbf, wo, l1g, l1b, wr_hi, wr_lo, br)


def _sc_mesh():
    return plsc.VectorSubcoreMesh(core_axis_name="c", subcore_axis_name="s")


def _sc_worker_id():
    return lax.axis_index("s") * SC_CORES + lax.axis_index("c")


def _sc_scatter_rows(src, idx, n_out):
    n_src = src.shape[0]
    w = SC_WINDOW
    n_chunks = n_src // (SC_WORKERS * w)
    copies = idx.shape[0]
    assert n_chunks % 2 == 0 and idx.shape == (copies, SC_WORKERS, n_chunks, w)

    @functools.partial(
        pl.kernel, mesh=_sc_mesh(),
        out_type=jax.ShapeDtypeStruct((n_out, ROW_TILES, LANES), ROW_DTYPE),
        scratch_types=[pltpu.VMEM((copies * n_chunks, w), jnp.int32),
                       pltpu.VMEM((2, w, ROW_TILES, LANES), ROW_DTYPE),
                       pltpu.SemaphoreType.DMA((2,)),
                       pltpu.SemaphoreType.DMA((2,))],
        name="moe_dispatch_scatter")
    def k(src_hbm, idx_hbm, out_hbm, idx_v, rows_v, rsem, wsem):
        wid = _sc_worker_id()
        base = wid * (n_chunks * w)
        for kk in range(copies):
            pltpu.sync_copy(idx_hbm.at[kk, wid], idx_v.at[pl.ds(kk * n_chunks, n_chunks)])

        def read(j, slot):
            return pltpu.make_async_copy(src_hbm.at[pl.ds(base + j * w, w)], rows_v.at[slot],
                                         rsem.at[slot])

        def scatter(j, kk, slot):
            return pltpu.make_async_copy(rows_v.at[slot], out_hbm.at[idx_v.at[kk * n_chunks + j]],
                                         wsem.at[slot])

        read(0, 0).start()

        @pl.loop(0, n_chunks, step=2)
        def _(jj):
            read(jj, 0).wait()

            @pl.when(jj > 0)
            def _():
                for kk in range(copies):
                    scatter(jj - 1, kk, 1).wait()

            read(jj + 1, 1).start()
            for kk in range(copies):
                scatter(jj, kk, 0).start()
            read(jj + 1, 1).wait()
            for kk in range(copies):
                scatter(jj, kk, 0).wait()

            @pl.when(jj + 2 < n_chunks)
            def _():
                read(jj + 2, 0).start()

            for kk in range(copies):
                scatter(jj + 1, kk, 1).start()

        for kk in range(copies):
            scatter(n_chunks - 1, kk, 1).wait()

    return k(src, idx)


def _sc_gather_rows(table, idx):
    _, n_chunks, w = idx.shape
    assert n_chunks % 2 == 0 and idx.shape[0] == SC_WORKERS and w == SC_WINDOW
    n_out = SC_WORKERS * n_chunks * w

    @functools.partial(
        pl.kernel, mesh=_sc_mesh(),
        out_type=jax.ShapeDtypeStruct((n_out, ROW_TILES, LANES), ROW_DTYPE),
        scratch_types=[pltpu.VMEM((n_chunks, w), jnp.int32),
                       pltpu.VMEM((2, w, ROW_TILES, LANES), ROW_DTYPE),
                       pltpu.SemaphoreType.DMA((2,)),
                       pltpu.SemaphoreType.DMA((2,))],
        name="moe_combine_gather")
    def k(table_hbm, idx_hbm, out_hbm, idx_v, rows_v, gsem, wsem):
        wid = _sc_worker_id()
        base = wid * (n_chunks * w)
        pltpu.sync_copy(idx_hbm.at[wid], idx_v)

        def gather(j, slot):
            return pltpu.make_async_copy(table_hbm.at[idx_v.at[j]], rows_v.at[slot], gsem.at[slot])

        def write(j, slot):
            return pltpu.make_async_copy(rows_v.at[slot], out_hbm.at[pl.ds(base + j * w, w)],
                                         wsem.at[slot])

        gather(0, 0).start()

        @pl.loop(0, n_chunks, step=2)
        def _(jj):
            gather(jj, 0).wait()

            @pl.when(jj > 0)
            def _():
                write(jj - 1, 1).wait()

            gather(jj + 1, 1).start()
            write(jj, 0).start()
            gather(jj + 1, 1).wait()
            write(jj, 0).wait()

            @pl.when(jj + 2 < n_chunks)
            def _():
                gather(jj + 2, 0).start()

            write(jj + 1, 1).start()

        write(n_chunks - 1, 1).wait()

    return k(table, idx)


def _moe_kernel(be_ref, nu_ref, nv_ref, slot_ref, nxt_ref, x_ref, wgu_hbm, bgu_ref, wd_hbm, bd_ref,
                o_ref, wgu_st, wd_st, wgu_bf, wd_bf, xb_ref, sem):
    b = pl.program_id(0)
    e = be_ref[b]
    prev = be_ref[jnp.maximum(b - 1, 0)]
    active = b < nu_ref[0]
    changed = (b == 0) | (e != prev)

    def weight_copies(expert, s):
        return (pltpu.make_async_copy(wgu_hbm.at[expert], wgu_st.at[s], sem.at[0, s]),
                pltpu.make_async_copy(wd_hbm.at[expert], wd_st.at[s], sem.at[1, s]))

    @pl.when(active & changed)
    def _():
        s = slot_ref[b]

        @pl.when(b == 0)
        def _():
            for cp in weight_copies(e, s):
                cp.start()

        for cp in weight_copies(e, s):
            cp.wait()
        wgu_bf[...] = wgu_st[s].astype(BF16)
        wd_bf[...] = wd_st[s].astype(BF16)
        nxt = nxt_ref[b]

        @pl.when(nxt >= 0)
        def _():
            for cp in weight_copies(nxt, 1 - s):
                cp.start()

    @pl.when(active)
    def _():
        valid = lax.broadcasted_iota(jnp.int32, (MOE_ROWS, LANES), 0) < nv_ref[b]
        for j in range(ROW_TILES):
            for half, xj in enumerate(_load_row_tile(x_ref, j, MOE_ROWS)):
                c0 = half * HALF_MODEL + j * LANES
                xb_ref[:, c0:c0 + LANES] = jnp.where(valid, xj, 0.0).astype(BF16)
        gu = _dot(xb_ref[...], wgu_bf[...]) + bgu_ref[0]
        gate = jnp.minimum(gu[:, :D_EXPERT], SWIGLU_LIMIT)
        up = jnp.clip(gu[:, D_EXPERT:], -SWIGLU_LIMIT, SWIGLU_LIMIT)
        glu = gate * _sigmoid(gate * SWIGLU_ALPHA)
        act = ((up + 1.0) * glu).astype(BF16)
        _store_row_tiles(o_ref, _dot(act, wd_bf[...]) + bd_ref[0])


def _moe(tables, xs, w_gate_up, b_gate_up, w_down, b_down):
    p_rows = xs.shape[0] // ROW_TILES
    n_blocks = p_rows // MOE_ROWS

    def blk(b, be, nu, *_):
        return jnp.minimum(b, nu[0] - 1)

    def expert(b, be, nu, *_):
        return (be[blk(b, be, nu)], 0, 0)

    def rows(b, be, nu, *_):
        return (blk(b, be, nu), 0)

    grid_spec = pltpu.PrefetchScalarGridSpec(
        num_scalar_prefetch=len(tables),
        grid=(n_blocks,),
        in_specs=[
            pl.BlockSpec((MOE_ROWS * ROW_TILES, LANES), rows),
            pl.BlockSpec(memory_space=pl.ANY),
            pl.BlockSpec((1, 1, 2 * D_EXPERT), expert),
            pl.BlockSpec(memory_space=pl.ANY),
            pl.BlockSpec((1, 1, D_MODEL), expert),
        ],
        out_specs=pl.BlockSpec((MOE_ROWS * ROW_TILES, LANES), rows),
        scratch_shapes=[pltpu.VMEM((2, D_MODEL, 2 * D_EXPERT), F32),
                        pltpu.VMEM((2, D_EXPERT, D_MODEL), F32),
                        pltpu.VMEM((D_MODEL, 2 * D_EXPERT), BF16),
                        pltpu.VMEM((D_EXPERT, D_MODEL), BF16),
                        pltpu.VMEM((MOE_ROWS, D_MODEL), BF16),
                        pltpu.SemaphoreType.DMA((2, 2))],
    )
    return pl.pallas_call(
        _moe_kernel,
        grid_spec=grid_spec,
        out_shape=jax.ShapeDtypeStruct((p_rows * ROW_TILES, LANES), ROW_DTYPE),
        compiler_params=pltpu.CompilerParams(
            dimension_semantics=("arbitrary",), vmem_limit_bytes=VMEM_LIMIT),
        name="moe_grouped_mlp",
    )(*tables, xs, w_gate_up, b_gate_up.reshape(N_EXPERTS, 1, 2 * D_EXPERT), w_down,
      b_down.reshape(N_EXPERTS, 1, D_MODEL))


def _combine_kernel(x1_ref, y0_ref, y1_ref, y2_ref, y3_ref, rw_ref, mod_ref, g_ref, b_ref, o_ref,
                    *, alpha):
    rw = rw_ref[...]
    y_refs = (y0_ref, y1_ref, y2_ref, y3_ref)
    pieces = [None] * (2 * ROW_TILES)
    for j in range(ROW_TILES):
        for kk in range(TOP_K):
            for half, yj in enumerate(_load_row_tile(y_refs[kk], j, rw.shape[0])):
                term = rw[:, kk:kk + 1] * yj
                slot = half * ROW_TILES + j
                pieces[slot] = term if kk == 0 else pieces[slot] + term
    ff = jnp.concatenate(pieces, axis=-1)
    y = alpha * x1_ref[...] + mod_ref[0, 5:6, :] * ff
    o_ref[...] = _layer_norm(y) * g_ref[...] + b_ref[...]


def _combine(x1, yg, rw, mod, l2g, l2b, *, row0, n_rows, mod_map, alpha):
    tm = TM_MIX
    t0 = row0 // tm
    tiles = n_rows // tm
    kern = functools.partial(_combine_kernel, alpha=alpha)

    def y_spec(kk):
        return pl.BlockSpec((tm * ROW_TILES, LANES), lambda i: (kk * tiles + i, 0))

    return pl.pallas_call(
        kern,
        grid=(n_rows // tm,),
        in_specs=[pl.BlockSpec((tm, D_MODEL), lambda i: (t0 + i, 0))]
        + [y_spec(kk) for kk in range(TOP_K)]
        + [pl.BlockSpec((tm, LANES), lambda i: (t0 + i, 0)),
           pl.BlockSpec((1, N_MOD, D_MODEL), mod_map),
           pl.BlockSpec((1, D_MODEL), lambda i: (0, 0)),
           pl.BlockSpec((1, D_MODEL), lambda i: (0, 0))],
        out_specs=pl.BlockSpec((tm, D_MODEL), lambda i: (i, 0)),
        out_shape=jax.ShapeDtypeStruct((n_rows, D_MODEL), F32),
        compiler_params=pltpu.CompilerParams(
            dimension_semantics=("arbitrary",), vmem_limit_bytes=VMEM_LIMIT),
        name="combine_ln2",
    )(x1, yg, yg, yg, yg, rw, mod, l2g, l2b)


def _route_kernel(ridx_ref, dest_ref, cnt_ref, run_ref, bst_ref):
    phase = pl.program_id(0)
    i = pl.program_id(1)
    tm = ridx_ref.shape[0]
    ridx = ridx_ref[...]
    lane = lax.broadcasted_iota(jnp.int32, (tm, LANES), 1)
    hits = [ridx[:, kk:kk + 1] == lane for kk in range(TOP_K)]
    chosen = jnp.where(hits[0], 1.0, 0.0)
    for kk in range(1, TOP_K):
        chosen = chosen + jnp.where(hits[kk], 1.0, 0.0)
    colsum = jnp.sum(chosen, axis=0, keepdims=True)

    @pl.when((phase == 0) & (i == 0))
    def _():
        run_ref[...] = jnp.zeros_like(run_ref)

    @pl.when(phase == 0)
    def _():
        run_ref[...] = run_ref[...] + colsum

    @pl.when((phase == 1) & (i == 0))
    def _():
        counts = run_ref[...]
        cnt_ref[...] = counts
        blocks = jnp.floor((counts + (MOE_ROWS - 1.0)) * (1.0 / MOE_ROWS))
        r = lax.broadcasted_iota(jnp.int32, (LANES, LANES), 0)
        c = lax.broadcasted_iota(jnp.int32, (LANES, LANES), 1)
        before = jnp.dot(blocks, (r < c).astype(F32), precision=HIGHEST, preferred_element_type=F32)
        bst_ref[...] = before * float(MOE_ROWS)
        run_ref[...] = jnp.zeros_like(run_ref)

    @pl.when(phase == 1)
    def _():
        rt = lax.broadcasted_iota(jnp.int32, (tm, tm), 0)
        ct = lax.broadcasted_iota(jnp.int32, (tm, tm), 1)
        earlier = _dot((ct < rt).astype(BF16), chosen.astype(BF16))
        row_of = bst_ref[0:1, :] + run_ref[0:1, :] + earlier
        out = jnp.zeros((tm, LANES), F32)
        for kk in range(TOP_K):
            dk = jnp.sum(jnp.where(hits[kk], row_of, 0.0), axis=-1, keepdims=True)
            out = jnp.where(lane == kk, dk, out)
        dest_ref[...] = out.T[0:SUBLANES, :].astype(jnp.int32)
        run_ref[...] = run_ref[...] + colsum


def _route(ridx):
    t_all = ridx.shape[0]
    tm = TM_ROUTE
    return pl.pallas_call(
        _route_kernel,
        grid=(2, t_all // tm),
        in_specs=[pl.BlockSpec((tm, LANES), lambda p, i: (i, 0))],
        out_specs=[pl.BlockSpec((SUBLANES, tm), lambda p, i: (0, i * p)),
                   pl.BlockSpec((SUBLANES, LANES), lambda p, i: (0, 0))],
        out_shape=[jax.ShapeDtypeStruct((SUBLANES, t_all), jnp.int32),
                   jax.ShapeDtypeStruct((SUBLANES, LANES), F32)],
        scratch_shapes=[pltpu.VMEM((SUBLANES, LANES), F32),
                        pltpu.VMEM((SUBLANES, LANES), F32)],
        compiler_params=pltpu.CompilerParams(
            dimension_semantics=("arbitrary", "arbitrary"), vmem_limit_bytes=VMEM_LIMIT),
        name="moe_route",
    )(ridx)


def _routing_tables(counts, n_blocks):
    experts = jnp.arange(N_EXPERTS, dtype=jnp.int32)
    blocks_per = (counts + MOE_ROWS - 1) // MOE_ROWS
    bends = jnp.cumsum(blocks_per)
    bstarts = bends - blocks_per
    blocks = jnp.arange(n_blocks, dtype=jnp.int32)
    block_expert = jnp.minimum(
        jnp.sum((bends[None, :] <= blocks[:, None]).astype(jnp.int32), axis=1), N_EXPERTS - 1)
    n_used = bends[-1:].astype(jnp.int32)
    owner = block_expert[:, None] == experts[None, :]

    def per_block(table):
        return jnp.sum(jnp.where(owner, table[None, :], 0), axis=1)

    n_valid = jnp.clip(per_block(counts) - (blocks - per_block(bstarts)) * MOE_ROWS,
                       0, MOE_ROWS).astype(jnp.int32)
    present = blocks_per > 0
    ordinal = jnp.cumsum(present.astype(jnp.int32)) - 1
    later = lax.cummin(jnp.where(present, experts, N_EXPERTS), reverse=True)
    succ = jnp.concatenate([later[1:], jnp.full((1,), N_EXPERTS, jnp.int32)])
    succ = jnp.where(succ >= N_EXPERTS, -1, succ)
    stage_slot = (per_block(ordinal) % 2).astype(jnp.int32)
    next_expert = per_block(succ).astype(jnp.int32)
    return (block_expert.astype(jnp.int32), n_used, n_valid, stage_slot, next_expert)


def _pos_embed_2d(n_tokens):
    rows = n_tokens // GRID_W
    r = jnp.repeat(jnp.arange(rows), GRID_W).astype(F32)
    col = jnp.tile(jnp.arange(GRID_W), rows).astype(F32)
    quarter = D_MODEL // 4
    omega = 1.0 / (10000.0 ** (jnp.arange(quarter, dtype=F32) / quarter))
    er = r[:, None] * omega
    ec = col[:, None] * omega
    return jnp.concatenate([jnp.sin(er), jnp.cos(er), jnp.sin(ec), jnp.cos(ec)], axis=-1)


def _reorder_w_in(w):
    o_r = 2 * GLA_DK + 2 * GLA_DV
    o_f = o_r + DECAY_RANK
    o_gate = o_f + FNET_DIM
    pad = jnp.zeros((w.shape[0], LANES - DECAY_RANK), w.dtype)
    w_q = w[:, :GLA_DK] * (DK_HEAD ** -0.5)
    return jnp.concatenate([w_q, w[:, GLA_DK:o_r], w[:, o_gate:], w[:, o_f:o_gate], w[:, o_r:o_f], pad],
                           axis=1)


def kernel(x_prompt, x_sample, state_gla, c, c_ctx, w_ada, b_ada, w_in, w_dec_fwd, b_dec_fwd,
           w_dec_bwd, b_dec_bwd, gla_norm_g, w_br_gla, w_br_fnet, w_out, ln1_g, ln1_b, w_router,
           b_router, w_gate_up, b_gate_up, w_down, b_down, ln2_g, ln2_b):
    n_req, ctx_len, _ = x_prompt.shape
    n_lat, lat_len, _ = x_sample.shape
    depth = w_in.shape[0]
    alpha = (2.0 * depth) ** 0.25
    t_ctx = n_req * ctx_len
    t_lat = n_lat * lat_len
    t_all = t_ctx + t_lat

    x_ctx = x_prompt.reshape(t_ctx, D_MODEL)
    x_lat = x_sample.reshape(t_lat, D_MODEL)
    pos = _pos_embed_2d(lat_len)
    zero_pos = jnp.zeros_like(pos)

    cond_rows = -(-(n_lat + 1) // SUBLANES) * SUBLANES
    cond = jnp.zeros((cond_rows, D_MODEL), F32).at[:n_lat].set(c).at[cond_rows - 1].set(c_ctx)

    n_moe_blocks = (t_all * TOP_K) // MOE_ROWS + N_EXPERTS
    tok_chunks = t_all // (SC_WORKERS * SC_WINDOW)
    states = []
    for l in range(depth):
        mod = _ada(cond, w_ada[l], b_ada[l]).reshape(cond_rows, N_MOD, D_MODEL)
        layer_pos = pos if l == 0 else zero_pos
        proj = _inproj(x_ctx, x_lat, layer_pos, mod, _reorder_w_in(w_in[l]).astype(BF16), lat_len)

        def pad_dec(w):
            return jnp.zeros((LANES, GLA_DK), F32).at[:DECAY_RANK].set(w)

        dec = (pad_dec(w_dec_fwd[l]), b_dec_fwd[l].reshape(1, GLA_DK),
               pad_dec(w_dec_bwd[l]), b_dec_bwd[l].reshape(1, GLA_DK),
               gla_norm_g[l].reshape(1, DV_HEAD))
        o_ctx, s_new = _gla(proj, *dec, None, n_seq=n_req, seq_len=ctx_len, row0=0, emit_state=True)
        (o_lat,) = _gla(proj, *dec, state_gla[:, l], n_seq=n_lat, seq_len=lat_len, row0=t_ctx,
                        emit_state=False)
        states.append(s_new)

        mixed_ctx = _fnet(proj, n_seq=n_req, seq_len=ctx_len, row0=0)
        mixed_lat = _fnet(proj, n_seq=n_lat, seq_len=lat_len, row0=t_ctx)

        wr = jnp.zeros((D_MODEL, LANES), F32).at[:, :N_EXPERTS].set(w_router[l])
        br = jnp.full((1, LANES), -1e30, F32).at[0, :N_EXPERTS].set(b_router[l])
        wr_hi = wr.astype(BF16)
        wr_lo = (wr - wr_hi.astype(F32)).astype(BF16)
        x1, h2, ridx, rw = _merge(
            x_ctx, x_lat, layer_pos, mod, o_ctx, o_lat, mixed_ctx, mixed_lat, proj,
            w_br_gla[l].astype(BF16), w_br_fnet[l].astype(BF16), w_out[l].astype(BF16),
            ln1_g[l].reshape(1, D_MODEL), ln1_b[l].reshape(1, D_MODEL), wr_hi, wr_lo, br, lat_len,
            alpha)

        dest, counts = _route(ridx)
        moe_tables = _routing_tables(counts[0, :N_EXPERTS].astype(jnp.int32), n_moe_blocks)
        dest = dest[:TOP_K]
        scatter_idx = dest.reshape(TOP_K, SC_WORKERS, tok_chunks, SC_WINDOW)
        p_rows = n_moe_blocks * MOE_ROWS
        xs = _sc_scatter_rows(h2.reshape(t_all, ROW_TILES, LANES), scatter_idx, p_rows)
        yb = _moe(moe_tables, xs.reshape(p_rows * ROW_TILES, LANES),
                  w_gate_up[l], b_gate_up[l], w_down[l], b_down[l])
        yb = yb.reshape(p_rows, ROW_TILES, LANES)

        def gathered(row0, n_rows):
            idx = dest[:, row0:row0 + n_rows].reshape(SC_WORKERS, -1, SC_WINDOW)
            return _sc_gather_rows(yb, idx).reshape(TOP_K * n_rows * ROW_TILES, LANES)

        l2g = ln2_g[l].reshape(1, D_MODEL)
        l2b = ln2_b[l].reshape(1, D_MODEL)
        tiles_per_seq = lat_len // TM_MIX
        yg_ctx = gathered(0, t_ctx)
        yg_lat = gathered(t_ctx, t_lat)
        x_ctx = _combine(x1, yg_ctx, rw, mod, l2g, l2b, row0=0, n_rows=t_ctx,
                         mod_map=lambda i: (cond_rows - 1, 0, 0), alpha=alpha)
        x_lat = _combine(x1, yg_lat, rw, mod, l2g, l2b, row0=t_ctx, n_rows=t_lat,
                         mod_map=lambda i: (i // tiles_per_seq, 0, 0), alpha=alpha)

    y_prompt = x_ctx.reshape(x_prompt.shape)
    y_sample = x_lat.reshape(x_sample.shape)
    new_state = jnp.stack(states, axis=1).astype(x_prompt.dtype)
    return (y_prompt, y_sample, new_state)
```

```python
import functools
import math

import numpy as np
import jax
import jax.numpy as jnp
from jax import lax
from jax.experimental import pallas as pl
from jax.experimental.pallas import tpu as pltpu
from jax.experimental.pallas import tpu_sc as plsc

F32 = jnp.float32
BF16 = jnp.bfloat16

D_MODEL = 1024
GRID_W = 64
GLA_HEADS = 4
DK_HEAD = 128
DV_HEAD = 256
GLA_DK = GLA_HEADS * DK_HEAD
GLA_DV = GLA_HEADS * DV_HEAD
DECAY_RANK = 16
GATE_NORMALIZER = 16.0
FNET_GROUPS = 4
FNET_GROUP_DIM = 128
FNET_DIM = FNET_GROUPS * FNET_GROUP_DIM
N_EXPERTS = 32
TOP_K = 4
D_EXPERT = 1024
SWIGLU_LIMIT = 7.0
SWIGLU_ALPHA = 1.702
LN_EPS = 1e-6
N_MOD = 6

LANES = 128
SUBLANES = 8
HALF_MODEL = D_MODEL // 2
ROW_TILES = HALF_MODEL // LANES
ROW_DTYPE = jnp.uint32
COL_Q = 0
COL_K = GLA_DK
COL_V = 2 * GLA_DK
COL_G = COL_V + GLA_DV
COL_GATE_A = COL_G + GLA_DV
COL_GATE_B = COL_GATE_A + D_MODEL
COL_F = COL_GATE_B + D_MODEL
COL_R = COL_F + FNET_DIM
PROJ_COLS = COL_R + LANES

GLA_CHUNK = 128
GLA_LEAF = 16
TM_PROJ = 256
TM_MIX = 256
TM_MERGE = 512
TM_MERGE_GROUP = 128
TM_ROUTE = 1024
MOE_ROWS = 512
MOE_GROUP = 256
VMEM_LIMIT = 56 * 1024 * 1024

SC_CORES = 2
SC_SUBCORES = 16
SC_WORKERS = SC_CORES * SC_SUBCORES
SC_WINDOW = 64

HIGHEST = lax.Precision.HIGHEST


def _layer_norm(x):
    mu = jnp.mean(x, axis=-1, keepdims=True)
    xc = x - mu
    var = jnp.mean(xc * xc, axis=-1, keepdims=True)
    return xc * lax.rsqrt(var + LN_EPS)


def _sigmoid(x):
    return 0.5 * jnp.tanh(0.5 * x) + 0.5


def _log_sigmoid(z):
    return jnp.minimum(z, 0.0) - jnp.log(1.0 + jnp.exp(-jnp.abs(z)))


def _dot(a, b):
    return jnp.dot(a, b, preferred_element_type=F32)


def _split_bf16(x, terms):
    parts = []
    for _ in range(terms):
        p = x.astype(BF16)
        parts.append(p)
        x = x - p.astype(F32)
    return parts


def _dot_nt(a, b):
    return lax.dot_general(a, b, (((1,), (1,)), ((), ())), preferred_element_type=F32)


def _dot_tn(a, b):
    return lax.dot_general(a, b, (((0,), (0,)), ((), ())), preferred_element_type=F32)


def _row_tile_slice(j, n_rows, first_row=0):
    return pl.ds(first_row * ROW_TILES + j, n_rows, stride=ROW_TILES)


def _store_row_tiles(ref, val, first_row=0):
    for j in range(ROW_TILES):
        lo = val[:, j * LANES:(j + 1) * LANES]
        hi = val[:, HALF_MODEL + j * LANES:HALF_MODEL + (j + 1) * LANES]
        ref[_row_tile_slice(j, val.shape[0], first_row), :] = pltpu.pack_elementwise(
            [lo, hi], packed_dtype=BF16)


def _load_row_tile(ref, j, n_rows, first_row=0):
    words = ref[_row_tile_slice(j, n_rows, first_row), :]
    return tuple(pltpu.unpack_elementwise(words, index=half, packed_dtype=BF16, unpacked_dtype=F32)
                 for half in range(2))


def _ada_kernel(c_ref, w_ref, b_ref, o_ref):
    c = c_ref[...]
    s = c * _sigmoid(c)
    o_ref[...] = _dot(s.astype(BF16), w_ref[...].astype(BF16)) + b_ref[...]


def _ada(cond, w_ada, b_ada):
    rows = cond.shape[0]
    n = w_ada.shape[1]
    tn = 1536
    return pl.pallas_call(
        _ada_kernel,
        grid=(n // tn,),
        in_specs=[pl.BlockSpec((rows, D_MODEL), lambda j: (0, 0)),
                  pl.BlockSpec((D_MODEL, tn), lambda j: (0, j)),
                  pl.BlockSpec((1, tn), lambda j: (0, j))],
        out_specs=pl.BlockSpec((rows, tn), lambda j: (0, j)),
        out_shape=jax.ShapeDtypeStruct((rows, n), F32),
        compiler_params=pltpu.CompilerParams(vmem_limit_bytes=VMEM_LIMIT),
        name="ada_mod",
    )(cond, w_ada, b_ada.reshape(1, n))


def _group_maps(n_ctx_tiles):
    def ctx_map(i, *_):
        return (jnp.minimum(i, n_ctx_tiles - 1), 0)

    def lat_map(i, *_):
        return (jnp.maximum(i - n_ctx_tiles, 0), 0)

    return ctx_map, lat_map


def _token_specs(tm, n_ctx_tiles, tiles_per_latent_seq, ctx_mod_row):
    ctx_map, lat_map = _group_maps(n_ctx_tiles)

    def pos_map(i, *_):
        return (jnp.maximum(i - n_ctx_tiles, 0) % tiles_per_latent_seq, 0)

    def mod_map(i, *_):
        return (jnp.where(i < n_ctx_tiles, ctx_mod_row,
                          jnp.maximum(i - n_ctx_tiles, 0) // tiles_per_latent_seq), 0, 0)

    return [pl.BlockSpec((tm, D_MODEL), ctx_map),
            pl.BlockSpec((tm, D_MODEL), lat_map),
            pl.BlockSpec((tm, D_MODEL), pos_map),
            pl.BlockSpec((1, N_MOD, D_MODEL), mod_map)]


def _inproj_kernel(xc_ref, xl_ref, pos_ref, mod_ref, w_ref, o_ref, *, n_ctx_tiles):
    i = pl.program_id(0)

    def project(x):
        h = _layer_norm(x) * (1.0 + mod_ref[0, 1:2, :]) + mod_ref[0, 0:1, :]
        o_ref[...] = _dot(h.astype(BF16), w_ref[...])

    @pl.when(i < n_ctx_tiles)
    def _():
        project(xc_ref[...])

    @pl.when(i >= n_ctx_tiles)
    def _():
        project(xl_ref[...] + pos_ref[...])


def _inproj(x_ctx, x_lat, pos, mod, w_in_bf, lat_len):
    t_ctx, t_lat = x_ctx.shape[0], x_lat.shape[0]
    n_ctx_tiles = t_ctx // TM_PROJ
    n_tiles = (t_ctx + t_lat) // TM_PROJ
    kern = functools.partial(_inproj_kernel, n_ctx_tiles=n_ctx_tiles)
    specs = _token_specs(TM_PROJ, n_ctx_tiles, lat_len // TM_PROJ, mod.shape[0] - 1)
    return pl.pallas_call(
        kern,
        grid=(n_tiles,),
        in_specs=specs + [pl.BlockSpec((D_MODEL, PROJ_COLS), lambda i: (0, 0),
                                       pipeline_mode=pl.Buffered(1))],
        out_specs=pl.BlockSpec((TM_PROJ, PROJ_COLS), lambda i: (i, 0)),
        out_shape=jax.ShapeDtypeStruct((t_ctx + t_lat, PROJ_COLS), F32),
        compiler_params=pltpu.CompilerParams(
            dimension_semantics=("arbitrary",), vmem_limit_bytes=VMEM_LIMIT),
        name="ln_inproj",
    )(x_ctx, x_lat, pos, mod, w_in_bf)


def _gla_kernel(*refs, seq_len, has_s0, emit_state):
    it = iter(refs)
    q_ref, k_ref, v_ref, r_ref = next(it), next(it), next(it), next(it)
    wdf_ref, bdf_ref, wdb_ref, bdb_ref, g_ref = next(it), next(it), next(it), next(it), next(it)
    s0_ref = next(it) if has_s0 else None
    o_ref = next(it)
    sout_ref = next(it) if emit_state else None
    cum_ref, a_ref, qi_ref, ko_ref, dec_ref, op_ref, st_ref = (next(it) for _ in range(7))

    C = GLA_CHUNK
    n_chunks = seq_len // C

    def rows(n):
        if isinstance(n, int):
            return pl.ds(n * C, C)
        return pl.ds(pl.multiple_of(n * C, C), C)

    def dec_rows(n):
        if isinstance(n, int):
            return pl.ds(n * SUBLANES, SUBLANES)
        return pl.ds(pl.multiple_of(n * SUBLANES, SUBLANES), SUBLANES)

    def loop(body):
        if n_chunks <= 2:
            for n in range(n_chunks):
                body(n)
        else:
            def step(m, carry):
                body(2 * m)
                body(2 * m + 1)
                return carry
            lax.fori_loop(0, n_chunks // 2, step, 0)

    rt = lax.broadcasted_iota(jnp.int32, (C, C), 0)
    ct = lax.broadcasted_iota(jnp.int32, (C, C), 1)
    tri = ((rt >= ct).astype(BF16), (ct >= rt).astype(BF16))
    row_id = lax.broadcasted_iota(jnp.int32, (C, DK_HEAD), 0)

    r_hi, r_lo = _split_bf16(r_ref[...], 2)
    for d, (w_ref, b_ref) in enumerate(((wdf_ref, bdf_ref), (wdb_ref, bdb_ref))):
        w_hi, w_lo = _split_bf16(w_ref[...], 2)
        z = _dot(r_hi, w_hi) + _dot(r_lo, w_hi) + _dot(r_hi, w_lo) + b_ref[...]
        cum_ref[d] = _log_sigmoid(z) * (1.0 / GATE_NORMALIZER)

    def cumsum_chunk(n):
        for d in range(2):
            la_hi, la_lo = _split_bf16(cum_ref[d, rows(n), :], 2)
            cum_ref[d, rows(n), :] = _dot(tri[d], la_hi) + _dot(tri[d], la_lo)

    loop(cumsum_chunk)

    query_rows = ({}, {})
    keep = ({}, {})
    for d in range(2):
        blk = C // 2
        while blk >= GLA_LEAF:
            q_parity = 1 if d == 0 else 0
            query_rows[d][blk] = ((row_id // blk) % 2) == q_parity
            qb, kb = rt // blk, ct // blk
            keep[d][blk] = ((qb % 2) == q_parity) & ((qb == kb + 1) if d == 0 else (kb == qb + 1))
            blk //= 2
        order = (rt >= ct) if d == 0 else (ct >= rt)
        keep[d][0] = ((rt // GLA_LEAF) == (ct // GLA_LEAF)) & order

    def block_rows(cum, first, step, count):
        span = C // count
        parts = [jnp.broadcast_to(cum[first + p * step:first + p * step + 1, :], (span, DK_HEAD))
                 for p in range(count)]
        return parts[0] if count == 1 else jnp.concatenate(parts, axis=0)

    def scores(n, d):
        cum = cum_ref[d, rows(n), :]
        q = q_ref[rows(n), :]
        k = k_ref[rows(n), :]
        acc = None
        blk = C // 2
        while blk >= GLA_LEAF:
            pairs = C // (2 * blk)
            bnd = blk - 1 if d == 0 else blk
            w = jnp.exp(-jnp.abs(cum - block_rows(cum, bnd, 2 * blk, pairs)))
            x = (jnp.where(query_rows[d][blk], q, k) * w).astype(BF16)
            s = jnp.where(keep[d][blk], _dot_nt(x, x), 0.0)
            acc = s if acc is None else acc + s
            blk //= 2
        mid = GLA_LEAF // 2 - 1 if d == 0 else GLA_LEAF // 2
        e = cum - block_rows(cum, mid, GLA_LEAF, C // GLA_LEAF)
        s = _dot_nt((q * jnp.exp(e)).astype(BF16), (k * jnp.exp(-e)).astype(BF16))
        acc = acc + jnp.where(keep[d][0], s, 0.0)
        a_ref[d, rows(n), :] = acc.astype(BF16)
        end = cum[C - 1:C, :] if d == 0 else cum[0:1, :]
        qi_ref[d, rows(n), :] = (q * jnp.exp(cum)).astype(BF16)
        ko_ref[d, rows(n), :] = (k * jnp.exp(end - cum)).astype(BF16)
        dec_ref[d, dec_rows(n), :] = jnp.broadcast_to(jnp.exp(end), (SUBLANES, DK_HEAD))

    def scores_chunk(n):
        scores(n, 0)
        scores(n, 1)

    loop(scores_chunk)

    for d in range(2):
        if has_s0:
            st_ref[d] = s0_ref[0, d, 0].T
        else:
            st_ref[d] = jnp.zeros((DV_HEAD, DK_HEAD), F32)

    def scan(n, d):
        v = v_ref[rows(n), :].astype(BF16)
        st = st_ref[d]
        o = _dot(a_ref[d, rows(n), :], v) + _dot_nt(qi_ref[d, rows(n), :], st.astype(BF16))
        st_ref[d] = st * dec_ref[d, dec_rows(n), :][0:1, :] + _dot_tn(v, ko_ref[d, rows(n), :])
        op_ref[d, rows(n), :] = o

    def scan_chunk(m):
        scan(m, 0)
        scan(n_chunks - 1 - m, 1)

    loop(scan_chunk)
    if emit_state:
        sout_ref[0, 0, 0] = st_ref[0].T
        sout_ref[0, 1, 0] = st_ref[1].T

    g = g_ref[...]

    def finish_chunk(n):
        o = op_ref[0, rows(n), :] + op_ref[1, rows(n), :]
        ms = jnp.mean(o * o, axis=-1, keepdims=True)
        o_ref[rows(n), :] = o * lax.rsqrt(ms + LN_EPS) * g

    loop(finish_chunk)


def _gla(proj, wdf, bdf, wdb, bdb, g, s0, *, n_seq, seq_len, row0, emit_state):
    has_s0 = s0 is not None
    blk0 = row0 // seq_len
    kern = functools.partial(_gla_kernel, seq_len=seq_len, has_s0=has_s0, emit_state=emit_state)
    in_specs = [
        pl.BlockSpec((seq_len, DK_HEAD), lambda b, h: (blk0 + b, COL_Q // DK_HEAD + h)),
        pl.BlockSpec((seq_len, DK_HEAD), lambda b, h: (blk0 + b, COL_K // DK_HEAD + h)),
        pl.BlockSpec((seq_len, DV_HEAD), lambda b, h: (blk0 + b, COL_V // DV_HEAD + h)),
        pl.BlockSpec((seq_len, LANES), lambda b, h: (blk0 + b, COL_R // LANES)),
        pl.BlockSpec((LANES, DK_HEAD), lambda b, h: (0, h)),
        pl.BlockSpec((1, DK_HEAD), lambda b, h: (0, h)),
        pl.BlockSpec((LANES, DK_HEAD), lambda b, h: (0, h)),
        pl.BlockSpec((1, DK_HEAD), lambda b, h: (0, h)),
        pl.BlockSpec((1, DV_HEAD), lambda b, h: (0, 0)),
    ]
    args = [proj, proj, proj, proj, wdf, bdf, wdb, bdb, g]
    if has_s0:
        in_specs.append(pl.BlockSpec((1, 2, 1, DK_HEAD, DV_HEAD), lambda b, h: (b, 0, h, 0, 0)))
        args.append(s0)
    out_specs = [pl.BlockSpec((seq_len, DV_HEAD), lambda b, h: (b, h))]
    out_shape = [jax.ShapeDtypeStruct((n_seq * seq_len, GLA_DV), F32)]
    if emit_state:
        out_specs.append(pl.BlockSpec((1, 2, 1, DK_HEAD, DV_HEAD), lambda b, h: (b, 0, h, 0, 0)))
        out_shape.append(jax.ShapeDtypeStruct((n_seq, 2, GLA_HEADS, DK_HEAD, DV_HEAD), F32))

    res = pl.pallas_call(
        kern,
        grid=(n_seq, GLA_HEADS),
        in_specs=in_specs,
        out_specs=out_specs,
        out_shape=out_shape,
        scratch_shapes=[pltpu.VMEM((2, seq_len, DK_HEAD), F32),
                        pltpu.VMEM((2, seq_len, GLA_CHUNK), BF16),
                        pltpu.VMEM((2, seq_len, DK_HEAD), BF16),
                        pltpu.VMEM((2, seq_len, DK_HEAD), BF16),
                        pltpu.VMEM((2, seq_len // GLA_CHUNK * SUBLANES, DK_HEAD), F32),
                        pltpu.VMEM((2, seq_len, DV_HEAD), F32),
                        pltpu.VMEM((2, DV_HEAD, DK_HEAD), F32)],
        compiler_params=pltpu.CompilerParams(
            dimension_semantics=("arbitrary", "arbitrary"), vmem_limit_bytes=VMEM_LIMIT),
        name="gla_seq%d" % seq_len,
    )(*args)
    return res


def _fnet_kernel(f_ref, cl_ref, sl_ref, cg_ref, sg_ref, o_ref, uc_ref, us_ref, *, seq_len):
    cg = cg_ref[...]
    sg = sg_ref[...]
    for grp in range(FNET_GROUPS):
        lo = grp * FNET_GROUP_DIM
        u = f_ref[:, lo:lo + FNET_GROUP_DIM].astype(BF16)
        uc_ref[:, lo:lo + FNET_GROUP_DIM] = _dot(u, cg).astype(BF16)
        us_ref[:, lo:lo + FNET_GROUP_DIM] = _dot(u, sg).astype(BF16)
    mixed = _dot(cl_ref[...], uc_ref[...]) - _dot(sl_ref[...], us_ref[...])
    o_ref[...] = mixed * (1.0 / math.sqrt(seq_len * FNET_GROUP_DIM))


def _dft_mats(n):
    j = np.arange(n, dtype=np.int64)
    ang = (2.0 * np.pi / n) * ((j[:, None] * j[None, :]) % n).astype(np.float64)
    return (jnp.asarray(np.cos(ang), dtype=F32).astype(BF16),
            jnp.asarray(np.sin(ang), dtype=F32).astype(BF16))


def _fnet(proj, *, n_seq, seq_len, row0):
    blk0 = row0 // seq_len
    cl, sl = _dft_mats(seq_len)
    cg, sg = _dft_mats(FNET_GROUP_DIM)
    kern = functools.partial(_fnet_kernel, seq_len=seq_len)
    return pl.pallas_call(
        kern,
        grid=(n_seq,),
        in_specs=[pl.BlockSpec((seq_len, FNET_DIM), lambda b: (blk0 + b, COL_F // FNET_DIM)),
                  pl.BlockSpec((seq_len, seq_len), lambda b: (0, 0)),
                  pl.BlockSpec((seq_len, seq_len), lambda b: (0, 0)),
                  pl.BlockSpec((FNET_GROUP_DIM, FNET_GROUP_DIM), lambda b: (0, 0)),
                  pl.BlockSpec((FNET_GROUP_DIM, FNET_GROUP_DIM), lambda b: (0, 0))],
        out_specs=pl.BlockSpec((seq_len, FNET_DIM), lambda b: (b, 0)),
        out_shape=jax.ShapeDtypeStruct((n_seq * seq_len, FNET_DIM), F32),
        scratch_shapes=[pltpu.VMEM((seq_len, FNET_DIM), BF16),
                        pltpu.VMEM((seq_len, FNET_DIM), BF16)],
        compiler_params=pltpu.CompilerParams(
            dimension_semantics=("arbitrary",), vmem_limit_bytes=VMEM_LIMIT),
        name="fnet_seq%d" % seq_len,
    )(proj, cl, sl, cg, sg)


def _merge_kernel(xc_ref, xl_ref, pos_ref, mod_ref, oc_ref, ol_ref, mc_ref, ml_ref,
                  g_ref, ga_ref, gb_ref, wbg_ref, wbf_ref, wo_ref, l1g_ref, l1b_ref, wrh_ref, wrl_ref,
                  br_ref, x1_ref, h2_ref, ridx_ref, rw_ref, *, n_ctx_tiles, alpha):
    i = pl.program_id(0)

    def compute(rows, x, o, mx):
        g = g_ref[rows, :]
        a = (o * (g * _sigmoid(g))).astype(BF16)
        gla_out = _dot(a, wbg_ref[...])
        fnet_out = _dot(mx.astype(BF16), wbf_ref[...])
        merged = _sigmoid(ga_ref[rows, :]) * gla_out + _sigmoid(gb_ref[rows, :]) * fnet_out
        mix = _dot(merged.astype(BF16), wo_ref[...])
        y = alpha * x + mod_ref[0, 2:3, :] * mix
        x1 = _layer_norm(y) * l1g_ref[...] + l1b_ref[...]
        x1_ref[rows, :] = x1
        h2 = _layer_norm(x1) * (1.0 + mod_ref[0, 4:5, :]) + mod_ref[0, 3:4, :]
        _store_row_tiles(h2_ref, h2, rows.start)

        h_hi, h_lo = _split_bf16(h2, 2)
        logits = (_dot(h_hi, wrh_ref[...]) + _dot(h_lo, wrh_ref[...]) + _dot(h_hi, wrl_ref[...])
                  + br_ref[...])
        lane_i = lax.broadcasted_iota(jnp.int32, logits.shape, 1)
        lane = lane_i.astype(F32)
        idx_out = jnp.zeros(logits.shape, F32)
        val_out = jnp.zeros(logits.shape, F32)
        top0 = None
        denom = None
        for kk in range(TOP_K):
            m = jnp.max(logits, axis=-1, keepdims=True)
            sel = jnp.min(jnp.where(logits == m, lane, float(LANES)), axis=-1, keepdims=True)
            if kk == 0:
                top0 = m
                p = jnp.ones_like(m)
                denom = p
            else:
                p = jnp.exp(m - top0)
                denom = denom + p
            idx_out = jnp.where(lane_i == kk, sel, idx_out)
            val_out = jnp.where(lane_i == kk, p, val_out)
            logits = jnp.where(lane == sel, -jnp.inf, logits)
        ridx_ref[rows, :] = idx_out.astype(jnp.int32)
        rw_ref[rows, :] = val_out / denom

    tm = x1_ref.shape[0]
    groups = [slice(r0, r0 + TM_MERGE_GROUP) for r0 in range(0, tm, TM_MERGE_GROUP)]

    @pl.when(i < n_ctx_tiles)
    def _():
        for rows in groups:
            compute(rows, xc_ref[rows, :], oc_ref[rows, :], mc_ref[rows, :])

    @pl.when(i >= n_ctx_tiles)
    def _():
        for rows in groups:
            compute(rows, xl_ref[rows, :] + pos_ref[rows, :], ol_ref[rows, :], ml_ref[rows, :])


def _merge(x_ctx, x_lat, pos, mod, o_ctx, o_lat, mixed_ctx, mixed_lat, proj,
           wbg, wbf, wo, l1g, l1b, wr_hi, wr_lo, br, lat_len, alpha):
    t_ctx, t_lat = x_ctx.shape[0], x_lat.shape[0]
    t_all = t_ctx + t_lat
    tm = TM_MERGE
    n_ctx_tiles = t_ctx // tm
    kern = functools.partial(_merge_kernel, n_ctx_tiles=n_ctx_tiles, alpha=alpha)
    specs = _token_specs(tm, n_ctx_tiles, lat_len // tm, mod.shape[0] - 1)
    ctx_map, lat_map = _group_maps(n_ctx_tiles)

    def const(shape):
        return pl.BlockSpec(shape, lambda i: (0,) * len(shape))

    in_specs = specs + [
        pl.BlockSpec((tm, GLA_DV), ctx_map),
        pl.BlockSpec((tm, GLA_DV), lat_map),
        pl.BlockSpec((tm, FNET_DIM), ctx_map),
        pl.BlockSpec((tm, FNET_DIM), lat_map),
        pl.BlockSpec((tm, GLA_DV), lambda i: (i, COL_G // GLA_DV)),
        pl.BlockSpec((tm, D_MODEL), lambda i: (i, COL_GATE_A // D_MODEL)),
        pl.BlockSpec((tm, D_MODEL), lambda i: (i, COL_GATE_B // D_MODEL)),
        const((GLA_DV, D_MODEL)), const((FNET_DIM, D_MODEL)), const((D_MODEL, D_MODEL)),
        const((1, D_MODEL)), const((1, D_MODEL)),
        const((D_MODEL, LANES)), const((D_MODEL, LANES)), const((1, LANES)),
    ]
    out_specs = [pl.BlockSpec((tm, D_MODEL), lambda i: (i, 0)),
                 pl.BlockSpec((tm * ROW_TILES, LANES), lambda i: (i, 0)),
                 pl.BlockSpec((tm, LANES), lambda i: (i, 0)),
                 pl.BlockSpec((tm, LANES), lambda i: (i, 0))]
    out_shape = [jax.ShapeDtypeStruct((t_all, D_MODEL), F32),
                 jax.ShapeDtypeStruct((t_all * ROW_TILES, LANES), ROW_DTYPE),
                 jax.ShapeDtypeStruct((t_all, LANES), jnp.int32),
                 jax.ShapeDtypeStruct((t_all, LANES), F32)]
    return pl.pallas_call(
        kern,
        grid=(t_all // tm,),
        in_specs=in_specs,
        out_specs=out_specs,
        out_shape=out_shape,
        compiler_params=pltpu.CompilerParams(
            dimension_semantics=("arbitrary",), vmem_limit_bytes=VMEM_LIMIT),
        name="merge_ln1_router",
    )(x_ctx, x_lat, pos, mod, o_ctx, o_lat, mixed_ctx, mixed_lat, proj, proj, proj,
      wbg, wbf, wo, l1g, l1b, wr_hi, wr_lo, br)


def _sc_mesh():
    return plsc.VectorSubcoreMesh(core_axis_name="c", subcore_axis_name="s")


def _sc_worker_id():
    return lax.axis_index("s") * SC_CORES + lax.axis_index("c")


def _sc_scatter_rows(src, idx, n_out):
    n_src = src.shape[0]
    w = SC_WINDOW
    n_chunks = n_src // (SC_WORKERS * w)
    copies = idx.shape[0]
    assert n_chunks % 2 == 0 and idx.shape == (copies, SC_WORKERS, n_chunks, w)

    @functools.partial(
        pl.kernel, mesh=_sc_mesh(),
        out_type=jax.ShapeDtypeStruct((n_out, ROW_TILES, LANES), ROW_DTYPE),
        scratch_types=[pltpu.VMEM((copies * n_chunks, w), jnp.int32),
                       pltpu.VMEM((2, w, ROW_TILES, LANES), ROW_DTYPE),
                       pltpu.SemaphoreType.DMA((2,)),
                       pltpu.SemaphoreType.DMA((2,))],
        name="moe_dispatch_scatter")
    def k(src_hbm, idx_hbm, out_hbm, idx_v, rows_v, rsem, wsem):
        wid = _sc_worker_id()
        base = wid * (n_chunks * w)
        for kk in range(copies):
            pltpu.sync_copy(idx_hbm.at[kk, wid], idx_v.at[pl.ds(kk * n_chunks, n_chunks)])

        def read(j, slot):
            return pltpu.make_async_copy(src_hbm.at[pl.ds(base + j * w, w)], rows_v.at[slot],
                                         rsem.at[slot])

        def scatter(j, kk, slot):
            return pltpu.make_async_copy(rows_v.at[slot], out_hbm.at[idx_v.at[kk * n_chunks + j]],
                                         wsem.at[slot])

        read(0, 0).start()

        @pl.loop(0, n_chunks, step=2)
        def _(jj):
            read(jj, 0).wait()

            @pl.when(jj > 0)
            def _():
                for kk in range(copies):
                    scatter(jj - 1, kk, 1).wait()

            read(jj + 1, 1).start()
            for kk in range(copies):
                scatter(jj, kk, 0).start()
            read(jj + 1, 1).wait()
            for kk in range(copies):
                scatter(jj, kk, 0).wait()

            @pl.when(jj + 2 < n_chunks)
            def _():
                read(jj + 2, 0).start()

            for kk in range(copies):
                scatter(jj + 1, kk, 1).start()

        for kk in range(copies):
            scatter(n_chunks - 1, kk, 1).wait()

    return k(src, idx)


def _sc_gather_rows(table, idx):
    _, n_chunks, w = idx.shape
    assert n_chunks % 2 == 0 and idx.shape[0] == SC_WORKERS and w == SC_WINDOW
    n_out = SC_WORKERS * n_chunks * w

    @functools.partial(
        pl.kernel, mesh=_sc_mesh(),
        out_type=jax.ShapeDtypeStruct((n_out, ROW_TILES, LANES), ROW_DTYPE),
        scratch_types=[pltpu.VMEM((n_chunks, w), jnp.int32),
                       pltpu.VMEM((2, w, ROW_TILES, LANES), ROW_DTYPE),
                       pltpu.SemaphoreType.DMA((2,)),
                       pltpu.SemaphoreType.DMA((2,))],
        name="moe_combine_gather")
    def k(table_hbm, idx_hbm, out_hbm, idx_v, rows_v, gsem, wsem):
        wid = _sc_worker_id()
        base = wid * (n_chunks * w)
        pltpu.sync_copy(idx_hbm.at[wid], idx_v)

        def gather(j, slot):
            return pltpu.make_async_copy(table_hbm.at[idx_v.at[j]], rows_v.at[slot], gsem.at[slot])

        def write(j, slot):
            return pltpu.make_async_copy(rows_v.at[slot], out_hbm.at[pl.ds(base + j * w, w)],
                                         wsem.at[slot])

        gather(0, 0).start()

        @pl.loop(0, n_chunks, step=2)
        def _(jj):
            gather(jj, 0).wait()

            @pl.when(jj > 0)
            def _():
                write(jj - 1, 1).wait()

            gather(jj + 1, 1).start()
            write(jj, 0).start()
            gather(jj + 1, 1).wait()
            write(jj, 0).wait()

            @pl.when(jj + 2 < n_chunks)
            def _():
                gather(jj + 2, 0).start()

            write(jj + 1, 1).start()

        write(n_chunks - 1, 1).wait()

    return k(table, idx)


def _moe_kernel(be_ref, nu_ref, nv_ref, slot_ref, nxt_ref, x_ref, wgu_hbm, bgu_ref, wd_hbm, bd_ref,
                o_ref, wgu_st, wd_st, wgu_bf, wd_bf, xb_ref, sem):
    b = pl.program_id(0)
    e = be_ref[b]
    prev = be_ref[jnp.maximum(b - 1, 0)]
    active = b < nu_ref[0]
    changed = (b == 0) | (e != prev)

    def weight_copies(expert, s):
        return (pltpu.make_async_copy(wgu_hbm.at[expert], wgu_st.at[s], sem.at[0, s]),
                pltpu.make_async_copy(wd_hbm.at[expert], wd_st.at[s], sem.at[1, s]))

    @pl.when(active & changed)
    def _():
        s = slot_ref[b]

        @pl.when(b == 0)
        def _():
            for cp in weight_copies(e, s):
                cp.start()

        for cp in weight_copies(e, s):
            cp.wait()
        wgu_bf[...] = wgu_st[s].astype(BF16)
        wd_bf[...] = wd_st[s].astype(BF16)
        nxt = nxt_ref[b]

        @pl.when(nxt >= 0)
        def _():
            for cp in weight_copies(nxt, 1 - s):
                cp.start()

    @pl.when(active)
    def _():
        n_valid = nv_ref[b]
        for r0 in range(0, MOE_ROWS, MOE_GROUP):
            valid = lax.broadcasted_iota(jnp.int32, (MOE_GROUP, LANES), 0) < n_valid - r0
            for j in range(ROW_TILES):
                for half, xj in enumerate(_load_row_tile(x_ref, j, MOE_GROUP, r0)):
                    c0 = half * HALF_MODEL + j * LANES
                    xb_ref[r0:r0 + MOE_GROUP, c0:c0 + LANES] = jnp.where(valid, xj, 0.0).astype(BF16)
            gu = _dot(xb_ref[r0:r0 + MOE_GROUP, :], wgu_bf[...]) + bgu_ref[0]
            gate = jnp.minimum(gu[:, :D_EXPERT], SWIGLU_LIMIT)
            up = jnp.clip(gu[:, D_EXPERT:], -SWIGLU_LIMIT, SWIGLU_LIMIT)
            glu = gate * _sigmoid(gate * SWIGLU_ALPHA)
            act = ((up + 1.0) * glu).astype(BF16)
            _store_row_tiles(o_ref, _dot(act, wd_bf[...]) + bd_ref[0], r0)


def _moe(tables, xs, w_gate_up, b_gate_up, w_down, b_down):
    p_rows = xs.shape[0] // ROW_TILES
    n_blocks = p_rows // MOE_ROWS

    def blk(b, be, nu, *_):
        return jnp.minimum(b, nu[0] - 1)

    def expert(b, be, nu, *_):
        return (be[blk(b, be, nu)], 0, 0)

    def rows(b, be, nu, *_):
        return (blk(b, be, nu), 0)

    grid_spec = pltpu.PrefetchScalarGridSpec(
        num_scalar_prefetch=len(tables),
        grid=(n_blocks,),
        in_specs=[
            pl.BlockSpec((MOE_ROWS * ROW_TILES, LANES), rows),
            pl.BlockSpec(memory_space=pl.ANY),
            pl.BlockSpec((1, 1, 2 * D_EXPERT), expert),
            pl.BlockSpec(memory_space=pl.ANY),
            pl.BlockSpec((1, 1, D_MODEL), expert),
        ],
        out_specs=pl.BlockSpec((MOE_ROWS * ROW_TILES, LANES), rows),
        scratch_shapes=[pltpu.VMEM((2, D_MODEL, 2 * D_EXPERT), F32),
                        pltpu.VMEM((2, D_EXPERT, D_MODEL), F32),
                        pltpu.VMEM((D_MODEL, 2 * D_EXPERT), BF16),
                        pltpu.VMEM((D_EXPERT, D_MODEL), BF16),
                        pltpu.VMEM((MOE_ROWS, D_MODEL), BF16),
                        pltpu.SemaphoreType.DMA((2, 2))],
    )
    return pl.pallas_call(
        _moe_kernel,
        grid_spec=grid_spec,
        out_shape=jax.ShapeDtypeStruct((p_rows * ROW_TILES, LANES), ROW_DTYPE),
        compiler_params=pltpu.CompilerParams(
            dimension_semantics=("arbitrary",), vmem_limit_bytes=VMEM_LIMIT),
        name="moe_grouped_mlp",
    )(*tables, xs, w_gate_up, b_gate_up.reshape(N_EXPERTS, 1, 2 * D_EXPERT), w_down,
      b_down.reshape(N_EXPERTS, 1, D_MODEL))


def _combine_kernel(x1_ref, y0_ref, y1_ref, y2_ref, y3_ref, rw_ref, mod_ref, g_ref, b_ref, o_ref,
                    *, alpha):
    rw = rw_ref[...]
    y_refs = (y0_ref, y1_ref, y2_ref, y3_ref)
    pieces = [None] * (2 * ROW_TILES)
    for j in range(ROW_TILES):
        for kk in range(TOP_K):
            for half, yj in enumerate(_load_row_tile(y_refs[kk], j, rw.shape[0])):
                term = rw[:, kk:kk + 1] * yj
                slot = half * ROW_TILES + j
                pieces[slot] = term if kk == 0 else pieces[slot] + term
    ff = jnp.concatenate(pieces, axis=-1)
    y = alpha * x1_ref[...] + mod_ref[0, 5:6, :] * ff
    o_ref[...] = _layer_norm(y) * g_ref[...] + b_ref[...]


def _combine(x1, yg, rw, mod, l2g, l2b, *, row0, n_rows, mod_map, alpha):
    tm = TM_MIX
    t0 = row0 // tm
    tiles = n_rows // tm
    kern = functools.partial(_combine_kernel, alpha=alpha)

    def y_spec(kk):
        return pl.BlockSpec((tm * ROW_TILES, LANES), lambda i: (kk * tiles + i, 0))

    return pl.pallas_call(
        kern,
        grid=(n_rows // tm,),
        in_specs=[pl.BlockSpec((tm, D_MODEL), lambda i: (t0 + i, 0))]
        + [y_spec(kk) for kk in range(TOP_K)]
        + [pl.BlockSpec((tm, LANES), lambda i: (t0 + i, 0)),
           pl.BlockSpec((1, N_MOD, D_MODEL), mod_map),
           pl.BlockSpec((1, D_MODEL), lambda i: (0, 0)),
           pl.BlockSpec((1, D_MODEL), lambda i: (0, 0))],
        out_specs=pl.BlockSpec((tm, D_MODEL), lambda i: (i, 0)),
        out_shape=jax.ShapeDtypeStruct((n_rows, D_MODEL), F32),
        compiler_params=pltpu.CompilerParams(
            dimension_semantics=("arbitrary",), vmem_limit_bytes=VMEM_LIMIT),
        name="combine_ln2",
    )(x1, yg, yg, yg, yg, rw, mod, l2g, l2b)


def _route_kernel(ridx_ref, dest_ref, cnt_ref, run_ref, bst_ref):
    phase = pl.program_id(0)
    i = pl.program_id(1)
    tm = ridx_ref.shape[0]
    ridx = ridx_ref[...]
    lane = lax.broadcasted_iota(jnp.int32, (tm, LANES), 1)
    hits = [ridx[:, kk:kk + 1] == lane for kk in range(TOP_K)]
    chosen = jnp.where(hits[0], 1.0, 0.0)
    for kk in range(1, TOP_K):
        chosen = chosen + jnp.where(hits[kk], 1.0, 0.0)
    colsum = jnp.sum(chosen, axis=0, keepdims=True)

    @pl.when((phase == 0) & (i == 0))
    def _():
        run_ref[...] = jnp.zeros_like(run_ref)

    @pl.when(phase == 0)
    def _():
        run_ref[...] = run_ref[...] + colsum

    @pl.when((phase == 1) & (i == 0))
    def _():
        counts = run_ref[...]
        cnt_ref[...] = counts
        blocks = jnp.floor((counts + (MOE_ROWS - 1.0)) * (1.0 / MOE_ROWS))
        r = lax.broadcasted_iota(jnp.int32, (LANES, LANES), 0)
        c = lax.broadcasted_iota(jnp.int32, (LANES, LANES), 1)
        before = jnp.dot(blocks, (r < c).astype(F32), precision=HIGHEST, preferred_element_type=F32)
        bst_ref[...] = before * float(MOE_ROWS)
        run_ref[...] = jnp.zeros_like(run_ref)

    @pl.when(phase == 1)
    def _():
        rt = lax.broadcasted_iota(jnp.int32, (tm, tm), 0)
        ct = lax.broadcasted_iota(jnp.int32, (tm, tm), 1)
        earlier = _dot((ct < rt).astype(BF16), chosen.astype(BF16))
        row_of = bst_ref[0:1, :] + run_ref[0:1, :] + earlier
        out = jnp.zeros((tm, LANES), F32)
        for kk in range(TOP_K):
            dk = jnp.sum(jnp.where(hits[kk], row_of, 0.0), axis=-1, keepdims=True)
            out = jnp.where(lane == kk, dk, out)
        dest_ref[...] = out.T[0:SUBLANES, :].astype(jnp.int32)
        run_ref[...] = run_ref[...] + colsum


def _route(ridx):
    t_all = ridx.shape[0]
    tm = TM_ROUTE
    return pl.pallas_call(
        _route_kernel,
        grid=(2, t_all // tm),
        in_specs=[pl.BlockSpec((tm, LANES), lambda p, i: (i, 0))],
        out_specs=[pl.BlockSpec((SUBLANES, tm), lambda p, i: (0, i * p)),
                   pl.BlockSpec((SUBLANES, LANES), lambda p, i: (0, 0))],
        out_shape=[jax.ShapeDtypeStruct((SUBLANES, t_all), jnp.int32),
                   jax.ShapeDtypeStruct((SUBLANES, LANES), F32)],
        scratch_shapes=[pltpu.VMEM((SUBLANES, LANES), F32),
                        pltpu.VMEM((SUBLANES, LANES), F32)],
        compiler_params=pltpu.CompilerParams(
            dimension_semantics=("arbitrary", "arbitrary"), vmem_limit_bytes=VMEM_LIMIT),
        name="moe_route",
    )(ridx)


def _routing_tables(counts, n_blocks):
    experts = jnp.arange(N_EXPERTS, dtype=jnp.int32)
    blocks_per = (counts + MOE_ROWS - 1) // MOE_ROWS
    bends = jnp.cumsum(blocks_per)
    bstarts = bends - blocks_per
    blocks = jnp.arange(n_blocks, dtype=jnp.int32)
    block_expert = jnp.minimum(
        jnp.sum((bends[None, :] <= blocks[:, None]).astype(jnp.int32), axis=1), N_EXPERTS - 1)
    n_used = bends[-1:].astype(jnp.int32)
    owner = block_expert[:, None] == experts[None, :]

    def per_block(table):
        return jnp.sum(jnp.where(owner, table[None, :], 0), axis=1)

    n_valid = jnp.clip(per_block(counts) - (blocks - per_block(bstarts)) * MOE_ROWS,
                       0, MOE_ROWS).astype(jnp.int32)
    present = blocks_per > 0
    ordinal = jnp.cumsum(present.astype(jnp.int32)) - 1
    later = lax.cummin(jnp.where(present, experts, N_EXPERTS), reverse=True)
    succ = jnp.concatenate([later[1:], jnp.full((1,), N_EXPERTS, jnp.int32)])
    succ = jnp.where(succ >= N_EXPERTS, -1, succ)
    stage_slot = (per_block(ordinal) % 2).astype(jnp.int32)
    next_expert = per_block(succ).astype(jnp.int32)
    return (block_expert.astype(jnp.int32), n_used, n_valid, stage_slot, next_expert)


def _pos_embed_2d(n_tokens):
    rows = n_tokens // GRID_W
    r = jnp.repeat(jnp.arange(rows), GRID_W).astype(F32)
    col = jnp.tile(jnp.arange(GRID_W), rows).astype(F32)
    quarter = D_MODEL // 4
    omega = 1.0 / (10000.0 ** (jnp.arange(quarter, dtype=F32) / quarter))
    er = r[:, None] * omega
    ec = col[:, None] * omega
    return jnp.concatenate([jnp.sin(er), jnp.cos(er), jnp.sin(ec), jnp.cos(ec)], axis=-1)


def _reorder_w_in(w):
    o_r = 2 * GLA_DK + 2 * GLA_DV
    o_f = o_r + DECAY_RANK
    o_gate = o_f + FNET_DIM
    pad = jnp.zeros((w.shape[0], LANES - DECAY_RANK), w.dtype)
    w_q = w[:, :GLA_DK] * (DK_HEAD ** -0.5)
    return jnp.concatenate([w_q, w[:, GLA_DK:o_r], w[:, o_gate:], w[:, o_f:o_gate], w[:, o_r:o_f], pad],
                           axis=1)


def kernel(x_prompt, x_sample, state_gla, c, c_ctx, w_ada, b_ada, w_in, w_dec_fwd, b_dec_fwd,
           w_dec_bwd, b_dec_bwd, gla_norm_g, w_br_gla, w_br_fnet, w_out, ln1_g, ln1_b, w_router,
           b_router, w_gate_up, b_gate_up, w_down, b_down, ln2_g, ln2_b):
    n_req, ctx_len, _ = x_prompt.shape
    n_lat, lat_len, _ = x_sample.shape
    depth = w_in.shape[0]
    alpha = (2.0 * depth) ** 0.25
    t_ctx = n_req * ctx_len
    t_lat = n_lat * lat_len
    t_all = t_ctx + t_lat

    x_ctx = x_prompt.reshape(t_ctx, D_MODEL)
    x_lat = x_sample.reshape(t_lat, D_MODEL)
    pos = _pos_embed_2d(lat_len)
    zero_pos = jnp.zeros_like(pos)

    cond_rows = -(-(n_lat + 1) // SUBLANES) * SUBLANES
    cond = jnp.zeros((cond_rows, D_MODEL), F32).at[:n_lat].set(c).at[cond_rows - 1].set(c_ctx)

    n_moe_blocks = (t_all * TOP_K) // MOE_ROWS + N_EXPERTS
    tok_chunks = t_all // (SC_WORKERS * SC_WINDOW)
    states = []
    for l in range(depth):
        mod = _ada(cond, w_ada[l], b_ada[l]).reshape(cond_rows, N_MOD, D_MODEL)
        layer_pos = pos if l == 0 else zero_pos
        proj = _inproj(x_ctx, x_lat, layer_pos, mod, _reorder_w_in(w_in[l]).astype(BF16), lat_len)

        def pad_dec(w):
            return jnp.zeros((LANES, GLA_DK), F32).at[:DECAY_RANK].set(w)

        dec = (pad_dec(w_dec_fwd[l]), b_dec_fwd[l].reshape(1, GLA_DK),
               pad_dec(w_dec_bwd[l]), b_dec_bwd[l].reshape(1, GLA_DK),
               gla_norm_g[l].reshape(1, DV_HEAD))
        o_ctx, s_new = _gla(proj, *dec, None, n_seq=n_req, seq_len=ctx_len, row0=0, emit_state=True)
        (o_lat,) = _gla(proj, *dec, state_gla[:, l], n_seq=n_lat, seq_len=lat_len, row0=t_ctx,
                        emit_state=False)
        states.append(s_new)

        mixed_ctx = _fnet(proj, n_seq=n_req, seq_len=ctx_len, row0=0)
        mixed_lat = _fnet(proj, n_seq=n_lat, seq_len=lat_len, row0=t_ctx)

        wr = jnp.zeros((D_MODEL, LANES), F32).at[:, :N_EXPERTS].set(w_router[l])
        br = jnp.full((1, LANES), -1e30, F32).at[0, :N_EXPERTS].set(b_router[l])
        wr_hi = wr.astype(BF16)
        wr_lo = (wr - wr_hi.astype(F32)).astype(BF16)
        x1, h2, ridx, rw = _merge(
            x_ctx, x_lat, layer_pos, mod, o_ctx, o_lat, mixed_ctx, mixed_lat, proj,
            w_br_gla[l].astype(BF16), w_br_fnet[l].astype(BF16), w_out[l].astype(BF16),
            ln1_g[l].reshape(1, D_MODEL), ln1_b[l].reshape(1, D_MODEL), wr_hi, wr_lo, br, lat_len,
            alpha)

        dest, counts = _route(ridx)
        moe_tables = _routing_tables(counts[0, :N_EXPERTS].astype(jnp.int32), n_moe_blocks)
        dest = dest[:TOP_K]
        scatter_idx = dest.reshape(TOP_K, SC_WORKERS, tok_chunks, SC_WINDOW)
        p_rows = n_moe_blocks * MOE_ROWS
        xs = _sc_scatter_rows(h2.reshape(t_all, ROW_TILES, LANES), scatter_idx, p_rows)
        yb = _moe(moe_tables, xs.reshape(p_rows * ROW_TILES, LANES),
                  w_gate_up[l], b_gate_up[l], w_down[l], b_down[l])
        yb = yb.reshape(p_rows, ROW_TILES, LANES)

        def gathered(row0, n_rows):
            idx = dest[:, row0:row0 + n_rows].reshape(SC_WORKERS, -1, SC_WINDOW)
            return _sc_gather_rows(yb, idx).reshape(TOP_K * n_rows * ROW_TILES, LANES)

        l2g = ln2_g[l].reshape(1, D_MODEL)
        l2b = ln2_b[l].reshape(1, D_MODEL)
        tiles_per_seq = lat_len // TM_MIX
        yg_ctx = gathered(0, t_ctx)
        yg_lat = gathered(t_ctx, t_lat)
        x_ctx = _combine(x1, yg_ctx, rw, mod, l2g, l2b, row0=0, n_rows=t_ctx,
                         mod_map=lambda i: (cond_rows - 1, 0, 0), alpha=alpha)
        x_lat = _combine(x1, yg_lat, rw, mod, l2g, l2b, row0=t_ctx, n_rows=t_lat,
                         mod_map=lambda i: (i // tiles_per_seq, 0, 0), alpha=alpha)

    y_prompt = x_ctx.reshape(x_prompt.shape)
    y_sample = x_lat.reshape(x_sample.shape)
    new_state = jnp.stack(states, axis=1).astype(x_prompt.dtype)
    return (y_prompt, y_sample, new_state)
```

```python
import functools
import math

import numpy as np
import jax
import jax.numpy as jnp
from jax import lax
from jax.experimental import pallas as pl
from jax.experimental.pallas import tpu as pltpu
from jax.experimental.pallas import tpu_sc as plsc

F32 = jnp.float32
BF16 = jnp.bfloat16

D_MODEL = 1024
GRID_W = 64
GLA_HEADS = 4
DK_HEAD = 128
DV_HEAD = 256
GLA_DK = GLA_HEADS * DK_HEAD
GLA_DV = GLA_HEADS * DV_HEAD
DECAY_RANK = 16
GATE_NORMALIZER = 16.0
FNET_GROUPS = 4
FNET_GROUP_DIM = 128
FNET_DIM = FNET_GROUPS * FNET_GROUP_DIM
N_EXPERTS = 32
TOP_K = 4
D_EXPERT = 1024
SWIGLU_LIMIT = 7.0
SWIGLU_ALPHA = 1.702
LN_EPS = 1e-6
N_MOD = 6

LANES = 128
SUBLANES = 8
HALF_MODEL = D_MODEL // 2
ROW_TILES = HALF_MODEL // LANES
ROW_DTYPE = jnp.uint32
COL_Q = 0
COL_K = GLA_DK
COL_V = 2 * GLA_DK
COL_G = COL_V + GLA_DV
COL_GATE_A = COL_G + GLA_DV
COL_GATE_B = COL_GATE_A + D_MODEL
COL_F = COL_GATE_B + D_MODEL
COL_R = COL_F + FNET_DIM
PROJ_COLS = COL_R + LANES

GLA_CHUNK = 128
GLA_LEAF = 16
TM_PROJ = 256
TM_MIX = 256
TM_MERGE = 512
TM_MERGE_GROUP = 128
TM_ROUTE = 1024
MOE_ROWS = 512
MOE_GROUP = 256
VMEM_LIMIT = 56 * 1024 * 1024

SC_CORES = 2
SC_SUBCORES = 16
SC_WORKERS = SC_CORES * SC_SUBCORES
SC_WINDOW = 64

HIGHEST = lax.Precision.HIGHEST


def _layer_norm(x):
    mu = jnp.mean(x, axis=-1, keepdims=True)
    xc = x - mu
    var = jnp.mean(xc * xc, axis=-1, keepdims=True)
    return xc * lax.rsqrt(var + LN_EPS)


def _sigmoid(x):
    return 0.5 * jnp.tanh(0.5 * x) + 0.5


def _log_sigmoid(z):
    return jnp.minimum(z, 0.0) - jnp.log(1.0 + jnp.exp(-jnp.abs(z)))


def _dot(a, b):
    return jnp.dot(a, b, preferred_element_type=F32)


def _split_bf16(x, terms):
    parts = []
    for _ in range(terms):
        p = x.astype(BF16)
        parts.append(p)
        x = x - p.astype(F32)
    return parts


def _dot_nt(a, b):
    return lax.dot_general(a, b, (((1,), (1,)), ((), ())), preferred_element_type=F32)


def _dot_tn(a, b):
    return lax.dot_general(a, b, (((0,), (0,)), ((), ())), preferred_element_type=F32)


def _row_tile_slice(j, n_rows, first_row=0):
    return pl.ds(first_row * ROW_TILES + j, n_rows, stride=ROW_TILES)


def _store_row_tiles(ref, val, first_row=0):
    for j in range(ROW_TILES):
        lo = val[:, j * LANES:(j + 1) * LANES]
        hi = val[:, HALF_MODEL + j * LANES:HALF_MODEL + (j + 1) * LANES]
        ref[_row_tile_slice(j, val.shape[0], first_row), :] = pltpu.pack_elementwise(
            [lo, hi], packed_dtype=BF16)


def _load_row_tile(ref, j, n_rows, first_row=0):
    words = ref[_row_tile_slice(j, n_rows, first_row), :]
    return tuple(pltpu.unpack_elementwise(words, index=half, packed_dtype=BF16, unpacked_dtype=F32)
                 for half in range(2))


def _ada_kernel(c_ref, w_ref, b_ref, o_ref):
    c = c_ref[...]
    s = c * _sigmoid(c)
    o_ref[...] = _dot(s.astype(BF16), w_ref[...].astype(BF16)) + b_ref[...]


def _ada(cond, w_ada, b_ada):
    rows = cond.shape[0]
    n = w_ada.shape[1]
    tn = 1536
    return pl.pallas_call(
        _ada_kernel,
        grid=(n // tn,),
        in_specs=[pl.BlockSpec((rows, D_MODEL), lambda j: (0, 0)),
                  pl.BlockSpec((D_MODEL, tn), lambda j: (0, j)),
                  pl.BlockSpec((1, tn), lambda j: (0, j))],
        out_specs=pl.BlockSpec((rows, tn), lambda j: (0, j)),
        out_shape=jax.ShapeDtypeStruct((rows, n), F32),
        compiler_params=pltpu.CompilerParams(vmem_limit_bytes=VMEM_LIMIT),
        name="ada_mod",
    )(cond, w_ada, b_ada.reshape(1, n))


def _group_maps(n_ctx_tiles):
    def ctx_map(i, *_):
        return (jnp.minimum(i, n_ctx_tiles - 1), 0)

    def lat_map(i, *_):
        return (jnp.maximum(i - n_ctx_tiles, 0), 0)

    return ctx_map, lat_map


def _token_specs(tm, n_ctx_tiles, tiles_per_latent_seq, ctx_mod_row):
    ctx_map, lat_map = _group_maps(n_ctx_tiles)

    def pos_map(i, *_):
        return (jnp.maximum(i - n_ctx_tiles, 0) % tiles_per_latent_seq, 0)

    def mod_map(i, *_):
        return (jnp.where(i < n_ctx_tiles, ctx_mod_row,
                          jnp.maximum(i - n_ctx_tiles, 0) // tiles_per_latent_seq), 0, 0)

    return [pl.BlockSpec((tm, D_MODEL), ctx_map),
            pl.BlockSpec((tm, D_MODEL), lat_map),
            pl.BlockSpec((tm, D_MODEL), pos_map),
            pl.BlockSpec((1, N_MOD, D_MODEL), mod_map)]


def _inproj_kernel(xc_ref, xl_ref, pos_ref, mod_ref, w_ref, o_ref, *, n_ctx_tiles):
    i = pl.program_id(0)

    def project(x):
        h = _layer_norm(x) * (1.0 + mod_ref[0, 1:2, :]) + mod_ref[0, 0:1, :]
        o_ref[...] = _dot(h.astype(BF16), w_ref[...])

    @pl.when(i < n_ctx_tiles)
    def _():
        project(xc_ref[...])

    @pl.when(i >= n_ctx_tiles)
    def _():
        project(xl_ref[...] + pos_ref[...])


def _inproj(x_ctx, x_lat, pos, mod, w_in_bf, lat_len):
    t_ctx, t_lat = x_ctx.shape[0], x_lat.shape[0]
    n_ctx_tiles = t_ctx // TM_PROJ
    n_tiles = (t_ctx + t_lat) // TM_PROJ
    kern = functools.partial(_inproj_kernel, n_ctx_tiles=n_ctx_tiles)
    specs = _token_specs(TM_PROJ, n_ctx_tiles, lat_len // TM_PROJ, mod.shape[0] - 1)
    return pl.pallas_call(
        kern,
        grid=(n_tiles,),
        in_specs=specs + [pl.BlockSpec((D_MODEL, PROJ_COLS), lambda i: (0, 0),
                                       pipeline_mode=pl.Buffered(1))],
        out_specs=pl.BlockSpec((TM_PROJ, PROJ_COLS), lambda i: (i, 0)),
        out_shape=jax.ShapeDtypeStruct((t_ctx + t_lat, PROJ_COLS), F32),
        compiler_params=pltpu.CompilerParams(
            dimension_semantics=("arbitrary",), vmem_limit_bytes=VMEM_LIMIT),
        name="ln_inproj",
    )(x_ctx, x_lat, pos, mod, w_in_bf)


def _gla_kernel(*refs, seq_len, has_s0, emit_state):
    it = iter(refs)
    q_ref, k_ref, v_ref, r_ref = next(it), next(it), next(it), next(it)
    wdf_ref, bdf_ref, wdb_ref, bdb_ref, g_ref = next(it), next(it), next(it), next(it), next(it)
    s0_ref = next(it) if has_s0 else None
    o_ref = next(it)
    sout_ref = next(it) if emit_state else None
    cum_ref, a_ref, qi_ref, ko_ref, dec_ref, op_ref, st_ref = (next(it) for _ in range(7))

    C = GLA_CHUNK
    n_chunks = seq_len // C

    def rows(n):
        if isinstance(n, int):
            return pl.ds(n * C, C)
        return pl.ds(pl.multiple_of(n * C, C), C)

    def dec_rows(n):
        if isinstance(n, int):
            return pl.ds(n * SUBLANES, SUBLANES)
        return pl.ds(pl.multiple_of(n * SUBLANES, SUBLANES), SUBLANES)

    def loop(body):
        if n_chunks <= 2:
            for n in range(n_chunks):
                body(n)
        else:
            def step(m, carry):
                body(2 * m)
                body(2 * m + 1)
                return carry
            lax.fori_loop(0, n_chunks // 2, step, 0)

    rt = lax.broadcasted_iota(jnp.int32, (C, C), 0)
    ct = lax.broadcasted_iota(jnp.int32, (C, C), 1)
    tri = ((rt >= ct).astype(BF16), (ct >= rt).astype(BF16))
    row_id = lax.broadcasted_iota(jnp.int32, (C, DK_HEAD), 0)

    r_hi, r_lo = _split_bf16(r_ref[...], 2)
    for d, (w_ref, b_ref) in enumerate(((wdf_ref, bdf_ref), (wdb_ref, bdb_ref))):
        w_hi, w_lo = _split_bf16(w_ref[...], 2)
        z = _dot(r_hi, w_hi) + _dot(r_lo, w_hi) + _dot(r_hi, w_lo) + b_ref[...]
        cum_ref[d] = _log_sigmoid(z) * (1.0 / GATE_NORMALIZER)

    def cumsum_chunk(n):
        for d in range(2):
            la_hi, la_lo = _split_bf16(cum_ref[d, rows(n), :], 2)
            cum_ref[d, rows(n), :] = _dot(tri[d], la_hi) + _dot(tri[d], la_lo)

    loop(cumsum_chunk)

    query_rows = ({}, {})
    keep = ({}, {})
    for d in range(2):
        blk = C // 2
        while blk >= GLA_LEAF:
            q_parity = 1 if d == 0 else 0
            query_rows[d][blk] = ((row_id // blk) % 2) == q_parity
            qb, kb = rt // blk, ct // blk
            keep[d][blk] = ((qb % 2) == q_parity) & ((qb == kb + 1) if d == 0 else (kb == qb + 1))
            blk //= 2
        order = (rt >= ct) if d == 0 else (ct >= rt)
        keep[d][0] = ((rt // GLA_LEAF) == (ct // GLA_LEAF)) & order

    def block_rows(cum, first, step, count):
        span = C // count
        parts = [jnp.broadcast_to(cum[first + p * step:first + p * step + 1, :], (span, DK_HEAD))
                 for p in range(count)]
        return parts[0] if count == 1 else jnp.concatenate(parts, axis=0)

    def scores(n, d):
        cum = cum_ref[d, rows(n), :]
        q = q_ref[rows(n), :]
        k = k_ref[rows(n), :]
        acc = None
        blk = C // 2
        while blk >= GLA_LEAF:
            pairs = C // (2 * blk)
            bnd = blk - 1 if d == 0 else blk
            w = jnp.exp(-jnp.abs(cum - block_rows(cum, bnd, 2 * blk, pairs)))
            x = (jnp.where(query_rows[d][blk], q, k) * w).astype(BF16)
            s = jnp.where(keep[d][blk], _dot_nt(x, x), 0.0)
            acc = s if acc is None else acc + s
            blk //= 2
        mid = GLA_LEAF // 2 - 1 if d == 0 else GLA_LEAF // 2
        e = cum - block_rows(cum, mid, GLA_LEAF, C // GLA_LEAF)
        s = _dot_nt((q * jnp.exp(e)).astype(BF16), (k * jnp.exp(-e)).astype(BF16))
        acc = acc + jnp.where(keep[d][0], s, 0.0)
        a_ref[d, rows(n), :] = acc.astype(BF16)
        end = cum[C - 1:C, :] if d == 0 else cum[0:1, :]
        qi_ref[d, rows(n), :] = (q * jnp.exp(cum)).astype(BF16)
        ko_ref[d, rows(n), :] = (k * jnp.exp(end - cum)).astype(BF16)
        dec_ref[d, dec_rows(n), :] = jnp.broadcast_to(jnp.exp(end), (SUBLANES, DK_HEAD))

    def scores_chunk(n):
        scores(n, 0)
        scores(n, 1)

    loop(scores_chunk)

    for d in range(2):
        if has_s0:
            st_ref[d] = s0_ref[0, d, 0].T
        else:
            st_ref[d] = jnp.zeros((DV_HEAD, DK_HEAD), F32)

    def scan(n, d):
        v = v_ref[rows(n), :].astype(BF16)
        st = st_ref[d]
        o = _dot(a_ref[d, rows(n), :], v) + _dot_nt(qi_ref[d, rows(n), :], st.astype(BF16))
        st_ref[d] = st * dec_ref[d, dec_rows(n), :][0:1, :] + _dot_tn(v, ko_ref[d, rows(n), :])
        op_ref[d, rows(n), :] = o

    def scan_chunk(m):
        scan(m, 0)
        scan(n_chunks - 1 - m, 1)

    loop(scan_chunk)
    if emit_state:
        sout_ref[0, 0, 0] = st_ref[0].T
        sout_ref[0, 1, 0] = st_ref[1].T

    g = g_ref[...]

    def finish_chunk(n):
        o = op_ref[0, rows(n), :] + op_ref[1, rows(n), :]
        ms = jnp.mean(o * o, axis=-1, keepdims=True)
        o_ref[rows(n), :] = o * lax.rsqrt(ms + LN_EPS) * g

    loop(finish_chunk)


def _gla(proj, wdf, bdf, wdb, bdb, g, s0, *, n_seq, seq_len, row0, emit_state):
    has_s0 = s0 is not None
    blk0 = row0 // seq_len
    kern = functools.partial(_gla_kernel, seq_len=seq_len, has_s0=has_s0, emit_state=emit_state)
    in_specs = [
        pl.BlockSpec((seq_len, DK_HEAD), lambda b, h: (blk0 + b, COL_Q // DK_HEAD + h)),
        pl.BlockSpec((seq_len, DK_HEAD), lambda b, h: (blk0 + b, COL_K // DK_HEAD + h)),
        pl.BlockSpec((seq_len, DV_HEAD), lambda b, h: (blk0 + b, COL_V // DV_HEAD + h)),
        pl.BlockSpec((seq_len, LANES), lambda b, h: (blk0 + b, COL_R // LANES)),
        pl.BlockSpec((LANES, DK_HEAD), lambda b, h: (0, h)),
        pl.BlockSpec((1, DK_HEAD), lambda b, h: (0, h)),
        pl.BlockSpec((LANES, DK_HEAD), lambda b, h: (0, h)),
        pl.BlockSpec((1, DK_HEAD), lambda b, h: (0, h)),
        pl.BlockSpec((1, DV_HEAD), lambda b, h: (0, 0)),
    ]
    args = [proj, proj, proj, proj, wdf, bdf, wdb, bdb, g]
    if has_s0:
        in_specs.append(pl.BlockSpec((1, 2, 1, DK_HEAD, DV_HEAD), lambda b, h: (b, 0, h, 0, 0)))
        args.append(s0)
    out_specs = [pl.BlockSpec((seq_len, DV_HEAD), lambda b, h: (b, h))]
    out_shape = [jax.ShapeDtypeStruct((n_seq * seq_len, GLA_DV), F32)]
    if emit_state:
        out_specs.append(pl.BlockSpec((1, 2, 1, DK_HEAD, DV_HEAD), lambda b, h: (b, 0, h, 0, 0)))
        out_shape.append(jax.ShapeDtypeStruct((n_seq, 2, GLA_HEADS, DK_HEAD, DV_HEAD), F32))

    res = pl.pallas_call(
        kern,
        grid=(n_seq, GLA_HEADS),
        in_specs=in_specs,
        out_specs=out_specs,
        out_shape=out_shape,
        scratch_shapes=[pltpu.VMEM((2, seq_len, DK_HEAD), F32),
                        pltpu.VMEM((2, seq_len, GLA_CHUNK), BF16),
                        pltpu.VMEM((2, seq_len, DK_HEAD), BF16),
                        pltpu.VMEM((2, seq_len, DK_HEAD), BF16),
                        pltpu.VMEM((2, seq_len // GLA_CHUNK * SUBLANES, DK_HEAD), F32),
                        pltpu.VMEM((2, seq_len, DV_HEAD), F32),
                        pltpu.VMEM((2, DV_HEAD, DK_HEAD), F32)],
        compiler_params=pltpu.CompilerParams(
            dimension_semantics=("arbitrary", "arbitrary"), vmem_limit_bytes=VMEM_LIMIT),
        name="gla_seq%d" % seq_len,
    )(*args)
    return res


def _fnet_kernel(f_ref, cl_ref, sl_ref, cg_ref, sg_ref, o_ref, uc_ref, us_ref, *, seq_len):
    cg = cg_ref[...]
    sg = sg_ref[...]
    for grp in range(FNET_GROUPS):
        lo = grp * FNET_GROUP_DIM
        u = f_ref[:, lo:lo + FNET_GROUP_DIM].astype(BF16)
        uc_ref[:, lo:lo + FNET_GROUP_DIM] = _dot(u, cg).astype(BF16)
        us_ref[:, lo:lo + FNET_GROUP_DIM] = _dot(u, sg).astype(BF16)
    mixed = _dot(cl_ref[...], uc_ref[...]) - _dot(sl_ref[...], us_ref[...])
    o_ref[...] = mixed * (1.0 / math.sqrt(seq_len * FNET_GROUP_DIM))


def _dft_mats(n):
    j = np.arange(n, dtype=np.int64)
    ang = (2.0 * np.pi / n) * ((j[:, None] * j[None, :]) % n).astype(np.float64)
    return (jnp.asarray(np.cos(ang), dtype=F32).astype(BF16),
            jnp.asarray(np.sin(ang), dtype=F32).astype(BF16))


def _fnet(proj, *, n_seq, seq_len, row0):
    blk0 = row0 // seq_len
    cl, sl = _dft_mats(seq_len)
    cg, sg = _dft_mats(FNET_GROUP_DIM)
    kern = functools.partial(_fnet_kernel, seq_len=seq_len)
    return pl.pallas_call(
        kern,
        grid=(n_seq,),
        in_specs=[pl.BlockSpec((seq_len, FNET_DIM), lambda b: (blk0 + b, COL_F // FNET_DIM)),
                  pl.BlockSpec((seq_len, seq_len), lambda b: (0, 0)),
                  pl.BlockSpec((seq_len, seq_len), lambda b: (0, 0)),
                  pl.BlockSpec((FNET_GROUP_DIM, FNET_GROUP_DIM), lambda b: (0, 0)),
                  pl.BlockSpec((FNET_GROUP_DIM, FNET_GROUP_DIM), lambda b: (0, 0))],
        out_specs=pl.BlockSpec((seq_len, FNET_DIM), lambda b: (b, 0)),
        out_shape=jax.ShapeDtypeStruct((n_seq * seq_len, FNET_DIM), F32),
        scratch_shapes=[pltpu.VMEM((seq_len, FNET_DIM), BF16),
                        pltpu.VMEM((seq_len, FNET_DIM), BF16)],
        compiler_params=pltpu.CompilerParams(
            dimension_semantics=("arbitrary",), vmem_limit_bytes=VMEM_LIMIT),
        name="fnet_seq%d" % seq_len,
    )(proj, cl, sl, cg, sg)


def _merge_kernel(xc_ref, xl_ref, pos_ref, mod_ref, oc_ref, ol_ref, mc_ref, ml_ref,
                  g_ref, ga_ref, gb_ref, wbg_ref, wbf_ref, wo_ref, l1g_ref, l1b_ref, wrh_ref, wrl_ref,
                  br_ref, x1_ref, h2_ref, ridx_ref, rw_ref, *, n_ctx_tiles, alpha):
    i = pl.program_id(0)

    def compute(rows, x, o, mx):
        hg = 0.5 * g_ref[rows, :]
        a = (o * (hg * (jnp.tanh(hg) + 1.0))).astype(BF16)
        gla_out = _dot(a, wbg_ref[...])
        fnet_out = _dot(mx.astype(BF16), wbf_ref[...])
        merged = ((jnp.tanh(0.5 * ga_ref[rows, :]) + 1.0) * gla_out
                  + (jnp.tanh(0.5 * gb_ref[rows, :]) + 1.0) * fnet_out)
        mix = _dot(merged.astype(BF16), wo_ref[...])
        y = alpha * x + mod_ref[0, 2:3, :] * mix
        x1 = _layer_norm(y) * l1g_ref[...] + l1b_ref[...]
        x1_ref[rows, :] = x1
        h2 = _layer_norm(x1) * (1.0 + mod_ref[0, 4:5, :]) + mod_ref[0, 3:4, :]
        _store_row_tiles(h2_ref, h2, rows.start)

        h_hi, h_lo = _split_bf16(h2, 2)
        logits = (_dot(h_hi, wrh_ref[...]) + _dot(h_lo, wrh_ref[...]) + _dot(h_hi, wrl_ref[...])
                  + br_ref[...])
        lane_i = lax.broadcasted_iota(jnp.int32, logits.shape, 1)
        lane = lane_i.astype(F32)
        idx_out = jnp.zeros(logits.shape, F32)
        val_out = jnp.zeros(logits.shape, F32)
        top0 = None
        denom = None
        for kk in range(TOP_K):
            m = jnp.max(logits, axis=-1, keepdims=True)
            sel = jnp.min(jnp.where(logits == m, lane, float(LANES)), axis=-1, keepdims=True)
            if kk == 0:
                top0 = m
                p = jnp.ones_like(m)
                denom = p
            else:
                p = jnp.exp(m - top0)
                denom = denom + p
            idx_out = jnp.where(lane_i == kk, sel, idx_out)
            val_out = jnp.where(lane_i == kk, p, val_out)
            logits = jnp.where(lane == sel, -jnp.inf, logits)
        ridx_ref[rows, :] = idx_out.astype(jnp.int32)
        rw_ref[rows, :] = val_out / denom

    tm = x1_ref.shape[0]
    groups = [slice(r0, r0 + TM_MERGE_GROUP) for r0 in range(0, tm, TM_MERGE_GROUP)]

    @pl.when(i < n_ctx_tiles)
    def _():
        for rows in groups:
            compute(rows, xc_ref[rows, :], oc_ref[rows, :], mc_ref[rows, :])

    @pl.when(i >= n_ctx_tiles)
    def _():
        for rows in groups:
            compute(rows, xl_ref[rows, :] + pos_ref[rows, :], ol_ref[rows, :], ml_ref[rows, :])


def _merge(x_ctx, x_lat, pos, mod, o_ctx, o_lat, mixed_ctx, mixed_lat, proj,
           wbg, wbf, wo, l1g, l1b, wr_hi, wr_lo, br, lat_len, alpha):
    t_ctx, t_lat = x_ctx.shape[0], x_lat.shape[0]
    t_all = t_ctx + t_lat
    tm = TM_MERGE
    n_ctx_tiles = t_ctx // tm
    kern = functools.partial(_merge_kernel, n_ctx_tiles=n_ctx_tiles, alpha=alpha)
    specs = _token_specs(tm, n_ctx_tiles, lat_len // tm, mod.shape[0] - 1)
    ctx_map, lat_map = _group_maps(n_ctx_tiles)

    def const(shape):
        return pl.BlockSpec(shape, lambda i: (0,) * len(shape))

    in_specs = specs + [
        pl.BlockSpec((tm, GLA_DV), ctx_map),
        pl.BlockSpec((tm, GLA_DV), lat_map),
        pl.BlockSpec((tm, FNET_DIM), ctx_map),
        pl.BlockSpec((tm, FNET_DIM), lat_map),
        pl.BlockSpec((tm, GLA_DV), lambda i: (i, COL_G // GLA_DV)),
        pl.BlockSpec((tm, D_MODEL), lambda i: (i, COL_GATE_A // D_MODEL)),
        pl.BlockSpec((tm, D_MODEL), lambda i: (i, COL_GATE_B // D_MODEL)),
        const((GLA_DV, D_MODEL)), const((FNET_DIM, D_MODEL)), const((D_MODEL, D_MODEL)),
        const((1, D_MODEL)), const((1, D_MODEL)),
        const((D_MODEL, LANES)), const((D_MODEL, LANES)), const((1, LANES)),
    ]
    out_specs = [pl.BlockSpec((tm, D_MODEL), lambda i: (i, 0)),
                 pl.BlockSpec((tm * ROW_TILES, LANES), lambda i: (i, 0)),
                 pl.BlockSpec((tm, LANES), lambda i: (i, 0)),
                 pl.BlockSpec((tm, LANES), lambda i: (i, 0))]
    out_shape = [jax.ShapeDtypeStruct((t_all, D_MODEL), F32),
                 jax.ShapeDtypeStruct((t_all * ROW_TILES, LANES), ROW_DTYPE),
                 jax.ShapeDtypeStruct((t_all, LANES), jnp.int32),
                 jax.ShapeDtypeStruct((t_all, LANES), F32)]
    return pl.pallas_call(
        kern,
        grid=(t_all // tm,),
        in_specs=in_specs,
        out_specs=out_specs,
        out_shape=out_shape,
        compiler_params=pltpu.CompilerParams(
            dimension_semantics=("arbitrary",), vmem_limit_bytes=VMEM_LIMIT),
        name="merge_ln1_router",
    )(x_ctx, x_lat, pos, mod, o_ctx, o_lat, mixed_ctx, mixed_lat, proj, proj, proj,
      wbg, wbf, wo, l1g, l1b, wr_hi, wr_lo, br)


def _sc_mesh():
    return plsc.VectorSubcoreMesh(core_axis_name="c", subcore_axis_name="s")


def _sc_worker_id():
    return lax.axis_index("s") * SC_CORES + lax.axis_index("c")


def _sc_scatter_rows(src, idx, n_out):
    n_src = src.shape[0]
    w = SC_WINDOW
    n_chunks = n_src // (SC_WORKERS * w)
    copies = idx.shape[0]
    assert n_chunks % 2 == 0 and idx.shape == (copies, SC_WORKERS, n_chunks, w)

    @functools.partial(
        pl.kernel, mesh=_sc_mesh(),
        out_type=jax.ShapeDtypeStruct((n_out, ROW_TILES, LANES), ROW_DTYPE),
        scratch_types=[pltpu.VMEM((copies * n_chunks, w), jnp.int32),
                       pltpu.VMEM((2, w, ROW_TILES, LANES), ROW_DTYPE),
                       pltpu.SemaphoreType.DMA((2,)),
                       pltpu.SemaphoreType.DMA((2,))],
        name="moe_dispatch_scatter")
    def k(src_hbm, idx_hbm, out_hbm, idx_v, rows_v, rsem, wsem):
        wid = _sc_worker_id()
        base = wid * (n_chunks * w)
        for kk in range(copies):
            pltpu.sync_copy(idx_hbm.at[kk, wid], idx_v.at[pl.ds(kk * n_chunks, n_chunks)])

        def read(j, slot):
            return pltpu.make_async_copy(src_hbm.at[pl.ds(base + j * w, w)], rows_v.at[slot],
                                         rsem.at[slot])

        def scatter(j, kk, slot):
            return pltpu.make_async_copy(rows_v.at[slot], out_hbm.at[idx_v.at[kk * n_chunks + j]],
                                         wsem.at[slot])

        read(0, 0).start()

        @pl.loop(0, n_chunks, step=2)
        def _(jj):
            read(jj, 0).wait()

            @pl.when(jj > 0)
            def _():
                for kk in range(copies):
                    scatter(jj - 1, kk, 1).wait()

            read(jj + 1, 1).start()
            for kk in range(copies):
                scatter(jj, kk, 0).start()
            read(jj + 1, 1).wait()
            for kk in range(copies):
                scatter(jj, kk, 0).wait()

            @pl.when(jj + 2 < n_chunks)
            def _():
                read(jj + 2, 0).start()

            for kk in range(copies):
                scatter(jj + 1, kk, 1).start()

        for kk in range(copies):
            scatter(n_chunks - 1, kk, 1).wait()

    return k(src, idx)


def _sc_gather_rows(table, idx):
    _, n_chunks, w = idx.shape
    assert n_chunks % 2 == 0 and idx.shape[0] == SC_WORKERS and w == SC_WINDOW
    n_out = SC_WORKERS * n_chunks * w

    @functools.partial(
        pl.kernel, mesh=_sc_mesh(),
        out_type=jax.ShapeDtypeStruct((n_out, ROW_TILES, LANES), ROW_DTYPE),
        scratch_types=[pltpu.VMEM((n_chunks, w), jnp.int32),
                       pltpu.VMEM((2, w, ROW_TILES, LANES), ROW_DTYPE),
                       pltpu.SemaphoreType.DMA((2,)),
                       pltpu.SemaphoreType.DMA((2,))],
        name="moe_combine_gather")
    def k(table_hbm, idx_hbm, out_hbm, idx_v, rows_v, gsem, wsem):
        wid = _sc_worker_id()
        base = wid * (n_chunks * w)
        pltpu.sync_copy(idx_hbm.at[wid], idx_v)

        def gather(j, slot):
            return pltpu.make_async_copy(table_hbm.at[idx_v.at[j]], rows_v.at[slot], gsem.at[slot])

        def write(j, slot):
            return pltpu.make_async_copy(rows_v.at[slot], out_hbm.at[pl.ds(base + j * w, w)],
                                         wsem.at[slot])

        gather(0, 0).start()

        @pl.loop(0, n_chunks, step=2)
        def _(jj):
            gather(jj, 0).wait()

            @pl.when(jj > 0)
            def _():
                write(jj - 1, 1).wait()

            gather(jj + 1, 1).start()
            write(jj, 0).start()
            gather(jj + 1, 1).wait()
            write(jj, 0).wait()

            @pl.when(jj + 2 < n_chunks)
            def _():
                gather(jj + 2, 0).start()

            write(jj + 1, 1).start()

        write(n_chunks - 1, 1).wait()

    return k(table, idx)


def _moe_kernel(be_ref, nu_ref, nv_ref, slot_ref, nxt_ref, x_ref, wgu_hbm, bgu_ref, wd_hbm, bd_ref,
                o_ref, wgu_st, wd_st, wgu_bf, wd_bf, xb_ref, sem):
    b = pl.program_id(0)
    e = be_ref[b]
    prev = be_ref[jnp.maximum(b - 1, 0)]
    active = b < nu_ref[0]
    changed = (b == 0) | (e != prev)

    def weight_copies(expert, s):
        return (pltpu.make_async_copy(wgu_hbm.at[expert], wgu_st.at[s], sem.at[0, s]),
                pltpu.make_async_copy(wd_hbm.at[expert], wd_st.at[s], sem.at[1, s]))

    @pl.when(active & changed)
    def _():
        s = slot_ref[b]

        @pl.when(b == 0)
        def _():
            for cp in weight_copies(e, s):
                cp.start()

        for cp in weight_copies(e, s):
            cp.wait()
        wgu_bf[...] = wgu_st[s].astype(BF16)
        wd_bf[...] = wd_st[s].astype(BF16)
        nxt = nxt_ref[b]

        @pl.when(nxt >= 0)
        def _():
            for cp in weight_copies(nxt, 1 - s):
                cp.start()

    n_valid = nv_ref[b]

    def expert_mlp(groups):
        for r0 in groups:
            valid = lax.broadcasted_iota(jnp.int32, (MOE_GROUP, LANES), 0) < n_valid - r0
            for j in range(ROW_TILES):
                for half, xj in enumerate(_load_row_tile(x_ref, j, MOE_GROUP, r0)):
                    c0 = half * HALF_MODEL + j * LANES
                    xb_ref[r0:r0 + MOE_GROUP, c0:c0 + LANES] = jnp.where(valid, xj, 0.0).astype(BF16)
            gu = _dot(xb_ref[r0:r0 + MOE_GROUP, :], wgu_bf[...]) + bgu_ref[0]
            gate = jnp.minimum(gu[:, :D_EXPERT], SWIGLU_LIMIT)
            up = jnp.clip(gu[:, D_EXPERT:], -SWIGLU_LIMIT, SWIGLU_LIMIT)
            glu = gate * _sigmoid(gate * SWIGLU_ALPHA)
            act = ((up + 1.0) * glu).astype(BF16)
            _store_row_tiles(o_ref, _dot(act, wd_bf[...]) + bd_ref[0], r0)

    all_groups = list(range(0, MOE_ROWS, MOE_GROUP))

    @pl.when(active & (n_valid > MOE_ROWS - MOE_GROUP))
    def _():
        expert_mlp(all_groups)

    @pl.when(active & (n_valid <= MOE_ROWS - MOE_GROUP))
    def _():
        expert_mlp(all_groups[:-1])


def _moe(tables, xs, w_gate_up, b_gate_up, w_down, b_down):
    p_rows = xs.shape[0] // ROW_TILES
    n_blocks = p_rows // MOE_ROWS

    def blk(b, be, nu, *_):
        return jnp.minimum(b, nu[0] - 1)

    def expert(b, be, nu, *_):
        return (be[blk(b, be, nu)], 0, 0)

    def rows(b, be, nu, *_):
        return (blk(b, be, nu), 0)

    grid_spec = pltpu.PrefetchScalarGridSpec(
        num_scalar_prefetch=len(tables),
        grid=(n_blocks,),
        in_specs=[
            pl.BlockSpec((MOE_ROWS * ROW_TILES, LANES), rows),
            pl.BlockSpec(memory_space=pl.ANY),
            pl.BlockSpec((1, 1, 2 * D_EXPERT), expert),
            pl.BlockSpec(memory_space=pl.ANY),
            pl.BlockSpec((1, 1, D_MODEL), expert),
        ],
        out_specs=pl.BlockSpec((MOE_ROWS * ROW_TILES, LANES), rows),
        scratch_shapes=[pltpu.VMEM((2, D_MODEL, 2 * D_EXPERT), F32),
                        pltpu.VMEM((2, D_EXPERT, D_MODEL), F32),
                        pltpu.VMEM((D_MODEL, 2 * D_EXPERT), BF16),
                        pltpu.VMEM((D_EXPERT, D_MODEL), BF16),
                        pltpu.VMEM((MOE_ROWS, D_MODEL), BF16),
                        pltpu.SemaphoreType.DMA((2, 2))],
    )
    return pl.pallas_call(
        _moe_kernel,
        grid_spec=grid_spec,
        out_shape=jax.ShapeDtypeStruct((p_rows * ROW_TILES, LANES), ROW_DTYPE),
        compiler_params=pltpu.CompilerParams(
            dimension_semantics=("arbitrary",), vmem_limit_bytes=VMEM_LIMIT),
        name="moe_grouped_mlp",
    )(*tables, xs, w_gate_up, b_gate_up.reshape(N_EXPERTS, 1, 2 * D_EXPERT), w_down,
      b_down.reshape(N_EXPERTS, 1, D_MODEL))


def _combine_kernel(x1_ref, y0_ref, y1_ref, y2_ref, y3_ref, rw_ref, mod_ref, g_ref, b_ref, o_ref,
                    *, alpha):
    rw = rw_ref[...]
    y_refs = (y0_ref, y1_ref, y2_ref, y3_ref)
    pieces = [None] * (2 * ROW_TILES)
    for j in range(ROW_TILES):
        for kk in range(TOP_K):
            for half, yj in enumerate(_load_row_tile(y_refs[kk], j, rw.shape[0])):
                term = rw[:, kk:kk + 1] * yj
                slot = half * ROW_TILES + j
                pieces[slot] = term if kk == 0 else pieces[slot] + term
    ff = jnp.concatenate(pieces, axis=-1)
    y = alpha * x1_ref[...] + mod_ref[0, 5:6, :] * ff
    o_ref[...] = _layer_norm(y) * g_ref[...] + b_ref[...]


def _combine(x1, yg, rw, mod, l2g, l2b, *, row0, n_rows, mod_map, alpha):
    tm = TM_MIX
    t0 = row0 // tm
    tiles = n_rows // tm
    kern = functools.partial(_combine_kernel, alpha=alpha)

    def y_spec(kk):
        return pl.BlockSpec((tm * ROW_TILES, LANES), lambda i: (kk * tiles + i, 0))

    return pl.pallas_call(
        kern,
        grid=(n_rows // tm,),
        in_specs=[pl.BlockSpec((tm, D_MODEL), lambda i: (t0 + i, 0))]
        + [y_spec(kk) for kk in range(TOP_K)]
        + [pl.BlockSpec((tm, LANES), lambda i: (t0 + i, 0)),
           pl.BlockSpec((1, N_MOD, D_MODEL), mod_map),
           pl.BlockSpec((1, D_MODEL), lambda i: (0, 0)),
           pl.BlockSpec((1, D_MODEL), lambda i: (0, 0))],
        out_specs=pl.BlockSpec((tm, D_MODEL), lambda i: (i, 0)),
        out_shape=jax.ShapeDtypeStruct((n_rows, D_MODEL), F32),
        compiler_params=pltpu.CompilerParams(
            dimension_semantics=("arbitrary",), vmem_limit_bytes=VMEM_LIMIT),
        name="combine_ln2",
    )(x1, yg, yg, yg, yg, rw, mod, l2g, l2b)


def _route_kernel(ridx_ref, dest_ref, cnt_ref, run_ref, bst_ref):
    phase = pl.program_id(0)
    i = pl.program_id(1)
    tm = ridx_ref.shape[0]
    ridx = ridx_ref[...]
    lane = lax.broadcasted_iota(jnp.int32, (tm, LANES), 1)
    hits = [ridx[:, kk:kk + 1] == lane for kk in range(TOP_K)]
    chosen = jnp.where(hits[0], 1.0, 0.0)
    for kk in range(1, TOP_K):
        chosen = chosen + jnp.where(hits[kk], 1.0, 0.0)
    colsum = jnp.sum(chosen, axis=0, keepdims=True)

    @pl.when((phase == 0) & (i == 0))
    def _():
        run_ref[...] = jnp.zeros_like(run_ref)

    @pl.when(phase == 0)
    def _():
        run_ref[...] = run_ref[...] + colsum

    @pl.when((phase == 1) & (i == 0))
    def _():
        counts = run_ref[...]
        cnt_ref[...] = counts
        blocks = jnp.floor((counts + (MOE_ROWS - 1.0)) * (1.0 / MOE_ROWS))
        r = lax.broadcasted_iota(jnp.int32, (LANES, LANES), 0)
        c = lax.broadcasted_iota(jnp.int32, (LANES, LANES), 1)
        before = jnp.dot(blocks, (r < c).astype(F32), precision=HIGHEST, preferred_element_type=F32)
        bst_ref[...] = before * float(MOE_ROWS)
        run_ref[...] = jnp.zeros_like(run_ref)

    @pl.when(phase == 1)
    def _():
        rt = lax.broadcasted_iota(jnp.int32, (tm, tm), 0)
        ct = lax.broadcasted_iota(jnp.int32, (tm, tm), 1)
        earlier = _dot((ct < rt).astype(BF16), chosen.astype(BF16))
        row_of = bst_ref[0:1, :] + run_ref[0:1, :] + earlier
        out = jnp.zeros((tm, LANES), F32)
        for kk in range(TOP_K):
            dk = jnp.sum(jnp.where(hits[kk], row_of, 0.0), axis=-1, keepdims=True)
            out = jnp.where(lane == kk, dk, out)
        dest_ref[...] = out.T[0:SUBLANES, :].astype(jnp.int32)
        run_ref[...] = run_ref[...] + colsum


def _route(ridx):
    t_all = ridx.shape[0]
    tm = TM_ROUTE
    return pl.pallas_call(
        _route_kernel,
        grid=(2, t_all // tm),
        in_specs=[pl.BlockSpec((tm, LANES), lambda p, i: (i, 0))],
        out_specs=[pl.BlockSpec((SUBLANES, tm), lambda p, i: (0, i * p)),
                   pl.BlockSpec((SUBLANES, LANES), lambda p, i: (0, 0))],
        out_shape=[jax.ShapeDtypeStruct((SUBLANES, t_all), jnp.int32),
                   jax.ShapeDtypeStruct((SUBLANES, LANES), F32)],
        scratch_shapes=[pltpu.VMEM((SUBLANES, LANES), F32),
                        pltpu.VMEM((SUBLANES, LANES), F32)],
        compiler_params=pltpu.CompilerParams(
            dimension_semantics=("arbitrary", "arbitrary"), vmem_limit_bytes=VMEM_LIMIT),
        name="moe_route",
    )(ridx)


def _routing_tables(counts, n_blocks):
    experts = jnp.arange(N_EXPERTS, dtype=jnp.int32)
    blocks_per = (counts + MOE_ROWS - 1) // MOE_ROWS
    bends = jnp.cumsum(blocks_per)
    bstarts = bends - blocks_per
    blocks = jnp.arange(n_blocks, dtype=jnp.int32)
    block_expert = jnp.minimum(
        jnp.sum((bends[None, :] <= blocks[:, None]).astype(jnp.int32), axis=1), N_EXPERTS - 1)
    n_used = bends[-1:].astype(jnp.int32)
    owner = block_expert[:, None] == experts[None, :]

    def per_block(table):
        return jnp.sum(jnp.where(owner, table[None, :], 0), axis=1)

    n_valid = jnp.clip(per_block(counts) - (blocks - per_block(bstarts)) * MOE_ROWS,
                       0, MOE_ROWS).astype(jnp.int32)
    present = blocks_per > 0
    ordinal = jnp.cumsum(present.astype(jnp.int32)) - 1
    later = lax.cummin(jnp.where(present, experts, N_EXPERTS), reverse=True)
    succ = jnp.concatenate([later[1:], jnp.full((1,), N_EXPERTS, jnp.int32)])
    succ = jnp.where(succ >= N_EXPERTS, -1, succ)
    stage_slot = (per_block(ordinal) % 2).astype(jnp.int32)
    next_expert = per_block(succ).astype(jnp.int32)
    return (block_expert.astype(jnp.int32), n_used, n_valid, stage_slot, next_expert)


def _pos_embed_2d(n_tokens):
    rows = n_tokens // GRID_W
    r = jnp.repeat(jnp.arange(rows), GRID_W).astype(F32)
    col = jnp.tile(jnp.arange(GRID_W), rows).astype(F32)
    quarter = D_MODEL // 4
    omega = 1.0 / (10000.0 ** (jnp.arange(quarter, dtype=F32) / quarter))
    er = r[:, None] * omega
    ec = col[:, None] * omega
    return jnp.concatenate([jnp.sin(er), jnp.cos(er), jnp.sin(ec), jnp.cos(ec)], axis=-1)


def _reorder_w_in(w):
    o_r = 2 * GLA_DK + 2 * GLA_DV
    o_f = o_r + DECAY_RANK
    o_gate = o_f + FNET_DIM
    pad = jnp.zeros((w.shape[0], LANES - DECAY_RANK), w.dtype)
    w_q = w[:, :GLA_DK] * (DK_HEAD ** -0.5)
    return jnp.concatenate([w_q, w[:, GLA_DK:o_r], w[:, o_gate:], w[:, o_f:o_gate], w[:, o_r:o_f], pad],
                           axis=1)


def kernel(x_prompt, x_sample, state_gla, c, c_ctx, w_ada, b_ada, w_in, w_dec_fwd, b_dec_fwd,
           w_dec_bwd, b_dec_bwd, gla_norm_g, w_br_gla, w_br_fnet, w_out, ln1_g, ln1_b, w_router,
           b_router, w_gate_up, b_gate_up, w_down, b_down, ln2_g, ln2_b):
    n_req, ctx_len, _ = x_prompt.shape
    n_lat, lat_len, _ = x_sample.shape
    depth = w_in.shape[0]
    alpha = (2.0 * depth) ** 0.25
    t_ctx = n_req * ctx_len
    t_lat = n_lat * lat_len
    t_all = t_ctx + t_lat

    x_ctx = x_prompt.reshape(t_ctx, D_MODEL)
    x_lat = x_sample.reshape(t_lat, D_MODEL)
    pos = _pos_embed_2d(lat_len)
    zero_pos = jnp.zeros_like(pos)

    cond_rows = -(-(n_lat + 1) // SUBLANES) * SUBLANES
    cond = jnp.zeros((cond_rows, D_MODEL), F32).at[:n_lat].set(c).at[cond_rows - 1].set(c_ctx)

    n_moe_blocks = (t_all * TOP_K) // MOE_ROWS + N_EXPERTS
    tok_chunks = t_all // (SC_WORKERS * SC_WINDOW)
    states = []
    for l in range(depth):
        mod = _ada(cond, w_ada[l], b_ada[l]).reshape(cond_rows, N_MOD, D_MODEL)
        layer_pos = pos if l == 0 else zero_pos
        proj = _inproj(x_ctx, x_lat, layer_pos, mod, _reorder_w_in(w_in[l]).astype(BF16), lat_len)

        def pad_dec(w):
            return jnp.zeros((LANES, GLA_DK), F32).at[:DECAY_RANK].set(w)

        dec = (pad_dec(w_dec_fwd[l]), b_dec_fwd[l].reshape(1, GLA_DK),
               pad_dec(w_dec_bwd[l]), b_dec_bwd[l].reshape(1, GLA_DK),
               gla_norm_g[l].reshape(1, DV_HEAD))
        o_ctx, s_new = _gla(proj, *dec, None, n_seq=n_req, seq_len=ctx_len, row0=0, emit_state=True)
        (o_lat,) = _gla(proj, *dec, state_gla[:, l], n_seq=n_lat, seq_len=lat_len, row0=t_ctx,
                        emit_state=False)
        states.append(s_new)

        mixed_ctx = _fnet(proj, n_seq=n_req, seq_len=ctx_len, row0=0)
        mixed_lat = _fnet(proj, n_seq=n_lat, seq_len=lat_len, row0=t_ctx)

        wr = jnp.zeros((D_MODEL, LANES), F32).at[:, :N_EXPERTS].set(w_router[l])
        br = jnp.full((1, LANES), -1e30, F32).at[0, :N_EXPERTS].set(b_router[l])
        wr_hi = wr.astype(BF16)
        wr_lo = (wr - wr_hi.astype(F32)).astype(BF16)
        x1, h2, ridx, rw = _merge(
            x_ctx, x_lat, layer_pos, mod, o_ctx, o_lat, mixed_ctx, mixed_lat, proj,
            w_br_gla[l].astype(BF16), w_br_fnet[l].astype(BF16), (0.5 * w_out[l]).astype(BF16),
            ln1_g[l].reshape(1, D_MODEL), ln1_b[l].reshape(1, D_MODEL), wr_hi, wr_lo, br, lat_len,
            alpha)

        dest, counts = _route(ridx)
        moe_tables = _routing_tables(counts[0, :N_EXPERTS].astype(jnp.int32), n_moe_blocks)
        dest = dest[:TOP_K]
        scatter_idx = dest.reshape(TOP_K, SC_WORKERS, tok_chunks, SC_WINDOW)
        p_rows = n_moe_blocks * MOE_ROWS
        xs = _sc_scatter_rows(h2.reshape(t_all, ROW_TILES, LANES), scatter_idx, p_rows)
        yb = _moe(moe_tables, xs.reshape(p_rows * ROW_TILES, LANES),
                  w_gate_up[l], b_gate_up[l], w_down[l], b_down[l])
        yb = yb.reshape(p_rows, ROW_TILES, LANES)

        def gathered(row0, n_rows):
            idx = dest[:, row0:row0 + n_rows].reshape(SC_WORKERS, -1, SC_WINDOW)
            return _sc_gather_rows(yb, idx).reshape(TOP_K * n_rows * ROW_TILES, LANES)

        l2g = ln2_g[l].reshape(1, D_MODEL)
        l2b = ln2_b[l].reshape(1, D_MODEL)
        tiles_per_seq = lat_len // TM_MIX
        yg_ctx = gathered(0, t_ctx)
        yg_lat = gathered(t_ctx, t_lat)
        x_ctx = _combine(x1, yg_ctx, rw, mod, l2g, l2b, row0=0, n_rows=t_ctx,
                         mod_map=lambda i: (cond_rows - 1, 0, 0), alpha=alpha)
        x_lat = _combine(x1, yg_lat, rw, mod, l2g, l2b, row0=t_ctx, n_rows=t_lat,
                         mod_map=lambda i: (i // tiles_per_seq, 0, 0), alpha=alpha)

    y_prompt = x_ctx.reshape(x_prompt.shape)
    y_sample = x_lat.reshape(x_sample.shape)
    new_state = jnp.stack(states, axis=1).astype(x_prompt.dtype)
    return (y_prompt, y_sample, new_state)
```

```python
import functools
import math

import numpy as np
import jax
import jax.numpy as jnp
from jax import lax
from jax.experimental import pallas as pl
from jax.experimental.pallas import tpu as pltpu
from jax.experimental.pallas import tpu_sc as plsc

F32 = jnp.float32
BF16 = jnp.bfloat16

D_MODEL = 1024
GRID_W = 64
GLA_HEADS = 4
DK_HEAD = 128
DV_HEAD = 256
GLA_DK = GLA_HEADS * DK_HEAD
GLA_DV = GLA_HEADS * DV_HEAD
DECAY_RANK = 16
GATE_NORMALIZER = 16.0
FNET_GROUPS = 4
FNET_GROUP_DIM = 128
FNET_DIM = FNET_GROUPS * FNET_GROUP_DIM
N_EXPERTS = 32
TOP_K = 4
D_EXPERT = 1024
SWIGLU_LIMIT = 7.0
SWIGLU_ALPHA = 1.702
LN_EPS = 1e-6
N_MOD = 6

LANES = 128
SUBLANES = 8
HALF_MODEL = D_MODEL // 2
ROW_TILES = HALF_MODEL // LANES
ROW_DTYPE = jnp.uint32
COL_Q = 0
COL_K = GLA_DK
COL_V = 2 * GLA_DK
COL_G = COL_V + GLA_DV
COL_GATE_A = COL_G + GLA_DV
COL_GATE_B = COL_GATE_A + D_MODEL
COL_F = COL_GATE_B + D_MODEL
COL_R = COL_F + FNET_DIM
PROJ_COLS = COL_R + LANES

GLA_CHUNK = 128
GLA_LEAF = 16
TM_PROJ = 256
TM_MIX = 256
TM_MERGE = 512
TM_MERGE_GROUP = 128
TM_ROUTE = 1024
MOE_ROWS = 512
MOE_GROUP = 256
VMEM_LIMIT = 56 * 1024 * 1024

SC_CORES = 2
SC_SUBCORES = 16
SC_WORKERS = SC_CORES * SC_SUBCORES
SC_WINDOW = 64

HIGHEST = lax.Precision.HIGHEST


def _layer_norm(x):
    mu = jnp.mean(x, axis=-1, keepdims=True)
    xc = x - mu
    var = jnp.mean(xc * xc, axis=-1, keepdims=True)
    return xc * lax.rsqrt(var + LN_EPS)


def _sigmoid(x):
    return 0.5 * jnp.tanh(0.5 * x) + 0.5


def _log_sigmoid(z):
    return jnp.minimum(z, 0.0) - jnp.log(1.0 + jnp.exp(-jnp.abs(z)))


def _dot(a, b):
    return jnp.dot(a, b, preferred_element_type=F32)


def _split_bf16(x, terms):
    parts = []
    for _ in range(terms):
        p = x.astype(BF16)
        parts.append(p)
        x = x - p.astype(F32)
    return parts


def _dot_nt(a, b):
    return lax.dot_general(a, b, (((1,), (1,)), ((), ())), preferred_element_type=F32)


def _dot_tn(a, b):
    return lax.dot_general(a, b, (((0,), (0,)), ((), ())), preferred_element_type=F32)


def _row_tile_slice(j, n_rows, first_row=0):
    return pl.ds(first_row * ROW_TILES + j, n_rows, stride=ROW_TILES)


def _store_row_tiles(ref, val, first_row=0):
    for j in range(ROW_TILES):
        lo = val[:, j * LANES:(j + 1) * LANES]
        hi = val[:, HALF_MODEL + j * LANES:HALF_MODEL + (j + 1) * LANES]
        ref[_row_tile_slice(j, val.shape[0], first_row), :] = pltpu.pack_elementwise(
            [lo, hi], packed_dtype=BF16)


def _load_row_tile(ref, j, n_rows, first_row=0):
    words = ref[_row_tile_slice(j, n_rows, first_row), :]
    return tuple(pltpu.unpack_elementwise(words, index=half, packed_dtype=BF16, unpacked_dtype=F32)
                 for half in range(2))


def _ada_kernel(c_ref, w_ref, b_ref, o_ref):
    c = c_ref[...]
    s = c * _sigmoid(c)
    o_ref[...] = _dot(s.astype(BF16), w_ref[...].astype(BF16)) + b_ref[...]


def _ada(cond, w_ada, b_ada):
    rows = cond.shape[0]
    n = w_ada.shape[1]
    tn = 1536
    return pl.pallas_call(
        _ada_kernel,
        grid=(n // tn,),
        in_specs=[pl.BlockSpec((rows, D_MODEL), lambda j: (0, 0)),
                  pl.BlockSpec((D_MODEL, tn), lambda j: (0, j)),
                  pl.BlockSpec((1, tn), lambda j: (0, j))],
        out_specs=pl.BlockSpec((rows, tn), lambda j: (0, j)),
        out_shape=jax.ShapeDtypeStruct((rows, n), F32),
        compiler_params=pltpu.CompilerParams(vmem_limit_bytes=VMEM_LIMIT),
        name="ada_mod",
    )(cond, w_ada, b_ada.reshape(1, n))


def _group_maps(n_ctx_tiles):
    def ctx_map(i, *_):
        return (jnp.minimum(i, n_ctx_tiles - 1), 0)

    def lat_map(i, *_):
        return (jnp.maximum(i - n_ctx_tiles, 0), 0)

    return ctx_map, lat_map


def _token_specs(tm, n_ctx_tiles, tiles_per_latent_seq, ctx_mod_row):
    ctx_map, lat_map = _group_maps(n_ctx_tiles)

    def pos_map(i, *_):
        return (jnp.maximum(i - n_ctx_tiles, 0) % tiles_per_latent_seq, 0)

    def mod_map(i, *_):
        return (jnp.where(i < n_ctx_tiles, ctx_mod_row,
                          jnp.maximum(i - n_ctx_tiles, 0) // tiles_per_latent_seq), 0, 0)

    return [pl.BlockSpec((tm, D_MODEL), ctx_map),
            pl.BlockSpec((tm, D_MODEL), lat_map),
            pl.BlockSpec((tm, D_MODEL), pos_map),
            pl.BlockSpec((1, N_MOD, D_MODEL), mod_map)]


def _inproj_kernel(xc_ref, xl_ref, pos_ref, mod_ref, w_ref, wdh_ref, wdl_ref, bd_ref, o_ref, la_ref,
                   *, n_ctx_tiles):
    i = pl.program_id(0)

    def project(x):
        h = (_layer_norm(x) * (1.0 + mod_ref[0, 1:2, :]) + mod_ref[0, 0:1, :]).astype(BF16)
        o_ref[:, :COL_G] = _dot(h, w_ref[:, :COL_G])
        hg = 0.5 * _dot(h, w_ref[:, COL_G:COL_GATE_A])
        o_ref[:, COL_G:COL_GATE_A] = hg * (jnp.tanh(hg) + 1.0)
        gates = _dot(h, w_ref[:, COL_GATE_A:COL_F])
        o_ref[:, COL_GATE_A:COL_F] = jnp.tanh(0.5 * gates) + 1.0
        o_ref[:, COL_F:COL_R] = _dot(h, w_ref[:, COL_F:COL_R])
        r_hi, r_lo = _split_bf16(_dot(h, w_ref[:, COL_R:]), 2)
        for d in range(2):
            z = (_dot(r_hi, wdh_ref[d]) + _dot(r_lo, wdh_ref[d]) + _dot(r_hi, wdl_ref[d])
                 + bd_ref[d])
            la_ref[:, d * GLA_DK:(d + 1) * GLA_DK] = _log_sigmoid(z) * (1.0 / GATE_NORMALIZER)

    @pl.when(i < n_ctx_tiles)
    def _():
        project(xc_ref[...])

    @pl.when(i >= n_ctx_tiles)
    def _():
        project(xl_ref[...] + pos_ref[...])


def _inproj(x_ctx, x_lat, pos, mod, w_in_bf, w_dec, b_dec, lat_len):
    t_ctx, t_lat = x_ctx.shape[0], x_lat.shape[0]
    t_all = t_ctx + t_lat
    n_ctx_tiles = t_ctx // TM_PROJ
    kern = functools.partial(_inproj_kernel, n_ctx_tiles=n_ctx_tiles)
    specs = _token_specs(TM_PROJ, n_ctx_tiles, lat_len // TM_PROJ, mod.shape[0] - 1)
    w_dec_hi = w_dec.astype(BF16)
    w_dec_lo = (w_dec - w_dec_hi.astype(F32)).astype(BF16)

    def const(shape):
        return pl.BlockSpec(shape, lambda i: (0,) * len(shape))

    return pl.pallas_call(
        kern,
        grid=(t_all // TM_PROJ,),
        in_specs=specs + [pl.BlockSpec((D_MODEL, PROJ_COLS), lambda i: (0, 0),
                                       pipeline_mode=pl.Buffered(1)),
                          const((2, LANES, GLA_DK)), const((2, LANES, GLA_DK)), const((2, 1, GLA_DK))],
        out_specs=[pl.BlockSpec((TM_PROJ, COL_R), lambda i: (i, 0)),
                   pl.BlockSpec((TM_PROJ, 2 * GLA_DK), lambda i: (i, 0))],
        out_shape=[jax.ShapeDtypeStruct((t_all, COL_R), F32),
                   jax.ShapeDtypeStruct((t_all, 2 * GLA_DK), F32)],
        compiler_params=pltpu.CompilerParams(
            dimension_semantics=("arbitrary",), vmem_limit_bytes=VMEM_LIMIT),
        name="ln_inproj",
    )(x_ctx, x_lat, pos, mod, w_in_bf, w_dec_hi, w_dec_lo, b_dec)


def _gla_kernel(*refs, seq_len, has_s0, emit_state):
    it = iter(refs)
    q_ref, k_ref, v_ref, laf_ref, lab_ref, g_ref = (next(it) for _ in range(6))
    s0_ref = next(it) if has_s0 else None
    o_ref = next(it)
    sout_ref = next(it) if emit_state else None
    cum_ref, a_ref, qi_ref, ko_ref, dec_ref, op_ref, st_ref = (next(it) for _ in range(7))

    C = GLA_CHUNK
    n_chunks = seq_len // C

    def rows(n):
        if isinstance(n, int):
            return pl.ds(n * C, C)
        return pl.ds(pl.multiple_of(n * C, C), C)

    def dec_rows(n):
        if isinstance(n, int):
            return pl.ds(n * SUBLANES, SUBLANES)
        return pl.ds(pl.multiple_of(n * SUBLANES, SUBLANES), SUBLANES)

    def loop(body):
        if n_chunks <= 2:
            for n in range(n_chunks):
                body(n)
        else:
            def step(m, carry):
                body(2 * m)
                body(2 * m + 1)
                return carry
            lax.fori_loop(0, n_chunks // 2, step, 0)

    rt = lax.broadcasted_iota(jnp.int32, (C, C), 0)
    ct = lax.broadcasted_iota(jnp.int32, (C, C), 1)
    tri = ((rt >= ct).astype(BF16), (ct >= rt).astype(BF16))
    row_id = lax.broadcasted_iota(jnp.int32, (C, DK_HEAD), 0)

    def cumsum_chunk(n):
        for d, la_ref in enumerate((laf_ref, lab_ref)):
            la_hi, la_lo = _split_bf16(la_ref[rows(n), :], 2)
            cum_ref[d, rows(n), :] = _dot(tri[d], la_hi) + _dot(tri[d], la_lo)

    loop(cumsum_chunk)

    query_rows = ({}, {})
    keep = ({}, {})
    for d in range(2):
        blk = C // 2
        while blk >= GLA_LEAF:
            q_parity = 1 if d == 0 else 0
            query_rows[d][blk] = ((row_id // blk) % 2) == q_parity
            qb, kb = rt // blk, ct // blk
            keep[d][blk] = ((qb % 2) == q_parity) & ((qb == kb + 1) if d == 0 else (kb == qb + 1))
            blk //= 2
        order = (rt >= ct) if d == 0 else (ct >= rt)
        keep[d][0] = ((rt // GLA_LEAF) == (ct // GLA_LEAF)) & order

    def block_rows(cum, first, step, count):
        span = C // count
        parts = [jnp.broadcast_to(cum[first + p * step:first + p * step + 1, :], (span, DK_HEAD))
                 for p in range(count)]
        return parts[0] if count == 1 else jnp.concatenate(parts, axis=0)

    def scores(n, d):
        cum = cum_ref[d, rows(n), :]
        q = q_ref[rows(n), :]
        k = k_ref[rows(n), :]
        acc = None
        blk = C // 2
        while blk >= GLA_LEAF:
            pairs = C // (2 * blk)
            bnd = blk - 1 if d == 0 else blk
            w = jnp.exp(-jnp.abs(cum - block_rows(cum, bnd, 2 * blk, pairs)))
            x = (jnp.where(query_rows[d][blk], q, k) * w).astype(BF16)
            s = jnp.where(keep[d][blk], _dot_nt(x, x), 0.0)
            acc = s if acc is None else acc + s
            blk //= 2
        mid = GLA_LEAF // 2 - 1 if d == 0 else GLA_LEAF // 2
        e = cum - block_rows(cum, mid, GLA_LEAF, C // GLA_LEAF)
        s = _dot_nt((q * jnp.exp(e)).astype(BF16), (k * jnp.exp(-e)).astype(BF16))
        acc = acc + jnp.where(keep[d][0], s, 0.0)
        a_ref[d, rows(n), :] = acc.astype(BF16)
        end = cum[C - 1:C, :] if d == 0 else cum[0:1, :]
        qi_ref[d, rows(n), :] = (q * jnp.exp(cum)).astype(BF16)
        ko_ref[d, rows(n), :] = (k * jnp.exp(end - cum)).astype(BF16)
        dec_ref[d, dec_rows(n), :] = jnp.broadcast_to(jnp.exp(end), (SUBLANES, DK_HEAD))

    def scores_chunk(n):
        scores(n, 0)
        scores(n, 1)

    loop(scores_chunk)

    for d in range(2):
        if has_s0:
            st_ref[d] = s0_ref[0, d, 0].T
        else:
            st_ref[d] = jnp.zeros((DV_HEAD, DK_HEAD), F32)

    def scan(n, d):
        v = v_ref[rows(n), :].astype(BF16)
        st = st_ref[d]
        o = _dot(a_ref[d, rows(n), :], v) + _dot_nt(qi_ref[d, rows(n), :], st.astype(BF16))
        st_ref[d] = st * dec_ref[d, dec_rows(n), :][0:1, :] + _dot_tn(v, ko_ref[d, rows(n), :])
        op_ref[d, rows(n), :] = o

    def scan_chunk(m):
        scan(m, 0)
        scan(n_chunks - 1 - m, 1)

    loop(scan_chunk)
    if emit_state:
        sout_ref[0, 0, 0] = st_ref[0].T
        sout_ref[0, 1, 0] = st_ref[1].T

    g = g_ref[...]

    def finish_chunk(n):
        o = op_ref[0, rows(n), :] + op_ref[1, rows(n), :]
        ms = jnp.mean(o * o, axis=-1, keepdims=True)
        o_ref[rows(n), :] = o * lax.rsqrt(ms + LN_EPS) * g

    loop(finish_chunk)


def _gla(proj, la, g, s0, *, n_seq, seq_len, row0, emit_state):
    has_s0 = s0 is not None
    blk0 = row0 // seq_len
    kern = functools.partial(_gla_kernel, seq_len=seq_len, has_s0=has_s0, emit_state=emit_state)
    in_specs = [
        pl.BlockSpec((seq_len, DK_HEAD), lambda b, h: (blk0 + b, COL_Q // DK_HEAD + h)),
        pl.BlockSpec((seq_len, DK_HEAD), lambda b, h: (blk0 + b, COL_K // DK_HEAD + h)),
        pl.BlockSpec((seq_len, DV_HEAD), lambda b, h: (blk0 + b, COL_V // DV_HEAD + h)),
        pl.BlockSpec((seq_len, DK_HEAD), lambda b, h: (blk0 + b, h)),
        pl.BlockSpec((seq_len, DK_HEAD), lambda b, h: (blk0 + b, GLA_HEADS + h)),
        pl.BlockSpec((1, DV_HEAD), lambda b, h: (0, 0)),
    ]
    args = [proj, proj, proj, la, la, g]
    if has_s0:
        in_specs.append(pl.BlockSpec((1, 2, 1, DK_HEAD, DV_HEAD), lambda b, h: (b, 0, h, 0, 0)))
        args.append(s0)
    out_specs = [pl.BlockSpec((seq_len, DV_HEAD), lambda b, h: (b, h))]
    out_shape = [jax.ShapeDtypeStruct((n_seq * seq_len, GLA_DV), F32)]
    if emit_state:
        out_specs.append(pl.BlockSpec((1, 2, 1, DK_HEAD, DV_HEAD), lambda b, h: (b, 0, h, 0, 0)))
        out_shape.append(jax.ShapeDtypeStruct((n_seq, 2, GLA_HEADS, DK_HEAD, DV_HEAD), F32))

    res = pl.pallas_call(
        kern,
        grid=(n_seq, GLA_HEADS),
        in_specs=in_specs,
        out_specs=out_specs,
        out_shape=out_shape,
        scratch_shapes=[pltpu.VMEM((2, seq_len, DK_HEAD), F32),
                        pltpu.VMEM((2, seq_len, GLA_CHUNK), BF16),
                        pltpu.VMEM((2, seq_len, DK_HEAD), BF16),
                        pltpu.VMEM((2, seq_len, DK_HEAD), BF16),
                        pltpu.VMEM((2, seq_len // GLA_CHUNK * SUBLANES, DK_HEAD), F32),
                        pltpu.VMEM((2, seq_len, DV_HEAD), F32),
                        pltpu.VMEM((2, DV_HEAD, DK_HEAD), F32)],
        compiler_params=pltpu.CompilerParams(
            dimension_semantics=("arbitrary", "arbitrary"), vmem_limit_bytes=VMEM_LIMIT),
        name="gla_seq%d" % seq_len,
    )(*args)
    return res


def _fnet_kernel(f_ref, cl_ref, sl_ref, cg_ref, sg_ref, o_ref, uc_ref, us_ref, *, seq_len):
    cg = cg_ref[...]
    sg = sg_ref[...]
    for grp in range(FNET_GROUPS):
        lo = grp * FNET_GROUP_DIM
        u = f_ref[:, lo:lo + FNET_GROUP_DIM].astype(BF16)
        uc_ref[:, lo:lo + FNET_GROUP_DIM] = _dot(u, cg).astype(BF16)
        us_ref[:, lo:lo + FNET_GROUP_DIM] = _dot(u, sg).astype(BF16)
    mixed = _dot(cl_ref[...], uc_ref[...]) - _dot(sl_ref[...], us_ref[...])
    o_ref[...] = mixed * (1.0 / math.sqrt(seq_len * FNET_GROUP_DIM))


def _dft_mats(n):
    j = np.arange(n, dtype=np.int64)
    ang = (2.0 * np.pi / n) * ((j[:, None] * j[None, :]) % n).astype(np.float64)
    return (jnp.asarray(np.cos(ang), dtype=F32).astype(BF16),
            jnp.asarray(np.sin(ang), dtype=F32).astype(BF16))


def _fnet(proj, *, n_seq, seq_len, row0):
    blk0 = row0 // seq_len
    cl, sl = _dft_mats(seq_len)
    cg, sg = _dft_mats(FNET_GROUP_DIM)
    kern = functools.partial(_fnet_kernel, seq_len=seq_len)
    return pl.pallas_call(
        kern,
        grid=(n_seq,),
        in_specs=[pl.BlockSpec((seq_len, FNET_DIM), lambda b: (blk0 + b, COL_F // FNET_DIM)),
                  pl.BlockSpec((seq_len, seq_len), lambda b: (0, 0)),
                  pl.BlockSpec((seq_len, seq_len), lambda b: (0, 0)),
                  pl.BlockSpec((FNET_GROUP_DIM, FNET_GROUP_DIM), lambda b: (0, 0)),
                  pl.BlockSpec((FNET_GROUP_DIM, FNET_GROUP_DIM), lambda b: (0, 0))],
        out_specs=pl.BlockSpec((seq_len, FNET_DIM), lambda b: (b, 0)),
        out_shape=jax.ShapeDtypeStruct((n_seq * seq_len, FNET_DIM), F32),
        scratch_shapes=[pltpu.VMEM((seq_len, FNET_DIM), BF16),
                        pltpu.VMEM((seq_len, FNET_DIM), BF16)],
        compiler_params=pltpu.CompilerParams(
            dimension_semantics=("arbitrary",), vmem_limit_bytes=VMEM_LIMIT),
        name="fnet_seq%d" % seq_len,
    )(proj, cl, sl, cg, sg)


def _merge_kernel(xc_ref, xl_ref, pos_ref, mod_ref, oc_ref, ol_ref, mc_ref, ml_ref,
                  g_ref, ga_ref, gb_ref, wbg_ref, wbf_ref, wo_ref, l1g_ref, l1b_ref, wrh_ref, wrl_ref,
                  br_ref, x1_ref, h2_ref, ridx_ref, rw_ref, *, n_ctx_tiles, alpha):
    i = pl.program_id(0)

    def compute(rows, x, o, mx):
        a = (o * g_ref[rows, :]).astype(BF16)
        gla_out = _dot(a, wbg_ref[...])
        fnet_out = _dot(mx.astype(BF16), wbf_ref[...])
        merged = ga_ref[rows, :] * gla_out + gb_ref[rows, :] * fnet_out
        mix = _dot(merged.astype(BF16), wo_ref[...])
        y = alpha * x + mod_ref[0, 2:3, :] * mix
        x1 = _layer_norm(y) * l1g_ref[...] + l1b_ref[...]
        x1_ref[rows, :] = x1
        h2 = _layer_norm(x1) * (1.0 + mod_ref[0, 4:5, :]) + mod_ref[0, 3:4, :]
        _store_row_tiles(h2_ref, h2, rows.start)

        h_hi, h_lo = _split_bf16(h2, 2)
        logits = (_dot(h_hi, wrh_ref[...]) + _dot(h_lo, wrh_ref[...]) + _dot(h_hi, wrl_ref[...])
                  + br_ref[...])
        lane_i = lax.broadcasted_iota(jnp.int32, logits.shape, 1)
        lane = lane_i.astype(F32)
        idx_out = jnp.zeros(logits.shape, F32)
        val_out = jnp.zeros(logits.shape, F32)
        top0 = None
        denom = None
        for kk in range(TOP_K):
            m = jnp.max(logits, axis=-1, keepdims=True)
            sel = jnp.min(jnp.where(logits == m, lane, float(LANES)), axis=-1, keepdims=True)
            if kk == 0:
                top0 = m
                p = jnp.ones_like(m)
                denom = p
            else:
                p = jnp.exp(m - top0)
                denom = denom + p
            idx_out = jnp.where(lane_i == kk, sel, idx_out)
            val_out = jnp.where(lane_i == kk, p, val_out)
            logits = jnp.where(lane == sel, -jnp.inf, logits)
        ridx_ref[rows, :] = idx_out.astype(jnp.int32)
        rw_ref[rows, :] = val_out / denom

    tm = x1_ref.shape[0]
    groups = [slice(r0, r0 + TM_MERGE_GROUP) for r0 in range(0, tm, TM_MERGE_GROUP)]

    @pl.when(i < n_ctx_tiles)
    def _():
        for rows in groups:
            compute(rows, xc_ref[rows, :], oc_ref[rows, :], mc_ref[rows, :])

    @pl.when(i >= n_ctx_tiles)
    def _():
        for rows in groups:
            compute(rows, xl_ref[rows, :] + pos_ref[rows, :], ol_ref[rows, :], ml_ref[rows, :])


def _merge(x_ctx, x_lat, pos, mod, o_ctx, o_lat, mixed_ctx, mixed_lat, proj,
           wbg, wbf, wo, l1g, l1b, wr_hi, wr_lo, br, lat_len, alpha):
    t_ctx, t_lat = x_ctx.shape[0], x_lat.shape[0]
    t_all = t_ctx + t_lat
    tm = TM_MERGE
    n_ctx_tiles = t_ctx // tm
    kern = functools.partial(_merge_kernel, n_ctx_tiles=n_ctx_tiles, alpha=alpha)
    specs = _token_specs(tm, n_ctx_tiles, lat_len // tm, mod.shape[0] - 1)
    ctx_map, lat_map = _group_maps(n_ctx_tiles)

    def const(shape):
        return pl.BlockSpec(shape, lambda i: (0,) * len(shape))

    in_specs = specs + [
        pl.BlockSpec((tm, GLA_DV), ctx_map),
        pl.BlockSpec((tm, GLA_DV), lat_map),
        pl.BlockSpec((tm, FNET_DIM), ctx_map),
        pl.BlockSpec((tm, FNET_DIM), lat_map),
        pl.BlockSpec((tm, GLA_DV), lambda i: (i, COL_G // GLA_DV)),
        pl.BlockSpec((tm, D_MODEL), lambda i: (i, COL_GATE_A // D_MODEL)),
        pl.BlockSpec((tm, D_MODEL), lambda i: (i, COL_GATE_B // D_MODEL)),
        const((GLA_DV, D_MODEL)), const((FNET_DIM, D_MODEL)), const((D_MODEL, D_MODEL)),
        const((1, D_MODEL)), const((1, D_MODEL)),
        const((D_MODEL, LANES)), const((D_MODEL, LANES)), const((1, LANES)),
    ]
    out_specs = [pl.BlockSpec((tm, D_MODEL), lambda i: (i, 0)),
                 pl.BlockSpec((tm * ROW_TILES, LANES), lambda i: (i, 0)),
                 pl.BlockSpec((tm, LANES), lambda i: (i, 0)),
                 pl.BlockSpec((tm, LANES), lambda i: (i, 0))]
    out_shape = [jax.ShapeDtypeStruct((t_all, D_MODEL), F32),
                 jax.ShapeDtypeStruct((t_all * ROW_TILES, LANES), ROW_DTYPE),
                 jax.ShapeDtypeStruct((t_all, LANES), jnp.int32),
                 jax.ShapeDtypeStruct((t_all, LANES), F32)]
    return pl.pallas_call(
        kern,
        grid=(t_all // tm,),
        in_specs=in_specs,
        out_specs=out_specs,
        out_shape=out_shape,
        compiler_params=pltpu.CompilerParams(
            dimension_semantics=("arbitrary",), vmem_limit_bytes=VMEM_LIMIT),
        name="merge_ln1_router",
    )(x_ctx, x_lat, pos, mod, o_ctx, o_lat, mixed_ctx, mixed_lat, proj, proj, proj,
      wbg, wbf, wo, l1g, l1b, wr_hi, wr_lo, br)


def _sc_mesh():
    return plsc.VectorSubcoreMesh(core_axis_name="c", subcore_axis_name="s")


def _sc_worker_id():
    return lax.axis_index("s") * SC_CORES + lax.axis_index("c")


def _sc_scatter_rows(src, idx, n_out):
    n_src = src.shape[0]
    w = SC_WINDOW
    n_chunks = n_src // (SC_WORKERS * w)
    copies = idx.shape[0]
    assert n_chunks % 2 == 0 and idx.shape == (copies, SC_WORKERS, n_chunks, w)

    @functools.partial(
        pl.kernel, mesh=_sc_mesh(),
        out_type=jax.ShapeDtypeStruct((n_out, ROW_TILES, LANES), ROW_DTYPE),
        scratch_types=[pltpu.VMEM((copies * n_chunks, w), jnp.int32),
                       pltpu.VMEM((2, w, ROW_TILES, LANES), ROW_DTYPE),
                       pltpu.SemaphoreType.DMA((2,)),
                       pltpu.SemaphoreType.DMA((2,))],
        name="moe_dispatch_scatter")
    def k(src_hbm, idx_hbm, out_hbm, idx_v, rows_v, rsem, wsem):
        wid = _sc_worker_id()
        base = wid * (n_chunks * w)
        for kk in range(copies):
            pltpu.sync_copy(idx_hbm.at[kk, wid], idx_v.at[pl.ds(kk * n_chunks, n_chunks)])

        def read(j, slot):
            return pltpu.make_async_copy(src_hbm.at[pl.ds(base + j * w, w)], rows_v.at[slot],
                                         rsem.at[slot])

        def scatter(j, kk, slot):
            return pltpu.make_async_copy(rows_v.at[slot], out_hbm.at[idx_v.at[kk * n_chunks + j]],
                                         wsem.at[slot])

        read(0, 0).start()

        @pl.loop(0, n_chunks, step=2)
        def _(jj):
            read(jj, 0).wait()

            @pl.when(jj > 0)
            def _():
                for kk in range(copies):
                    scatter(jj - 1, kk, 1).wait()

            read(jj + 1, 1).start()
            for kk in range(copies):
                scatter(jj, kk, 0).start()
            read(jj + 1, 1).wait()
            for kk in range(copies):
                scatter(jj, kk, 0).wait()

            @pl.when(jj + 2 < n_chunks)
            def _():
                read(jj + 2, 0).start()

            for kk in range(copies):
                scatter(jj + 1, kk, 1).start()

        for kk in range(copies):
            scatter(n_chunks - 1, kk, 1).wait()

    return k(src, idx)


def _sc_gather_rows(table, idx):
    _, n_chunks, w = idx.shape
    assert n_chunks % 2 == 0 and idx.shape[0] == SC_WORKERS and w == SC_WINDOW
    n_out = SC_WORKERS * n_chunks * w

    @functools.partial(
        pl.kernel, mesh=_sc_mesh(),
        out_type=jax.ShapeDtypeStruct((n_out, ROW_TILES, LANES), ROW_DTYPE),
        scratch_types=[pltpu.VMEM((n_chunks, w), jnp.int32),
                       pltpu.VMEM((2, w, ROW_TILES, LANES), ROW_DTYPE),
                       pltpu.SemaphoreType.DMA((2,)),
                       pltpu.SemaphoreType.DMA((2,))],
        name="moe_combine_gather")
    def k(table_hbm, idx_hbm, out_hbm, idx_v, rows_v, gsem, wsem):
        wid = _sc_worker_id()
        base = wid * (n_chunks * w)
        pltpu.sync_copy(idx_hbm.at[wid], idx_v)

        def gather(j, slot):
            return pltpu.make_async_copy(table_hbm.at[idx_v.at[j]], rows_v.at[slot], gsem.at[slot])

        def write(j, slot):
            return pltpu.make_async_copy(rows_v.at[slot], out_hbm.at[pl.ds(base + j * w, w)],
                                         wsem.at[slot])

        gather(0, 0).start()

        @pl.loop(0, n_chunks, step=2)
        def _(jj):
            gather(jj, 0).wait()

            @pl.when(jj > 0)
            def _():
                write(jj - 1, 1).wait()

            gather(jj + 1, 1).start()
            write(jj, 0).start()
            gather(jj + 1, 1).wait()
            write(jj, 0).wait()

            @pl.when(jj + 2 < n_chunks)
            def _():
                gather(jj + 2, 0).start()

            write(jj + 1, 1).start()

        write(n_chunks - 1, 1).wait()

    return k(table, idx)


def _moe_kernel(be_ref, nu_ref, nv_ref, slot_ref, nxt_ref, x_ref, wgu_hbm, bgu_ref, wd_hbm, bd_ref,
                o_ref, wgu_st, wd_st, wgu_bf, wd_bf, xb_ref, sem):
    b = pl.program_id(0)
    e = be_ref[b]
    prev = be_ref[jnp.maximum(b - 1, 0)]
    active = b < nu_ref[0]
    changed = (b == 0) | (e != prev)

    def weight_copies(expert, s):
        return (pltpu.make_async_copy(wgu_hbm.at[expert], wgu_st.at[s], sem.at[0, s]),
                pltpu.make_async_copy(wd_hbm.at[expert], wd_st.at[s], sem.at[1, s]))

    @pl.when(active & changed)
    def _():
        s = slot_ref[b]

        @pl.when(b == 0)
        def _():
            for cp in weight_copies(e, s):
                cp.start()

        for cp in weight_copies(e, s):
            cp.wait()
        wgu_bf[...] = wgu_st[s].astype(BF16)
        wd_bf[...] = wd_st[s].astype(BF16)
        nxt = nxt_ref[b]

        @pl.when(nxt >= 0)
        def _():
            for cp in weight_copies(nxt, 1 - s):
                cp.start()

    n_valid = nv_ref[b]

    def expert_mlp(groups):
        for r0 in groups:
            valid = lax.broadcasted_iota(jnp.int32, (MOE_GROUP, LANES), 0) < n_valid - r0
            for j in range(ROW_TILES):
                for half, xj in enumerate(_load_row_tile(x_ref, j, MOE_GROUP, r0)):
                    c0 = half * HALF_MODEL + j * LANES
                    xb_ref[r0:r0 + MOE_GROUP, c0:c0 + LANES] = jnp.where(valid, xj, 0.0).astype(BF16)
            gu = _dot(xb_ref[r0:r0 + MOE_GROUP, :], wgu_bf[...]) + bgu_ref[0]
            gate = jnp.minimum(gu[:, :D_EXPERT], SWIGLU_LIMIT)
            up = jnp.clip(gu[:, D_EXPERT:], -SWIGLU_LIMIT, SWIGLU_LIMIT)
            glu = gate * _sigmoid(gate * SWIGLU_ALPHA)
            act = ((up + 1.0) * glu).astype(BF16)
            _store_row_tiles(o_ref, _dot(act, wd_bf[...]) + bd_ref[0], r0)

    all_groups = list(range(0, MOE_ROWS, MOE_GROUP))

    @pl.when(active & (n_valid > MOE_ROWS - MOE_GROUP))
    def _():
        expert_mlp(all_groups)

    @pl.when(active & (n_valid <= MOE_ROWS - MOE_GROUP))
    def _():
        expert_mlp(all_groups[:-1])


def _moe(tables, xs, w_gate_up, b_gate_up, w_down, b_down):
    p_rows = xs.shape[0] // ROW_TILES
    n_blocks = p_rows // MOE_ROWS

    def blk(b, be, nu, *_):
        return jnp.minimum(b, nu[0] - 1)

    def expert(b, be, nu, *_):
        return (be[blk(b, be, nu)], 0, 0)

    def rows(b, be, nu, *_):
        return (blk(b, be, nu), 0)

    grid_spec = pltpu.PrefetchScalarGridSpec(
        num_scalar_prefetch=len(tables),
        grid=(n_blocks,),
        in_specs=[
            pl.BlockSpec((MOE_ROWS * ROW_TILES, LANES), rows),
            pl.BlockSpec(memory_space=pl.ANY),
            pl.BlockSpec((1, 1, 2 * D_EXPERT), expert),
            pl.BlockSpec(memory_space=pl.ANY),
            pl.BlockSpec((1, 1, D_MODEL), expert),
        ],
        out_specs=pl.BlockSpec((MOE_ROWS * ROW_TILES, LANES), rows),
        scratch_shapes=[pltpu.VMEM((2, D_MODEL, 2 * D_EXPERT), F32),
                        pltpu.VMEM((2, D_EXPERT, D_MODEL), F32),
                        pltpu.VMEM((D_MODEL, 2 * D_EXPERT), BF16),
                        pltpu.VMEM((D_EXPERT, D_MODEL), BF16),
                        pltpu.VMEM((MOE_ROWS, D_MODEL), BF16),
                        pltpu.SemaphoreType.DMA((2, 2))],
    )
    return pl.pallas_call(
        _moe_kernel,
        grid_spec=grid_spec,
        out_shape=jax.ShapeDtypeStruct((p_rows * ROW_TILES, LANES), ROW_DTYPE),
        compiler_params=pltpu.CompilerParams(
            dimension_semantics=("arbitrary",), vmem_limit_bytes=VMEM_LIMIT),
        name="moe_grouped_mlp",
    )(*tables, xs, w_gate_up, b_gate_up.reshape(N_EXPERTS, 1, 2 * D_EXPERT), w_down,
      b_down.reshape(N_EXPERTS, 1, D_MODEL))


def _combine_kernel(x1_ref, y0_ref, y1_ref, y2_ref, y3_ref, rw_ref, mod_ref, g_ref, b_ref, o_ref,
                    *, alpha):
    rw = rw_ref[...]
    y_refs = (y0_ref, y1_ref, y2_ref, y3_ref)
    pieces = [None] * (2 * ROW_TILES)
    for j in range(ROW_TILES):
        for kk in range(TOP_K):
            for half, yj in enumerate(_load_row_tile(y_refs[kk], j, rw.shape[0])):
                term = rw[:, kk:kk + 1] * yj
                slot = half * ROW_TILES + j
                pieces[slot] = term if kk == 0 else pieces[slot] + term
    ff = jnp.concatenate(pieces, axis=-1)
    y = alpha * x1_ref[...] + mod_ref[0, 5:6, :] * ff
    o_ref[...] = _layer_norm(y) * g_ref[...] + b_ref[...]


def _combine(x1, yg, rw, mod, l2g, l2b, *, row0, n_rows, mod_map, alpha):
    tm = TM_MIX
    t0 = row0 // tm
    tiles = n_rows // tm
    kern = functools.partial(_combine_kernel, alpha=alpha)

    def y_spec(kk):
        return pl.BlockSpec((tm * ROW_TILES, LANES), lambda i: (kk * tiles + i, 0))

    return pl.pallas_call(
        kern,
        grid=(n_rows // tm,),
        in_specs=[pl.BlockSpec((tm, D_MODEL), lambda i: (t0 + i, 0))]
        + [y_spec(kk) for kk in range(TOP_K)]
        + [pl.BlockSpec((tm, LANES), lambda i: (t0 + i, 0)),
           pl.BlockSpec((1, N_MOD, D_MODEL), mod_map),
           pl.BlockSpec((1, D_MODEL), lambda i: (0, 0)),
           pl.BlockSpec((1, D_MODEL), lambda i: (0, 0))],
        out_specs=pl.BlockSpec((tm, D_MODEL), lambda i: (i, 0)),
        out_shape=jax.ShapeDtypeStruct((n_rows, D_MODEL), F32),
        compiler_params=pltpu.CompilerParams(
            dimension_semantics=("arbitrary",), vmem_limit_bytes=VMEM_LIMIT),
        name="combine_ln2",
    )(x1, yg, yg, yg, yg, rw, mod, l2g, l2b)


def _route_kernel(ridx_ref, dest_ref, cnt_ref, run_ref, bst_ref):
    phase = pl.program_id(0)
    i = pl.program_id(1)
    tm = ridx_ref.shape[0]
    ridx = ridx_ref[...]
    lane = lax.broadcasted_iota(jnp.int32, (tm, LANES), 1)
    hits = [ridx[:, kk:kk + 1] == lane for kk in range(TOP_K)]
    chosen = jnp.where(hits[0], 1.0, 0.0)
    for kk in range(1, TOP_K):
        chosen = chosen + jnp.where(hits[kk], 1.0, 0.0)
    colsum = jnp.sum(chosen, axis=0, keepdims=True)

    @pl.when((phase == 0) & (i == 0))
    def _():
        run_ref[...] = jnp.zeros_like(run_ref)

    @pl.when(phase == 0)
    def _():
        run_ref[...] = run_ref[...] + colsum

    @pl.when((phase == 1) & (i == 0))
    def _():
        counts = run_ref[...]
        cnt_ref[...] = counts
        blocks = jnp.floor((counts + (MOE_ROWS - 1.0)) * (1.0 / MOE_ROWS))
        r = lax.broadcasted_iota(jnp.int32, (LANES, LANES), 0)
        c = lax.broadcasted_iota(jnp.int32, (LANES, LANES), 1)
        before = jnp.dot(blocks, (r < c).astype(F32), precision=HIGHEST, preferred_element_type=F32)
        bst_ref[...] = before * float(MOE_ROWS)
        run_ref[...] = jnp.zeros_like(run_ref)

    @pl.when(phase == 1)
    def _():
        rt = lax.broadcasted_iota(jnp.int32, (tm, tm), 0)
        ct = lax.broadcasted_iota(jnp.int32, (tm, tm), 1)
        earlier = _dot((ct < rt).astype(BF16), chosen.astype(BF16))
        row_of = bst_ref[0:1, :] + run_ref[0:1, :] + earlier
        out = jnp.zeros((tm, LANES), F32)
        for kk in range(TOP_K):
            dk = jnp.sum(jnp.where(hits[kk], row_of, 0.0), axis=-1, keepdims=True)
            out = jnp.where(lane == kk, dk, out)
        dest_ref[...] = out.T[0:SUBLANES, :].astype(jnp.int32)
        run_ref[...] = run_ref[...] + colsum


def _route(ridx):
    t_all = ridx.shape[0]
    tm = TM_ROUTE
    return pl.pallas_call(
        _route_kernel,
        grid=(2, t_all // tm),
        in_specs=[pl.BlockSpec((tm, LANES), lambda p, i: (i, 0))],
        out_specs=[pl.BlockSpec((SUBLANES, tm), lambda p, i: (0, i * p)),
                   pl.BlockSpec((SUBLANES, LANES), lambda p, i: (0, 0))],
        out_shape=[jax.ShapeDtypeStruct((SUBLANES, t_all), jnp.int32),
                   jax.ShapeDtypeStruct((SUBLANES, LANES), F32)],
        scratch_shapes=[pltpu.VMEM((SUBLANES, LANES), F32),
                        pltpu.VMEM((SUBLANES, LANES), F32)],
        compiler_params=pltpu.CompilerParams(
            dimension_semantics=("arbitrary", "arbitrary"), vmem_limit_bytes=VMEM_LIMIT),
        name="moe_route",
    )(ridx)


def _routing_tables(counts, n_blocks):
    experts = jnp.arange(N_EXPERTS, dtype=jnp.int32)
    blocks_per = (counts + MOE_ROWS - 1) // MOE_ROWS
    bends = jnp.cumsum(blocks_per)
    bstarts = bends - blocks_per
    blocks = jnp.arange(n_blocks, dtype=jnp.int32)
    block_expert = jnp.minimum(
        jnp.sum((bends[None, :] <= blocks[:, None]).astype(jnp.int32), axis=1), N_EXPERTS - 1)
    n_used = bends[-1:].astype(jnp.int32)
    owner = block_expert[:, None] == experts[None, :]

    def per_block(table):
        return jnp.sum(jnp.where(owner, table[None, :], 0), axis=1)

    n_valid = jnp.clip(per_block(counts) - (blocks - per_block(bstarts)) * MOE_ROWS,
                       0, MOE_ROWS).astype(jnp.int32)
    present = blocks_per > 0
    ordinal = jnp.cumsum(present.astype(jnp.int32)) - 1
    later = lax.cummin(jnp.where(present, experts, N_EXPERTS), reverse=True)
    succ = jnp.concatenate([later[1:], jnp.full((1,), N_EXPERTS, jnp.int32)])
    succ = jnp.where(succ >= N_EXPERTS, -1, succ)
    stage_slot = (per_block(ordinal) % 2).astype(jnp.int32)
    next_expert = per_block(succ).astype(jnp.int32)
    return (block_expert.astype(jnp.int32), n_used, n_valid, stage_slot, next_expert)


def _pos_embed_2d(n_tokens):
    rows = n_tokens // GRID_W
    r = jnp.repeat(jnp.arange(rows), GRID_W).astype(F32)
    col = jnp.tile(jnp.arange(GRID_W), rows).astype(F32)
    quarter = D_MODEL // 4
    omega = 1.0 / (10000.0 ** (jnp.arange(quarter, dtype=F32) / quarter))
    er = r[:, None] * omega
    ec = col[:, None] * omega
    return jnp.concatenate([jnp.sin(er), jnp.cos(er), jnp.sin(ec), jnp.cos(ec)], axis=-1)


def _reorder_w_in(w):
    o_r = 2 * GLA_DK + 2 * GLA_DV
    o_f = o_r + DECAY_RANK
    o_gate = o_f + FNET_DIM
    pad = jnp.zeros((w.shape[0], LANES - DECAY_RANK), w.dtype)
    w_q = w[:, :GLA_DK] * (DK_HEAD ** -0.5)
    return jnp.concatenate([w_q, w[:, GLA_DK:o_r], w[:, o_gate:], w[:, o_f:o_gate], w[:, o_r:o_f], pad],
                           axis=1)


def kernel(x_prompt, x_sample, state_gla, c, c_ctx, w_ada, b_ada, w_in, w_dec_fwd, b_dec_fwd,
           w_dec_bwd, b_dec_bwd, gla_norm_g, w_br_gla, w_br_fnet, w_out, ln1_g, ln1_b, w_router,
           b_router, w_gate_up, b_gate_up, w_down, b_down, ln2_g, ln2_b):
    n_req, ctx_len, _ = x_prompt.shape
    n_lat, lat_len, _ = x_sample.shape
    depth = w_in.shape[0]
    alpha = (2.0 * depth) ** 0.25
    t_ctx = n_req * ctx_len
    t_lat = n_lat * lat_len
    t_all = t_ctx + t_lat

    x_ctx = x_prompt.reshape(t_ctx, D_MODEL)
    x_lat = x_sample.reshape(t_lat, D_MODEL)
    pos = _pos_embed_2d(lat_len)
    zero_pos = jnp.zeros_like(pos)

    cond_rows = -(-(n_lat + 1) // SUBLANES) * SUBLANES
    cond = jnp.zeros((cond_rows, D_MODEL), F32).at[:n_lat].set(c).at[cond_rows - 1].set(c_ctx)

    n_moe_blocks = (t_all * TOP_K) // MOE_ROWS + N_EXPERTS
    tok_chunks = t_all // (SC_WORKERS * SC_WINDOW)
    states = []
    for l in range(depth):
        mod = _ada(cond, w_ada[l], b_ada[l]).reshape(cond_rows, N_MOD, D_MODEL)
        layer_pos = pos if l == 0 else zero_pos
        w_dec = jnp.zeros((2, LANES, GLA_DK), F32)
        w_dec = w_dec.at[0, :DECAY_RANK].set(w_dec_fwd[l]).at[1, :DECAY_RANK].set(w_dec_bwd[l])
        b_dec = jnp.stack([b_dec_fwd[l], b_dec_bwd[l]]).reshape(2, 1, GLA_DK)
        proj, la = _inproj(x_ctx, x_lat, layer_pos, mod, _reorder_w_in(w_in[l]).astype(BF16),
                           w_dec, b_dec, lat_len)

        norm_g = gla_norm_g[l].reshape(1, DV_HEAD)
        o_ctx, s_new = _gla(proj, la, norm_g, None, n_seq=n_req, seq_len=ctx_len, row0=0,
                            emit_state=True)
        (o_lat,) = _gla(proj, la, norm_g, state_gla[:, l], n_seq=n_lat, seq_len=lat_len, row0=t_ctx,
                        emit_state=False)
        states.append(s_new)

        mixed_ctx = _fnet(proj, n_seq=n_req, seq_len=ctx_len, row0=0)
        mixed_lat = _fnet(proj, n_seq=n_lat, seq_len=lat_len, row0=t_ctx)

        wr = jnp.zeros((D_MODEL, LANES), F32).at[:, :N_EXPERTS].set(w_router[l])
        br = jnp.full((1, LANES), -1e30, F32).at[0, :N_EXPERTS].set(b_router[l])
        wr_hi = wr.astype(BF16)
        wr_lo = (wr - wr_hi.astype(F32)).astype(BF16)
        x1, h2, ridx, rw = _merge(
            x_ctx, x_lat, layer_pos, mod, o_ctx, o_lat, mixed_ctx, mixed_lat, proj,
            w_br_gla[l].astype(BF16), w_br_fnet[l].astype(BF16), (0.5 * w_out[l]).astype(BF16),
            ln1_g[l].reshape(1, D_MODEL), ln1_b[l].reshape(1, D_MODEL), wr_hi, wr_lo, br, lat_len,
            alpha)

        dest, counts = _route(ridx)
        moe_tables = _routing_tables(counts[0, :N_EXPERTS].astype(jnp.int32), n_moe_blocks)
        dest = dest[:TOP_K]
        scatter_idx = dest.reshape(TOP_K, SC_WORKERS, tok_chunks, SC_WINDOW)
        p_rows = n_moe_blocks * MOE_ROWS
        xs = _sc_scatter_rows(h2.reshape(t_all, ROW_TILES, LANES), scatter_idx, p_rows)
        yb = _moe(moe_tables, xs.reshape(p_rows * ROW_TILES, LANES),
                  w_gate_up[l], b_gate_up[l], w_down[l], b_down[l])
        yb = yb.reshape(p_rows, ROW_TILES, LANES)

        def gathered(row0, n_rows):
            idx = dest[:, row0:row0 + n_rows].reshape(SC_WORKERS, -1, SC_WINDOW)
            return _sc_gather_rows(yb, idx).reshape(TOP_K * n_rows * ROW_TILES, LANES)

        l2g = ln2_g[l].reshape(1, D_MODEL)
        l2b = ln2_b[l].reshape(1, D_MODEL)
        tiles_per_seq = lat_len // TM_MIX
        yg_ctx = gathered(0, t_ctx)
        yg_lat = gathered(t_ctx, t_lat)
        x_ctx = _combine(x1, yg_ctx, rw, mod, l2g, l2b, row0=0, n_rows=t_ctx,
                         mod_map=lambda i: (cond_rows - 1, 0, 0), alpha=alpha)
        x_lat = _combine(x1, yg_lat, rw, mod, l2g, l2b, row0=t_ctx, n_rows=t_lat,
                         mod_map=lambda i: (i // tiles_per_seq, 0, 0), alpha=alpha)

    y_prompt = x_ctx.reshape(x_prompt.shape)
    y_sample = x_lat.reshape(x_sample.shape)
    new_state = jnp.stack(states, axis=1).astype(x_prompt.dtype)
    return (y_prompt, y_sample, new_state)
```

```python
import functools
import math

import numpy as np
import jax
import jax.numpy as jnp
from jax import lax
from jax.experimental import pallas as pl
from jax.experimental.pallas import tpu as pltpu
from jax.experimental.pallas import tpu_sc as plsc

F32 = jnp.float32
BF16 = jnp.bfloat16

D_MODEL = 1024
GRID_W = 64
GLA_HEADS = 4
DK_HEAD = 128
DV_HEAD = 256
GLA_DK = GLA_HEADS * DK_HEAD
GLA_DV = GLA_HEADS * DV_HEAD
DECAY_RANK = 16
GATE_NORMALIZER = 16.0
FNET_GROUPS = 4
FNET_GROUP_DIM = 128
FNET_DIM = FNET_GROUPS * FNET_GROUP_DIM
N_EXPERTS = 32
TOP_K = 4
D_EXPERT = 1024
SWIGLU_LIMIT = 7.0
SWIGLU_ALPHA = 1.702
LN_EPS = 1e-6
N_MOD = 6

LANES = 128
SUBLANES = 8
HALF_MODEL = D_MODEL // 2
ROW_TILES = HALF_MODEL // LANES
ROW_DTYPE = jnp.uint32
COL_Q = 0
COL_K = GLA_DK
COL_V = 2 * GLA_DK
COL_G = COL_V + GLA_DV
COL_GATE_A = COL_G + GLA_DV
COL_GATE_B = COL_GATE_A + D_MODEL
COL_F = COL_GATE_B + D_MODEL
COL_R = COL_F + FNET_DIM
PROJ_COLS = COL_R + LANES

GLA_CHUNK = 128
GLA_LEAF = 16
TM_PROJ = 512
PROJ_GROUP = 256
TM_MIX = 256
TM_MERGE = 512
TM_MERGE_GROUP = 128
TM_ROUTE = 1024
MOE_ROWS = 512
MOE_GROUP = 256
VMEM_LIMIT = 56 * 1024 * 1024

SC_CORES = 2
SC_SUBCORES = 16
SC_WORKERS = SC_CORES * SC_SUBCORES
SC_WINDOW = 64

HIGHEST = lax.Precision.HIGHEST


def _layer_norm(x):
    mu = jnp.mean(x, axis=-1, keepdims=True)
    xc = x - mu
    var = jnp.mean(xc * xc, axis=-1, keepdims=True)
    return xc * lax.rsqrt(var + LN_EPS)


def _sigmoid(x):
    return 0.5 * jnp.tanh(0.5 * x) + 0.5


def _log_sigmoid(z):
    return jnp.minimum(z, 0.0) - jnp.log(1.0 + jnp.exp(-jnp.abs(z)))


def _dot(a, b):
    return jnp.dot(a, b, preferred_element_type=F32)


def _split_bf16(x, terms):
    parts = []
    for _ in range(terms):
        p = x.astype(BF16)
        parts.append(p)
        x = x - p.astype(F32)
    return parts


def _dot_nt(a, b):
    return lax.dot_general(a, b, (((1,), (1,)), ((), ())), preferred_element_type=F32)


def _dot_tn(a, b):
    return lax.dot_general(a, b, (((0,), (0,)), ((), ())), preferred_element_type=F32)


def _row_tile_slice(j, n_rows, first_row=0):
    return pl.ds(first_row * ROW_TILES + j, n_rows, stride=ROW_TILES)


def _store_row_tiles(ref, val, first_row=0):
    for j in range(ROW_TILES):
        lo = val[:, j * LANES:(j + 1) * LANES]
        hi = val[:, HALF_MODEL + j * LANES:HALF_MODEL + (j + 1) * LANES]
        ref[_row_tile_slice(j, val.shape[0], first_row), :] = pltpu.pack_elementwise(
            [lo, hi], packed_dtype=BF16)


def _load_row_tile(ref, j, n_rows, first_row=0):
    words = ref[_row_tile_slice(j, n_rows, first_row), :]
    return tuple(pltpu.unpack_elementwise(words, index=half, packed_dtype=BF16, unpacked_dtype=F32)
                 for half in range(2))


def _ada_kernel(c_ref, w_ref, b_ref, o_ref):
    c = c_ref[...]
    s = c * _sigmoid(c)
    o_ref[...] = _dot(s.astype(BF16), w_ref[...].astype(BF16)) + b_ref[...]


def _ada(cond, w_ada, b_ada):
    rows = cond.shape[0]
    n = w_ada.shape[1]
    tn = 1536
    return pl.pallas_call(
        _ada_kernel,
        grid=(n // tn,),
        in_specs=[pl.BlockSpec((rows, D_MODEL), lambda j: (0, 0)),
                  pl.BlockSpec((D_MODEL, tn), lambda j: (0, j)),
                  pl.BlockSpec((1, tn), lambda j: (0, j))],
        out_specs=pl.BlockSpec((rows, tn), lambda j: (0, j)),
        out_shape=jax.ShapeDtypeStruct((rows, n), F32),
        compiler_params=pltpu.CompilerParams(vmem_limit_bytes=VMEM_LIMIT),
        name="ada_mod",
    )(cond, w_ada, b_ada.reshape(1, n))


def _group_maps(n_ctx_tiles):
    def ctx_map(i, *_):
        return (jnp.minimum(i, n_ctx_tiles - 1), 0)

    def lat_map(i, *_):
        return (jnp.maximum(i - n_ctx_tiles, 0), 0)

    return ctx_map, lat_map


def _token_specs(tm, n_ctx_tiles, tiles_per_latent_seq, ctx_mod_row):
    ctx_map, lat_map = _group_maps(n_ctx_tiles)

    def pos_map(i, *_):
        return (jnp.maximum(i - n_ctx_tiles, 0) % tiles_per_latent_seq, 0)

    def mod_map(i, *_):
        return (jnp.where(i < n_ctx_tiles, ctx_mod_row,
                          jnp.maximum(i - n_ctx_tiles, 0) // tiles_per_latent_seq), 0, 0)

    return [pl.BlockSpec((tm, D_MODEL), ctx_map),
            pl.BlockSpec((tm, D_MODEL), lat_map),
            pl.BlockSpec((tm, D_MODEL), pos_map),
            pl.BlockSpec((1, N_MOD, D_MODEL), mod_map)]


def _inproj_kernel(xc_ref, xl_ref, pos_ref, mod_ref, w_ref, wdh_ref, wdl_ref, bd_ref, o_ref, la_ref,
                   *, n_ctx_tiles):
    i = pl.program_id(0)

    def project(x_of):
        for r0 in range(0, TM_PROJ, PROJ_GROUP):
            project_rows(slice(r0, r0 + PROJ_GROUP), x_of(slice(r0, r0 + PROJ_GROUP)))

    def project_rows(rows, x):
        h = (_layer_norm(x) * (1.0 + mod_ref[0, 1:2, :]) + mod_ref[0, 0:1, :]).astype(BF16)
        r_hi, r_lo = _split_bf16(_dot(h, w_ref[:, COL_R:]), 2)
        for d in range(2):
            z = (_dot(r_hi, wdh_ref[d]) + _dot(r_lo, wdh_ref[d]) + _dot(r_hi, wdl_ref[d])
                 + bd_ref[d])
            la_ref[rows, d * GLA_DK:(d + 1) * GLA_DK] = _log_sigmoid(z) * (1.0 / GATE_NORMALIZER)
        hg = 0.5 * _dot(h, w_ref[:, COL_G:COL_GATE_A])
        o_ref[rows, COL_G:COL_GATE_A] = hg * (jnp.tanh(hg) + 1.0)
        gates = _dot(h, w_ref[:, COL_GATE_A:COL_F])
        o_ref[rows, COL_GATE_A:COL_F] = jnp.tanh(0.5 * gates) + 1.0
        o_ref[rows, :COL_G] = _dot(h, w_ref[:, :COL_G])
        o_ref[rows, COL_F:COL_R] = _dot(h, w_ref[:, COL_F:COL_R])

    @pl.when(i < n_ctx_tiles)
    def _():
        project(lambda rows: xc_ref[rows, :])

    @pl.when(i >= n_ctx_tiles)
    def _():
        project(lambda rows: xl_ref[rows, :] + pos_ref[rows, :])


def _inproj(x_ctx, x_lat, pos, mod, w_in_bf, w_dec, b_dec, lat_len):
    t_ctx, t_lat = x_ctx.shape[0], x_lat.shape[0]
    t_all = t_ctx + t_lat
    n_ctx_tiles = t_ctx // TM_PROJ
    kern = functools.partial(_inproj_kernel, n_ctx_tiles=n_ctx_tiles)
    specs = _token_specs(TM_PROJ, n_ctx_tiles, lat_len // TM_PROJ, mod.shape[0] - 1)
    w_dec_hi = w_dec.astype(BF16)
    w_dec_lo = (w_dec - w_dec_hi.astype(F32)).astype(BF16)

    def const(shape):
        return pl.BlockSpec(shape, lambda i: (0,) * len(shape))

    return pl.pallas_call(
        kern,
        grid=(t_all // TM_PROJ,),
        in_specs=specs + [pl.BlockSpec((D_MODEL, PROJ_COLS), lambda i: (0, 0),
                                       pipeline_mode=pl.Buffered(1)),
                          const((2, LANES, GLA_DK)), const((2, LANES, GLA_DK)), const((2, 1, GLA_DK))],
        out_specs=[pl.BlockSpec((TM_PROJ, COL_R), lambda i: (i, 0)),
                   pl.BlockSpec((TM_PROJ, 2 * GLA_DK), lambda i: (i, 0))],
        out_shape=[jax.ShapeDtypeStruct((t_all, COL_R), F32),
                   jax.ShapeDtypeStruct((t_all, 2 * GLA_DK), F32)],
        compiler_params=pltpu.CompilerParams(
            dimension_semantics=("arbitrary",), vmem_limit_bytes=VMEM_LIMIT),
        name="ln_inproj",
    )(x_ctx, x_lat, pos, mod, w_in_bf, w_dec_hi, w_dec_lo, b_dec)


def _gla_kernel(*refs, seq_len, has_s0, emit_state):
    it = iter(refs)
    q_ref, k_ref, v_ref, laf_ref, lab_ref, g_ref = (next(it) for _ in range(6))
    s0_ref = next(it) if has_s0 else None
    o_ref = next(it)
    sout_ref = next(it) if emit_state else None
    cum_ref, a_ref, qi_ref, ko_ref, dec_ref, op_ref, st_ref = (next(it) for _ in range(7))

    C = GLA_CHUNK
    n_chunks = seq_len // C

    def rows(n):
        if isinstance(n, int):
            return pl.ds(n * C, C)
        return pl.ds(pl.multiple_of(n * C, C), C)

    def dec_rows(n):
        if isinstance(n, int):
            return pl.ds(n * SUBLANES, SUBLANES)
        return pl.ds(pl.multiple_of(n * SUBLANES, SUBLANES), SUBLANES)

    def loop(body):
        if n_chunks <= 2:
            for n in range(n_chunks):
                body(n)
        else:
            def step(m, carry):
                body(2 * m)
                body(2 * m + 1)
                return carry
            lax.fori_loop(0, n_chunks // 2, step, 0)

    rt = lax.broadcasted_iota(jnp.int32, (C, C), 0)
    ct = lax.broadcasted_iota(jnp.int32, (C, C), 1)
    tri = ((rt >= ct).astype(BF16), (ct >= rt).astype(BF16))
    row_id = lax.broadcasted_iota(jnp.int32, (C, DK_HEAD), 0)

    def cumsum_chunk(n):
        for d, la_ref in enumerate((laf_ref, lab_ref)):
            la_hi, la_lo = _split_bf16(la_ref[rows(n), :], 2)
            cum_ref[d, rows(n), :] = _dot(tri[d], la_hi) + _dot(tri[d], la_lo)

    loop(cumsum_chunk)

    query_rows = ({}, {})
    keep = ({}, {})
    for d in range(2):
        blk = C // 2
        while blk >= GLA_LEAF:
            q_parity = 1 if d == 0 else 0
            query_rows[d][blk] = ((row_id // blk) % 2) == q_parity
            qb, kb = rt // blk, ct // blk
            keep[d][blk] = ((qb % 2) == q_parity) & ((qb == kb + 1) if d == 0 else (kb == qb + 1))
            blk //= 2
        order = (rt >= ct) if d == 0 else (ct >= rt)
        keep[d][0] = ((rt // GLA_LEAF) == (ct // GLA_LEAF)) & order

    def block_rows(cum, first, step, count):
        span = C // count
        parts = [jnp.broadcast_to(cum[first + p * step:first + p * step + 1, :], (span, DK_HEAD))
                 for p in range(count)]
        return parts[0] if count == 1 else jnp.concatenate(parts, axis=0)

    def scores(n, d):
        cum = cum_ref[d, rows(n), :]
        q = q_ref[rows(n), :]
        k = k_ref[rows(n), :]
        acc = None
        blk = C // 2
        while blk >= GLA_LEAF:
            pairs = C // (2 * blk)
            bnd = blk - 1 if d == 0 else blk
            w = jnp.exp(-jnp.abs(cum - block_rows(cum, bnd, 2 * blk, pairs)))
            x = (jnp.where(query_rows[d][blk], q, k) * w).astype(BF16)
            s = jnp.where(keep[d][blk], _dot_nt(x, x), 0.0)
            acc = s if acc is None else acc + s
            blk //= 2
        mid = GLA_LEAF // 2 - 1 if d == 0 else GLA_LEAF // 2
        e = cum - block_rows(cum, mid, GLA_LEAF, C // GLA_LEAF)
        s = _dot_nt((q * jnp.exp(e)).astype(BF16), (k * jnp.exp(-e)).astype(BF16))
        acc = acc + jnp.where(keep[d][0], s, 0.0)
        a_ref[d, rows(n), :] = acc.astype(BF16)
        end = cum[C - 1:C, :] if d == 0 else cum[0:1, :]
        qi_ref[d, rows(n), :] = (q * jnp.exp(cum)).astype(BF16)
        ko_ref[d, rows(n), :] = (k * jnp.exp(end - cum)).astype(BF16)
        dec_ref[d, dec_rows(n), :] = jnp.broadcast_to(jnp.exp(end), (SUBLANES, DK_HEAD))

    def scores_chunk(n):
        scores(n, 0)
        scores(n, 1)

    loop(scores_chunk)

    for d in range(2):
        if has_s0:
            st_ref[d] = s0_ref[0, d, 0].T
        else:
            st_ref[d] = jnp.zeros((DV_HEAD, DK_HEAD), F32)

    def scan(n, d):
        v = v_ref[rows(n), :].astype(BF16)
        st = st_ref[d]
        o = _dot(a_ref[d, rows(n), :], v) + _dot_nt(qi_ref[d, rows(n), :], st.astype(BF16))
        st_ref[d] = st * dec_ref[d, dec_rows(n), :][0:1, :] + _dot_tn(v, ko_ref[d, rows(n), :])
        op_ref[d, rows(n), :] = o

    def scan_chunk(m):
        scan(m, 0)
        scan(n_chunks - 1 - m, 1)

    loop(scan_chunk)
    if emit_state:
        sout_ref[0, 0, 0] = st_ref[0].T
        sout_ref[0, 1, 0] = st_ref[1].T

    g = g_ref[...]

    def finish_chunk(n):
        o = op_ref[0, rows(n), :] + op_ref[1, rows(n), :]
        ms = jnp.mean(o * o, axis=-1, keepdims=True)
        o_ref[rows(n), :] = o * lax.rsqrt(ms + LN_EPS) * g

    loop(finish_chunk)


def _gla(proj, la, g, s0, *, n_seq, seq_len, row0, emit_state):
    has_s0 = s0 is not None
    blk0 = row0 // seq_len
    kern = functools.partial(_gla_kernel, seq_len=seq_len, has_s0=has_s0, emit_state=emit_state)
    in_specs = [
        pl.BlockSpec((seq_len, DK_HEAD), lambda b, h: (blk0 + b, COL_Q // DK_HEAD + h)),
        pl.BlockSpec((seq_len, DK_HEAD), lambda b, h: (blk0 + b, COL_K // DK_HEAD + h)),
        pl.BlockSpec((seq_len, DV_HEAD), lambda b, h: (blk0 + b, COL_V // DV_HEAD + h)),
        pl.BlockSpec((seq_len, DK_HEAD), lambda b, h: (blk0 + b, h)),
        pl.BlockSpec((seq_len, DK_HEAD), lambda b, h: (blk0 + b, GLA_HEADS + h)),
        pl.BlockSpec((1, DV_HEAD), lambda b, h: (0, 0)),
    ]
    args = [proj, proj, proj, la, la, g]
    if has_s0:
        in_specs.append(pl.BlockSpec((1, 2, 1, DK_HEAD, DV_HEAD), lambda b, h: (b, 0, h, 0, 0)))
        args.append(s0)
    out_specs = [pl.BlockSpec((seq_len, DV_HEAD), lambda b, h: (b, h))]
    out_shape = [jax.ShapeDtypeStruct((n_seq * seq_len, GLA_DV), F32)]
    if emit_state:
        out_specs.append(pl.BlockSpec((1, 2, 1, DK_HEAD, DV_HEAD), lambda b, h: (b, 0, h, 0, 0)))
        out_shape.append(jax.ShapeDtypeStruct((n_seq, 2, GLA_HEADS, DK_HEAD, DV_HEAD), F32))

    res = pl.pallas_call(
        kern,
        grid=(n_seq, GLA_HEADS),
        in_specs=in_specs,
        out_specs=out_specs,
        out_shape=out_shape,
        scratch_shapes=[pltpu.VMEM((2, seq_len, DK_HEAD), F32),
                        pltpu.VMEM((2, seq_len, GLA_CHUNK), BF16),
                        pltpu.VMEM((2, seq_len, DK_HEAD), BF16),
                        pltpu.VMEM((2, seq_len, DK_HEAD), BF16),
                        pltpu.VMEM((2, seq_len // GLA_CHUNK * SUBLANES, DK_HEAD), F32),
                        pltpu.VMEM((2, seq_len, DV_HEAD), F32),
                        pltpu.VMEM((2, DV_HEAD, DK_HEAD), F32)],
        compiler_params=pltpu.CompilerParams(
            dimension_semantics=("arbitrary", "arbitrary"), vmem_limit_bytes=VMEM_LIMIT),
        name="gla_seq%d" % seq_len,
    )(*args)
    return res


def _fnet_kernel(f_ref, cl_ref, sl_ref, cg_ref, sg_ref, o_ref, uc_ref, us_ref, *, seq_len):
    cg = cg_ref[...]
    sg = sg_ref[...]
    for grp in range(FNET_GROUPS):
        lo = grp * FNET_GROUP_DIM
        u = f_ref[:, lo:lo + FNET_GROUP_DIM].astype(BF16)
        uc_ref[:, lo:lo + FNET_GROUP_DIM] = _dot(u, cg).astype(BF16)
        us_ref[:, lo:lo + FNET_GROUP_DIM] = _dot(u, sg).astype(BF16)
    mixed = _dot(cl_ref[...], uc_ref[...]) - _dot(sl_ref[...], us_ref[...])
    o_ref[...] = mixed * (1.0 / math.sqrt(seq_len * FNET_GROUP_DIM))


def _dft_mats(n):
    j = np.arange(n, dtype=np.int64)
    ang = (2.0 * np.pi / n) * ((j[:, None] * j[None, :]) % n).astype(np.float64)
    return (jnp.asarray(np.cos(ang), dtype=F32).astype(BF16),
            jnp.asarray(np.sin(ang), dtype=F32).astype(BF16))


def _fnet(proj, *, n_seq, seq_len, row0):
    blk0 = row0 // seq_len
    cl, sl = _dft_mats(seq_len)
    cg, sg = _dft_mats(FNET_GROUP_DIM)
    kern = functools.partial(_fnet_kernel, seq_len=seq_len)
    return pl.pallas_call(
        kern,
        grid=(n_seq,),
        in_specs=[pl.BlockSpec((seq_len, FNET_DIM), lambda b: (blk0 + b, COL_F // FNET_DIM)),
                  pl.BlockSpec((seq_len, seq_len), lambda b: (0, 0)),
                  pl.BlockSpec((seq_len, seq_len), lambda b: (0, 0)),
                  pl.BlockSpec((FNET_GROUP_DIM, FNET_GROUP_DIM), lambda b: (0, 0)),
                  pl.BlockSpec((FNET_GROUP_DIM, FNET_GROUP_DIM), lambda b: (0, 0))],
        out_specs=pl.BlockSpec((seq_len, FNET_DIM), lambda b: (b, 0)),
        out_shape=jax.ShapeDtypeStruct((n_seq * seq_len, FNET_DIM), F32),
        scratch_shapes=[pltpu.VMEM((seq_len, FNET_DIM), BF16),
                        pltpu.VMEM((seq_len, FNET_DIM), BF16)],
        compiler_params=pltpu.CompilerParams(
            dimension_semantics=("arbitrary",), vmem_limit_bytes=VMEM_LIMIT),
        name="fnet_seq%d" % seq_len,
    )(proj, cl, sl, cg, sg)


def _merge_kernel(xc_ref, xl_ref, pos_ref, mod_ref, oc_ref, ol_ref, mc_ref, ml_ref,
                  g_ref, ga_ref, gb_ref, wbg_ref, wbf_ref, wo_ref, l1g_ref, l1b_ref, wrh_ref, wrl_ref,
                  br_ref, x1_ref, h2_ref, ridx_ref, rw_ref, *, n_ctx_tiles, alpha):
    i = pl.program_id(0)

    def compute(rows, x, o, mx):
        a = (o * g_ref[rows, :]).astype(BF16)
        gla_out = _dot(a, wbg_ref[...])
        fnet_out = _dot(mx.astype(BF16), wbf_ref[...])
        merged = ga_ref[rows, :] * gla_out + gb_ref[rows, :] * fnet_out
        mix = _dot(merged.astype(BF16), wo_ref[...])
        y = alpha * x + mod_ref[0, 2:3, :] * mix
        x1 = _layer_norm(y) * l1g_ref[...] + l1b_ref[...]
        x1_ref[rows, :] = x1
        h2 = _layer_norm(x1) * (1.0 + mod_ref[0, 4:5, :]) + mod_ref[0, 3:4, :]
        _store_row_tiles(h2_ref, h2, rows.start)

        h_hi, h_lo = _split_bf16(h2, 2)
        logits = (_dot(h_hi, wrh_ref[...]) + _dot(h_lo, wrh_ref[...]) + _dot(h_hi, wrl_ref[...])
                  + br_ref[...])
        lane_i = lax.broadcasted_iota(jnp.int32, logits.shape, 1)
        lane = lane_i.astype(F32)
        idx_out = jnp.zeros(logits.shape, F32)
        val_out = jnp.zeros(logits.shape, F32)
        top0 = None
        denom = None
        for kk in range(TOP_K):
            m = jnp.max(logits, axis=-1, keepdims=True)
            sel = jnp.min(jnp.where(logits == m, lane, float(LANES)), axis=-1, keepdims=True)
            if kk == 0:
                top0 = m
                p = jnp.ones_like(m)
                denom = p
            else:
                p = jnp.exp(m - top0)
                denom = denom + p
            idx_out = jnp.where(lane_i == kk, sel, idx_out)
            val_out = jnp.where(lane_i == kk, p, val_out)
            logits = jnp.where(lane == sel, -jnp.inf, logits)
        ridx_ref[rows, :] = idx_out.astype(jnp.int32)
        rw_ref[rows, :] = val_out / denom

    tm = x1_ref.shape[0]
    groups = [slice(r0, r0 + TM_MERGE_GROUP) for r0 in range(0, tm, TM_MERGE_GROUP)]

    @pl.when(i < n_ctx_tiles)
    def _():
        for rows in groups:
            compute(rows, xc_ref[rows, :], oc_ref[rows, :], mc_ref[rows, :])

    @pl.when(i >= n_ctx_tiles)
    def _():
        for rows in groups:
            compute(rows, xl_ref[rows, :] + pos_ref[rows, :], ol_ref[rows, :], ml_ref[rows, :])


def _merge(x_ctx, x_lat, pos, mod, o_ctx, o_lat, mixed_ctx, mixed_lat, proj,
           wbg, wbf, wo, l1g, l1b, wr_hi, wr_lo, br, lat_len, alpha):
    t_ctx, t_lat = x_ctx.shape[0], x_lat.shape[0]
    t_all = t_ctx + t_lat
    tm = TM_MERGE
    n_ctx_tiles = t_ctx // tm
    kern = functools.partial(_merge_kernel, n_ctx_tiles=n_ctx_tiles, alpha=alpha)
    specs = _token_specs(tm, n_ctx_tiles, lat_len // tm, mod.shape[0] - 1)
    ctx_map, lat_map = _group_maps(n_ctx_tiles)

    def const(shape):
        return pl.BlockSpec(shape, lambda i: (0,) * len(shape))

    in_specs = specs + [
        pl.BlockSpec((tm, GLA_DV), ctx_map),
        pl.BlockSpec((tm, GLA_DV), lat_map),
        pl.BlockSpec((tm, FNET_DIM), ctx_map),
        pl.BlockSpec((tm, FNET_DIM), lat_map),
        pl.BlockSpec((tm, GLA_DV), lambda i: (i, COL_G // GLA_DV)),
        pl.BlockSpec((tm, D_MODEL), lambda i: (i, COL_GATE_A // D_MODEL)),
        pl.BlockSpec((tm, D_MODEL), lambda i: (i, COL_GATE_B // D_MODEL)),
        const((GLA_DV, D_MODEL)), const((FNET_DIM, D_MODEL)), const((D_MODEL, D_MODEL)),
        const((1, D_MODEL)), const((1, D_MODEL)),
        const((D_MODEL, LANES)), const((D_MODEL, LANES)), const((1, LANES)),
    ]
    out_specs = [pl.BlockSpec((tm, D_MODEL), lambda i: (i, 0)),
                 pl.BlockSpec((tm * ROW_TILES, LANES), lambda i: (i, 0)),
                 pl.BlockSpec((tm, LANES), lambda i: (i, 0)),
                 pl.BlockSpec((tm, LANES), lambda i: (i, 0))]
    out_shape = [jax.ShapeDtypeStruct((t_all, D_MODEL), F32),
                 jax.ShapeDtypeStruct((t_all * ROW_TILES, LANES), ROW_DTYPE),
                 jax.ShapeDtypeStruct((t_all, LANES), jnp.int32),
                 jax.ShapeDtypeStruct((t_all, LANES), F32)]
    return pl.pallas_call(
        kern,
        grid=(t_all // tm,),
        in_specs=in_specs,
        out_specs=out_specs,
        out_shape=out_shape,
        compiler_params=pltpu.CompilerParams(
            dimension_semantics=("arbitrary",), vmem_limit_bytes=VMEM_LIMIT),
        name="merge_ln1_router",
    )(x_ctx, x_lat, pos, mod, o_ctx, o_lat, mixed_ctx, mixed_lat, proj, proj, proj,
      wbg, wbf, wo, l1g, l1b, wr_hi, wr_lo, br)


def _sc_mesh():
    return plsc.VectorSubcoreMesh(core_axis_name="c", subcore_axis_name="s")


def _sc_worker_id():
    return lax.axis_index("s") * SC_CORES + lax.axis_index("c")


def _sc_scatter_rows(src, idx, n_out):
    n_src = src.shape[0]
    w = SC_WINDOW
    n_chunks = n_src // (SC_WORKERS * w)
    copies = idx.shape[0]
    assert n_chunks % 2 == 0 and idx.shape == (copies, SC_WORKERS, n_chunks, w)

    @functools.partial(
        pl.kernel, mesh=_sc_mesh(),
        out_type=jax.ShapeDtypeStruct((n_out, ROW_TILES, LANES), ROW_DTYPE),
        scratch_types=[pltpu.VMEM((copies * n_chunks, w), jnp.int32),
                       pltpu.VMEM((2, w, ROW_TILES, LANES), ROW_DTYPE),
                       pltpu.SemaphoreType.DMA((2,)),
                       pltpu.SemaphoreType.DMA((2,))],
        name="moe_dispatch_scatter")
    def k(src_hbm, idx_hbm, out_hbm, idx_v, rows_v, rsem, wsem):
        wid = _sc_worker_id()
        base = wid * (n_chunks * w)
        for kk in range(copies):
            pltpu.sync_copy(idx_hbm.at[kk, wid], idx_v.at[pl.ds(kk * n_chunks, n_chunks)])

        def read(j, slot):
            return pltpu.make_async_copy(src_hbm.at[pl.ds(base + j * w, w)], rows_v.at[slot],
                                         rsem.at[slot])

        def scatter(j, kk, slot):
            return pltpu.make_async_copy(rows_v.at[slot], out_hbm.at[idx_v.at[kk * n_chunks + j]],
                                         wsem.at[slot])

        read(0, 0).start()

        @pl.loop(0, n_chunks, step=2)
        def _(jj):
            read(jj, 0).wait()

            @pl.when(jj > 0)
            def _():
                for kk in range(copies):
                    scatter(jj - 1, kk, 1).wait()

            read(jj + 1, 1).start()
            for kk in range(copies):
                scatter(jj, kk, 0).start()
            read(jj + 1, 1).wait()
            for kk in range(copies):
                scatter(jj, kk, 0).wait()

            @pl.when(jj + 2 < n_chunks)
            def _():
                read(jj + 2, 0).start()

            for kk in range(copies):
                scatter(jj + 1, kk, 1).start()

        for kk in range(copies):
            scatter(n_chunks - 1, kk, 1).wait()

    return k(src, idx)


def _sc_gather_rows(table, idx):
    _, n_chunks, w = idx.shape
    assert n_chunks % 2 == 0 and idx.shape[0] == SC_WORKERS and w == SC_WINDOW
    n_out = SC_WORKERS * n_chunks * w

    @functools.partial(
        pl.kernel, mesh=_sc_mesh(),
        out_type=jax.ShapeDtypeStruct((n_out, ROW_TILES, LANES), ROW_DTYPE),
        scratch_types=[pltpu.VMEM((n_chunks, w), jnp.int32),
                       pltpu.VMEM((2, w, ROW_TILES, LANES), ROW_DTYPE),
                       pltpu.SemaphoreType.DMA((2,)),
                       pltpu.SemaphoreType.DMA((2,))],
        name="moe_combine_gather")
    def k(table_hbm, idx_hbm, out_hbm, idx_v, rows_v, gsem, wsem):
        wid = _sc_worker_id()
        base = wid * (n_chunks * w)
        pltpu.sync_copy(idx_hbm.at[wid], idx_v)

        def gather(j, slot):
            return pltpu.make_async_copy(table_hbm.at[idx_v.at[j]], rows_v.at[slot], gsem.at[slot])

        def write(j, slot):
            return pltpu.make_async_copy(rows_v.at[slot], out_hbm.at[pl.ds(base + j * w, w)],
                                         wsem.at[slot])

        gather(0, 0).start()

        @pl.loop(0, n_chunks, step=2)
        def _(jj):
            gather(jj, 0).wait()

            @pl.when(jj > 0)
            def _():
                write(jj - 1, 1).wait()

            gather(jj + 1, 1).start()
            write(jj, 0).start()
            gather(jj + 1, 1).wait()
            write(jj, 0).wait()

            @pl.when(jj + 2 < n_chunks)
            def _():
                gather(jj + 2, 0).start()

            write(jj + 1, 1).start()

        write(n_chunks - 1, 1).wait()

    return k(table, idx)


def _moe_kernel(be_ref, nu_ref, nv_ref, slot_ref, nxt_ref, x_ref, wgu_hbm, bgu_ref, wd_hbm, bd_ref,
                o_ref, wgu_st, wd_st, wgu_bf, wd_bf, xb_ref, sem):
    b = pl.program_id(0)
    e = be_ref[b]
    prev = be_ref[jnp.maximum(b - 1, 0)]
    active = b < nu_ref[0]
    changed = (b == 0) | (e != prev)

    def weight_copies(expert, s):
        return (pltpu.make_async_copy(wgu_hbm.at[expert], wgu_st.at[s], sem.at[0, s]),
                pltpu.make_async_copy(wd_hbm.at[expert], wd_st.at[s], sem.at[1, s]))

    @pl.when(active & changed)
    def _():
        s = slot_ref[b]

        @pl.when(b == 0)
        def _():
            for cp in weight_copies(e, s):
                cp.start()

        for cp in weight_copies(e, s):
            cp.wait()
        wgu_bf[...] = wgu_st[s].astype(BF16)
        wd_bf[...] = wd_st[s].astype(BF16)
        nxt = nxt_ref[b]

        @pl.when(nxt >= 0)
        def _():
            for cp in weight_copies(nxt, 1 - s):
                cp.start()

    n_valid = nv_ref[b]

    def expert_mlp(groups):
        for r0 in groups:
            valid = lax.broadcasted_iota(jnp.int32, (MOE_GROUP, LANES), 0) < n_valid - r0
            for j in range(ROW_TILES):
                for half, xj in enumerate(_load_row_tile(x_ref, j, MOE_GROUP, r0)):
                    c0 = half * HALF_MODEL + j * LANES
                    xb_ref[r0:r0 + MOE_GROUP, c0:c0 + LANES] = jnp.where(valid, xj, 0.0).astype(BF16)
            gu = _dot(xb_ref[r0:r0 + MOE_GROUP, :], wgu_bf[...]) + bgu_ref[0]
            gate = jnp.minimum(gu[:, :D_EXPERT], SWIGLU_LIMIT)
            up = jnp.clip(gu[:, D_EXPERT:], -SWIGLU_LIMIT, SWIGLU_LIMIT)
            glu = gate * _sigmoid(gate * SWIGLU_ALPHA)
            act = ((up + 1.0) * glu).astype(BF16)
            _store_row_tiles(o_ref, _dot(act, wd_bf[...]) + bd_ref[0], r0)

    all_groups = list(range(0, MOE_ROWS, MOE_GROUP))

    @pl.when(active & (n_valid > MOE_ROWS - MOE_GROUP))
    def _():
        expert_mlp(all_groups)

    @pl.when(active & (n_valid <= MOE_ROWS - MOE_GROUP))
    def _():
        expert_mlp(all_groups[:-1])


def _moe(tables, xs, w_gate_up, b_gate_up, w_down, b_down):
    p_rows = xs.shape[0] // ROW_TILES
    n_blocks = p_rows // MOE_ROWS

    def blk(b, be, nu, *_):
        return jnp.minimum(b, nu[0] - 1)

    def expert(b, be, nu, *_):
        return (be[blk(b, be, nu)], 0, 0)

    def rows(b, be, nu, *_):
        return (blk(b, be, nu), 0)

    grid_spec = pltpu.PrefetchScalarGridSpec(
        num_scalar_prefetch=len(tables),
        grid=(n_blocks,),
        in_specs=[
            pl.BlockSpec((MOE_ROWS * ROW_TILES, LANES), rows),
            pl.BlockSpec(memory_space=pl.ANY),
            pl.BlockSpec((1, 1, 2 * D_EXPERT), expert),
            pl.BlockSpec(memory_space=pl.ANY),
            pl.BlockSpec((1, 1, D_MODEL), expert),
        ],
        out_specs=pl.BlockSpec((MOE_ROWS * ROW_TILES, LANES), rows),
        scratch_shapes=[pltpu.VMEM((2, D_MODEL, 2 * D_EXPERT), F32),
                        pltpu.VMEM((2, D_EXPERT, D_MODEL), F32),
                        pltpu.VMEM((D_MODEL, 2 * D_EXPERT), BF16),
                        pltpu.VMEM((D_EXPERT, D_MODEL), BF16),
                        pltpu.VMEM((MOE_ROWS, D_MODEL), BF16),
                        pltpu.SemaphoreType.DMA((2, 2))],
    )
    return pl.pallas_call(
        _moe_kernel,
        grid_spec=grid_spec,
        out_shape=jax.ShapeDtypeStruct((p_rows * ROW_TILES, LANES), ROW_DTYPE),
        compiler_params=pltpu.CompilerParams(
            dimension_semantics=("arbitrary",), vmem_limit_bytes=VMEM_LIMIT),
        name="moe_grouped_mlp",
    )(*tables, xs, w_gate_up, b_gate_up.reshape(N_EXPERTS, 1, 2 * D_EXPERT), w_down,
      b_down.reshape(N_EXPERTS, 1, D_MODEL))


def _combine_kernel(x1_ref, y0_ref, y1_ref, y2_ref, y3_ref, rw_ref, mod_ref, g_ref, b_ref, o_ref,
                    *, alpha):
    rw = rw_ref[...]
    y_refs = (y0_ref, y1_ref, y2_ref, y3_ref)
    pieces = [None] * (2 * ROW_TILES)
    for j in range(ROW_TILES):
        for kk in range(TOP_K):
            for half, yj in enumerate(_load_row_tile(y_refs[kk], j, rw.shape[0])):
                term = rw[:, kk:kk + 1] * yj
                slot = half * ROW_TILES + j
                pieces[slot] = term if kk == 0 else pieces[slot] + term
    ff = jnp.concatenate(pieces, axis=-1)
    y = alpha * x1_ref[...] + mod_ref[0, 5:6, :] * ff
    o_ref[...] = _layer_norm(y) * g_ref[...] + b_ref[...]


def _combine(x1, yg, rw, mod, l2g, l2b, *, row0, n_rows, mod_map, alpha):
    tm = TM_MIX
    t0 = row0 // tm
    tiles = n_rows // tm
    kern = functools.partial(_combine_kernel, alpha=alpha)

    def y_spec(kk):
        return pl.BlockSpec((tm * ROW_TILES, LANES), lambda i: (kk * tiles + i, 0))

    return pl.pallas_call(
        kern,
        grid=(n_rows // tm,),
        in_specs=[pl.BlockSpec((tm, D_MODEL), lambda i: (t0 + i, 0))]
        + [y_spec(kk) for kk in range(TOP_K)]
        + [pl.BlockSpec((tm, LANES), lambda i: (t0 + i, 0)),
           pl.BlockSpec((1, N_MOD, D_MODEL), mod_map),
           pl.BlockSpec((1, D_MODEL), lambda i: (0, 0)),
           pl.BlockSpec((1, D_MODEL), lambda i: (0, 0))],
        out_specs=pl.BlockSpec((tm, D_MODEL), lambda i: (i, 0)),
        out_shape=jax.ShapeDtypeStruct((n_rows, D_MODEL), F32),
        compiler_params=pltpu.CompilerParams(
            dimension_semantics=("arbitrary",), vmem_limit_bytes=VMEM_LIMIT),
        name="combine_ln2",
    )(x1, yg, yg, yg, yg, rw, mod, l2g, l2b)


def _route_kernel(ridx_ref, dest_ref, cnt_ref, run_ref, bst_ref):
    phase = pl.program_id(0)
    i = pl.program_id(1)
    tm = ridx_ref.shape[0]
    ridx = ridx_ref[...]
    lane = lax.broadcasted_iota(jnp.int32, (tm, LANES), 1)
    hits = [ridx[:, kk:kk + 1] == lane for kk in range(TOP_K)]
    chosen = jnp.where(hits[0], 1.0, 0.0)
    for kk in range(1, TOP_K):
        chosen = chosen + jnp.where(hits[kk], 1.0, 0.0)
    colsum = jnp.sum(chosen, axis=0, keepdims=True)

    @pl.when((phase == 0) & (i == 0))
    def _():
        run_ref[...] = jnp.zeros_like(run_ref)

    @pl.when(phase == 0)
    def _():
        run_ref[...] = run_ref[...] + colsum

    @pl.when((phase == 1) & (i == 0))
    def _():
        counts = run_ref[...]
        cnt_ref[...] = counts
        blocks = jnp.floor((counts + (MOE_ROWS - 1.0)) * (1.0 / MOE_ROWS))
        r = lax.broadcasted_iota(jnp.int32, (LANES, LANES), 0)
        c = lax.broadcasted_iota(jnp.int32, (LANES, LANES), 1)
        before = jnp.dot(blocks, (r < c).astype(F32), precision=HIGHEST, preferred_element_type=F32)
        bst_ref[...] = before * float(MOE_ROWS)
        run_ref[...] = jnp.zeros_like(run_ref)

    @pl.when(phase == 1)
    def _():
        rt = lax.broadcasted_iota(jnp.int32, (tm, tm), 0)
        ct = lax.broadcasted_iota(jnp.int32, (tm, tm), 1)
        earlier = _dot((ct < rt).astype(BF16), chosen.astype(BF16))
        row_of = bst_ref[0:1, :] + run_ref[0:1, :] + earlier
        out = jnp.zeros((tm, LANES), F32)
        for kk in range(TOP_K):
            dk = jnp.sum(jnp.where(hits[kk], row_of, 0.0), axis=-1, keepdims=True)
            out = jnp.where(lane == kk, dk, out)
        dest_ref[...] = out.T[0:SUBLANES, :].astype(jnp.int32)
        run_ref[...] = run_ref[...] + colsum


def _route(ridx):
    t_all = ridx.shape[0]
    tm = TM_ROUTE
    return pl.pallas_call(
        _route_kernel,
        grid=(2, t_all // tm),
        in_specs=[pl.BlockSpec((tm, LANES), lambda p, i: (i, 0))],
        out_specs=[pl.BlockSpec((SUBLANES, tm), lambda p, i: (0, i * p)),
                   pl.BlockSpec((SUBLANES, LANES), lambda p, i: (0, 0))],
        out_shape=[jax.ShapeDtypeStruct((SUBLANES, t_all), jnp.int32),
                   jax.ShapeDtypeStruct((SUBLANES, LANES), F32)],
        scratch_shapes=[pltpu.VMEM((SUBLANES, LANES), F32),
                        pltpu.VMEM((SUBLANES, LANES), F32)],
        compiler_params=pltpu.CompilerParams(
            dimension_semantics=("arbitrary", "arbitrary"), vmem_limit_bytes=VMEM_LIMIT),
        name="moe_route",
    )(ridx)


def _routing_tables(counts, n_blocks):
    experts = jnp.arange(N_EXPERTS, dtype=jnp.int32)
    blocks_per = (counts + MOE_ROWS - 1) // MOE_ROWS
    bends = jnp.cumsum(blocks_per)
    bstarts = bends - blocks_per
    blocks = jnp.arange(n_blocks, dtype=jnp.int32)
    block_expert = jnp.minimum(
        jnp.sum((bends[None, :] <= blocks[:, None]).astype(jnp.int32), axis=1), N_EXPERTS - 1)
    n_used = bends[-1:].astype(jnp.int32)
    owner = block_expert[:, None] == experts[None, :]

    def per_block(table):
        return jnp.sum(jnp.where(owner, table[None, :], 0), axis=1)

    n_valid = jnp.clip(per_block(counts) - (blocks - per_block(bstarts)) * MOE_ROWS,
                       0, MOE_ROWS).astype(jnp.int32)
    present = blocks_per > 0
    ordinal = jnp.cumsum(present.astype(jnp.int32)) - 1
    later = lax.cummin(jnp.where(present, experts, N_EXPERTS), reverse=True)
    succ = jnp.concatenate([later[1:], jnp.full((1,), N_EXPERTS, jnp.int32)])
    succ = jnp.where(succ >= N_EXPERTS, -1, succ)
    stage_slot = (per_block(ordinal) % 2).astype(jnp.int32)
    next_expert = per_block(succ).astype(jnp.int32)
    return (block_expert.astype(jnp.int32), n_used, n_valid, stage_slot, next_expert)


def _pos_embed_2d(n_tokens):
    rows = n_tokens // GRID_W
    r = jnp.repeat(jnp.arange(rows), GRID_W).astype(F32)
    col = jnp.tile(jnp.arange(GRID_W), rows).astype(F32)
    quarter = D_MODEL // 4
    omega = 1.0 / (10000.0 ** (jnp.arange(quarter, dtype=F32) / quarter))
    er = r[:, None] * omega
    ec = col[:, None] * omega
    return jnp.concatenate([jnp.sin(er), jnp.cos(er), jnp.sin(ec), jnp.cos(ec)], axis=-1)


def _reorder_w_in(w):
    o_r = 2 * GLA_DK + 2 * GLA_DV
    o_f = o_r + DECAY_RANK
    o_gate = o_f + FNET_DIM
    pad = jnp.zeros((w.shape[0], LANES - DECAY_RANK), w.dtype)
    w_q = w[:, :GLA_DK] * (DK_HEAD ** -0.5)
    return jnp.concatenate([w_q, w[:, GLA_DK:o_r], w[:, o_gate:], w[:, o_f:o_gate], w[:, o_r:o_f], pad],
                           axis=1)


def kernel(x_prompt, x_sample, state_gla, c, c_ctx, w_ada, b_ada, w_in, w_dec_fwd, b_dec_fwd,
           w_dec_bwd, b_dec_bwd, gla_norm_g, w_br_gla, w_br_fnet, w_out, ln1_g, ln1_b, w_router,
           b_router, w_gate_up, b_gate_up, w_down, b_down, ln2_g, ln2_b):
    n_req, ctx_len, _ = x_prompt.shape
    n_lat, lat_len, _ = x_sample.shape
    depth = w_in.shape[0]
    alpha = (2.0 * depth) ** 0.25
    t_ctx = n_req * ctx_len
    t_lat = n_lat * lat_len
    t_all = t_ctx + t_lat

    x_ctx = x_prompt.reshape(t_ctx, D_MODEL)
    x_lat = x_sample.reshape(t_lat, D_MODEL)
    pos = _pos_embed_2d(lat_len)
    zero_pos = jnp.zeros_like(pos)

    cond_rows = -(-(n_lat + 1) // SUBLANES) * SUBLANES
    cond = jnp.zeros((cond_rows, D_MODEL), F32).at[:n_lat].set(c).at[cond_rows - 1].set(c_ctx)

    n_moe_blocks = (t_all * TOP_K) // MOE_ROWS + N_EXPERTS
    tok_chunks = t_all // (SC_WORKERS * SC_WINDOW)
    states = []
    for l in range(depth):
        mod = _ada(cond, w_ada[l], b_ada[l]).reshape(cond_rows, N_MOD, D_MODEL)
        layer_pos = pos if l == 0 else zero_pos
        w_dec = jnp.zeros((2, LANES, GLA_DK), F32)
        w_dec = w_dec.at[0, :DECAY_RANK].set(w_dec_fwd[l]).at[1, :DECAY_RANK].set(w_dec_bwd[l])
        b_dec = jnp.stack([b_dec_fwd[l], b_dec_bwd[l]]).reshape(2, 1, GLA_DK)
        proj, la = _inproj(x_ctx, x_lat, layer_pos, mod, _reorder_w_in(w_in[l]).astype(BF16),
                           w_dec, b_dec, lat_len)

        norm_g = gla_norm_g[l].reshape(1, DV_HEAD)
        o_ctx, s_new = _gla(proj, la, norm_g, None, n_seq=n_req, seq_len=ctx_len, row0=0,
                            emit_state=True)
        (o_lat,) = _gla(proj, la, norm_g, state_gla[:, l], n_seq=n_lat, seq_len=lat_len, row0=t_ctx,
                        emit_state=False)
        states.append(s_new)

        mixed_ctx = _fnet(proj, n_seq=n_req, seq_len=ctx_len, row0=0)
        mixed_lat = _fnet(proj, n_seq=n_lat, seq_len=lat_len, row0=t_ctx)

        wr = jnp.zeros((D_MODEL, LANES), F32).at[:, :N_EXPERTS].set(w_router[l])
        br = jnp.full((1, LANES), -1e30, F32).at[0, :N_EXPERTS].set(b_router[l])
        wr_hi = wr.astype(BF16)
        wr_lo = (wr - wr_hi.astype(F32)).astype(BF16)
        x1, h2, ridx, rw = _merge(
            x_ctx, x_lat, layer_pos, mod, o_ctx, o_lat, mixed_ctx, mixed_lat, proj,
            w_br_gla[l].astype(BF16), w_br_fnet[l].astype(BF16), (0.5 * w_out[l]).astype(BF16),
            ln1_g[l].reshape(1, D_MODEL), ln1_b[l].reshape(1, D_MODEL), wr_hi, wr_lo, br, lat_len,
            alpha)

        dest, counts = _route(ridx)
        moe_tables = _routing_tables(counts[0, :N_EXPERTS].astype(jnp.int32), n_moe_blocks)
        dest = dest[:TOP_K]
        scatter_idx = dest.reshape(TOP_K, SC_WORKERS, tok_chunks, SC_WINDOW)
        p_rows = n_moe_blocks * MOE_ROWS
        xs = _sc_scatter_rows(h2.reshape(t_all, ROW_TILES, LANES), scatter_idx, p_rows)
        yb = _moe(moe_tables, xs.reshape(p_rows * ROW_TILES, LANES),
                  w_gate_up[l], b_gate_up[l], w_down[l], b_down[l])
        yb = yb.reshape(p_rows, ROW_TILES, LANES)

        def gathered(row0, n_rows):
            idx = dest[:, row0:row0 + n_rows].reshape(SC_WORKERS, -1, SC_WINDOW)
            return _sc_gather_rows(yb, idx).reshape(TOP_K * n_rows * ROW_TILES, LANES)

        l2g = ln2_g[l].reshape(1, D_MODEL)
        l2b = ln2_b[l].reshape(1, D_MODEL)
        tiles_per_seq = lat_len // TM_MIX
        yg_ctx = gathered(0, t_ctx)
        yg_lat = gathered(t_ctx, t_lat)
        x_ctx = _combine(x1, yg_ctx, rw, mod, l2g, l2b, row0=0, n_rows=t_ctx,
                         mod_map=lambda i: (cond_rows - 1, 0, 0), alpha=alpha)
        x_lat = _combine(x1, yg_lat, rw, mod, l2g, l2b, row0=t_ctx, n_rows=t_lat,
                         mod_map=lambda i: (i // tiles_per_seq, 0, 0), alpha=alpha)

    y_prompt = x_ctx.reshape(x_prompt.shape)
    y_sample = x_lat.reshape(x_sample.shape)
    new_state = jnp.stack(states, axis=1).astype(x_prompt.dtype)
    return (y_prompt, y_sample, new_state)
```

```python
import functools
import math

import numpy as np
import jax
import jax.numpy as jnp
from jax import lax
from jax.experimental import pallas as pl
from jax.experimental.pallas import tpu as pltpu
from jax.experimental.pallas import tpu_sc as plsc

F32 = jnp.float32
BF16 = jnp.bfloat16

D_MODEL = 1024
GRID_W = 64
GLA_HEADS = 4
DK_HEAD = 128
DV_HEAD = 256
GLA_DK = GLA_HEADS * DK_HEAD
GLA_DV = GLA_HEADS * DV_HEAD
DECAY_RANK = 16
GATE_NORMALIZER = 16.0
FNET_GROUPS = 4
FNET_GROUP_DIM = 128
FNET_DIM = FNET_GROUPS * FNET_GROUP_DIM
N_EXPERTS = 32
TOP_K = 4
D_EXPERT = 1024
SWIGLU_LIMIT = 7.0
SWIGLU_ALPHA = 1.702
LN_EPS = 1e-6
N_MOD = 6

LANES = 128
SUBLANES = 8
HALF_MODEL = D_MODEL // 2
ROW_TILES = HALF_MODEL // LANES
ROW_DTYPE = jnp.uint32
COL_Q = 0
COL_K = GLA_DK
COL_V = 2 * GLA_DK
COL_G = COL_V + GLA_DV
COL_GATE_A = COL_G + GLA_DV
COL_GATE_B = COL_GATE_A + D_MODEL
COL_F = COL_GATE_B + D_MODEL
COL_R = COL_F + FNET_DIM
PROJ_COLS = COL_R + LANES

GLA_CHUNK = 128
GLA_LEAF = 16
TM_PROJ = 512
PROJ_GROUP = 256
TM_MIX = 256
TM_MERGE = 512
TM_MERGE_GROUP = 128
TM_ROUTE = 1024
MOE_ROWS = 512
MOE_GROUP = 256
VMEM_LIMIT = 56 * 1024 * 1024

SC_CORES = 2
SC_SUBCORES = 16
SC_WORKERS = SC_CORES * SC_SUBCORES
SC_WINDOW = 64

HIGHEST = lax.Precision.HIGHEST


def _layer_norm(x):
    mu = jnp.mean(x, axis=-1, keepdims=True)
    xc = x - mu
    var = jnp.mean(xc * xc, axis=-1, keepdims=True)
    return xc * lax.rsqrt(var + LN_EPS)


def _sigmoid(x):
    return 0.5 * jnp.tanh(0.5 * x) + 0.5


def _log_sigmoid(z):
    return jnp.minimum(z, 0.0) - jnp.log(1.0 + jnp.exp(-jnp.abs(z)))


def _dot(a, b):
    return jnp.dot(a, b, preferred_element_type=F32)


def _split_bf16(x, terms):
    parts = []
    for _ in range(terms):
        p = x.astype(BF16)
        parts.append(p)
        x = x - p.astype(F32)
    return parts


def _dot_nt(a, b):
    return lax.dot_general(a, b, (((1,), (1,)), ((), ())), preferred_element_type=F32)


def _dot_tn(a, b):
    return lax.dot_general(a, b, (((0,), (0,)), ((), ())), preferred_element_type=F32)


def _row_tile_slice(j, n_rows, first_row=0):
    return pl.ds(first_row * ROW_TILES + j, n_rows, stride=ROW_TILES)


def _store_row_tiles(ref, val, first_row=0):
    for j in range(ROW_TILES):
        lo = val[:, j * LANES:(j + 1) * LANES]
        hi = val[:, HALF_MODEL + j * LANES:HALF_MODEL + (j + 1) * LANES]
        ref[_row_tile_slice(j, val.shape[0], first_row), :] = pltpu.pack_elementwise(
            [lo, hi], packed_dtype=BF16)


def _load_row_tile(ref, j, n_rows, first_row=0):
    words = ref[_row_tile_slice(j, n_rows, first_row), :]
    return tuple(pltpu.unpack_elementwise(words, index=half, packed_dtype=BF16, unpacked_dtype=F32)
                 for half in range(2))


def _ada_kernel(c_ref, w_ref, b_ref, o_ref):
    c = c_ref[...]
    s = c * _sigmoid(c)
    o_ref[...] = _dot(s.astype(BF16), w_ref[...].astype(BF16)) + b_ref[...]


def _ada(cond, w_ada, b_ada):
    rows = cond.shape[0]
    n = w_ada.shape[1]
    tn = 1536
    return pl.pallas_call(
        _ada_kernel,
        grid=(n // tn,),
        in_specs=[pl.BlockSpec((rows, D_MODEL), lambda j: (0, 0)),
                  pl.BlockSpec((D_MODEL, tn), lambda j: (0, j)),
                  pl.BlockSpec((1, tn), lambda j: (0, j))],
        out_specs=pl.BlockSpec((rows, tn), lambda j: (0, j)),
        out_shape=jax.ShapeDtypeStruct((rows, n), F32),
        compiler_params=pltpu.CompilerParams(vmem_limit_bytes=VMEM_LIMIT),
        name="ada_mod",
    )(cond, w_ada, b_ada.reshape(1, n))


def _group_maps(n_ctx_tiles):
    def ctx_map(i, *_):
        return (jnp.minimum(i, n_ctx_tiles - 1), 0)

    def lat_map(i, *_):
        return (jnp.maximum(i - n_ctx_tiles, 0), 0)

    return ctx_map, lat_map


def _token_specs(tm, n_ctx_tiles, tiles_per_latent_seq, ctx_mod_row):
    ctx_map, lat_map = _group_maps(n_ctx_tiles)

    def pos_map(i, *_):
        return (jnp.maximum(i - n_ctx_tiles, 0) % tiles_per_latent_seq, 0)

    def mod_map(i, *_):
        return (jnp.where(i < n_ctx_tiles, ctx_mod_row,
                          jnp.maximum(i - n_ctx_tiles, 0) // tiles_per_latent_seq), 0, 0)

    return [pl.BlockSpec((tm, D_MODEL), ctx_map),
            pl.BlockSpec((tm, D_MODEL), lat_map),
            pl.BlockSpec((tm, D_MODEL), pos_map),
            pl.BlockSpec((1, N_MOD, D_MODEL), mod_map)]


def _inproj_kernel(xc_ref, xl_ref, pos_ref, mod_ref, wm_ref, wt_ref, wdh_ref, wdl_ref, bd_ref,
                   o_ref, la_ref, *, n_ctx_tiles):
    tail = lambda lo, hi: wt_ref[:, lo - COL_GATE_A:hi - COL_GATE_A]
    i = pl.program_id(0)

    def project(x_of):
        for r0 in range(0, TM_PROJ, PROJ_GROUP):
            project_rows(slice(r0, r0 + PROJ_GROUP), x_of(slice(r0, r0 + PROJ_GROUP)))

    def project_rows(rows, x):
        h = (_layer_norm(x) * (1.0 + mod_ref[0, 1:2, :]) + mod_ref[0, 0:1, :]).astype(BF16)
        r_hi, r_lo = _split_bf16(_dot(h, tail(COL_R, PROJ_COLS)), 2)
        for d in range(2):
            z = (_dot(r_hi, wdh_ref[d]) + _dot(r_lo, wdh_ref[d]) + _dot(r_hi, wdl_ref[d])
                 + bd_ref[d])
            la_ref[rows, d * GLA_DK:(d + 1) * GLA_DK] = _log_sigmoid(z) * (1.0 / GATE_NORMALIZER)
        hg = 0.5 * _dot(h, wm_ref[:, COL_G:COL_GATE_A])
        o_ref[rows, COL_G:COL_GATE_A] = hg * (jnp.tanh(hg) + 1.0)
        gates = _dot(h, tail(COL_GATE_A, COL_F))
        o_ref[rows, COL_GATE_A:COL_F] = jnp.tanh(0.5 * gates) + 1.0
        o_ref[rows, :COL_G] = _dot(h, wm_ref[:, :COL_G])
        o_ref[rows, COL_F:COL_R] = _dot(h, tail(COL_F, COL_R))

    @pl.when(i < n_ctx_tiles)
    def _():
        project(lambda rows: xc_ref[rows, :])

    @pl.when(i >= n_ctx_tiles)
    def _():
        project(lambda rows: xl_ref[rows, :] + pos_ref[rows, :])


def _inproj(x_ctx, x_lat, pos, mod, w_main, w_tail, w_dec, b_dec, lat_len):
    t_ctx, t_lat = x_ctx.shape[0], x_lat.shape[0]
    t_all = t_ctx + t_lat
    n_ctx_tiles = t_ctx // TM_PROJ
    kern = functools.partial(_inproj_kernel, n_ctx_tiles=n_ctx_tiles)
    specs = _token_specs(TM_PROJ, n_ctx_tiles, lat_len // TM_PROJ, mod.shape[0] - 1)
    w_dec_hi = w_dec.astype(BF16)
    w_dec_lo = (w_dec - w_dec_hi.astype(F32)).astype(BF16)

    def const(shape):
        return pl.BlockSpec(shape, lambda i: (0,) * len(shape))

    return pl.pallas_call(
        kern,
        grid=(t_all // TM_PROJ,),
        in_specs=specs + [pl.BlockSpec(w_main.shape, lambda i: (0, 0), pipeline_mode=pl.Buffered(1)),
                          pl.BlockSpec(w_tail.shape, lambda i: (0, 0), pipeline_mode=pl.Buffered(1)),
                          const((2, LANES, GLA_DK)), const((2, LANES, GLA_DK)), const((2, 1, GLA_DK))],
        out_specs=[pl.BlockSpec((TM_PROJ, COL_R), lambda i: (i, 0)),
                   pl.BlockSpec((TM_PROJ, 2 * GLA_DK), lambda i: (i, 0))],
        out_shape=[jax.ShapeDtypeStruct((t_all, COL_R), F32),
                   jax.ShapeDtypeStruct((t_all, 2 * GLA_DK), F32)],
        compiler_params=pltpu.CompilerParams(
            dimension_semantics=("arbitrary",), vmem_limit_bytes=VMEM_LIMIT),
        name="ln_inproj",
    )(x_ctx, x_lat, pos, mod, w_main, w_tail, w_dec_hi, w_dec_lo, b_dec)


def _gla_kernel(*refs, seq_len, has_s0, emit_state):
    it = iter(refs)
    q_ref, k_ref, v_ref, laf_ref, lab_ref, g_ref = (next(it) for _ in range(6))
    s0_ref = next(it) if has_s0 else None
    o_ref = next(it)
    sout_ref = next(it) if emit_state else None
    cum_ref, a_ref, qi_ref, ko_ref, dec_ref, op_ref, st_ref = (next(it) for _ in range(7))

    C = GLA_CHUNK
    n_chunks = seq_len // C

    def rows(n):
        if isinstance(n, int):
            return pl.ds(n * C, C)
        return pl.ds(pl.multiple_of(n * C, C), C)

    def dec_rows(n):
        if isinstance(n, int):
            return pl.ds(n * SUBLANES, SUBLANES)
        return pl.ds(pl.multiple_of(n * SUBLANES, SUBLANES), SUBLANES)

    def loop(body):
        if n_chunks <= 2:
            for n in range(n_chunks):
                body(n)
        else:
            def step(m, carry):
                body(2 * m)
                body(2 * m + 1)
                return carry
            lax.fori_loop(0, n_chunks // 2, step, 0)

    rt = lax.broadcasted_iota(jnp.int32, (C, C), 0)
    ct = lax.broadcasted_iota(jnp.int32, (C, C), 1)
    tri = ((rt >= ct).astype(BF16), (ct >= rt).astype(BF16))
    row_id = lax.broadcasted_iota(jnp.int32, (C, DK_HEAD), 0)

    def cumsum_chunk(n):
        for d, la_ref in enumerate((laf_ref, lab_ref)):
            la_hi, la_lo = _split_bf16(la_ref[rows(n), :], 2)
            cum_ref[d, rows(n), :] = _dot(tri[d], la_hi) + _dot(tri[d], la_lo)

    loop(cumsum_chunk)

    query_rows = ({}, {})
    keep = ({}, {})
    for d in range(2):
        blk = C // 2
        while blk >= GLA_LEAF:
            q_parity = 1 if d == 0 else 0
            query_rows[d][blk] = ((row_id // blk) % 2) == q_parity
            qb, kb = rt // blk, ct // blk
            keep[d][blk] = ((qb % 2) == q_parity) & ((qb == kb + 1) if d == 0 else (kb == qb + 1))
            blk //= 2
        order = (rt >= ct) if d == 0 else (ct >= rt)
        keep[d][0] = ((rt // GLA_LEAF) == (ct // GLA_LEAF)) & order

    def block_rows(cum, first, step, count):
        span = C // count
        parts = [jnp.broadcast_to(cum[first + p * step:first + p * step + 1, :], (span, DK_HEAD))
                 for p in range(count)]
        return parts[0] if count == 1 else jnp.concatenate(parts, axis=0)

    def scores(n, d):
        cum = cum_ref[d, rows(n), :]
        q = q_ref[rows(n), :]
        k = k_ref[rows(n), :]
        acc = None
        blk = C // 2
        while blk >= GLA_LEAF:
            pairs = C // (2 * blk)
            bnd = blk - 1 if d == 0 else blk
            w = jnp.exp(-jnp.abs(cum - block_rows(cum, bnd, 2 * blk, pairs)))
            x = (jnp.where(query_rows[d][blk], q, k) * w).astype(BF16)
            s = jnp.where(keep[d][blk], _dot_nt(x, x), 0.0)
            acc = s if acc is None else acc + s
            blk //= 2
        mid = GLA_LEAF // 2 - 1 if d == 0 else GLA_LEAF // 2
        e = cum - block_rows(cum, mid, GLA_LEAF, C // GLA_LEAF)
        s = _dot_nt((q * jnp.exp(e)).astype(BF16), (k * jnp.exp(-e)).astype(BF16))
        acc = acc + jnp.where(keep[d][0], s, 0.0)
        a_ref[d, rows(n), :] = acc.astype(BF16)
        end = cum[C - 1:C, :] if d == 0 else cum[0:1, :]
        qi_ref[d, rows(n), :] = (q * jnp.exp(cum)).astype(BF16)
        ko_ref[d, rows(n), :] = (k * jnp.exp(end - cum)).astype(BF16)
        dec_ref[d, dec_rows(n), :] = jnp.broadcast_to(jnp.exp(end), (SUBLANES, DK_HEAD))

    def scores_chunk(n):
        scores(n, 0)
        scores(n, 1)

    loop(scores_chunk)

    for d in range(2):
        if has_s0:
            st_ref[d] = s0_ref[0, d, 0].T
        else:
            st_ref[d] = jnp.zeros((DV_HEAD, DK_HEAD), F32)

    def scan(n, d):
        v = v_ref[rows(n), :].astype(BF16)
        st = st_ref[d]
        o = _dot(a_ref[d, rows(n), :], v) + _dot_nt(qi_ref[d, rows(n), :], st.astype(BF16))
        st_ref[d] = st * dec_ref[d, dec_rows(n), :][0:1, :] + _dot_tn(v, ko_ref[d, rows(n), :])
        op_ref[d, rows(n), :] = o

    def scan_chunk(m):
        scan(m, 0)
        scan(n_chunks - 1 - m, 1)

    loop(scan_chunk)
    if emit_state:
        sout_ref[0, 0, 0] = st_ref[0].T
        sout_ref[0, 1, 0] = st_ref[1].T

    g = g_ref[...]

    def finish_chunk(n):
        o = op_ref[0, rows(n), :] + op_ref[1, rows(n), :]
        ms = jnp.mean(o * o, axis=-1, keepdims=True)
        o_ref[rows(n), :] = o * lax.rsqrt(ms + LN_EPS) * g

    loop(finish_chunk)


def _gla(proj, la, g, s0, *, n_seq, seq_len, row0, emit_state):
    has_s0 = s0 is not None
    blk0 = row0 // seq_len
    kern = functools.partial(_gla_kernel, seq_len=seq_len, has_s0=has_s0, emit_state=emit_state)
    in_specs = [
        pl.BlockSpec((seq_len, DK_HEAD), lambda b, h: (blk0 + b, COL_Q // DK_HEAD + h)),
        pl.BlockSpec((seq_len, DK_HEAD), lambda b, h: (blk0 + b, COL_K // DK_HEAD + h)),
        pl.BlockSpec((seq_len, DV_HEAD), lambda b, h: (blk0 + b, COL_V // DV_HEAD + h)),
        pl.BlockSpec((seq_len, DK_HEAD), lambda b, h: (blk0 + b, h)),
        pl.BlockSpec((seq_len, DK_HEAD), lambda b, h: (blk0 + b, GLA_HEADS + h)),
        pl.BlockSpec((1, DV_HEAD), lambda b, h: (0, 0)),
    ]
    args = [proj, proj, proj, la, la, g]
    if has_s0:
        in_specs.append(pl.BlockSpec((1, 2, 1, DK_HEAD, DV_HEAD), lambda b, h: (b, 0, h, 0, 0)))
        args.append(s0)
    out_specs = [pl.BlockSpec((seq_len, DV_HEAD), lambda b, h: (b, h))]
    out_shape = [jax.ShapeDtypeStruct((n_seq * seq_len, GLA_DV), F32)]
    if emit_state:
        out_specs.append(pl.BlockSpec((1, 2, 1, DK_HEAD, DV_HEAD), lambda b, h: (b, 0, h, 0, 0)))
        out_shape.append(jax.ShapeDtypeStruct((n_seq, 2, GLA_HEADS, DK_HEAD, DV_HEAD), F32))

    res = pl.pallas_call(
        kern,
        grid=(n_seq, GLA_HEADS),
        in_specs=in_specs,
        out_specs=out_specs,
        out_shape=out_shape,
        scratch_shapes=[pltpu.VMEM((2, seq_len, DK_HEAD), F32),
                        pltpu.VMEM((2, seq_len, GLA_CHUNK), BF16),
                        pltpu.VMEM((2, seq_len, DK_HEAD), BF16),
                        pltpu.VMEM((2, seq_len, DK_HEAD), BF16),
                        pltpu.VMEM((2, seq_len // GLA_CHUNK * SUBLANES, DK_HEAD), F32),
                        pltpu.VMEM((2, seq_len, DV_HEAD), F32),
                        pltpu.VMEM((2, DV_HEAD, DK_HEAD), F32)],
        compiler_params=pltpu.CompilerParams(
            dimension_semantics=("arbitrary", "arbitrary"), vmem_limit_bytes=VMEM_LIMIT),
        name="gla_seq%d" % seq_len,
    )(*args)
    return res


def _fnet_kernel(f_ref, cl_ref, sl_ref, cg_ref, sg_ref, o_ref, uc_ref, us_ref, *, seq_len):
    cg = cg_ref[...]
    sg = sg_ref[...]
    for grp in range(FNET_GROUPS):
        lo = grp * FNET_GROUP_DIM
        u = f_ref[:, lo:lo + FNET_GROUP_DIM].astype(BF16)
        uc_ref[:, lo:lo + FNET_GROUP_DIM] = _dot(u, cg).astype(BF16)
        us_ref[:, lo:lo + FNET_GROUP_DIM] = _dot(u, sg).astype(BF16)
    mixed = _dot(cl_ref[...], uc_ref[...]) - _dot(sl_ref[...], us_ref[...])
    o_ref[...] = mixed * (1.0 / math.sqrt(seq_len * FNET_GROUP_DIM))


def _dft_mats(n):
    j = np.arange(n, dtype=np.int64)
    ang = (2.0 * np.pi / n) * ((j[:, None] * j[None, :]) % n).astype(np.float64)
    return (jnp.asarray(np.cos(ang), dtype=F32).astype(BF16),
            jnp.asarray(np.sin(ang), dtype=F32).astype(BF16))


def _fnet(proj, *, n_seq, seq_len, row0):
    blk0 = row0 // seq_len
    cl, sl = _dft_mats(seq_len)
    cg, sg = _dft_mats(FNET_GROUP_DIM)
    kern = functools.partial(_fnet_kernel, seq_len=seq_len)
    return pl.pallas_call(
        kern,
        grid=(n_seq,),
        in_specs=[pl.BlockSpec((seq_len, FNET_DIM), lambda b: (blk0 + b, COL_F // FNET_DIM)),
                  pl.BlockSpec((seq_len, seq_len), lambda b: (0, 0)),
                  pl.BlockSpec((seq_len, seq_len), lambda b: (0, 0)),
                  pl.BlockSpec((FNET_GROUP_DIM, FNET_GROUP_DIM), lambda b: (0, 0)),
                  pl.BlockSpec((FNET_GROUP_DIM, FNET_GROUP_DIM), lambda b: (0, 0))],
        out_specs=pl.BlockSpec((seq_len, FNET_DIM), lambda b: (b, 0)),
        out_shape=jax.ShapeDtypeStruct((n_seq * seq_len, FNET_DIM), F32),
        scratch_shapes=[pltpu.VMEM((seq_len, FNET_DIM), BF16),
                        pltpu.VMEM((seq_len, FNET_DIM), BF16)],
        compiler_params=pltpu.CompilerParams(
            dimension_semantics=("arbitrary",), vmem_limit_bytes=VMEM_LIMIT),
        name="fnet_seq%d" % seq_len,
    )(proj, cl, sl, cg, sg)


def _merge_kernel(xc_ref, xl_ref, pos_ref, mod_ref, oc_ref, ol_ref, mc_ref, ml_ref,
                  g_ref, ga_ref, gb_ref, wbg_ref, wbf_ref, wo_ref, l1g_ref, l1b_ref, wrh_ref, wrl_ref,
                  br_ref, x1_ref, h2_ref, ridx_ref, rw_ref, *, n_ctx_tiles, alpha):
    i = pl.program_id(0)

    def compute(rows, x, o, mx):
        a = (o * g_ref[rows, :]).astype(BF16)
        gla_out = _dot(a, wbg_ref[...])
        fnet_out = _dot(mx.astype(BF16), wbf_ref[...])
        merged = ga_ref[rows, :] * gla_out + gb_ref[rows, :] * fnet_out
        mix = _dot(merged.astype(BF16), wo_ref[...])
        y = alpha * x + mod_ref[0, 2:3, :] * mix
        x1 = _layer_norm(y) * l1g_ref[...] + l1b_ref[...]
        x1_ref[rows, :] = x1
        h2 = _layer_norm(x1) * (1.0 + mod_ref[0, 4:5, :]) + mod_ref[0, 3:4, :]
        _store_row_tiles(h2_ref, h2, rows.start)

        h_hi, h_lo = _split_bf16(h2, 2)
        logits = (_dot(h_hi, wrh_ref[...]) + _dot(h_lo, wrh_ref[...]) + _dot(h_hi, wrl_ref[...])
                  + br_ref[...])
        lane_i = lax.broadcasted_iota(jnp.int32, logits.shape, 1)
        lane = lane_i.astype(F32)
        idx_out = jnp.zeros(logits.shape, F32)
        val_out = jnp.zeros(logits.shape, F32)
        top0 = None
        denom = None
        for kk in range(TOP_K):
            m = jnp.max(logits, axis=-1, keepdims=True)
            sel = jnp.min(jnp.where(logits == m, lane, float(LANES)), axis=-1, keepdims=True)
            if kk == 0:
                top0 = m
                p = jnp.ones_like(m)
                denom = p
            else:
                p = jnp.exp(m - top0)
                denom = denom + p
            idx_out = jnp.where(lane_i == kk, sel, idx_out)
            val_out = jnp.where(lane_i == kk, p, val_out)
            logits = jnp.where(lane == sel, -jnp.inf, logits)
        ridx_ref[rows, :] = idx_out.astype(jnp.int32)
        rw_ref[rows, :] = val_out / denom

    tm = x1_ref.shape[0]
    groups = [slice(r0, r0 + TM_MERGE_GROUP) for r0 in range(0, tm, TM_MERGE_GROUP)]

    @pl.when(i < n_ctx_tiles)
    def _():
        for rows in groups:
            compute(rows, xc_ref[rows, :], oc_ref[rows, :], mc_ref[rows, :])

    @pl.when(i >= n_ctx_tiles)
    def _():
        for rows in groups:
            compute(rows, xl_ref[rows, :] + pos_ref[rows, :], ol_ref[rows, :], ml_ref[rows, :])


def _merge(x_ctx, x_lat, pos, mod, o_ctx, o_lat, mixed_ctx, mixed_lat, proj,
           wbg, wbf, wo, l1g, l1b, wr_hi, wr_lo, br, lat_len, alpha):
    t_ctx, t_lat = x_ctx.shape[0], x_lat.shape[0]
    t_all = t_ctx + t_lat
    tm = TM_MERGE
    n_ctx_tiles = t_ctx // tm
    kern = functools.partial(_merge_kernel, n_ctx_tiles=n_ctx_tiles, alpha=alpha)
    specs = _token_specs(tm, n_ctx_tiles, lat_len // tm, mod.shape[0] - 1)
    ctx_map, lat_map = _group_maps(n_ctx_tiles)

    def const(shape):
        return pl.BlockSpec(shape, lambda i: (0,) * len(shape))

    in_specs = specs + [
        pl.BlockSpec((tm, GLA_DV), ctx_map),
        pl.BlockSpec((tm, GLA_DV), lat_map),
        pl.BlockSpec((tm, FNET_DIM), ctx_map),
        pl.BlockSpec((tm, FNET_DIM), lat_map),
        pl.BlockSpec((tm, GLA_DV), lambda i: (i, COL_G // GLA_DV)),
        pl.BlockSpec((tm, D_MODEL), lambda i: (i, COL_GATE_A // D_MODEL)),
        pl.BlockSpec((tm, D_MODEL), lambda i: (i, COL_GATE_B // D_MODEL)),
        const((GLA_DV, D_MODEL)), const((FNET_DIM, D_MODEL)), const((D_MODEL, D_MODEL)),
        const((1, D_MODEL)), const((1, D_MODEL)),
        const((D_MODEL, LANES)), const((D_MODEL, LANES)), const((1, LANES)),
    ]
    out_specs = [pl.BlockSpec((tm, D_MODEL), lambda i: (i, 0)),
                 pl.BlockSpec((tm * ROW_TILES, LANES), lambda i: (i, 0)),
                 pl.BlockSpec((tm, LANES), lambda i: (i, 0)),
                 pl.BlockSpec((tm, LANES), lambda i: (i, 0))]
    out_shape = [jax.ShapeDtypeStruct((t_all, D_MODEL), F32),
                 jax.ShapeDtypeStruct((t_all * ROW_TILES, LANES), ROW_DTYPE),
                 jax.ShapeDtypeStruct((t_all, LANES), jnp.int32),
                 jax.ShapeDtypeStruct((t_all, LANES), F32)]
    return pl.pallas_call(
        kern,
        grid=(t_all // tm,),
        in_specs=in_specs,
        out_specs=out_specs,
        out_shape=out_shape,
        compiler_params=pltpu.CompilerParams(
            dimension_semantics=("arbitrary",), vmem_limit_bytes=VMEM_LIMIT),
        name="merge_ln1_router",
    )(x_ctx, x_lat, pos, mod, o_ctx, o_lat, mixed_ctx, mixed_lat, proj, proj, proj,
      wbg, wbf, wo, l1g, l1b, wr_hi, wr_lo, br)


def _sc_mesh():
    return plsc.VectorSubcoreMesh(core_axis_name="c", subcore_axis_name="s")


def _sc_worker_id():
    return lax.axis_index("s") * SC_CORES + lax.axis_index("c")


def _sc_scatter_rows(src, idx, n_out):
    n_src = src.shape[0]
    w = SC_WINDOW
    n_chunks = n_src // (SC_WORKERS * w)
    copies = idx.shape[0]
    assert n_chunks % 2 == 0 and idx.shape == (copies, SC_WORKERS, n_chunks, w)

    @functools.partial(
        pl.kernel, mesh=_sc_mesh(),
        out_type=jax.ShapeDtypeStruct((n_out, ROW_TILES, LANES), ROW_DTYPE),
        scratch_types=[pltpu.VMEM((copies * n_chunks, w), jnp.int32),
                       pltpu.VMEM((2, w, ROW_TILES, LANES), ROW_DTYPE),
                       pltpu.SemaphoreType.DMA((2,)),
                       pltpu.SemaphoreType.DMA((2,))],
        name="moe_dispatch_scatter")
    def k(src_hbm, idx_hbm, out_hbm, idx_v, rows_v, rsem, wsem):
        wid = _sc_worker_id()
        base = wid * (n_chunks * w)
        for kk in range(copies):
            pltpu.sync_copy(idx_hbm.at[kk, wid], idx_v.at[pl.ds(kk * n_chunks, n_chunks)])

        def read(j, slot):
            return pltpu.make_async_copy(src_hbm.at[pl.ds(base + j * w, w)], rows_v.at[slot],
                                         rsem.at[slot])

        def scatter(j, kk, slot):
            return pltpu.make_async_copy(rows_v.at[slot], out_hbm.at[idx_v.at[kk * n_chunks + j]],
                                         wsem.at[slot])

        read(0, 0).start()

        @pl.loop(0, n_chunks, step=2)
        def _(jj):
            read(jj, 0).wait()

            @pl.when(jj > 0)
            def _():
                for kk in range(copies):
                    scatter(jj - 1, kk, 1).wait()

            read(jj + 1, 1).start()
            for kk in range(copies):
                scatter(jj, kk, 0).start()
            read(jj + 1, 1).wait()
            for kk in range(copies):
                scatter(jj, kk, 0).wait()

            @pl.when(jj + 2 < n_chunks)
            def _():
                read(jj + 2, 0).start()

            for kk in range(copies):
                scatter(jj + 1, kk, 1).start()

        for kk in range(copies):
            scatter(n_chunks - 1, kk, 1).wait()

    return k(src, idx)


def _sc_gather_rows(table, idx):
    _, n_chunks, w = idx.shape
    assert n_chunks % 2 == 0 and idx.shape[0] == SC_WORKERS and w == SC_WINDOW
    n_out = SC_WORKERS * n_chunks * w

    @functools.partial(
        pl.kernel, mesh=_sc_mesh(),
        out_type=jax.ShapeDtypeStruct((n_out, ROW_TILES, LANES), ROW_DTYPE),
        scratch_types=[pltpu.VMEM((n_chunks, w), jnp.int32),
                       pltpu.VMEM((2, w, ROW_TILES, LANES), ROW_DTYPE),
                       pltpu.SemaphoreType.DMA((2,)),
                       pltpu.SemaphoreType.DMA((2,))],
        name="moe_combine_gather")
    def k(table_hbm, idx_hbm, out_hbm, idx_v, rows_v, gsem, wsem):
        wid = _sc_worker_id()
        base = wid * (n_chunks * w)
        pltpu.sync_copy(idx_hbm.at[wid], idx_v)

        def gather(j, slot):
            return pltpu.make_async_copy(table_hbm.at[idx_v.at[j]], rows_v.at[slot], gsem.at[slot])

        def write(j, slot):
            return pltpu.make_async_copy(rows_v.at[slot], out_hbm.at[pl.ds(base + j * w, w)],
                                         wsem.at[slot])

        gather(0, 0).start()

        @pl.loop(0, n_chunks, step=2)
        def _(jj):
            gather(jj, 0).wait()

            @pl.when(jj > 0)
            def _():
                write(jj - 1, 1).wait()

            gather(jj + 1, 1).start()
            write(jj, 0).start()
            gather(jj + 1, 1).wait()
            write(jj, 0).wait()

            @pl.when(jj + 2 < n_chunks)
            def _():
                gather(jj + 2, 0).start()

            write(jj + 1, 1).start()

        write(n_chunks - 1, 1).wait()

    return k(table, idx)


def _moe_kernel(be_ref, nu_ref, nv_ref, slot_ref, nxt_ref, x_ref, wgu_hbm, bgu_ref, wd_hbm, bd_ref,
                o_ref, wgu_st, wd_st, wgu_bf, wd_bf, xb_ref, sem):
    b = pl.program_id(0)
    e = be_ref[b]
    prev = be_ref[jnp.maximum(b - 1, 0)]
    active = b < nu_ref[0]
    changed = (b == 0) | (e != prev)

    def weight_copies(expert, s):
        return (pltpu.make_async_copy(wgu_hbm.at[expert], wgu_st.at[s], sem.at[0, s]),
                pltpu.make_async_copy(wd_hbm.at[expert], wd_st.at[s], sem.at[1, s]))

    @pl.when(active & changed)
    def _():
        s = slot_ref[b]

        @pl.when(b == 0)
        def _():
            for cp in weight_copies(e, s):
                cp.start()

        for cp in weight_copies(e, s):
            cp.wait()
        wgu_bf[...] = wgu_st[s].astype(BF16)
        wd_bf[...] = wd_st[s].astype(BF16)
        nxt = nxt_ref[b]

        @pl.when(nxt >= 0)
        def _():
            for cp in weight_copies(nxt, 1 - s):
                cp.start()

    n_valid = nv_ref[b]

    def expert_mlp(groups):
        for r0 in groups:
            valid = lax.broadcasted_iota(jnp.int32, (MOE_GROUP, LANES), 0) < n_valid - r0
            for j in range(ROW_TILES):
                for half, xj in enumerate(_load_row_tile(x_ref, j, MOE_GROUP, r0)):
                    c0 = half * HALF_MODEL + j * LANES
                    xb_ref[r0:r0 + MOE_GROUP, c0:c0 + LANES] = jnp.where(valid, xj, 0.0).astype(BF16)
            gu = _dot(xb_ref[r0:r0 + MOE_GROUP, :], wgu_bf[...]) + bgu_ref[0]
            gate = jnp.minimum(gu[:, :D_EXPERT], SWIGLU_LIMIT)
            up = jnp.clip(gu[:, D_EXPERT:], -SWIGLU_LIMIT, SWIGLU_LIMIT)
            glu = gate * _sigmoid(gate * SWIGLU_ALPHA)
            act = ((up + 1.0) * glu).astype(BF16)
            _store_row_tiles(o_ref, _dot(act, wd_bf[...]) + bd_ref[0], r0)

    all_groups = list(range(0, MOE_ROWS, MOE_GROUP))

    @pl.when(active & (n_valid > MOE_ROWS - MOE_GROUP))
    def _():
        expert_mlp(all_groups)

    @pl.when(active & (n_valid <= MOE_ROWS - MOE_GROUP))
    def _():
        expert_mlp(all_groups[:-1])


def _moe(tables, xs, w_gate_up, b_gate_up, w_down, b_down):
    p_rows = xs.shape[0] // ROW_TILES
    n_blocks = p_rows // MOE_ROWS

    def blk(b, be, nu, *_):
        return jnp.minimum(b, nu[0] - 1)

    def expert(b, be, nu, *_):
        return (be[blk(b, be, nu)], 0, 0)

    def rows(b, be, nu, *_):
        return (blk(b, be, nu), 0)

    grid_spec = pltpu.PrefetchScalarGridSpec(
        num_scalar_prefetch=len(tables),
        grid=(n_blocks,),
        in_specs=[
            pl.BlockSpec((MOE_ROWS * ROW_TILES, LANES), rows),
            pl.BlockSpec(memory_space=pl.ANY),
            pl.BlockSpec((1, 1, 2 * D_EXPERT), expert),
            pl.BlockSpec(memory_space=pl.ANY),
            pl.BlockSpec((1, 1, D_MODEL), expert),
        ],
        out_specs=pl.BlockSpec((MOE_ROWS * ROW_TILES, LANES), rows),
        scratch_shapes=[pltpu.VMEM((2, D_MODEL, 2 * D_EXPERT), F32),
                        pltpu.VMEM((2, D_EXPERT, D_MODEL), F32),
                        pltpu.VMEM((D_MODEL, 2 * D_EXPERT), BF16),
                        pltpu.VMEM((D_EXPERT, D_MODEL), BF16),
                        pltpu.VMEM((MOE_ROWS, D_MODEL), BF16),
                        pltpu.SemaphoreType.DMA((2, 2))],
    )
    return pl.pallas_call(
        _moe_kernel,
        grid_spec=grid_spec,
        out_shape=jax.ShapeDtypeStruct((p_rows * ROW_TILES, LANES), ROW_DTYPE),
        compiler_params=pltpu.CompilerParams(
            dimension_semantics=("arbitrary",), vmem_limit_bytes=VMEM_LIMIT),
        name="moe_grouped_mlp",
    )(*tables, xs, w_gate_up, b_gate_up.reshape(N_EXPERTS, 1, 2 * D_EXPERT), w_down,
      b_down.reshape(N_EXPERTS, 1, D_MODEL))


def _combine_kernel(x1_ref, y0_ref, y1_ref, y2_ref, y3_ref, rw_ref, mod_ref, g_ref, b_ref, o_ref,
                    *, alpha):
    rw = rw_ref[...]
    y_refs = (y0_ref, y1_ref, y2_ref, y3_ref)
    pieces = [None] * (2 * ROW_TILES)
    for j in range(ROW_TILES):
        for kk in range(TOP_K):
            for half, yj in enumerate(_load_row_tile(y_refs[kk], j, rw.shape[0])):
                term = rw[:, kk:kk + 1] * yj
                slot = half * ROW_TILES + j
                pieces[slot] = term if kk == 0 else pieces[slot] + term
    ff = jnp.concatenate(pieces, axis=-1)
    y = alpha * x1_ref[...] + mod_ref[0, 5:6, :] * ff
    o_ref[...] = _layer_norm(y) * g_ref[...] + b_ref[...]


def _combine(x1, yg, rw, mod, l2g, l2b, *, row0, n_rows, mod_map, alpha):
    tm = TM_MIX
    t0 = row0 // tm
    tiles = n_rows // tm
    kern = functools.partial(_combine_kernel, alpha=alpha)

    def y_spec(kk):
        return pl.BlockSpec((tm * ROW_TILES, LANES), lambda i: (kk * tiles + i, 0))

    return pl.pallas_call(
        kern,
        grid=(n_rows // tm,),
        in_specs=[pl.BlockSpec((tm, D_MODEL), lambda i: (t0 + i, 0))]
        + [y_spec(kk) for kk in range(TOP_K)]
        + [pl.BlockSpec((tm, LANES), lambda i: (t0 + i, 0)),
           pl.BlockSpec((1, N_MOD, D_MODEL), mod_map),
           pl.BlockSpec((1, D_MODEL), lambda i: (0, 0)),
           pl.BlockSpec((1, D_MODEL), lambda i: (0, 0))],
        out_specs=pl.BlockSpec((tm, D_MODEL), lambda i: (i, 0)),
        out_shape=jax.ShapeDtypeStruct((n_rows, D_MODEL), F32),
        compiler_params=pltpu.CompilerParams(
            dimension_semantics=("arbitrary",), vmem_limit_bytes=VMEM_LIMIT),
        name="combine_ln2",
    )(x1, yg, yg, yg, yg, rw, mod, l2g, l2b)


def _route_kernel(ridx_ref, dest_ref, cnt_ref, run_ref, bst_ref):
    phase = pl.program_id(0)
    i = pl.program_id(1)
    tm = ridx_ref.shape[0]
    ridx = ridx_ref[...]
    lane = lax.broadcasted_iota(jnp.int32, (tm, LANES), 1)
    hits = [ridx[:, kk:kk + 1] == lane for kk in range(TOP_K)]
    chosen = jnp.where(hits[0], 1.0, 0.0)
    for kk in range(1, TOP_K):
        chosen = chosen + jnp.where(hits[kk], 1.0, 0.0)
    colsum = jnp.sum(chosen, axis=0, keepdims=True)

    @pl.when((phase == 0) & (i == 0))
    def _():
        run_ref[...] = jnp.zeros_like(run_ref)

    @pl.when(phase == 0)
    def _():
        run_ref[...] = run_ref[...] + colsum

    @pl.when((phase == 1) & (i == 0))
    def _():
        counts = run_ref[...]
        cnt_ref[...] = counts
        blocks = jnp.floor((counts + (MOE_ROWS - 1.0)) * (1.0 / MOE_ROWS))
        r = lax.broadcasted_iota(jnp.int32, (LANES, LANES), 0)
        c = lax.broadcasted_iota(jnp.int32, (LANES, LANES), 1)
        before = jnp.dot(blocks, (r < c).astype(F32), precision=HIGHEST, preferred_element_type=F32)
        bst_ref[...] = before * float(MOE_ROWS)
        run_ref[...] = jnp.zeros_like(run_ref)

    @pl.when(phase == 1)
    def _():
        rt = lax.broadcasted_iota(jnp.int32, (tm, tm), 0)
        ct = lax.broadcasted_iota(jnp.int32, (tm, tm), 1)
        earlier = _dot((ct < rt).astype(BF16), chosen.astype(BF16))
        row_of = bst_ref[0:1, :] + run_ref[0:1, :] + earlier
        out = jnp.zeros((tm, LANES), F32)
        for kk in range(TOP_K):
            dk = jnp.sum(jnp.where(hits[kk], row_of, 0.0), axis=-1, keepdims=True)
            out = jnp.where(lane == kk, dk, out)
        dest_ref[...] = out.T[0:SUBLANES, :].astype(jnp.int32)
        run_ref[...] = run_ref[...] + colsum


def _route(ridx):
    t_all = ridx.shape[0]
    tm = TM_ROUTE
    return pl.pallas_call(
        _route_kernel,
        grid=(2, t_all // tm),
        in_specs=[pl.BlockSpec((tm, LANES), lambda p, i: (i, 0))],
        out_specs=[pl.BlockSpec((SUBLANES, tm), lambda p, i: (0, i * p)),
                   pl.BlockSpec((SUBLANES, LANES), lambda p, i: (0, 0))],
        out_shape=[jax.ShapeDtypeStruct((SUBLANES, t_all), jnp.int32),
                   jax.ShapeDtypeStruct((SUBLANES, LANES), F32)],
        scratch_shapes=[pltpu.VMEM((SUBLANES, LANES), F32),
                        pltpu.VMEM((SUBLANES, LANES), F32)],
        compiler_params=pltpu.CompilerParams(
            dimension_semantics=("arbitrary", "arbitrary"), vmem_limit_bytes=VMEM_LIMIT),
        name="moe_route",
    )(ridx)


def _routing_tables(counts, n_blocks):
    experts = jnp.arange(N_EXPERTS, dtype=jnp.int32)
    blocks_per = (counts + MOE_ROWS - 1) // MOE_ROWS
    bends = jnp.cumsum(blocks_per)
    bstarts = bends - blocks_per
    blocks = jnp.arange(n_blocks, dtype=jnp.int32)
    block_expert = jnp.minimum(
        jnp.sum((bends[None, :] <= blocks[:, None]).astype(jnp.int32), axis=1), N_EXPERTS - 1)
    n_used = bends[-1:].astype(jnp.int32)
    owner = block_expert[:, None] == experts[None, :]

    def per_block(table):
        return jnp.sum(jnp.where(owner, table[None, :], 0), axis=1)

    n_valid = jnp.clip(per_block(counts) - (blocks - per_block(bstarts)) * MOE_ROWS,
                       0, MOE_ROWS).astype(jnp.int32)
    present = blocks_per > 0
    ordinal = jnp.cumsum(present.astype(jnp.int32)) - 1
    later = lax.cummin(jnp.where(present, experts, N_EXPERTS), reverse=True)
    succ = jnp.concatenate([later[1:], jnp.full((1,), N_EXPERTS, jnp.int32)])
    succ = jnp.where(succ >= N_EXPERTS, -1, succ)
    stage_slot = (per_block(ordinal) % 2).astype(jnp.int32)
    next_expert = per_block(succ).astype(jnp.int32)
    return (block_expert.astype(jnp.int32), n_used, n_valid, stage_slot, next_expert)


def _pos_embed_2d(n_tokens):
    rows = n_tokens // GRID_W
    r = jnp.repeat(jnp.arange(rows), GRID_W).astype(F32)
    col = jnp.tile(jnp.arange(GRID_W), rows).astype(F32)
    quarter = D_MODEL // 4
    omega = 1.0 / (10000.0 ** (jnp.arange(quarter, dtype=F32) / quarter))
    er = r[:, None] * omega
    ec = col[:, None] * omega
    return jnp.concatenate([jnp.sin(er), jnp.cos(er), jnp.sin(ec), jnp.cos(ec)], axis=-1)


def _split_w_in(w):
    o_r = 2 * GLA_DK + 2 * GLA_DV
    o_f = o_r + DECAY_RANK
    o_gate = o_f + FNET_DIM
    col_scale = jnp.where(jnp.arange(o_r) < GLA_DK, DK_HEAD ** -0.5, 1.0).astype(w.dtype)
    w_main = (w[:, :o_r] * col_scale[None, :]).astype(BF16)
    pad = jnp.zeros((w.shape[0], LANES - DECAY_RANK), BF16)
    w_tail = jnp.concatenate([w[:, o_gate:].astype(BF16), w[:, o_f:o_gate].astype(BF16),
                              w[:, o_r:o_f].astype(BF16), pad], axis=1)
    return w_main, w_tail


def kernel(x_prompt, x_sample, state_gla, c, c_ctx, w_ada, b_ada, w_in, w_dec_fwd, b_dec_fwd,
           w_dec_bwd, b_dec_bwd, gla_norm_g, w_br_gla, w_br_fnet, w_out, ln1_g, ln1_b, w_router,
           b_router, w_gate_up, b_gate_up, w_down, b_down, ln2_g, ln2_b):
    n_req, ctx_len, _ = x_prompt.shape
    n_lat, lat_len, _ = x_sample.shape
    depth = w_in.shape[0]
    alpha = (2.0 * depth) ** 0.25
    t_ctx = n_req * ctx_len
    t_lat = n_lat * lat_len
    t_all = t_ctx + t_lat

    x_ctx = x_prompt.reshape(t_ctx, D_MODEL)
    x_lat = x_sample.reshape(t_lat, D_MODEL)
    pos = _pos_embed_2d(lat_len)
    zero_pos = jnp.zeros_like(pos)

    cond_rows = -(-(n_lat + 1) // SUBLANES) * SUBLANES
    cond = jnp.zeros((cond_rows, D_MODEL), F32).at[:n_lat].set(c).at[cond_rows - 1].set(c_ctx)

    n_moe_blocks = (t_all * TOP_K) // MOE_ROWS + N_EXPERTS
    tok_chunks = t_all // (SC_WORKERS * SC_WINDOW)
    states = []
    for l in range(depth):
        mod = _ada(cond, w_ada[l], b_ada[l]).reshape(cond_rows, N_MOD, D_MODEL)
        layer_pos = pos if l == 0 else zero_pos
        w_dec = jnp.zeros((2, LANES, GLA_DK), F32)
        w_dec = w_dec.at[0, :DECAY_RANK].set(w_dec_fwd[l]).at[1, :DECAY_RANK].set(w_dec_bwd[l])
        b_dec = jnp.stack([b_dec_fwd[l], b_dec_bwd[l]]).reshape(2, 1, GLA_DK)
        proj, la = _inproj(x_ctx, x_lat, layer_pos, mod, *_split_w_in(w_in[l]), w_dec, b_dec, lat_len)

        norm_g = gla_norm_g[l].reshape(1, DV_HEAD)
        o_ctx, s_new = _gla(proj, la, norm_g, None, n_seq=n_req, seq_len=ctx_len, row0=0,
                            emit_state=True)
        (o_lat,) = _gla(proj, la, norm_g, state_gla[:, l], n_seq=n_lat, seq_len=lat_len, row0=t_ctx,
                        emit_state=False)
        states.append(s_new)

        mixed_ctx = _fnet(proj, n_seq=n_req, seq_len=ctx_len, row0=0)
        mixed_lat = _fnet(proj, n_seq=n_lat, seq_len=lat_len, row0=t_ctx)

        wr = jnp.zeros((D_MODEL, LANES), F32).at[:, :N_EXPERTS].set(w_router[l])
        br = jnp.full((1, LANES), -1e30, F32).at[0, :N_EXPERTS].set(b_router[l])
        wr_hi = wr.astype(BF16)
        wr_lo = (wr - wr_hi.astype(F32)).astype(BF16)
        x1, h2, ridx, rw = _merge(
            x_ctx, x_lat, layer_pos, mod, o_ctx, o_lat, mixed_ctx, mixed_lat, proj,
            w_br_gla[l].astype(BF16), w_br_fnet[l].astype(BF16), (0.5 * w_out[l]).astype(BF16),
            ln1_g[l].reshape(1, D_MODEL), ln1_b[l].reshape(1, D_MODEL), wr_hi, wr_lo, br, lat_len,
            alpha)

        dest, counts = _route(ridx)
        moe_tables = _routing_tables(counts[0, :N_EXPERTS].astype(jnp.int32), n_moe_blocks)
        dest = dest[:TOP_K]
        scatter_idx = dest.reshape(TOP_K, SC_WORKERS, tok_chunks, SC_WINDOW)
        p_rows = n_moe_blocks * MOE_ROWS
        xs = _sc_scatter_rows(h2.reshape(t_all, ROW_TILES, LANES), scatter_idx, p_rows)
        yb = _moe(moe_tables, xs.reshape(p_rows * ROW_TILES, LANES),
                  w_gate_up[l], b_gate_up[l], w_down[l], b_down[l])
        yb = yb.reshape(p_rows, ROW_TILES, LANES)

        def gathered(row0, n_rows):
            idx = dest[:, row0:row0 + n_rows].reshape(SC_WORKERS, -1, SC_WINDOW)
            return _sc_gather_rows(yb, idx).reshape(TOP_K * n_rows * ROW_TILES, LANES)

        l2g = ln2_g[l].reshape(1, D_MODEL)
        l2b = ln2_b[l].reshape(1, D_MODEL)
        tiles_per_seq = lat_len // TM_MIX
        yg_ctx = gathered(0, t_ctx)
        yg_lat = gathered(t_ctx, t_lat)
        x_ctx = _combine(x1, yg_ctx, rw, mod, l2g, l2b, row0=0, n_rows=t_ctx,
                         mod_map=lambda i: (cond_rows - 1, 0, 0), alpha=alpha)
        x_lat = _combine(x1, yg_lat, rw, mod, l2g, l2b, row0=t_ctx, n_rows=t_lat,
                         mod_map=lambda i: (i // tiles_per_seq, 0, 0), alpha=alpha)

    y_prompt = x_ctx.reshape(x_prompt.shape)
    y_sample = x_lat.reshape(x_sample.shape)
    new_state = jnp.stack(states, axis=1).astype(x_prompt.dtype)
    return (y_prompt, y_sample, new_state)
```

```python
import functools
import math

import numpy as np
import jax
import jax.numpy as jnp
from jax import lax
from jax.experimental import pallas as pl
from jax.experimental.pallas import tpu as pltpu
from jax.experimental.pallas import tpu_sc as plsc

F32 = jnp.float32
BF16 = jnp.bfloat16

D_MODEL = 1024
GRID_W = 64
GLA_HEADS = 4
DK_HEAD = 128
DV_HEAD = 256
GLA_DK = GLA_HEADS * DK_HEAD
GLA_DV = GLA_HEADS * DV_HEAD
DECAY_RANK = 16
GATE_NORMALIZER = 16.0
FNET_GROUPS = 4
FNET_GROUP_DIM = 128
FNET_DIM = FNET_GROUPS * FNET_GROUP_DIM
N_EXPERTS = 32
TOP_K = 4
D_EXPERT = 1024
SWIGLU_LIMIT = 7.0
SWIGLU_ALPHA = 1.702
LN_EPS = 1e-6
N_MOD = 6

LANES = 128
SUBLANES = 8
HALF_MODEL = D_MODEL // 2
ROW_TILES = HALF_MODEL // LANES
ROW_DTYPE = jnp.uint32
COL_Q = 0
COL_K = GLA_DK
COL_V = 2 * GLA_DK
COL_G = COL_V + GLA_DV
COL_GATE_A = COL_G + GLA_DV
COL_GATE_B = COL_GATE_A + D_MODEL
COL_F = COL_GATE_B + D_MODEL
COL_R = COL_F + FNET_DIM
PROJ_COLS = COL_R + LANES

GLA_CHUNK = 128
GLA_LEAF = 16
assert GLA_CHUNK == DK_HEAD
TM_PROJ = 512
PROJ_GROUP = 256
TM_MIX = 256
TM_MERGE = 512
TM_MERGE_GROUP = 128
TM_ROUTE = 1024
MOE_ROWS = 512
MOE_GROUP = 256
VMEM_LIMIT = 56 * 1024 * 1024

SC_CORES = 2
SC_SUBCORES = 16
SC_WORKERS = SC_CORES * SC_SUBCORES
SC_WINDOW = 64

HIGHEST = lax.Precision.HIGHEST


def _layer_norm(x):
    mu = jnp.mean(x, axis=-1, keepdims=True)
    xc = x - mu
    var = jnp.mean(xc * xc, axis=-1, keepdims=True)
    return xc * lax.rsqrt(var + LN_EPS)


def _sigmoid(x):
    return 0.5 * jnp.tanh(0.5 * x) + 0.5


def _log_sigmoid(z):
    return jnp.minimum(z, 0.0) - jnp.log(1.0 + jnp.exp(-jnp.abs(z)))


def _dot(a, b):
    return jnp.dot(a, b, preferred_element_type=F32)


def _split_bf16(x, terms):
    parts = []
    for _ in range(terms):
        p = x.astype(BF16)
        parts.append(p)
        x = x - p.astype(F32)
    return parts


def _dot_nt(a, b):
    return lax.dot_general(a, b, (((1,), (1,)), ((), ())), preferred_element_type=F32)


def _dot_tn(a, b):
    return lax.dot_general(a, b, (((0,), (0,)), ((), ())), preferred_element_type=F32)


def _row_tile_slice(j, n_rows, first_row=0):
    return pl.ds(first_row * ROW_TILES + j, n_rows, stride=ROW_TILES)


def _store_row_tiles(ref, val, first_row=0):
    for j in range(ROW_TILES):
        lo = val[:, j * LANES:(j + 1) * LANES]
        hi = val[:, HALF_MODEL + j * LANES:HALF_MODEL + (j + 1) * LANES]
        ref[_row_tile_slice(j, val.shape[0], first_row), :] = pltpu.pack_elementwise(
            [lo, hi], packed_dtype=BF16)


def _load_row_tile(ref, j, n_rows, first_row=0):
    words = ref[_row_tile_slice(j, n_rows, first_row), :]
    return tuple(pltpu.unpack_elementwise(words, index=half, packed_dtype=BF16, unpacked_dtype=F32)
                 for half in range(2))


def _ada_kernel(c_ref, w_ref, b_ref, o_ref):
    c = c_ref[...]
    s = c * _sigmoid(c)
    o_ref[...] = _dot(s.astype(BF16), w_ref[...].astype(BF16)) + b_ref[...]


def _ada(cond, w_ada, b_ada):
    rows = cond.shape[0]
    n = w_ada.shape[1]
    tn = 1536
    return pl.pallas_call(
        _ada_kernel,
        grid=(n // tn,),
        in_specs=[pl.BlockSpec((rows, D_MODEL), lambda j: (0, 0)),
                  pl.BlockSpec((D_MODEL, tn), lambda j: (0, j)),
                  pl.BlockSpec((1, tn), lambda j: (0, j))],
        out_specs=pl.BlockSpec((rows, tn), lambda j: (0, j)),
        out_shape=jax.ShapeDtypeStruct((rows, n), F32),
        compiler_params=pltpu.CompilerParams(vmem_limit_bytes=VMEM_LIMIT),
        name="ada_mod",
    )(cond, w_ada, b_ada.reshape(1, n))


def _group_maps(n_ctx_tiles):
    def ctx_map(i, *_):
        return (jnp.minimum(i, n_ctx_tiles - 1), 0)

    def lat_map(i, *_):
        return (jnp.maximum(i - n_ctx_tiles, 0), 0)

    return ctx_map, lat_map


def _token_specs(tm, n_ctx_tiles, tiles_per_latent_seq, ctx_mod_row):
    ctx_map, lat_map = _group_maps(n_ctx_tiles)

    def pos_map(i, *_):
        return (jnp.maximum(i - n_ctx_tiles, 0) % tiles_per_latent_seq, 0)

    def mod_map(i, *_):
        return (jnp.where(i < n_ctx_tiles, ctx_mod_row,
                          jnp.maximum(i - n_ctx_tiles, 0) // tiles_per_latent_seq), 0, 0)

    return [pl.BlockSpec((tm, D_MODEL), ctx_map),
            pl.BlockSpec((tm, D_MODEL), lat_map),
            pl.BlockSpec((tm, D_MODEL), pos_map),
            pl.BlockSpec((1, N_MOD, D_MODEL), mod_map)]


def _inproj_kernel(xc_ref, xl_ref, pos_ref, mod_ref, wm_ref, wt_ref, wdh_ref, wdl_ref, bd_ref,
                   o_ref, la_ref, *, n_ctx_tiles):
    tail = lambda lo, hi: wt_ref[lo - COL_GATE_A:hi - COL_GATE_A, :]
    i = pl.program_id(0)

    def project(x_of):
        for r0 in range(0, TM_PROJ, PROJ_GROUP):
            project_rows(slice(r0, r0 + PROJ_GROUP), x_of(slice(r0, r0 + PROJ_GROUP)))

    def project_rows(rows, x):
        h = (_layer_norm(x) * (1.0 + mod_ref[0, 1:2, :]) + mod_ref[0, 0:1, :]).astype(BF16)
        r_hi, r_lo = _split_bf16(_dot_nt(h, tail(COL_R, PROJ_COLS)), 2)
        for d in range(2):
            z = (_dot(r_hi, wdh_ref[d]) + _dot(r_lo, wdh_ref[d]) + _dot(r_hi, wdl_ref[d])
                 + bd_ref[d])
            la_ref[rows, d * GLA_DK:(d + 1) * GLA_DK] = _log_sigmoid(z) * (1.0 / GATE_NORMALIZER)
        hg = 0.5 * _dot_nt(h, wm_ref[COL_G:COL_GATE_A, :])
        o_ref[rows, COL_G:COL_GATE_A] = hg * (jnp.tanh(hg) + 1.0)
        gates = _dot_nt(h, tail(COL_GATE_A, COL_F))
        o_ref[rows, COL_GATE_A:COL_F] = jnp.tanh(0.5 * gates) + 1.0
        o_ref[rows, :COL_G] = _dot_nt(h, wm_ref[:COL_G, :])
        o_ref[rows, COL_F:COL_R] = _dot_nt(h, tail(COL_F, COL_R))

    @pl.when(i < n_ctx_tiles)
    def _():
        project(lambda rows: xc_ref[rows, :])

    @pl.when(i >= n_ctx_tiles)
    def _():
        project(lambda rows: xl_ref[rows, :] + pos_ref[rows, :])


def _inproj(x_ctx, x_lat, pos, mod, w_main, w_tail, w_dec, b_dec, lat_len):
    t_ctx, t_lat = x_ctx.shape[0], x_lat.shape[0]
    t_all = t_ctx + t_lat
    n_ctx_tiles = t_ctx // TM_PROJ
    kern = functools.partial(_inproj_kernel, n_ctx_tiles=n_ctx_tiles)
    specs = _token_specs(TM_PROJ, n_ctx_tiles, lat_len // TM_PROJ, mod.shape[0] - 1)
    w_dec_hi = w_dec.astype(BF16)
    w_dec_lo = (w_dec - w_dec_hi.astype(F32)).astype(BF16)

    def const(shape):
        return pl.BlockSpec(shape, lambda i: (0,) * len(shape))

    return pl.pallas_call(
        kern,
        grid=(t_all // TM_PROJ,),
        in_specs=specs + [pl.BlockSpec(w_main.shape, lambda i: (0, 0), pipeline_mode=pl.Buffered(1)),
                          pl.BlockSpec(w_tail.shape, lambda i: (0, 0), pipeline_mode=pl.Buffered(1)),
                          const((2, LANES, GLA_DK)), const((2, LANES, GLA_DK)), const((2, 1, GLA_DK))],
        out_specs=[pl.BlockSpec((TM_PROJ, COL_R), lambda i: (i, 0)),
                   pl.BlockSpec((TM_PROJ, 2 * GLA_DK), lambda i: (i, 0))],
        out_shape=[jax.ShapeDtypeStruct((t_all, COL_R), F32),
                   jax.ShapeDtypeStruct((t_all, 2 * GLA_DK), F32)],
        compiler_params=pltpu.CompilerParams(
            dimension_semantics=("arbitrary",), vmem_limit_bytes=VMEM_LIMIT),
        name="ln_inproj",
    )(x_ctx, x_lat, pos, mod, w_main, w_tail, w_dec_hi, w_dec_lo, b_dec)


def _gla_kernel(*refs, seq_len, has_s0, emit_state):
    it = iter(refs)
    q_ref, k_ref, v_ref, laf_ref, lab_ref, g_ref = (next(it) for _ in range(6))
    s0_ref = next(it) if has_s0 else None
    o_ref = next(it)
    sout_ref = next(it) if emit_state else None
    cum_ref, aq_ref, ko_ref, dec_ref, op_ref, st_ref = (next(it) for _ in range(6))

    C = GLA_CHUNK
    n_chunks = seq_len // C

    def rows(n):
        if isinstance(n, int):
            return pl.ds(n * C, C)
        return pl.ds(pl.multiple_of(n * C, C), C)

    def loop(body):
        if n_chunks <= 2:
            for n in range(n_chunks):
                body(n)
        else:
            def step(m, carry):
                body(2 * m)
                body(2 * m + 1)
                return carry
            lax.fori_loop(0, n_chunks // 2, step, 0)

    rt = lax.broadcasted_iota(jnp.int32, (C, C), 0)
    ct = lax.broadcasted_iota(jnp.int32, (C, C), 1)
    tri = ((rt >= ct).astype(BF16), (ct >= rt).astype(BF16))
    row_id = lax.broadcasted_iota(jnp.int32, (C, DK_HEAD), 0)

    def cumsum_chunk(n):
        for d, la_ref in enumerate((laf_ref, lab_ref)):
            la_hi, la_lo = _split_bf16(la_ref[rows(n), :], 2)
            cum_ref[d, rows(n), :] = _dot(tri[d], la_hi) + _dot(tri[d], la_lo)

    loop(cumsum_chunk)

    query_rows = ({}, {})
    keep = ({}, {})
    for d in range(2):
        blk = C // 2
        while blk >= GLA_LEAF:
            q_parity = 1 if d == 0 else 0
            query_rows[d][blk] = ((row_id // blk) % 2) == q_parity
            qb, kb = rt // blk, ct // blk
            keep[d][blk] = ((qb % 2) == q_parity) & ((qb == kb + 1) if d == 0 else (kb == qb + 1))
            blk //= 2
        order = (rt >= ct) if d == 0 else (ct >= rt)
        keep[d][0] = ((rt // GLA_LEAF) == (ct // GLA_LEAF)) & order

    def block_rows(cum, first, step, count):
        span = C // count
        parts = [jnp.broadcast_to(cum[first + p * step:first + p * step + 1, :], (span, DK_HEAD))
                 for p in range(count)]
        return parts[0] if count == 1 else jnp.concatenate(parts, axis=0)

    def scores(n, d):
        cum = cum_ref[d, rows(n), :]
        q = q_ref[rows(n), :]
        k = k_ref[rows(n), :]
        acc = None
        blk = C // 2
        while blk >= GLA_LEAF:
            pairs = C // (2 * blk)
            bnd = blk - 1 if d == 0 else blk
            w = jnp.exp(-jnp.abs(cum - block_rows(cum, bnd, 2 * blk, pairs)))
            x = (jnp.where(query_rows[d][blk], q, k) * w).astype(BF16)
            s = jnp.where(keep[d][blk], _dot_nt(x, x), 0.0)
            acc = s if acc is None else acc + s
            blk //= 2
        mid = GLA_LEAF // 2 - 1 if d == 0 else GLA_LEAF // 2
        e = cum - block_rows(cum, mid, GLA_LEAF, C // GLA_LEAF)
        s = _dot_nt((q * jnp.exp(e)).astype(BF16), (k * jnp.exp(-e)).astype(BF16))
        acc = acc + jnp.where(keep[d][0], s, 0.0)
        aq_ref[d, rows(n), 0:C] = acc.astype(BF16)
        aq_ref[d, rows(n), C:C + DK_HEAD] = (q * jnp.exp(cum)).astype(BF16)
        end = cum[C - 1:C, :] if d == 0 else cum[0:1, :]
        ko_ref[d, rows(n), :] = (k * jnp.exp(end - cum)).astype(BF16)
        dec_ref[d, rows(n), :] = jnp.broadcast_to(jnp.exp(end), (DK_HEAD, DK_HEAD)).T

    def scores_chunk(n):
        scores(n, 0)
        scores(n, 1)

    loop(scores_chunk)

    for d in range(2):
        if has_s0:
            st_ref[d] = s0_ref[0, d, 0]
        else:
            st_ref[d] = jnp.zeros((DK_HEAD, DV_HEAD), F32)

    def scan(n, d):
        v = v_ref[rows(n), :].astype(BF16)
        st = st_ref[d]
        op_ref[d, rows(n), :] = _dot(aq_ref[d, rows(n), :],
                                     jnp.concatenate([v, st.astype(BF16)], axis=0))
        dec = dec_ref[d, rows(n), :]
        st_ref[d] = (st * jnp.concatenate([dec] * (DV_HEAD // DK_HEAD), axis=1)
                     + _dot_tn(ko_ref[d, rows(n), :], v))

    def scan_chunk(m):
        scan(m, 0)
        scan(n_chunks - 1 - m, 1)

    loop(scan_chunk)
    if emit_state:
        sout_ref[0, 0, 0] = st_ref[0]
        sout_ref[0, 1, 0] = st_ref[1]

    g = g_ref[...]

    def finish_chunk(n):
        o = op_ref[0, rows(n), :] + op_ref[1, rows(n), :]
        ms = jnp.mean(o * o, axis=-1, keepdims=True)
        o_ref[rows(n), :] = o * lax.rsqrt(ms + LN_EPS) * g

    loop(finish_chunk)


def _gla(proj, la, g, s0, *, n_seq, seq_len, row0, emit_state):
    has_s0 = s0 is not None
    blk0 = row0 // seq_len
    kern = functools.partial(_gla_kernel, seq_len=seq_len, has_s0=has_s0, emit_state=emit_state)
    in_specs = [
        pl.BlockSpec((seq_len, DK_HEAD), lambda b, h: (blk0 + b, COL_Q // DK_HEAD + h)),
        pl.BlockSpec((seq_len, DK_HEAD), lambda b, h: (blk0 + b, COL_K // DK_HEAD + h)),
        pl.BlockSpec((seq_len, DV_HEAD), lambda b, h: (blk0 + b, COL_V // DV_HEAD + h)),
        pl.BlockSpec((seq_len, DK_HEAD), lambda b, h: (blk0 + b, h)),
        pl.BlockSpec((seq_len, DK_HEAD), lambda b, h: (blk0 + b, GLA_HEADS + h)),
        pl.BlockSpec((1, DV_HEAD), lambda b, h: (0, 0)),
    ]
    args = [proj, proj, proj, la, la, g]
    if has_s0:
        in_specs.append(pl.BlockSpec((1, 2, 1, DK_HEAD, DV_HEAD), lambda b, h: (b, 0, h, 0, 0)))
        args.append(s0)
    out_specs = [pl.BlockSpec((seq_len, DV_HEAD), lambda b, h: (b, h))]
    out_shape = [jax.ShapeDtypeStruct((n_seq * seq_len, GLA_DV), F32)]
    if emit_state:
        out_specs.append(pl.BlockSpec((1, 2, 1, DK_HEAD, DV_HEAD), lambda b, h: (b, 0, h, 0, 0)))
        out_shape.append(jax.ShapeDtypeStruct((n_seq, 2, GLA_HEADS, DK_HEAD, DV_HEAD), F32))

    res = pl.pallas_call(
        kern,
        grid=(n_seq, GLA_HEADS),
        in_specs=in_specs,
        out_specs=out_specs,
        out_shape=out_shape,
        scratch_shapes=[pltpu.VMEM((2, seq_len, DK_HEAD), F32),
                        pltpu.VMEM((2, seq_len, GLA_CHUNK + DK_HEAD), BF16),
                        pltpu.VMEM((2, seq_len, DK_HEAD), BF16),
                        pltpu.VMEM((2, seq_len, DK_HEAD), F32),
                        pltpu.VMEM((2, seq_len, DV_HEAD), F32),
                        pltpu.VMEM((2, DK_HEAD, DV_HEAD), F32)],
        compiler_params=pltpu.CompilerParams(
            dimension_semantics=("arbitrary", "arbitrary"), vmem_limit_bytes=VMEM_LIMIT),
        name="gla_seq%d" % seq_len,
    )(*args)
    return res


def _fnet_kernel(f_ref, cl_ref, sl_ref, cg_ref, sg_ref, o_ref, uc_ref, us_ref, *, seq_len):
    cg = cg_ref[...]
    sg = sg_ref[...]
    for grp in range(FNET_GROUPS):
        lo = grp * FNET_GROUP_DIM
        u = f_ref[:, lo:lo + FNET_GROUP_DIM].astype(BF16)
        uc_ref[:, lo:lo + FNET_GROUP_DIM] = _dot(u, cg).astype(BF16)
        us_ref[:, lo:lo + FNET_GROUP_DIM] = _dot(u, sg).astype(BF16)
    mixed = _dot(cl_ref[...], uc_ref[...]) - _dot(sl_ref[...], us_ref[...])
    o_ref[...] = mixed * (1.0 / math.sqrt(seq_len * FNET_GROUP_DIM))


def _dft_mats(n):
    j = np.arange(n, dtype=np.int64)
    ang = (2.0 * np.pi / n) * ((j[:, None] * j[None, :]) % n).astype(np.float64)
    return (jnp.asarray(np.cos(ang), dtype=F32).astype(BF16),
            jnp.asarray(np.sin(ang), dtype=F32).astype(BF16))


def _fnet(proj, *, n_seq, seq_len, row0):
    blk0 = row0 // seq_len
    cl, sl = _dft_mats(seq_len)
    cg, sg = _dft_mats(FNET_GROUP_DIM)
    kern = functools.partial(_fnet_kernel, seq_len=seq_len)
    return pl.pallas_call(
        kern,
        grid=(n_seq,),
        in_specs=[pl.BlockSpec((seq_len, FNET_DIM), lambda b: (blk0 + b, COL_F // FNET_DIM)),
                  pl.BlockSpec((seq_len, seq_len), lambda b: (0, 0)),
                  pl.BlockSpec((seq_len, seq_len), lambda b: (0, 0)),
                  pl.BlockSpec((FNET_GROUP_DIM, FNET_GROUP_DIM), lambda b: (0, 0)),
                  pl.BlockSpec((FNET_GROUP_DIM, FNET_GROUP_DIM), lambda b: (0, 0))],
        out_specs=pl.BlockSpec((seq_len, FNET_DIM), lambda b: (b, 0)),
        out_shape=jax.ShapeDtypeStruct((n_seq * seq_len, FNET_DIM), F32),
        scratch_shapes=[pltpu.VMEM((seq_len, FNET_DIM), BF16),
                        pltpu.VMEM((seq_len, FNET_DIM), BF16)],
        compiler_params=pltpu.CompilerParams(
            dimension_semantics=("arbitrary",), vmem_limit_bytes=VMEM_LIMIT),
        name="fnet_seq%d" % seq_len,
    )(proj, cl, sl, cg, sg)


def _merge_kernel(xc_ref, xl_ref, pos_ref, mod_ref, oc_ref, ol_ref, mc_ref, ml_ref,
                  g_ref, ga_ref, gb_ref, wbg_ref, wbf_ref, wo_ref, l1g_ref, l1b_ref, wrh_ref, wrl_ref,
                  br_ref, x1_ref, h2_ref, ridx_ref, rw_ref, *, n_ctx_tiles, alpha):
    i = pl.program_id(0)

    def compute(rows, x, o, mx):
        a = (o * g_ref[rows, :]).astype(BF16)
        gla_out = _dot(a, wbg_ref[...])
        fnet_out = _dot(mx.astype(BF16), wbf_ref[...])
        merged = ga_ref[rows, :] * gla_out + gb_ref[rows, :] * fnet_out
        mix = _dot(merged.astype(BF16), wo_ref[...])
        y = alpha * x + mod_ref[0, 2:3, :] * mix
        x1 = _layer_norm(y) * l1g_ref[...] + l1b_ref[...]
        x1_ref[rows, :] = x1
        h2 = _layer_norm(x1) * (1.0 + mod_ref[0, 4:5, :]) + mod_ref[0, 3:4, :]
        _store_row_tiles(h2_ref, h2, rows.start)

        h_hi, h_lo = _split_bf16(h2, 2)
        logits = (_dot(h_hi, wrh_ref[...]) + _dot(h_lo, wrh_ref[...]) + _dot(h_hi, wrl_ref[...])
                  + br_ref[...])
        lane_i = lax.broadcasted_iota(jnp.int32, logits.shape, 1)
        lane = lane_i.astype(F32)
        idx_out = jnp.zeros(logits.shape, F32)
        val_out = jnp.zeros(logits.shape, F32)
        top0 = None
        denom = None
        for kk in range(TOP_K):
            m = jnp.max(logits, axis=-1, keepdims=True)
            sel = jnp.min(jnp.where(logits == m, lane, float(LANES)), axis=-1, keepdims=True)
            if kk == 0:
                top0 = m
                p = jnp.ones_like(m)
                denom = p
            else:
                p = jnp.exp(m - top0)
                denom = denom + p
            idx_out = jnp.where(lane_i == kk, sel, idx_out)
            val_out = jnp.where(lane_i == kk, p, val_out)
            logits = jnp.where(lane == sel, -jnp.inf, logits)
        ridx_ref[rows, :] = idx_out.astype(jnp.int32)
        rw_ref[rows, :] = val_out / denom

    tm = x1_ref.shape[0]
    groups = [slice(r0, r0 + TM_MERGE_GROUP) for r0 in range(0, tm, TM_MERGE_GROUP)]

    @pl.when(i < n_ctx_tiles)
    def _():
        for rows in groups:
            compute(rows, xc_ref[rows, :], oc_ref[rows, :], mc_ref[rows, :])

    @pl.when(i >= n_ctx_tiles)
    def _():
        for rows in groups:
            compute(rows, xl_ref[rows, :] + pos_ref[rows, :], ol_ref[rows, :], ml_ref[rows, :])


def _merge(x_ctx, x_lat, pos, mod, o_ctx, o_lat, mixed_ctx, mixed_lat, proj,
           wbg, wbf, wo, l1g, l1b, wr_hi, wr_lo, br, lat_len, alpha):
    t_ctx, t_lat = x_ctx.shape[0], x_lat.shape[0]
    t_all = t_ctx + t_lat
    tm = TM_MERGE
    n_ctx_tiles = t_ctx // tm
    kern = functools.partial(_merge_kernel, n_ctx_tiles=n_ctx_tiles, alpha=alpha)
    specs = _token_specs(tm, n_ctx_tiles, lat_len // tm, mod.shape[0] - 1)
    ctx_map, lat_map = _group_maps(n_ctx_tiles)

    def const(shape):
        return pl.BlockSpec(shape, lambda i: (0,) * len(shape))

    in_specs = specs + [
        pl.BlockSpec((tm, GLA_DV), ctx_map),
        pl.BlockSpec((tm, GLA_DV), lat_map),
        pl.BlockSpec((tm, FNET_DIM), ctx_map),
        pl.BlockSpec((tm, FNET_DIM), lat_map),
        pl.BlockSpec((tm, GLA_DV), lambda i: (i, COL_G // GLA_DV)),
        pl.BlockSpec((tm, D_MODEL), lambda i: (i, COL_GATE_A // D_MODEL)),
        pl.BlockSpec((tm, D_MODEL), lambda i: (i, COL_GATE_B // D_MODEL)),
        const((GLA_DV, D_MODEL)), const((FNET_DIM, D_MODEL)), const((D_MODEL, D_MODEL)),
        const((1, D_MODEL)), const((1, D_MODEL)),
        const((D_MODEL, LANES)), const((D_MODEL, LANES)), const((1, LANES)),
    ]
    out_specs = [pl.BlockSpec((tm, D_MODEL), lambda i: (i, 0)),
                 pl.BlockSpec((tm * ROW_TILES, LANES), lambda i: (i, 0)),
                 pl.BlockSpec((tm, LANES), lambda i: (i, 0)),
                 pl.BlockSpec((tm, LANES), lambda i: (i, 0))]
    out_shape = [jax.ShapeDtypeStruct((t_all, D_MODEL), F32),
                 jax.ShapeDtypeStruct((t_all * ROW_TILES, LANES), ROW_DTYPE),
                 jax.ShapeDtypeStruct((t_all, LANES), jnp.int32),
                 jax.ShapeDtypeStruct((t_all, LANES), F32)]
    return pl.pallas_call(
        kern,
        grid=(t_all // tm,),
        in_specs=in_specs,
        out_specs=out_specs,
        out_shape=out_shape,
        compiler_params=pltpu.CompilerParams(
            dimension_semantics=("arbitrary",), vmem_limit_bytes=VMEM_LIMIT),
        name="merge_ln1_router",
    )(x_ctx, x_lat, pos, mod, o_ctx, o_lat, mixed_ctx, mixed_lat, proj, proj, proj,
      wbg, wbf, wo, l1g, l1b, wr_hi, wr_lo, br)


def _sc_mesh():
    return plsc.VectorSubcoreMesh(core_axis_name="c", subcore_axis_name="s")


def _sc_worker_id():
    return lax.axis_index("s") * SC_CORES + lax.axis_index("c")


def _sc_scatter_rows(src, idx, n_out):
    n_src = src.shape[0]
    w = SC_WINDOW
    n_chunks = n_src // (SC_WORKERS * w)
    copies = idx.shape[0]
    assert n_chunks % 2 == 0 and idx.shape == (copies, SC_WORKERS, n_chunks, w)

    @functools.partial(
        pl.kernel, mesh=_sc_mesh(),
        out_type=jax.ShapeDtypeStruct((n_out, ROW_TILES, LANES), ROW_DTYPE),
        scratch_types=[pltpu.VMEM((copies * n_chunks, w), jnp.int32),
                       pltpu.VMEM((2, w, ROW_TILES, LANES), ROW_DTYPE),
                       pltpu.SemaphoreType.DMA((2,)),
                       pltpu.SemaphoreType.DMA((2,))],
        name="moe_dispatch_scatter")
    def k(src_hbm, idx_hbm, out_hbm, idx_v, rows_v, rsem, wsem):
        wid = _sc_worker_id()
        base = wid * (n_chunks * w)
        for kk in range(copies):
            pltpu.sync_copy(idx_hbm.at[kk, wid], idx_v.at[pl.ds(kk * n_chunks, n_chunks)])

        def read(j, slot):
            return pltpu.make_async_copy(src_hbm.at[pl.ds(base + j * w, w)], rows_v.at[slot],
                                         rsem.at[slot])

        def scatter(j, kk, slot):
            return pltpu.make_async_copy(rows_v.at[slot], out_hbm.at[idx_v.at[kk * n_chunks + j]],
                                         wsem.at[slot])

        read(0, 0).start()

        @pl.loop(0, n_chunks, step=2)
        def _(jj):
            read(jj, 0).wait()

            @pl.when(jj > 0)
            def _():
                for kk in range(copies):
                    scatter(jj - 1, kk, 1).wait()

            read(jj + 1, 1).start()
            for kk in range(copies):
                scatter(jj, kk, 0).start()
            read(jj + 1, 1).wait()
            for kk in range(copies):
                scatter(jj, kk, 0).wait()

            @pl.when(jj + 2 < n_chunks)
            def _():
                read(jj + 2, 0).start()

            for kk in range(copies):
                scatter(jj + 1, kk, 1).start()

        for kk in range(copies):
            scatter(n_chunks - 1, kk, 1).wait()

    return k(src, idx)


def _sc_gather_rows(table, idx):
    _, n_chunks, w = idx.shape
    assert n_chunks % 2 == 0 and idx.shape[0] == SC_WORKERS and w == SC_WINDOW
    n_out = SC_WORKERS * n_chunks * w

    @functools.partial(
        pl.kernel, mesh=_sc_mesh(),
        out_type=jax.ShapeDtypeStruct((n_out, ROW_TILES, LANES), ROW_DTYPE),
        scratch_types=[pltpu.VMEM((n_chunks, w), jnp.int32),
                       pltpu.VMEM((2, w, ROW_TILES, LANES), ROW_DTYPE),
                       pltpu.SemaphoreType.DMA((2,)),
                       pltpu.SemaphoreType.DMA((2,))],
        name="moe_combine_gather")
    def k(table_hbm, idx_hbm, out_hbm, idx_v, rows_v, gsem, wsem):
        wid = _sc_worker_id()
        base = wid * (n_chunks * w)
        pltpu.sync_copy(idx_hbm.at[wid], idx_v)

        def gather(j, slot):
            return pltpu.make_async_copy(table_hbm.at[idx_v.at[j]], rows_v.at[slot], gsem.at[slot])

        def write(j, slot):
            return pltpu.make_async_copy(rows_v.at[slot], out_hbm.at[pl.ds(base + j * w, w)],
                                         wsem.at[slot])

        gather(0, 0).start()

        @pl.loop(0, n_chunks, step=2)
        def _(jj):
            gather(jj, 0).wait()

            @pl.when(jj > 0)
            def _():
                write(jj - 1, 1).wait()

            gather(jj + 1, 1).start()
            write(jj, 0).start()
            gather(jj + 1, 1).wait()
            write(jj, 0).wait()

            @pl.when(jj + 2 < n_chunks)
            def _():
                gather(jj + 2, 0).start()

            write(jj + 1, 1).start()

        write(n_chunks - 1, 1).wait()

    return k(table, idx)


def _moe_kernel(be_ref, nu_ref, nv_ref, slot_ref, nxt_ref, x_ref, wgu_hbm, bgu_ref, wd_hbm, bd_ref,
                o_ref, wgu_st, wd_st, wgu_bf, wd_bf, xb_ref, sem):
    b = pl.program_id(0)
    e = be_ref[b]
    prev = be_ref[jnp.maximum(b - 1, 0)]
    active = b < nu_ref[0]
    changed = (b == 0) | (e != prev)

    def weight_copies(expert, s):
        return (pltpu.make_async_copy(wgu_hbm.at[expert], wgu_st.at[s], sem.at[0, s]),
                pltpu.make_async_copy(wd_hbm.at[expert], wd_st.at[s], sem.at[1, s]))

    @pl.when(active & changed)
    def _():
        s = slot_ref[b]

        @pl.when(b == 0)
        def _():
            for cp in weight_copies(e, s):
                cp.start()

        for cp in weight_copies(e, s):
            cp.wait()
        wgu_bf[...] = wgu_st[s].astype(BF16)
        wd_bf[...] = wd_st[s].astype(BF16)
        nxt = nxt_ref[b]

        @pl.when(nxt >= 0)
        def _():
            for cp in weight_copies(nxt, 1 - s):
                cp.start()

    n_valid = nv_ref[b]

    def expert_mlp(groups):
        for r0 in groups:
            valid = lax.broadcasted_iota(jnp.int32, (MOE_GROUP, LANES), 0) < n_valid - r0
            for j in range(ROW_TILES):
                for half, xj in enumerate(_load_row_tile(x_ref, j, MOE_GROUP, r0)):
                    c0 = half * HALF_MODEL + j * LANES
                    xb_ref[r0:r0 + MOE_GROUP, c0:c0 + LANES] = jnp.where(valid, xj, 0.0).astype(BF16)
            gu = _dot(xb_ref[r0:r0 + MOE_GROUP, :], wgu_bf[...]) + bgu_ref[0]
            gate = jnp.minimum(gu[:, :D_EXPERT], SWIGLU_LIMIT)
            up = jnp.clip(gu[:, D_EXPERT:], -SWIGLU_LIMIT, SWIGLU_LIMIT)
            glu = gate * _sigmoid(gate * SWIGLU_ALPHA)
            act = ((up + 1.0) * glu).astype(BF16)
            _store_row_tiles(o_ref, _dot(act, wd_bf[...]) + bd_ref[0], r0)

    all_groups = list(range(0, MOE_ROWS, MOE_GROUP))

    @pl.when(active & (n_valid > MOE_ROWS - MOE_GROUP))
    def _():
        expert_mlp(all_groups)

    @pl.when(active & (n_valid <= MOE_ROWS - MOE_GROUP))
    def _():
        expert_mlp(all_groups[:-1])


def _moe(tables, xs, w_gate_up, b_gate_up, w_down, b_down):
    p_rows = xs.shape[0] // ROW_TILES
    n_blocks = p_rows // MOE_ROWS

    def blk(b, be, nu, *_):
        return jnp.minimum(b, nu[0] - 1)

    def expert(b, be, nu, *_):
        return (be[blk(b, be, nu)], 0, 0)

    def rows(b, be, nu, *_):
        return (blk(b, be, nu), 0)

    grid_spec = pltpu.PrefetchScalarGridSpec(
        num_scalar_prefetch=len(tables),
        grid=(n_blocks,),
        in_specs=[
            pl.BlockSpec((MOE_ROWS * ROW_TILES, LANES), rows),
            pl.BlockSpec(memory_space=pl.ANY),
            pl.BlockSpec((1, 1, 2 * D_EXPERT), expert),
            pl.BlockSpec(memory_space=pl.ANY),
            pl.BlockSpec((1, 1, D_MODEL), expert),
        ],
        out_specs=pl.BlockSpec((MOE_ROWS * ROW_TILES, LANES), rows),
        scratch_shapes=[pltpu.VMEM((2, D_MODEL, 2 * D_EXPERT), F32),
                        pltpu.VMEM((2, D_EXPERT, D_MODEL), F32),
                        pltpu.VMEM((D_MODEL, 2 * D_EXPERT), BF16),
                        pltpu.VMEM((D_EXPERT, D_MODEL), BF16),
                        pltpu.VMEM((MOE_ROWS, D_MODEL), BF16),
                        pltpu.SemaphoreType.DMA((2, 2))],
    )
    return pl.pallas_call(
        _moe_kernel,
        grid_spec=grid_spec,
        out_shape=jax.ShapeDtypeStruct((p_rows * ROW_TILES, LANES), ROW_DTYPE),
        compiler_params=pltpu.CompilerParams(
            dimension_semantics=("arbitrary",), vmem_limit_bytes=VMEM_LIMIT),
        name="moe_grouped_mlp",
    )(*tables, xs, w_gate_up, b_gate_up.reshape(N_EXPERTS, 1, 2 * D_EXPERT), w_down,
      b_down.reshape(N_EXPERTS, 1, D_MODEL))


def _combine_kernel(x1_ref, y0_ref, y1_ref, y2_ref, y3_ref, rw_ref, mod_ref, g_ref, b_ref, o_ref,
                    *, alpha):
    rw = rw_ref[...]
    y_refs = (y0_ref, y1_ref, y2_ref, y3_ref)
    pieces = [None] * (2 * ROW_TILES)
    for j in range(ROW_TILES):
        for kk in range(TOP_K):
            for half, yj in enumerate(_load_row_tile(y_refs[kk], j, rw.shape[0])):
                term = rw[:, kk:kk + 1] * yj
                slot = half * ROW_TILES + j
                pieces[slot] = term if kk == 0 else pieces[slot] + term
    ff = jnp.concatenate(pieces, axis=-1)
    y = alpha * x1_ref[...] + mod_ref[0, 5:6, :] * ff
    o_ref[...] = _layer_norm(y) * g_ref[...] + b_ref[...]


def _combine(x1, yg, rw, mod, l2g, l2b, *, row0, n_rows, mod_map, alpha):
    tm = TM_MIX
    t0 = row0 // tm
    tiles = n_rows // tm
    kern = functools.partial(_combine_kernel, alpha=alpha)

    def y_spec(kk):
        return pl.BlockSpec((tm * ROW_TILES, LANES), lambda i: (kk * tiles + i, 0))

    return pl.pallas_call(
        kern,
        grid=(n_rows // tm,),
        in_specs=[pl.BlockSpec((tm, D_MODEL), lambda i: (t0 + i, 0))]
        + [y_spec(kk) for kk in range(TOP_K)]
        + [pl.BlockSpec((tm, LANES), lambda i: (t0 + i, 0)),
           pl.BlockSpec((1, N_MOD, D_MODEL), mod_map),
           pl.BlockSpec((1, D_MODEL), lambda i: (0, 0)),
           pl.BlockSpec((1, D_MODEL), lambda i: (0, 0))],
        out_specs=pl.BlockSpec((tm, D_MODEL), lambda i: (i, 0)),
        out_shape=jax.ShapeDtypeStruct((n_rows, D_MODEL), F32),
        compiler_params=pltpu.CompilerParams(
            dimension_semantics=("arbitrary",), vmem_limit_bytes=VMEM_LIMIT),
        name="combine_ln2",
    )(x1, yg, yg, yg, yg, rw, mod, l2g, l2b)


def _route_kernel(ridx_ref, dest_ref, cnt_ref, run_ref, bst_ref):
    phase = pl.program_id(0)
    i = pl.program_id(1)
    tm = ridx_ref.shape[0]
    ridx = ridx_ref[...]
    lane = lax.broadcasted_iota(jnp.int32, (tm, LANES), 1)
    hits = [ridx[:, kk:kk + 1] == lane for kk in range(TOP_K)]
    chosen = jnp.where(hits[0], 1.0, 0.0)
    for kk in range(1, TOP_K):
        chosen = chosen + jnp.where(hits[kk], 1.0, 0.0)
    colsum = jnp.sum(chosen, axis=0, keepdims=True)

    @pl.when((phase == 0) & (i == 0))
    def _():
        run_ref[...] = jnp.zeros_like(run_ref)

    @pl.when(phase == 0)
    def _():
        run_ref[...] = run_ref[...] + colsum

    @pl.when((phase == 1) & (i == 0))
    def _():
        counts = run_ref[...]
        cnt_ref[...] = counts
        blocks = jnp.floor((counts + (MOE_ROWS - 1.0)) * (1.0 / MOE_ROWS))
        r = lax.broadcasted_iota(jnp.int32, (LANES, LANES), 0)
        c = lax.broadcasted_iota(jnp.int32, (LANES, LANES), 1)
        before = jnp.dot(blocks, (r < c).astype(F32), precision=HIGHEST, preferred_element_type=F32)
        bst_ref[...] = before * float(MOE_ROWS)
        run_ref[...] = jnp.zeros_like(run_ref)

    @pl.when(phase == 1)
    def _():
        rt = lax.broadcasted_iota(jnp.int32, (tm, tm), 0)
        ct = lax.broadcasted_iota(jnp.int32, (tm, tm), 1)
        earlier = _dot((ct < rt).astype(BF16), chosen.astype(BF16))
        row_of = bst_ref[0:1, :] + run_ref[0:1, :] + earlier
        out = jnp.zeros((tm, LANES), F32)
        for kk in range(TOP_K):
            dk = jnp.sum(jnp.where(hits[kk], row_of, 0.0), axis=-1, keepdims=True)
            out = jnp.where(lane == kk, dk, out)
        dest_ref[...] = out.T[0:SUBLANES, :].astype(jnp.int32)
        run_ref[...] = run_ref[...] + colsum


def _route(ridx):
    t_all = ridx.shape[0]
    tm = TM_ROUTE
    return pl.pallas_call(
        _route_kernel,
        grid=(2, t_all // tm),
        in_specs=[pl.BlockSpec((tm, LANES), lambda p, i: (i, 0))],
        out_specs=[pl.BlockSpec((SUBLANES, tm), lambda p, i: (0, i * p)),
                   pl.BlockSpec((SUBLANES, LANES), lambda p, i: (0, 0))],
        out_shape=[jax.ShapeDtypeStruct((SUBLANES, t_all), jnp.int32),
                   jax.ShapeDtypeStruct((SUBLANES, LANES), F32)],
        scratch_shapes=[pltpu.VMEM((SUBLANES, LANES), F32),
                        pltpu.VMEM((SUBLANES, LANES), F32)],
        compiler_params=pltpu.CompilerParams(
            dimension_semantics=("arbitrary", "arbitrary"), vmem_limit_bytes=VMEM_LIMIT),
        name="moe_route",
    )(ridx)


def _routing_tables(counts, n_blocks):
    experts = jnp.arange(N_EXPERTS, dtype=jnp.int32)
    blocks_per = (counts + MOE_ROWS - 1) // MOE_ROWS
    bends = jnp.cumsum(blocks_per)
    bstarts = bends - blocks_per
    blocks = jnp.arange(n_blocks, dtype=jnp.int32)
    block_expert = jnp.minimum(
        jnp.sum((bends[None, :] <= blocks[:, None]).astype(jnp.int32), axis=1), N_EXPERTS - 1)
    n_used = bends[-1:].astype(jnp.int32)
    owner = block_expert[:, None] == experts[None, :]

    def per_block(table):
        return jnp.sum(jnp.where(owner, table[None, :], 0), axis=1)

    n_valid = jnp.clip(per_block(counts) - (blocks - per_block(bstarts)) * MOE_ROWS,
                       0, MOE_ROWS).astype(jnp.int32)
    present = blocks_per > 0
    ordinal = jnp.cumsum(present.astype(jnp.int32)) - 1
    later = lax.cummin(jnp.where(present, experts, N_EXPERTS), reverse=True)
    succ = jnp.concatenate([later[1:], jnp.full((1,), N_EXPERTS, jnp.int32)])
    succ = jnp.where(succ >= N_EXPERTS, -1, succ)
    stage_slot = (per_block(ordinal) % 2).astype(jnp.int32)
    next_expert = per_block(succ).astype(jnp.int32)
    return (block_expert.astype(jnp.int32), n_used, n_valid, stage_slot, next_expert)


def _pos_embed_2d(n_tokens):
    rows = n_tokens // GRID_W
    r = jnp.repeat(jnp.arange(rows), GRID_W).astype(F32)
    col = jnp.tile(jnp.arange(GRID_W), rows).astype(F32)
    quarter = D_MODEL // 4
    omega = 1.0 / (10000.0 ** (jnp.arange(quarter, dtype=F32) / quarter))
    er = r[:, None] * omega
    ec = col[:, None] * omega
    return jnp.concatenate([jnp.sin(er), jnp.cos(er), jnp.sin(ec), jnp.cos(ec)], axis=-1)


def _split_w_in(w):
    wt = w.T
    o_r = 2 * GLA_DK + 2 * GLA_DV
    o_f = o_r + DECAY_RANK
    o_gate = o_f + FNET_DIM
    row_scale = jnp.where(jnp.arange(o_r) < GLA_DK, DK_HEAD ** -0.5, 1.0).astype(w.dtype)
    w_main = (wt[:o_r] * row_scale[:, None]).astype(BF16)
    pad = jnp.zeros((LANES - DECAY_RANK, w.shape[0]), BF16)
    w_tail = jnp.concatenate([wt[o_gate:].astype(BF16), wt[o_f:o_gate].astype(BF16),
                              wt[o_r:o_f].astype(BF16), pad], axis=0)
    return w_main, w_tail


def kernel(x_prompt, x_sample, state_gla, c, c_ctx, w_ada, b_ada, w_in, w_dec_fwd, b_dec_fwd,
           w_dec_bwd, b_dec_bwd, gla_norm_g, w_br_gla, w_br_fnet, w_out, ln1_g, ln1_b, w_router,
           b_router, w_gate_up, b_gate_up, w_down, b_down, ln2_g, ln2_b):
    n_req, ctx_len, _ = x_prompt.shape
    n_lat, lat_len, _ = x_sample.shape
    depth = w_in.shape[0]
    alpha = (2.0 * depth) ** 0.25
    t_ctx = n_req * ctx_len
    t_lat = n_lat * lat_len
    t_all = t_ctx + t_lat

    x_ctx = x_prompt.reshape(t_ctx, D_MODEL)
    x_lat = x_sample.reshape(t_lat, D_MODEL)
    pos = _pos_embed_2d(lat_len)
    zero_pos = jnp.zeros_like(pos)

    cond_rows = -(-(n_lat + 1) // SUBLANES) * SUBLANES
    cond = jnp.zeros((cond_rows, D_MODEL), F32).at[:n_lat].set(c).at[cond_rows - 1].set(c_ctx)

    n_moe_blocks = (t_all * TOP_K) // MOE_ROWS + N_EXPERTS
    tok_chunks = t_all // (SC_WORKERS * SC_WINDOW)
    states = []
    for l in range(depth):
        mod = _ada(cond, w_ada[l], b_ada[l]).reshape(cond_rows, N_MOD, D_MODEL)
        layer_pos = pos if l == 0 else zero_pos
        w_dec = jnp.zeros((2, LANES, GLA_DK), F32)
        w_dec = w_dec.at[0, :DECAY_RANK].set(w_dec_fwd[l]).at[1, :DECAY_RANK].set(w_dec_bwd[l])
        b_dec = jnp.stack([b_dec_fwd[l], b_dec_bwd[l]]).reshape(2, 1, GLA_DK)
        proj, la = _inproj(x_ctx, x_lat, layer_pos, mod, *_split_w_in(w_in[l]), w_dec, b_dec, lat_len)

        norm_g = gla_norm_g[l].reshape(1, DV_HEAD)
        o_ctx, s_new = _gla(proj, la, norm_g, None, n_seq=n_req, seq_len=ctx_len, row0=0,
                            emit_state=True)
        (o_lat,) = _gla(proj, la, norm_g, state_gla[:, l], n_seq=n_lat, seq_len=lat_len, row0=t_ctx,
                        emit_state=False)
        states.append(s_new)

        mixed_ctx = _fnet(proj, n_seq=n_req, seq_len=ctx_len, row0=0)
        mixed_lat = _fnet(proj, n_seq=n_lat, seq_len=lat_len, row0=t_ctx)

        wr = jnp.zeros((D_MODEL, LANES), F32).at[:, :N_EXPERTS].set(w_router[l])
        br = jnp.full((1, LANES), -1e30, F32).at[0, :N_EXPERTS].set(b_router[l])
        wr_hi = wr.astype(BF16)
        wr_lo = (wr - wr_hi.astype(F32)).astype(BF16)
        x1, h2, ridx, rw = _merge(
            x_ctx, x_lat, layer_pos, mod, o_ctx, o_lat, mixed_ctx, mixed_lat, proj,
            w_br_gla[l].astype(BF16), w_br_fnet[l].astype(BF16), (0.5 * w_out[l]).astype(BF16),
            ln1_g[l].reshape(1, D_MODEL), ln1_b[l].reshape(1, D_MODEL), wr_hi, wr_lo, br, lat_len,
            alpha)

        dest, counts = _route(ridx)
        moe_tables = _routing_tables(counts[0, :N_EXPERTS].astype(jnp.int32), n_moe_blocks)
        dest = dest[:TOP_K]
        scatter_idx = dest.reshape(TOP_K, SC_WORKERS, tok_chunks, SC_WINDOW)
        p_rows = n_moe_blocks * MOE_ROWS
        xs = _sc_scatter_rows(h2.reshape(t_all, ROW_TILES, LANES), scatter_idx, p_rows)
        yb = _moe(moe_tables, xs.reshape(p_rows * ROW_TILES, LANES),
                  w_gate_up[l], b_gate_up[l], w_down[l], b_down[l])
        yb = yb.reshape(p_rows, ROW_TILES, LANES)

        def gathered(row0, n_rows):
            idx = dest[:, row0:row0 + n_rows].reshape(SC_WORKERS, -1, SC_WINDOW)
            return _sc_gather_rows(yb, idx).reshape(TOP_K * n_rows * ROW_TILES, LANES)

        l2g = ln2_g[l].reshape(1, D_MODEL)
        l2b = ln2_b[l].reshape(1, D_MODEL)
        tiles_per_seq = lat_len // TM_MIX
        yg_ctx = gathered(0, t_ctx)
        yg_lat = gathered(t_ctx, t_lat)
        x_ctx = _combine(x1, yg_ctx, rw, mod, l2g, l2b, row0=0, n_rows=t_ctx,
                         mod_map=lambda i: (cond_rows - 1, 0, 0), alpha=alpha)
        x_lat = _combine(x1, yg_lat, rw, mod, l2g, l2b, row0=t_ctx, n_rows=t_lat,
                         mod_map=lambda i: (i // tiles_per_seq, 0, 0), alpha=alpha)

    y_prompt = x_ctx.reshape(x_prompt.shape)
    y_sample = x_lat.reshape(x_sample.shape)
    new_state = jnp.stack(states, axis=1).astype(x_prompt.dtype)
    return (y_prompt, y_sample, new_state)
```

```python
import functools
import math

import numpy as np
import jax
import jax.numpy as jnp
from jax import lax
from jax.experimental import pallas as pl
from jax.experimental.pallas import tpu as pltpu
from jax.experimental.pallas import tpu_sc as plsc

F32 = jnp.float32
BF16 = jnp.bfloat16

D_MODEL = 1024
GRID_W = 64
GLA_HEADS = 4
DK_HEAD = 128
DV_HEAD = 256
GLA_DK = GLA_HEADS * DK_HEAD
GLA_DV = GLA_HEADS * DV_HEAD
DECAY_RANK = 16
GATE_NORMALIZER = 16.0
FNET_GROUPS = 4
FNET_GROUP_DIM = 128
FNET_DIM = FNET_GROUPS * FNET_GROUP_DIM
N_EXPERTS = 32
TOP_K = 4
D_EXPERT = 1024
SWIGLU_LIMIT = 7.0
SWIGLU_ALPHA = 1.702
LN_EPS = 1e-6
N_MOD = 6

LANES = 128
SUBLANES = 8
HALF_MODEL = D_MODEL // 2
ROW_TILES = HALF_MODEL // LANES
ROW_DTYPE = jnp.uint32
COL_Q = 0
COL_K = GLA_DK
COL_V = 2 * GLA_DK
COL_G = COL_V + GLA_DV
COL_GATE_A = COL_G + GLA_DV
COL_GATE_B = COL_GATE_A + D_MODEL
COL_F = COL_GATE_B + D_MODEL
COL_R = COL_F + FNET_DIM
PROJ_COLS = COL_R + LANES

GLA_CHUNK = 128
GLA_LEAF = 16
assert GLA_CHUNK == DK_HEAD
TM_PROJ = 512
PROJ_GROUP = 256
TM_MIX = 256
TM_MERGE = 512
TM_MERGE_GROUP = 128
TM_ROUTE = 1024
MOE_ROWS = 512
MOE_SHORT = 256
VMEM_LIMIT = 56 * 1024 * 1024

SC_CORES = 2
SC_SUBCORES = 16
SC_WORKERS = SC_CORES * SC_SUBCORES
SC_WINDOW = 64

HIGHEST = lax.Precision.HIGHEST


def _layer_norm(x):
    mu = jnp.mean(x, axis=-1, keepdims=True)
    xc = x - mu
    var = jnp.mean(xc * xc, axis=-1, keepdims=True)
    return xc * lax.rsqrt(var + LN_EPS)


def _sigmoid(x):
    return 0.5 * jnp.tanh(0.5 * x) + 0.5


def _log_sigmoid(z):
    return jnp.minimum(z, 0.0) - jnp.log(1.0 + jnp.exp(-jnp.abs(z)))


def _dot(a, b):
    return jnp.dot(a, b, preferred_element_type=F32)


def _split_bf16(x, terms):
    parts = []
    for _ in range(terms):
        p = x.astype(BF16)
        parts.append(p)
        x = x - p.astype(F32)
    return parts


def _dot_nt(a, b):
    return lax.dot_general(a, b, (((1,), (1,)), ((), ())), preferred_element_type=F32)


def _dot_tn(a, b):
    return lax.dot_general(a, b, (((0,), (0,)), ((), ())), preferred_element_type=F32)


def _row_tile_slice(j, n_rows, first_row=0):
    return pl.ds(first_row * ROW_TILES + j, n_rows, stride=ROW_TILES)


def _store_row_tiles(ref, val, first_row=0):
    for j in range(ROW_TILES):
        lo = val[:, j * LANES:(j + 1) * LANES]
        hi = val[:, HALF_MODEL + j * LANES:HALF_MODEL + (j + 1) * LANES]
        ref[_row_tile_slice(j, val.shape[0], first_row), :] = pltpu.pack_elementwise(
            [lo, hi], packed_dtype=BF16)


def _load_row_tile(ref, j, n_rows, first_row=0):
    words = ref[_row_tile_slice(j, n_rows, first_row), :]
    return tuple(pltpu.unpack_elementwise(words, index=half, packed_dtype=BF16, unpacked_dtype=F32)
                 for half in range(2))


def _ada_kernel(c_ref, w_ref, b_ref, o_ref):
    c = c_ref[...]
    s = c * _sigmoid(c)
    o_ref[...] = _dot(s.astype(BF16), w_ref[...].astype(BF16)) + b_ref[...]


def _ada(cond, w_ada, b_ada):
    rows = cond.shape[0]
    n = w_ada.shape[1]
    tn = 1536
    return pl.pallas_call(
        _ada_kernel,
        grid=(n // tn,),
        in_specs=[pl.BlockSpec((rows, D_MODEL), lambda j: (0, 0)),
                  pl.BlockSpec((D_MODEL, tn), lambda j: (0, j)),
                  pl.BlockSpec((1, tn), lambda j: (0, j))],
        out_specs=pl.BlockSpec((rows, tn), lambda j: (0, j)),
        out_shape=jax.ShapeDtypeStruct((rows, n), F32),
        compiler_params=pltpu.CompilerParams(vmem_limit_bytes=VMEM_LIMIT),
        name="ada_mod",
    )(cond, w_ada, b_ada.reshape(1, n))


def _group_maps(n_ctx_tiles):
    def ctx_map(i, *_):
        return (jnp.minimum(i, n_ctx_tiles - 1), 0)

    def lat_map(i, *_):
        return (jnp.maximum(i - n_ctx_tiles, 0), 0)

    return ctx_map, lat_map


def _token_specs(tm, n_ctx_tiles, tiles_per_latent_seq, ctx_mod_row):
    ctx_map, lat_map = _group_maps(n_ctx_tiles)

    def pos_map(i, *_):
        return (jnp.maximum(i - n_ctx_tiles, 0) % tiles_per_latent_seq, 0)

    def mod_map(i, *_):
        return (jnp.where(i < n_ctx_tiles, ctx_mod_row,
                          jnp.maximum(i - n_ctx_tiles, 0) // tiles_per_latent_seq), 0, 0)

    return [pl.BlockSpec((tm, D_MODEL), ctx_map),
            pl.BlockSpec((tm, D_MODEL), lat_map),
            pl.BlockSpec((tm, D_MODEL), pos_map),
            pl.BlockSpec((1, N_MOD, D_MODEL), mod_map)]


def _inproj_kernel(xc_ref, xl_ref, pos_ref, mod_ref, wm_ref, wt_ref, wdh_ref, wdl_ref, bd_ref,
                   o_ref, la_ref, *, n_ctx_tiles):
    tail = lambda lo, hi: wt_ref[lo - COL_GATE_A:hi - COL_GATE_A, :]
    i = pl.program_id(0)

    def project(x_of):
        for r0 in range(0, TM_PROJ, PROJ_GROUP):
            project_rows(slice(r0, r0 + PROJ_GROUP), x_of(slice(r0, r0 + PROJ_GROUP)))

    def project_rows(rows, x):
        h = (_layer_norm(x) * (1.0 + mod_ref[0, 1:2, :]) + mod_ref[0, 0:1, :]).astype(BF16)
        r_hi, r_lo = _split_bf16(_dot_nt(h, tail(COL_R, PROJ_COLS)), 2)
        for d in range(2):
            z = (_dot(r_hi, wdh_ref[d]) + _dot(r_lo, wdh_ref[d]) + _dot(r_hi, wdl_ref[d])
                 + bd_ref[d])
            la_ref[rows, d * GLA_DK:(d + 1) * GLA_DK] = _log_sigmoid(z) * (1.0 / GATE_NORMALIZER)
        hg = 0.5 * _dot_nt(h, wm_ref[COL_G:COL_GATE_A, :])
        o_ref[rows, COL_G:COL_GATE_A] = hg * (jnp.tanh(hg) + 1.0)
        gates = _dot_nt(h, tail(COL_GATE_A, COL_F))
        o_ref[rows, COL_GATE_A:COL_F] = jnp.tanh(0.5 * gates) + 1.0
        o_ref[rows, :COL_G] = _dot_nt(h, wm_ref[:COL_G, :])
        o_ref[rows, COL_F:COL_R] = _dot_nt(h, tail(COL_F, COL_R))

    @pl.when(i < n_ctx_tiles)
    def _():
        project(lambda rows: xc_ref[rows, :])

    @pl.when(i >= n_ctx_tiles)
    def _():
        project(lambda rows: xl_ref[rows, :] + pos_ref[rows, :])


def _inproj(x_ctx, x_lat, pos, mod, w_main, w_tail, w_dec, b_dec, lat_len):
    t_ctx, t_lat = x_ctx.shape[0], x_lat.shape[0]
    t_all = t_ctx + t_lat
    n_ctx_tiles = t_ctx // TM_PROJ
    kern = functools.partial(_inproj_kernel, n_ctx_tiles=n_ctx_tiles)
    specs = _token_specs(TM_PROJ, n_ctx_tiles, lat_len // TM_PROJ, mod.shape[0] - 1)
    w_dec_hi = w_dec.astype(BF16)
    w_dec_lo = (w_dec - w_dec_hi.astype(F32)).astype(BF16)

    def const(shape):
        return pl.BlockSpec(shape, lambda i: (0,) * len(shape))

    return pl.pallas_call(
        kern,
        grid=(t_all // TM_PROJ,),
        in_specs=specs + [pl.BlockSpec(w_main.shape, lambda i: (0, 0), pipeline_mode=pl.Buffered(1)),
                          pl.BlockSpec(w_tail.shape, lambda i: (0, 0), pipeline_mode=pl.Buffered(1)),
                          const((2, LANES, GLA_DK)), const((2, LANES, GLA_DK)), const((2, 1, GLA_DK))],
        out_specs=[pl.BlockSpec((TM_PROJ, COL_R), lambda i: (i, 0)),
                   pl.BlockSpec((TM_PROJ, 2 * GLA_DK), lambda i: (i, 0))],
        out_shape=[jax.ShapeDtypeStruct((t_all, COL_R), F32),
                   jax.ShapeDtypeStruct((t_all, 2 * GLA_DK), F32)],
        compiler_params=pltpu.CompilerParams(
            dimension_semantics=("arbitrary",), vmem_limit_bytes=VMEM_LIMIT),
        name="ln_inproj",
    )(x_ctx, x_lat, pos, mod, w_main, w_tail, w_dec_hi, w_dec_lo, b_dec)


def _gla_kernel(*refs, seq_len, has_s0, emit_state):
    it = iter(refs)
    q_ref, k_ref, v_ref, laf_ref, lab_ref, g_ref = (next(it) for _ in range(6))
    s0_ref = next(it) if has_s0 else None
    o_ref = next(it)
    sout_ref = next(it) if emit_state else None
    cum_ref, aq_ref, ko_ref, dec_ref, op_ref, st_ref = (next(it) for _ in range(6))

    C = GLA_CHUNK
    n_chunks = seq_len // C

    def rows(n):
        if isinstance(n, int):
            return pl.ds(n * C, C)
        return pl.ds(pl.multiple_of(n * C, C), C)

    def loop(body):
        if n_chunks <= 2:
            for n in range(n_chunks):
                body(n)
        else:
            def step(m, carry):
                body(2 * m)
                body(2 * m + 1)
                return carry
            lax.fori_loop(0, n_chunks // 2, step, 0)

    rt = lax.broadcasted_iota(jnp.int32, (C, C), 0)
    ct = lax.broadcasted_iota(jnp.int32, (C, C), 1)
    tri = ((rt >= ct).astype(BF16), (ct >= rt).astype(BF16))
    row_id = lax.broadcasted_iota(jnp.int32, (C, DK_HEAD), 0)

    def cumsum_chunk(n):
        for d, la_ref in enumerate((laf_ref, lab_ref)):
            la_hi, la_lo = _split_bf16(la_ref[rows(n), :], 2)
            cum_ref[d, rows(n), :] = _dot(tri[d], la_hi) + _dot(tri[d], la_lo)

    loop(cumsum_chunk)

    query_rows = ({}, {})
    keep = ({}, {})
    for d in range(2):
        blk = C // 2
        while blk >= GLA_LEAF:
            q_parity = 1 if d == 0 else 0
            query_rows[d][blk] = ((row_id // blk) % 2) == q_parity
            qb, kb = rt // blk, ct // blk
            keep[d][blk] = ((qb % 2) == q_parity) & ((qb == kb + 1) if d == 0 else (kb == qb + 1))
            blk //= 2
        order = (rt >= ct) if d == 0 else (ct >= rt)
        keep[d][0] = ((rt // GLA_LEAF) == (ct // GLA_LEAF)) & order

    n_leaves = C // GLA_LEAF

    def per_leaf(rows_of_cum):
        return jnp.concatenate(rows_of_cum, axis=0)

    def expand(per_leaf_rows):
        return jnp.concatenate(
            [jnp.broadcast_to(per_leaf_rows[j:j + 1, :], (GLA_LEAF, DK_HEAD)) for j in range(n_leaves)],
            axis=0)

    def scores(n, d):
        cum = cum_ref[d, rows(n), :]
        mid = GLA_LEAF // 2 - 1 if d == 0 else GLA_LEAF // 2
        at_mid = per_leaf([cum[mid + j * GLA_LEAF:mid + j * GLA_LEAF + 1, :] for j in range(n_leaves)])
        e = cum - expand(at_mid)
        qe = q_ref[rows(n), :] * jnp.exp(e)
        ke = k_ref[rows(n), :] * jnp.exp(-e)
        acc = jnp.where(keep[d][0], _dot_nt(qe.astype(BF16), ke.astype(BF16)), 0.0)
        blk = C // 2
        while blk >= GLA_LEAF:
            bnd = blk - 1 if d == 0 else blk
            at_bnd = per_leaf([cum[bnd + (j * GLA_LEAF) // (2 * blk) * (2 * blk):
                                   bnd + (j * GLA_LEAF) // (2 * blk) * (2 * blk) + 1, :]
                               for j in range(n_leaves)])
            through = expand(jnp.exp(-jnp.abs(at_mid - at_bnd)))
            x = (jnp.where(query_rows[d][blk], qe, ke) * through).astype(BF16)
            acc = acc + jnp.where(keep[d][blk], _dot_nt(x, x), 0.0)
            blk //= 2
        aq_ref[d, rows(n), 0:C] = acc.astype(BF16)
        aq_ref[d, rows(n), C:C + DK_HEAD] = (qe * expand(jnp.exp(at_mid))).astype(BF16)
        end = cum[C - 1:C, :] if d == 0 else cum[0:1, :]
        ko_ref[d, rows(n), :] = (ke * expand(jnp.exp(end - at_mid))).astype(BF16)
        dec_ref[d, rows(n), :] = jnp.broadcast_to(jnp.exp(end), (DK_HEAD, DK_HEAD)).T

    def scores_chunk(n):
        scores(n, 0)
        scores(n, 1)

    loop(scores_chunk)

    for d in range(2):
        if has_s0:
            st_ref[d] = s0_ref[0, d, 0]
        else:
            st_ref[d] = jnp.zeros((DK_HEAD, DV_HEAD), F32)

    def scan(n, d):
        v = v_ref[rows(n), :].astype(BF16)
        st = st_ref[d]
        op_ref[d, rows(n), :] = _dot(aq_ref[d, rows(n), :],
                                     jnp.concatenate([v, st.astype(BF16)], axis=0))
        dec = dec_ref[d, rows(n), :]
        st_ref[d] = (st * jnp.concatenate([dec] * (DV_HEAD // DK_HEAD), axis=1)
                     + _dot_tn(ko_ref[d, rows(n), :], v))

    def scan_chunk(m):
        scan(m, 0)
        scan(n_chunks - 1 - m, 1)

    loop(scan_chunk)
    if emit_state:
        sout_ref[0, 0, 0] = st_ref[0]
        sout_ref[0, 1, 0] = st_ref[1]

    g = g_ref[...]

    def finish_chunk(n):
        o = op_ref[0, rows(n), :] + op_ref[1, rows(n), :]
        ms = jnp.mean(o * o, axis=-1, keepdims=True)
        o_ref[rows(n), :] = o * lax.rsqrt(ms + LN_EPS) * g

    loop(finish_chunk)


def _gla(proj, la, g, s0, *, n_seq, seq_len, row0, emit_state):
    has_s0 = s0 is not None
    blk0 = row0 // seq_len
    kern = functools.partial(_gla_kernel, seq_len=seq_len, has_s0=has_s0, emit_state=emit_state)
    in_specs = [
        pl.BlockSpec((seq_len, DK_HEAD), lambda b, h: (blk0 + b, COL_Q // DK_HEAD + h)),
        pl.BlockSpec((seq_len, DK_HEAD), lambda b, h: (blk0 + b, COL_K // DK_HEAD + h)),
        pl.BlockSpec((seq_len, DV_HEAD), lambda b, h: (blk0 + b, COL_V // DV_HEAD + h)),
        pl.BlockSpec((seq_len, DK_HEAD), lambda b, h: (blk0 + b, h)),
        pl.BlockSpec((seq_len, DK_HEAD), lambda b, h: (blk0 + b, GLA_HEADS + h)),
        pl.BlockSpec((1, DV_HEAD), lambda b, h: (0, 0)),
    ]
    args = [proj, proj, proj, la, la, g]
    if has_s0:
        in_specs.append(pl.BlockSpec((1, 2, 1, DK_HEAD, DV_HEAD), lambda b, h: (b, 0, h, 0, 0)))
        args.append(s0)
    out_specs = [pl.BlockSpec((seq_len, DV_HEAD), lambda b, h: (b, h))]
    out_shape = [jax.ShapeDtypeStruct((n_seq * seq_len, GLA_DV), F32)]
    if emit_state:
        out_specs.append(pl.BlockSpec((1, 2, 1, DK_HEAD, DV_HEAD), lambda b, h: (b, 0, h, 0, 0)))
        out_shape.append(jax.ShapeDtypeStruct((n_seq, 2, GLA_HEADS, DK_HEAD, DV_HEAD), F32))

    res = pl.pallas_call(
        kern,
        grid=(n_seq, GLA_HEADS),
        in_specs=in_specs,
        out_specs=out_specs,
        out_shape=out_shape,
        scratch_shapes=[pltpu.VMEM((2, seq_len, DK_HEAD), F32),
                        pltpu.VMEM((2, seq_len, GLA_CHUNK + DK_HEAD), BF16),
                        pltpu.VMEM((2, seq_len, DK_HEAD), BF16),
                        pltpu.VMEM((2, seq_len, DK_HEAD), F32),
                        pltpu.VMEM((2, seq_len, DV_HEAD), F32),
                        pltpu.VMEM((2, DK_HEAD, DV_HEAD), F32)],
        compiler_params=pltpu.CompilerParams(
            dimension_semantics=("arbitrary", "arbitrary"), vmem_limit_bytes=VMEM_LIMIT),
        name="gla_seq%d" % seq_len,
    )(*args)
    return res


def _fnet_kernel(f_ref, cl_ref, sl_ref, cg_ref, sg_ref, o_ref, uc_ref, us_ref, *, seq_len):
    cg = cg_ref[...]
    sg = sg_ref[...]
    for grp in range(FNET_GROUPS):
        lo = grp * FNET_GROUP_DIM
        u = f_ref[:, lo:lo + FNET_GROUP_DIM].astype(BF16)
        uc_ref[:, lo:lo + FNET_GROUP_DIM] = _dot(u, cg).astype(BF16)
        us_ref[:, lo:lo + FNET_GROUP_DIM] = _dot(u, sg).astype(BF16)
    mixed = _dot(cl_ref[...], uc_ref[...]) - _dot(sl_ref[...], us_ref[...])
    o_ref[...] = mixed * (1.0 / math.sqrt(seq_len * FNET_GROUP_DIM))


def _dft_mats(n):
    j = np.arange(n, dtype=np.int64)
    ang = (2.0 * np.pi / n) * ((j[:, None] * j[None, :]) % n).astype(np.float64)
    return (jnp.asarray(np.cos(ang), dtype=F32).astype(BF16),
            jnp.asarray(np.sin(ang), dtype=F32).astype(BF16))


def _fnet(proj, *, n_seq, seq_len, row0):
    blk0 = row0 // seq_len
    cl, sl = _dft_mats(seq_len)
    cg, sg = _dft_mats(FNET_GROUP_DIM)
    kern = functools.partial(_fnet_kernel, seq_len=seq_len)
    return pl.pallas_call(
        kern,
        grid=(n_seq,),
        in_specs=[pl.BlockSpec((seq_len, FNET_DIM), lambda b: (blk0 + b, COL_F // FNET_DIM)),
                  pl.BlockSpec((seq_len, seq_len), lambda b: (0, 0)),
                  pl.BlockSpec((seq_len, seq_len), lambda b: (0, 0)),
                  pl.BlockSpec((FNET_GROUP_DIM, FNET_GROUP_DIM), lambda b: (0, 0)),
                  pl.BlockSpec((FNET_GROUP_DIM, FNET_GROUP_DIM), lambda b: (0, 0))],
        out_specs=pl.BlockSpec((seq_len, FNET_DIM), lambda b: (b, 0)),
        out_shape=jax.ShapeDtypeStruct((n_seq * seq_len, FNET_DIM), F32),
        scratch_shapes=[pltpu.VMEM((seq_len, FNET_DIM), BF16),
                        pltpu.VMEM((seq_len, FNET_DIM), BF16)],
        compiler_params=pltpu.CompilerParams(
            dimension_semantics=("arbitrary",), vmem_limit_bytes=VMEM_LIMIT),
        name="fnet_seq%d" % seq_len,
    )(proj, cl, sl, cg, sg)


def _merge_kernel(xc_ref, xl_ref, pos_ref, mod_ref, oc_ref, ol_ref, mc_ref, ml_ref,
                  g_ref, ga_ref, gb_ref, wbg_ref, wbf_ref, wo_ref, l1g_ref, l1b_ref, wrh_ref, wrl_ref,
                  br_ref, x1_ref, h2_ref, ridx_ref, rw_ref, *, n_ctx_tiles, alpha):
    i = pl.program_id(0)

    def compute(rows, x, o, mx):
        a = (o * g_ref[rows, :]).astype(BF16)
        gla_out = _dot(a, wbg_ref[...])
        fnet_out = _dot(mx.astype(BF16), wbf_ref[...])
        merged = ga_ref[rows, :] * gla_out + gb_ref[rows, :] * fnet_out
        mix = _dot(merged.astype(BF16), wo_ref[...])
        y = alpha * x + mod_ref[0, 2:3, :] * mix
        x1 = _layer_norm(y) * l1g_ref[...] + l1b_ref[...]
        x1_ref[rows, :] = x1
        h2 = _layer_norm(x1) * (1.0 + mod_ref[0, 4:5, :]) + mod_ref[0, 3:4, :]
        _store_row_tiles(h2_ref, h2, rows.start)

        h_hi, h_lo = _split_bf16(h2, 2)
        logits = (_dot(h_hi, wrh_ref[...]) + _dot(h_lo, wrh_ref[...]) + _dot(h_hi, wrl_ref[...])
                  + br_ref[...])
        lane_i = lax.broadcasted_iota(jnp.int32, logits.shape, 1)
        lane = lane_i.astype(F32)
        idx_out = jnp.zeros(logits.shape, F32)
        val_out = jnp.zeros(logits.shape, F32)
        top0 = None
        denom = None
        for kk in range(TOP_K):
            m = jnp.max(logits, axis=-1, keepdims=True)
            sel = jnp.min(jnp.where(logits == m, lane, float(LANES)), axis=-1, keepdims=True)
            if kk == 0:
                top0 = m
                p = jnp.ones_like(m)
                denom = p
            else:
                p = jnp.exp(m - top0)
                denom = denom + p
            idx_out = jnp.where(lane_i == kk, sel, idx_out)
            val_out = jnp.where(lane_i == kk, p, val_out)
            logits = jnp.where(lane == sel, -jnp.inf, logits)
        ridx_ref[rows, :] = idx_out.astype(jnp.int32)
        rw_ref[rows, :] = val_out / denom

    tm = x1_ref.shape[0]
    groups = [slice(r0, r0 + TM_MERGE_GROUP) for r0 in range(0, tm, TM_MERGE_GROUP)]

    @pl.when(i < n_ctx_tiles)
    def _():
        for rows in groups:
            compute(rows, xc_ref[rows, :], oc_ref[rows, :], mc_ref[rows, :])

    @pl.when(i >= n_ctx_tiles)
    def _():
        for rows in groups:
            compute(rows, xl_ref[rows, :] + pos_ref[rows, :], ol_ref[rows, :], ml_ref[rows, :])


def _merge(x_ctx, x_lat, pos, mod, o_ctx, o_lat, mixed_ctx, mixed_lat, proj,
           wbg, wbf, wo, l1g, l1b, wr_hi, wr_lo, br, lat_len, alpha):
    t_ctx, t_lat = x_ctx.shape[0], x_lat.shape[0]
    t_all = t_ctx + t_lat
    tm = TM_MERGE
    n_ctx_tiles = t_ctx // tm
    kern = functools.partial(_merge_kernel, n_ctx_tiles=n_ctx_tiles, alpha=alpha)
    specs = _token_specs(tm, n_ctx_tiles, lat_len // tm, mod.shape[0] - 1)
    ctx_map, lat_map = _group_maps(n_ctx_tiles)

    def const(shape):
        return pl.BlockSpec(shape, lambda i: (0,) * len(shape))

    in_specs = specs + [
        pl.BlockSpec((tm, GLA_DV), ctx_map),
        pl.BlockSpec((tm, GLA_DV), lat_map),
        pl.BlockSpec((tm, FNET_DIM), ctx_map),
        pl.BlockSpec((tm, FNET_DIM), lat_map),
        pl.BlockSpec((tm, GLA_DV), lambda i: (i, COL_G // GLA_DV)),
        pl.BlockSpec((tm, D_MODEL), lambda i: (i, COL_GATE_A // D_MODEL)),
        pl.BlockSpec((tm, D_MODEL), lambda i: (i, COL_GATE_B // D_MODEL)),
        const((GLA_DV, D_MODEL)), const((FNET_DIM, D_MODEL)), const((D_MODEL, D_MODEL)),
        const((1, D_MODEL)), const((1, D_MODEL)),
        const((D_MODEL, LANES)), const((D_MODEL, LANES)), const((1, LANES)),
    ]
    out_specs = [pl.BlockSpec((tm, D_MODEL), lambda i: (i, 0)),
                 pl.BlockSpec((tm * ROW_TILES, LANES), lambda i: (i, 0)),
                 pl.BlockSpec((tm, LANES), lambda i: (i, 0)),
                 pl.BlockSpec((tm, LANES), lambda i: (i, 0))]
    out_shape = [jax.ShapeDtypeStruct((t_all, D_MODEL), F32),
                 jax.ShapeDtypeStruct((t_all * ROW_TILES, LANES), ROW_DTYPE),
                 jax.ShapeDtypeStruct((t_all, LANES), jnp.int32),
                 jax.ShapeDtypeStruct((t_all, LANES), F32)]
    return pl.pallas_call(
        kern,
        grid=(t_all // tm,),
        in_specs=in_specs,
        out_specs=out_specs,
        out_shape=out_shape,
        compiler_params=pltpu.CompilerParams(
            dimension_semantics=("arbitrary",), vmem_limit_bytes=VMEM_LIMIT),
        name="merge_ln1_router",
    )(x_ctx, x_lat, pos, mod, o_ctx, o_lat, mixed_ctx, mixed_lat, proj, proj, proj,
      wbg, wbf, wo, l1g, l1b, wr_hi, wr_lo, br)


def _sc_mesh():
    return plsc.VectorSubcoreMesh(core_axis_name="c", subcore_axis_name="s")


def _sc_worker_id():
    return lax.axis_index("s") * SC_CORES + lax.axis_index("c")


def _sc_scatter_rows(src, idx, n_out):
    n_src = src.shape[0]
    w = SC_WINDOW
    n_chunks = n_src // (SC_WORKERS * w)
    copies = idx.shape[0]
    assert n_chunks % 2 == 0 and idx.shape == (copies, SC_WORKERS, n_chunks, w)

    @functools.partial(
        pl.kernel, mesh=_sc_mesh(),
        out_type=jax.ShapeDtypeStruct((n_out, ROW_TILES, LANES), ROW_DTYPE),
        scratch_types=[pltpu.VMEM((copies * n_chunks, w), jnp.int32),
                       pltpu.VMEM((2, w, ROW_TILES, LANES), ROW_DTYPE),
                       pltpu.SemaphoreType.DMA((2,)),
                       pltpu.SemaphoreType.DMA((2,))],
        name="moe_dispatch_scatter")
    def k(src_hbm, idx_hbm, out_hbm, idx_v, rows_v, rsem, wsem):
        wid = _sc_worker_id()
        base = wid * (n_chunks * w)
        for kk in range(copies):
            pltpu.sync_copy(idx_hbm.at[kk, wid], idx_v.at[pl.ds(kk * n_chunks, n_chunks)])

        def read(j, slot):
            return pltpu.make_async_copy(src_hbm.at[pl.ds(base + j * w, w)], rows_v.at[slot],
                                         rsem.at[slot])

        def scatter(j, kk, slot):
            return pltpu.make_async_copy(rows_v.at[slot], out_hbm.at[idx_v.at[kk * n_chunks + j]],
                                         wsem.at[slot])

        read(0, 0).start()

        @pl.loop(0, n_chunks, step=2)
        def _(jj):
            read(jj, 0).wait()

            @pl.when(jj > 0)
            def _():
                for kk in range(copies):
                    scatter(jj - 1, kk, 1).wait()

            read(jj + 1, 1).start()
            for kk in range(copies):
                scatter(jj, kk, 0).start()
            read(jj + 1, 1).wait()
            for kk in range(copies):
                scatter(jj, kk, 0).wait()

            @pl.when(jj + 2 < n_chunks)
            def _():
                read(jj + 2, 0).start()

            for kk in range(copies):
                scatter(jj + 1, kk, 1).start()

        for kk in range(copies):
            scatter(n_chunks - 1, kk, 1).wait()

    return k(src, idx)


def _sc_gather_rows(table, idx):
    _, n_chunks, w = idx.shape
    assert n_chunks % 2 == 0 and idx.shape[0] == SC_WORKERS and w == SC_WINDOW
    n_out = SC_WORKERS * n_chunks * w

    @functools.partial(
        pl.kernel, mesh=_sc_mesh(),
        out_type=jax.ShapeDtypeStruct((n_out, ROW_TILES, LANES), ROW_DTYPE),
        scratch_types=[pltpu.VMEM((n_chunks, w), jnp.int32),
                       pltpu.VMEM((2, w, ROW_TILES, LANES), ROW_DTYPE),
                       pltpu.SemaphoreType.DMA((2,)),
                       pltpu.SemaphoreType.DMA((2,))],
        name="moe_combine_gather")
    def k(table_hbm, idx_hbm, out_hbm, idx_v, rows_v, gsem, wsem):
        wid = _sc_worker_id()
        base = wid * (n_chunks * w)
        pltpu.sync_copy(idx_hbm.at[wid], idx_v)

        def gather(j, slot):
            return pltpu.make_async_copy(table_hbm.at[idx_v.at[j]], rows_v.at[slot], gsem.at[slot])

        def write(j, slot):
            return pltpu.make_async_copy(rows_v.at[slot], out_hbm.at[pl.ds(base + j * w, w)],
                                         wsem.at[slot])

        gather(0, 0).start()

        @pl.loop(0, n_chunks, step=2)
        def _(jj):
            gather(jj, 0).wait()

            @pl.when(jj > 0)
            def _():
                write(jj - 1, 1).wait()

            gather(jj + 1, 1).start()
            write(jj, 0).start()
            gather(jj + 1, 1).wait()
            write(jj, 0).wait()

            @pl.when(jj + 2 < n_chunks)
            def _():
                gather(jj + 2, 0).start()

            write(jj + 1, 1).start()

        write(n_chunks - 1, 1).wait()

    return k(table, idx)


def _moe_kernel(be_ref, nu_ref, nv_ref, slot_ref, nxt_ref, x_ref, wgu_hbm, bgu_ref, wd_hbm, bd_ref,
                o_ref, wgu_st, wd_st, wgu_bf, wd_bf, xb_ref, sem):
    b = pl.program_id(0)
    e = be_ref[b]
    prev = be_ref[jnp.maximum(b - 1, 0)]
    active = b < nu_ref[0]
    changed = (b == 0) | (e != prev)

    def weight_copies(expert, s):
        return (pltpu.make_async_copy(wgu_hbm.at[expert], wgu_st.at[s], sem.at[0, s]),
                pltpu.make_async_copy(wd_hbm.at[expert], wd_st.at[s], sem.at[1, s]))

    @pl.when(active & changed)
    def _():
        s = slot_ref[b]

        @pl.when(b == 0)
        def _():
            for cp in weight_copies(e, s):
                cp.start()

        for cp in weight_copies(e, s):
            cp.wait()
        wgu_bf[...] = wgu_st[s].astype(BF16)
        wd_bf[...] = wd_st[s].astype(BF16)
        nxt = nxt_ref[b]

        @pl.when(nxt >= 0)
        def _():
            for cp in weight_copies(nxt, 1 - s):
                cp.start()

    n_valid = nv_ref[b]

    def expert_mlp(n_rows):
        valid = lax.broadcasted_iota(jnp.int32, (n_rows, LANES), 0) < n_valid
        for j in range(ROW_TILES):
            for half, xj in enumerate(_load_row_tile(x_ref, j, n_rows)):
                c0 = half * HALF_MODEL + j * LANES
                xb_ref[0:n_rows, c0:c0 + LANES] = jnp.where(valid, xj, 0.0).astype(BF16)
        gu = _dot(xb_ref[0:n_rows, :], wgu_bf[...]) + bgu_ref[0]
        gate = jnp.minimum(gu[:, :D_EXPERT], SWIGLU_LIMIT)
        up = jnp.clip(gu[:, D_EXPERT:], -SWIGLU_LIMIT, SWIGLU_LIMIT)
        glu = gate * _sigmoid(gate * SWIGLU_ALPHA)
        act = ((up + 1.0) * glu).astype(BF16)
        _store_row_tiles(o_ref, _dot(act, wd_bf[...]) + bd_ref[0])

    @pl.when(active & (n_valid > MOE_SHORT))
    def _():
        expert_mlp(MOE_ROWS)

    @pl.when(active & (n_valid <= MOE_SHORT))
    def _():
        expert_mlp(MOE_SHORT)


def _moe(tables, xs, w_gate_up, b_gate_up, w_down, b_down):
    p_rows = xs.shape[0] // ROW_TILES
    n_blocks = p_rows // MOE_ROWS

    def blk(b, be, nu, *_):
        return jnp.minimum(b, nu[0] - 1)

    def expert(b, be, nu, *_):
        return (be[blk(b, be, nu)], 0, 0)

    def rows(b, be, nu, *_):
        return (blk(b, be, nu), 0)

    grid_spec = pltpu.PrefetchScalarGridSpec(
        num_scalar_prefetch=len(tables),
        grid=(n_blocks,),
        in_specs=[
            pl.BlockSpec((MOE_ROWS * ROW_TILES, LANES), rows),
            pl.BlockSpec(memory_space=pl.ANY),
            pl.BlockSpec((1, 1, 2 * D_EXPERT), expert),
            pl.BlockSpec(memory_space=pl.ANY),
            pl.BlockSpec((1, 1, D_MODEL), expert),
        ],
        out_specs=pl.BlockSpec((MOE_ROWS * ROW_TILES, LANES), rows),
        scratch_shapes=[pltpu.VMEM((2, D_MODEL, 2 * D_EXPERT), F32),
                        pltpu.VMEM((2, D_EXPERT, D_MODEL), F32),
                        pltpu.VMEM((D_MODEL, 2 * D_EXPERT), BF16),
                        pltpu.VMEM((D_EXPERT, D_MODEL), BF16),
                        pltpu.VMEM((MOE_ROWS, D_MODEL), BF16),
                        pltpu.SemaphoreType.DMA((2, 2))],
    )
    return pl.pallas_call(
        _moe_kernel,
        grid_spec=grid_spec,
        out_shape=jax.ShapeDtypeStruct((p_rows * ROW_TILES, LANES), ROW_DTYPE),
        compiler_params=pltpu.CompilerParams(
            dimension_semantics=("arbitrary",), vmem_limit_bytes=VMEM_LIMIT),
        name="moe_grouped_mlp",
    )(*tables, xs, w_gate_up, b_gate_up.reshape(N_EXPERTS, 1, 2 * D_EXPERT), w_down,
      b_down.reshape(N_EXPERTS, 1, D_MODEL))


def _combine_kernel(x1_ref, y0_ref, y1_ref, y2_ref, y3_ref, rw_ref, mod_ref, g_ref, b_ref, o_ref,
                    *, alpha):
    rw = rw_ref[...]
    y_refs = (y0_ref, y1_ref, y2_ref, y3_ref)
    pieces = [None] * (2 * ROW_TILES)
    for j in range(ROW_TILES):
        for kk in range(TOP_K):
            for half, yj in enumerate(_load_row_tile(y_refs[kk], j, rw.shape[0])):
                term = rw[:, kk:kk + 1] * yj
                slot = half * ROW_TILES + j
                pieces[slot] = term if kk == 0 else pieces[slot] + term
    ff = jnp.concatenate(pieces, axis=-1)
    y = alpha * x1_ref[...] + mod_ref[0, 5:6, :] * ff
    o_ref[...] = _layer_norm(y) * g_ref[...] + b_ref[...]


def _combine(x1, yg, rw, mod, l2g, l2b, *, row0, n_rows, mod_map, alpha):
    tm = TM_MIX
    t0 = row0 // tm
    tiles = n_rows // tm
    kern = functools.partial(_combine_kernel, alpha=alpha)

    def y_spec(kk):
        return pl.BlockSpec((tm * ROW_TILES, LANES), lambda i: (kk * tiles + i, 0))

    return pl.pallas_call(
        kern,
        grid=(n_rows // tm,),
        in_specs=[pl.BlockSpec((tm, D_MODEL), lambda i: (t0 + i, 0))]
        + [y_spec(kk) for kk in range(TOP_K)]
        + [pl.BlockSpec((tm, LANES), lambda i: (t0 + i, 0)),
           pl.BlockSpec((1, N_MOD, D_MODEL), mod_map),
           pl.BlockSpec((1, D_MODEL), lambda i: (0, 0)),
           pl.BlockSpec((1, D_MODEL), lambda i: (0, 0))],
        out_specs=pl.BlockSpec((tm, D_MODEL), lambda i: (i, 0)),
        out_shape=jax.ShapeDtypeStruct((n_rows, D_MODEL), F32),
        compiler_params=pltpu.CompilerParams(
            dimension_semantics=("arbitrary",), vmem_limit_bytes=VMEM_LIMIT),
        name="combine_ln2",
    )(x1, yg, yg, yg, yg, rw, mod, l2g, l2b)


def _route_kernel(ridx_ref, dest_ref, cnt_ref, run_ref, bst_ref):
    phase = pl.program_id(0)
    i = pl.program_id(1)
    tm = ridx_ref.shape[0]
    ridx = ridx_ref[...]
    lane = lax.broadcasted_iota(jnp.int32, (tm, LANES), 1)
    hits = [ridx[:, kk:kk + 1] == lane for kk in range(TOP_K)]
    chosen = jnp.where(hits[0], 1.0, 0.0)
    for kk in range(1, TOP_K):
        chosen = chosen + jnp.where(hits[kk], 1.0, 0.0)
    colsum = jnp.sum(chosen, axis=0, keepdims=True)

    @pl.when((phase == 0) & (i == 0))
    def _():
        run_ref[...] = jnp.zeros_like(run_ref)

    @pl.when(phase == 0)
    def _():
        run_ref[...] = run_ref[...] + colsum

    @pl.when((phase == 1) & (i == 0))
    def _():
        counts = run_ref[...]
        cnt_ref[...] = counts
        blocks = jnp.floor((counts + (MOE_ROWS - 1.0)) * (1.0 / MOE_ROWS))
        r = lax.broadcasted_iota(jnp.int32, (LANES, LANES), 0)
        c = lax.broadcasted_iota(jnp.int32, (LANES, LANES), 1)
        before = jnp.dot(blocks, (r < c).astype(F32), precision=HIGHEST, preferred_element_type=F32)
        bst_ref[...] = before * float(MOE_ROWS)
        run_ref[...] = jnp.zeros_like(run_ref)

    @pl.when(phase == 1)
    def _():
        rt = lax.broadcasted_iota(jnp.int32, (tm, tm), 0)
        ct = lax.broadcasted_iota(jnp.int32, (tm, tm), 1)
        earlier = _dot((ct < rt).astype(BF16), chosen.astype(BF16))
        row_of = bst_ref[0:1, :] + run_ref[0:1, :] + earlier
        out = jnp.zeros((tm, LANES), F32)
        for kk in range(TOP_K):
            dk = jnp.sum(jnp.where(hits[kk], row_of, 0.0), axis=-1, keepdims=True)
            out = jnp.where(lane == kk, dk, out)
        dest_ref[...] = out.T[0:SUBLANES, :].astype(jnp.int32)
        run_ref[...] = run_ref[...] + colsum


def _route(ridx):
    t_all = ridx.shape[0]
    tm = TM_ROUTE
    return pl.pallas_call(
        _route_kernel,
        grid=(2, t_all // tm),
        in_specs=[pl.BlockSpec((tm, LANES), lambda p, i: (i, 0))],
        out_specs=[pl.BlockSpec((SUBLANES, tm), lambda p, i: (0, i * p)),
                   pl.BlockSpec((SUBLANES, LANES), lambda p, i: (0, 0))],
        out_shape=[jax.ShapeDtypeStruct((SUBLANES, t_all), jnp.int32),
                   jax.ShapeDtypeStruct((SUBLANES, LANES), F32)],
        scratch_shapes=[pltpu.VMEM((SUBLANES, LANES), F32),
                        pltpu.VMEM((SUBLANES, LANES), F32)],
        compiler_params=pltpu.CompilerParams(
            dimension_semantics=("arbitrary", "arbitrary"), vmem_limit_bytes=VMEM_LIMIT),
        name="moe_route",
    )(ridx)


def _routing_tables(counts, n_blocks):
    experts = jnp.arange(N_EXPERTS, dtype=jnp.int32)
    blocks_per = (counts + MOE_ROWS - 1) // MOE_ROWS
    bends = jnp.cumsum(blocks_per)
    bstarts = bends - blocks_per
    blocks = jnp.arange(n_blocks, dtype=jnp.int32)
    block_expert = jnp.minimum(
        jnp.sum((bends[None, :] <= blocks[:, None]).astype(jnp.int32), axis=1), N_EXPERTS - 1)
    n_used = bends[-1:].astype(jnp.int32)
    owner = block_expert[:, None] == experts[None, :]

    def per_block(table):
        return jnp.sum(jnp.where(owner, table[None, :], 0), axis=1)

    n_valid = jnp.clip(per_block(counts) - (blocks - per_block(bstarts)) * MOE_ROWS,
                       0, MOE_ROWS).astype(jnp.int32)
    present = blocks_per > 0
    ordinal = jnp.cumsum(present.astype(jnp.int32)) - 1
    later = lax.cummin(jnp.where(present, experts, N_EXPERTS), reverse=True)
    succ = jnp.concatenate([later[1:], jnp.full((1,), N_EXPERTS, jnp.int32)])
    succ = jnp.where(succ >= N_EXPERTS, -1, succ)
    stage_slot = (per_block(ordinal) % 2).astype(jnp.int32)
    next_expert = per_block(succ).astype(jnp.int32)
    return (block_expert.astype(jnp.int32), n_used, n_valid, stage_slot, next_expert)


def _pos_embed_2d(n_tokens):
    rows = n_tokens // GRID_W
    r = np.repeat(np.arange(rows), GRID_W).astype(np.float32)
    col = np.tile(np.arange(GRID_W), rows).astype(np.float32)
    quarter = D_MODEL // 4
    omega = (np.float32(1.0)
             / np.power(np.float32(10000.0), np.arange(quarter, dtype=np.float32) / np.float32(quarter)))
    er = (r[:, None] * omega).astype(np.float64)
    ec = (col[:, None] * omega).astype(np.float64)
    table = np.concatenate([np.sin(er), np.cos(er), np.sin(ec), np.cos(ec)], axis=-1)
    return jnp.asarray(table, dtype=F32)


def _split_w_in(w):
    wt = w.T
    o_r = 2 * GLA_DK + 2 * GLA_DV
    o_f = o_r + DECAY_RANK
    o_gate = o_f + FNET_DIM
    row_scale = jnp.where(jnp.arange(o_r) < GLA_DK, DK_HEAD ** -0.5, 1.0).astype(w.dtype)
    w_main = (wt[:o_r] * row_scale[:, None]).astype(BF16)
    pad = jnp.zeros((LANES - DECAY_RANK, w.shape[0]), BF16)
    w_tail = jnp.concatenate([wt[o_gate:].astype(BF16), wt[o_f:o_gate].astype(BF16),
                              wt[o_r:o_f].astype(BF16), pad], axis=0)
    return w_main, w_tail


def kernel(x_prompt, x_sample, state_gla, c, c_ctx, w_ada, b_ada, w_in, w_dec_fwd, b_dec_fwd,
           w_dec_bwd, b_dec_bwd, gla_norm_g, w_br_gla, w_br_fnet, w_out, ln1_g, ln1_b, w_router,
           b_router, w_gate_up, b_gate_up, w_down, b_down, ln2_g, ln2_b):
    n_req, ctx_len, _ = x_prompt.shape
    n_lat, lat_len, _ = x_sample.shape
    depth = w_in.shape[0]
    alpha = (2.0 * depth) ** 0.25
    t_ctx = n_req * ctx_len
    t_lat = n_lat * lat_len
    t_all = t_ctx + t_lat

    x_ctx = x_prompt.reshape(t_ctx, D_MODEL)
    x_lat = x_sample.reshape(t_lat, D_MODEL)
    pos = _pos_embed_2d(lat_len)
    zero_pos = jnp.zeros_like(pos)

    cond_rows = -(-(n_lat + 1) // SUBLANES) * SUBLANES
    cond = jnp.zeros((cond_rows, D_MODEL), F32).at[:n_lat].set(c).at[cond_rows - 1].set(c_ctx)

    n_moe_blocks = (t_all * TOP_K) // MOE_ROWS + N_EXPERTS
    tok_chunks = t_all // (SC_WORKERS * SC_WINDOW)
    states = []
    for l in range(depth):
        mod = _ada(cond, w_ada[l], b_ada[l]).reshape(cond_rows, N_MOD, D_MODEL)
        layer_pos = pos if l == 0 else zero_pos
        w_dec = jnp.zeros((2, LANES, GLA_DK), F32)
        w_dec = w_dec.at[0, :DECAY_RANK].set(w_dec_fwd[l]).at[1, :DECAY_RANK].set(w_dec_bwd[l])
        b_dec = jnp.stack([b_dec_fwd[l], b_dec_bwd[l]]).reshape(2, 1, GLA_DK)
        proj, la = _inproj(x_ctx, x_lat, layer_pos, mod, *_split_w_in(w_in[l]), w_dec, b_dec, lat_len)

        norm_g = gla_norm_g[l].reshape(1, DV_HEAD)
        o_ctx, s_new = _gla(proj, la, norm_g, None, n_seq=n_req, seq_len=ctx_len, row0=0,
                            emit_state=True)
        (o_lat,) = _gla(proj, la, norm_g, state_gla[:, l], n_seq=n_lat, seq_len=lat_len, row0=t_ctx,
                        emit_state=False)
        states.append(s_new)

        mixed_ctx = _fnet(proj, n_seq=n_req, seq_len=ctx_len, row0=0)
        mixed_lat = _fnet(proj, n_seq=n_lat, seq_len=lat_len, row0=t_ctx)

        wr = jnp.zeros((D_MODEL, LANES), F32).at[:, :N_EXPERTS].set(w_router[l])
        br = jnp.full((1, LANES), -1e30, F32).at[0, :N_EXPERTS].set(b_router[l])
        wr_hi = wr.astype(BF16)
        wr_lo = (wr - wr_hi.astype(F32)).astype(BF16)
        x1, h2, ridx, rw = _merge(
            x_ctx, x_lat, layer_pos, mod, o_ctx, o_lat, mixed_ctx, mixed_lat, proj,
            w_br_gla[l].astype(BF16), w_br_fnet[l].astype(BF16), (0.5 * w_out[l]).astype(BF16),
            ln1_g[l].reshape(1, D_MODEL), ln1_b[l].reshape(1, D_MODEL), wr_hi, wr_lo, br, lat_len,
            alpha)

        dest, counts = _route(ridx)
        moe_tables = _routing_tables(counts[0, :N_EXPERTS].astype(jnp.int32), n_moe_blocks)
        dest = dest[:TOP_K]
        scatter_idx = dest.reshape(TOP_K, SC_WORKERS, tok_chunks, SC_WINDOW)
        p_rows = n_moe_blocks * MOE_ROWS
        xs = _sc_scatter_rows(h2.reshape(t_all, ROW_TILES, LANES), scatter_idx, p_rows)
        yb = _moe(moe_tables, xs.reshape(p_rows * ROW_TILES, LANES),
                  w_gate_up[l], b_gate_up[l], w_down[l], b_down[l])
        yb = yb.reshape(p_rows, ROW_TILES, LANES)

        def gathered(row0, n_rows):
            idx = dest[:, row0:row0 + n_rows].reshape(SC_WORKERS, -1, SC_WINDOW)
            return _sc_gather_rows(yb, idx).reshape(TOP_K * n_rows * ROW_TILES, LANES)

        l2g = ln2_g[l].reshape(1, D_MODEL)
        l2b = ln2_b[l].reshape(1, D_MODEL)
        tiles_per_seq = lat_len // TM_MIX
        yg_ctx = gathered(0, t_ctx)
        yg_lat = gathered(t_ctx, t_lat)
        x_ctx = _combine(x1, yg_ctx, rw, mod, l2g, l2b, row0=0, n_rows=t_ctx,
                         mod_map=lambda i: (cond_rows - 1, 0, 0), alpha=alpha)
        x_lat = _combine(x1, yg_lat, rw, mod, l2g, l2b, row0=t_ctx, n_rows=t_lat,
                         mod_map=lambda i: (i // tiles_per_seq, 0, 0), alpha=alpha)

    y_prompt = x_ctx.reshape(x_prompt.shape)
    y_sample = x_lat.reshape(x_sample.shape)
    new_state = jnp.stack(states, axis=1).astype(x_prompt.dtype)
    return (y_prompt, y_sample, new_state)
```

```python
import functools
import math

import numpy as np
import jax
import jax.numpy as jnp
from jax import lax
from jax.experimental import pallas as pl
from jax.experimental.pallas import tpu as pltpu
from jax.experimental.pallas import tpu_sc as plsc

F32 = jnp.float32
BF16 = jnp.bfloat16

D_MODEL = 1024
GRID_W = 64
GLA_HEADS = 4
DK_HEAD = 128
DV_HEAD = 256
GLA_DK = GLA_HEADS * DK_HEAD
GLA_DV = GLA_HEADS * DV_HEAD
DECAY_RANK = 16
GATE_NORMALIZER = 16.0
FNET_GROUPS = 4
FNET_GROUP_DIM = 128
FNET_DIM = FNET_GROUPS * FNET_GROUP_DIM
N_EXPERTS = 32
TOP_K = 4
D_EXPERT = 1024
SWIGLU_LIMIT = 7.0
SWIGLU_ALPHA = 1.702
LN_EPS = 1e-6
N_MOD = 6

LANES = 128
SUBLANES = 8
HALF_MODEL = D_MODEL // 2
ROW_TILES = HALF_MODEL // LANES
ROW_DTYPE = jnp.uint32
COL_Q = 0
COL_K = GLA_DK
COL_V = 2 * GLA_DK
COL_G = COL_V + GLA_DV
COL_GATE_A = COL_G + GLA_DV
COL_GATE_B = COL_GATE_A + D_MODEL
COL_F = COL_GATE_B + D_MODEL
COL_R = COL_F + FNET_DIM
PROJ_COLS = COL_R + LANES

GLA_CHUNK = 128
GLA_LEAF = 16
assert GLA_CHUNK == DK_HEAD
TM_PROJ = 512
PROJ_GROUP = 256
TM_MIX = 256
TM_MERGE = 512
TM_MERGE_GROUP = 256
TM_ROUTE = 1024
MOE_ROWS = 512
MOE_SHORT = 256
VMEM_LIMIT = 56 * 1024 * 1024

SC_CORES = 2
SC_SUBCORES = 16
SC_WORKERS = SC_CORES * SC_SUBCORES
SC_WINDOW = 64

HIGHEST = lax.Precision.HIGHEST


def _layer_norm(x):
    mu = jnp.mean(x, axis=-1, keepdims=True)
    xc = x - mu
    var = jnp.mean(xc * xc, axis=-1, keepdims=True)
    return xc * lax.rsqrt(var + LN_EPS)


def _sigmoid(x):
    return 0.5 * jnp.tanh(0.5 * x) + 0.5


def _log_sigmoid(z):
    return jnp.minimum(z, 0.0) - jnp.log(1.0 + jnp.exp(-jnp.abs(z)))


def _dot(a, b):
    return jnp.dot(a, b, preferred_element_type=F32)


def _split_bf16(x, terms):
    parts = []
    for _ in range(terms):
        p = x.astype(BF16)
        parts.append(p)
        x = x - p.astype(F32)
    return parts


def _dot_nt(a, b):
    return lax.dot_general(a, b, (((1,), (1,)), ((), ())), preferred_element_type=F32)


def _dot_tn(a, b):
    return lax.dot_general(a, b, (((0,), (0,)), ((), ())), preferred_element_type=F32)


def _row_tile_slice(j, n_rows, first_row=0):
    return pl.ds(first_row * ROW_TILES + j, n_rows, stride=ROW_TILES)


def _store_row_tiles(ref, val, first_row=0):
    for j in range(ROW_TILES):
        lo = val[:, j * LANES:(j + 1) * LANES]
        hi = val[:, HALF_MODEL + j * LANES:HALF_MODEL + (j + 1) * LANES]
        ref[_row_tile_slice(j, val.shape[0], first_row), :] = pltpu.pack_elementwise(
            [lo, hi], packed_dtype=BF16)


def _load_row_tile(ref, j, n_rows, first_row=0):
    words = ref[_row_tile_slice(j, n_rows, first_row), :]
    return tuple(pltpu.unpack_elementwise(words, index=half, packed_dtype=BF16, unpacked_dtype=F32)
                 for half in range(2))


def _ada_kernel(c_ref, w_ref, b_ref, o_ref):
    c = c_ref[...]
    s = c * _sigmoid(c)
    o_ref[...] = _dot(s.astype(BF16), w_ref[...].astype(BF16)) + b_ref[...]


def _ada(cond, w_ada, b_ada):
    rows = cond.shape[0]
    n = w_ada.shape[1]
    tn = 1536
    return pl.pallas_call(
        _ada_kernel,
        grid=(n // tn,),
        in_specs=[pl.BlockSpec((rows, D_MODEL), lambda j: (0, 0)),
                  pl.BlockSpec((D_MODEL, tn), lambda j: (0, j)),
                  pl.BlockSpec((1, tn), lambda j: (0, j))],
        out_specs=pl.BlockSpec((rows, tn), lambda j: (0, j)),
        out_shape=jax.ShapeDtypeStruct((rows, n), F32),
        compiler_params=pltpu.CompilerParams(vmem_limit_bytes=VMEM_LIMIT),
        name="ada_mod",
    )(cond, w_ada, b_ada.reshape(1, n))


def _group_maps(n_ctx_tiles):
    def ctx_map(i, *_):
        return (jnp.minimum(i, n_ctx_tiles - 1), 0)

    def lat_map(i, *_):
        return (jnp.maximum(i - n_ctx_tiles, 0), 0)

    return ctx_map, lat_map


def _token_specs(tm, n_ctx_tiles, tiles_per_latent_seq, ctx_mod_row):
    ctx_map, lat_map = _group_maps(n_ctx_tiles)

    def pos_map(i, *_):
        return (jnp.maximum(i - n_ctx_tiles, 0) % tiles_per_latent_seq, 0)

    def mod_map(i, *_):
        return (jnp.where(i < n_ctx_tiles, ctx_mod_row,
                          jnp.maximum(i - n_ctx_tiles, 0) // tiles_per_latent_seq), 0, 0)

    return [pl.BlockSpec((tm, D_MODEL), ctx_map),
            pl.BlockSpec((tm, D_MODEL), lat_map),
            pl.BlockSpec((tm, D_MODEL), pos_map),
            pl.BlockSpec((1, N_MOD, D_MODEL), mod_map)]


def _inproj_kernel(xc_ref, xl_ref, pos_ref, mod_ref, wm_ref, wt_ref, wdh_ref, wdl_ref, bd_ref,
                   o_ref, la_ref, *, n_ctx_tiles):
    tail = lambda lo, hi: wt_ref[lo - COL_GATE_A:hi - COL_GATE_A, :]
    i = pl.program_id(0)

    def project(x_of):
        groups = [slice(r0, r0 + PROJ_GROUP) for r0 in range(0, TM_PROJ, PROJ_GROUP)]
        hs = [(_layer_norm(x_of(rows)) * (1.0 + mod_ref[0, 1:2, :]) + mod_ref[0, 0:1, :]).astype(BF16)
              for rows in groups]
        rs = [_split_bf16(_dot_nt(h, tail(COL_R, PROJ_COLS)), 2) for h in hs]
        for rows, (r_hi, r_lo) in zip(groups, rs):
            for d in range(2):
                z = (_dot(r_hi, wdh_ref[d]) + _dot(r_lo, wdh_ref[d]) + _dot(r_hi, wdl_ref[d])
                     + bd_ref[d])
                la_ref[rows, d * GLA_DK:(d + 1) * GLA_DK] = _log_sigmoid(z) * (1.0 / GATE_NORMALIZER)
        for rows, h in zip(groups, hs):
            hg = 0.5 * _dot_nt(h, wm_ref[COL_G:COL_GATE_A, :])
            o_ref[rows, COL_G:COL_GATE_A] = hg * (jnp.tanh(hg) + 1.0)
        for rows, h in zip(groups, hs):
            gates = _dot_nt(h, tail(COL_GATE_A, COL_F))
            o_ref[rows, COL_GATE_A:COL_F] = jnp.tanh(0.5 * gates) + 1.0
        for rows, h in zip(groups, hs):
            o_ref[rows, :COL_G] = _dot_nt(h, wm_ref[:COL_G, :])
        for rows, h in zip(groups, hs):
            o_ref[rows, COL_F:COL_R] = _dot_nt(h, tail(COL_F, COL_R))

    @pl.when(i < n_ctx_tiles)
    def _():
        project(lambda rows: xc_ref[rows, :])

    @pl.when(i >= n_ctx_tiles)
    def _():
        project(lambda rows: xl_ref[rows, :] + pos_ref[rows, :])


def _inproj(x_ctx, x_lat, pos, mod, w_main, w_tail, w_dec, b_dec, lat_len):
    t_ctx, t_lat = x_ctx.shape[0], x_lat.shape[0]
    t_all = t_ctx + t_lat
    n_ctx_tiles = t_ctx // TM_PROJ
    kern = functools.partial(_inproj_kernel, n_ctx_tiles=n_ctx_tiles)
    specs = _token_specs(TM_PROJ, n_ctx_tiles, lat_len // TM_PROJ, mod.shape[0] - 1)
    w_dec_hi = w_dec.astype(BF16)
    w_dec_lo = (w_dec - w_dec_hi.astype(F32)).astype(BF16)

    def const(shape):
        return pl.BlockSpec(shape, lambda i: (0,) * len(shape))

    return pl.pallas_call(
        kern,
        grid=(t_all // TM_PROJ,),
        in_specs=specs + [pl.BlockSpec(w_main.shape, lambda i: (0, 0), pipeline_mode=pl.Buffered(1)),
                          pl.BlockSpec(w_tail.shape, lambda i: (0, 0), pipeline_mode=pl.Buffered(1)),
                          const((2, LANES, GLA_DK)), const((2, LANES, GLA_DK)), const((2, 1, GLA_DK))],
        out_specs=[pl.BlockSpec((TM_PROJ, COL_R), lambda i: (i, 0)),
                   pl.BlockSpec((TM_PROJ, 2 * GLA_DK), lambda i: (i, 0))],
        out_shape=[jax.ShapeDtypeStruct((t_all, COL_R), F32),
                   jax.ShapeDtypeStruct((t_all, 2 * GLA_DK), F32)],
        compiler_params=pltpu.CompilerParams(
            dimension_semantics=("arbitrary",), vmem_limit_bytes=VMEM_LIMIT),
        name="ln_inproj",
    )(x_ctx, x_lat, pos, mod, w_main, w_tail, w_dec_hi, w_dec_lo, b_dec)


def _gla_kernel(*refs, seq_len, has_s0, emit_state):
    it = iter(refs)
    q_ref, k_ref, v_ref, laf_ref, lab_ref, g_ref = (next(it) for _ in range(6))
    s0_ref = next(it) if has_s0 else None
    o_ref = next(it)
    sout_ref = next(it) if emit_state else None
    cum_ref, aq_ref, ko_ref, dec_ref, op_ref, st_ref = (next(it) for _ in range(6))

    C = GLA_CHUNK
    n_chunks = seq_len // C

    def rows(n):
        if isinstance(n, int):
            return pl.ds(n * C, C)
        return pl.ds(pl.multiple_of(n * C, C), C)

    def loop(body):
        if n_chunks <= 2:
            for n in range(n_chunks):
                body(n)
        else:
            def step(m, carry):
                body(2 * m)
                body(2 * m + 1)
                return carry
            lax.fori_loop(0, n_chunks // 2, step, 0)

    rt = lax.broadcasted_iota(jnp.int32, (C, C), 0)
    ct = lax.broadcasted_iota(jnp.int32, (C, C), 1)
    tri = ((rt >= ct).astype(BF16), (ct >= rt).astype(BF16))
    row_id = lax.broadcasted_iota(jnp.int32, (C, DK_HEAD), 0)

    def cumsum_chunk(n):
        for d, la_ref in enumerate((laf_ref, lab_ref)):
            la_hi, la_lo = _split_bf16(la_ref[rows(n), :], 2)
            cum_ref[d, rows(n), :] = _dot(tri[d], la_hi) + _dot(tri[d], la_lo)

    loop(cumsum_chunk)

    query_rows = ({}, {})
    keep = ({}, {})
    for d in range(2):
        blk = C // 2
        while blk >= GLA_LEAF:
            q_parity = 1 if d == 0 else 0
            query_rows[d][blk] = ((row_id // blk) % 2) == q_parity
            qb, kb = rt // blk, ct // blk
            keep[d][blk] = ((qb % 2) == q_parity) & ((qb == kb + 1) if d == 0 else (kb == qb + 1))
            blk //= 2
        order = (rt >= ct) if d == 0 else (ct >= rt)
        keep[d][0] = ((rt // GLA_LEAF) == (ct // GLA_LEAF)) & order

    n_leaves = C // GLA_LEAF

    def per_leaf(rows_of_cum):
        return jnp.concatenate(rows_of_cum, axis=0)

    def expand(per_leaf_rows):
        return jnp.concatenate(
            [jnp.broadcast_to(per_leaf_rows[j:j + 1, :], (GLA_LEAF, DK_HEAD)) for j in range(n_leaves)],
            axis=0)

    def scores(n, d):
        cum = cum_ref[d, rows(n), :]
        mid = GLA_LEAF // 2 - 1 if d == 0 else GLA_LEAF // 2
        at_mid = per_leaf([cum[mid + j * GLA_LEAF:mid + j * GLA_LEAF + 1, :] for j in range(n_leaves)])
        e = cum - expand(at_mid)
        qe = q_ref[rows(n), :] * jnp.exp(e)
        ke = k_ref[rows(n), :] * jnp.exp(-e)
        acc = jnp.where(keep[d][0], _dot_nt(qe.astype(BF16), ke.astype(BF16)), 0.0)
        blk = C // 2
        while blk >= GLA_LEAF:
            bnd = blk - 1 if d == 0 else blk
            at_bnd = per_leaf([cum[bnd + (j * GLA_LEAF) // (2 * blk) * (2 * blk):
                                   bnd + (j * GLA_LEAF) // (2 * blk) * (2 * blk) + 1, :]
                               for j in range(n_leaves)])
            through = expand(jnp.exp(-jnp.abs(at_mid - at_bnd)))
            x = (jnp.where(query_rows[d][blk], qe, ke) * through).astype(BF16)
            acc = acc + jnp.where(keep[d][blk], _dot_nt(x, x), 0.0)
            blk //= 2
        aq_ref[d, rows(n), 0:C] = acc.astype(BF16)
        aq_ref[d, rows(n), C:C + DK_HEAD] = (qe * expand(jnp.exp(at_mid))).astype(BF16)
        end = cum[C - 1:C, :] if d == 0 else cum[0:1, :]
        ko_ref[d, rows(n), :] = (ke * expand(jnp.exp(end - at_mid))).astype(BF16)
        dec_ref[d, rows(n), :] = jnp.broadcast_to(jnp.exp(end), (DK_HEAD, DK_HEAD)).T

    def scores_chunk(n):
        scores(n, 0)
        scores(n, 1)

    loop(scores_chunk)

    for d in range(2):
        if has_s0:
            st_ref[d] = s0_ref[0, d, 0]
        else:
            st_ref[d] = jnp.zeros((DK_HEAD, DV_HEAD), F32)

    def scan(n, d):
        v = v_ref[rows(n), :].astype(BF16)
        st = st_ref[d]
        op_ref[d, rows(n), :] = _dot(aq_ref[d, rows(n), :],
                                     jnp.concatenate([v, st.astype(BF16)], axis=0))
        dec = dec_ref[d, rows(n), :]
        st_ref[d] = (st * jnp.concatenate([dec] * (DV_HEAD // DK_HEAD), axis=1)
                     + _dot_tn(ko_ref[d, rows(n), :], v))

    def scan_chunk(m):
        scan(m, 0)
        scan(n_chunks - 1 - m, 1)

    loop(scan_chunk)
    if emit_state:
        sout_ref[0, 0, 0] = st_ref[0]
        sout_ref[0, 1, 0] = st_ref[1]

    g = g_ref[...]

    def finish_chunk(n):
        o = op_ref[0, rows(n), :] + op_ref[1, rows(n), :]
        ms = jnp.mean(o * o, axis=-1, keepdims=True)
        o_ref[rows(n), :] = o * lax.rsqrt(ms + LN_EPS) * g

    loop(finish_chunk)


def _gla(proj, la, g, s0, *, n_seq, seq_len, row0, emit_state):
    has_s0 = s0 is not None
    blk0 = row0 // seq_len
    kern = functools.partial(_gla_kernel, seq_len=seq_len, has_s0=has_s0, emit_state=emit_state)
    in_specs = [
        pl.BlockSpec((seq_len, DK_HEAD), lambda b, h: (blk0 + b, COL_Q // DK_HEAD + h)),
        pl.BlockSpec((seq_len, DK_HEAD), lambda b, h: (blk0 + b, COL_K // DK_HEAD + h)),
        pl.BlockSpec((seq_len, DV_HEAD), lambda b, h: (blk0 + b, COL_V // DV_HEAD + h)),
        pl.BlockSpec((seq_len, DK_HEAD), lambda b, h: (blk0 + b, h)),
        pl.BlockSpec((seq_len, DK_HEAD), lambda b, h: (blk0 + b, GLA_HEADS + h)),
        pl.BlockSpec((1, DV_HEAD), lambda b, h: (0, 0)),
    ]
    args = [proj, proj, proj, la, la, g]
    if has_s0:
        in_specs.append(pl.BlockSpec((1, 2, 1, DK_HEAD, DV_HEAD), lambda b, h: (b, 0, h, 0, 0)))
        args.append(s0)
    out_specs = [pl.BlockSpec((seq_len, DV_HEAD), lambda b, h: (b, h))]
    out_shape = [jax.ShapeDtypeStruct((n_seq * seq_len, GLA_DV), F32)]
    if emit_state:
        out_specs.append(pl.BlockSpec((1, 2, 1, DK_HEAD, DV_HEAD), lambda b, h: (b, 0, h, 0, 0)))
        out_shape.append(jax.ShapeDtypeStruct((n_seq, 2, GLA_HEADS, DK_HEAD, DV_HEAD), F32))

    res = pl.pallas_call(
        kern,
        grid=(n_seq, GLA_HEADS),
        in_specs=in_specs,
        out_specs=out_specs,
        out_shape=out_shape,
        scratch_shapes=[pltpu.VMEM((2, seq_len, DK_HEAD), F32),
                        pltpu.VMEM((2, seq_len, GLA_CHUNK + DK_HEAD), BF16),
                        pltpu.VMEM((2, seq_len, DK_HEAD), BF16),
                        pltpu.VMEM((2, seq_len, DK_HEAD), F32),
                        pltpu.VMEM((2, seq_len, DV_HEAD), F32),
                        pltpu.VMEM((2, DK_HEAD, DV_HEAD), F32)],
        compiler_params=pltpu.CompilerParams(
            dimension_semantics=("arbitrary", "arbitrary"), vmem_limit_bytes=VMEM_LIMIT),
        name="gla_seq%d" % seq_len,
    )(*args)
    return res


def _fnet_kernel(f_ref, cl_ref, sl_ref, cg_ref, sg_ref, o_ref, uc_ref, us_ref, *, seq_len):
    cg = cg_ref[...]
    sg = sg_ref[...]
    for grp in range(FNET_GROUPS):
        lo = grp * FNET_GROUP_DIM
        u = f_ref[:, lo:lo + FNET_GROUP_DIM].astype(BF16)
        uc_ref[:, lo:lo + FNET_GROUP_DIM] = _dot(u, cg).astype(BF16)
        us_ref[:, lo:lo + FNET_GROUP_DIM] = _dot(u, sg).astype(BF16)
    mixed = _dot(cl_ref[...], uc_ref[...]) - _dot(sl_ref[...], us_ref[...])
    o_ref[...] = mixed * (1.0 / math.sqrt(seq_len * FNET_GROUP_DIM))


def _dft_mats(n):
    j = np.arange(n, dtype=np.int64)
    ang = (2.0 * np.pi / n) * ((j[:, None] * j[None, :]) % n).astype(np.float64)
    return (jnp.asarray(np.cos(ang), dtype=F32).astype(BF16),
            jnp.asarray(np.sin(ang), dtype=F32).astype(BF16))


def _fnet(proj, *, n_seq, seq_len, row0):
    blk0 = row0 // seq_len
    cl, sl = _dft_mats(seq_len)
    cg, sg = _dft_mats(FNET_GROUP_DIM)
    kern = functools.partial(_fnet_kernel, seq_len=seq_len)
    return pl.pallas_call(
        kern,
        grid=(n_seq,),
        in_specs=[pl.BlockSpec((seq_len, FNET_DIM), lambda b: (blk0 + b, COL_F // FNET_DIM)),
                  pl.BlockSpec((seq_len, seq_len), lambda b: (0, 0)),
                  pl.BlockSpec((seq_len, seq_len), lambda b: (0, 0)),
                  pl.BlockSpec((FNET_GROUP_DIM, FNET_GROUP_DIM), lambda b: (0, 0)),
                  pl.BlockSpec((FNET_GROUP_DIM, FNET_GROUP_DIM), lambda b: (0, 0))],
        out_specs=pl.BlockSpec((seq_len, FNET_DIM), lambda b: (b, 0)),
        out_shape=jax.ShapeDtypeStruct((n_seq * seq_len, FNET_DIM), F32),
        scratch_shapes=[pltpu.VMEM((seq_len, FNET_DIM), BF16),
                        pltpu.VMEM((seq_len, FNET_DIM), BF16)],
        compiler_params=pltpu.CompilerParams(
            dimension_semantics=("arbitrary",), vmem_limit_bytes=VMEM_LIMIT),
        name="fnet_seq%d" % seq_len,
    )(proj, cl, sl, cg, sg)


def _merge_kernel(xc_ref, xl_ref, pos_ref, mod_ref, oc_ref, ol_ref, mc_ref, ml_ref,
                  g_ref, ga_ref, gb_ref, wbg_ref, wbf_ref, wo_ref, l1g_ref, l1b_ref, wrh_ref, wrl_ref,
                  br_ref, x1_ref, h2_ref, ridx_ref, rw_ref, *, n_ctx_tiles, alpha):
    i = pl.program_id(0)

    tm = x1_ref.shape[0]
    groups = [slice(r0, r0 + TM_MERGE_GROUP) for r0 in range(0, tm, TM_MERGE_GROUP)]
    shape = (TM_MERGE_GROUP, LANES)
    lane_i = lax.broadcasted_iota(jnp.int32, shape, 1)
    lane = lane_i.astype(F32)

    def compute(x_of, o_ref, mx_ref):
        branch = []
        for rows in groups:
            a = (o_ref[rows, :] * g_ref[rows, :]).astype(BF16)
            branch.append((_dot(a, wbg_ref[...]), _dot(mx_ref[rows, :].astype(BF16), wbf_ref[...])))
        mix = []
        for rows, (gla_out, fnet_out) in zip(groups, branch):
            merged = ga_ref[rows, :] * gla_out + gb_ref[rows, :] * fnet_out
            mix.append(_dot(merged.astype(BF16), wo_ref[...]))
        logits = []
        for rows, mix_g in zip(groups, mix):
            y = alpha * x_of(rows) + mod_ref[0, 2:3, :] * mix_g
            x1 = _layer_norm(y) * l1g_ref[...] + l1b_ref[...]
            x1_ref[rows, :] = x1
            h2 = _layer_norm(x1) * (1.0 + mod_ref[0, 4:5, :]) + mod_ref[0, 3:4, :]
            _store_row_tiles(h2_ref, h2, rows.start)
            h_hi, h_lo = _split_bf16(h2, 2)
            logits.append(_dot(h_hi, wrh_ref[...]) + _dot(h_lo, wrh_ref[...])
                          + _dot(h_hi, wrl_ref[...]) + br_ref[...])
        idx_out = [jnp.zeros(shape, F32) for _ in groups]
        val_out = [jnp.zeros(shape, F32) for _ in groups]
        top0 = [None] * len(groups)
        denom = [None] * len(groups)
        for kk in range(TOP_K):
            for gi in range(len(groups)):
                m = jnp.max(logits[gi], axis=-1, keepdims=True)
                sel = jnp.min(jnp.where(logits[gi] == m, lane, float(LANES)), axis=-1, keepdims=True)
                if kk == 0:
                    top0[gi] = m
                    p = jnp.ones_like(m)
                    denom[gi] = p
                else:
                    p = jnp.exp(m - top0[gi])
                    denom[gi] = denom[gi] + p
                idx_out[gi] = jnp.where(lane_i == kk, sel, idx_out[gi])
                val_out[gi] = jnp.where(lane_i == kk, p, val_out[gi])
                logits[gi] = jnp.where(lane == sel, -jnp.inf, logits[gi])
        for gi, rows in enumerate(groups):
            ridx_ref[rows, :] = idx_out[gi].astype(jnp.int32)
            rw_ref[rows, :] = val_out[gi] / denom[gi]

    @pl.when(i < n_ctx_tiles)
    def _():
        compute(lambda rows: xc_ref[rows, :], oc_ref, mc_ref)

    @pl.when(i >= n_ctx_tiles)
    def _():
        compute(lambda rows: xl_ref[rows, :] + pos_ref[rows, :], ol_ref, ml_ref)


def _merge(x_ctx, x_lat, pos, mod, o_ctx, o_lat, mixed_ctx, mixed_lat, proj,
           wbg, wbf, wo, l1g, l1b, wr_hi, wr_lo, br, lat_len, alpha):
    t_ctx, t_lat = x_ctx.shape[0], x_lat.shape[0]
    t_all = t_ctx + t_lat
    tm = TM_MERGE
    n_ctx_tiles = t_ctx // tm
    kern = functools.partial(_merge_kernel, n_ctx_tiles=n_ctx_tiles, alpha=alpha)
    specs = _token_specs(tm, n_ctx_tiles, lat_len // tm, mod.shape[0] - 1)
    ctx_map, lat_map = _group_maps(n_ctx_tiles)

    def const(shape):
        return pl.BlockSpec(shape, lambda i: (0,) * len(shape))

    in_specs = specs + [
        pl.BlockSpec((tm, GLA_DV), ctx_map),
        pl.BlockSpec((tm, GLA_DV), lat_map),
        pl.BlockSpec((tm, FNET_DIM), ctx_map),
        pl.BlockSpec((tm, FNET_DIM), lat_map),
        pl.BlockSpec((tm, GLA_DV), lambda i: (i, COL_G // GLA_DV)),
        pl.BlockSpec((tm, D_MODEL), lambda i: (i, COL_GATE_A // D_MODEL)),
        pl.BlockSpec((tm, D_MODEL), lambda i: (i, COL_GATE_B // D_MODEL)),
        const((GLA_DV, D_MODEL)), const((FNET_DIM, D_MODEL)), const((D_MODEL, D_MODEL)),
        const((1, D_MODEL)), const((1, D_MODEL)),
        const((D_MODEL, LANES)), const((D_MODEL, LANES)), const((1, LANES)),
    ]
    out_specs = [pl.BlockSpec((tm, D_MODEL), lambda i: (i, 0)),
                 pl.BlockSpec((tm * ROW_TILES, LANES), lambda i: (i, 0)),
                 pl.BlockSpec((tm, LANES), lambda i: (i, 0)),
                 pl.BlockSpec((tm, LANES), lambda i: (i, 0))]
    out_shape = [jax.ShapeDtypeStruct((t_all, D_MODEL), F32),
                 jax.ShapeDtypeStruct((t_all * ROW_TILES, LANES), ROW_DTYPE),
                 jax.ShapeDtypeStruct((t_all, LANES), jnp.int32),
                 jax.ShapeDtypeStruct((t_all, LANES), F32)]
    return pl.pallas_call(
        kern,
        grid=(t_all // tm,),
        in_specs=in_specs,
        out_specs=out_specs,
        out_shape=out_shape,
        compiler_params=pltpu.CompilerParams(
            dimension_semantics=("arbitrary",), vmem_limit_bytes=VMEM_LIMIT),
        name="merge_ln1_router",
    )(x_ctx, x_lat, pos, mod, o_ctx, o_lat, mixed_ctx, mixed_lat, proj, proj, proj,
      wbg, wbf, wo, l1g, l1b, wr_hi, wr_lo, br)


def _sc_mesh():
    return plsc.VectorSubcoreMesh(core_axis_name="c", subcore_axis_name="s")


def _sc_worker_id():
    return lax.axis_index("s") * SC_CORES + lax.axis_index("c")


def _sc_scatter_rows(src, idx, n_out):
    n_src = src.shape[0]
    w = SC_WINDOW
    n_chunks = n_src // (SC_WORKERS * w)
    copies = idx.shape[0]
    assert n_chunks % 2 == 0 and idx.shape == (copies, SC_WORKERS, n_chunks, w)

    @functools.partial(
        pl.kernel, mesh=_sc_mesh(),
        out_type=jax.ShapeDtypeStruct((n_out, ROW_TILES, LANES), ROW_DTYPE),
        scratch_types=[pltpu.VMEM((copies * n_chunks, w), jnp.int32),
                       pltpu.VMEM((2, w, ROW_TILES, LANES), ROW_DTYPE),
                       pltpu.SemaphoreType.DMA((2,)),
                       pltpu.SemaphoreType.DMA((2,))],
        name="moe_dispatch_scatter")
    def k(src_hbm, idx_hbm, out_hbm, idx_v, rows_v, rsem, wsem):
        wid = _sc_worker_id()
        base = wid * (n_chunks * w)
        for kk in range(copies):
            pltpu.sync_copy(idx_hbm.at[kk, wid], idx_v.at[pl.ds(kk * n_chunks, n_chunks)])

        def read(j, slot):
            return pltpu.make_async_copy(src_hbm.at[pl.ds(base + j * w, w)], rows_v.at[slot],
                                         rsem.at[slot])

        def scatter(j, kk, slot):
            return pltpu.make_async_copy(rows_v.at[slot], out_hbm.at[idx_v.at[kk * n_chunks + j]],
                                         wsem.at[slot])

        read(0, 0).start()

        @pl.loop(0, n_chunks, step=2)
        def _(jj):
            read(jj, 0).wait()

            @pl.when(jj > 0)
            def _():
                for kk in range(copies):
                    scatter(jj - 1, kk, 1).wait()

            read(jj + 1, 1).start()
            for kk in range(copies):
                scatter(jj, kk, 0).start()
            read(jj + 1, 1).wait()
            for kk in range(copies):
                scatter(jj, kk, 0).wait()

            @pl.when(jj + 2 < n_chunks)
            def _():
                read(jj + 2, 0).start()

            for kk in range(copies):
                scatter(jj + 1, kk, 1).start()

        for kk in range(copies):
            scatter(n_chunks - 1, kk, 1).wait()

    return k(src, idx)


def _sc_gather_rows(table, idx):
    _, n_chunks, w = idx.shape
    assert n_chunks % 2 == 0 and idx.shape[0] == SC_WORKERS and w == SC_WINDOW
    n_out = SC_WORKERS * n_chunks * w

    @functools.partial(
        pl.kernel, mesh=_sc_mesh(),
        out_type=jax.ShapeDtypeStruct((n_out, ROW_TILES, LANES), ROW_DTYPE),
        scratch_types=[pltpu.VMEM((n_chunks, w), jnp.int32),
                       pltpu.VMEM((2, w, ROW_TILES, LANES), ROW_DTYPE),
                       pltpu.SemaphoreType.DMA((2,)),
                       pltpu.SemaphoreType.DMA((2,))],
        name="moe_combine_gather")
    def k(table_hbm, idx_hbm, out_hbm, idx_v, rows_v, gsem, wsem):
        wid = _sc_worker_id()
        base = wid * (n_chunks * w)
        pltpu.sync_copy(idx_hbm.at[wid], idx_v)

        def gather(j, slot):
            return pltpu.make_async_copy(table_hbm.at[idx_v.at[j]], rows_v.at[slot], gsem.at[slot])

        def write(j, slot):
            return pltpu.make_async_copy(rows_v.at[slot], out_hbm.at[pl.ds(base + j * w, w)],
                                         wsem.at[slot])

        gather(0, 0).start()

        @pl.loop(0, n_chunks, step=2)
        def _(jj):
            gather(jj, 0).wait()

            @pl.when(jj > 0)
            def _():
                write(jj - 1, 1).wait()

            gather(jj + 1, 1).start()
            write(jj, 0).start()
            gather(jj + 1, 1).wait()
            write(jj, 0).wait()

            @pl.when(jj + 2 < n_chunks)
            def _():
                gather(jj + 2, 0).start()

            write(jj + 1, 1).start()

        write(n_chunks - 1, 1).wait()

    return k(table, idx)


def _moe_kernel(be_ref, nu_ref, nv_ref, slot_ref, nxt_ref, x_ref, wgu_hbm, bgu_ref, wd_hbm, bd_ref,
                o_ref, wgu_st, wd_st, wgu_bf, wd_bf, xb_ref, sem):
    b = pl.program_id(0)
    e = be_ref[b]
    prev = be_ref[jnp.maximum(b - 1, 0)]
    active = b < nu_ref[0]
    changed = (b == 0) | (e != prev)

    def weight_copies(expert, s):
        return (pltpu.make_async_copy(wgu_hbm.at[expert], wgu_st.at[s], sem.at[0, s]),
                pltpu.make_async_copy(wd_hbm.at[expert], wd_st.at[s], sem.at[1, s]))

    @pl.when(active & changed)
    def _():
        s = slot_ref[b]

        @pl.when(b == 0)
        def _():
            for cp in weight_copies(e, s):
                cp.start()

        for cp in weight_copies(e, s):
            cp.wait()
        wgu_bf[...] = wgu_st[s].astype(BF16)
        wd_bf[...] = wd_st[s].astype(BF16)
        nxt = nxt_ref[b]

        @pl.when(nxt >= 0)
        def _():
            for cp in weight_copies(nxt, 1 - s):
                cp.start()

    n_valid = nv_ref[b]

    def expert_mlp(n_rows):
        valid = lax.broadcasted_iota(jnp.int32, (n_rows, LANES), 0) < n_valid
        for j in range(ROW_TILES):
            for half, xj in enumerate(_load_row_tile(x_ref, j, n_rows)):
                c0 = half * HALF_MODEL + j * LANES
                xb_ref[0:n_rows, c0:c0 + LANES] = jnp.where(valid, xj, 0.0).astype(BF16)
        gu = _dot(xb_ref[0:n_rows, :], wgu_bf[...]) + bgu_ref[0]
        gate = jnp.minimum(gu[:, :D_EXPERT], SWIGLU_LIMIT)
        up = jnp.clip(gu[:, D_EXPERT:], -SWIGLU_LIMIT, SWIGLU_LIMIT)
        glu = gate * _sigmoid(gate * SWIGLU_ALPHA)
        act = ((up + 1.0) * glu).astype(BF16)
        _store_row_tiles(o_ref, _dot(act, wd_bf[...]) + bd_ref[0])

    @pl.when(active & (n_valid > MOE_SHORT))
    def _():
        expert_mlp(MOE_ROWS)

    @pl.when(active & (n_valid <= MOE_SHORT))
    def _():
        expert_mlp(MOE_SHORT)


def _moe(tables, xs, w_gate_up, b_gate_up, w_down, b_down):
    p_rows = xs.shape[0] // ROW_TILES
    n_blocks = p_rows // MOE_ROWS

    def blk(b, be, nu, *_):
        return jnp.minimum(b, nu[0] - 1)

    def expert(b, be, nu, *_):
        return (be[blk(b, be, nu)], 0, 0)

    def rows(b, be, nu, *_):
        return (blk(b, be, nu), 0)

    grid_spec = pltpu.PrefetchScalarGridSpec(
        num_scalar_prefetch=len(tables),
        grid=(n_blocks,),
        in_specs=[
            pl.BlockSpec((MOE_ROWS * ROW_TILES, LANES), rows),
            pl.BlockSpec(memory_space=pl.ANY),
            pl.BlockSpec((1, 1, 2 * D_EXPERT), expert),
            pl.BlockSpec(memory_space=pl.ANY),
            pl.BlockSpec((1, 1, D_MODEL), expert),
        ],
        out_specs=pl.BlockSpec((MOE_ROWS * ROW_TILES, LANES), rows),
        scratch_shapes=[pltpu.VMEM((2, D_MODEL, 2 * D_EXPERT), F32),
                        pltpu.VMEM((2, D_EXPERT, D_MODEL), F32),
                        pltpu.VMEM((D_MODEL, 2 * D_EXPERT), BF16),
                        pltpu.VMEM((D_EXPERT, D_MODEL), BF16),
                        pltpu.VMEM((MOE_ROWS, D_MODEL), BF16),
                        pltpu.SemaphoreType.DMA((2, 2))],
    )
    return pl.pallas_call(
        _moe_kernel,
        grid_spec=grid_spec,
        out_shape=jax.ShapeDtypeStruct((p_rows * ROW_TILES, LANES), ROW_DTYPE),
        compiler_params=pltpu.CompilerParams(
            dimension_semantics=("arbitrary",), vmem_limit_bytes=VMEM_LIMIT),
        name="moe_grouped_mlp",
    )(*tables, xs, w_gate_up, b_gate_up.reshape(N_EXPERTS, 1, 2 * D_EXPERT), w_down,
      b_down.reshape(N_EXPERTS, 1, D_MODEL))


def _combine_kernel(x1_ref, y0_ref, y1_ref, y2_ref, y3_ref, rw_ref, mod_ref, g_ref, b_ref, o_ref,
                    *, alpha):
    rw = rw_ref[...]
    y_refs = (y0_ref, y1_ref, y2_ref, y3_ref)
    pieces = [None] * (2 * ROW_TILES)
    for j in range(ROW_TILES):
        for kk in range(TOP_K):
            for half, yj in enumerate(_load_row_tile(y_refs[kk], j, rw.shape[0])):
                term = rw[:, kk:kk + 1] * yj
                slot = half * ROW_TILES + j
                pieces[slot] = term if kk == 0 else pieces[slot] + term
    ff = jnp.concatenate(pieces, axis=-1)
    y = alpha * x1_ref[...] + mod_ref[0, 5:6, :] * ff
    o_ref[...] = _layer_norm(y) * g_ref[...] + b_ref[...]


def _combine(x1, yg, rw, mod, l2g, l2b, *, row0, n_rows, mod_map, alpha):
    tm = TM_MIX
    t0 = row0 // tm
    tiles = n_rows // tm
    kern = functools.partial(_combine_kernel, alpha=alpha)

    def y_spec(kk):
        return pl.BlockSpec((tm * ROW_TILES, LANES), lambda i: (kk * tiles + i, 0))

    return pl.pallas_call(
        kern,
        grid=(n_rows // tm,),
        in_specs=[pl.BlockSpec((tm, D_MODEL), lambda i: (t0 + i, 0))]
        + [y_spec(kk) for kk in range(TOP_K)]
        + [pl.BlockSpec((tm, LANES), lambda i: (t0 + i, 0)),
           pl.BlockSpec((1, N_MOD, D_MODEL), mod_map),
           pl.BlockSpec((1, D_MODEL), lambda i: (0, 0)),
           pl.BlockSpec((1, D_MODEL), lambda i: (0, 0))],
        out_specs=pl.BlockSpec((tm, D_MODEL), lambda i: (i, 0)),
        out_shape=jax.ShapeDtypeStruct((n_rows, D_MODEL), F32),
        compiler_params=pltpu.CompilerParams(
            dimension_semantics=("arbitrary",), vmem_limit_bytes=VMEM_LIMIT),
        name="combine_ln2",
    )(x1, yg, yg, yg, yg, rw, mod, l2g, l2b)


def _route_kernel(ridx_ref, dest_ref, cnt_ref, run_ref, bst_ref):
    phase = pl.program_id(0)
    i = pl.program_id(1)
    tm = ridx_ref.shape[0]
    ridx = ridx_ref[...]
    lane = lax.broadcasted_iota(jnp.int32, (tm, LANES), 1)
    hits = [ridx[:, kk:kk + 1] == lane for kk in range(TOP_K)]
    chosen = jnp.where(hits[0], 1.0, 0.0)
    for kk in range(1, TOP_K):
        chosen = chosen + jnp.where(hits[kk], 1.0, 0.0)
    colsum = jnp.sum(chosen, axis=0, keepdims=True)

    @pl.when((phase == 0) & (i == 0))
    def _():
        run_ref[...] = jnp.zeros_like(run_ref)

    @pl.when(phase == 0)
    def _():
        run_ref[...] = run_ref[...] + colsum

    @pl.when((phase == 1) & (i == 0))
    def _():
        counts = run_ref[...]
        cnt_ref[...] = counts
        blocks = jnp.floor((counts + (MOE_ROWS - 1.0)) * (1.0 / MOE_ROWS))
        r = lax.broadcasted_iota(jnp.int32, (LANES, LANES), 0)
        c = lax.broadcasted_iota(jnp.int32, (LANES, LANES), 1)
        before = jnp.dot(blocks, (r < c).astype(F32), precision=HIGHEST, preferred_element_type=F32)
        bst_ref[...] = before * float(MOE_ROWS)
        run_ref[...] = jnp.zeros_like(run_ref)

    @pl.when(phase == 1)
    def _():
        rt = lax.broadcasted_iota(jnp.int32, (tm, tm), 0)
        ct = lax.broadcasted_iota(jnp.int32, (tm, tm), 1)
        earlier = _dot((ct < rt).astype(BF16), chosen.astype(BF16))
        row_of = bst_ref[0:1, :] + run_ref[0:1, :] + earlier
        out = jnp.zeros((tm, LANES), F32)
        for kk in range(TOP_K):
            dk = jnp.sum(jnp.where(hits[kk], row_of, 0.0), axis=-1, keepdims=True)
            out = jnp.where(lane == kk, dk, out)
        dest_ref[...] = out.T[0:SUBLANES, :].astype(jnp.int32)
        run_ref[...] = run_ref[...] + colsum


def _route(ridx):
    t_all = ridx.shape[0]
    tm = TM_ROUTE
    return pl.pallas_call(
        _route_kernel,
        grid=(2, t_all // tm),
        in_specs=[pl.BlockSpec((tm, LANES), lambda p, i: (i, 0))],
        out_specs=[pl.BlockSpec((SUBLANES, tm), lambda p, i: (0, i * p)),
                   pl.BlockSpec((SUBLANES, LANES), lambda p, i: (0, 0))],
        out_shape=[jax.ShapeDtypeStruct((SUBLANES, t_all), jnp.int32),
                   jax.ShapeDtypeStruct((SUBLANES, LANES), F32)],
        scratch_shapes=[pltpu.VMEM((SUBLANES, LANES), F32),
                        pltpu.VMEM((SUBLANES, LANES), F32)],
        compiler_params=pltpu.CompilerParams(
            dimension_semantics=("arbitrary", "arbitrary"), vmem_limit_bytes=VMEM_LIMIT),
        name="moe_route",
    )(ridx)


def _routing_tables(counts, n_blocks):
    experts = jnp.arange(N_EXPERTS, dtype=jnp.int32)
    blocks_per = (counts + MOE_ROWS - 1) // MOE_ROWS
    bends = jnp.cumsum(blocks_per)
    bstarts = bends - blocks_per
    blocks = jnp.arange(n_blocks, dtype=jnp.int32)
    block_expert = jnp.minimum(
        jnp.sum((bends[None, :] <= blocks[:, None]).astype(jnp.int32), axis=1), N_EXPERTS - 1)
    n_used = bends[-1:].astype(jnp.int32)
    owner = block_expert[:, None] == experts[None, :]

    def per_block(table):
        return jnp.sum(jnp.where(owner, table[None, :], 0), axis=1)

    n_valid = jnp.clip(per_block(counts) - (blocks - per_block(bstarts)) * MOE_ROWS,
                       0, MOE_ROWS).astype(jnp.int32)
    present = blocks_per > 0
    ordinal = jnp.cumsum(present.astype(jnp.int32)) - 1
    later = lax.cummin(jnp.where(present, experts, N_EXPERTS), reverse=True)
    succ = jnp.concatenate([later[1:], jnp.full((1,), N_EXPERTS, jnp.int32)])
    succ = jnp.where(succ >= N_EXPERTS, -1, succ)
    stage_slot = (per_block(ordinal) % 2).astype(jnp.int32)
    next_expert = per_block(succ).astype(jnp.int32)
    return (block_expert.astype(jnp.int32), n_used, n_valid, stage_slot, next_expert)


def _pos_embed_2d(n_tokens):
    rows = n_tokens // GRID_W
    r = np.repeat(np.arange(rows), GRID_W).astype(np.float32)
    col = np.tile(np.arange(GRID_W), rows).astype(np.float32)
    quarter = D_MODEL // 4
    omega = (np.float32(1.0)
             / np.power(np.float32(10000.0), np.arange(quarter, dtype=np.float32) / np.float32(quarter)))
    er = (r[:, None] * omega).astype(np.float64)
    ec = (col[:, None] * omega).astype(np.float64)
    table = np.concatenate([np.sin(er), np.cos(er), np.sin(ec), np.cos(ec)], axis=-1)
    return jnp.asarray(table, dtype=F32)


def _split_w_in(w):
    wt = w.T
    o_r = 2 * GLA_DK + 2 * GLA_DV
    o_f = o_r + DECAY_RANK
    o_gate = o_f + FNET_DIM
    row_scale = jnp.where(jnp.arange(o_r) < GLA_DK, DK_HEAD ** -0.5, 1.0).astype(w.dtype)
    w_main = (wt[:o_r] * row_scale[:, None]).astype(BF16)
    pad = jnp.zeros((LANES - DECAY_RANK, w.shape[0]), BF16)
    w_tail = jnp.concatenate([wt[o_gate:].astype(BF16), wt[o_f:o_gate].astype(BF16),
                              wt[o_r:o_f].astype(BF16), pad], axis=0)
    return w_main, w_tail


def kernel(x_prompt, x_sample, state_gla, c, c_ctx, w_ada, b_ada, w_in, w_dec_fwd, b_dec_fwd,
           w_dec_bwd, b_dec_bwd, gla_norm_g, w_br_gla, w_br_fnet, w_out, ln1_g, ln1_b, w_router,
           b_router, w_gate_up, b_gate_up, w_down, b_down, ln2_g, ln2_b):
    n_req, ctx_len, _ = x_prompt.shape
    n_lat, lat_len, _ = x_sample.shape
    depth = w_in.shape[0]
    alpha = (2.0 * depth) ** 0.25
    t_ctx = n_req * ctx_len
    t_lat = n_lat * lat_len
    t_all = t_ctx + t_lat

    x_ctx = x_prompt.reshape(t_ctx, D_MODEL)
    x_lat = x_sample.reshape(t_lat, D_MODEL)
    pos = _pos_embed_2d(lat_len)
    zero_pos = jnp.zeros_like(pos)

    cond_rows = -(-(n_lat + 1) // SUBLANES) * SUBLANES
    cond = jnp.zeros((cond_rows, D_MODEL), F32).at[:n_lat].set(c).at[cond_rows - 1].set(c_ctx)

    n_moe_blocks = (t_all * TOP_K) // MOE_ROWS + N_EXPERTS
    tok_chunks = t_all // (SC_WORKERS * SC_WINDOW)
    states = []
    for l in range(depth):
        mod = _ada(cond, w_ada[l], b_ada[l]).reshape(cond_rows, N_MOD, D_MODEL)
        layer_pos = pos if l == 0 else zero_pos
        w_dec = jnp.zeros((2, LANES, GLA_DK), F32)
        w_dec = w_dec.at[0, :DECAY_RANK].set(w_dec_fwd[l]).at[1, :DECAY_RANK].set(w_dec_bwd[l])
        b_dec = jnp.stack([b_dec_fwd[l], b_dec_bwd[l]]).reshape(2, 1, GLA_DK)
        proj, la = _inproj(x_ctx, x_lat, layer_pos, mod, *_split_w_in(w_in[l]), w_dec, b_dec, lat_len)

        norm_g = gla_norm_g[l].reshape(1, DV_HEAD)
        o_ctx, s_new = _gla(proj, la, norm_g, None, n_seq=n_req, seq_len=ctx_len, row0=0,
                            emit_state=True)
        (o_lat,) = _gla(proj, la, norm_g, state_gla[:, l], n_seq=n_lat, seq_len=lat_len, row0=t_ctx,
                        emit_state=False)
        states.append(s_new)

        mixed_ctx = _fnet(proj, n_seq=n_req, seq_len=ctx_len, row0=0)
        mixed_lat = _fnet(proj, n_seq=n_lat, seq_len=lat_len, row0=t_ctx)

        wr = jnp.zeros((D_MODEL, LANES), F32).at[:, :N_EXPERTS].set(w_router[l])
        br = jnp.full((1, LANES), -1e30, F32).at[0, :N_EXPERTS].set(b_router[l])
        wr_hi = wr.astype(BF16)
        wr_lo = (wr - wr_hi.astype(F32)).astype(BF16)
        x1, h2, ridx, rw = _merge(
            x_ctx, x_lat, layer_pos, mod, o_ctx, o_lat, mixed_ctx, mixed_lat, proj,
            w_br_gla[l].astype(BF16), w_br_fnet[l].astype(BF16), (0.5 * w_out[l]).astype(BF16),
            ln1_g[l].reshape(1, D_MODEL), ln1_b[l].reshape(1, D_MODEL), wr_hi, wr_lo, br, lat_len,
            alpha)

        dest, counts = _route(ridx)
        moe_tables = _routing_tables(counts[0, :N_EXPERTS].astype(jnp.int32), n_moe_blocks)
        dest = dest[:TOP_K]
        scatter_idx = dest.reshape(TOP_K, SC_WORKERS, tok_chunks, SC_WINDOW)
        p_rows = n_moe_blocks * MOE_ROWS
        xs = _sc_scatter_rows(h2.reshape(t_all, ROW_TILES, LANES), scatter_idx, p_rows)
        yb = _moe(moe_tables, xs.reshape(p_rows * ROW_TILES, LANES),
                  w_gate_up[l], b_gate_up[l], w_down[l], b_down[l])
        yb = yb.reshape(p_rows, ROW_TILES, LANES)

        def gathered(row0, n_rows):
            idx = dest[:, row0:row0 + n_rows].reshape(SC_WORKERS, -1, SC_WINDOW)
            return _sc_gather_rows(yb, idx).reshape(TOP_K * n_rows * ROW_TILES, LANES)

        l2g = ln2_g[l].reshape(1, D_MODEL)
        l2b = ln2_b[l].reshape(1, D_MODEL)
        tiles_per_seq = lat_len // TM_MIX
        yg_ctx = gathered(0, t_ctx)
        yg_lat = gathered(t_ctx, t_lat)
        x_ctx = _combine(x1, yg_ctx, rw, mod, l2g, l2b, row0=0, n_rows=t_ctx,
                         mod_map=lambda i: (cond_rows - 1, 0, 0), alpha=alpha)
        x_lat = _combine(x1, yg_lat, rw, mod, l2g, l2b, row0=t_ctx, n_rows=t_lat,
                         mod_map=lambda i: (i // tiles_per_seq, 0, 0), alpha=alpha)

    y_prompt = x_ctx.reshape(x_prompt.shape)
    y_sample = x_lat.reshape(x_sample.shape)
    new_state = jnp.stack(states, axis=1).astype(x_prompt.dtype)
    return (y_prompt, y_sample, new_state)
```

```python
import functools
import math

import numpy as np
import jax
import jax.numpy as jnp
from jax import lax
from jax.experimental import pallas as pl
from jax.experimental.pallas import tpu as pltpu
from jax.experimental.pallas import tpu_sc as plsc

F32 = jnp.float32
BF16 = jnp.bfloat16

D_MODEL = 1024
GRID_W = 64
GLA_HEADS = 4
DK_HEAD = 128
DV_HEAD = 256
GLA_DK = GLA_HEADS * DK_HEAD
GLA_DV = GLA_HEADS * DV_HEAD
DECAY_RANK = 16
GATE_NORMALIZER = 16.0
FNET_GROUPS = 4
FNET_GROUP_DIM = 128
FNET_DIM = FNET_GROUPS * FNET_GROUP_DIM
N_EXPERTS = 32
TOP_K = 4
D_EXPERT = 1024
SWIGLU_LIMIT = 7.0
SWIGLU_ALPHA = 1.702
LN_EPS = 1e-6
N_MOD = 6

LANES = 128
SUBLANES = 8
HALF_MODEL = D_MODEL // 2
ROW_TILES = HALF_MODEL // LANES
ROW_DTYPE = jnp.uint32
COL_Q = 0
COL_K = GLA_DK
COL_V = 2 * GLA_DK
COL_G = COL_V + GLA_DV
COL_GATE_A = COL_G + GLA_DV
COL_GATE_B = COL_GATE_A + D_MODEL
COL_F = COL_GATE_B + D_MODEL
COL_R = COL_F + FNET_DIM
PROJ_COLS = COL_R + LANES

GLA_CHUNK = 128
GLA_LEAF = 16
assert GLA_CHUNK == DK_HEAD
TM_PROJ = 512
PROJ_GROUP = 256
TM_MIX = 256
TM_MERGE = 512
TM_MERGE_GROUP = 256
TM_ROUTE = 1024
MOE_ROWS = 512
MOE_STEP = 128
VMEM_LIMIT = 56 * 1024 * 1024

SC_CORES = 2
SC_SUBCORES = 16
SC_WORKERS = SC_CORES * SC_SUBCORES
SC_WINDOW = 64

HIGHEST = lax.Precision.HIGHEST


def _layer_norm(x):
    mu = jnp.mean(x, axis=-1, keepdims=True)
    xc = x - mu
    var = jnp.mean(xc * xc, axis=-1, keepdims=True)
    return xc * lax.rsqrt(var + LN_EPS)


def _sigmoid(x):
    return 0.5 * jnp.tanh(0.5 * x) + 0.5


def _log_sigmoid(z):
    return jnp.minimum(z, 0.0) - jnp.log(1.0 + jnp.exp(-jnp.abs(z)))


def _dot(a, b):
    return jnp.dot(a, b, preferred_element_type=F32)


def _split_bf16(x, terms):
    parts = []
    for _ in range(terms):
        p = x.astype(BF16)
        parts.append(p)
        x = x - p.astype(F32)
    return parts


def _dot_nt(a, b):
    return lax.dot_general(a, b, (((1,), (1,)), ((), ())), preferred_element_type=F32)


def _dot_tn(a, b):
    return lax.dot_general(a, b, (((0,), (0,)), ((), ())), preferred_element_type=F32)


def _row_tile_slice(j, n_rows, first_row=0):
    return pl.ds(first_row * ROW_TILES + j, n_rows, stride=ROW_TILES)


def _store_row_tiles(ref, val, first_row=0):
    for j in range(ROW_TILES):
        lo = val[:, j * LANES:(j + 1) * LANES]
        hi = val[:, HALF_MODEL + j * LANES:HALF_MODEL + (j + 1) * LANES]
        ref[_row_tile_slice(j, val.shape[0], first_row), :] = pltpu.pack_elementwise(
            [lo, hi], packed_dtype=BF16)


def _load_row_tile(ref, j, n_rows, first_row=0):
    words = ref[_row_tile_slice(j, n_rows, first_row), :]
    return tuple(pltpu.unpack_elementwise(words, index=half, packed_dtype=BF16, unpacked_dtype=F32)
                 for half in range(2))


def _ada_kernel(c_ref, w_ref, b_ref, o_ref):
    c = c_ref[...]
    s = c * _sigmoid(c)
    o_ref[...] = _dot(s.astype(BF16), w_ref[...].astype(BF16)) + b_ref[...]


def _ada(cond, w_ada, b_ada):
    rows = cond.shape[0]
    n = w_ada.shape[1]
    tn = 1536
    return pl.pallas_call(
        _ada_kernel,
        grid=(n // tn,),
        in_specs=[pl.BlockSpec((rows, D_MODEL), lambda j: (0, 0)),
                  pl.BlockSpec((D_MODEL, tn), lambda j: (0, j)),
                  pl.BlockSpec((1, tn), lambda j: (0, j))],
        out_specs=pl.BlockSpec((rows, tn), lambda j: (0, j)),
        out_shape=jax.ShapeDtypeStruct((rows, n), F32),
        compiler_params=pltpu.CompilerParams(vmem_limit_bytes=VMEM_LIMIT),
        name="ada_mod",
    )(cond, w_ada, b_ada.reshape(1, n))


def _group_maps(n_ctx_tiles):
    def ctx_map(i, *_):
        return (jnp.minimum(i, n_ctx_tiles - 1), 0)

    def lat_map(i, *_):
        return (jnp.maximum(i - n_ctx_tiles, 0), 0)

    return ctx_map, lat_map


def _token_specs(tm, n_ctx_tiles, tiles_per_latent_seq, ctx_mod_row):
    ctx_map, lat_map = _group_maps(n_ctx_tiles)

    def pos_map(i, *_):
        return (jnp.maximum(i - n_ctx_tiles, 0) % tiles_per_latent_seq, 0)

    def mod_map(i, *_):
        return (jnp.where(i < n_ctx_tiles, ctx_mod_row,
                          jnp.maximum(i - n_ctx_tiles, 0) // tiles_per_latent_seq), 0, 0)

    return [pl.BlockSpec((tm, D_MODEL), ctx_map),
            pl.BlockSpec((tm, D_MODEL), lat_map),
            pl.BlockSpec((tm, D_MODEL), pos_map),
            pl.BlockSpec((1, N_MOD, D_MODEL), mod_map)]


def _inproj_kernel(xc_ref, xl_ref, pos_ref, mod_ref, wm_ref, wt_ref, wdh_ref, wdl_ref, bd_ref,
                   o_ref, la_ref, *, n_ctx_tiles):
    tail = lambda lo, hi: wt_ref[lo - COL_GATE_A:hi - COL_GATE_A, :]
    i = pl.program_id(0)

    def project(x_of):
        groups = [slice(r0, r0 + PROJ_GROUP) for r0 in range(0, TM_PROJ, PROJ_GROUP)]
        hs = [(_layer_norm(x_of(rows)) * (1.0 + mod_ref[0, 1:2, :]) + mod_ref[0, 0:1, :]).astype(BF16)
              for rows in groups]
        rs = [_split_bf16(_dot_nt(h, tail(COL_R, PROJ_COLS)), 2) for h in hs]
        for rows, (r_hi, r_lo) in zip(groups, rs):
            for d in range(2):
                z = (_dot(r_hi, wdh_ref[d]) + _dot(r_lo, wdh_ref[d]) + _dot(r_hi, wdl_ref[d])
                     + bd_ref[d])
                la_ref[rows, d * GLA_DK:(d + 1) * GLA_DK] = _log_sigmoid(z) * (1.0 / GATE_NORMALIZER)
        for rows, h in zip(groups, hs):
            hg = 0.5 * _dot_nt(h, wm_ref[COL_G:COL_GATE_A, :])
            o_ref[rows, COL_G:COL_GATE_A] = hg * (jnp.tanh(hg) + 1.0)
        for rows, h in zip(groups, hs):
            gates = _dot_nt(h, tail(COL_GATE_A, COL_F))
            o_ref[rows, COL_GATE_A:COL_F] = jnp.tanh(0.5 * gates) + 1.0
        for rows, h in zip(groups, hs):
            o_ref[rows, :COL_G] = _dot_nt(h, wm_ref[:COL_G, :])
        for rows, h in zip(groups, hs):
            o_ref[rows, COL_F:COL_R] = _dot_nt(h, tail(COL_F, COL_R))

    @pl.when(i < n_ctx_tiles)
    def _():
        project(lambda rows: xc_ref[rows, :])

    @pl.when(i >= n_ctx_tiles)
    def _():
        project(lambda rows: xl_ref[rows, :] + pos_ref[rows, :])


def _inproj(x_ctx, x_lat, pos, mod, w_main, w_tail, w_dec, b_dec, lat_len):
    t_ctx, t_lat = x_ctx.shape[0], x_lat.shape[0]
    t_all = t_ctx + t_lat
    n_ctx_tiles = t_ctx // TM_PROJ
    kern = functools.partial(_inproj_kernel, n_ctx_tiles=n_ctx_tiles)
    specs = _token_specs(TM_PROJ, n_ctx_tiles, lat_len // TM_PROJ, mod.shape[0] - 1)
    w_dec_hi = w_dec.astype(BF16)
    w_dec_lo = (w_dec - w_dec_hi.astype(F32)).astype(BF16)

    def const(shape):
        return pl.BlockSpec(shape, lambda i: (0,) * len(shape))

    return pl.pallas_call(
        kern,
        grid=(t_all // TM_PROJ,),
        in_specs=specs + [pl.BlockSpec(w_main.shape, lambda i: (0, 0), pipeline_mode=pl.Buffered(1)),
                          pl.BlockSpec(w_tail.shape, lambda i: (0, 0), pipeline_mode=pl.Buffered(1)),
                          const((2, LANES, GLA_DK)), const((2, LANES, GLA_DK)), const((2, 1, GLA_DK))],
        out_specs=[pl.BlockSpec((TM_PROJ, COL_R), lambda i: (i, 0)),
                   pl.BlockSpec((TM_PROJ, 2 * GLA_DK), lambda i: (i, 0))],
        out_shape=[jax.ShapeDtypeStruct((t_all, COL_R), F32),
                   jax.ShapeDtypeStruct((t_all, 2 * GLA_DK), F32)],
        compiler_params=pltpu.CompilerParams(
            dimension_semantics=("arbitrary",), vmem_limit_bytes=VMEM_LIMIT),
        name="ln_inproj",
    )(x_ctx, x_lat, pos, mod, w_main, w_tail, w_dec_hi, w_dec_lo, b_dec)


def _gla_kernel(*refs, seq_len, has_s0, emit_state):
    it = iter(refs)
    q_ref, k_ref, v_ref, laf_ref, lab_ref, g_ref = (next(it) for _ in range(6))
    s0_ref = next(it) if has_s0 else None
    o_ref = next(it)
    sout_ref = next(it) if emit_state else None
    cum_ref, aq_ref, ko_ref, dec_ref, op_ref, st_ref = (next(it) for _ in range(6))

    C = GLA_CHUNK
    n_chunks = seq_len // C

    def rows(n):
        if isinstance(n, int):
            return pl.ds(n * C, C)
        return pl.ds(pl.multiple_of(n * C, C), C)

    def loop(body):
        if n_chunks <= 2:
            for n in range(n_chunks):
                body(n)
        else:
            def step(m, carry):
                body(2 * m)
                body(2 * m + 1)
                return carry
            lax.fori_loop(0, n_chunks // 2, step, 0)

    rt = lax.broadcasted_iota(jnp.int32, (C, C), 0)
    ct = lax.broadcasted_iota(jnp.int32, (C, C), 1)
    tri = ((rt >= ct).astype(BF16), (ct >= rt).astype(BF16))
    row_id = lax.broadcasted_iota(jnp.int32, (C, DK_HEAD), 0)

    def cumsum_chunk(n):
        for d, la_ref in enumerate((laf_ref, lab_ref)):
            la_hi, la_lo = _split_bf16(la_ref[rows(n), :], 2)
            cum_ref[d, rows(n), :] = _dot(tri[d], la_hi) + _dot(tri[d], la_lo)

    loop(cumsum_chunk)

    query_rows = ({}, {})
    keep = ({}, {})
    for d in range(2):
        blk = C // 2
        while blk >= GLA_LEAF:
            q_parity = 1 if d == 0 else 0
            query_rows[d][blk] = ((row_id // blk) % 2) == q_parity
            qb, kb = rt // blk, ct // blk
            keep[d][blk] = ((qb % 2) == q_parity) & ((qb == kb + 1) if d == 0 else (kb == qb + 1))
            blk //= 2
        order = (rt >= ct) if d == 0 else (ct >= rt)
        keep[d][0] = ((rt // GLA_LEAF) == (ct // GLA_LEAF)) & order

    n_leaves = C // GLA_LEAF

    def per_leaf(rows_of_cum):
        return jnp.concatenate(rows_of_cum, axis=0)

    def expand(per_leaf_rows):
        return jnp.concatenate(
            [jnp.broadcast_to(per_leaf_rows[j:j + 1, :], (GLA_LEAF, DK_HEAD)) for j in range(n_leaves)],
            axis=0)

    def scores(n, d):
        cum = cum_ref[d, rows(n), :]
        mid = GLA_LEAF // 2 - 1 if d == 0 else GLA_LEAF // 2
        at_mid = per_leaf([cum[mid + j * GLA_LEAF:mid + j * GLA_LEAF + 1, :] for j in range(n_leaves)])
        e = cum - expand(at_mid)
        qe = q_ref[rows(n), :] * jnp.exp(e)
        ke = k_ref[rows(n), :] * jnp.exp(-e)
        acc = jnp.where(keep[d][0], _dot_nt(qe.astype(BF16), ke.astype(BF16)), 0.0)
        blk = C // 2
        while blk >= GLA_LEAF:
            bnd = blk - 1 if d == 0 else blk
            at_bnd = per_leaf([cum[bnd + (j * GLA_LEAF) // (2 * blk) * (2 * blk):
                                   bnd + (j * GLA_LEAF) // (2 * blk) * (2 * blk) + 1, :]
                               for j in range(n_leaves)])
            through = expand(jnp.exp(-jnp.abs(at_mid - at_bnd)))
            x = (jnp.where(query_rows[d][blk], qe, ke) * through).astype(BF16)
            acc = acc + jnp.where(keep[d][blk], _dot_nt(x, x), 0.0)
            blk //= 2
        aq_ref[d, rows(n), 0:C] = acc.astype(BF16)
        aq_ref[d, rows(n), C:C + DK_HEAD] = (qe * expand(jnp.exp(at_mid))).astype(BF16)
        end = cum[C - 1:C, :] if d == 0 else cum[0:1, :]
        ko_ref[d, rows(n), :] = (ke * expand(jnp.exp(end - at_mid))).astype(BF16)
        dec_ref[d, rows(n), :] = jnp.broadcast_to(jnp.exp(end), (DK_HEAD, DK_HEAD)).T

    def scores_chunk(n):
        scores(n, 0)
        scores(n, 1)

    loop(scores_chunk)

    for d in range(2):
        if has_s0:
            st_ref[d] = s0_ref[0, d, 0]
        else:
            st_ref[d] = jnp.zeros((DK_HEAD, DV_HEAD), F32)

    def scan(n, d):
        v = v_ref[rows(n), :].astype(BF16)
        st = st_ref[d]
        op_ref[d, rows(n), :] = _dot(aq_ref[d, rows(n), :],
                                     jnp.concatenate([v, st.astype(BF16)], axis=0))
        dec = dec_ref[d, rows(n), :]
        st_ref[d] = (st * jnp.concatenate([dec] * (DV_HEAD // DK_HEAD), axis=1)
                     + _dot_tn(ko_ref[d, rows(n), :], v))

    def scan_chunk(m):
        scan(m, 0)
        scan(n_chunks - 1 - m, 1)

    loop(scan_chunk)
    if emit_state:
        sout_ref[0, 0, 0] = st_ref[0]
        sout_ref[0, 1, 0] = st_ref[1]

    g = g_ref[...]

    def finish_chunk(n):
        o = op_ref[0, rows(n), :] + op_ref[1, rows(n), :]
        ms = jnp.mean(o * o, axis=-1, keepdims=True)
        o_ref[rows(n), :] = o * lax.rsqrt(ms + LN_EPS) * g

    loop(finish_chunk)


def _gla(proj, la, g, s0, *, n_seq, seq_len, row0, emit_state):
    has_s0 = s0 is not None
    blk0 = row0 // seq_len
    kern = functools.partial(_gla_kernel, seq_len=seq_len, has_s0=has_s0, emit_state=emit_state)
    in_specs = [
        pl.BlockSpec((seq_len, DK_HEAD), lambda b, h: (blk0 + b, COL_Q // DK_HEAD + h)),
        pl.BlockSpec((seq_len, DK_HEAD), lambda b, h: (blk0 + b, COL_K // DK_HEAD + h)),
        pl.BlockSpec((seq_len, DV_HEAD), lambda b, h: (blk0 + b, COL_V // DV_HEAD + h)),
        pl.BlockSpec((seq_len, DK_HEAD), lambda b, h: (blk0 + b, h)),
        pl.BlockSpec((seq_len, DK_HEAD), lambda b, h: (blk0 + b, GLA_HEADS + h)),
        pl.BlockSpec((1, DV_HEAD), lambda b, h: (0, 0)),
    ]
    args = [proj, proj, proj, la, la, g]
    if has_s0:
        in_specs.append(pl.BlockSpec((1, 2, 1, DK_HEAD, DV_HEAD), lambda b, h: (b, 0, h, 0, 0)))
        args.append(s0)
    out_specs = [pl.BlockSpec((seq_len, DV_HEAD), lambda b, h: (b, h))]
    out_shape = [jax.ShapeDtypeStruct((n_seq * seq_len, GLA_DV), F32)]
    if emit_state:
        out_specs.append(pl.BlockSpec((1, 2, 1, DK_HEAD, DV_HEAD), lambda b, h: (b, 0, h, 0, 0)))
        out_shape.append(jax.ShapeDtypeStruct((n_seq, 2, GLA_HEADS, DK_HEAD, DV_HEAD), F32))

    res = pl.pallas_call(
        kern,
        grid=(n_seq, GLA_HEADS),
        in_specs=in_specs,
        out_specs=out_specs,
        out_shape=out_shape,
        scratch_shapes=[pltpu.VMEM((2, seq_len, DK_HEAD), F32),
                        pltpu.VMEM((2, seq_len, GLA_CHUNK + DK_HEAD), BF16),
                        pltpu.VMEM((2, seq_len, DK_HEAD), BF16),
                        pltpu.VMEM((2, seq_len, DK_HEAD), F32),
                        pltpu.VMEM((2, seq_len, DV_HEAD), F32),
                        pltpu.VMEM((2, DK_HEAD, DV_HEAD), F32)],
        compiler_params=pltpu.CompilerParams(
            dimension_semantics=("arbitrary", "arbitrary"), vmem_limit_bytes=VMEM_LIMIT),
        name="gla_seq%d" % seq_len,
    )(*args)
    return res


def _fnet_kernel(f_ref, cl_ref, sl_ref, cg_ref, sg_ref, o_ref, uc_ref, us_ref, *, seq_len):
    cg = cg_ref[...]
    sg = sg_ref[...]
    for grp in range(FNET_GROUPS):
        lo = grp * FNET_GROUP_DIM
        u = f_ref[:, lo:lo + FNET_GROUP_DIM].astype(BF16)
        uc_ref[:, lo:lo + FNET_GROUP_DIM] = _dot(u, cg).astype(BF16)
        us_ref[:, lo:lo + FNET_GROUP_DIM] = _dot(u, sg).astype(BF16)
    mixed = _dot(cl_ref[...], uc_ref[...]) - _dot(sl_ref[...], us_ref[...])
    o_ref[...] = mixed * (1.0 / math.sqrt(seq_len * FNET_GROUP_DIM))


def _dft_mats(n):
    j = np.arange(n, dtype=np.int64)
    ang = (2.0 * np.pi / n) * ((j[:, None] * j[None, :]) % n).astype(np.float64)
    return (jnp.asarray(np.cos(ang), dtype=F32).astype(BF16),
            jnp.asarray(np.sin(ang), dtype=F32).astype(BF16))


def _fnet(proj, *, n_seq, seq_len, row0):
    blk0 = row0 // seq_len
    cl, sl = _dft_mats(seq_len)
    cg, sg = _dft_mats(FNET_GROUP_DIM)
    kern = functools.partial(_fnet_kernel, seq_len=seq_len)
    return pl.pallas_call(
        kern,
        grid=(n_seq,),
        in_specs=[pl.BlockSpec((seq_len, FNET_DIM), lambda b: (blk0 + b, COL_F // FNET_DIM)),
                  pl.BlockSpec((seq_len, seq_len), lambda b: (0, 0)),
                  pl.BlockSpec((seq_len, seq_len), lambda b: (0, 0)),
                  pl.BlockSpec((FNET_GROUP_DIM, FNET_GROUP_DIM), lambda b: (0, 0)),
                  pl.BlockSpec((FNET_GROUP_DIM, FNET_GROUP_DIM), lambda b: (0, 0))],
        out_specs=pl.BlockSpec((seq_len, FNET_DIM), lambda b: (b, 0)),
        out_shape=jax.ShapeDtypeStruct((n_seq * seq_len, FNET_DIM), F32),
        scratch_shapes=[pltpu.VMEM((seq_len, FNET_DIM), BF16),
                        pltpu.VMEM((seq_len, FNET_DIM), BF16)],
        compiler_params=pltpu.CompilerParams(
            dimension_semantics=("arbitrary",), vmem_limit_bytes=VMEM_LIMIT),
        name="fnet_seq%d" % seq_len,
    )(proj, cl, sl, cg, sg)


def _merge_kernel(xc_ref, xl_ref, pos_ref, mod_ref, oc_ref, ol_ref, mc_ref, ml_ref,
                  g_ref, ga_ref, gb_ref, wbg_ref, wbf_ref, wo_ref, l1g_ref, l1b_ref, wrh_ref, wrl_ref,
                  br_ref, x1_ref, h2_ref, ridx_ref, rw_ref, *, n_ctx_tiles, alpha):
    i = pl.program_id(0)

    tm = x1_ref.shape[0]
    groups = [slice(r0, r0 + TM_MERGE_GROUP) for r0 in range(0, tm, TM_MERGE_GROUP)]
    shape = (TM_MERGE_GROUP, LANES)
    lane_i = lax.broadcasted_iota(jnp.int32, shape, 1)
    lane = lane_i.astype(F32)

    def compute(x_of, o_ref, mx_ref):
        branch = []
        for rows in groups:
            a = (o_ref[rows, :] * g_ref[rows, :]).astype(BF16)
            branch.append((_dot(a, wbg_ref[...]), _dot(mx_ref[rows, :].astype(BF16), wbf_ref[...])))
        mix = []
        for rows, (gla_out, fnet_out) in zip(groups, branch):
            merged = ga_ref[rows, :] * gla_out + gb_ref[rows, :] * fnet_out
            mix.append(_dot(merged.astype(BF16), wo_ref[...]))
        logits = []
        for rows, mix_g in zip(groups, mix):
            y = alpha * x_of(rows) + mod_ref[0, 2:3, :] * mix_g
            x1 = _layer_norm(y) * l1g_ref[...] + l1b_ref[...]
            x1_ref[rows, :] = x1
            h2 = _layer_norm(x1) * (1.0 + mod_ref[0, 4:5, :]) + mod_ref[0, 3:4, :]
            _store_row_tiles(h2_ref, h2, rows.start)
            h_hi, h_lo = _split_bf16(h2, 2)
            logits.append(_dot(h_hi, wrh_ref[...]) + _dot(h_lo, wrh_ref[...])
                          + _dot(h_hi, wrl_ref[...]) + br_ref[...])
        idx_out = [jnp.zeros(shape, F32) for _ in groups]
        val_out = [jnp.zeros(shape, F32) for _ in groups]
        top0 = [None] * len(groups)
        denom = [None] * len(groups)
        for kk in range(TOP_K):
            for gi in range(len(groups)):
                m = jnp.max(logits[gi], axis=-1, keepdims=True)
                sel = jnp.min(jnp.where(logits[gi] == m, lane, float(LANES)), axis=-1, keepdims=True)
                if kk == 0:
                    top0[gi] = m
                    p = jnp.ones_like(m)
                    denom[gi] = p
                else:
                    p = jnp.exp(m - top0[gi])
                    denom[gi] = denom[gi] + p
                idx_out[gi] = jnp.where(lane_i == kk, sel, idx_out[gi])
                val_out[gi] = jnp.where(lane_i == kk, p, val_out[gi])
                logits[gi] = jnp.where(lane == sel, -jnp.inf, logits[gi])
        for gi, rows in enumerate(groups):
            ridx_ref[rows, :] = idx_out[gi].astype(jnp.int32)
            rw_ref[rows, :] = val_out[gi] / denom[gi]

    @pl.when(i < n_ctx_tiles)
    def _():
        compute(lambda rows: xc_ref[rows, :], oc_ref, mc_ref)

    @pl.when(i >= n_ctx_tiles)
    def _():
        compute(lambda rows: xl_ref[rows, :] + pos_ref[rows, :], ol_ref, ml_ref)


def _merge(x_ctx, x_lat, pos, mod, o_ctx, o_lat, mixed_ctx, mixed_lat, proj,
           wbg, wbf, wo, l1g, l1b, wr_hi, wr_lo, br, lat_len, alpha):
    t_ctx, t_lat = x_ctx.shape[0], x_lat.shape[0]
    t_all = t_ctx + t_lat
    tm = TM_MERGE
    n_ctx_tiles = t_ctx // tm
    kern = functools.partial(_merge_kernel, n_ctx_tiles=n_ctx_tiles, alpha=alpha)
    specs = _token_specs(tm, n_ctx_tiles, lat_len // tm, mod.shape[0] - 1)
    ctx_map, lat_map = _group_maps(n_ctx_tiles)

    def const(shape):
        return pl.BlockSpec(shape, lambda i: (0,) * len(shape))

    in_specs = specs + [
        pl.BlockSpec((tm, GLA_DV), ctx_map),
        pl.BlockSpec((tm, GLA_DV), lat_map),
        pl.BlockSpec((tm, FNET_DIM), ctx_map),
        pl.BlockSpec((tm, FNET_DIM), lat_map),
        pl.BlockSpec((tm, GLA_DV), lambda i: (i, COL_G // GLA_DV)),
        pl.BlockSpec((tm, D_MODEL), lambda i: (i, COL_GATE_A // D_MODEL)),
        pl.BlockSpec((tm, D_MODEL), lambda i: (i, COL_GATE_B // D_MODEL)),
        const((GLA_DV, D_MODEL)), const((FNET_DIM, D_MODEL)), const((D_MODEL, D_MODEL)),
        const((1, D_MODEL)), const((1, D_MODEL)),
        const((D_MODEL, LANES)), const((D_MODEL, LANES)), const((1, LANES)),
    ]
    out_specs = [pl.BlockSpec((tm, D_MODEL), lambda i: (i, 0)),
                 pl.BlockSpec((tm * ROW_TILES, LANES), lambda i: (i, 0)),
                 pl.BlockSpec((tm, LANES), lambda i: (i, 0)),
                 pl.BlockSpec((tm, LANES), lambda i: (i, 0))]
    out_shape = [jax.ShapeDtypeStruct((t_all, D_MODEL), F32),
                 jax.ShapeDtypeStruct((t_all * ROW_TILES, LANES), ROW_DTYPE),
                 jax.ShapeDtypeStruct((t_all, LANES), jnp.int32),
                 jax.ShapeDtypeStruct((t_all, LANES), F32)]
    return pl.pallas_call(
        kern,
        grid=(t_all // tm,),
        in_specs=in_specs,
        out_specs=out_specs,
        out_shape=out_shape,
        compiler_params=pltpu.CompilerParams(
            dimension_semantics=("arbitrary",), vmem_limit_bytes=VMEM_LIMIT),
        name="merge_ln1_router",
    )(x_ctx, x_lat, pos, mod, o_ctx, o_lat, mixed_ctx, mixed_lat, proj, proj, proj,
      wbg, wbf, wo, l1g, l1b, wr_hi, wr_lo, br)


def _sc_mesh():
    return plsc.VectorSubcoreMesh(core_axis_name="c", subcore_axis_name="s")


def _sc_worker_id():
    return lax.axis_index("s") * SC_CORES + lax.axis_index("c")


def _sc_scatter_rows(src, idx, n_out):
    n_src = src.shape[0]
    w = SC_WINDOW
    n_chunks = n_src // (SC_WORKERS * w)
    copies = idx.shape[0]
    assert n_chunks % 2 == 0 and idx.shape == (copies, SC_WORKERS, n_chunks, w)

    @functools.partial(
        pl.kernel, mesh=_sc_mesh(),
        out_type=jax.ShapeDtypeStruct((n_out, ROW_TILES, LANES), ROW_DTYPE),
        scratch_types=[pltpu.VMEM((copies * n_chunks, w), jnp.int32),
                       pltpu.VMEM((2, w, ROW_TILES, LANES), ROW_DTYPE),
                       pltpu.SemaphoreType.DMA((2,)),
                       pltpu.SemaphoreType.DMA((2,))],
        name="moe_dispatch_scatter")
    def k(src_hbm, idx_hbm, out_hbm, idx_v, rows_v, rsem, wsem):
        wid = _sc_worker_id()
        base = wid * (n_chunks * w)
        for kk in range(copies):
            pltpu.sync_copy(idx_hbm.at[kk, wid], idx_v.at[pl.ds(kk * n_chunks, n_chunks)])

        def read(j, slot):
            return pltpu.make_async_copy(src_hbm.at[pl.ds(base + j * w, w)], rows_v.at[slot],
                                         rsem.at[slot])

        def scatter(j, kk, slot):
            return pltpu.make_async_copy(rows_v.at[slot], out_hbm.at[idx_v.at[kk * n_chunks + j]],
                                         wsem.at[slot])

        read(0, 0).start()

        @pl.loop(0, n_chunks, step=2)
        def _(jj):
            read(jj, 0).wait()

            @pl.when(jj > 0)
            def _():
                for kk in range(copies):
                    scatter(jj - 1, kk, 1).wait()

            read(jj + 1, 1).start()
            for kk in range(copies):
                scatter(jj, kk, 0).start()
            read(jj + 1, 1).wait()
            for kk in range(copies):
                scatter(jj, kk, 0).wait()

            @pl.when(jj + 2 < n_chunks)
            def _():
                read(jj + 2, 0).start()

            for kk in range(copies):
                scatter(jj + 1, kk, 1).start()

        for kk in range(copies):
            scatter(n_chunks - 1, kk, 1).wait()

    return k(src, idx)


def _sc_gather_rows(table, idx):
    _, n_chunks, w = idx.shape
    assert n_chunks % 2 == 0 and idx.shape[0] == SC_WORKERS and w == SC_WINDOW
    n_out = SC_WORKERS * n_chunks * w

    @functools.partial(
        pl.kernel, mesh=_sc_mesh(),
        out_type=jax.ShapeDtypeStruct((n_out, ROW_TILES, LANES), ROW_DTYPE),
        scratch_types=[pltpu.VMEM((n_chunks, w), jnp.int32),
                       pltpu.VMEM((2, w, ROW_TILES, LANES), ROW_DTYPE),
                       pltpu.SemaphoreType.DMA((2,)),
                       pltpu.SemaphoreType.DMA((2,))],
        name="moe_combine_gather")
    def k(table_hbm, idx_hbm, out_hbm, idx_v, rows_v, gsem, wsem):
        wid = _sc_worker_id()
        base = wid * (n_chunks * w)
        pltpu.sync_copy(idx_hbm.at[wid], idx_v)

        def gather(j, slot):
            return pltpu.make_async_copy(table_hbm.at[idx_v.at[j]], rows_v.at[slot], gsem.at[slot])

        def write(j, slot):
            return pltpu.make_async_copy(rows_v.at[slot], out_hbm.at[pl.ds(base + j * w, w)],
                                         wsem.at[slot])

        gather(0, 0).start()

        @pl.loop(0, n_chunks, step=2)
        def _(jj):
            gather(jj, 0).wait()

            @pl.when(jj > 0)
            def _():
                write(jj - 1, 1).wait()

            gather(jj + 1, 1).start()
            write(jj, 0).start()
            gather(jj + 1, 1).wait()
            write(jj, 0).wait()

            @pl.when(jj + 2 < n_chunks)
            def _():
                gather(jj + 2, 0).start()

            write(jj + 1, 1).start()

        write(n_chunks - 1, 1).wait()

    return k(table, idx)


def _moe_kernel(be_ref, nu_ref, nv_ref, slot_ref, nxt_ref, x_ref, wgu_hbm, bgu_ref, wd_hbm, bd_ref,
                o_ref, wgu_st, wd_st, wgu_bf, wd_bf, xb_ref, sem):
    b = pl.program_id(0)
    e = be_ref[b]
    prev = be_ref[jnp.maximum(b - 1, 0)]
    active = b < nu_ref[0]
    changed = (b == 0) | (e != prev)

    def weight_copies(expert, s):
        return (pltpu.make_async_copy(wgu_hbm.at[expert], wgu_st.at[s], sem.at[0, s]),
                pltpu.make_async_copy(wd_hbm.at[expert], wd_st.at[s], sem.at[1, s]))

    @pl.when(active & changed)
    def _():
        s = slot_ref[b]

        @pl.when(b == 0)
        def _():
            for cp in weight_copies(e, s):
                cp.start()

        for cp in weight_copies(e, s):
            cp.wait()
        wgu_bf[...] = wgu_st[s].astype(BF16)
        wd_bf[...] = wd_st[s].astype(BF16)
        nxt = nxt_ref[b]

        @pl.when(nxt >= 0)
        def _():
            for cp in weight_copies(nxt, 1 - s):
                cp.start()

    n_valid = nv_ref[b]

    def expert_mlp(n_rows):
        valid = lax.broadcasted_iota(jnp.int32, (n_rows, LANES), 0) < n_valid
        for j in range(ROW_TILES):
            for half, xj in enumerate(_load_row_tile(x_ref, j, n_rows)):
                c0 = half * HALF_MODEL + j * LANES
                xb_ref[0:n_rows, c0:c0 + LANES] = jnp.where(valid, xj, 0.0).astype(BF16)
        gu = _dot(xb_ref[0:n_rows, :], wgu_bf[...]) + bgu_ref[0]
        gate = jnp.minimum(gu[:, :D_EXPERT], SWIGLU_LIMIT)
        up = jnp.clip(gu[:, D_EXPERT:], -SWIGLU_LIMIT, SWIGLU_LIMIT)
        glu = gate * _sigmoid(gate * SWIGLU_ALPHA)
        act = ((up + 1.0) * glu).astype(BF16)
        _store_row_tiles(o_ref, _dot(act, wd_bf[...]) + bd_ref[0])

    for n_rows in range(MOE_STEP, MOE_ROWS + 1, MOE_STEP):
        @pl.when(active & (n_valid > n_rows - MOE_STEP) & (n_valid <= n_rows))
        def _(n_rows=n_rows):
            expert_mlp(n_rows)


def _moe(tables, xs, w_gate_up, b_gate_up, w_down, b_down):
    p_rows = xs.shape[0] // ROW_TILES
    n_blocks = p_rows // MOE_ROWS

    def blk(b, be, nu, *_):
        return jnp.minimum(b, nu[0] - 1)

    def expert(b, be, nu, *_):
        return (be[blk(b, be, nu)], 0, 0)

    def rows(b, be, nu, *_):
        return (blk(b, be, nu), 0)

    grid_spec = pltpu.PrefetchScalarGridSpec(
        num_scalar_prefetch=len(tables),
        grid=(n_blocks,),
        in_specs=[
            pl.BlockSpec((MOE_ROWS * ROW_TILES, LANES), rows),
            pl.BlockSpec(memory_space=pl.ANY),
            pl.BlockSpec((1, 1, 2 * D_EXPERT), expert),
            pl.BlockSpec(memory_space=pl.ANY),
            pl.BlockSpec((1, 1, D_MODEL), expert),
        ],
        out_specs=pl.BlockSpec((MOE_ROWS * ROW_TILES, LANES), rows),
        scratch_shapes=[pltpu.VMEM((2, D_MODEL, 2 * D_EXPERT), F32),
                        pltpu.VMEM((2, D_EXPERT, D_MODEL), F32),
                        pltpu.VMEM((D_MODEL, 2 * D_EXPERT), BF16),
                        pltpu.VMEM((D_EXPERT, D_MODEL), BF16),
                        pltpu.VMEM((MOE_ROWS, D_MODEL), BF16),
                        pltpu.SemaphoreType.DMA((2, 2))],
    )
    return pl.pallas_call(
        _moe_kernel,
        grid_spec=grid_spec,
        out_shape=jax.ShapeDtypeStruct((p_rows * ROW_TILES, LANES), ROW_DTYPE),
        compiler_params=pltpu.CompilerParams(
            dimension_semantics=("arbitrary",), vmem_limit_bytes=VMEM_LIMIT),
        name="moe_grouped_mlp",
    )(*tables, xs, w_gate_up, b_gate_up.reshape(N_EXPERTS, 1, 2 * D_EXPERT), w_down,
      b_down.reshape(N_EXPERTS, 1, D_MODEL))


def _combine_kernel(x1_ref, y0_ref, y1_ref, y2_ref, y3_ref, rw_ref, mod_ref, g_ref, b_ref, o_ref,
                    *, alpha):
    rw = rw_ref[...]
    y_refs = (y0_ref, y1_ref, y2_ref, y3_ref)
    pieces = [None] * (2 * ROW_TILES)
    for j in range(ROW_TILES):
        for kk in range(TOP_K):
            for half, yj in enumerate(_load_row_tile(y_refs[kk], j, rw.shape[0])):
                term = rw[:, kk:kk + 1] * yj
                slot = half * ROW_TILES + j
                pieces[slot] = term if kk == 0 else pieces[slot] + term
    ff = jnp.concatenate(pieces, axis=-1)
    y = alpha * x1_ref[...] + mod_ref[0, 5:6, :] * ff
    o_ref[...] = _layer_norm(y) * g_ref[...] + b_ref[...]


def _combine(x1, yg, rw, mod, l2g, l2b, *, row0, n_rows, mod_map, alpha):
    tm = TM_MIX
    t0 = row0 // tm
    tiles = n_rows // tm
    kern = functools.partial(_combine_kernel, alpha=alpha)

    def y_spec(kk):
        return pl.BlockSpec((tm * ROW_TILES, LANES), lambda i: (kk * tiles + i, 0))

    return pl.pallas_call(
        kern,
        grid=(n_rows // tm,),
        in_specs=[pl.BlockSpec((tm, D_MODEL), lambda i: (t0 + i, 0))]
        + [y_spec(kk) for kk in range(TOP_K)]
        + [pl.BlockSpec((tm, LANES), lambda i: (t0 + i, 0)),
           pl.BlockSpec((1, N_MOD, D_MODEL), mod_map),
           pl.BlockSpec((1, D_MODEL), lambda i: (0, 0)),
           pl.BlockSpec((1, D_MODEL), lambda i: (0, 0))],
        out_specs=pl.BlockSpec((tm, D_MODEL), lambda i: (i, 0)),
        out_shape=jax.ShapeDtypeStruct((n_rows, D_MODEL), F32),
        compiler_params=pltpu.CompilerParams(
            dimension_semantics=("arbitrary",), vmem_limit_bytes=VMEM_LIMIT),
        name="combine_ln2",
    )(x1, yg, yg, yg, yg, rw, mod, l2g, l2b)


def _route_kernel(ridx_ref, dest_ref, cnt_ref, run_ref, bst_ref):
    phase = pl.program_id(0)
    i = pl.program_id(1)
    tm = ridx_ref.shape[0]
    ridx = ridx_ref[...]
    lane = lax.broadcasted_iota(jnp.int32, (tm, LANES), 1)
    hits = [ridx[:, kk:kk + 1] == lane for kk in range(TOP_K)]
    chosen = jnp.where(hits[0], 1.0, 0.0)
    for kk in range(1, TOP_K):
        chosen = chosen + jnp.where(hits[kk], 1.0, 0.0)
    colsum = jnp.sum(chosen, axis=0, keepdims=True)

    @pl.when((phase == 0) & (i == 0))
    def _():
        run_ref[...] = jnp.zeros_like(run_ref)

    @pl.when(phase == 0)
    def _():
        run_ref[...] = run_ref[...] + colsum

    @pl.when((phase == 1) & (i == 0))
    def _():
        counts = run_ref[...]
        cnt_ref[...] = counts
        blocks = jnp.floor((counts + (MOE_ROWS - 1.0)) * (1.0 / MOE_ROWS))
        r = lax.broadcasted_iota(jnp.int32, (LANES, LANES), 0)
        c = lax.broadcasted_iota(jnp.int32, (LANES, LANES), 1)
        before = jnp.dot(blocks, (r < c).astype(F32), precision=HIGHEST, preferred_element_type=F32)
        bst_ref[...] = before * float(MOE_ROWS)
        run_ref[...] = jnp.zeros_like(run_ref)

    @pl.when(phase == 1)
    def _():
        rt = lax.broadcasted_iota(jnp.int32, (tm, tm), 0)
        ct = lax.broadcasted_iota(jnp.int32, (tm, tm), 1)
        earlier = _dot((ct < rt).astype(BF16), chosen.astype(BF16))
        row_of = bst_ref[0:1, :] + run_ref[0:1, :] + earlier
        out = jnp.zeros((tm, LANES), F32)
        for kk in range(TOP_K):
            dk = jnp.sum(jnp.where(hits[kk], row_of, 0.0), axis=-1, keepdims=True)
            out = jnp.where(lane == kk, dk, out)
        dest_ref[...] = out.T[0:SUBLANES, :].astype(jnp.int32)
        run_ref[...] = run_ref[...] + colsum


def _route(ridx):
    t_all = ridx.shape[0]
    tm = TM_ROUTE
    return pl.pallas_call(
        _route_kernel,
        grid=(2, t_all // tm),
        in_specs=[pl.BlockSpec((tm, LANES), lambda p, i: (i, 0))],
        out_specs=[pl.BlockSpec((SUBLANES, tm), lambda p, i: (0, i * p)),
                   pl.BlockSpec((SUBLANES, LANES), lambda p, i: (0, 0))],
        out_shape=[jax.ShapeDtypeStruct((SUBLANES, t_all), jnp.int32),
                   jax.ShapeDtypeStruct((SUBLANES, LANES), F32)],
        scratch_shapes=[pltpu.VMEM((SUBLANES, LANES), F32),
                        pltpu.VMEM((SUBLANES, LANES), F32)],
        compiler_params=pltpu.CompilerParams(
            dimension_semantics=("arbitrary", "arbitrary"), vmem_limit_bytes=VMEM_LIMIT),
        name="moe_route",
    )(ridx)


def _routing_tables(counts, n_blocks):
    experts = jnp.arange(N_EXPERTS, dtype=jnp.int32)
    blocks_per = (counts + MOE_ROWS - 1) // MOE_ROWS
    bends = jnp.cumsum(blocks_per)
    bstarts = bends - blocks_per
    blocks = jnp.arange(n_blocks, dtype=jnp.int32)
    block_expert = jnp.minimum(
        jnp.sum((bends[None, :] <= blocks[:, None]).astype(jnp.int32), axis=1), N_EXPERTS - 1)
    n_used = bends[-1:].astype(jnp.int32)
    owner = block_expert[:, None] == experts[None, :]

    def per_block(table):
        return jnp.sum(jnp.where(owner, table[None, :], 0), axis=1)

    n_valid = jnp.clip(per_block(counts) - (blocks - per_block(bstarts)) * MOE_ROWS,
                       0, MOE_ROWS).astype(jnp.int32)
    present = blocks_per > 0
    ordinal = jnp.cumsum(present.astype(jnp.int32)) - 1
    later = lax.cummin(jnp.where(present, experts, N_EXPERTS), reverse=True)
    succ = jnp.concatenate([later[1:], jnp.full((1,), N_EXPERTS, jnp.int32)])
    succ = jnp.where(succ >= N_EXPERTS, -1, succ)
    stage_slot = (per_block(ordinal) % 2).astype(jnp.int32)
    next_expert = per_block(succ).astype(jnp.int32)
    return (block_expert.astype(jnp.int32), n_used, n_valid, stage_slot, next_expert)


def _pos_embed_2d(n_tokens):
    rows = n_tokens // GRID_W
    r = np.repeat(np.arange(rows), GRID_W).astype(np.float32)
    col = np.tile(np.arange(GRID_W), rows).astype(np.float32)
    quarter = D_MODEL // 4
    omega = (np.float32(1.0)
             / np.power(np.float32(10000.0), np.arange(quarter, dtype=np.float32) / np.float32(quarter)))
    er = (r[:, None] * omega).astype(np.float64)
    ec = (col[:, None] * omega).astype(np.float64)
    table = np.concatenate([np.sin(er), np.cos(er), np.sin(ec), np.cos(ec)], axis=-1)
    return jnp.asarray(table, dtype=F32)


def _split_w_in(w):
    wt = w.T
    o_r = 2 * GLA_DK + 2 * GLA_DV
    o_f = o_r + DECAY_RANK
    o_gate = o_f + FNET_DIM
    row_scale = jnp.where(jnp.arange(o_r) < GLA_DK, DK_HEAD ** -0.5, 1.0).astype(w.dtype)
    w_main = (wt[:o_r] * row_scale[:, None]).astype(BF16)
    pad = jnp.zeros((LANES - DECAY_RANK, w.shape[0]), BF16)
    w_tail = jnp.concatenate([wt[o_gate:].astype(BF16), wt[o_f:o_gate].astype(BF16),
                              wt[o_r:o_f].astype(BF16), pad], axis=0)
    return w_main, w_tail


def kernel(x_prompt, x_sample, state_gla, c, c_ctx, w_ada, b_ada, w_in, w_dec_fwd, b_dec_fwd,
           w_dec_bwd, b_dec_bwd, gla_norm_g, w_br_gla, w_br_fnet, w_out, ln1_g, ln1_b, w_router,
           b_router, w_gate_up, b_gate_up, w_down, b_down, ln2_g, ln2_b):
    n_req, ctx_len, _ = x_prompt.shape
    n_lat, lat_len, _ = x_sample.shape
    depth = w_in.shape[0]
    alpha = (2.0 * depth) ** 0.25
    t_ctx = n_req * ctx_len
    t_lat = n_lat * lat_len
    t_all = t_ctx + t_lat

    x_ctx = x_prompt.reshape(t_ctx, D_MODEL)
    x_lat = x_sample.reshape(t_lat, D_MODEL)
    pos = _pos_embed_2d(lat_len)
    zero_pos = jnp.zeros_like(pos)

    cond_rows = -(-(n_lat + 1) // SUBLANES) * SUBLANES
    cond = jnp.zeros((cond_rows, D_MODEL), F32).at[:n_lat].set(c).at[cond_rows - 1].set(c_ctx)

    n_moe_blocks = (t_all * TOP_K) // MOE_ROWS + N_EXPERTS
    tok_chunks = t_all // (SC_WORKERS * SC_WINDOW)
    states = []
    for l in range(depth):
        mod = _ada(cond, w_ada[l], b_ada[l]).reshape(cond_rows, N_MOD, D_MODEL)
        layer_pos = pos if l == 0 else zero_pos
        w_dec = jnp.zeros((2, LANES, GLA_DK), F32)
        w_dec = w_dec.at[0, :DECAY_RANK].set(w_dec_fwd[l]).at[1, :DECAY_RANK].set(w_dec_bwd[l])
        b_dec = jnp.stack([b_dec_fwd[l], b_dec_bwd[l]]).reshape(2, 1, GLA_DK)
        proj, la = _inproj(x_ctx, x_lat, layer_pos, mod, *_split_w_in(w_in[l]), w_dec, b_dec, lat_len)

        norm_g = gla_norm_g[l].reshape(1, DV_HEAD)
        o_ctx, s_new = _gla(proj, la, norm_g, None, n_seq=n_req, seq_len=ctx_len, row0=0,
                            emit_state=True)
        (o_lat,) = _gla(proj, la, norm_g, state_gla[:, l], n_seq=n_lat, seq_len=lat_len, row0=t_ctx,
                        emit_state=False)
        states.append(s_new)

        mixed_ctx = _fnet(proj, n_seq=n_req, seq_len=ctx_len, row0=0)
        mixed_lat = _fnet(proj, n_seq=n_lat, seq_len=lat_len, row0=t_ctx)

        wr = jnp.zeros((D_MODEL, LANES), F32).at[:, :N_EXPERTS].set(w_router[l])
        br = jnp.full((1, LANES), -1e30, F32).at[0, :N_EXPERTS].set(b_router[l])
        wr_hi = wr.astype(BF16)
        wr_lo = (wr - wr_hi.astype(F32)).astype(BF16)
        x1, h2, ridx, rw = _merge(
            x_ctx, x_lat, layer_pos, mod, o_ctx, o_lat, mixed_ctx, mixed_lat, proj,
            w_br_gla[l].astype(BF16), w_br_fnet[l].astype(BF16), (0.5 * w_out[l]).astype(BF16),
            ln1_g[l].reshape(1, D_MODEL), ln1_b[l].reshape(1, D_MODEL), wr_hi, wr_lo, br, lat_len,
            alpha)

        dest, counts = _route(ridx)
        moe_tables = _routing_tables(counts[0, :N_EXPERTS].astype(jnp.int32), n_moe_blocks)
        dest = dest[:TOP_K]
        scatter_idx = dest.reshape(TOP_K, SC_WORKERS, tok_chunks, SC_WINDOW)
        p_rows = n_moe_blocks * MOE_ROWS
        xs = _sc_scatter_rows(h2.reshape(t_all, ROW_TILES, LANES), scatter_idx, p_rows)
        yb = _moe(moe_tables, xs.reshape(p_rows * ROW_TILES, LANES),
                  w_gate_up[l], b_gate_up[l], w_down[l], b_down[l])
        yb = yb.reshape(p_rows, ROW_TILES, LANES)

        def gathered(row0, n_rows):
            idx = dest[:, row0:row0 + n_rows].reshape(SC_WORKERS, -1, SC_WINDOW)
            return _sc_gather_rows(yb, idx).reshape(TOP_K * n_rows * ROW_TILES, LANES)

        l2g = ln2_g[l].reshape(1, D_MODEL)
        l2b = ln2_b[l].reshape(1, D_MODEL)
        tiles_per_seq = lat_len // TM_MIX
        yg_ctx = gathered(0, t_ctx)
        yg_lat = gathered(t_ctx, t_lat)
        x_ctx = _combine(x1, yg_ctx, rw, mod, l2g, l2b, row0=0, n_rows=t_ctx,
                         mod_map=lambda i: (cond_rows - 1, 0, 0), alpha=alpha)
        x_lat = _combine(x1, yg_lat, rw, mod, l2g, l2b, row0=t_ctx, n_rows=t_lat,
                         mod_map=lambda i: (i // tiles_per_seq, 0, 0), alpha=alpha)

    y_prompt = x_ctx.reshape(x_prompt.shape)
    y_sample = x_lat.reshape(x_sample.shape)
    new_state = jnp.stack(states, axis=1).astype(x_prompt.dtype)
    return (y_prompt, y_sample, new_state)
```

```python
import functools
import math

import numpy as np
import jax
import jax.numpy as jnp
from jax import lax
from jax.experimental import pallas as pl
from jax.experimental.pallas import tpu as pltpu
from jax.experimental.pallas import tpu_sc as plsc

F32 = jnp.float32
BF16 = jnp.bfloat16

D_MODEL = 1024
GRID_W = 64
GLA_HEADS = 4
DK_HEAD = 128
DV_HEAD = 256
GLA_DK = GLA_HEADS * DK_HEAD
GLA_DV = GLA_HEADS * DV_HEAD
DECAY_RANK = 16
GATE_NORMALIZER = 16.0
FNET_GROUPS = 4
FNET_GROUP_DIM = 128
FNET_DIM = FNET_GROUPS * FNET_GROUP_DIM
N_EXPERTS = 32
TOP_K = 4
D_EXPERT = 1024
SWIGLU_LIMIT = 7.0
SWIGLU_ALPHA = 1.702
LN_EPS = 1e-6
N_MOD = 6

LANES = 128
SUBLANES = 8
HALF_MODEL = D_MODEL // 2
ROW_TILES = HALF_MODEL // LANES
ROW_DTYPE = jnp.uint32
COL_Q = 0
COL_K = GLA_DK
COL_V = 2 * GLA_DK
COL_G = COL_V + GLA_DV
COL_GATE_A = COL_G + GLA_DV
COL_GATE_B = COL_GATE_A + D_MODEL
COL_F = COL_GATE_B + D_MODEL
COL_R = COL_F + FNET_DIM
PROJ_COLS = COL_R + LANES

GLA_CHUNK = 128
GLA_LEAF = 16
assert GLA_CHUNK == DK_HEAD
GLA_FAST_MAX_STEP_DECAY = 8.0
TM_PROJ = 512
PROJ_GROUP = 256
TM_MIX = 256
TM_MERGE = 512
TM_MERGE_GROUP = 256
TM_ROUTE = 1024
MOE_ROWS = 512
MOE_STEP = 128
VMEM_LIMIT = 56 * 1024 * 1024

SC_CORES = 2
SC_SUBCORES = 16
SC_WORKERS = SC_CORES * SC_SUBCORES
SC_WINDOW = 64

HIGHEST = lax.Precision.HIGHEST


def _layer_norm(x):
    mu = jnp.mean(x, axis=-1, keepdims=True)
    xc = x - mu
    var = jnp.mean(xc * xc, axis=-1, keepdims=True)
    return xc * lax.rsqrt(var + LN_EPS)


def _sigmoid(x):
    return 0.5 * jnp.tanh(0.5 * x) + 0.5


def _log_sigmoid(z):
    return jnp.minimum(z, 0.0) - jnp.log(1.0 + jnp.exp(-jnp.abs(z)))


def _dot(a, b):
    return jnp.dot(a, b, preferred_element_type=F32)


def _split_bf16(x, terms):
    parts = []
    for _ in range(terms):
        p = x.astype(BF16)
        parts.append(p)
        x = x - p.astype(F32)
    return parts


def _dot_nt(a, b):
    return lax.dot_general(a, b, (((1,), (1,)), ((), ())), preferred_element_type=F32)


def _dot_tn(a, b):
    return lax.dot_general(a, b, (((0,), (0,)), ((), ())), preferred_element_type=F32)


def _row_tile_slice(j, n_rows, first_row=0):
    return pl.ds(first_row * ROW_TILES + j, n_rows, stride=ROW_TILES)


def _store_row_tiles(ref, val, first_row=0):
    for j in range(ROW_TILES):
        lo = val[:, j * LANES:(j + 1) * LANES]
        hi = val[:, HALF_MODEL + j * LANES:HALF_MODEL + (j + 1) * LANES]
        ref[_row_tile_slice(j, val.shape[0], first_row), :] = pltpu.pack_elementwise(
            [lo, hi], packed_dtype=BF16)


def _load_row_tile(ref, j, n_rows, first_row=0):
    words = ref[_row_tile_slice(j, n_rows, first_row), :]
    return tuple(pltpu.unpack_elementwise(words, index=half, packed_dtype=BF16, unpacked_dtype=F32)
                 for half in range(2))


def _ada_kernel(c_ref, w_ref, b_ref, o_ref):
    c = c_ref[...]
    s = c * _sigmoid(c)
    o_ref[...] = _dot(s.astype(BF16), w_ref[...].astype(BF16)) + b_ref[...]


def _ada(cond, w_ada, b_ada):
    rows = cond.shape[0]
    n = w_ada.shape[1]
    tn = 1536
    return pl.pallas_call(
        _ada_kernel,
        grid=(n // tn,),
        in_specs=[pl.BlockSpec((rows, D_MODEL), lambda j: (0, 0)),
                  pl.BlockSpec((D_MODEL, tn), lambda j: (0, j)),
                  pl.BlockSpec((1, tn), lambda j: (0, j))],
        out_specs=pl.BlockSpec((rows, tn), lambda j: (0, j)),
        out_shape=jax.ShapeDtypeStruct((rows, n), F32),
        compiler_params=pltpu.CompilerParams(vmem_limit_bytes=VMEM_LIMIT),
        name="ada_mod",
    )(cond, w_ada, b_ada.reshape(1, n))


def _group_maps(n_ctx_tiles):
    def ctx_map(i, *_):
        return (jnp.minimum(i, n_ctx_tiles - 1), 0)

    def lat_map(i, *_):
        return (jnp.maximum(i - n_ctx_tiles, 0), 0)

    return ctx_map, lat_map


def _token_specs(tm, n_ctx_tiles, tiles_per_latent_seq, ctx_mod_row):
    ctx_map, lat_map = _group_maps(n_ctx_tiles)

    def pos_map(i, *_):
        return (jnp.maximum(i - n_ctx_tiles, 0) % tiles_per_latent_seq, 0)

    def mod_map(i, *_):
        return (jnp.where(i < n_ctx_tiles, ctx_mod_row,
                          jnp.maximum(i - n_ctx_tiles, 0) // tiles_per_latent_seq), 0, 0)

    return [pl.BlockSpec((tm, D_MODEL), ctx_map),
            pl.BlockSpec((tm, D_MODEL), lat_map),
            pl.BlockSpec((tm, D_MODEL), pos_map),
            pl.BlockSpec((1, N_MOD, D_MODEL), mod_map)]


def _inproj_kernel(xc_ref, xl_ref, pos_ref, mod_ref, wm_ref, wt_ref, wdh_ref, wdl_ref, bd_ref,
                   o_ref, la_ref, *, n_ctx_tiles):
    tail = lambda lo, hi: wt_ref[lo - COL_GATE_A:hi - COL_GATE_A, :]
    i = pl.program_id(0)

    def project(x_of):
        groups = [slice(r0, r0 + PROJ_GROUP) for r0 in range(0, TM_PROJ, PROJ_GROUP)]
        hs = [(_layer_norm(x_of(rows)) * (1.0 + mod_ref[0, 1:2, :]) + mod_ref[0, 0:1, :]).astype(BF16)
              for rows in groups]
        rs = [_split_bf16(_dot_nt(h, tail(COL_R, PROJ_COLS)), 2) for h in hs]
        for rows, (r_hi, r_lo) in zip(groups, rs):
            for d in range(2):
                z = (_dot(r_hi, wdh_ref[d]) + _dot(r_lo, wdh_ref[d]) + _dot(r_hi, wdl_ref[d])
                     + bd_ref[d])
                la_ref[rows, d * GLA_DK:(d + 1) * GLA_DK] = _log_sigmoid(z) * (1.0 / GATE_NORMALIZER)
        for rows, h in zip(groups, hs):
            hg = 0.5 * _dot_nt(h, wm_ref[COL_G:COL_GATE_A, :])
            o_ref[rows, COL_G:COL_GATE_A] = hg * (jnp.tanh(hg) + 1.0)
        for rows, h in zip(groups, hs):
            gates = _dot_nt(h, tail(COL_GATE_A, COL_F))
            o_ref[rows, COL_GATE_A:COL_F] = jnp.tanh(0.5 * gates) + 1.0
        for rows, h in zip(groups, hs):
            o_ref[rows, :COL_G] = _dot_nt(h, wm_ref[:COL_G, :])
        for rows, h in zip(groups, hs):
            o_ref[rows, COL_F:COL_R] = _dot_nt(h, tail(COL_F, COL_R))

    @pl.when(i < n_ctx_tiles)
    def _():
        project(lambda rows: xc_ref[rows, :])

    @pl.when(i >= n_ctx_tiles)
    def _():
        project(lambda rows: xl_ref[rows, :] + pos_ref[rows, :])


def _inproj(x_ctx, x_lat, pos, mod, w_main, w_tail, w_dec, b_dec, lat_len):
    t_ctx, t_lat = x_ctx.shape[0], x_lat.shape[0]
    t_all = t_ctx + t_lat
    n_ctx_tiles = t_ctx // TM_PROJ
    kern = functools.partial(_inproj_kernel, n_ctx_tiles=n_ctx_tiles)
    specs = _token_specs(TM_PROJ, n_ctx_tiles, lat_len // TM_PROJ, mod.shape[0] - 1)
    w_dec_hi = w_dec.astype(BF16)
    w_dec_lo = (w_dec - w_dec_hi.astype(F32)).astype(BF16)

    def const(shape):
        return pl.BlockSpec(shape, lambda i: (0,) * len(shape))

    return pl.pallas_call(
        kern,
        grid=(t_all // TM_PROJ,),
        in_specs=specs + [pl.BlockSpec(w_main.shape, lambda i: (0, 0), pipeline_mode=pl.Buffered(1)),
                          pl.BlockSpec(w_tail.shape, lambda i: (0, 0), pipeline_mode=pl.Buffered(1)),
                          const((2, LANES, GLA_DK)), const((2, LANES, GLA_DK)), const((2, 1, GLA_DK))],
        out_specs=[pl.BlockSpec((TM_PROJ, COL_R), lambda i: (i, 0)),
                   pl.BlockSpec((TM_PROJ, 2 * GLA_DK), lambda i: (i, 0))],
        out_shape=[jax.ShapeDtypeStruct((t_all, COL_R), F32),
                   jax.ShapeDtypeStruct((t_all, 2 * GLA_DK), F32)],
        compiler_params=pltpu.CompilerParams(
            dimension_semantics=("arbitrary",), vmem_limit_bytes=VMEM_LIMIT),
        name="ln_inproj",
    )(x_ctx, x_lat, pos, mod, w_main, w_tail, w_dec_hi, w_dec_lo, b_dec)


def _gla_kernel(*refs, seq_len, has_s0, emit_state):
    it = iter(refs)
    q_ref, k_ref, v_ref, laf_ref, lab_ref, g_ref = (next(it) for _ in range(6))
    s0_ref = next(it) if has_s0 else None
    o_ref = next(it)
    sout_ref = next(it) if emit_state else None
    cum_ref, aq_ref, ko_ref, dec_ref, op_ref, st_ref = (next(it) for _ in range(6))

    C = GLA_CHUNK
    n_chunks = seq_len // C

    def rows(n):
        if isinstance(n, int):
            return pl.ds(n * C, C)
        return pl.ds(pl.multiple_of(n * C, C), C)

    def loop(body):
        if n_chunks <= 2:
            for n in range(n_chunks):
                body(n)
        else:
            def step(m, carry):
                body(2 * m)
                body(2 * m + 1)
                return carry
            lax.fori_loop(0, n_chunks // 2, step, 0)

    rt = lax.broadcasted_iota(jnp.int32, (C, C), 0)
    ct = lax.broadcasted_iota(jnp.int32, (C, C), 1)
    tri = ((rt >= ct).astype(BF16), (ct >= rt).astype(BF16))
    row_id = lax.broadcasted_iota(jnp.int32, (C, DK_HEAD), 0)

    def cumsum_chunk(n):
        for d, la_ref in enumerate((laf_ref, lab_ref)):
            la_hi, la_lo = _split_bf16(la_ref[rows(n), :], 2)
            cum_ref[d, rows(n), :] = _dot(tri[d], la_hi) + _dot(tri[d], la_lo)

    loop(cumsum_chunk)

    query_rows = ({}, {})
    keep = ({}, {})
    for d in range(2):
        blk = C // 2
        while blk >= GLA_LEAF:
            q_parity = 1 if d == 0 else 0
            query_rows[d][blk] = ((row_id // blk) % 2) == q_parity
            qb, kb = rt // blk, ct // blk
            keep[d][blk] = ((qb % 2) == q_parity) & ((qb == kb + 1) if d == 0 else (kb == qb + 1))
            blk //= 2
        order = (rt >= ct) if d == 0 else (ct >= rt)
        keep[d][0] = ((rt // GLA_LEAF) == (ct // GLA_LEAF)) & order

    n_leaves = C // GLA_LEAF

    def per_leaf(rows_of_cum):
        return jnp.concatenate(rows_of_cum, axis=0)

    def expand(per_leaf_rows):
        return jnp.concatenate(
            [jnp.broadcast_to(per_leaf_rows[j:j + 1, :], (GLA_LEAF, DK_HEAD)) for j in range(n_leaves)],
            axis=0)

    def store_scores(n, d, acc, q_in, k_out, end):
        aq_ref[d, rows(n), 0:C] = acc.astype(BF16)
        aq_ref[d, rows(n), C:C + DK_HEAD] = q_in.astype(BF16)
        ko_ref[d, rows(n), :] = k_out.astype(BF16)
        dec_ref[d, rows(n), :] = jnp.broadcast_to(jnp.exp(end), (DK_HEAD, DK_HEAD)).T

    def block_level_scores(cum, q, k, d, blk, at_bnd_scale):
        x = (jnp.where(query_rows[d][blk], q, k) * at_bnd_scale).astype(BF16)
        return jnp.where(keep[d][blk], _dot_nt(x, x), 0.0)

    def boundary_rows(cum, d, blk):
        bnd = blk - 1 if d == 0 else blk
        return per_leaf([cum[bnd + (j * GLA_LEAF) // (2 * blk) * (2 * blk):
                             bnd + (j * GLA_LEAF) // (2 * blk) * (2 * blk) + 1, :]
                         for j in range(n_leaves)])

    def scores_exact(n, d):
        cum = cum_ref[d, rows(n), :]
        q = q_ref[rows(n), :]
        k = k_ref[rows(n), :]
        in_leaf_pos = row_id % GLA_LEAF
        acc = jnp.zeros((C, C), F32)
        for lag in range(GLA_LEAF):
            shift = lag if d == 0 else (C - lag) % C
            k_s = pltpu.roll(k, shift, 0) if lag else k
            cum_s = pltpu.roll(cum, shift, 0) if lag else cum
            paired = (in_leaf_pos >= lag) if d == 0 else (in_leaf_pos < GLA_LEAF - lag)
            term = q * k_s * jnp.exp(jnp.where(paired, cum - cum_s, 0.0))
            s = jnp.sum(jnp.where(paired, term, 0.0), axis=-1, keepdims=True)
            diagonal = (rt - ct == lag) if d == 0 else (ct - rt == lag)
            acc = acc + jnp.where(diagonal, s, 0.0)
        blk = C // 2
        while blk >= GLA_LEAF:
            w = jnp.exp(-jnp.abs(cum - expand(boundary_rows(cum, d, blk))))
            acc = acc + block_level_scores(cum, q, k, d, blk, w)
            blk //= 2
        end = cum[C - 1:C, :] if d == 0 else cum[0:1, :]
        store_scores(n, d, acc, q * jnp.exp(cum), k * jnp.exp(end - cum), end)

    def scores(n, d):
        cum = cum_ref[d, rows(n), :]
        mid = GLA_LEAF // 2 - 1 if d == 0 else GLA_LEAF // 2
        at_mid = per_leaf([cum[mid + j * GLA_LEAF:mid + j * GLA_LEAF + 1, :] for j in range(n_leaves)])
        e = cum - expand(at_mid)
        qe = q_ref[rows(n), :] * jnp.exp(e)
        ke = k_ref[rows(n), :] * jnp.exp(-e)
        acc = jnp.where(keep[d][0], _dot_nt(qe.astype(BF16), ke.astype(BF16)), 0.0)
        blk = C // 2
        while blk >= GLA_LEAF:
            through = expand(jnp.exp(-jnp.abs(at_mid - boundary_rows(cum, d, blk))))
            acc = acc + block_level_scores(cum, qe, ke, d, blk, through)
            blk //= 2
        end = cum[C - 1:C, :] if d == 0 else cum[0:1, :]
        store_scores(n, d, acc, qe * expand(jnp.exp(at_mid)), ke * expand(jnp.exp(end - at_mid)), end)

    steepest = jnp.minimum(jnp.min(laf_ref[...]), jnp.min(lab_ref[...]))
    fast_ok = steepest >= -GLA_FAST_MAX_STEP_DECAY

    @pl.when(fast_ok)
    def _():
        def scores_chunk(n):
            scores(n, 0)
            scores(n, 1)
        loop(scores_chunk)

    @pl.when(jnp.logical_not(fast_ok))
    def _():
        def scores_chunk(n):
            scores_exact(n, 0)
            scores_exact(n, 1)
        loop(scores_chunk)

    for d in range(2):
        if has_s0:
            st_ref[d] = s0_ref[0, d, 0]
        else:
            st_ref[d] = jnp.zeros((DK_HEAD, DV_HEAD), F32)

    def scan(n, d):
        v = v_ref[rows(n), :].astype(BF16)
        st = st_ref[d]
        op_ref[d, rows(n), :] = _dot(aq_ref[d, rows(n), :],
                                     jnp.concatenate([v, st.astype(BF16)], axis=0))
        dec = dec_ref[d, rows(n), :]
        st_ref[d] = (st * jnp.concatenate([dec] * (DV_HEAD // DK_HEAD), axis=1)
                     + _dot_tn(ko_ref[d, rows(n), :], v))

    def scan_chunk(m):
        scan(m, 0)
        scan(n_chunks - 1 - m, 1)

    loop(scan_chunk)
    if emit_state:
        sout_ref[0, 0, 0] = st_ref[0]
        sout_ref[0, 1, 0] = st_ref[1]

    g = g_ref[...]

    def finish_chunk(n):
        o = op_ref[0, rows(n), :] + op_ref[1, rows(n), :]
        ms = jnp.mean(o * o, axis=-1, keepdims=True)
        o_ref[rows(n), :] = o * lax.rsqrt(ms + LN_EPS) * g

    loop(finish_chunk)


def _gla(proj, la, g, s0, *, n_seq, seq_len, row0, emit_state):
    has_s0 = s0 is not None
    blk0 = row0 // seq_len
    kern = functools.partial(_gla_kernel, seq_len=seq_len, has_s0=has_s0, emit_state=emit_state)
    in_specs = [
        pl.BlockSpec((seq_len, DK_HEAD), lambda b, h: (blk0 + b, COL_Q // DK_HEAD + h)),
        pl.BlockSpec((seq_len, DK_HEAD), lambda b, h: (blk0 + b, COL_K // DK_HEAD + h)),
        pl.BlockSpec((seq_len, DV_HEAD), lambda b, h: (blk0 + b, COL_V // DV_HEAD + h)),
        pl.BlockSpec((seq_len, DK_HEAD), lambda b, h: (blk0 + b, h)),
        pl.BlockSpec((seq_len, DK_HEAD), lambda b, h: (blk0 + b, GLA_HEADS + h)),
        pl.BlockSpec((1, DV_HEAD), lambda b, h: (0, 0)),
    ]
    args = [proj, proj, proj, la, la, g]
    if has_s0:
        in_specs.append(pl.BlockSpec((1, 2, 1, DK_HEAD, DV_HEAD), lambda b, h: (b, 0, h, 0, 0)))
        args.append(s0)
    out_specs = [pl.BlockSpec((seq_len, DV_HEAD), lambda b, h: (b, h))]
    out_shape = [jax.ShapeDtypeStruct((n_seq * seq_len, GLA_DV), F32)]
    if emit_state:
        out_specs.append(pl.BlockSpec((1, 2, 1, DK_HEAD, DV_HEAD), lambda b, h: (b, 0, h, 0, 0)))
        out_shape.append(jax.ShapeDtypeStruct((n_seq, 2, GLA_HEADS, DK_HEAD, DV_HEAD), F32))

    res = pl.pallas_call(
        kern,
        grid=(n_seq, GLA_HEADS),
        in_specs=in_specs,
        out_specs=out_specs,
        out_shape=out_shape,
        scratch_shapes=[pltpu.VMEM((2, seq_len, DK_HEAD), F32),
                        pltpu.VMEM((2, seq_len, GLA_CHUNK + DK_HEAD), BF16),
                        pltpu.VMEM((2, seq_len, DK_HEAD), BF16),
                        pltpu.VMEM((2, seq_len, DK_HEAD), F32),
                        pltpu.VMEM((2, seq_len, DV_HEAD), F32),
                        pltpu.VMEM((2, DK_HEAD, DV_HEAD), F32)],
        compiler_params=pltpu.CompilerParams(
            dimension_semantics=("arbitrary", "arbitrary"), vmem_limit_bytes=VMEM_LIMIT),
        name="gla_seq%d" % seq_len,
    )(*args)
    return res


def _fnet_kernel(f_ref, cl_ref, sl_ref, cg_ref, sg_ref, o_ref, uc_ref, us_ref, *, seq_len):
    cg = cg_ref[...]
    sg = sg_ref[...]
    for grp in range(FNET_GROUPS):
        lo = grp * FNET_GROUP_DIM
        u = f_ref[:, lo:lo + FNET_GROUP_DIM].astype(BF16)
        uc_ref[:, lo:lo + FNET_GROUP_DIM] = _dot(u, cg).astype(BF16)
        us_ref[:, lo:lo + FNET_GROUP_DIM] = _dot(u, sg).astype(BF16)
    mixed = _dot(cl_ref[...], uc_ref[...]) - _dot(sl_ref[...], us_ref[...])
    o_ref[...] = mixed * (1.0 / math.sqrt(seq_len * FNET_GROUP_DIM))


def _dft_mats(n):
    j = np.arange(n, dtype=np.int64)
    ang = (2.0 * np.pi / n) * ((j[:, None] * j[None, :]) % n).astype(np.float64)
    return (jnp.asarray(np.cos(ang), dtype=F32).astype(BF16),
            jnp.asarray(np.sin(ang), dtype=F32).astype(BF16))


def _fnet(proj, *, n_seq, seq_len, row0):
    blk0 = row0 // seq_len
    cl, sl = _dft_mats(seq_len)
    cg, sg = _dft_mats(FNET_GROUP_DIM)
    kern = functools.partial(_fnet_kernel, seq_len=seq_len)
    return pl.pallas_call(
        kern,
        grid=(n_seq,),
        in_specs=[pl.BlockSpec((seq_len, FNET_DIM), lambda b: (blk0 + b, COL_F // FNET_DIM)),
                  pl.BlockSpec((seq_len, seq_len), lambda b: (0, 0)),
                  pl.BlockSpec((seq_len, seq_len), lambda b: (0, 0)),
                  pl.BlockSpec((FNET_GROUP_DIM, FNET_GROUP_DIM), lambda b: (0, 0)),
                  pl.BlockSpec((FNET_GROUP_DIM, FNET_GROUP_DIM), lambda b: (0, 0))],
        out_specs=pl.BlockSpec((seq_len, FNET_DIM), lambda b: (b, 0)),
        out_shape=jax.ShapeDtypeStruct((n_seq * seq_len, FNET_DIM), F32),
        scratch_shapes=[pltpu.VMEM((seq_len, FNET_DIM), BF16),
                        pltpu.VMEM((seq_len, FNET_DIM), BF16)],
        compiler_params=pltpu.CompilerParams(
            dimension_semantics=("arbitrary",), vmem_limit_bytes=VMEM_LIMIT),
        name="fnet_seq%d" % seq_len,
    )(proj, cl, sl, cg, sg)


def _merge_kernel(xc_ref, xl_ref, pos_ref, mod_ref, oc_ref, ol_ref, mc_ref, ml_ref,
                  g_ref, ga_ref, gb_ref, wbg_ref, wbf_ref, wo_ref, l1g_ref, l1b_ref, wrh_ref, wrl_ref,
                  br_ref, x1_ref, h2_ref, ridx_ref, rw_ref, *, n_ctx_tiles, alpha):
    i = pl.program_id(0)

    tm = x1_ref.shape[0]
    groups = [slice(r0, r0 + TM_MERGE_GROUP) for r0 in range(0, tm, TM_MERGE_GROUP)]
    shape = (TM_MERGE_GROUP, LANES)
    lane_i = lax.broadcasted_iota(jnp.int32, shape, 1)
    lane = lane_i.astype(F32)

    def compute(x_of, o_ref, mx_ref):
        branch = []
        for rows in groups:
            a = (o_ref[rows, :] * g_ref[rows, :]).astype(BF16)
            branch.append((_dot(a, wbg_ref[...]), _dot(mx_ref[rows, :].astype(BF16), wbf_ref[...])))
        mix = []
        for rows, (gla_out, fnet_out) in zip(groups, branch):
            merged = ga_ref[rows, :] * gla_out + gb_ref[rows, :] * fnet_out
            mix.append(_dot(merged.astype(BF16), wo_ref[...]))
        logits = []
        for rows, mix_g in zip(groups, mix):
            y = alpha * x_of(rows) + mod_ref[0, 2:3, :] * mix_g
            x1 = _layer_norm(y) * l1g_ref[...] + l1b_ref[...]
            x1_ref[rows, :] = x1
            h2 = _layer_norm(x1) * (1.0 + mod_ref[0, 4:5, :]) + mod_ref[0, 3:4, :]
            _store_row_tiles(h2_ref, h2, rows.start)
            h_hi, h_lo = _split_bf16(h2, 2)
            logits.append(_dot(h_hi, wrh_ref[...]) + _dot(h_lo, wrh_ref[...])
                          + _dot(h_hi, wrl_ref[...]) + br_ref[...])
        idx_out = [jnp.zeros(shape, F32) for _ in groups]
        val_out = [jnp.zeros(shape, F32) for _ in groups]
        top0 = [None] * len(groups)
        denom = [None] * len(groups)
        for kk in range(TOP_K):
            for gi in range(len(groups)):
                m = jnp.max(logits[gi], axis=-1, keepdims=True)
                sel = jnp.min(jnp.where(logits[gi] == m, lane, float(LANES)), axis=-1, keepdims=True)
                if kk == 0:
                    top0[gi] = m
                    p = jnp.ones_like(m)
                    denom[gi] = p
                else:
                    p = jnp.exp(m - top0[gi])
                    denom[gi] = denom[gi] + p
                idx_out[gi] = jnp.where(lane_i == kk, sel, idx_out[gi])
                val_out[gi] = jnp.where(lane_i == kk, p, val_out[gi])
                logits[gi] = jnp.where(lane == sel, -jnp.inf, logits[gi])
        for gi, rows in enumerate(groups):
            ridx_ref[rows, :] = idx_out[gi].astype(jnp.int32)
            rw_ref[rows, :] = val_out[gi] / denom[gi]

    @pl.when(i < n_ctx_tiles)
    def _():
        compute(lambda rows: xc_ref[rows, :], oc_ref, mc_ref)

    @pl.when(i >= n_ctx_tiles)
    def _():
        compute(lambda rows: xl_ref[rows, :] + pos_ref[rows, :], ol_ref, ml_ref)


def _merge(x_ctx, x_lat, pos, mod, o_ctx, o_lat, mixed_ctx, mixed_lat, proj,
           wbg, wbf, wo, l1g, l1b, wr_hi, wr_lo, br, lat_len, alpha):
    t_ctx, t_lat = x_ctx.shape[0], x_lat.shape[0]
    t_all = t_ctx + t_lat
    tm = TM_MERGE
    n_ctx_tiles = t_ctx // tm
    kern = functools.partial(_merge_kernel, n_ctx_tiles=n_ctx_tiles, alpha=alpha)
    specs = _token_specs(tm, n_ctx_tiles, lat_len // tm, mod.shape[0] - 1)
    ctx_map, lat_map = _group_maps(n_ctx_tiles)

    def const(shape):
        return pl.BlockSpec(shape, lambda i: (0,) * len(shape))

    in_specs = specs + [
        pl.BlockSpec((tm, GLA_DV), ctx_map),
        pl.BlockSpec((tm, GLA_DV), lat_map),
        pl.BlockSpec((tm, FNET_DIM), ctx_map),
        pl.BlockSpec((tm, FNET_DIM), lat_map),
        pl.BlockSpec((tm, GLA_DV), lambda i: (i, COL_G // GLA_DV)),
        pl.BlockSpec((tm, D_MODEL), lambda i: (i, COL_GATE_A // D_MODEL)),
        pl.BlockSpec((tm, D_MODEL), lambda i: (i, COL_GATE_B // D_MODEL)),
        const((GLA_DV, D_MODEL)), const((FNET_DIM, D_MODEL)), const((D_MODEL, D_MODEL)),
        const((1, D_MODEL)), const((1, D_MODEL)),
        const((D_MODEL, LANES)), const((D_MODEL, LANES)), const((1, LANES)),
    ]
    out_specs = [pl.BlockSpec((tm, D_MODEL), lambda i: (i, 0)),
                 pl.BlockSpec((tm * ROW_TILES, LANES), lambda i: (i, 0)),
                 pl.BlockSpec((tm, LANES), lambda i: (i, 0)),
                 pl.BlockSpec((tm, LANES), lambda i: (i, 0))]
    out_shape = [jax.ShapeDtypeStruct((t_all, D_MODEL), F32),
                 jax.ShapeDtypeStruct((t_all * ROW_TILES, LANES), ROW_DTYPE),
                 jax.ShapeDtypeStruct((t_all, LANES), jnp.int32),
                 jax.ShapeDtypeStruct((t_all, LANES), F32)]
    return pl.pallas_call(
        kern,
        grid=(t_all // tm,),
        in_specs=in_specs,
        out_specs=out_specs,
        out_shape=out_shape,
        compiler_params=pltpu.CompilerParams(
            dimension_semantics=("arbitrary",), vmem_limit_bytes=VMEM_LIMIT),
        name="merge_ln1_router",
    )(x_ctx, x_lat, pos, mod, o_ctx, o_lat, mixed_ctx, mixed_lat, proj, proj, proj,
      wbg, wbf, wo, l1g, l1b, wr_hi, wr_lo, br)


def _sc_mesh():
    return plsc.VectorSubcoreMesh(core_axis_name="c", subcore_axis_name="s")


def _sc_worker_id():
    return lax.axis_index("s") * SC_CORES + lax.axis_index("c")


def _sc_scatter_rows(src, idx, n_out):
    n_src = src.shape[0]
    w = SC_WINDOW
    n_chunks = n_src // (SC_WORKERS * w)
    copies = idx.shape[0]
    assert n_chunks % 2 == 0 and idx.shape == (copies, SC_WORKERS, n_chunks, w)

    @functools.partial(
        pl.kernel, mesh=_sc_mesh(),
        out_type=jax.ShapeDtypeStruct((n_out, ROW_TILES, LANES), ROW_DTYPE),
        scratch_types=[pltpu.VMEM((copies * n_chunks, w), jnp.int32),
                       pltpu.VMEM((2, w, ROW_TILES, LANES), ROW_DTYPE),
                       pltpu.SemaphoreType.DMA((2,)),
                       pltpu.SemaphoreType.DMA((2,))],
        name="moe_dispatch_scatter")
    def k(src_hbm, idx_hbm, out_hbm, idx_v, rows_v, rsem, wsem):
        wid = _sc_worker_id()
        base = wid * (n_chunks * w)
        for kk in range(copies):
            pltpu.sync_copy(idx_hbm.at[kk, wid], idx_v.at[pl.ds(kk * n_chunks, n_chunks)])

        def read(j, slot):
            return pltpu.make_async_copy(src_hbm.at[pl.ds(base + j * w, w)], rows_v.at[slot],
                                         rsem.at[slot])

        def scatter(j, kk, slot):
            return pltpu.make_async_copy(rows_v.at[slot], out_hbm.at[idx_v.at[kk * n_chunks + j]],
                                         wsem.at[slot])

        read(0, 0).start()

        @pl.loop(0, n_chunks, step=2)
        def _(jj):
            read(jj, 0).wait()

            @pl.when(jj > 0)
            def _():
                for kk in range(copies):
                    scatter(jj - 1, kk, 1).wait()

            read(jj + 1, 1).start()
            for kk in range(copies):
                scatter(jj, kk, 0).start()
            read(jj + 1, 1).wait()
            for kk in range(copies):
                scatter(jj, kk, 0).wait()

            @pl.when(jj + 2 < n_chunks)
            def _():
                read(jj + 2, 0).start()

            for kk in range(copies):
                scatter(jj + 1, kk, 1).start()

        for kk in range(copies):
            scatter(n_chunks - 1, kk, 1).wait()

    return k(src, idx)


def _sc_gather_rows(table, idx):
    _, n_chunks, w = idx.shape
    assert n_chunks % 2 == 0 and idx.shape[0] == SC_WORKERS and w == SC_WINDOW
    n_out = SC_WORKERS * n_chunks * w

    @functools.partial(
        pl.kernel, mesh=_sc_mesh(),
        out_type=jax.ShapeDtypeStruct((n_out, ROW_TILES, LANES), ROW_DTYPE),
        scratch_types=[pltpu.VMEM((n_chunks, w), jnp.int32),
                       pltpu.VMEM((2, w, ROW_TILES, LANES), ROW_DTYPE),
                       pltpu.SemaphoreType.DMA((2,)),
                       pltpu.SemaphoreType.DMA((2,))],
        name="moe_combine_gather")
    def k(table_hbm, idx_hbm, out_hbm, idx_v, rows_v, gsem, wsem):
        wid = _sc_worker_id()
        base = wid * (n_chunks * w)
        pltpu.sync_copy(idx_hbm.at[wid], idx_v)

        def gather(j, slot):
            return pltpu.make_async_copy(table_hbm.at[idx_v.at[j]], rows_v.at[slot], gsem.at[slot])

        def write(j, slot):
            return pltpu.make_async_copy(rows_v.at[slot], out_hbm.at[pl.ds(base + j * w, w)],
                                         wsem.at[slot])

        gather(0, 0).start()

        @pl.loop(0, n_chunks, step=2)
        def _(jj):
            gather(jj, 0).wait()

            @pl.when(jj > 0)
            def _():
                write(jj - 1, 1).wait()

            gather(jj + 1, 1).start()
            write(jj, 0).start()
            gather(jj + 1, 1).wait()
            write(jj, 0).wait()

            @pl.when(jj + 2 < n_chunks)
            def _():
                gather(jj + 2, 0).start()

            write(jj + 1, 1).start()

        write(n_chunks - 1, 1).wait()

    return k(table, idx)


def _moe_kernel(be_ref, nu_ref, nv_ref, slot_ref, nxt_ref, x_ref, wgu_hbm, bgu_ref, wd_hbm, bd_ref,
                o_ref, wgu_st, wd_st, wgu_bf, wd_bf, xb_ref, sem):
    b = pl.program_id(0)
    e = be_ref[b]
    prev = be_ref[jnp.maximum(b - 1, 0)]
    active = b < nu_ref[0]
    changed = (b == 0) | (e != prev)

    def weight_copies(expert, s):
        return (pltpu.make_async_copy(wgu_hbm.at[expert], wgu_st.at[s], sem.at[0, s]),
                pltpu.make_async_copy(wd_hbm.at[expert], wd_st.at[s], sem.at[1, s]))

    @pl.when(active & changed)
    def _():
        s = slot_ref[b]

        @pl.when(b == 0)
        def _():
            for cp in weight_copies(e, s):
                cp.start()

        for cp in weight_copies(e, s):
            cp.wait()
        wgu_bf[...] = wgu_st[s].astype(BF16)
        wd_bf[...] = wd_st[s].astype(BF16)
        nxt = nxt_ref[b]

        @pl.when(nxt >= 0)
        def _():
            for cp in weight_copies(nxt, 1 - s):
                cp.start()

    n_valid = nv_ref[b]

    def expert_mlp(n_rows):
        valid = lax.broadcasted_iota(jnp.int32, (n_rows, LANES), 0) < n_valid
        for j in range(ROW_TILES):
            for half, xj in enumerate(_load_row_tile(x_ref, j, n_rows)):
                c0 = half * HALF_MODEL + j * LANES
                xb_ref[0:n_rows, c0:c0 + LANES] = jnp.where(valid, xj, 0.0).astype(BF16)
        gu = _dot(xb_ref[0:n_rows, :], wgu_bf[...]) + bgu_ref[0]
        gate = jnp.minimum(gu[:, :D_EXPERT], SWIGLU_LIMIT)
        up = jnp.clip(gu[:, D_EXPERT:], -SWIGLU_LIMIT, SWIGLU_LIMIT)
        glu = gate * _sigmoid(gate * SWIGLU_ALPHA)
        act = ((up + 1.0) * glu).astype(BF16)
        _store_row_tiles(o_ref, _dot(act, wd_bf[...]) + bd_ref[0])

    for n_rows in range(MOE_STEP, MOE_ROWS + 1, MOE_STEP):
        @pl.when(active & (n_valid > n_rows - MOE_STEP) & (n_valid <= n_rows))
        def _(n_rows=n_rows):
            expert_mlp(n_rows)


def _moe(tables, xs, w_gate_up, b_gate_up, w_down, b_down):
    p_rows = xs.shape[0] // ROW_TILES
    n_blocks = p_rows // MOE_ROWS

    def blk(b, be, nu, *_):
        return jnp.minimum(b, nu[0] - 1)

    def expert(b, be, nu, *_):
        return (be[blk(b, be, nu)], 0, 0)

    def rows(b, be, nu, *_):
        return (blk(b, be, nu), 0)

    grid_spec = pltpu.PrefetchScalarGridSpec(
        num_scalar_prefetch=len(tables),
        grid=(n_blocks,),
        in_specs=[
            pl.BlockSpec((MOE_ROWS * ROW_TILES, LANES), rows),
            pl.BlockSpec(memory_space=pl.ANY),
            pl.BlockSpec((1, 1, 2 * D_EXPERT), expert),
            pl.BlockSpec(memory_space=pl.ANY),
            pl.BlockSpec((1, 1, D_MODEL), expert),
        ],
        out_specs=pl.BlockSpec((MOE_ROWS * ROW_TILES, LANES), rows),
        scratch_shapes=[pltpu.VMEM((2, D_MODEL, 2 * D_EXPERT), F32),
                        pltpu.VMEM((2, D_EXPERT, D_MODEL), F32),
                        pltpu.VMEM((D_MODEL, 2 * D_EXPERT), BF16),
                        pltpu.VMEM((D_EXPERT, D_MODEL), BF16),
                        pltpu.VMEM((MOE_ROWS, D_MODEL), BF16),
                        pltpu.SemaphoreType.DMA((2, 2))],
    )
    return pl.pallas_call(
        _moe_kernel,
        grid_spec=grid_spec,
        out_shape=jax.ShapeDtypeStruct((p_rows * ROW_TILES, LANES), ROW_DTYPE),
        compiler_params=pltpu.CompilerParams(
            dimension_semantics=("arbitrary",), vmem_limit_bytes=VMEM_LIMIT),
        name="moe_grouped_mlp",
    )(*tables, xs, w_gate_up, b_gate_up.reshape(N_EXPERTS, 1, 2 * D_EXPERT), w_down,
      b_down.reshape(N_EXPERTS, 1, D_MODEL))


def _combine_kernel(x1_ref, y0_ref, y1_ref, y2_ref, y3_ref, rw_ref, mod_ref, g_ref, b_ref, o_ref,
                    *, alpha):
    rw = rw_ref[...]
    y_refs = (y0_ref, y1_ref, y2_ref, y3_ref)
    pieces = [None] * (2 * ROW_TILES)
    for j in range(ROW_TILES):
        for kk in range(TOP_K):
            for half, yj in enumerate(_load_row_tile(y_refs[kk], j, rw.shape[0])):
                term = rw[:, kk:kk + 1] * yj
                slot = half * ROW_TILES + j
                pieces[slot] = term if kk == 0 else pieces[slot] + term
    ff = jnp.concatenate(pieces, axis=-1)
    y = alpha * x1_ref[...] + mod_ref[0, 5:6, :] * ff
    o_ref[...] = _layer_norm(y) * g_ref[...] + b_ref[...]


def _combine(x1, yg, rw, mod, l2g, l2b, *, row0, n_rows, mod_map, alpha):
    tm = TM_MIX
    t0 = row0 // tm
    tiles = n_rows // tm
    kern = functools.partial(_combine_kernel, alpha=alpha)

    def y_spec(kk):
        return pl.BlockSpec((tm * ROW_TILES, LANES), lambda i: (kk * tiles + i, 0))

    return pl.pallas_call(
        kern,
        grid=(n_rows // tm,),
        in_specs=[pl.BlockSpec((tm, D_MODEL), lambda i: (t0 + i, 0))]
        + [y_spec(kk) for kk in range(TOP_K)]
        + [pl.BlockSpec((tm, LANES), lambda i: (t0 + i, 0)),
           pl.BlockSpec((1, N_MOD, D_MODEL), mod_map),
           pl.BlockSpec((1, D_MODEL), lambda i: (0, 0)),
           pl.BlockSpec((1, D_MODEL), lambda i: (0, 0))],
        out_specs=pl.BlockSpec((tm, D_MODEL), lambda i: (i, 0)),
        out_shape=jax.ShapeDtypeStruct((n_rows, D_MODEL), F32),
        compiler_params=pltpu.CompilerParams(
            dimension_semantics=("arbitrary",), vmem_limit_bytes=VMEM_LIMIT),
        name="combine_ln2",
    )(x1, yg, yg, yg, yg, rw, mod, l2g, l2b)


def _route_kernel(ridx_ref, dest_ref, cnt_ref, run_ref, bst_ref):
    phase = pl.program_id(0)
    i = pl.program_id(1)
    tm = ridx_ref.shape[0]
    ridx = ridx_ref[...]
    lane = lax.broadcasted_iota(jnp.int32, (tm, LANES), 1)
    hits = [ridx[:, kk:kk + 1] == lane for kk in range(TOP_K)]
    chosen = jnp.where(hits[0], 1.0, 0.0)
    for kk in range(1, TOP_K):
        chosen = chosen + jnp.where(hits[kk], 1.0, 0.0)
    colsum = jnp.sum(chosen, axis=0, keepdims=True)

    @pl.when((phase == 0) & (i == 0))
    def _():
        run_ref[...] = jnp.zeros_like(run_ref)

    @pl.when(phase == 0)
    def _():
        run_ref[...] = run_ref[...] + colsum

    @pl.when((phase == 1) & (i == 0))
    def _():
        counts = run_ref[...]
        cnt_ref[...] = counts
        blocks = jnp.floor((counts + (MOE_ROWS - 1.0)) * (1.0 / MOE_ROWS))
        r = lax.broadcasted_iota(jnp.int32, (LANES, LANES), 0)
        c = lax.broadcasted_iota(jnp.int32, (LANES, LANES), 1)
        before = jnp.dot(blocks, (r < c).astype(F32), precision=HIGHEST, preferred_element_type=F32)
        bst_ref[...] = before * float(MOE_ROWS)
        run_ref[...] = jnp.zeros_like(run_ref)

    @pl.when(phase == 1)
    def _():
        rt = lax.broadcasted_iota(jnp.int32, (tm, tm), 0)
        ct = lax.broadcasted_iota(jnp.int32, (tm, tm), 1)
        earlier = _dot((ct < rt).astype(BF16), chosen.astype(BF16))
        row_of = bst_ref[0:1, :] + run_ref[0:1, :] + earlier
        out = jnp.take_along_axis(row_of, ridx, axis=1)
        dest_ref[...] = out.T[0:SUBLANES, :].astype(jnp.int32)
        run_ref[...] = run_ref[...] + colsum


def _route(ridx):
    t_all = ridx.shape[0]
    tm = TM_ROUTE
    return pl.pallas_call(
        _route_kernel,
        grid=(2, t_all // tm),
        in_specs=[pl.BlockSpec((tm, LANES), lambda p, i: (i, 0))],
        out_specs=[pl.BlockSpec((SUBLANES, tm), lambda p, i: (0, i * p)),
                   pl.BlockSpec((SUBLANES, LANES), lambda p, i: (0, 0))],
        out_shape=[jax.ShapeDtypeStruct((SUBLANES, t_all), jnp.int32),
                   jax.ShapeDtypeStruct((SUBLANES, LANES), F32)],
        scratch_shapes=[pltpu.VMEM((SUBLANES, LANES), F32),
                        pltpu.VMEM((SUBLANES, LANES), F32)],
        compiler_params=pltpu.CompilerParams(
            dimension_semantics=("arbitrary", "arbitrary"), vmem_limit_bytes=VMEM_LIMIT),
        name="moe_route",
    )(ridx)


def _routing_tables(counts, n_blocks):
    experts = jnp.arange(N_EXPERTS, dtype=jnp.int32)
    blocks_per = (counts + MOE_ROWS - 1) // MOE_ROWS
    bends = jnp.cumsum(blocks_per)
    bstarts = bends - blocks_per
    blocks = jnp.arange(n_blocks, dtype=jnp.int32)
    block_expert = jnp.minimum(
        jnp.sum((bends[None, :] <= blocks[:, None]).astype(jnp.int32), axis=1), N_EXPERTS - 1)
    n_used = bends[-1:].astype(jnp.int32)
    owner = block_expert[:, None] == experts[None, :]

    def per_block(table):
        return jnp.sum(jnp.where(owner, table[None, :], 0), axis=1)

    n_valid = jnp.clip(per_block(counts) - (blocks - per_block(bstarts)) * MOE_ROWS,
                       0, MOE_ROWS).astype(jnp.int32)
    present = blocks_per > 0
    ordinal = jnp.cumsum(present.astype(jnp.int32)) - 1
    later = lax.cummin(jnp.where(present, experts, N_EXPERTS), reverse=True)
    succ = jnp.concatenate([later[1:], jnp.full((1,), N_EXPERTS, jnp.int32)])
    succ = jnp.where(succ >= N_EXPERTS, -1, succ)
    stage_slot = (per_block(ordinal) % 2).astype(jnp.int32)
    next_expert = per_block(succ).astype(jnp.int32)
    return (block_expert.astype(jnp.int32), n_used, n_valid, stage_slot, next_expert)


def _pos_embed_2d(n_tokens):
    rows = n_tokens // GRID_W
    r = np.repeat(np.arange(rows), GRID_W).astype(np.float32)
    col = np.tile(np.arange(GRID_W), rows).astype(np.float32)
    quarter = D_MODEL // 4
    omega = (np.float32(1.0)
             / np.power(np.float32(10000.0), np.arange(quarter, dtype=np.float32) / np.float32(quarter)))
    er = (r[:, None] * omega).astype(np.float64)
    ec = (col[:, None] * omega).astype(np.float64)
    table = np.concatenate([np.sin(er), np.cos(er), np.sin(ec), np.cos(ec)], axis=-1)
    return jnp.asarray(table, dtype=F32)


def _split_w_in(w):
    wt = w.T
    o_r = 2 * GLA_DK + 2 * GLA_DV
    o_f = o_r + DECAY_RANK
    o_gate = o_f + FNET_DIM
    row_scale = jnp.where(jnp.arange(o_r) < GLA_DK, DK_HEAD ** -0.5, 1.0).astype(w.dtype)
    w_main = (wt[:o_r] * row_scale[:, None]).astype(BF16)
    pad = jnp.zeros((LANES - DECAY_RANK, w.shape[0]), BF16)
    w_tail = jnp.concatenate([wt[o_gate:].astype(BF16), wt[o_f:o_gate].astype(BF16),
                              wt[o_r:o_f].astype(BF16), pad], axis=0)
    return w_main, w_tail


def kernel(x_prompt, x_sample, state_gla, c, c_ctx, w_ada, b_ada, w_in, w_dec_fwd, b_dec_fwd,
           w_dec_bwd, b_dec_bwd, gla_norm_g, w_br_gla, w_br_fnet, w_out, ln1_g, ln1_b, w_router,
           b_router, w_gate_up, b_gate_up, w_down, b_down, ln2_g, ln2_b):
    n_req, ctx_len, _ = x_prompt.shape
    n_lat, lat_len, _ = x_sample.shape
    depth = w_in.shape[0]
    alpha = (2.0 * depth) ** 0.25
    t_ctx = n_req * ctx_len
    t_lat = n_lat * lat_len
    t_all = t_ctx + t_lat

    x_ctx = x_prompt.reshape(t_ctx, D_MODEL)
    x_lat = x_sample.reshape(t_lat, D_MODEL)
    pos = _pos_embed_2d(lat_len)
    zero_pos = jnp.zeros_like(pos)

    cond_rows = -(-(n_lat + 1) // SUBLANES) * SUBLANES
    cond = jnp.zeros((cond_rows, D_MODEL), F32).at[:n_lat].set(c).at[cond_rows - 1].set(c_ctx)

    n_moe_blocks = (t_all * TOP_K) // MOE_ROWS + N_EXPERTS
    tok_chunks = t_all // (SC_WORKERS * SC_WINDOW)
    states = []
    for l in range(depth):
        mod = _ada(cond, w_ada[l], b_ada[l]).reshape(cond_rows, N_MOD, D_MODEL)
        layer_pos = pos if l == 0 else zero_pos
        w_dec = jnp.zeros((2, LANES, GLA_DK), F32)
        w_dec = w_dec.at[0, :DECAY_RANK].set(w_dec_fwd[l]).at[1, :DECAY_RANK].set(w_dec_bwd[l])
        b_dec = jnp.stack([b_dec_fwd[l], b_dec_bwd[l]]).reshape(2, 1, GLA_DK)
        proj, la = _inproj(x_ctx, x_lat, layer_pos, mod, *_split_w_in(w_in[l]), w_dec, b_dec, lat_len)

        norm_g = gla_norm_g[l].reshape(1, DV_HEAD)
        o_ctx, s_new = _gla(proj, la, norm_g, None, n_seq=n_req, seq_len=ctx_len, row0=0,
                            emit_state=True)
        (o_lat,) = _gla(proj, la, norm_g, state_gla[:, l], n_seq=n_lat, seq_len=lat_len, row0=t_ctx,
                        emit_state=False)
        states.append(s_new)

        mixed_ctx = _fnet(proj, n_seq=n_req, seq_len=ctx_len, row0=0)
        mixed_lat = _fnet(proj, n_seq=n_lat, seq_len=lat_len, row0=t_ctx)

        wr = jnp.zeros((D_MODEL, LANES), F32).at[:, :N_EXPERTS].set(w_router[l])
        br = jnp.full((1, LANES), -1e30, F32).at[0, :N_EXPERTS].set(b_router[l])
        wr_hi = wr.astype(BF16)
        wr_lo = (wr - wr_hi.astype(F32)).astype(BF16)
        x1, h2, ridx, rw = _merge(
            x_ctx, x_lat, layer_pos, mod, o_ctx, o_lat, mixed_ctx, mixed_lat, proj,
            w_br_gla[l].astype(BF16), w_br_fnet[l].astype(BF16), (0.5 * w_out[l]).astype(BF16),
            ln1_g[l].reshape(1, D_MODEL), ln1_b[l].reshape(1, D_MODEL), wr_hi, wr_lo, br, lat_len,
            alpha)

        dest, counts = _route(ridx)
        moe_tables = _routing_tables(counts[0, :N_EXPERTS].astype(jnp.int32), n_moe_blocks)
        dest = dest[:TOP_K]
        scatter_idx = dest.reshape(TOP_K, SC_WORKERS, tok_chunks, SC_WINDOW)
        p_rows = n_moe_blocks * MOE_ROWS
        xs = _sc_scatter_rows(h2.reshape(t_all, ROW_TILES, LANES), scatter_idx, p_rows)
        yb = _moe(moe_tables, xs.reshape(p_rows * ROW_TILES, LANES),
                  w_gate_up[l], b_gate_up[l], w_down[l], b_down[l])
        yb = yb.reshape(p_rows, ROW_TILES, LANES)

        def gathered(row0, n_rows):
            idx = dest[:, row0:row0 + n_rows].reshape(SC_WORKERS, -1, SC_WINDOW)
            return _sc_gather_rows(yb, idx).reshape(TOP_K * n_rows * ROW_TILES, LANES)

        l2g = ln2_g[l].reshape(1, D_MODEL)
        l2b = ln2_b[l].reshape(1, D_MODEL)
        tiles_per_seq = lat_len // TM_MIX
        yg_ctx = gathered(0, t_ctx)
        yg_lat = gathered(t_ctx, t_lat)
        x_ctx = _combine(x1, yg_ctx, rw, mod, l2g, l2b, row0=0, n_rows=t_ctx,
                         mod_map=lambda i: (cond_rows - 1, 0, 0), alpha=alpha)
        x_lat = _combine(x1, yg_lat, rw, mod, l2g, l2b, row0=t_ctx, n_rows=t_lat,
                         mod_map=lambda i: (i // tiles_per_seq, 0, 0), alpha=alpha)

    y_prompt = x_ctx.reshape(x_prompt.shape)
    y_sample = x_lat.reshape(x_sample.shape)
    new_state = jnp.stack(states, axis=1).astype(x_prompt.dtype)
    return (y_prompt, y_sample, new_state)
```

```python
import functools
import math

import numpy as np
import jax
import jax.numpy as jnp
from jax import lax
from jax.experimental import pallas as pl
from jax.experimental.pallas import tpu as pltpu
from jax.experimental.pallas import tpu_sc as plsc

F32 = jnp.float32
BF16 = jnp.bfloat16

D_MODEL = 1024
GRID_W = 64
GLA_HEADS = 4
DK_HEAD = 128
DV_HEAD = 256
GLA_DK = GLA_HEADS * DK_HEAD
GLA_DV = GLA_HEADS * DV_HEAD
DECAY_RANK = 16
GATE_NORMALIZER = 16.0
FNET_GROUPS = 4
FNET_GROUP_DIM = 128
FNET_DIM = FNET_GROUPS * FNET_GROUP_DIM
N_EXPERTS = 32
TOP_K = 4
D_EXPERT = 1024
SWIGLU_LIMIT = 7.0
SWIGLU_ALPHA = 1.702
LN_EPS = 1e-6
N_MOD = 6

LANES = 128
SUBLANES = 8
HALF_MODEL = D_MODEL // 2
ROW_TILES = HALF_MODEL // LANES
ROW_DTYPE = jnp.uint32
COL_Q = 0
COL_K = GLA_DK
COL_V = 2 * GLA_DK
COL_G = COL_V + GLA_DV
COL_GATE_A = COL_G + GLA_DV
COL_GATE_B = COL_GATE_A + D_MODEL
COL_F = COL_GATE_B + D_MODEL
COL_R = COL_F + FNET_DIM
PROJ_COLS = COL_R + LANES

GLA_CHUNK = 128
GLA_LEAF = 16
assert GLA_CHUNK == DK_HEAD
GLA_FAST_MAX_STEP_DECAY = 8.0
TM_PROJ = 512
PROJ_GROUP = 256
TM_MIX = 256
TM_MERGE = 512
TM_MERGE_GROUP = 256
TM_ROUTE = 1024
MOE_ROWS = 512
MOE_STEP = 128
VMEM_LIMIT = 56 * 1024 * 1024

SC_CORES = 2
SC_SUBCORES = 16
SC_WORKERS = SC_CORES * SC_SUBCORES
SC_WINDOW = 64

HIGHEST = lax.Precision.HIGHEST


def _layer_norm(x):
    mu = jnp.mean(x, axis=-1, keepdims=True)
    xc = x - mu
    var = jnp.mean(xc * xc, axis=-1, keepdims=True)
    return xc * lax.rsqrt(var + LN_EPS)


def _sigmoid(x):
    return 0.5 * jnp.tanh(0.5 * x) + 0.5


def _log_sigmoid(z):
    return jnp.minimum(z, 0.0) - jnp.log(1.0 + jnp.exp(-jnp.abs(z)))


def _dot(a, b):
    return jnp.dot(a, b, preferred_element_type=F32)


def _split_bf16(x, terms):
    parts = []
    for _ in range(terms):
        p = x.astype(BF16)
        parts.append(p)
        x = x - p.astype(F32)
    return parts


def _dot_nt(a, b):
    return lax.dot_general(a, b, (((1,), (1,)), ((), ())), preferred_element_type=F32)


def _dot_tn(a, b):
    return lax.dot_general(a, b, (((0,), (0,)), ((), ())), preferred_element_type=F32)


def _row_tile_slice(j, n_rows, first_row=0):
    return pl.ds(first_row * ROW_TILES + j, n_rows, stride=ROW_TILES)


def _store_row_tiles(ref, val, first_row=0):
    for j in range(ROW_TILES):
        lo = val[:, j * LANES:(j + 1) * LANES]
        hi = val[:, HALF_MODEL + j * LANES:HALF_MODEL + (j + 1) * LANES]
        ref[_row_tile_slice(j, val.shape[0], first_row), :] = pltpu.pack_elementwise(
            [lo, hi], packed_dtype=BF16)


def _load_row_tile(ref, j, n_rows, first_row=0):
    words = ref[_row_tile_slice(j, n_rows, first_row), :]
    return tuple(pltpu.unpack_elementwise(words, index=half, packed_dtype=BF16, unpacked_dtype=F32)
                 for half in range(2))


def _ada_kernel(c_ref, w_ref, b_ref, o_ref):
    c = c_ref[...]
    s = c * _sigmoid(c)
    o_ref[...] = _dot(s.astype(BF16), w_ref[...].astype(BF16)) + b_ref[...]


def _ada(cond, w_ada, b_ada):
    rows = cond.shape[0]
    n = w_ada.shape[1]
    tn = 1536
    return pl.pallas_call(
        _ada_kernel,
        grid=(n // tn,),
        in_specs=[pl.BlockSpec((rows, D_MODEL), lambda j: (0, 0)),
                  pl.BlockSpec((D_MODEL, tn), lambda j: (0, j)),
                  pl.BlockSpec((1, tn), lambda j: (0, j))],
        out_specs=pl.BlockSpec((rows, tn), lambda j: (0, j)),
        out_shape=jax.ShapeDtypeStruct((rows, n), F32),
        compiler_params=pltpu.CompilerParams(vmem_limit_bytes=VMEM_LIMIT),
        name="ada_mod",
    )(cond, w_ada, b_ada.reshape(1, n))


def _group_maps(n_ctx_tiles):
    def ctx_map(i, *_):
        return (jnp.minimum(i, n_ctx_tiles - 1), 0)

    def lat_map(i, *_):
        return (jnp.maximum(i - n_ctx_tiles, 0), 0)

    return ctx_map, lat_map


def _token_specs(tm, n_ctx_tiles, tiles_per_latent_seq, ctx_mod_row):
    ctx_map, lat_map = _group_maps(n_ctx_tiles)

    def pos_map(i, *_):
        return (jnp.maximum(i - n_ctx_tiles, 0) % tiles_per_latent_seq, 0)

    def mod_map(i, *_):
        return (jnp.where(i < n_ctx_tiles, ctx_mod_row,
                          jnp.maximum(i - n_ctx_tiles, 0) // tiles_per_latent_seq), 0, 0)

    return [pl.BlockSpec((tm, D_MODEL), ctx_map),
            pl.BlockSpec((tm, D_MODEL), lat_map),
            pl.BlockSpec((tm, D_MODEL), pos_map),
            pl.BlockSpec((1, N_MOD, D_MODEL), mod_map)]


def _inproj_kernel(xc_ref, xl_ref, pos_ref, mod_ref, wm_ref, wt_ref, wdh_ref, wdl_ref, bd_ref,
                   o_ref, la_ref, lamin_ref, *, n_ctx_tiles):
    tail = lambda lo, hi: wt_ref[lo - COL_GATE_A:hi - COL_GATE_A, :]
    i = pl.program_id(0)

    def project(x_of):
        groups = [slice(r0, r0 + PROJ_GROUP) for r0 in range(0, TM_PROJ, PROJ_GROUP)]
        hs = [(_layer_norm(x_of(rows)) * (1.0 + mod_ref[0, 1:2, :]) + mod_ref[0, 0:1, :]).astype(BF16)
              for rows in groups]
        rs = [_split_bf16(_dot_nt(h, tail(COL_R, PROJ_COLS)), 2) for h in hs]
        steepest = None
        for rows, (r_hi, r_lo) in zip(groups, rs):
            for d in range(2):
                z = (_dot(r_hi, wdh_ref[d]) + _dot(r_lo, wdh_ref[d]) + _dot(r_hi, wdl_ref[d])
                     + bd_ref[d])
                la = _log_sigmoid(z) * (1.0 / GATE_NORMALIZER)
                la_ref[rows, d * GLA_DK:(d + 1) * GLA_DK] = la
                low = jnp.min(la, axis=0, keepdims=True)
                for c0 in range(0, GLA_DK, LANES):
                    piece = low[:, c0:c0 + LANES]
                    steepest = piece if steepest is None else jnp.minimum(steepest, piece)
        lamin_ref[...] = jnp.broadcast_to(steepest, (SUBLANES, LANES))
        for rows, h in zip(groups, hs):
            hg = 0.5 * _dot_nt(h, wm_ref[COL_G:COL_GATE_A, :])
            o_ref[rows, COL_G:COL_GATE_A] = hg * (jnp.tanh(hg) + 1.0)
        for rows, h in zip(groups, hs):
            gates = _dot_nt(h, tail(COL_GATE_A, COL_F))
            o_ref[rows, COL_GATE_A:COL_F] = jnp.tanh(0.5 * gates) + 1.0
        for rows, h in zip(groups, hs):
            o_ref[rows, :COL_G] = _dot_nt(h, wm_ref[:COL_G, :])
        for rows, h in zip(groups, hs):
            o_ref[rows, COL_F:COL_R] = _dot_nt(h, tail(COL_F, COL_R))

    @pl.when(i < n_ctx_tiles)
    def _():
        project(lambda rows: xc_ref[rows, :])

    @pl.when(i >= n_ctx_tiles)
    def _():
        project(lambda rows: xl_ref[rows, :] + pos_ref[rows, :])


def _inproj(x_ctx, x_lat, pos, mod, w_main, w_tail, w_dec, b_dec, lat_len):
    t_ctx, t_lat = x_ctx.shape[0], x_lat.shape[0]
    t_all = t_ctx + t_lat
    n_ctx_tiles = t_ctx // TM_PROJ
    kern = functools.partial(_inproj_kernel, n_ctx_tiles=n_ctx_tiles)
    specs = _token_specs(TM_PROJ, n_ctx_tiles, lat_len // TM_PROJ, mod.shape[0] - 1)
    w_dec_hi = w_dec.astype(BF16)
    w_dec_lo = (w_dec - w_dec_hi.astype(F32)).astype(BF16)

    def const(shape):
        return pl.BlockSpec(shape, lambda i: (0,) * len(shape))

    return pl.pallas_call(
        kern,
        grid=(t_all // TM_PROJ,),
        in_specs=specs + [pl.BlockSpec(w_main.shape, lambda i: (0, 0), pipeline_mode=pl.Buffered(1)),
                          pl.BlockSpec(w_tail.shape, lambda i: (0, 0), pipeline_mode=pl.Buffered(1)),
                          const((2, LANES, GLA_DK)), const((2, LANES, GLA_DK)), const((2, 1, GLA_DK))],
        out_specs=[pl.BlockSpec((TM_PROJ, COL_R), lambda i: (i, 0)),
                   pl.BlockSpec((TM_PROJ, 2 * GLA_DK), lambda i: (i, 0)),
                   pl.BlockSpec((SUBLANES, LANES), lambda i: (i, 0))],
        out_shape=[jax.ShapeDtypeStruct((t_all, COL_R), F32),
                   jax.ShapeDtypeStruct((t_all, 2 * GLA_DK), F32),
                   jax.ShapeDtypeStruct((t_all // TM_PROJ * SUBLANES, LANES), F32)],
        compiler_params=pltpu.CompilerParams(
            dimension_semantics=("arbitrary",), vmem_limit_bytes=VMEM_LIMIT),
        name="ln_inproj",
    )(x_ctx, x_lat, pos, mod, w_main, w_tail, w_dec_hi, w_dec_lo, b_dec)


def _gla_kernel(*refs, seq_len, has_s0, emit_state, exact):
    it = iter(refs)
    q_ref, k_ref, v_ref, laf_ref, lab_ref, g_ref = (next(it) for _ in range(6))
    s0_ref = next(it) if has_s0 else None
    o_ref = next(it)
    sout_ref = next(it) if emit_state else None
    cum_ref, aq_ref, ko_ref, dec_ref, op_ref, st_ref = (next(it) for _ in range(6))

    C = GLA_CHUNK
    n_chunks = seq_len // C

    def rows(n):
        if isinstance(n, int):
            return pl.ds(n * C, C)
        return pl.ds(pl.multiple_of(n * C, C), C)

    def loop(body):
        if n_chunks <= 2:
            for n in range(n_chunks):
                body(n)
        else:
            def step(m, carry):
                body(2 * m)
                body(2 * m + 1)
                return carry
            lax.fori_loop(0, n_chunks // 2, step, 0)

    rt = lax.broadcasted_iota(jnp.int32, (C, C), 0)
    ct = lax.broadcasted_iota(jnp.int32, (C, C), 1)
    tri = ((rt >= ct).astype(BF16), (ct >= rt).astype(BF16))
    row_id = lax.broadcasted_iota(jnp.int32, (C, DK_HEAD), 0)

    def cumsum_chunk(n):
        for d, la_ref in enumerate((laf_ref, lab_ref)):
            la_hi, la_lo = _split_bf16(la_ref[rows(n), :], 2)
            cum_ref[d, rows(n), :] = _dot(tri[d], la_hi) + _dot(tri[d], la_lo)

    loop(cumsum_chunk)

    query_rows = ({}, {})
    keep = ({}, {})
    for d in range(2):
        blk = C // 2
        while blk >= GLA_LEAF:
            q_parity = 1 if d == 0 else 0
            query_rows[d][blk] = ((row_id // blk) % 2) == q_parity
            qb, kb = rt // blk, ct // blk
            keep[d][blk] = ((qb % 2) == q_parity) & ((qb == kb + 1) if d == 0 else (kb == qb + 1))
            blk //= 2
        order = (rt >= ct) if d == 0 else (ct >= rt)
        keep[d][0] = ((rt // GLA_LEAF) == (ct // GLA_LEAF)) & order

    n_leaves = C // GLA_LEAF

    def per_leaf(rows_of_cum):
        return jnp.concatenate(rows_of_cum, axis=0)

    def expand(per_leaf_rows):
        return jnp.concatenate(
            [jnp.broadcast_to(per_leaf_rows[j:j + 1, :], (GLA_LEAF, DK_HEAD)) for j in range(n_leaves)],
            axis=0)

    def store_scores(n, d, acc, q_in, k_out, end):
        aq_ref[d, rows(n), 0:C] = acc.astype(BF16)
        aq_ref[d, rows(n), C:C + DK_HEAD] = q_in.astype(BF16)
        ko_ref[d, rows(n), :] = k_out.astype(BF16)
        dec_ref[d, rows(n), :] = jnp.broadcast_to(jnp.exp(end), (DK_HEAD, DK_HEAD)).T

    def block_level_scores(cum, q, k, d, blk, at_bnd_scale):
        x = (jnp.where(query_rows[d][blk], q, k) * at_bnd_scale).astype(BF16)
        return jnp.where(keep[d][blk], _dot_nt(x, x), 0.0)

    def boundary_rows(cum, d, blk):
        bnd = blk - 1 if d == 0 else blk
        return per_leaf([cum[bnd + (j * GLA_LEAF) // (2 * blk) * (2 * blk):
                             bnd + (j * GLA_LEAF) // (2 * blk) * (2 * blk) + 1, :]
                         for j in range(n_leaves)])

    def scores_exact(n, d):
        cum = cum_ref[d, rows(n), :]
        q = q_ref[rows(n), :]
        k = k_ref[rows(n), :]
        in_leaf_pos = row_id % GLA_LEAF
        acc = jnp.zeros((C, C), F32)
        for lag in range(GLA_LEAF):
            shift = lag if d == 0 else (C - lag) % C
            k_s = pltpu.roll(k, shift, 0) if lag else k
            cum_s = pltpu.roll(cum, shift, 0) if lag else cum
            paired = (in_leaf_pos >= lag) if d == 0 else (in_leaf_pos < GLA_LEAF - lag)
            term = q * k_s * jnp.exp(jnp.where(paired, cum - cum_s, 0.0))
            s = jnp.sum(jnp.where(paired, term, 0.0), axis=-1, keepdims=True)
            diagonal = (rt - ct == lag) if d == 0 else (ct - rt == lag)
            acc = acc + jnp.where(diagonal, s, 0.0)
        blk = C // 2
        while blk >= GLA_LEAF:
            w = jnp.exp(-jnp.abs(cum - expand(boundary_rows(cum, d, blk))))
            acc = acc + block_level_scores(cum, q, k, d, blk, w)
            blk //= 2
        end = cum[C - 1:C, :] if d == 0 else cum[0:1, :]
        store_scores(n, d, acc, q * jnp.exp(cum), k * jnp.exp(end - cum), end)

    def scores(n, d):
        cum = cum_ref[d, rows(n), :]
        mid = GLA_LEAF // 2 - 1 if d == 0 else GLA_LEAF // 2
        at_mid = per_leaf([cum[mid + j * GLA_LEAF:mid + j * GLA_LEAF + 1, :] for j in range(n_leaves)])
        e = cum - expand(at_mid)
        qe = q_ref[rows(n), :] * jnp.exp(e)
        ke = k_ref[rows(n), :] * jnp.exp(-e)
        acc = jnp.where(keep[d][0], _dot_nt(qe.astype(BF16), ke.astype(BF16)), 0.0)
        blk = C // 2
        while blk >= GLA_LEAF:
            through = expand(jnp.exp(-jnp.abs(at_mid - boundary_rows(cum, d, blk))))
            acc = acc + block_level_scores(cum, qe, ke, d, blk, through)
            blk //= 2
        end = cum[C - 1:C, :] if d == 0 else cum[0:1, :]
        store_scores(n, d, acc, qe * expand(jnp.exp(at_mid)), ke * expand(jnp.exp(end - at_mid)), end)

    def scores_chunk(n):
        for d in range(2):
            (scores_exact if exact else scores)(n, d)

    loop(scores_chunk)

    for d in range(2):
        if has_s0:
            st_ref[d] = s0_ref[0, d, 0]
        else:
            st_ref[d] = jnp.zeros((DK_HEAD, DV_HEAD), F32)

    def scan(n, d):
        v = v_ref[rows(n), :].astype(BF16)
        st = st_ref[d]
        op_ref[d, rows(n), :] = _dot(aq_ref[d, rows(n), :],
                                     jnp.concatenate([v, st.astype(BF16)], axis=0))
        dec = dec_ref[d, rows(n), :]
        st_ref[d] = (st * jnp.concatenate([dec] * (DV_HEAD // DK_HEAD), axis=1)
                     + _dot_tn(ko_ref[d, rows(n), :], v))

    def scan_chunk(m):
        scan(m, 0)
        scan(n_chunks - 1 - m, 1)

    loop(scan_chunk)
    if emit_state:
        sout_ref[0, 0, 0] = st_ref[0]
        sout_ref[0, 1, 0] = st_ref[1]

    g = g_ref[...]

    def finish_chunk(n):
        o = op_ref[0, rows(n), :] + op_ref[1, rows(n), :]
        ms = jnp.mean(o * o, axis=-1, keepdims=True)
        o_ref[rows(n), :] = o * lax.rsqrt(ms + LN_EPS) * g

    loop(finish_chunk)


def _gla(proj, la, g, s0, *, n_seq, seq_len, row0, emit_state, exact):
    has_s0 = s0 is not None
    blk0 = row0 // seq_len
    kern = functools.partial(_gla_kernel, seq_len=seq_len, has_s0=has_s0, emit_state=emit_state,
                             exact=exact)
    in_specs = [
        pl.BlockSpec((seq_len, DK_HEAD), lambda b, h: (blk0 + b, COL_Q // DK_HEAD + h)),
        pl.BlockSpec((seq_len, DK_HEAD), lambda b, h: (blk0 + b, COL_K // DK_HEAD + h)),
        pl.BlockSpec((seq_len, DV_HEAD), lambda b, h: (blk0 + b, COL_V // DV_HEAD + h)),
        pl.BlockSpec((seq_len, DK_HEAD), lambda b, h: (blk0 + b, h)),
        pl.BlockSpec((seq_len, DK_HEAD), lambda b, h: (blk0 + b, GLA_HEADS + h)),
        pl.BlockSpec((1, DV_HEAD), lambda b, h: (0, 0)),
    ]
    args = [proj, proj, proj, la, la, g]
    if has_s0:
        in_specs.append(pl.BlockSpec((1, 2, 1, DK_HEAD, DV_HEAD), lambda b, h: (b, 0, h, 0, 0)))
        args.append(s0)
    out_specs = [pl.BlockSpec((seq_len, DV_HEAD), lambda b, h: (b, h))]
    out_shape = [jax.ShapeDtypeStruct((n_seq * seq_len, GLA_DV), F32)]
    if emit_state:
        out_specs.append(pl.BlockSpec((1, 2, 1, DK_HEAD, DV_HEAD), lambda b, h: (b, 0, h, 0, 0)))
        out_shape.append(jax.ShapeDtypeStruct((n_seq, 2, GLA_HEADS, DK_HEAD, DV_HEAD), F32))

    res = pl.pallas_call(
        kern,
        grid=(n_seq, GLA_HEADS),
        in_specs=in_specs,
        out_specs=out_specs,
        out_shape=out_shape,
        scratch_shapes=[pltpu.VMEM((2, seq_len, DK_HEAD), F32),
                        pltpu.VMEM((2, seq_len, GLA_CHUNK + DK_HEAD), BF16),
                        pltpu.VMEM((2, seq_len, DK_HEAD), BF16),
                        pltpu.VMEM((2, seq_len, DK_HEAD), F32),
                        pltpu.VMEM((2, seq_len, DV_HEAD), F32),
                        pltpu.VMEM((2, DK_HEAD, DV_HEAD), F32)],
        compiler_params=pltpu.CompilerParams(
            dimension_semantics=("arbitrary", "arbitrary"), vmem_limit_bytes=VMEM_LIMIT),
        name="gla%s_seq%d" % ("_exact" if exact else "", seq_len),
    )(*args)
    return res


def _fnet_kernel(f_ref, cl_ref, sl_ref, cg_ref, sg_ref, o_ref, uc_ref, us_ref, *, seq_len):
    cg = cg_ref[...]
    sg = sg_ref[...]
    for grp in range(FNET_GROUPS):
        lo = grp * FNET_GROUP_DIM
        u = f_ref[:, lo:lo + FNET_GROUP_DIM].astype(BF16)
        uc_ref[:, lo:lo + FNET_GROUP_DIM] = _dot(u, cg).astype(BF16)
        us_ref[:, lo:lo + FNET_GROUP_DIM] = _dot(u, sg).astype(BF16)
    mixed = _dot(cl_ref[...], uc_ref[...]) - _dot(sl_ref[...], us_ref[...])
    o_ref[...] = mixed * (1.0 / math.sqrt(seq_len * FNET_GROUP_DIM))


def _dft_mats(n):
    j = np.arange(n, dtype=np.int64)
    ang = (2.0 * np.pi / n) * ((j[:, None] * j[None, :]) % n).astype(np.float64)
    return (jnp.asarray(np.cos(ang), dtype=F32).astype(BF16),
            jnp.asarray(np.sin(ang), dtype=F32).astype(BF16))


def _fnet(proj, *, n_seq, seq_len, row0):
    blk0 = row0 // seq_len
    cl, sl = _dft_mats(seq_len)
    cg, sg = _dft_mats(FNET_GROUP_DIM)
    kern = functools.partial(_fnet_kernel, seq_len=seq_len)
    return pl.pallas_call(
        kern,
        grid=(n_seq,),
        in_specs=[pl.BlockSpec((seq_len, FNET_DIM), lambda b: (blk0 + b, COL_F // FNET_DIM)),
                  pl.BlockSpec((seq_len, seq_len), lambda b: (0, 0)),
                  pl.BlockSpec((seq_len, seq_len), lambda b: (0, 0)),
                  pl.BlockSpec((FNET_GROUP_DIM, FNET_GROUP_DIM), lambda b: (0, 0)),
                  pl.BlockSpec((FNET_GROUP_DIM, FNET_GROUP_DIM), lambda b: (0, 0))],
        out_specs=pl.BlockSpec((seq_len, FNET_DIM), lambda b: (b, 0)),
        out_shape=jax.ShapeDtypeStruct((n_seq * seq_len, FNET_DIM), F32),
        scratch_shapes=[pltpu.VMEM((seq_len, FNET_DIM), BF16),
                        pltpu.VMEM((seq_len, FNET_DIM), BF16)],
        compiler_params=pltpu.CompilerParams(
            dimension_semantics=("arbitrary",), vmem_limit_bytes=VMEM_LIMIT),
        name="fnet_seq%d" % seq_len,
    )(proj, cl, sl, cg, sg)


def _merge_kernel(xc_ref, xl_ref, pos_ref, mod_ref, oc_ref, ol_ref, mc_ref, ml_ref,
                  g_ref, ga_ref, gb_ref, wbg_ref, wbf_ref, wo_ref, l1g_ref, l1b_ref, wrh_ref, wrl_ref,
                  br_ref, x1_ref, h2_ref, ridx_ref, rw_ref, *, n_ctx_tiles, alpha):
    i = pl.program_id(0)

    tm = x1_ref.shape[0]
    groups = [slice(r0, r0 + TM_MERGE_GROUP) for r0 in range(0, tm, TM_MERGE_GROUP)]
    shape = (TM_MERGE_GROUP, LANES)
    lane_i = lax.broadcasted_iota(jnp.int32, shape, 1)
    lane = lane_i.astype(F32)

    def compute(x_of, o_ref, mx_ref):
        branch = []
        for rows in groups:
            a = (o_ref[rows, :] * g_ref[rows, :]).astype(BF16)
            branch.append((_dot(a, wbg_ref[...]), _dot(mx_ref[rows, :].astype(BF16), wbf_ref[...])))
        mix = []
        for rows, (gla_out, fnet_out) in zip(groups, branch):
            merged = ga_ref[rows, :] * gla_out + gb_ref[rows, :] * fnet_out
            mix.append(_dot(merged.astype(BF16), wo_ref[...]))
        logits = []
        for rows, mix_g in zip(groups, mix):
            y = alpha * x_of(rows) + mod_ref[0, 2:3, :] * mix_g
            x1 = _layer_norm(y) * l1g_ref[...] + l1b_ref[...]
            x1_ref[rows, :] = x1
            h2 = _layer_norm(x1) * (1.0 + mod_ref[0, 4:5, :]) + mod_ref[0, 3:4, :]
            _store_row_tiles(h2_ref, h2, rows.start)
            h_hi, h_lo = _split_bf16(h2, 2)
            logits.append(_dot(h_hi, wrh_ref[...]) + _dot(h_lo, wrh_ref[...])
                          + _dot(h_hi, wrl_ref[...]) + br_ref[...])
        idx_out = [jnp.zeros(shape, F32) for _ in groups]
        val_out = [jnp.zeros(shape, F32) for _ in groups]
        top0 = [None] * len(groups)
        denom = [None] * len(groups)
        for kk in range(TOP_K):
            for gi in range(len(groups)):
                m = jnp.max(logits[gi], axis=-1, keepdims=True)
                sel = jnp.min(jnp.where(logits[gi] == m, lane, float(LANES)), axis=-1, keepdims=True)
                if kk == 0:
                    top0[gi] = m
                    p = jnp.ones_like(m)
                    denom[gi] = p
                else:
                    p = jnp.exp(m - top0[gi])
                    denom[gi] = denom[gi] + p
                idx_out[gi] = jnp.where(lane_i == kk, sel, idx_out[gi])
                val_out[gi] = jnp.where(lane_i == kk, p, val_out[gi])
                logits[gi] = jnp.where(lane == sel, -jnp.inf, logits[gi])
        for gi, rows in enumerate(groups):
            ridx_ref[rows, :] = idx_out[gi].astype(jnp.int32)
            rw_ref[rows, :] = val_out[gi] / denom[gi]

    @pl.when(i < n_ctx_tiles)
    def _():
        compute(lambda rows: xc_ref[rows, :], oc_ref, mc_ref)

    @pl.when(i >= n_ctx_tiles)
    def _():
        compute(lambda rows: xl_ref[rows, :] + pos_ref[rows, :], ol_ref, ml_ref)


def _merge(x_ctx, x_lat, pos, mod, o_ctx, o_lat, mixed_ctx, mixed_lat, proj,
           wbg, wbf, wo, l1g, l1b, wr_hi, wr_lo, br, lat_len, alpha):
    t_ctx, t_lat = x_ctx.shape[0], x_lat.shape[0]
    t_all = t_ctx + t_lat
    tm = TM_MERGE
    n_ctx_tiles = t_ctx // tm
    kern = functools.partial(_merge_kernel, n_ctx_tiles=n_ctx_tiles, alpha=alpha)
    specs = _token_specs(tm, n_ctx_tiles, lat_len // tm, mod.shape[0] - 1)
    ctx_map, lat_map = _group_maps(n_ctx_tiles)

    def const(shape):
        return pl.BlockSpec(shape, lambda i: (0,) * len(shape))

    in_specs = specs + [
        pl.BlockSpec((tm, GLA_DV), ctx_map),
        pl.BlockSpec((tm, GLA_DV), lat_map),
        pl.BlockSpec((tm, FNET_DIM), ctx_map),
        pl.BlockSpec((tm, FNET_DIM), lat_map),
        pl.BlockSpec((tm, GLA_DV), lambda i: (i, COL_G // GLA_DV)),
        pl.BlockSpec((tm, D_MODEL), lambda i: (i, COL_GATE_A // D_MODEL)),
        pl.BlockSpec((tm, D_MODEL), lambda i: (i, COL_GATE_B // D_MODEL)),
        const((GLA_DV, D_MODEL)), const((FNET_DIM, D_MODEL)), const((D_MODEL, D_MODEL)),
        const((1, D_MODEL)), const((1, D_MODEL)),
        const((D_MODEL, LANES)), const((D_MODEL, LANES)), const((1, LANES)),
    ]
    out_specs = [pl.BlockSpec((tm, D_MODEL), lambda i: (i, 0)),
                 pl.BlockSpec((tm * ROW_TILES, LANES), lambda i: (i, 0)),
                 pl.BlockSpec((tm, LANES), lambda i: (i, 0)),
                 pl.BlockSpec((tm, LANES), lambda i: (i, 0))]
    out_shape = [jax.ShapeDtypeStruct((t_all, D_MODEL), F32),
                 jax.ShapeDtypeStruct((t_all * ROW_TILES, LANES), ROW_DTYPE),
                 jax.ShapeDtypeStruct((t_all, LANES), jnp.int32),
                 jax.ShapeDtypeStruct((t_all, LANES), F32)]
    return pl.pallas_call(
        kern,
        grid=(t_all // tm,),
        in_specs=in_specs,
        out_specs=out_specs,
        out_shape=out_shape,
        compiler_params=pltpu.CompilerParams(
            dimension_semantics=("arbitrary",), vmem_limit_bytes=VMEM_LIMIT),
        name="merge_ln1_router",
    )(x_ctx, x_lat, pos, mod, o_ctx, o_lat, mixed_ctx, mixed_lat, proj, proj, proj,
      wbg, wbf, wo, l1g, l1b, wr_hi, wr_lo, br)


def _sc_mesh():
    return plsc.VectorSubcoreMesh(core_axis_name="c", subcore_axis_name="s")


def _sc_worker_id():
    return lax.axis_index("s") * SC_CORES + lax.axis_index("c")


def _sc_scatter_rows(src, idx, n_out):
    n_src = src.shape[0]
    w = SC_WINDOW
    n_chunks = n_src // (SC_WORKERS * w)
    copies = idx.shape[0]
    assert n_chunks % 2 == 0 and idx.shape == (copies, SC_WORKERS, n_chunks, w)

    @functools.partial(
        pl.kernel, mesh=_sc_mesh(),
        out_type=jax.ShapeDtypeStruct((n_out, ROW_TILES, LANES), ROW_DTYPE),
        scratch_types=[pltpu.VMEM((copies * n_chunks, w), jnp.int32),
                       pltpu.VMEM((2, w, ROW_TILES, LANES), ROW_DTYPE),
                       pltpu.SemaphoreType.DMA((2,)),
                       pltpu.SemaphoreType.DMA((2,))],
        name="moe_dispatch_scatter")
    def k(src_hbm, idx_hbm, out_hbm, idx_v, rows_v, rsem, wsem):
        wid = _sc_worker_id()
        base = wid * (n_chunks * w)
        for kk in range(copies):
            pltpu.sync_copy(idx_hbm.at[kk, wid], idx_v.at[pl.ds(kk * n_chunks, n_chunks)])

        def read(j, slot):
            return pltpu.make_async_copy(src_hbm.at[pl.ds(base + j * w, w)], rows_v.at[slot],
                                         rsem.at[slot])

        def scatter(j, kk, slot):
            return pltpu.make_async_copy(rows_v.at[slot], out_hbm.at[idx_v.at[kk * n_chunks + j]],
                                         wsem.at[slot])

        read(0, 0).start()

        @pl.loop(0, n_chunks, step=2)
        def _(jj):
            read(jj, 0).wait()

            @pl.when(jj > 0)
            def _():
                for kk in range(copies):
                    scatter(jj - 1, kk, 1).wait()

            read(jj + 1, 1).start()
            for kk in range(copies):
                scatter(jj, kk, 0).start()
            read(jj + 1, 1).wait()
            for kk in range(copies):
                scatter(jj, kk, 0).wait()

            @pl.when(jj + 2 < n_chunks)
            def _():
                read(jj + 2, 0).start()

            for kk in range(copies):
                scatter(jj + 1, kk, 1).start()

        for kk in range(copies):
            scatter(n_chunks - 1, kk, 1).wait()

    return k(src, idx)


def _sc_gather_rows(table, idx):
    _, n_chunks, w = idx.shape
    assert n_chunks % 2 == 0 and idx.shape[0] == SC_WORKERS and w == SC_WINDOW
    n_out = SC_WORKERS * n_chunks * w

    @functools.partial(
        pl.kernel, mesh=_sc_mesh(),
        out_type=jax.ShapeDtypeStruct((n_out, ROW_TILES, LANES), ROW_DTYPE),
        scratch_types=[pltpu.VMEM((n_chunks, w), jnp.int32),
                       pltpu.VMEM((2, w, ROW_TILES, LANES), ROW_DTYPE),
                       pltpu.SemaphoreType.DMA((2,)),
                       pltpu.SemaphoreType.DMA((2,))],
        name="moe_combine_gather")
    def k(table_hbm, idx_hbm, out_hbm, idx_v, rows_v, gsem, wsem):
        wid = _sc_worker_id()
        base = wid * (n_chunks * w)
        pltpu.sync_copy(idx_hbm.at[wid], idx_v)

        def gather(j, slot):
            return pltpu.make_async_copy(table_hbm.at[idx_v.at[j]], rows_v.at[slot], gsem.at[slot])

        def write(j, slot):
            return pltpu.make_async_copy(rows_v.at[slot], out_hbm.at[pl.ds(base + j * w, w)],
                                         wsem.at[slot])

        gather(0, 0).start()

        @pl.loop(0, n_chunks, step=2)
        def _(jj):
            gather(jj, 0).wait()

            @pl.when(jj > 0)
            def _():
                write(jj - 1, 1).wait()

            gather(jj + 1, 1).start()
            write(jj, 0).start()
            gather(jj + 1, 1).wait()
            write(jj, 0).wait()

            @pl.when(jj + 2 < n_chunks)
            def _():
                gather(jj + 2, 0).start()

            write(jj + 1, 1).start()

        write(n_chunks - 1, 1).wait()

    return k(table, idx)


def _moe_kernel(be_ref, nu_ref, nv_ref, slot_ref, nxt_ref, x_ref, wgu_hbm, bgu_ref, wd_hbm, bd_ref,
                o_ref, wgu_st, wd_st, wgu_bf, wd_bf, xb_ref, sem):
    b = pl.program_id(0)
    e = be_ref[b]
    prev = be_ref[jnp.maximum(b - 1, 0)]
    active = b < nu_ref[0]
    changed = (b == 0) | (e != prev)

    def weight_copies(expert, s):
        return (pltpu.make_async_copy(wgu_hbm.at[expert], wgu_st.at[s], sem.at[0, s]),
                pltpu.make_async_copy(wd_hbm.at[expert], wd_st.at[s], sem.at[1, s]))

    @pl.when(active & changed)
    def _():
        s = slot_ref[b]

        @pl.when(b == 0)
        def _():
            for cp in weight_copies(e, s):
                cp.start()

        for cp in weight_copies(e, s):
            cp.wait()
        wgu_bf[...] = wgu_st[s].astype(BF16)
        wd_bf[...] = wd_st[s].astype(BF16)
        nxt = nxt_ref[b]

        @pl.when(nxt >= 0)
        def _():
            for cp in weight_copies(nxt, 1 - s):
                cp.start()

    n_valid = nv_ref[b]

    def expert_mlp(n_rows):
        valid = lax.broadcasted_iota(jnp.int32, (n_rows, LANES), 0) < n_valid
        for j in range(ROW_TILES):
            for half, xj in enumerate(_load_row_tile(x_ref, j, n_rows)):
                c0 = half * HALF_MODEL + j * LANES
                xb_ref[0:n_rows, c0:c0 + LANES] = jnp.where(valid, xj, 0.0).astype(BF16)
        gu = _dot(xb_ref[0:n_rows, :], wgu_bf[...]) + bgu_ref[0]
        gate = jnp.minimum(gu[:, :D_EXPERT], SWIGLU_LIMIT)
        up = jnp.clip(gu[:, D_EXPERT:], -SWIGLU_LIMIT, SWIGLU_LIMIT)
        glu = gate * _sigmoid(gate * SWIGLU_ALPHA)
        act = ((up + 1.0) * glu).astype(BF16)
        _store_row_tiles(o_ref, _dot(act, wd_bf[...]) + bd_ref[0])

    for n_rows in range(MOE_STEP, MOE_ROWS + 1, MOE_STEP):
        @pl.when(active & (n_valid > n_rows - MOE_STEP) & (n_valid <= n_rows))
        def _(n_rows=n_rows):
            expert_mlp(n_rows)


def _moe(tables, xs, w_gate_up, b_gate_up, w_down, b_down):
    p_rows = xs.shape[0] // ROW_TILES
    n_blocks = p_rows // MOE_ROWS

    def blk(b, be, nu, *_):
        return jnp.minimum(b, nu[0] - 1)

    def expert(b, be, nu, *_):
        return (be[blk(b, be, nu)], 0, 0)

    def rows(b, be, nu, *_):
        return (blk(b, be, nu), 0)

    grid_spec = pltpu.PrefetchScalarGridSpec(
        num_scalar_prefetch=len(tables),
        grid=(n_blocks,),
        in_specs=[
            pl.BlockSpec((MOE_ROWS * ROW_TILES, LANES), rows),
            pl.BlockSpec(memory_space=pl.ANY),
            pl.BlockSpec((1, 1, 2 * D_EXPERT), expert),
            pl.BlockSpec(memory_space=pl.ANY),
            pl.BlockSpec((1, 1, D_MODEL), expert),
        ],
        out_specs=pl.BlockSpec((MOE_ROWS * ROW_TILES, LANES), rows),
        scratch_shapes=[pltpu.VMEM((2, D_MODEL, 2 * D_EXPERT), F32),
                        pltpu.VMEM((2, D_EXPERT, D_MODEL), F32),
                        pltpu.VMEM((D_MODEL, 2 * D_EXPERT), BF16),
                        pltpu.VMEM((D_EXPERT, D_MODEL), BF16),
                        pltpu.VMEM((MOE_ROWS, D_MODEL), BF16),
                        pltpu.SemaphoreType.DMA((2, 2))],
    )
    return pl.pallas_call(
        _moe_kernel,
        grid_spec=grid_spec,
        out_shape=jax.ShapeDtypeStruct((p_rows * ROW_TILES, LANES), ROW_DTYPE),
        compiler_params=pltpu.CompilerParams(
            dimension_semantics=("arbitrary",), vmem_limit_bytes=VMEM_LIMIT),
        name="moe_grouped_mlp",
    )(*tables, xs, w_gate_up, b_gate_up.reshape(N_EXPERTS, 1, 2 * D_EXPERT), w_down,
      b_down.reshape(N_EXPERTS, 1, D_MODEL))


def _combine_kernel(x1_ref, y0_ref, y1_ref, y2_ref, y3_ref, rw_ref, mod_ref, g_ref, b_ref, o_ref,
                    *, alpha):
    rw = rw_ref[...]
    y_refs = (y0_ref, y1_ref, y2_ref, y3_ref)
    pieces = [None] * (2 * ROW_TILES)
    for j in range(ROW_TILES):
        for kk in range(TOP_K):
            for half, yj in enumerate(_load_row_tile(y_refs[kk], j, rw.shape[0])):
                term = rw[:, kk:kk + 1] * yj
                slot = half * ROW_TILES + j
                pieces[slot] = term if kk == 0 else pieces[slot] + term
    ff = jnp.concatenate(pieces, axis=-1)
    y = alpha * x1_ref[...] + mod_ref[0, 5:6, :] * ff
    o_ref[...] = _layer_norm(y) * g_ref[...] + b_ref[...]


def _combine(x1, yg, rw, mod, l2g, l2b, *, row0, n_rows, mod_map, alpha):
    tm = TM_MIX
    t0 = row0 // tm
    tiles = n_rows // tm
    kern = functools.partial(_combine_kernel, alpha=alpha)

    def y_spec(kk):
        return pl.BlockSpec((tm * ROW_TILES, LANES), lambda i: (kk * tiles + i, 0))

    return pl.pallas_call(
        kern,
        grid=(n_rows // tm,),
        in_specs=[pl.BlockSpec((tm, D_MODEL), lambda i: (t0 + i, 0))]
        + [y_spec(kk) for kk in range(TOP_K)]
        + [pl.BlockSpec((tm, LANES), lambda i: (t0 + i, 0)),
           pl.BlockSpec((1, N_MOD, D_MODEL), mod_map),
           pl.BlockSpec((1, D_MODEL), lambda i: (0, 0)),
           pl.BlockSpec((1, D_MODEL), lambda i: (0, 0))],
        out_specs=pl.BlockSpec((tm, D_MODEL), lambda i: (i, 0)),
        out_shape=jax.ShapeDtypeStruct((n_rows, D_MODEL), F32),
        compiler_params=pltpu.CompilerParams(
            dimension_semantics=("arbitrary",), vmem_limit_bytes=VMEM_LIMIT),
        name="combine_ln2",
    )(x1, yg, yg, yg, yg, rw, mod, l2g, l2b)


def _route_kernel(ridx_ref, dest_ref, cnt_ref, run_ref, bst_ref):
    phase = pl.program_id(0)
    i = pl.program_id(1)
    tm = ridx_ref.shape[0]
    ridx = ridx_ref[...]
    lane = lax.broadcasted_iota(jnp.int32, (tm, LANES), 1)
    hits = [ridx[:, kk:kk + 1] == lane for kk in range(TOP_K)]
    chosen = jnp.where(hits[0], 1.0, 0.0)
    for kk in range(1, TOP_K):
        chosen = chosen + jnp.where(hits[kk], 1.0, 0.0)
    colsum = jnp.sum(chosen, axis=0, keepdims=True)

    @pl.when((phase == 0) & (i == 0))
    def _():
        run_ref[...] = jnp.zeros_like(run_ref)

    @pl.when(phase == 0)
    def _():
        run_ref[...] = run_ref[...] + colsum

    @pl.when((phase == 1) & (i == 0))
    def _():
        counts = run_ref[...]
        cnt_ref[...] = counts
        blocks = jnp.floor((counts + (MOE_ROWS - 1.0)) * (1.0 / MOE_ROWS))
        r = lax.broadcasted_iota(jnp.int32, (LANES, LANES), 0)
        c = lax.broadcasted_iota(jnp.int32, (LANES, LANES), 1)
        before = jnp.dot(blocks, (r < c).astype(F32), precision=HIGHEST, preferred_element_type=F32)
        bst_ref[...] = before * float(MOE_ROWS)
        run_ref[...] = jnp.zeros_like(run_ref)

    @pl.when(phase == 1)
    def _():
        rt = lax.broadcasted_iota(jnp.int32, (tm, tm), 0)
        ct = lax.broadcasted_iota(jnp.int32, (tm, tm), 1)
        earlier = _dot((ct < rt).astype(BF16), chosen.astype(BF16))
        row_of = bst_ref[0:1, :] + run_ref[0:1, :] + earlier
        out = jnp.take_along_axis(row_of, ridx, axis=1)
        dest_ref[...] = out.T[0:SUBLANES, :].astype(jnp.int32)
        run_ref[...] = run_ref[...] + colsum


def _route(ridx):
    t_all = ridx.shape[0]
    tm = TM_ROUTE
    return pl.pallas_call(
        _route_kernel,
        grid=(2, t_all // tm),
        in_specs=[pl.BlockSpec((tm, LANES), lambda p, i: (i, 0))],
        out_specs=[pl.BlockSpec((SUBLANES, tm), lambda p, i: (0, i * p)),
                   pl.BlockSpec((SUBLANES, LANES), lambda p, i: (0, 0))],
        out_shape=[jax.ShapeDtypeStruct((SUBLANES, t_all), jnp.int32),
                   jax.ShapeDtypeStruct((SUBLANES, LANES), F32)],
        scratch_shapes=[pltpu.VMEM((SUBLANES, LANES), F32),
                        pltpu.VMEM((SUBLANES, LANES), F32)],
        compiler_params=pltpu.CompilerParams(
            dimension_semantics=("arbitrary", "arbitrary"), vmem_limit_bytes=VMEM_LIMIT),
        name="moe_route",
    )(ridx)


def _routing_tables(counts, n_blocks):
    experts = jnp.arange(N_EXPERTS, dtype=jnp.int32)
    blocks_per = (counts + MOE_ROWS - 1) // MOE_ROWS
    bends = jnp.cumsum(blocks_per)
    bstarts = bends - blocks_per
    blocks = jnp.arange(n_blocks, dtype=jnp.int32)
    block_expert = jnp.minimum(
        jnp.sum((bends[None, :] <= blocks[:, None]).astype(jnp.int32), axis=1), N_EXPERTS - 1)
    n_used = bends[-1:].astype(jnp.int32)
    owner = block_expert[:, None] == experts[None, :]

    def per_block(table):
        return jnp.sum(jnp.where(owner, table[None, :], 0), axis=1)

    n_valid = jnp.clip(per_block(counts) - (blocks - per_block(bstarts)) * MOE_ROWS,
                       0, MOE_ROWS).astype(jnp.int32)
    present = blocks_per > 0
    ordinal = jnp.cumsum(present.astype(jnp.int32)) - 1
    later = lax.cummin(jnp.where(present, experts, N_EXPERTS), reverse=True)
    succ = jnp.concatenate([later[1:], jnp.full((1,), N_EXPERTS, jnp.int32)])
    succ = jnp.where(succ >= N_EXPERTS, -1, succ)
    stage_slot = (per_block(ordinal) % 2).astype(jnp.int32)
    next_expert = per_block(succ).astype(jnp.int32)
    return (block_expert.astype(jnp.int32), n_used, n_valid, stage_slot, next_expert)


def _pos_embed_2d(n_tokens):
    rows = n_tokens // GRID_W
    r = np.repeat(np.arange(rows), GRID_W).astype(np.float32)
    col = np.tile(np.arange(GRID_W), rows).astype(np.float32)
    quarter = D_MODEL // 4
    omega = (np.float32(1.0)
             / np.power(np.float32(10000.0), np.arange(quarter, dtype=np.float32) / np.float32(quarter)))
    er = (r[:, None] * omega).astype(np.float64)
    ec = (col[:, None] * omega).astype(np.float64)
    table = np.concatenate([np.sin(er), np.cos(er), np.sin(ec), np.cos(ec)], axis=-1)
    return jnp.asarray(table, dtype=F32)


def _split_w_in(w):
    wt = w.T
    o_r = 2 * GLA_DK + 2 * GLA_DV
    o_f = o_r + DECAY_RANK
    o_gate = o_f + FNET_DIM
    row_scale = jnp.where(jnp.arange(o_r) < GLA_DK, DK_HEAD ** -0.5, 1.0).astype(w.dtype)
    w_main = (wt[:o_r] * row_scale[:, None]).astype(BF16)
    pad = jnp.zeros((LANES - DECAY_RANK, w.shape[0]), BF16)
    w_tail = jnp.concatenate([wt[o_gate:].astype(BF16), wt[o_f:o_gate].astype(BF16),
                              wt[o_r:o_f].astype(BF16), pad], axis=0)
    return w_main, w_tail


def kernel(x_prompt, x_sample, state_gla, c, c_ctx, w_ada, b_ada, w_in, w_dec_fwd, b_dec_fwd,
           w_dec_bwd, b_dec_bwd, gla_norm_g, w_br_gla, w_br_fnet, w_out, ln1_g, ln1_b, w_router,
           b_router, w_gate_up, b_gate_up, w_down, b_down, ln2_g, ln2_b):
    n_req, ctx_len, _ = x_prompt.shape
    n_lat, lat_len, _ = x_sample.shape
    depth = w_in.shape[0]
    alpha = (2.0 * depth) ** 0.25
    t_ctx = n_req * ctx_len
    t_lat = n_lat * lat_len
    t_all = t_ctx + t_lat

    x_ctx = x_prompt.reshape(t_ctx, D_MODEL)
    x_lat = x_sample.reshape(t_lat, D_MODEL)
    pos = _pos_embed_2d(lat_len)
    zero_pos = jnp.zeros_like(pos)

    cond_rows = -(-(n_lat + 1) // SUBLANES) * SUBLANES
    cond = jnp.zeros((cond_rows, D_MODEL), F32).at[:n_lat].set(c).at[cond_rows - 1].set(c_ctx)

    n_moe_blocks = (t_all * TOP_K) // MOE_ROWS + N_EXPERTS
    tok_chunks = t_all // (SC_WORKERS * SC_WINDOW)
    states = []
    for l in range(depth):
        mod = _ada(cond, w_ada[l], b_ada[l]).reshape(cond_rows, N_MOD, D_MODEL)
        layer_pos = pos if l == 0 else zero_pos
        w_dec = jnp.zeros((2, LANES, GLA_DK), F32)
        w_dec = w_dec.at[0, :DECAY_RANK].set(w_dec_fwd[l]).at[1, :DECAY_RANK].set(w_dec_bwd[l])
        b_dec = jnp.stack([b_dec_fwd[l], b_dec_bwd[l]]).reshape(2, 1, GLA_DK)
        proj, la, la_min = _inproj(x_ctx, x_lat, layer_pos, mod, *_split_w_in(w_in[l]), w_dec, b_dec,
                                   lat_len)

        norm_g = gla_norm_g[l].reshape(1, DV_HEAD)
        s0_lat = state_gla[:, l]

        def gla_both(exact):
            o_c, s_c = _gla(proj, la, norm_g, None, n_seq=n_req, seq_len=ctx_len, row0=0,
                            emit_state=True, exact=exact)
            (o_l,) = _gla(proj, la, norm_g, s0_lat, n_seq=n_lat, seq_len=lat_len, row0=t_ctx,
                          emit_state=False, exact=exact)
            return o_c, s_c, o_l

        fast_ok = jnp.min(la_min) >= -GLA_FAST_MAX_STEP_DECAY
        o_ctx, s_new, o_lat = lax.cond(fast_ok, lambda: gla_both(False), lambda: gla_both(True))
        states.append(s_new)

        mixed_ctx = _fnet(proj, n_seq=n_req, seq_len=ctx_len, row0=0)
        mixed_lat = _fnet(proj, n_seq=n_lat, seq_len=lat_len, row0=t_ctx)

        wr = jnp.zeros((D_MODEL, LANES), F32).at[:, :N_EXPERTS].set(w_router[l])
        br = jnp.full((1, LANES), -1e30, F32).at[0, :N_EXPERTS].set(b_router[l])
        wr_hi = wr.astype(BF16)
        wr_lo = (wr - wr_hi.astype(F32)).astype(BF16)
        x1, h2, ridx, rw = _merge(
            x_ctx, x_lat, layer_pos, mod, o_ctx, o_lat, mixed_ctx, mixed_lat, proj,
            w_br_gla[l].astype(BF16), w_br_fnet[l].astype(BF16), (0.5 * w_out[l]).astype(BF16),
            ln1_g[l].reshape(1, D_MODEL), ln1_b[l].reshape(1, D_MODEL), wr_hi, wr_lo, br, lat_len,
            alpha)

        dest, counts = _route(ridx)
        moe_tables = _routing_tables(counts[0, :N_EXPERTS].astype(jnp.int32), n_moe_blocks)
        dest = dest[:TOP_K]
        scatter_idx = dest.reshape(TOP_K, SC_WORKERS, tok_chunks, SC_WINDOW)
        p_rows = n_moe_blocks * MOE_ROWS
        xs = _sc_scatter_rows(h2.reshape(t_all, ROW_TILES, LANES), scatter_idx, p_rows)
        yb = _moe(moe_tables, xs.reshape(p_rows * ROW_TILES, LANES),
                  w_gate_up[l], b_gate_up[l], w_down[l], b_down[l])
        yb = yb.reshape(p_rows, ROW_TILES, LANES)

        def gathered(row0, n_rows):
            idx = dest[:, row0:row0 + n_rows].reshape(SC_WORKERS, -1, SC_WINDOW)
            return _sc_gather_rows(yb, idx).reshape(TOP_K * n_rows * ROW_TILES, LANES)

        l2g = ln2_g[l].reshape(1, D_MODEL)
        l2b = ln2_b[l].reshape(1, D_MODEL)
        tiles_per_seq = lat_len // TM_MIX
        yg_ctx = gathered(0, t_ctx)
        yg_lat = gathered(t_ctx, t_lat)
        x_ctx = _combine(x1, yg_ctx, rw, mod, l2g, l2b, row0=0, n_rows=t_ctx,
                         mod_map=lambda i: (cond_rows - 1, 0, 0), alpha=alpha)
        x_lat = _combine(x1, yg_lat, rw, mod, l2g, l2b, row0=t_ctx, n_rows=t_lat,
                         mod_map=lambda i: (i // tiles_per_seq, 0, 0), alpha=alpha)

    y_prompt = x_ctx.reshape(x_prompt.shape)
    y_sample = x_lat.reshape(x_sample.shape)
    new_state = jnp.stack(states, axis=1).astype(x_prompt.dtype)
    return (y_prompt, y_sample, new_state)
```

```python
import functools
import math

import numpy as np
import jax
import jax.numpy as jnp
from jax import lax
from jax.experimental import pallas as pl
from jax.experimental.pallas import tpu as pltpu
from jax.experimental.pallas import tpu_sc as plsc

F32 = jnp.float32
BF16 = jnp.bfloat16

D_MODEL = 1024
GRID_W = 64
GLA_HEADS = 4
DK_HEAD = 128
DV_HEAD = 256
GLA_DK = GLA_HEADS * DK_HEAD
GLA_DV = GLA_HEADS * DV_HEAD
DECAY_RANK = 16
GATE_NORMALIZER = 16.0
FNET_GROUPS = 4
FNET_GROUP_DIM = 128
FNET_DIM = FNET_GROUPS * FNET_GROUP_DIM
N_EXPERTS = 32
TOP_K = 4
D_EXPERT = 1024
SWIGLU_LIMIT = 7.0
SWIGLU_ALPHA = 1.702
LN_EPS = 1e-6
N_MOD = 6

LANES = 128
SUBLANES = 8
HALF_MODEL = D_MODEL // 2
ROW_TILES = HALF_MODEL // LANES
ROW_DTYPE = jnp.uint32
COL_Q = 0
COL_K = GLA_DK
COL_V = 2 * GLA_DK
COL_G = COL_V + GLA_DV
COL_GATE_A = COL_G + GLA_DV
COL_GATE_B = COL_GATE_A + D_MODEL
COL_F = COL_GATE_B + D_MODEL
COL_R = COL_F + FNET_DIM
PROJ_COLS = COL_R + LANES

GLA_CHUNK = 128
GLA_LEAF = 16
assert GLA_CHUNK == DK_HEAD
GLA_FAST_MAX_STEP_DECAY = 8.0
TM_PROJ = 512
PROJ_GROUP = 256
TM_MIX = 256
TM_MERGE = 512
TM_MERGE_GROUP = 256
TM_ROUTE = 1024
MOE_ROWS = 512
MOE_STEP = 128
VMEM_LIMIT = 56 * 1024 * 1024

SC_CORES = 2
SC_SUBCORES = 16
SC_WORKERS = SC_CORES * SC_SUBCORES
SC_WINDOW = 64

HIGHEST = lax.Precision.HIGHEST


def _layer_norm(x):
    mu = jnp.mean(x, axis=-1, keepdims=True)
    xc = x - mu
    var = jnp.mean(xc * xc, axis=-1, keepdims=True)
    return xc * lax.rsqrt(var + LN_EPS)


def _sigmoid(x):
    return 0.5 * jnp.tanh(0.5 * x) + 0.5


def _log_sigmoid(z):
    return jnp.minimum(z, 0.0) - jnp.log(1.0 + jnp.exp(-jnp.abs(z)))


def _dot(a, b):
    return jnp.dot(a, b, preferred_element_type=F32)


def _split_bf16(x, terms):
    parts = []
    for _ in range(terms):
        p = x.astype(BF16)
        parts.append(p)
        x = x - p.astype(F32)
    return parts


def _dot_nt(a, b):
    return lax.dot_general(a, b, (((1,), (1,)), ((), ())), preferred_element_type=F32)


def _dot_tn(a, b):
    return lax.dot_general(a, b, (((0,), (0,)), ((), ())), preferred_element_type=F32)


def _row_tile_slice(j, n_rows, first_row=0):
    return pl.ds(first_row * ROW_TILES + j, n_rows, stride=ROW_TILES)


def _store_row_tiles(ref, val, first_row=0):
    for j in range(ROW_TILES):
        lo = val[:, j * LANES:(j + 1) * LANES]
        hi = val[:, HALF_MODEL + j * LANES:HALF_MODEL + (j + 1) * LANES]
        ref[_row_tile_slice(j, val.shape[0], first_row), :] = pltpu.pack_elementwise(
            [lo, hi], packed_dtype=BF16)


def _load_row_tile(ref, j, n_rows, first_row=0):
    words = ref[_row_tile_slice(j, n_rows, first_row), :]
    return tuple(pltpu.unpack_elementwise(words, index=half, packed_dtype=BF16, unpacked_dtype=F32)
                 for half in range(2))


def _ada_kernel(c_ref, w_ref, b_ref, o_ref):
    c = c_ref[...]
    s = c * _sigmoid(c)
    o_ref[...] = _dot(s.astype(BF16), w_ref[...].astype(BF16)) + b_ref[...]


def _ada(cond, w_ada, b_ada):
    rows = cond.shape[0]
    n = w_ada.shape[1]
    tn = 1536
    return pl.pallas_call(
        _ada_kernel,
        grid=(n // tn,),
        in_specs=[pl.BlockSpec((rows, D_MODEL), lambda j: (0, 0)),
                  pl.BlockSpec((D_MODEL, tn), lambda j: (0, j)),
                  pl.BlockSpec((1, tn), lambda j: (0, j))],
        out_specs=pl.BlockSpec((rows, tn), lambda j: (0, j)),
        out_shape=jax.ShapeDtypeStruct((rows, n), F32),
        compiler_params=pltpu.CompilerParams(vmem_limit_bytes=VMEM_LIMIT),
        name="ada_mod",
    )(cond, w_ada, b_ada.reshape(1, n))


def _group_maps(n_ctx_tiles):
    def ctx_map(i, *_):
        return (jnp.minimum(i, n_ctx_tiles - 1), 0)

    def lat_map(i, *_):
        return (jnp.maximum(i - n_ctx_tiles, 0), 0)

    return ctx_map, lat_map


def _token_specs(tm, n_ctx_tiles, tiles_per_latent_seq, ctx_mod_row):
    ctx_map, lat_map = _group_maps(n_ctx_tiles)

    def pos_map(i, *_):
        return (jnp.maximum(i - n_ctx_tiles, 0) % tiles_per_latent_seq, 0)

    def mod_map(i, *_):
        return (jnp.where(i < n_ctx_tiles, ctx_mod_row,
                          jnp.maximum(i - n_ctx_tiles, 0) // tiles_per_latent_seq), 0, 0)

    return [pl.BlockSpec((tm, D_MODEL), ctx_map),
            pl.BlockSpec((tm, D_MODEL), lat_map),
            pl.BlockSpec((tm, D_MODEL), pos_map),
            pl.BlockSpec((1, N_MOD, D_MODEL), mod_map)]


def _inproj_kernel(xc_ref, xl_ref, pos_ref, mod_ref, wm_ref, wt_ref, wdh_ref, wdl_ref, bd_ref,
                   o_ref, la_ref, lamin_ref, *, n_ctx_tiles):
    tail = lambda lo, hi: wt_ref[lo - COL_GATE_A:hi - COL_GATE_A, :]
    i = pl.program_id(0)

    def project(x_of):
        groups = [slice(r0, r0 + PROJ_GROUP) for r0 in range(0, TM_PROJ, PROJ_GROUP)]
        hs = [(_layer_norm(x_of(rows)) * (1.0 + mod_ref[0, 1:2, :]) + mod_ref[0, 0:1, :]).astype(BF16)
              for rows in groups]
        rs = [_split_bf16(_dot_nt(h, tail(COL_R, PROJ_COLS)), 2) for h in hs]
        steepest = None
        for rows, (r_hi, r_lo) in zip(groups, rs):
            for d in range(2):
                z = (_dot(r_hi, wdh_ref[d]) + _dot(r_lo, wdh_ref[d]) + _dot(r_hi, wdl_ref[d])
                     + bd_ref[d])
                la = _log_sigmoid(z) * (1.0 / GATE_NORMALIZER)
                la_ref[rows, d * GLA_DK:(d + 1) * GLA_DK] = la
                low = jnp.min(la, axis=0, keepdims=True)
                for c0 in range(0, GLA_DK, LANES):
                    piece = low[:, c0:c0 + LANES]
                    steepest = piece if steepest is None else jnp.minimum(steepest, piece)
        lamin_ref[...] = jnp.broadcast_to(steepest, (SUBLANES, LANES))
        for rows, h in zip(groups, hs):
            hg = 0.5 * _dot_nt(h, wm_ref[COL_G:COL_GATE_A, :])
            o_ref[rows, COL_G:COL_GATE_A] = hg * (jnp.tanh(hg) + 1.0)
        for rows, h in zip(groups, hs):
            gates = _dot_nt(h, tail(COL_GATE_A, COL_F))
            o_ref[rows, COL_GATE_A:COL_F] = jnp.tanh(0.5 * gates) + 1.0
        for rows, h in zip(groups, hs):
            o_ref[rows, :COL_G] = _dot_nt(h, wm_ref[:COL_G, :])
        for rows, h in zip(groups, hs):
            o_ref[rows, COL_F:COL_R] = _dot_nt(h, tail(COL_F, COL_R))

    @pl.when(i < n_ctx_tiles)
    def _():
        project(lambda rows: xc_ref[rows, :])

    @pl.when(i >= n_ctx_tiles)
    def _():
        project(lambda rows: xl_ref[rows, :] + pos_ref[rows, :])


def _inproj(x_ctx, x_lat, pos, mod, w_main, w_tail, w_dec, b_dec, lat_len):
    t_ctx, t_lat = x_ctx.shape[0], x_lat.shape[0]
    t_all = t_ctx + t_lat
    n_ctx_tiles = t_ctx // TM_PROJ
    kern = functools.partial(_inproj_kernel, n_ctx_tiles=n_ctx_tiles)
    specs = _token_specs(TM_PROJ, n_ctx_tiles, lat_len // TM_PROJ, mod.shape[0] - 1)
    w_dec_hi = w_dec.astype(BF16)
    w_dec_lo = (w_dec - w_dec_hi.astype(F32)).astype(BF16)

    def const(shape):
        return pl.BlockSpec(shape, lambda i: (0,) * len(shape))

    return pl.pallas_call(
        kern,
        grid=(t_all // TM_PROJ,),
        in_specs=specs + [pl.BlockSpec(w_main.shape, lambda i: (0, 0), pipeline_mode=pl.Buffered(1)),
                          pl.BlockSpec(w_tail.shape, lambda i: (0, 0), pipeline_mode=pl.Buffered(1)),
                          const((2, LANES, GLA_DK)), const((2, LANES, GLA_DK)), const((2, 1, GLA_DK))],
        out_specs=[pl.BlockSpec((TM_PROJ, COL_R), lambda i: (i, 0)),
                   pl.BlockSpec((TM_PROJ, 2 * GLA_DK), lambda i: (i, 0)),
                   pl.BlockSpec((SUBLANES, LANES), lambda i: (i, 0))],
        out_shape=[jax.ShapeDtypeStruct((t_all, COL_R), F32),
                   jax.ShapeDtypeStruct((t_all, 2 * GLA_DK), F32),
                   jax.ShapeDtypeStruct((t_all // TM_PROJ * SUBLANES, LANES), F32)],
        compiler_params=pltpu.CompilerParams(
            dimension_semantics=("arbitrary",), vmem_limit_bytes=VMEM_LIMIT),
        name="ln_inproj",
    )(x_ctx, x_lat, pos, mod, w_main, w_tail, w_dec_hi, w_dec_lo, b_dec)


def _gla_kernel(*refs, seq_len, has_s0, emit_state, exact):
    it = iter(refs)
    q_ref, k_ref, v_ref, laf_ref, lab_ref, g_ref = (next(it) for _ in range(6))
    s0_ref = next(it) if has_s0 else None
    o_ref = next(it)
    sout_ref = next(it) if emit_state else None
    cum_ref, aq_ref, ko_ref, dec_ref, op_ref, st_ref = (next(it) for _ in range(6))

    C = GLA_CHUNK
    n_chunks = seq_len // C

    def rows(n):
        if isinstance(n, int):
            return pl.ds(n * C, C)
        return pl.ds(pl.multiple_of(n * C, C), C)

    def loop(body):
        if n_chunks <= 2:
            for n in range(n_chunks):
                body(n)
        else:
            def step(m, carry):
                body(2 * m)
                body(2 * m + 1)
                return carry
            lax.fori_loop(0, n_chunks // 2, step, 0)

    rt = lax.broadcasted_iota(jnp.int32, (C, C), 0)
    ct = lax.broadcasted_iota(jnp.int32, (C, C), 1)
    tri = ((rt >= ct).astype(BF16), (ct >= rt).astype(BF16))
    row_id = lax.broadcasted_iota(jnp.int32, (C, DK_HEAD), 0)

    def cumsum_chunk(n):
        for d, la_ref in enumerate((laf_ref, lab_ref)):
            la_hi, la_lo = _split_bf16(la_ref[rows(n), :], 2)
            cum_ref[d, rows(n), :] = _dot(tri[d], la_hi) + _dot(tri[d], la_lo)

    loop(cumsum_chunk)

    query_rows = ({}, {})
    keep = ({}, {})
    for d in range(2):
        blk = C // 2
        while blk >= GLA_LEAF:
            q_parity = 1 if d == 0 else 0
            query_rows[d][blk] = ((row_id // blk) % 2) == q_parity
            qb, kb = rt // blk, ct // blk
            keep[d][blk] = ((qb % 2) == q_parity) & ((qb == kb + 1) if d == 0 else (kb == qb + 1))
            blk //= 2
        order = (rt >= ct) if d == 0 else (ct >= rt)
        keep[d][0] = ((rt // GLA_LEAF) == (ct // GLA_LEAF)) & order

    n_leaves = C // GLA_LEAF

    def per_leaf(rows_of_cum):
        return jnp.concatenate(rows_of_cum, axis=0)

    def expand(per_leaf_rows):
        return jnp.concatenate(
            [jnp.broadcast_to(per_leaf_rows[j:j + 1, :], (GLA_LEAF, DK_HEAD)) for j in range(n_leaves)],
            axis=0)

    def store_scores(n, d, acc, q_in, k_out, end):
        aq_ref[d, rows(n), 0:C] = acc.astype(BF16)
        aq_ref[d, rows(n), C:C + DK_HEAD] = q_in.astype(BF16)
        ko_ref[d, rows(n), :] = k_out.astype(BF16)
        dec_ref[d, rows(n), :] = jnp.broadcast_to(jnp.exp(end), (DK_HEAD, DK_HEAD)).T

    def block_level_scores(cum, q, k, d, blk, at_bnd_scale):
        x = (jnp.where(query_rows[d][blk], q, k) * at_bnd_scale).astype(BF16)
        return jnp.where(keep[d][blk], _dot_nt(x, x), 0.0)

    def boundary_rows(cum, d, blk):
        bnd = blk - 1 if d == 0 else blk
        return per_leaf([cum[bnd + (j * GLA_LEAF) // (2 * blk) * (2 * blk):
                             bnd + (j * GLA_LEAF) // (2 * blk) * (2 * blk) + 1, :]
                         for j in range(n_leaves)])

    def scores_exact(n, d):
        cum = cum_ref[d, rows(n), :]
        q = q_ref[rows(n), :]
        k = k_ref[rows(n), :]
        in_leaf_pos = row_id % GLA_LEAF
        acc = jnp.zeros((C, C), F32)
        for lag in range(GLA_LEAF):
            shift = lag if d == 0 else (C - lag) % C
            k_s = pltpu.roll(k, shift, 0) if lag else k
            cum_s = pltpu.roll(cum, shift, 0) if lag else cum
            paired = (in_leaf_pos >= lag) if d == 0 else (in_leaf_pos < GLA_LEAF - lag)
            term = q * k_s * jnp.exp(jnp.where(paired, cum - cum_s, 0.0))
            s = jnp.sum(jnp.where(paired, term, 0.0), axis=-1, keepdims=True)
            diagonal = (rt - ct == lag) if d == 0 else (ct - rt == lag)
            acc = acc + jnp.where(diagonal, s, 0.0)
        blk = C // 2
        while blk >= GLA_LEAF:
            w = jnp.exp(-jnp.abs(cum - expand(boundary_rows(cum, d, blk))))
            acc = acc + block_level_scores(cum, q, k, d, blk, w)
            blk //= 2
        end = cum[C - 1:C, :] if d == 0 else cum[0:1, :]
        store_scores(n, d, acc, q * jnp.exp(cum), k * jnp.exp(end - cum), end)

    def scores(n, d):
        cum = cum_ref[d, rows(n), :]
        mid = GLA_LEAF // 2 - 1 if d == 0 else GLA_LEAF // 2
        at_mid = per_leaf([cum[mid + j * GLA_LEAF:mid + j * GLA_LEAF + 1, :] for j in range(n_leaves)])
        e = cum - expand(at_mid)
        qe = q_ref[rows(n), :] * jnp.exp(e)
        ke = k_ref[rows(n), :] * jnp.exp(-e)
        acc = jnp.where(keep[d][0], _dot_nt(qe.astype(BF16), ke.astype(BF16)), 0.0)
        blk = C // 2
        while blk >= GLA_LEAF:
            through = expand(jnp.exp(-jnp.abs(at_mid - boundary_rows(cum, d, blk))))
            acc = acc + block_level_scores(cum, qe, ke, d, blk, through)
            blk //= 2
        end = cum[C - 1:C, :] if d == 0 else cum[0:1, :]
        store_scores(n, d, acc, qe * expand(jnp.exp(at_mid)), ke * expand(jnp.exp(end - at_mid)), end)

    def scores_chunk(n):
        for d in range(2):
            (scores_exact if exact else scores)(n, d)

    loop(scores_chunk)

    for d in range(2):
        if has_s0:
            st_ref[d] = s0_ref[0, d, 0]
        else:
            st_ref[d] = jnp.zeros((DK_HEAD, DV_HEAD), F32)

    def scan(n, d):
        v = v_ref[rows(n), :].astype(BF16)
        st = st_ref[d]
        op_ref[d, rows(n), :] = _dot(aq_ref[d, rows(n), :],
                                     jnp.concatenate([v, st.astype(BF16)], axis=0))
        dec = dec_ref[d, rows(n), :]
        st_ref[d] = (st * jnp.concatenate([dec] * (DV_HEAD // DK_HEAD), axis=1)
                     + _dot_tn(ko_ref[d, rows(n), :], v))

    def scan_chunk(m):
        scan(m, 0)
        scan(n_chunks - 1 - m, 1)

    loop(scan_chunk)
    if emit_state:
        sout_ref[0, 0, 0] = st_ref[0]
        sout_ref[0, 1, 0] = st_ref[1]

    g = g_ref[...]

    def finish_chunk(n):
        o = op_ref[0, rows(n), :] + op_ref[1, rows(n), :]
        ms = jnp.mean(o * o, axis=-1, keepdims=True)
        o_ref[rows(n), :] = o * lax.rsqrt(ms + LN_EPS) * g

    loop(finish_chunk)


def _gla(proj, la, g, s0, *, n_seq, seq_len, row0, emit_state, exact):
    has_s0 = s0 is not None
    blk0 = row0 // seq_len
    kern = functools.partial(_gla_kernel, seq_len=seq_len, has_s0=has_s0, emit_state=emit_state,
                             exact=exact)
    in_specs = [
        pl.BlockSpec((seq_len, DK_HEAD), lambda b, h: (blk0 + b, COL_Q // DK_HEAD + h)),
        pl.BlockSpec((seq_len, DK_HEAD), lambda b, h: (blk0 + b, COL_K // DK_HEAD + h)),
        pl.BlockSpec((seq_len, DV_HEAD), lambda b, h: (blk0 + b, COL_V // DV_HEAD + h)),
        pl.BlockSpec((seq_len, DK_HEAD), lambda b, h: (blk0 + b, h)),
        pl.BlockSpec((seq_len, DK_HEAD), lambda b, h: (blk0 + b, GLA_HEADS + h)),
        pl.BlockSpec((1, DV_HEAD), lambda b, h: (0, 0)),
    ]
    args = [proj, proj, proj, la, la, g]
    if has_s0:
        in_specs.append(pl.BlockSpec((1, 2, 1, DK_HEAD, DV_HEAD), lambda b, h: (b, 0, h, 0, 0)))
        args.append(s0)
    out_specs = [pl.BlockSpec((seq_len, DV_HEAD), lambda b, h: (b, h))]
    out_shape = [jax.ShapeDtypeStruct((n_seq * seq_len, GLA_DV), F32)]
    if emit_state:
        out_specs.append(pl.BlockSpec((1, 2, 1, DK_HEAD, DV_HEAD), lambda b, h: (b, 0, h, 0, 0)))
        out_shape.append(jax.ShapeDtypeStruct((n_seq, 2, GLA_HEADS, DK_HEAD, DV_HEAD), F32))

    res = pl.pallas_call(
        kern,
        grid=(n_seq, GLA_HEADS),
        in_specs=in_specs,
        out_specs=out_specs,
        out_shape=out_shape,
        scratch_shapes=[pltpu.VMEM((2, seq_len, DK_HEAD), F32),
                        pltpu.VMEM((2, seq_len, GLA_CHUNK + DK_HEAD), BF16),
                        pltpu.VMEM((2, seq_len, DK_HEAD), BF16),
                        pltpu.VMEM((2, seq_len, DK_HEAD), F32),
                        pltpu.VMEM((2, seq_len, DV_HEAD), F32),
                        pltpu.VMEM((2, DK_HEAD, DV_HEAD), F32)],
        compiler_params=pltpu.CompilerParams(
            dimension_semantics=("arbitrary", "arbitrary"), vmem_limit_bytes=VMEM_LIMIT),
        name="gla%s_seq%d" % ("_exact" if exact else "", seq_len),
    )(*args)
    return res


def _fnet_kernel(f_ref, cl_ref, sl_ref, cg_ref, sg_ref, o_ref, uc_ref, us_ref, *, seq_len):
    cg = cg_ref[...]
    sg = sg_ref[...]
    for grp in range(FNET_GROUPS):
        lo = grp * FNET_GROUP_DIM
        u = f_ref[:, lo:lo + FNET_GROUP_DIM].astype(BF16)
        uc_ref[:, lo:lo + FNET_GROUP_DIM] = _dot(u, cg).astype(BF16)
        us_ref[:, lo:lo + FNET_GROUP_DIM] = _dot(u, sg).astype(BF16)
    mixed = _dot(cl_ref[...], uc_ref[...]) - _dot(sl_ref[...], us_ref[...])
    o_ref[...] = mixed * (1.0 / math.sqrt(seq_len * FNET_GROUP_DIM))


def _dft_mats(n):
    j = np.arange(n, dtype=np.int64)
    ang = (2.0 * np.pi / n) * ((j[:, None] * j[None, :]) % n).astype(np.float64)
    return (jnp.asarray(np.cos(ang), dtype=F32).astype(BF16),
            jnp.asarray(np.sin(ang), dtype=F32).astype(BF16))


def _fnet(proj, *, n_seq, seq_len, row0):
    blk0 = row0 // seq_len
    cl, sl = _dft_mats(seq_len)
    cg, sg = _dft_mats(FNET_GROUP_DIM)
    kern = functools.partial(_fnet_kernel, seq_len=seq_len)
    return pl.pallas_call(
        kern,
        grid=(n_seq,),
        in_specs=[pl.BlockSpec((seq_len, FNET_DIM), lambda b: (blk0 + b, COL_F // FNET_DIM)),
                  pl.BlockSpec((seq_len, seq_len), lambda b: (0, 0)),
                  pl.BlockSpec((seq_len, seq_len), lambda b: (0, 0)),
                  pl.BlockSpec((FNET_GROUP_DIM, FNET_GROUP_DIM), lambda b: (0, 0)),
                  pl.BlockSpec((FNET_GROUP_DIM, FNET_GROUP_DIM), lambda b: (0, 0))],
        out_specs=pl.BlockSpec((seq_len, FNET_DIM), lambda b: (b, 0)),
        out_shape=jax.ShapeDtypeStruct((n_seq * seq_len, FNET_DIM), F32),
        scratch_shapes=[pltpu.VMEM((seq_len, FNET_DIM), BF16),
                        pltpu.VMEM((seq_len, FNET_DIM), BF16)],
        compiler_params=pltpu.CompilerParams(
            dimension_semantics=("arbitrary",), vmem_limit_bytes=VMEM_LIMIT),
        name="fnet_seq%d" % seq_len,
    )(proj, cl, sl, cg, sg)


def _merge_kernel(xc_ref, xl_ref, pos_ref, mod_ref, oc_ref, ol_ref, mc_ref, ml_ref,
                  g_ref, ga_ref, gb_ref, wbg_ref, wbf_ref, wo_ref, l1g_ref, l1b_ref, wrh_ref, wrl_ref,
                  br_ref, x1_ref, h2_ref, ridx_ref, rw_ref, *, n_ctx_tiles, alpha):
    i = pl.program_id(0)

    tm = x1_ref.shape[0]
    groups = [slice(r0, r0 + TM_MERGE_GROUP) for r0 in range(0, tm, TM_MERGE_GROUP)]
    shape = (TM_MERGE_GROUP, LANES)
    lane_i = lax.broadcasted_iota(jnp.int32, shape, 1)
    lane = lane_i.astype(F32)

    def compute(x_of, o_ref, mx_ref):
        branch = []
        for rows in groups:
            a = (o_ref[rows, :] * g_ref[rows, :]).astype(BF16)
            branch.append((_dot(a, wbg_ref[...]), _dot(mx_ref[rows, :].astype(BF16), wbf_ref[...])))
        mix = []
        for rows, (gla_out, fnet_out) in zip(groups, branch):
            merged = ga_ref[rows, :] * gla_out + gb_ref[rows, :] * fnet_out
            mix.append(_dot(merged.astype(BF16), wo_ref[...]))
        logits = []
        for rows, mix_g in zip(groups, mix):
            y = alpha * x_of(rows) + mod_ref[0, 2:3, :] * mix_g
            x1 = _layer_norm(y) * l1g_ref[...] + l1b_ref[...]
            x1_ref[rows, :] = x1
            h2 = _layer_norm(x1) * (1.0 + mod_ref[0, 4:5, :]) + mod_ref[0, 3:4, :]
            _store_row_tiles(h2_ref, h2, rows.start)
            h_hi, h_lo = _split_bf16(h2, 2)
            logits.append(_dot(h_hi, wrh_ref[...]) + _dot(h_lo, wrh_ref[...])
                          + _dot(h_hi, wrl_ref[...]) + br_ref[...])
        idx_out = [jnp.zeros(shape, F32) for _ in groups]
        val_out = [jnp.zeros(shape, F32) for _ in groups]
        top0 = [None] * len(groups)
        denom = [None] * len(groups)
        for kk in range(TOP_K):
            for gi in range(len(groups)):
                m = jnp.max(logits[gi], axis=-1, keepdims=True)
                sel = jnp.min(jnp.where(logits[gi] == m, lane, float(LANES)), axis=-1, keepdims=True)
                if kk == 0:
                    top0[gi] = m
                    p = jnp.ones_like(m)
                    denom[gi] = p
                else:
                    p = jnp.exp(m - top0[gi])
                    denom[gi] = denom[gi] + p
                idx_out[gi] = jnp.where(lane_i == kk, sel, idx_out[gi])
                val_out[gi] = jnp.where(lane_i == kk, p, val_out[gi])
                logits[gi] = jnp.where(lane == sel, -jnp.inf, logits[gi])
        for gi, rows in enumerate(groups):
            ridx_ref[rows, :] = idx_out[gi].astype(jnp.int32)
            rw_ref[rows, :] = val_out[gi] / denom[gi]

    @pl.when(i < n_ctx_tiles)
    def _():
        compute(lambda rows: xc_ref[rows, :], oc_ref, mc_ref)

    @pl.when(i >= n_ctx_tiles)
    def _():
        compute(lambda rows: xl_ref[rows, :] + pos_ref[rows, :], ol_ref, ml_ref)


def _merge(x_ctx, x_lat, pos, mod, o_ctx, o_lat, mixed_ctx, mixed_lat, proj,
           wbg, wbf, wo, l1g, l1b, wr_hi, wr_lo, br, lat_len, alpha):
    t_ctx, t_lat = x_ctx.shape[0], x_lat.shape[0]
    t_all = t_ctx + t_lat
    tm = TM_MERGE
    n_ctx_tiles = t_ctx // tm
    kern = functools.partial(_merge_kernel, n_ctx_tiles=n_ctx_tiles, alpha=alpha)
    specs = _token_specs(tm, n_ctx_tiles, lat_len // tm, mod.shape[0] - 1)
    ctx_map, lat_map = _group_maps(n_ctx_tiles)

    def const(shape):
        return pl.BlockSpec(shape, lambda i: (0,) * len(shape))

    in_specs = specs + [
        pl.BlockSpec((tm, GLA_DV), ctx_map),
        pl.BlockSpec((tm, GLA_DV), lat_map),
        pl.BlockSpec((tm, FNET_DIM), ctx_map),
        pl.BlockSpec((tm, FNET_DIM), lat_map),
        pl.BlockSpec((tm, GLA_DV), lambda i: (i, COL_G // GLA_DV)),
        pl.BlockSpec((tm, D_MODEL), lambda i: (i, COL_GATE_A // D_MODEL)),
        pl.BlockSpec((tm, D_MODEL), lambda i: (i, COL_GATE_B // D_MODEL)),
        const((GLA_DV, D_MODEL)), const((FNET_DIM, D_MODEL)), const((D_MODEL, D_MODEL)),
        const((1, D_MODEL)), const((1, D_MODEL)),
        const((D_MODEL, LANES)), const((D_MODEL, LANES)), const((1, LANES)),
    ]
    out_specs = [pl.BlockSpec((tm, D_MODEL), lambda i: (i, 0)),
                 pl.BlockSpec((tm * ROW_TILES, LANES), lambda i: (i, 0)),
                 pl.BlockSpec((tm, LANES), lambda i: (i, 0)),
                 pl.BlockSpec((tm, LANES), lambda i: (i, 0))]
    out_shape = [jax.ShapeDtypeStruct((t_all, D_MODEL), F32),
                 jax.ShapeDtypeStruct((t_all * ROW_TILES, LANES), ROW_DTYPE),
                 jax.ShapeDtypeStruct((t_all, LANES), jnp.int32),
                 jax.ShapeDtypeStruct((t_all, LANES), F32)]
    return pl.pallas_call(
        kern,
        grid=(t_all // tm,),
        in_specs=in_specs,
        out_specs=out_specs,
        out_shape=out_shape,
        compiler_params=pltpu.CompilerParams(
            dimension_semantics=("arbitrary",), vmem_limit_bytes=VMEM_LIMIT),
        name="merge_ln1_router",
    )(x_ctx, x_lat, pos, mod, o_ctx, o_lat, mixed_ctx, mixed_lat, proj, proj, proj,
      wbg, wbf, wo, l1g, l1b, wr_hi, wr_lo, br)


def _sc_mesh():
    return plsc.VectorSubcoreMesh(core_axis_name="c", subcore_axis_name="s")


def _sc_worker_id():
    return lax.axis_index("s") * SC_CORES + lax.axis_index("c")


def _sc_scatter_rows(src, idx, n_out):
    n_src = src.shape[0]
    w = SC_WINDOW
    n_chunks = n_src // (SC_WORKERS * w)
    copies = idx.shape[0]
    assert n_chunks % 2 == 0 and idx.shape == (copies, SC_WORKERS, n_chunks, w)

    @functools.partial(
        pl.kernel, mesh=_sc_mesh(),
        out_type=jax.ShapeDtypeStruct((n_out, ROW_TILES, LANES), ROW_DTYPE),
        scratch_types=[pltpu.VMEM((copies * n_chunks, w), jnp.int32),
                       pltpu.VMEM((2, w, ROW_TILES, LANES), ROW_DTYPE),
                       pltpu.SemaphoreType.DMA((2,)),
                       pltpu.SemaphoreType.DMA((2,))],
        name="moe_dispatch_scatter")
    def k(src_hbm, idx_hbm, out_hbm, idx_v, rows_v, rsem, wsem):
        wid = _sc_worker_id()
        base = wid * (n_chunks * w)
        for kk in range(copies):
            pltpu.sync_copy(idx_hbm.at[kk, wid], idx_v.at[pl.ds(kk * n_chunks, n_chunks)])

        def read(j, slot):
            return pltpu.make_async_copy(src_hbm.at[pl.ds(base + j * w, w)], rows_v.at[slot],
                                         rsem.at[slot])

        def scatter(j, kk, slot):
            return pltpu.make_async_copy(rows_v.at[slot], out_hbm.at[idx_v.at[kk * n_chunks + j]],
                                         wsem.at[slot])

        read(0, 0).start()

        @pl.loop(0, n_chunks, step=2)
        def _(jj):
            read(jj, 0).wait()

            @pl.when(jj > 0)
            def _():
                for kk in range(copies):
                    scatter(jj - 1, kk, 1).wait()

            read(jj + 1, 1).start()
            for kk in range(copies):
                scatter(jj, kk, 0).start()
            read(jj + 1, 1).wait()
            for kk in range(copies):
                scatter(jj, kk, 0).wait()

            @pl.when(jj + 2 < n_chunks)
            def _():
                read(jj + 2, 0).start()

            for kk in range(copies):
                scatter(jj + 1, kk, 1).start()

        for kk in range(copies):
            scatter(n_chunks - 1, kk, 1).wait()

    return k(src, idx)


def _sc_gather_rows(table, idx):
    _, n_chunks, w = idx.shape
    assert n_chunks % 2 == 0 and idx.shape[0] == SC_WORKERS and w == SC_WINDOW
    n_out = SC_WORKERS * n_chunks * w

    @functools.partial(
        pl.kernel, mesh=_sc_mesh(),
        out_type=jax.ShapeDtypeStruct((n_out, ROW_TILES, LANES), ROW_DTYPE),
        scratch_types=[pltpu.VMEM((n_chunks, w), jnp.int32),
                       pltpu.VMEM((2, w, ROW_TILES, LANES), ROW_DTYPE),
                       pltpu.SemaphoreType.DMA((2,)),
                       pltpu.SemaphoreType.DMA((2,))],
        name="moe_combine_gather")
    def k(table_hbm, idx_hbm, out_hbm, idx_v, rows_v, gsem, wsem):
        wid = _sc_worker_id()
        base = wid * (n_chunks * w)
        pltpu.sync_copy(idx_hbm.at[wid], idx_v)

        def gather(j, slot):
            return pltpu.make_async_copy(table_hbm.at[idx_v.at[j]], rows_v.at[slot], gsem.at[slot])

        def write(j, slot):
            return pltpu.make_async_copy(rows_v.at[slot], out_hbm.at[pl.ds(base + j * w, w)],
                                         wsem.at[slot])

        gather(0, 0).start()

        @pl.loop(0, n_chunks, step=2)
        def _(jj):
            gather(jj, 0).wait()

            @pl.when(jj > 0)
            def _():
                write(jj - 1, 1).wait()

            gather(jj + 1, 1).start()
            write(jj, 0).start()
            gather(jj + 1, 1).wait()
            write(jj, 0).wait()

            @pl.when(jj + 2 < n_chunks)
            def _():
                gather(jj + 2, 0).start()

            write(jj + 1, 1).start()

        write(n_chunks - 1, 1).wait()

    return k(table, idx)


def _moe_kernel(be_ref, nu_ref, nv_ref, slot_ref, nxt_ref, x_ref, wgu_hbm, bgu_ref, wd_hbm, bd_ref,
                o_ref, wgu_st, wd_st, wgu_bf, wd_bf, xb_ref, sem):
    b = pl.program_id(0)
    e = be_ref[b]
    prev = be_ref[jnp.maximum(b - 1, 0)]
    active = b < nu_ref[0]
    changed = (b == 0) | (e != prev)

    def weight_copies(expert, s):
        return (pltpu.make_async_copy(wgu_hbm.at[expert], wgu_st.at[s], sem.at[0, s]),
                pltpu.make_async_copy(wd_hbm.at[expert], wd_st.at[s], sem.at[1, s]))

    @pl.when(active & changed)
    def _():
        s = slot_ref[b]

        @pl.when(b == 0)
        def _():
            for cp in weight_copies(e, s):
                cp.start()

        for cp in weight_copies(e, s):
            cp.wait()
        wgu_bf[...] = wgu_st[s].astype(BF16)
        wd_bf[...] = wd_st[s].astype(BF16)
        nxt = nxt_ref[b]

        @pl.when(nxt >= 0)
        def _():
            for cp in weight_copies(nxt, 1 - s):
                cp.start()

    n_valid = nv_ref[b]

    def expert_mlp(n_rows):
        valid = lax.broadcasted_iota(jnp.int32, (n_rows, LANES), 0) < n_valid
        for j in range(ROW_TILES):
            for half, xj in enumerate(_load_row_tile(x_ref, j, n_rows)):
                c0 = half * HALF_MODEL + j * LANES
                xb_ref[0:n_rows, c0:c0 + LANES] = jnp.where(valid, xj, 0.0).astype(BF16)
        gu = _dot(xb_ref[0:n_rows, :], wgu_bf[...]) + bgu_ref[0]
        gate = jnp.minimum(gu[:, :D_EXPERT], SWIGLU_LIMIT)
        up = jnp.clip(gu[:, D_EXPERT:], -SWIGLU_LIMIT, SWIGLU_LIMIT)
        glu = gate * _sigmoid(gate * SWIGLU_ALPHA)
        act = ((up + 1.0) * glu).astype(BF16)
        _store_row_tiles(o_ref, _dot(act, wd_bf[...]) + bd_ref[0])

    for n_rows in range(MOE_STEP, MOE_ROWS + 1, MOE_STEP):
        @pl.when(active & (n_valid > n_rows - MOE_STEP) & (n_valid <= n_rows))
        def _(n_rows=n_rows):
            expert_mlp(n_rows)


def _moe(tables, xs, w_gate_up, b_gate_up, w_down, b_down):
    p_rows = xs.shape[0] // ROW_TILES
    n_blocks = p_rows // MOE_ROWS

    def blk(b, be, nu, *_):
        return jnp.minimum(b, nu[0] - 1)

    def expert(b, be, nu, *_):
        return (be[blk(b, be, nu)], 0, 0)

    def rows(b, be, nu, *_):
        return (blk(b, be, nu), 0)

    grid_spec = pltpu.PrefetchScalarGridSpec(
        num_scalar_prefetch=len(tables),
        grid=(n_blocks,),
        in_specs=[
            pl.BlockSpec((MOE_ROWS * ROW_TILES, LANES), rows),
            pl.BlockSpec(memory_space=pl.ANY),
            pl.BlockSpec((1, 1, 2 * D_EXPERT), expert),
            pl.BlockSpec(memory_space=pl.ANY),
            pl.BlockSpec((1, 1, D_MODEL), expert),
        ],
        out_specs=pl.BlockSpec((MOE_ROWS * ROW_TILES, LANES), rows),
        scratch_shapes=[pltpu.VMEM((2, D_MODEL, 2 * D_EXPERT), F32),
                        pltpu.VMEM((2, D_EXPERT, D_MODEL), F32),
                        pltpu.VMEM((D_MODEL, 2 * D_EXPERT), BF16),
                        pltpu.VMEM((D_EXPERT, D_MODEL), BF16),
                        pltpu.VMEM((MOE_ROWS, D_MODEL), BF16),
                        pltpu.SemaphoreType.DMA((2, 2))],
    )
    return pl.pallas_call(
        _moe_kernel,
        grid_spec=grid_spec,
        out_shape=jax.ShapeDtypeStruct((p_rows * ROW_TILES, LANES), ROW_DTYPE),
        compiler_params=pltpu.CompilerParams(
            dimension_semantics=("arbitrary",), vmem_limit_bytes=VMEM_LIMIT),
        name="moe_grouped_mlp",
    )(*tables, xs, w_gate_up, b_gate_up.reshape(N_EXPERTS, 1, 2 * D_EXPERT), w_down,
      b_down.reshape(N_EXPERTS, 1, D_MODEL))


def _combine_kernel(x1_ref, y0_ref, y1_ref, y2_ref, y3_ref, rw_ref, mod_ref, g_ref, b_ref, o_ref,
                    *, alpha):
    rw = rw_ref[...]
    y_refs = (y0_ref, y1_ref, y2_ref, y3_ref)
    pieces = [None] * (2 * ROW_TILES)
    for j in range(ROW_TILES):
        for kk in range(TOP_K):
            for half, yj in enumerate(_load_row_tile(y_refs[kk], j, rw.shape[0])):
                term = rw[:, kk:kk + 1] * yj
                slot = half * ROW_TILES + j
                pieces[slot] = term if kk == 0 else pieces[slot] + term
    ff = jnp.concatenate(pieces, axis=-1)
    y = alpha * x1_ref[...] + mod_ref[0, 5:6, :] * ff
    o_ref[...] = _layer_norm(y) * g_ref[...] + b_ref[...]


def _combine(x1, yg, rw, mod, l2g, l2b, *, row0, n_rows, mod_map, alpha):
    tm = TM_MIX
    t0 = row0 // tm
    tiles = n_rows // tm
    kern = functools.partial(_combine_kernel, alpha=alpha)

    def y_spec(kk):
        return pl.BlockSpec((tm * ROW_TILES, LANES), lambda i: (kk * tiles + i, 0))

    return pl.pallas_call(
        kern,
        grid=(n_rows // tm,),
        in_specs=[pl.BlockSpec((tm, D_MODEL), lambda i: (t0 + i, 0))]
        + [y_spec(kk) for kk in range(TOP_K)]
        + [pl.BlockSpec((tm, LANES), lambda i: (t0 + i, 0)),
           pl.BlockSpec((1, N_MOD, D_MODEL), mod_map),
           pl.BlockSpec((1, D_MODEL), lambda i: (0, 0)),
           pl.BlockSpec((1, D_MODEL), lambda i: (0, 0))],
        out_specs=pl.BlockSpec((tm, D_MODEL), lambda i: (i, 0)),
        out_shape=jax.ShapeDtypeStruct((n_rows, D_MODEL), F32),
        compiler_params=pltpu.CompilerParams(
            dimension_semantics=("arbitrary",), vmem_limit_bytes=VMEM_LIMIT),
        name="combine_ln2",
    )(x1, yg, yg, yg, yg, rw, mod, l2g, l2b)


def _route_kernel(ridx_ref, dest_ref, cnt_ref, run_ref, bst_ref):
    phase = pl.program_id(0)
    i = pl.program_id(1)
    tm = ridx_ref.shape[0]
    ridx = ridx_ref[...]
    lane = lax.broadcasted_iota(jnp.int32, (tm, LANES), 1)
    hits = [ridx[:, kk:kk + 1] == lane for kk in range(TOP_K)]
    chosen = jnp.where(hits[0], 1.0, 0.0)
    for kk in range(1, TOP_K):
        chosen = chosen + jnp.where(hits[kk], 1.0, 0.0)
    colsum = jnp.sum(chosen, axis=0, keepdims=True)

    @pl.when((phase == 0) & (i == 0))
    def _():
        run_ref[...] = jnp.zeros_like(run_ref)

    @pl.when(phase == 0)
    def _():
        run_ref[...] = run_ref[...] + colsum

    @pl.when((phase == 1) & (i == 0))
    def _():
        counts = run_ref[...]
        cnt_ref[...] = counts
        blocks = jnp.floor((counts + (MOE_ROWS - 1.0)) * (1.0 / MOE_ROWS))
        r = lax.broadcasted_iota(jnp.int32, (LANES, LANES), 0)
        c = lax.broadcasted_iota(jnp.int32, (LANES, LANES), 1)
        before = jnp.dot(blocks, (r < c).astype(F32), precision=HIGHEST, preferred_element_type=F32)
        bst_ref[...] = before * float(MOE_ROWS)
        run_ref[...] = jnp.zeros_like(run_ref)

    @pl.when(phase == 1)
    def _():
        rt = lax.broadcasted_iota(jnp.int32, (tm, tm), 0)
        ct = lax.broadcasted_iota(jnp.int32, (tm, tm), 1)
        earlier = _dot((ct < rt).astype(BF16), chosen.astype(BF16))
        row_of = bst_ref[0:1, :] + run_ref[0:1, :] + earlier
        out = jnp.take_along_axis(row_of, ridx, axis=1)
        dest_ref[...] = out.T[0:SUBLANES, :].astype(jnp.int32)
        run_ref[...] = run_ref[...] + colsum


def _route(ridx):
    t_all = ridx.shape[0]
    tm = TM_ROUTE
    return pl.pallas_call(
        _route_kernel,
        grid=(2, t_all // tm),
        in_specs=[pl.BlockSpec((tm, LANES), lambda p, i: (i, 0))],
        out_specs=[pl.BlockSpec((SUBLANES, tm), lambda p, i: (0, i * p)),
                   pl.BlockSpec((SUBLANES, LANES), lambda p, i: (0, 0))],
        out_shape=[jax.ShapeDtypeStruct((SUBLANES, t_all), jnp.int32),
                   jax.ShapeDtypeStruct((SUBLANES, LANES), F32)],
        scratch_shapes=[pltpu.VMEM((SUBLANES, LANES), F32),
                        pltpu.VMEM((SUBLANES, LANES), F32)],
        compiler_params=pltpu.CompilerParams(
            dimension_semantics=("arbitrary", "arbitrary"), vmem_limit_bytes=VMEM_LIMIT),
        name="moe_route",
    )(ridx)


def _routing_tables(counts, n_blocks):
    experts = jnp.arange(N_EXPERTS, dtype=jnp.int32)
    blocks_per = (counts + MOE_ROWS - 1) // MOE_ROWS
    bends = jnp.cumsum(blocks_per)
    bstarts = bends - blocks_per
    blocks = jnp.arange(n_blocks, dtype=jnp.int32)
    block_expert = jnp.minimum(
        jnp.sum((bends[None, :] <= blocks[:, None]).astype(jnp.int32), axis=1), N_EXPERTS - 1)
    n_used = bends[-1:].astype(jnp.int32)
    owner = block_expert[:, None] == experts[None, :]

    def per_block(table):
        return jnp.sum(jnp.where(owner, table[None, :], 0), axis=1)

    n_valid = jnp.clip(per_block(counts) - (blocks - per_block(bstarts)) * MOE_ROWS,
                       0, MOE_ROWS).astype(jnp.int32)
    present = blocks_per > 0
    ordinal = jnp.cumsum(present.astype(jnp.int32)) - 1
    later = lax.cummin(jnp.where(present, experts, N_EXPERTS), reverse=True)
    succ = jnp.concatenate([later[1:], jnp.full((1,), N_EXPERTS, jnp.int32)])
    succ = jnp.where(succ >= N_EXPERTS, -1, succ)
    stage_slot = (per_block(ordinal) % 2).astype(jnp.int32)
    next_expert = per_block(succ).astype(jnp.int32)
    return (block_expert.astype(jnp.int32), n_used, n_valid, stage_slot, next_expert)


def _pos_embed_2d(n_tokens):
    rows = n_tokens // GRID_W
    r = np.repeat(np.arange(rows), GRID_W).astype(np.float32)
    col = np.tile(np.arange(GRID_W), rows).astype(np.float32)
    quarter = D_MODEL // 4
    omega = (np.float32(1.0)
             / np.power(np.float32(10000.0), np.arange(quarter, dtype=np.float32) / np.float32(quarter)))
    er = (r[:, None] * omega).astype(np.float64)
    ec = (col[:, None] * omega).astype(np.float64)
    table = np.concatenate([np.sin(er), np.cos(er), np.sin(ec), np.cos(ec)], axis=-1)
    return jnp.asarray(table, dtype=F32)


def _split_w_in(w):
    wt = w.T
    o_r = 2 * GLA_DK + 2 * GLA_DV
    o_f = o_r + DECAY_RANK
    o_gate = o_f + FNET_DIM
    row_scale = jnp.where(jnp.arange(o_r) < GLA_DK, DK_HEAD ** -0.5, 1.0).astype(w.dtype)
    w_main = (wt[:o_r] * row_scale[:, None]).astype(BF16)
    pad = jnp.zeros((LANES - DECAY_RANK, w.shape[0]), BF16)
    w_tail = jnp.concatenate([wt[o_gate:].astype(BF16), wt[o_f:o_gate].astype(BF16),
                              wt[o_r:o_f].astype(BF16), pad], axis=0)
    return w_main, w_tail


def kernel(x_prompt, x_sample, state_gla, c, c_ctx, w_ada, b_ada, w_in, w_dec_fwd, b_dec_fwd,
           w_dec_bwd, b_dec_bwd, gla_norm_g, w_br_gla, w_br_fnet, w_out, ln1_g, ln1_b, w_router,
           b_router, w_gate_up, b_gate_up, w_down, b_down, ln2_g, ln2_b):
    n_req, ctx_len, _ = x_prompt.shape
    n_lat, lat_len, _ = x_sample.shape
    depth = w_in.shape[0]
    alpha = (2.0 * depth) ** 0.25
    t_ctx = n_req * ctx_len
    t_lat = n_lat * lat_len
    t_all = t_ctx + t_lat
    tile_rows = max(TM_PROJ, TM_MERGE, TM_MIX)
    assert t_ctx % tile_rows == 0 and lat_len % tile_rows == 0
    assert ctx_len % GLA_CHUNK == 0 and lat_len % GLA_CHUNK == 0 and t_all % TM_ROUTE == 0
    assert t_ctx % (SC_WORKERS * SC_WINDOW * 2) == 0 and t_lat % (SC_WORKERS * SC_WINDOW * 2) == 0
    assert MOE_ROWS & (MOE_ROWS - 1) == 0 and (t_all * TOP_K) % MOE_ROWS == 0

    x_ctx = x_prompt.reshape(t_ctx, D_MODEL)
    x_lat = x_sample.reshape(t_lat, D_MODEL)
    pos = _pos_embed_2d(lat_len)
    zero_pos = jnp.zeros_like(pos)

    cond_rows = -(-(n_lat + 1) // SUBLANES) * SUBLANES
    cond = jnp.zeros((cond_rows, D_MODEL), F32).at[:n_lat].set(c).at[cond_rows - 1].set(c_ctx)

    n_moe_blocks = (t_all * TOP_K) // MOE_ROWS + N_EXPERTS
    tok_chunks = t_all // (SC_WORKERS * SC_WINDOW)
    states = []
    for l in range(depth):
        mod = _ada(cond, w_ada[l], b_ada[l]).reshape(cond_rows, N_MOD, D_MODEL)
        layer_pos = pos if l == 0 else zero_pos
        w_dec = jnp.zeros((2, LANES, GLA_DK), F32)
        w_dec = w_dec.at[0, :DECAY_RANK].set(w_dec_fwd[l]).at[1, :DECAY_RANK].set(w_dec_bwd[l])
        b_dec = jnp.stack([b_dec_fwd[l], b_dec_bwd[l]]).reshape(2, 1, GLA_DK)
        proj, la, la_min = _inproj(x_ctx, x_lat, layer_pos, mod, *_split_w_in(w_in[l]), w_dec, b_dec,
                                   lat_len)

        norm_g = gla_norm_g[l].reshape(1, DV_HEAD)
        s0_lat = state_gla[:, l]

        def gla_both(exact):
            o_c, s_c = _gla(proj, la, norm_g, None, n_seq=n_req, seq_len=ctx_len, row0=0,
                            emit_state=True, exact=exact)
            (o_l,) = _gla(proj, la, norm_g, s0_lat, n_seq=n_lat, seq_len=lat_len, row0=t_ctx,
                          emit_state=False, exact=exact)
            return o_c, s_c, o_l

        fast_ok = jnp.min(la_min) >= -GLA_FAST_MAX_STEP_DECAY
        o_ctx, s_new, o_lat = lax.cond(fast_ok, lambda: gla_both(False), lambda: gla_both(True))
        states.append(s_new)

        mixed_ctx = _fnet(proj, n_seq=n_req, seq_len=ctx_len, row0=0)
        mixed_lat = _fnet(proj, n_seq=n_lat, seq_len=lat_len, row0=t_ctx)

        wr = jnp.zeros((D_MODEL, LANES), F32).at[:, :N_EXPERTS].set(w_router[l])
        br = jnp.full((1, LANES), -1e30, F32).at[0, :N_EXPERTS].set(b_router[l])
        wr_hi = wr.astype(BF16)
        wr_lo = (wr - wr_hi.astype(F32)).astype(BF16)
        x1, h2, ridx, rw = _merge(
            x_ctx, x_lat, layer_pos, mod, o_ctx, o_lat, mixed_ctx, mixed_lat, proj,
            w_br_gla[l].astype(BF16), w_br_fnet[l].astype(BF16), (0.5 * w_out[l]).astype(BF16),
            ln1_g[l].reshape(1, D_MODEL), ln1_b[l].reshape(1, D_MODEL), wr_hi, wr_lo, br, lat_len,
            alpha)

        dest, counts = _route(ridx)
        moe_tables = _routing_tables(counts[0, :N_EXPERTS].astype(jnp.int32), n_moe_blocks)
        dest = dest[:TOP_K]
        scatter_idx = dest.reshape(TOP_K, SC_WORKERS, tok_chunks, SC_WINDOW)
        p_rows = n_moe_blocks * MOE_ROWS
        xs = _sc_scatter_rows(h2.reshape(t_all, ROW_TILES, LANES), scatter_idx, p_rows)
        yb = _moe(moe_tables, xs.reshape(p_rows * ROW_TILES, LANES),
                  w_gate_up[l], b_gate_up[l], w_down[l], b_down[l])
        yb = yb.reshape(p_rows, ROW_TILES, LANES)

        def gathered(row0, n_rows):
            idx = dest[:, row0:row0 + n_rows].reshape(SC_WORKERS, -1, SC_WINDOW)
            return _sc_gather_rows(yb, idx).reshape(TOP_K * n_rows * ROW_TILES, LANES)

        l2g = ln2_g[l].reshape(1, D_MODEL)
        l2b = ln2_b[l].reshape(1, D_MODEL)
        tiles_per_seq = lat_len // TM_MIX
        yg_ctx = gathered(0, t_ctx)
        yg_lat = gathered(t_ctx, t_lat)
        x_ctx = _combine(x1, yg_ctx, rw, mod, l2g, l2b, row0=0, n_rows=t_ctx,
                         mod_map=lambda i: (cond_rows - 1, 0, 0), alpha=alpha)
        x_lat = _combine(x1, yg_lat, rw, mod, l2g, l2b, row0=t_ctx, n_rows=t_lat,
                         mod_map=lambda i: (i // tiles_per_seq, 0, 0), alpha=alpha)

    y_prompt = x_ctx.reshape(x_prompt.shape)
    y_sample = x_lat.reshape(x_sample.shape)
    new_state = jnp.stack(states, axis=1).astype(x_prompt.dtype)
    return (y_prompt, y_sample, new_state)
```

```python
import functools
import math

import numpy as np
import jax
import jax.numpy as jnp
from jax import lax
from jax.experimental import pallas as pl
from jax.experimental.pallas import tpu as pltpu
from jax.experimental.pallas import tpu_sc as plsc

F32 = jnp.float32
BF16 = jnp.bfloat16

D_MODEL = 1024
GRID_W = 64
GLA_HEADS = 4
DK_HEAD = 128
DV_HEAD = 256
GLA_DK = GLA_HEADS * DK_HEAD
GLA_DV = GLA_HEADS * DV_HEAD
DECAY_RANK = 16
GATE_NORMALIZER = 16.0
FNET_GROUPS = 4
FNET_GROUP_DIM = 128
FNET_DIM = FNET_GROUPS * FNET_GROUP_DIM
N_EXPERTS = 32
TOP_K = 4
D_EXPERT = 1024
SWIGLU_LIMIT = 7.0
SWIGLU_ALPHA = 1.702
LN_EPS = 1e-6
N_MOD = 6

LANES = 128
SUBLANES = 8
HALF_MODEL = D_MODEL // 2
ROW_TILES = HALF_MODEL // LANES
ROW_DTYPE = jnp.uint32
COL_Q = 0
COL_K = GLA_DK
COL_V = 2 * GLA_DK
COL_G = COL_V + GLA_DV
COL_GATE_A = COL_G + GLA_DV
COL_GATE_B = COL_GATE_A + D_MODEL
COL_F = COL_GATE_B + D_MODEL
COL_R = COL_F + FNET_DIM
PROJ_COLS = COL_R + LANES

GLA_CHUNK = 128
GLA_LEAF = 16
GLA_UNROLL = 8
assert GLA_CHUNK == DK_HEAD
GLA_FAST_MAX_STEP_DECAY = 8.0
TM_PROJ = 512
PROJ_GROUP = 256
TM_MIX = 256
TM_MERGE = 512
TM_MERGE_GROUP = 256
TM_ROUTE = 1024
MOE_ROWS = 512
MOE_STEP = 128
VMEM_LIMIT = 56 * 1024 * 1024

SC_CORES = 2
SC_SUBCORES = 16
SC_WORKERS = SC_CORES * SC_SUBCORES
SC_WINDOW = 64

HIGHEST = lax.Precision.HIGHEST


def _layer_norm(x):
    mu = jnp.mean(x, axis=-1, keepdims=True)
    xc = x - mu
    var = jnp.mean(xc * xc, axis=-1, keepdims=True)
    return xc * lax.rsqrt(var + LN_EPS)


def _sigmoid(x):
    return 0.5 * jnp.tanh(0.5 * x) + 0.5


def _log_sigmoid(z):
    return jnp.minimum(z, 0.0) - jnp.log(1.0 + jnp.exp(-jnp.abs(z)))


def _dot(a, b):
    return jnp.dot(a, b, preferred_element_type=F32)


def _split_bf16(x, terms):
    parts = []
    for _ in range(terms):
        p = x.astype(BF16)
        parts.append(p)
        x = x - p.astype(F32)
    return parts


def _dot_nt(a, b):
    return lax.dot_general(a, b, (((1,), (1,)), ((), ())), preferred_element_type=F32)


def _dot_tn(a, b):
    return lax.dot_general(a, b, (((0,), (0,)), ((), ())), preferred_element_type=F32)


def _row_tile_slice(j, n_rows, first_row=0):
    return pl.ds(first_row * ROW_TILES + j, n_rows, stride=ROW_TILES)


def _store_row_tiles(ref, val, first_row=0):
    for j in range(ROW_TILES):
        lo = val[:, j * LANES:(j + 1) * LANES]
        hi = val[:, HALF_MODEL + j * LANES:HALF_MODEL + (j + 1) * LANES]
        ref[_row_tile_slice(j, val.shape[0], first_row), :] = pltpu.pack_elementwise(
            [lo, hi], packed_dtype=BF16)


def _load_row_tile(ref, j, n_rows, first_row=0):
    words = ref[_row_tile_slice(j, n_rows, first_row), :]
    return tuple(pltpu.unpack_elementwise(words, index=half, packed_dtype=BF16, unpacked_dtype=F32)
                 for half in range(2))


def _ada_kernel(c_ref, w_ref, b_ref, o_ref):
    c = c_ref[...]
    s = c * _sigmoid(c)
    o_ref[...] = _dot(s.astype(BF16), w_ref[...].astype(BF16)) + b_ref[...]


def _ada(cond, w_ada, b_ada):
    rows = cond.shape[0]
    n = w_ada.shape[1]
    tn = 1536
    return pl.pallas_call(
        _ada_kernel,
        grid=(n // tn,),
        in_specs=[pl.BlockSpec((rows, D_MODEL), lambda j: (0, 0)),
                  pl.BlockSpec((D_MODEL, tn), lambda j: (0, j)),
                  pl.BlockSpec((1, tn), lambda j: (0, j))],
        out_specs=pl.BlockSpec((rows, tn), lambda j: (0, j)),
        out_shape=jax.ShapeDtypeStruct((rows, n), F32),
        compiler_params=pltpu.CompilerParams(vmem_limit_bytes=VMEM_LIMIT),
        name="ada_mod",
    )(cond, w_ada, b_ada.reshape(1, n))


def _group_maps(n_ctx_tiles):
    def ctx_map(i, *_):
        return (jnp.minimum(i, n_ctx_tiles - 1), 0)

    def lat_map(i, *_):
        return (jnp.maximum(i - n_ctx_tiles, 0), 0)

    return ctx_map, lat_map


def _token_specs(tm, n_ctx_tiles, tiles_per_latent_seq, ctx_mod_row):
    ctx_map, lat_map = _group_maps(n_ctx_tiles)

    def pos_map(i, *_):
        return (jnp.maximum(i - n_ctx_tiles, 0) % tiles_per_latent_seq, 0)

    def mod_map(i, *_):
        return (jnp.where(i < n_ctx_tiles, ctx_mod_row,
                          jnp.maximum(i - n_ctx_tiles, 0) // tiles_per_latent_seq), 0, 0)

    return [pl.BlockSpec((tm, D_MODEL), ctx_map),
            pl.BlockSpec((tm, D_MODEL), lat_map),
            pl.BlockSpec((tm, D_MODEL), pos_map),
            pl.BlockSpec((1, N_MOD, D_MODEL), mod_map)]


def _inproj_kernel(xc_ref, xl_ref, pos_ref, mod_ref, wm_ref, wt_ref, wdh_ref, wdl_ref, bd_ref,
                   o_ref, la_ref, lamin_ref, *, n_ctx_tiles):
    tail = lambda lo, hi: wt_ref[lo - COL_GATE_A:hi - COL_GATE_A, :]
    i = pl.program_id(0)

    def project(x_of):
        groups = [slice(r0, r0 + PROJ_GROUP) for r0 in range(0, TM_PROJ, PROJ_GROUP)]
        hs = [(_layer_norm(x_of(rows)) * (1.0 + mod_ref[0, 1:2, :]) + mod_ref[0, 0:1, :]).astype(BF16)
              for rows in groups]
        rs = [_split_bf16(_dot_nt(h, tail(COL_R, PROJ_COLS)), 2) for h in hs]
        steepest = None
        for rows, (r_hi, r_lo) in zip(groups, rs):
            for d in range(2):
                z = (_dot(r_hi, wdh_ref[d]) + _dot(r_lo, wdh_ref[d]) + _dot(r_hi, wdl_ref[d])
                     + bd_ref[d])
                la = _log_sigmoid(z) * (1.0 / GATE_NORMALIZER)
                la_ref[rows, d * GLA_DK:(d + 1) * GLA_DK] = la
                low = jnp.min(la, axis=0, keepdims=True)
                for c0 in range(0, GLA_DK, LANES):
                    piece = low[:, c0:c0 + LANES]
                    steepest = piece if steepest is None else jnp.minimum(steepest, piece)
        lamin_ref[...] = jnp.broadcast_to(steepest, (SUBLANES, LANES))
        for rows, h in zip(groups, hs):
            hg = 0.5 * _dot_nt(h, wm_ref[COL_G:COL_GATE_A, :])
            o_ref[rows, COL_G:COL_GATE_A] = hg * (jnp.tanh(hg) + 1.0)
        for rows, h in zip(groups, hs):
            gates = _dot_nt(h, tail(COL_GATE_A, COL_F))
            o_ref[rows, COL_GATE_A:COL_F] = jnp.tanh(0.5 * gates) + 1.0
        for rows, h in zip(groups, hs):
            o_ref[rows, :COL_G] = _dot_nt(h, wm_ref[:COL_G, :])
        for rows, h in zip(groups, hs):
            o_ref[rows, COL_F:COL_R] = _dot_nt(h, tail(COL_F, COL_R))

    @pl.when(i < n_ctx_tiles)
    def _():
        project(lambda rows: xc_ref[rows, :])

    @pl.when(i >= n_ctx_tiles)
    def _():
        project(lambda rows: xl_ref[rows, :] + pos_ref[rows, :])


def _inproj(x_ctx, x_lat, pos, mod, w_main, w_tail, w_dec, b_dec, lat_len):
    t_ctx, t_lat = x_ctx.shape[0], x_lat.shape[0]
    t_all = t_ctx + t_lat
    n_ctx_tiles = t_ctx // TM_PROJ
    kern = functools.partial(_inproj_kernel, n_ctx_tiles=n_ctx_tiles)
    specs = _token_specs(TM_PROJ, n_ctx_tiles, lat_len // TM_PROJ, mod.shape[0] - 1)
    w_dec_hi = w_dec.astype(BF16)
    w_dec_lo = (w_dec - w_dec_hi.astype(F32)).astype(BF16)

    def const(shape):
        return pl.BlockSpec(shape, lambda i: (0,) * len(shape))

    return pl.pallas_call(
        kern,
        grid=(t_all // TM_PROJ,),
        in_specs=specs + [pl.BlockSpec(w_main.shape, lambda i: (0, 0), pipeline_mode=pl.Buffered(1)),
                          pl.BlockSpec(w_tail.shape, lambda i: (0, 0), pipeline_mode=pl.Buffered(1)),
                          const((2, LANES, GLA_DK)), const((2, LANES, GLA_DK)), const((2, 1, GLA_DK))],
        out_specs=[pl.BlockSpec((TM_PROJ, COL_R), lambda i: (i, 0)),
                   pl.BlockSpec((TM_PROJ, 2 * GLA_DK), lambda i: (i, 0)),
                   pl.BlockSpec((SUBLANES, LANES), lambda i: (i, 0))],
        out_shape=[jax.ShapeDtypeStruct((t_all, COL_R), F32),
                   jax.ShapeDtypeStruct((t_all, 2 * GLA_DK), F32),
                   jax.ShapeDtypeStruct((t_all // TM_PROJ * SUBLANES, LANES), F32)],
        compiler_params=pltpu.CompilerParams(
            dimension_semantics=("arbitrary",), vmem_limit_bytes=VMEM_LIMIT),
        name="ln_inproj",
    )(x_ctx, x_lat, pos, mod, w_main, w_tail, w_dec_hi, w_dec_lo, b_dec)


def _gla_kernel(*refs, seq_len, has_s0, emit_state, exact):
    it = iter(refs)
    q_ref, k_ref, v_ref, laf_ref, lab_ref, g_ref = (next(it) for _ in range(6))
    s0_ref = next(it) if has_s0 else None
    o_ref = next(it)
    sout_ref = next(it) if emit_state else None
    cum_ref, aq_ref, ko_ref, dec_ref, op_ref, st_ref = (next(it) for _ in range(6))

    C = GLA_CHUNK
    n_chunks = seq_len // C
    assert n_chunks <= GLA_UNROLL or n_chunks % GLA_UNROLL == 0

    def rows(n):
        if isinstance(n, int):
            return pl.ds(n * C, C)
        return pl.ds(pl.multiple_of(n * C, C), C)

    def loop(body):
        if n_chunks <= GLA_UNROLL:
            for n in range(n_chunks):
                body(n)
        else:
            def step(m, carry):
                for u in range(GLA_UNROLL):
                    body(GLA_UNROLL * m + u)
                return carry
            lax.fori_loop(0, n_chunks // GLA_UNROLL, step, 0)

    rt = lax.broadcasted_iota(jnp.int32, (C, C), 0)
    ct = lax.broadcasted_iota(jnp.int32, (C, C), 1)
    tri = ((rt >= ct).astype(BF16), (ct >= rt).astype(BF16))
    row_id = lax.broadcasted_iota(jnp.int32, (C, DK_HEAD), 0)

    def cumsum_chunk(n):
        for d, la_ref in enumerate((laf_ref, lab_ref)):
            la_hi, la_lo = _split_bf16(la_ref[rows(n), :], 2)
            cum_ref[d, rows(n), :] = _dot(tri[d], la_hi) + _dot(tri[d], la_lo)

    loop(cumsum_chunk)

    query_rows = ({}, {})
    keep = ({}, {})
    for d in range(2):
        blk = C // 2
        while blk >= GLA_LEAF:
            q_parity = 1 if d == 0 else 0
            query_rows[d][blk] = ((row_id // blk) % 2) == q_parity
            qb, kb = rt // blk, ct // blk
            keep[d][blk] = ((qb % 2) == q_parity) & ((qb == kb + 1) if d == 0 else (kb == qb + 1))
            blk //= 2
        order = (rt >= ct) if d == 0 else (ct >= rt)
        keep[d][0] = ((rt // GLA_LEAF) == (ct // GLA_LEAF)) & order

    n_leaves = C // GLA_LEAF

    def per_leaf(rows_of_cum):
        return jnp.concatenate(rows_of_cum, axis=0)

    def expand(per_leaf_rows):
        return jnp.concatenate(
            [jnp.broadcast_to(per_leaf_rows[j:j + 1, :], (GLA_LEAF, DK_HEAD)) for j in range(n_leaves)],
            axis=0)

    def store_scores(n, d, acc, q_in, k_out, end):
        aq_ref[d, rows(n), 0:C] = acc.astype(BF16)
        aq_ref[d, rows(n), C:C + DK_HEAD] = q_in.astype(BF16)
        ko_ref[d, rows(n), :] = k_out.astype(BF16)
        dec_ref[d, rows(n), :] = jnp.broadcast_to(jnp.exp(end), (DK_HEAD, DK_HEAD)).T

    def block_level_scores(cum, q, k, d, blk, at_bnd_scale):
        x = (jnp.where(query_rows[d][blk], q, k) * at_bnd_scale).astype(BF16)
        return jnp.where(keep[d][blk], _dot_nt(x, x), 0.0)

    def boundary_rows(cum, d, blk):
        bnd = blk - 1 if d == 0 else blk
        return per_leaf([cum[bnd + (j * GLA_LEAF) // (2 * blk) * (2 * blk):
                             bnd + (j * GLA_LEAF) // (2 * blk) * (2 * blk) + 1, :]
                         for j in range(n_leaves)])

    def scores_exact(n, d):
        cum = cum_ref[d, rows(n), :]
        q = q_ref[rows(n), :]
        k = k_ref[rows(n), :]
        in_leaf_pos = row_id % GLA_LEAF
        acc = jnp.zeros((C, C), F32)
        for lag in range(GLA_LEAF):
            shift = lag if d == 0 else (C - lag) % C
            k_s = pltpu.roll(k, shift, 0) if lag else k
            cum_s = pltpu.roll(cum, shift, 0) if lag else cum
            paired = (in_leaf_pos >= lag) if d == 0 else (in_leaf_pos < GLA_LEAF - lag)
            term = q * k_s * jnp.exp(jnp.where(paired, cum - cum_s, 0.0))
            s = jnp.sum(jnp.where(paired, term, 0.0), axis=-1, keepdims=True)
            diagonal = (rt - ct == lag) if d == 0 else (ct - rt == lag)
            acc = acc + jnp.where(diagonal, s, 0.0)
        blk = C // 2
        while blk >= GLA_LEAF:
            w = jnp.exp(-jnp.abs(cum - expand(boundary_rows(cum, d, blk))))
            acc = acc + block_level_scores(cum, q, k, d, blk, w)
            blk //= 2
        end = cum[C - 1:C, :] if d == 0 else cum[0:1, :]
        store_scores(n, d, acc, q * jnp.exp(cum), k * jnp.exp(end - cum), end)

    def scores(n, d):
        cum = cum_ref[d, rows(n), :]
        mid = GLA_LEAF // 2 - 1 if d == 0 else GLA_LEAF // 2
        at_mid = per_leaf([cum[mid + j * GLA_LEAF:mid + j * GLA_LEAF + 1, :] for j in range(n_leaves)])
        e = cum - expand(at_mid)
        qe = q_ref[rows(n), :] * jnp.exp(e)
        ke = k_ref[rows(n), :] * jnp.exp(-e)
        acc = jnp.where(keep[d][0], _dot_nt(qe.astype(BF16), ke.astype(BF16)), 0.0)
        blk = C // 2
        while blk >= GLA_LEAF:
            through = expand(jnp.exp(-jnp.abs(at_mid - boundary_rows(cum, d, blk))))
            acc = acc + block_level_scores(cum, qe, ke, d, blk, through)
            blk //= 2
        end = cum[C - 1:C, :] if d == 0 else cum[0:1, :]
        store_scores(n, d, acc, qe * expand(jnp.exp(at_mid)), ke * expand(jnp.exp(end - at_mid)), end)

    def scores_chunk(n):
        for d in range(2):
            (scores_exact if exact else scores)(n, d)

    loop(scores_chunk)

    for d in range(2):
        if has_s0:
            st_ref[d] = s0_ref[0, d, 0]
        else:
            st_ref[d] = jnp.zeros((DK_HEAD, DV_HEAD), F32)

    def scan(n, d):
        v = v_ref[rows(n), :].astype(BF16)
        st = st_ref[d]
        op_ref[d, rows(n), :] = _dot(aq_ref[d, rows(n), :],
                                     jnp.concatenate([v, st.astype(BF16)], axis=0))
        dec = dec_ref[d, rows(n), :]
        st_ref[d] = (st * jnp.concatenate([dec] * (DV_HEAD // DK_HEAD), axis=1)
                     + _dot_tn(ko_ref[d, rows(n), :], v))

    def scan_chunk(m):
        scan(m, 0)
        scan(n_chunks - 1 - m, 1)

    loop(scan_chunk)
    if emit_state:
        sout_ref[0, 0, 0] = st_ref[0]
        sout_ref[0, 1, 0] = st_ref[1]

    g = g_ref[...]

    def finish_chunk(n):
        o = op_ref[0, rows(n), :] + op_ref[1, rows(n), :]
        ms = jnp.mean(o * o, axis=-1, keepdims=True)
        o_ref[rows(n), :] = o * lax.rsqrt(ms + LN_EPS) * g

    loop(finish_chunk)


def _gla(proj, la, g, s0, *, n_seq, seq_len, row0, emit_state, exact):
    has_s0 = s0 is not None
    blk0 = row0 // seq_len
    kern = functools.partial(_gla_kernel, seq_len=seq_len, has_s0=has_s0, emit_state=emit_state,
                             exact=exact)
    in_specs = [
        pl.BlockSpec((seq_len, DK_HEAD), lambda b, h: (blk0 + b, COL_Q // DK_HEAD + h)),
        pl.BlockSpec((seq_len, DK_HEAD), lambda b, h: (blk0 + b, COL_K // DK_HEAD + h)),
        pl.BlockSpec((seq_len, DV_HEAD), lambda b, h: (blk0 + b, COL_V // DV_HEAD + h)),
        pl.BlockSpec((seq_len, DK_HEAD), lambda b, h: (blk0 + b, h)),
        pl.BlockSpec((seq_len, DK_HEAD), lambda b, h: (blk0 + b, GLA_HEADS + h)),
        pl.BlockSpec((1, DV_HEAD), lambda b, h: (0, 0)),
    ]
    args = [proj, proj, proj, la, la, g]
    if has_s0:
        in_specs.append(pl.BlockSpec((1, 2, 1, DK_HEAD, DV_HEAD), lambda b, h: (b, 0, h, 0, 0)))
        args.append(s0)
    out_specs = [pl.BlockSpec((seq_len, DV_HEAD), lambda b, h: (b, h))]
    out_shape = [jax.ShapeDtypeStruct((n_seq * seq_len, GLA_DV), F32)]
    if emit_state:
        out_specs.append(pl.BlockSpec((1, 2, 1, DK_HEAD, DV_HEAD), lambda b, h: (b, 0, h, 0, 0)))
        out_shape.append(jax.ShapeDtypeStruct((n_seq, 2, GLA_HEADS, DK_HEAD, DV_HEAD), F32))

    res = pl.pallas_call(
        kern,
        grid=(n_seq, GLA_HEADS),
        in_specs=in_specs,
        out_specs=out_specs,
        out_shape=out_shape,
        scratch_shapes=[pltpu.VMEM((2, seq_len, DK_HEAD), F32),
                        pltpu.VMEM((2, seq_len, GLA_CHUNK + DK_HEAD), BF16),
                        pltpu.VMEM((2, seq_len, DK_HEAD), BF16),
                        pltpu.VMEM((2, seq_len, DK_HEAD), F32),
                        pltpu.VMEM((2, seq_len, DV_HEAD), F32),
                        pltpu.VMEM((2, DK_HEAD, DV_HEAD), F32)],
        compiler_params=pltpu.CompilerParams(
            dimension_semantics=("arbitrary", "arbitrary"), vmem_limit_bytes=VMEM_LIMIT),
        name="gla%s_seq%d" % ("_exact" if exact else "", seq_len),
    )(*args)
    return res


def _fnet_kernel(f_ref, cl_ref, sl_ref, cg_ref, sg_ref, o_ref, uc_ref, us_ref, *, seq_len):
    cg = cg_ref[...]
    sg = sg_ref[...]
    for grp in range(FNET_GROUPS):
        lo = grp * FNET_GROUP_DIM
        u = f_ref[:, lo:lo + FNET_GROUP_DIM].astype(BF16)
        uc_ref[:, lo:lo + FNET_GROUP_DIM] = _dot(u, cg).astype(BF16)
        us_ref[:, lo:lo + FNET_GROUP_DIM] = _dot(u, sg).astype(BF16)
    mixed = _dot(cl_ref[...], uc_ref[...]) - _dot(sl_ref[...], us_ref[...])
    o_ref[...] = mixed * (1.0 / math.sqrt(seq_len * FNET_GROUP_DIM))


def _dft_mats(n):
    j = np.arange(n, dtype=np.int64)
    ang = (2.0 * np.pi / n) * ((j[:, None] * j[None, :]) % n).astype(np.float64)
    return (jnp.asarray(np.cos(ang), dtype=F32).astype(BF16),
            jnp.asarray(np.sin(ang), dtype=F32).astype(BF16))


def _fnet(proj, *, n_seq, seq_len, row0):
    blk0 = row0 // seq_len
    cl, sl = _dft_mats(seq_len)
    cg, sg = _dft_mats(FNET_GROUP_DIM)
    kern = functools.partial(_fnet_kernel, seq_len=seq_len)
    return pl.pallas_call(
        kern,
        grid=(n_seq,),
        in_specs=[pl.BlockSpec((seq_len, FNET_DIM), lambda b: (blk0 + b, COL_F // FNET_DIM)),
                  pl.BlockSpec((seq_len, seq_len), lambda b: (0, 0)),
                  pl.BlockSpec((seq_len, seq_len), lambda b: (0, 0)),
                  pl.BlockSpec((FNET_GROUP_DIM, FNET_GROUP_DIM), lambda b: (0, 0)),
                  pl.BlockSpec((FNET_GROUP_DIM, FNET_GROUP_DIM), lambda b: (0, 0))],
        out_specs=pl.BlockSpec((seq_len, FNET_DIM), lambda b: (b, 0)),
        out_shape=jax.ShapeDtypeStruct((n_seq * seq_len, FNET_DIM), F32),
        scratch_shapes=[pltpu.VMEM((seq_len, FNET_DIM), BF16),
                        pltpu.VMEM((seq_len, FNET_DIM), BF16)],
        compiler_params=pltpu.CompilerParams(
            dimension_semantics=("arbitrary",), vmem_limit_bytes=VMEM_LIMIT),
        name="fnet_seq%d" % seq_len,
    )(proj, cl, sl, cg, sg)


def _merge_kernel(xc_ref, xl_ref, pos_ref, mod_ref, oc_ref, ol_ref, mc_ref, ml_ref,
                  g_ref, ga_ref, gb_ref, wbg_ref, wbf_ref, wo_ref, l1g_ref, l1b_ref, wrh_ref, wrl_ref,
                  br_ref, x1_ref, h2_ref, ridx_ref, rw_ref, *, n_ctx_tiles, alpha):
    i = pl.program_id(0)

    tm = x1_ref.shape[0]
    groups = [slice(r0, r0 + TM_MERGE_GROUP) for r0 in range(0, tm, TM_MERGE_GROUP)]
    shape = (TM_MERGE_GROUP, LANES)
    lane_i = lax.broadcasted_iota(jnp.int32, shape, 1)
    lane = lane_i.astype(F32)

    def compute(x_of, o_ref, mx_ref):
        branch = []
        for rows in groups:
            a = (o_ref[rows, :] * g_ref[rows, :]).astype(BF16)
            branch.append((_dot(a, wbg_ref[...]), _dot(mx_ref[rows, :].astype(BF16), wbf_ref[...])))
        mix = []
        for rows, (gla_out, fnet_out) in zip(groups, branch):
            merged = ga_ref[rows, :] * gla_out + gb_ref[rows, :] * fnet_out
            mix.append(_dot(merged.astype(BF16), wo_ref[...]))
        logits = []
        for rows, mix_g in zip(groups, mix):
            y = alpha * x_of(rows) + mod_ref[0, 2:3, :] * mix_g
            x1 = _layer_norm(y) * l1g_ref[...] + l1b_ref[...]
            x1_ref[rows, :] = x1
            h2 = _layer_norm(x1) * (1.0 + mod_ref[0, 4:5, :]) + mod_ref[0, 3:4, :]
            _store_row_tiles(h2_ref, h2, rows.start)
            h_hi, h_lo = _split_bf16(h2, 2)
            logits.append(_dot(h_hi, wrh_ref[...]) + _dot(h_lo, wrh_ref[...])
                          + _dot(h_hi, wrl_ref[...]) + br_ref[...])
        idx_out = [jnp.zeros(shape, F32) for _ in groups]
        val_out = [jnp.zeros(shape, F32) for _ in groups]
        top0 = [None] * len(groups)
        denom = [None] * len(groups)
        for kk in range(TOP_K):
            for gi in range(len(groups)):
                m = jnp.max(logits[gi], axis=-1, keepdims=True)
                sel = jnp.min(jnp.where(logits[gi] == m, lane, float(LANES)), axis=-1, keepdims=True)
                if kk == 0:
                    top0[gi] = m
                    p = jnp.ones_like(m)
                    denom[gi] = p
                else:
                    p = jnp.exp(m - top0[gi])
                    denom[gi] = denom[gi] + p
                idx_out[gi] = jnp.where(lane_i == kk, sel, idx_out[gi])
                val_out[gi] = jnp.where(lane_i == kk, p, val_out[gi])
                logits[gi] = jnp.where(lane == sel, -jnp.inf, logits[gi])
        for gi, rows in enumerate(groups):
            ridx_ref[rows, :] = idx_out[gi].astype(jnp.int32)
            rw_ref[rows, :] = val_out[gi] / denom[gi]

    @pl.when(i < n_ctx_tiles)
    def _():
        compute(lambda rows: xc_ref[rows, :], oc_ref, mc_ref)

    @pl.when(i >= n_ctx_tiles)
    def _():
        compute(lambda rows: xl_ref[rows, :] + pos_ref[rows, :], ol_ref, ml_ref)


def _merge(x_ctx, x_lat, pos, mod, o_ctx, o_lat, mixed_ctx, mixed_lat, proj,
           wbg, wbf, wo, l1g, l1b, wr_hi, wr_lo, br, lat_len, alpha):
    t_ctx, t_lat = x_ctx.shape[0], x_lat.shape[0]
    t_all = t_ctx + t_lat
    tm = TM_MERGE
    n_ctx_tiles = t_ctx // tm
    kern = functools.partial(_merge_kernel, n_ctx_tiles=n_ctx_tiles, alpha=alpha)
    specs = _token_specs(tm, n_ctx_tiles, lat_len // tm, mod.shape[0] - 1)
    ctx_map, lat_map = _group_maps(n_ctx_tiles)

    def const(shape):
        return pl.BlockSpec(shape, lambda i: (0,) * len(shape))

    in_specs = specs + [
        pl.BlockSpec((tm, GLA_DV), ctx_map),
        pl.BlockSpec((tm, GLA_DV), lat_map),
        pl.BlockSpec((tm, FNET_DIM), ctx_map),
        pl.BlockSpec((tm, FNET_DIM), lat_map),
        pl.BlockSpec((tm, GLA_DV), lambda i: (i, COL_G // GLA_DV)),
        pl.BlockSpec((tm, D_MODEL), lambda i: (i, COL_GATE_A // D_MODEL)),
        pl.BlockSpec((tm, D_MODEL), lambda i: (i, COL_GATE_B // D_MODEL)),
        const((GLA_DV, D_MODEL)), const((FNET_DIM, D_MODEL)), const((D_MODEL, D_MODEL)),
        const((1, D_MODEL)), const((1, D_MODEL)),
        const((D_MODEL, LANES)), const((D_MODEL, LANES)), const((1, LANES)),
    ]
    out_specs = [pl.BlockSpec((tm, D_MODEL), lambda i: (i, 0)),
                 pl.BlockSpec((tm * ROW_TILES, LANES), lambda i: (i, 0)),
                 pl.BlockSpec((tm, LANES), lambda i: (i, 0)),
                 pl.BlockSpec((tm, LANES), lambda i: (i, 0))]
    out_shape = [jax.ShapeDtypeStruct((t_all, D_MODEL), F32),
                 jax.ShapeDtypeStruct((t_all * ROW_TILES, LANES), ROW_DTYPE),
                 jax.ShapeDtypeStruct((t_all, LANES), jnp.int32),
                 jax.ShapeDtypeStruct((t_all, LANES), F32)]
    return pl.pallas_call(
        kern,
        grid=(t_all // tm,),
        in_specs=in_specs,
        out_specs=out_specs,
        out_shape=out_shape,
        compiler_params=pltpu.CompilerParams(
            dimension_semantics=("arbitrary",), vmem_limit_bytes=VMEM_LIMIT),
        name="merge_ln1_router",
    )(x_ctx, x_lat, pos, mod, o_ctx, o_lat, mixed_ctx, mixed_lat, proj, proj, proj,
      wbg, wbf, wo, l1g, l1b, wr_hi, wr_lo, br)


def _sc_mesh():
    return plsc.VectorSubcoreMesh(core_axis_name="c", subcore_axis_name="s")


def _sc_worker_id():
    return lax.axis_index("s") * SC_CORES + lax.axis_index("c")


def _sc_scatter_rows(src, idx, n_out):
    n_src = src.shape[0]
    w = SC_WINDOW
    n_chunks = n_src // (SC_WORKERS * w)
    copies = idx.shape[0]
    assert n_chunks % 2 == 0 and idx.shape == (copies, SC_WORKERS, n_chunks, w)

    @functools.partial(
        pl.kernel, mesh=_sc_mesh(),
        out_type=jax.ShapeDtypeStruct((n_out, ROW_TILES, LANES), ROW_DTYPE),
        scratch_types=[pltpu.VMEM((copies * n_chunks, w), jnp.int32),
                       pltpu.VMEM((2, w, ROW_TILES, LANES), ROW_DTYPE),
                       pltpu.SemaphoreType.DMA((2,)),
                       pltpu.SemaphoreType.DMA((2,))],
        name="moe_dispatch_scatter")
    def k(src_hbm, idx_hbm, out_hbm, idx_v, rows_v, rsem, wsem):
        wid = _sc_worker_id()
        base = wid * (n_chunks * w)
        for kk in range(copies):
            pltpu.sync_copy(idx_hbm.at[kk, wid], idx_v.at[pl.ds(kk * n_chunks, n_chunks)])

        def read(j, slot):
            return pltpu.make_async_copy(src_hbm.at[pl.ds(base + j * w, w)], rows_v.at[slot],
                                         rsem.at[slot])

        def scatter(j, kk, slot):
            return pltpu.make_async_copy(rows_v.at[slot], out_hbm.at[idx_v.at[kk * n_chunks + j]],
                                         wsem.at[slot])

        read(0, 0).start()

        @pl.loop(0, n_chunks, step=2)
        def _(jj):
            read(jj, 0).wait()

            @pl.when(jj > 0)
            def _():
                for kk in range(copies):
                    scatter(jj - 1, kk, 1).wait()

            read(jj + 1, 1).start()
            for kk in range(copies):
                scatter(jj, kk, 0).start()
            read(jj + 1, 1).wait()
            for kk in range(copies):
                scatter(jj, kk, 0).wait()

            @pl.when(jj + 2 < n_chunks)
            def _():
                read(jj + 2, 0).start()

            for kk in range(copies):
                scatter(jj + 1, kk, 1).start()

        for kk in range(copies):
            scatter(n_chunks - 1, kk, 1).wait()

    return k(src, idx)


def _sc_gather_rows(table, idx):
    _, n_chunks, w = idx.shape
    assert n_chunks % 2 == 0 and idx.shape[0] == SC_WORKERS and w == SC_WINDOW
    n_out = SC_WORKERS * n_chunks * w

    @functools.partial(
        pl.kernel, mesh=_sc_mesh(),
        out_type=jax.ShapeDtypeStruct((n_out, ROW_TILES, LANES), ROW_DTYPE),
        scratch_types=[pltpu.VMEM((n_chunks, w), jnp.int32),
                       pltpu.VMEM((2, w, ROW_TILES, LANES), ROW_DTYPE),
                       pltpu.SemaphoreType.DMA((2,)),
                       pltpu.SemaphoreType.DMA((2,))],
        name="moe_combine_gather")
    def k(table_hbm, idx_hbm, out_hbm, idx_v, rows_v, gsem, wsem):
        wid = _sc_worker_id()
        base = wid * (n_chunks * w)
        pltpu.sync_copy(idx_hbm.at[wid], idx_v)

        def gather(j, slot):
            return pltpu.make_async_copy(table_hbm.at[idx_v.at[j]], rows_v.at[slot], gsem.at[slot])

        def write(j, slot):
            return pltpu.make_async_copy(rows_v.at[slot], out_hbm.at[pl.ds(base + j * w, w)],
                                         wsem.at[slot])

        gather(0, 0).start()

        @pl.loop(0, n_chunks, step=2)
        def _(jj):
            gather(jj, 0).wait()

            @pl.when(jj > 0)
            def _():
                write(jj - 1, 1).wait()

            gather(jj + 1, 1).start()
            write(jj, 0).start()
            gather(jj + 1, 1).wait()
            write(jj, 0).wait()

            @pl.when(jj + 2 < n_chunks)
            def _():
                gather(jj + 2, 0).start()

            write(jj + 1, 1).start()

        write(n_chunks - 1, 1).wait()

    return k(table, idx)


def _moe_kernel(be_ref, nu_ref, nv_ref, slot_ref, nxt_ref, x_ref, wgu_hbm, bgu_ref, wd_hbm, bd_ref,
                o_ref, wgu_st, wd_st, wgu_bf, wd_bf, xb_ref, sem):
    b = pl.program_id(0)
    e = be_ref[b]
    prev = be_ref[jnp.maximum(b - 1, 0)]
    active = b < nu_ref[0]
    changed = (b == 0) | (e != prev)

    def weight_copies(expert, s):
        return (pltpu.make_async_copy(wgu_hbm.at[expert], wgu_st.at[s], sem.at[0, s]),
                pltpu.make_async_copy(wd_hbm.at[expert], wd_st.at[s], sem.at[1, s]))

    @pl.when(active & changed)
    def _():
        s = slot_ref[b]

        @pl.when(b == 0)
        def _():
            for cp in weight_copies(e, s):
                cp.start()

        for cp in weight_copies(e, s):
            cp.wait()
        wgu_bf[...] = wgu_st[s].astype(BF16)
        wd_bf[...] = wd_st[s].astype(BF16)
        nxt = nxt_ref[b]

        @pl.when(nxt >= 0)
        def _():
            for cp in weight_copies(nxt, 1 - s):
                cp.start()

    n_valid = nv_ref[b]

    def expert_mlp(n_rows):
        valid = lax.broadcasted_iota(jnp.int32, (n_rows, LANES), 0) < n_valid
        for j in range(ROW_TILES):
            for half, xj in enumerate(_load_row_tile(x_ref, j, n_rows)):
                c0 = half * HALF_MODEL + j * LANES
                xb_ref[0:n_rows, c0:c0 + LANES] = jnp.where(valid, xj, 0.0).astype(BF16)
        gu = _dot(xb_ref[0:n_rows, :], wgu_bf[...]) + bgu_ref[0]
        gate = jnp.minimum(gu[:, :D_EXPERT], SWIGLU_LIMIT)
        up = jnp.clip(gu[:, D_EXPERT:], -SWIGLU_LIMIT, SWIGLU_LIMIT)
        glu = gate * _sigmoid(gate * SWIGLU_ALPHA)
        act = ((up + 1.0) * glu).astype(BF16)
        _store_row_tiles(o_ref, _dot(act, wd_bf[...]) + bd_ref[0])

    for n_rows in range(MOE_STEP, MOE_ROWS + 1, MOE_STEP):
        @pl.when(active & (n_valid > n_rows - MOE_STEP) & (n_valid <= n_rows))
        def _(n_rows=n_rows):
            expert_mlp(n_rows)


def _moe(tables, xs, w_gate_up, b_gate_up, w_down, b_down):
    p_rows = xs.shape[0] // ROW_TILES
    n_blocks = p_rows // MOE_ROWS

    def blk(b, be, nu, *_):
        return jnp.minimum(b, nu[0] - 1)

    def expert(b, be, nu, *_):
        return (be[blk(b, be, nu)], 0, 0)

    def rows(b, be, nu, *_):
        return (blk(b, be, nu), 0)

    grid_spec = pltpu.PrefetchScalarGridSpec(
        num_scalar_prefetch=len(tables),
        grid=(n_blocks,),
        in_specs=[
            pl.BlockSpec((MOE_ROWS * ROW_TILES, LANES), rows),
            pl.BlockSpec(memory_space=pl.ANY),
            pl.BlockSpec((1, 1, 2 * D_EXPERT), expert),
            pl.BlockSpec(memory_space=pl.ANY),
            pl.BlockSpec((1, 1, D_MODEL), expert),
        ],
        out_specs=pl.BlockSpec((MOE_ROWS * ROW_TILES, LANES), rows),
        scratch_shapes=[pltpu.VMEM((2, D_MODEL, 2 * D_EXPERT), F32),
                        pltpu.VMEM((2, D_EXPERT, D_MODEL), F32),
                        pltpu.VMEM((D_MODEL, 2 * D_EXPERT), BF16),
                        pltpu.VMEM((D_EXPERT, D_MODEL), BF16),
                        pltpu.VMEM((MOE_ROWS, D_MODEL), BF16),
                        pltpu.SemaphoreType.DMA((2, 2))],
    )
    return pl.pallas_call(
        _moe_kernel,
        grid_spec=grid_spec,
        out_shape=jax.ShapeDtypeStruct((p_rows * ROW_TILES, LANES), ROW_DTYPE),
        compiler_params=pltpu.CompilerParams(
            dimension_semantics=("arbitrary",), vmem_limit_bytes=VMEM_LIMIT),
        name="moe_grouped_mlp",
    )(*tables, xs, w_gate_up, b_gate_up.reshape(N_EXPERTS, 1, 2 * D_EXPERT), w_down,
      b_down.reshape(N_EXPERTS, 1, D_MODEL))


def _combine_kernel(x1_ref, y0_ref, y1_ref, y2_ref, y3_ref, rw_ref, mod_ref, g_ref, b_ref, o_ref,
                    *, alpha):
    rw = rw_ref[...]
    y_refs = (y0_ref, y1_ref, y2_ref, y3_ref)
    pieces = [None] * (2 * ROW_TILES)
    for j in range(ROW_TILES):
        for kk in range(TOP_K):
            for half, yj in enumerate(_load_row_tile(y_refs[kk], j, rw.shape[0])):
                term = rw[:, kk:kk + 1] * yj
                slot = half * ROW_TILES + j
                pieces[slot] = term if kk == 0 else pieces[slot] + term
    ff = jnp.concatenate(pieces, axis=-1)
    y = alpha * x1_ref[...] + mod_ref[0, 5:6, :] * ff
    o_ref[...] = _layer_norm(y) * g_ref[...] + b_ref[...]


def _combine(x1, yg, rw, mod, l2g, l2b, *, row0, n_rows, mod_map, alpha):
    tm = TM_MIX
    t0 = row0 // tm
    tiles = n_rows // tm
    kern = functools.partial(_combine_kernel, alpha=alpha)

    def y_spec(kk):
        return pl.BlockSpec((tm * ROW_TILES, LANES), lambda i: (kk * tiles + i, 0))

    return pl.pallas_call(
        kern,
        grid=(n_rows // tm,),
        in_specs=[pl.BlockSpec((tm, D_MODEL), lambda i: (t0 + i, 0))]
        + [y_spec(kk) for kk in range(TOP_K)]
        + [pl.BlockSpec((tm, LANES), lambda i: (t0 + i, 0)),
           pl.BlockSpec((1, N_MOD, D_MODEL), mod_map),
           pl.BlockSpec((1, D_MODEL), lambda i: (0, 0)),
           pl.BlockSpec((1, D_MODEL), lambda i: (0, 0))],
        out_specs=pl.BlockSpec((tm, D_MODEL), lambda i: (i, 0)),
        out_shape=jax.ShapeDtypeStruct((n_rows, D_MODEL), F32),
        compiler_params=pltpu.CompilerParams(
            dimension_semantics=("arbitrary",), vmem_limit_bytes=VMEM_LIMIT),
        name="combine_ln2",
    )(x1, yg, yg, yg, yg, rw, mod, l2g, l2b)


def _route_kernel(ridx_ref, dest_ref, cnt_ref, run_ref, bst_ref):
    phase = pl.program_id(0)
    i = pl.program_id(1)
    tm = ridx_ref.shape[0]
    ridx = ridx_ref[...]
    lane = lax.broadcasted_iota(jnp.int32, (tm, LANES), 1)
    hits = [ridx[:, kk:kk + 1] == lane for kk in range(TOP_K)]
    chosen = jnp.where(hits[0], 1.0, 0.0)
    for kk in range(1, TOP_K):
        chosen = chosen + jnp.where(hits[kk], 1.0, 0.0)
    colsum = jnp.sum(chosen, axis=0, keepdims=True)

    @pl.when((phase == 0) & (i == 0))
    def _():
        run_ref[...] = jnp.zeros_like(run_ref)

    @pl.when(phase == 0)
    def _():
        run_ref[...] = run_ref[...] + colsum

    @pl.when((phase == 1) & (i == 0))
    def _():
        counts = run_ref[...]
        cnt_ref[...] = counts
        blocks = jnp.floor((counts + (MOE_ROWS - 1.0)) * (1.0 / MOE_ROWS))
        r = lax.broadcasted_iota(jnp.int32, (LANES, LANES), 0)
        c = lax.broadcasted_iota(jnp.int32, (LANES, LANES), 1)
        before = jnp.dot(blocks, (r < c).astype(F32), precision=HIGHEST, preferred_element_type=F32)
        bst_ref[...] = before * float(MOE_ROWS)
        run_ref[...] = jnp.zeros_like(run_ref)

    @pl.when(phase == 1)
    def _():
        rt = lax.broadcasted_iota(jnp.int32, (tm, tm), 0)
        ct = lax.broadcasted_iota(jnp.int32, (tm, tm), 1)
        earlier = _dot((ct < rt).astype(BF16), chosen.astype(BF16))
        row_of = bst_ref[0:1, :] + run_ref[0:1, :] + earlier
        out = jnp.take_along_axis(row_of, ridx, axis=1)
        dest_ref[...] = out.T[0:SUBLANES, :].astype(jnp.int32)
        run_ref[...] = run_ref[...] + colsum


def _route(ridx):
    t_all = ridx.shape[0]
    tm = TM_ROUTE
    return pl.pallas_call(
        _route_kernel,
        grid=(2, t_all // tm),
        in_specs=[pl.BlockSpec((tm, LANES), lambda p, i: (i, 0))],
        out_specs=[pl.BlockSpec((SUBLANES, tm), lambda p, i: (0, i * p)),
                   pl.BlockSpec((SUBLANES, LANES), lambda p, i: (0, 0))],
        out_shape=[jax.ShapeDtypeStruct((SUBLANES, t_all), jnp.int32),
                   jax.ShapeDtypeStruct((SUBLANES, LANES), F32)],
        scratch_shapes=[pltpu.VMEM((SUBLANES, LANES), F32),
                        pltpu.VMEM((SUBLANES, LANES), F32)],
        compiler_params=pltpu.CompilerParams(
            dimension_semantics=("arbitrary", "arbitrary"), vmem_limit_bytes=VMEM_LIMIT),
        name="moe_route",
    )(ridx)


def _routing_tables(counts, n_blocks):
    experts = jnp.arange(N_EXPERTS, dtype=jnp.int32)
    blocks_per = (counts + MOE_ROWS - 1) // MOE_ROWS
    bends = jnp.cumsum(blocks_per)
    bstarts = bends - blocks_per
    blocks = jnp.arange(n_blocks, dtype=jnp.int32)
    block_expert = jnp.minimum(
        jnp.sum((bends[None, :] <= blocks[:, None]).astype(jnp.int32), axis=1), N_EXPERTS - 1)
    n_used = bends[-1:].astype(jnp.int32)
    owner = block_expert[:, None] == experts[None, :]

    def per_block(table):
        return jnp.sum(jnp.where(owner, table[None, :], 0), axis=1)

    n_valid = jnp.clip(per_block(counts) - (blocks - per_block(bstarts)) * MOE_ROWS,
                       0, MOE_ROWS).astype(jnp.int32)
    present = blocks_per > 0
    ordinal = jnp.cumsum(present.astype(jnp.int32)) - 1
    later = lax.cummin(jnp.where(present, experts, N_EXPERTS), reverse=True)
    succ = jnp.concatenate([later[1:], jnp.full((1,), N_EXPERTS, jnp.int32)])
    succ = jnp.where(succ >= N_EXPERTS, -1, succ)
    stage_slot = (per_block(ordinal) % 2).astype(jnp.int32)
    next_expert = per_block(succ).astype(jnp.int32)
    return (block_expert.astype(jnp.int32), n_used, n_valid, stage_slot, next_expert)


def _pos_embed_2d(n_tokens):
    rows = n_tokens // GRID_W
    r = np.repeat(np.arange(rows), GRID_W).astype(np.float32)
    col = np.tile(np.arange(GRID_W), rows).astype(np.float32)
    quarter = D_MODEL // 4
    omega = (np.float32(1.0)
             / np.power(np.float32(10000.0), np.arange(quarter, dtype=np.float32) / np.float32(quarter)))
    er = (r[:, None] * omega).astype(np.float64)
    ec = (col[:, None] * omega).astype(np.float64)
    table = np.concatenate([np.sin(er), np.cos(er), np.sin(ec), np.cos(ec)], axis=-1)
    return jnp.asarray(table, dtype=F32)


def _split_w_in(w):
    wt = w.T
    o_r = 2 * GLA_DK + 2 * GLA_DV
    o_f = o_r + DECAY_RANK
    o_gate = o_f + FNET_DIM
    row_scale = jnp.where(jnp.arange(o_r) < GLA_DK, DK_HEAD ** -0.5, 1.0).astype(w.dtype)
    w_main = (wt[:o_r] * row_scale[:, None]).astype(BF16)
    pad = jnp.zeros((LANES - DECAY_RANK, w.shape[0]), BF16)
    w_tail = jnp.concatenate([wt[o_gate:].astype(BF16), wt[o_f:o_gate].astype(BF16),
                              wt[o_r:o_f].astype(BF16), pad], axis=0)
    return w_main, w_tail


def kernel(x_prompt, x_sample, state_gla, c, c_ctx, w_ada, b_ada, w_in, w_dec_fwd, b_dec_fwd,
           w_dec_bwd, b_dec_bwd, gla_norm_g, w_br_gla, w_br_fnet, w_out, ln1_g, ln1_b, w_router,
           b_router, w_gate_up, b_gate_up, w_down, b_down, ln2_g, ln2_b):
    n_req, ctx_len, _ = x_prompt.shape
    n_lat, lat_len, _ = x_sample.shape
    depth = w_in.shape[0]
    alpha = (2.0 * depth) ** 0.25
    t_ctx = n_req * ctx_len
    t_lat = n_lat * lat_len
    t_all = t_ctx + t_lat
    tile_rows = max(TM_PROJ, TM_MERGE, TM_MIX)
    assert t_ctx % tile_rows == 0 and lat_len % tile_rows == 0
    assert ctx_len % GLA_CHUNK == 0 and lat_len % GLA_CHUNK == 0 and t_all % TM_ROUTE == 0
    assert t_ctx % (SC_WORKERS * SC_WINDOW * 2) == 0 and t_lat % (SC_WORKERS * SC_WINDOW * 2) == 0
    assert MOE_ROWS & (MOE_ROWS - 1) == 0 and (t_all * TOP_K) % MOE_ROWS == 0

    x_ctx = x_prompt.reshape(t_ctx, D_MODEL)
    x_lat = x_sample.reshape(t_lat, D_MODEL)
    pos = _pos_embed_2d(lat_len)
    zero_pos = jnp.zeros_like(pos)

    cond_rows = -(-(n_lat + 1) // SUBLANES) * SUBLANES
    cond = jnp.zeros((cond_rows, D_MODEL), F32).at[:n_lat].set(c).at[cond_rows - 1].set(c_ctx)

    n_moe_blocks = (t_all * TOP_K) // MOE_ROWS + N_EXPERTS
    tok_chunks = t_all // (SC_WORKERS * SC_WINDOW)
    states = []
    for l in range(depth):
        mod = _ada(cond, w_ada[l], b_ada[l]).reshape(cond_rows, N_MOD, D_MODEL)
        layer_pos = pos if l == 0 else zero_pos
        w_dec = jnp.zeros((2, LANES, GLA_DK), F32)
        w_dec = w_dec.at[0, :DECAY_RANK].set(w_dec_fwd[l]).at[1, :DECAY_RANK].set(w_dec_bwd[l])
        b_dec = jnp.stack([b_dec_fwd[l], b_dec_bwd[l]]).reshape(2, 1, GLA_DK)
        proj, la, la_min = _inproj(x_ctx, x_lat, layer_pos, mod, *_split_w_in(w_in[l]), w_dec, b_dec,
                                   lat_len)

        norm_g = gla_norm_g[l].reshape(1, DV_HEAD)
        s0_lat = state_gla[:, l]

        def gla_both(exact):
            o_c, s_c = _gla(proj, la, norm_g, None, n_seq=n_req, seq_len=ctx_len, row0=0,
                            emit_state=True, exact=exact)
            (o_l,) = _gla(proj, la, norm_g, s0_lat, n_seq=n_lat, seq_len=lat_len, row0=t_ctx,
                          emit_state=False, exact=exact)
            return o_c, s_c, o_l

        fast_ok = jnp.min(la_min) >= -GLA_FAST_MAX_STEP_DECAY
        o_ctx, s_new, o_lat = lax.cond(fast_ok, lambda: gla_both(False), lambda: gla_both(True))
        states.append(s_new)

        mixed_ctx = _fnet(proj, n_seq=n_req, seq_len=ctx_len, row0=0)
        mixed_lat = _fnet(proj, n_seq=n_lat, seq_len=lat_len, row0=t_ctx)

        wr = jnp.zeros((D_MODEL, LANES), F32).at[:, :N_EXPERTS].set(w_router[l])
        br = jnp.full((1, LANES), -1e30, F32).at[0, :N_EXPERTS].set(b_router[l])
        wr_hi = wr.astype(BF16)
        wr_lo = (wr - wr_hi.astype(F32)).astype(BF16)
        x1, h2, ridx, rw = _merge(
            x_ctx, x_lat, layer_pos, mod, o_ctx, o_lat, mixed_ctx, mixed_lat, proj,
            w_br_gla[l].astype(BF16), w_br_fnet[l].astype(BF16), (0.5 * w_out[l]).astype(BF16),
            ln1_g[l].reshape(1, D_MODEL), ln1_b[l].reshape(1, D_MODEL), wr_hi, wr_lo, br, lat_len,
            alpha)

        dest, counts = _route(ridx)
        moe_tables = _routing_tables(counts[0, :N_EXPERTS].astype(jnp.int32), n_moe_blocks)
        dest = dest[:TOP_K]
        scatter_idx = dest.reshape(TOP_K, SC_WORKERS, tok_chunks, SC_WINDOW)
        p_rows = n_moe_blocks * MOE_ROWS
        xs = _sc_scatter_rows(h2.reshape(t_all, ROW_TILES, LANES), scatter_idx, p_rows)
        yb = _moe(moe_tables, xs.reshape(p_rows * ROW_TILES, LANES),
                  w_gate_up[l], b_gate_up[l], w_down[l], b_down[l])
        yb = yb.reshape(p_rows, ROW_TILES, LANES)

        def gathered(row0, n_rows):
            idx = dest[:, row0:row0 + n_rows].reshape(SC_WORKERS, -1, SC_WINDOW)
            return _sc_gather_rows(yb, idx).reshape(TOP_K * n_rows * ROW_TILES, LANES)

        l2g = ln2_g[l].reshape(1, D_MODEL)
        l2b = ln2_b[l].reshape(1, D_MODEL)
        tiles_per_seq = lat_len // TM_MIX
        yg_ctx = gathered(0, t_ctx)
        yg_lat = gathered(t_ctx, t_lat)
        x_ctx = _combine(x1, yg_ctx, rw, mod, l2g, l2b, row0=0, n_rows=t_ctx,
                         mod_map=lambda i: (cond_rows - 1, 0, 0), alpha=alpha)
        x_lat = _combine(x1, yg_lat, rw, mod, l2g, l2b, row0=t_ctx, n_rows=t_lat,
                         mod_map=lambda i: (i // tiles_per_seq, 0, 0), alpha=alpha)

    y_prompt = x_ctx.reshape(x_prompt.shape)
    y_sample = x_lat.reshape(x_sample.shape)
    new_state = jnp.stack(states, axis=1).astype(x_prompt.dtype)
    return (y_prompt, y_sample, new_state)
```

```python
import functools
import math

import numpy as np
import jax
import jax.numpy as jnp
from jax import lax
from jax.experimental import pallas as pl
from jax.experimental.pallas import tpu as pltpu
from jax.experimental.pallas import tpu_sc as plsc

F32 = jnp.float32
BF16 = jnp.bfloat16

D_MODEL = 1024
GRID_W = 64
GLA_HEADS = 4
DK_HEAD = 128
DV_HEAD = 256
GLA_DK = GLA_HEADS * DK_HEAD
GLA_DV = GLA_HEADS * DV_HEAD
DECAY_RANK = 16
GATE_NORMALIZER = 16.0
FNET_GROUPS = 4
FNET_GROUP_DIM = 128
FNET_DIM = FNET_GROUPS * FNET_GROUP_DIM
N_EXPERTS = 32
TOP_K = 4
D_EXPERT = 1024
SWIGLU_LIMIT = 7.0
SWIGLU_ALPHA = 1.702
LN_EPS = 1e-6
N_MOD = 6

LANES = 128
SUBLANES = 8
HALF_MODEL = D_MODEL // 2
ROW_TILES = HALF_MODEL // LANES
ROW_DTYPE = jnp.uint32
COL_Q = 0
COL_K = GLA_DK
COL_V = 2 * GLA_DK
COL_G = COL_V + GLA_DV
COL_GATE_A = COL_G + GLA_DV
COL_GATE_B = COL_GATE_A + D_MODEL
COL_F = COL_GATE_B + D_MODEL
COL_R = COL_F + FNET_DIM
PROJ_COLS = COL_R + LANES

GLA_CHUNK = 128
GLA_LEAF = 16
GLA_UNROLL = 8
assert GLA_CHUNK == DK_HEAD
GLA_FAST_MAX_STEP_DECAY = 8.0
TM_PROJ = 512
PROJ_GROUP = 256
TM_MIX = 256
TM_MERGE = 512
TM_MERGE_GROUP = 256
TM_ROUTE = 1024
MOE_ROWS = 512
MOE_STEP = 128
VMEM_LIMIT = 56 * 1024 * 1024

SC_CORES = 2
SC_SUBCORES = 16
SC_WORKERS = SC_CORES * SC_SUBCORES
SC_WINDOW = 64

HIGHEST = lax.Precision.HIGHEST


def _layer_norm(x):
    mu = jnp.mean(x, axis=-1, keepdims=True)
    xc = x - mu
    var = jnp.mean(xc * xc, axis=-1, keepdims=True)
    return xc * lax.rsqrt(var + LN_EPS)


def _sigmoid(x):
    return 0.5 * jnp.tanh(0.5 * x) + 0.5


def _log_sigmoid(z):
    return jnp.minimum(z, 0.0) - jnp.log(1.0 + jnp.exp(-jnp.abs(z)))


def _dot(a, b):
    return jnp.dot(a, b, preferred_element_type=F32)


def _split_bf16(x, terms):
    parts = []
    for _ in range(terms):
        p = x.astype(BF16)
        parts.append(p)
        x = x - p.astype(F32)
    return parts


def _dot_nt(a, b):
    return lax.dot_general(a, b, (((1,), (1,)), ((), ())), preferred_element_type=F32)


def _dot_tn(a, b):
    return lax.dot_general(a, b, (((0,), (0,)), ((), ())), preferred_element_type=F32)


def _row_tile_slice(j, n_rows, first_row=0):
    return pl.ds(first_row * ROW_TILES + j, n_rows, stride=ROW_TILES)


def _store_row_tiles(ref, val, first_row=0):
    for j in range(ROW_TILES):
        lo = val[:, j * LANES:(j + 1) * LANES]
        hi = val[:, HALF_MODEL + j * LANES:HALF_MODEL + (j + 1) * LANES]
        ref[_row_tile_slice(j, val.shape[0], first_row), :] = pltpu.pack_elementwise(
            [lo, hi], packed_dtype=BF16)


def _load_row_tile(ref, j, n_rows, first_row=0):
    words = ref[_row_tile_slice(j, n_rows, first_row), :]
    return tuple(pltpu.unpack_elementwise(words, index=half, packed_dtype=BF16, unpacked_dtype=F32)
                 for half in range(2))


def _ada_kernel(c_ref, w_ref, b_ref, o_ref):
    c = c_ref[...]
    s = c * _sigmoid(c)
    o_ref[...] = _dot(s.astype(BF16), w_ref[...].astype(BF16)) + b_ref[...]


def _ada(cond, w_ada, b_ada):
    rows = cond.shape[0]
    n = w_ada.shape[1]
    tn = 1536
    return pl.pallas_call(
        _ada_kernel,
        grid=(n // tn,),
        in_specs=[pl.BlockSpec((rows, D_MODEL), lambda j: (0, 0)),
                  pl.BlockSpec((D_MODEL, tn), lambda j: (0, j)),
                  pl.BlockSpec((1, tn), lambda j: (0, j))],
        out_specs=pl.BlockSpec((rows, tn), lambda j: (0, j)),
        out_shape=jax.ShapeDtypeStruct((rows, n), F32),
        compiler_params=pltpu.CompilerParams(vmem_limit_bytes=VMEM_LIMIT),
        name="ada_mod",
    )(cond, w_ada, b_ada.reshape(1, n))


def _group_maps(n_ctx_tiles):
    def ctx_map(i, *_):
        return (jnp.minimum(i, n_ctx_tiles - 1), 0)

    def lat_map(i, *_):
        return (jnp.maximum(i - n_ctx_tiles, 0), 0)

    return ctx_map, lat_map


def _token_specs(tm, n_ctx_tiles, tiles_per_latent_seq, ctx_mod_row):
    ctx_map, lat_map = _group_maps(n_ctx_tiles)

    def pos_map(i, *_):
        return (jnp.maximum(i - n_ctx_tiles, 0) % tiles_per_latent_seq, 0)

    def mod_map(i, *_):
        return (jnp.where(i < n_ctx_tiles, ctx_mod_row,
                          jnp.maximum(i - n_ctx_tiles, 0) // tiles_per_latent_seq), 0, 0)

    return [pl.BlockSpec((tm, D_MODEL), ctx_map),
            pl.BlockSpec((tm, D_MODEL), lat_map),
            pl.BlockSpec((tm, D_MODEL), pos_map),
            pl.BlockSpec((1, N_MOD, D_MODEL), mod_map)]


def _inproj_kernel(xc_ref, xl_ref, pos_ref, mod_ref, wm_ref, wt_ref, wdh_ref, wdl_ref, bd_ref,
                   o_ref, la_ref, lamin_ref, *, n_ctx_tiles):
    tail = lambda lo, hi: wt_ref[lo - COL_GATE_A:hi - COL_GATE_A, :]
    i = pl.program_id(0)

    def project(x_of):
        groups = [slice(r0, r0 + PROJ_GROUP) for r0 in range(0, TM_PROJ, PROJ_GROUP)]
        hs = [(_layer_norm(x_of(rows)) * (1.0 + mod_ref[0, 1:2, :]) + mod_ref[0, 0:1, :]).astype(BF16)
              for rows in groups]
        rs = [_split_bf16(_dot_nt(h, tail(COL_R, PROJ_COLS)), 2) for h in hs]
        steepest = None
        for rows, (r_hi, r_lo) in zip(groups, rs):
            for d in range(2):
                z = (_dot(r_hi, wdh_ref[d]) + _dot(r_lo, wdh_ref[d]) + _dot(r_hi, wdl_ref[d])
                     + bd_ref[d])
                la = _log_sigmoid(z) * (1.0 / GATE_NORMALIZER)
                la_ref[rows, d * GLA_DK:(d + 1) * GLA_DK] = la
                low = jnp.min(la, axis=0, keepdims=True)
                for c0 in range(0, GLA_DK, LANES):
                    piece = low[:, c0:c0 + LANES]
                    steepest = piece if steepest is None else jnp.minimum(steepest, piece)
        lamin_ref[...] = jnp.broadcast_to(steepest, (SUBLANES, LANES))
        for rows, h in zip(groups, hs):
            hg = 0.5 * _dot_nt(h, wm_ref[COL_G:COL_GATE_A, :])
            o_ref[rows, COL_G:COL_GATE_A] = hg * (jnp.tanh(hg) + 1.0)
        for rows, h in zip(groups, hs):
            gates = _dot_nt(h, tail(COL_GATE_A, COL_F))
            o_ref[rows, COL_GATE_A:COL_F] = jnp.tanh(0.5 * gates) + 1.0
        for rows, h in zip(groups, hs):
            o_ref[rows, :COL_G] = _dot_nt(h, wm_ref[:COL_G, :])
        for rows, h in zip(groups, hs):
            o_ref[rows, COL_F:COL_R] = _dot_nt(h, tail(COL_F, COL_R))

    @pl.when(i < n_ctx_tiles)
    def _():
        project(lambda rows: xc_ref[rows, :])

    @pl.when(i >= n_ctx_tiles)
    def _():
        project(lambda rows: xl_ref[rows, :] + pos_ref[rows, :])


def _inproj(x_ctx, x_lat, pos, mod, w_main, w_tail, w_dec, b_dec, lat_len):
    t_ctx, t_lat = x_ctx.shape[0], x_lat.shape[0]
    t_all = t_ctx + t_lat
    n_ctx_tiles = t_ctx // TM_PROJ
    kern = functools.partial(_inproj_kernel, n_ctx_tiles=n_ctx_tiles)
    specs = _token_specs(TM_PROJ, n_ctx_tiles, lat_len // TM_PROJ, mod.shape[0] - 1)
    w_dec_hi = w_dec.astype(BF16)
    w_dec_lo = (w_dec - w_dec_hi.astype(F32)).astype(BF16)

    def const(shape):
        return pl.BlockSpec(shape, lambda i: (0,) * len(shape))

    return pl.pallas_call(
        kern,
        grid=(t_all // TM_PROJ,),
        in_specs=specs + [pl.BlockSpec(w_main.shape, lambda i: (0, 0), pipeline_mode=pl.Buffered(1)),
                          pl.BlockSpec(w_tail.shape, lambda i: (0, 0), pipeline_mode=pl.Buffered(1)),
                          const((2, LANES, GLA_DK)), const((2, LANES, GLA_DK)), const((2, 1, GLA_DK))],
        out_specs=[pl.BlockSpec((TM_PROJ, COL_R), lambda i: (i, 0)),
                   pl.BlockSpec((TM_PROJ, 2 * GLA_DK), lambda i: (i, 0)),
                   pl.BlockSpec((SUBLANES, LANES), lambda i: (i, 0))],
        out_shape=[jax.ShapeDtypeStruct((t_all, COL_R), F32),
                   jax.ShapeDtypeStruct((t_all, 2 * GLA_DK), F32),
                   jax.ShapeDtypeStruct((t_all // TM_PROJ * SUBLANES, LANES), F32)],
        compiler_params=pltpu.CompilerParams(
            dimension_semantics=("arbitrary",), vmem_limit_bytes=VMEM_LIMIT),
        name="ln_inproj",
    )(x_ctx, x_lat, pos, mod, w_main, w_tail, w_dec_hi, w_dec_lo, b_dec)


def _gla_kernel(*refs, seq_len, seqs, has_s0, emit_state, exact):
    it = iter(refs)
    q_ref, k_ref, v_ref, laf_ref, lab_ref, g_ref = (next(it) for _ in range(6))
    s0_ref = next(it) if has_s0 else None
    o_ref = next(it)
    sout_ref = next(it) if emit_state else None
    cum_ref, aq_ref, ko_ref, dec_ref, op_ref, st_ref = (next(it) for _ in range(6))

    C = GLA_CHUNK
    n_chunks = seq_len // C
    all_chunks = seqs * n_chunks
    assert all_chunks <= GLA_UNROLL or (seqs == 1 and n_chunks % GLA_UNROLL == 0)

    def rows(n):
        if isinstance(n, int):
            return pl.ds(n * C, C)
        return pl.ds(pl.multiple_of(n * C, C), C)

    def loop(body, count=all_chunks):
        if count <= GLA_UNROLL:
            for n in range(count):
                body(n)
        else:
            def step(m, carry):
                for u in range(GLA_UNROLL):
                    body(GLA_UNROLL * m + u)
                return carry
            lax.fori_loop(0, count // GLA_UNROLL, step, 0)

    rt = lax.broadcasted_iota(jnp.int32, (C, C), 0)
    ct = lax.broadcasted_iota(jnp.int32, (C, C), 1)
    tri = ((rt >= ct).astype(BF16), (ct >= rt).astype(BF16))
    row_id = lax.broadcasted_iota(jnp.int32, (C, DK_HEAD), 0)

    def cumsum_chunk(n):
        for d, la_ref in enumerate((laf_ref, lab_ref)):
            la_hi, la_lo = _split_bf16(la_ref[rows(n), :], 2)
            cum_ref[d, rows(n), :] = _dot(tri[d], la_hi) + _dot(tri[d], la_lo)

    loop(cumsum_chunk)

    query_rows = ({}, {})
    keep = ({}, {})
    for d in range(2):
        blk = C // 2
        while blk >= GLA_LEAF:
            q_parity = 1 if d == 0 else 0
            query_rows[d][blk] = ((row_id // blk) % 2) == q_parity
            qb, kb = rt // blk, ct // blk
            keep[d][blk] = ((qb % 2) == q_parity) & ((qb == kb + 1) if d == 0 else (kb == qb + 1))
            blk //= 2
        order = (rt >= ct) if d == 0 else (ct >= rt)
        keep[d][0] = ((rt // GLA_LEAF) == (ct // GLA_LEAF)) & order

    n_leaves = C // GLA_LEAF

    def per_leaf(rows_of_cum):
        return jnp.concatenate(rows_of_cum, axis=0)

    def expand(per_leaf_rows):
        return jnp.concatenate(
            [jnp.broadcast_to(per_leaf_rows[j:j + 1, :], (GLA_LEAF, DK_HEAD)) for j in range(n_leaves)],
            axis=0)

    def store_scores(n, d, acc, q_in, k_out, end):
        aq_ref[d, rows(n), 0:C] = acc.astype(BF16)
        aq_ref[d, rows(n), C:C + DK_HEAD] = q_in.astype(BF16)
        ko_ref[d, rows(n), :] = k_out.astype(BF16)
        dec_ref[d, rows(n), :] = jnp.broadcast_to(jnp.exp(end), (DK_HEAD, DK_HEAD)).T

    def block_level_scores(cum, q, k, d, blk, at_bnd_scale):
        x = (jnp.where(query_rows[d][blk], q, k) * at_bnd_scale).astype(BF16)
        return jnp.where(keep[d][blk], _dot_nt(x, x), 0.0)

    def boundary_rows(cum, d, blk):
        bnd = blk - 1 if d == 0 else blk
        return per_leaf([cum[bnd + (j * GLA_LEAF) // (2 * blk) * (2 * blk):
                             bnd + (j * GLA_LEAF) // (2 * blk) * (2 * blk) + 1, :]
                         for j in range(n_leaves)])

    def scores_exact(n, d):
        cum = cum_ref[d, rows(n), :]
        q = q_ref[rows(n), :]
        k = k_ref[rows(n), :]
        in_leaf_pos = row_id % GLA_LEAF
        acc = jnp.zeros((C, C), F32)
        for lag in range(GLA_LEAF):
            shift = lag if d == 0 else (C - lag) % C
            k_s = pltpu.roll(k, shift, 0) if lag else k
            cum_s = pltpu.roll(cum, shift, 0) if lag else cum
            paired = (in_leaf_pos >= lag) if d == 0 else (in_leaf_pos < GLA_LEAF - lag)
            term = q * k_s * jnp.exp(jnp.where(paired, cum - cum_s, 0.0))
            s = jnp.sum(jnp.where(paired, term, 0.0), axis=-1, keepdims=True)
            diagonal = (rt - ct == lag) if d == 0 else (ct - rt == lag)
            acc = acc + jnp.where(diagonal, s, 0.0)
        blk = C // 2
        while blk >= GLA_LEAF:
            w = jnp.exp(-jnp.abs(cum - expand(boundary_rows(cum, d, blk))))
            acc = acc + block_level_scores(cum, q, k, d, blk, w)
            blk //= 2
        end = cum[C - 1:C, :] if d == 0 else cum[0:1, :]
        store_scores(n, d, acc, q * jnp.exp(cum), k * jnp.exp(end - cum), end)

    def scores(n, d):
        cum = cum_ref[d, rows(n), :]
        mid = GLA_LEAF // 2 - 1 if d == 0 else GLA_LEAF // 2
        at_mid = per_leaf([cum[mid + j * GLA_LEAF:mid + j * GLA_LEAF + 1, :] for j in range(n_leaves)])
        e = cum - expand(at_mid)
        qe = q_ref[rows(n), :] * jnp.exp(e)
        ke = k_ref[rows(n), :] * jnp.exp(-e)
        acc = jnp.where(keep[d][0], _dot_nt(qe.astype(BF16), ke.astype(BF16)), 0.0)
        blk = C // 2
        while blk >= GLA_LEAF:
            through = expand(jnp.exp(-jnp.abs(at_mid - boundary_rows(cum, d, blk))))
            acc = acc + block_level_scores(cum, qe, ke, d, blk, through)
            blk //= 2
        end = cum[C - 1:C, :] if d == 0 else cum[0:1, :]
        store_scores(n, d, acc, qe * expand(jnp.exp(at_mid)), ke * expand(jnp.exp(end - at_mid)), end)

    def scores_chunk(n):
        for d in range(2):
            (scores_exact if exact else scores)(n, d)

    loop(scores_chunk)

    for s in range(seqs):
        for d in range(2):
            if has_s0:
                st_ref[2 * s + d] = s0_ref[s, d, 0]
            else:
                st_ref[2 * s + d] = jnp.zeros((DK_HEAD, DV_HEAD), F32)

    def scan(s, m, d):
        n = s * n_chunks + (m if d == 0 else n_chunks - 1 - m)
        v = v_ref[rows(n), :].astype(BF16)
        st = st_ref[2 * s + d]
        op_ref[d, rows(n), :] = _dot(aq_ref[d, rows(n), :],
                                     jnp.concatenate([v, st.astype(BF16)], axis=0))
        dec = dec_ref[d, rows(n), :]
        st_ref[2 * s + d] = (st * jnp.concatenate([dec] * (DV_HEAD // DK_HEAD), axis=1)
                             + _dot_tn(ko_ref[d, rows(n), :], v))

    def scan_step(m):
        for s in range(seqs):
            for d in range(2):
                scan(s, m, d)

    loop(scan_step, n_chunks)
    if emit_state:
        for s in range(seqs):
            for d in range(2):
                sout_ref[s, d, 0] = st_ref[2 * s + d]

    g = g_ref[...]

    def finish_chunk(n):
        o = op_ref[0, rows(n), :] + op_ref[1, rows(n), :]
        ms = jnp.mean(o * o, axis=-1, keepdims=True)
        o_ref[rows(n), :] = o * lax.rsqrt(ms + LN_EPS) * g

    loop(finish_chunk)


def _gla(proj, la, g, s0, *, n_seq, seq_len, row0, emit_state, exact):
    has_s0 = s0 is not None
    seqs = max(1, min(n_seq, GLA_UNROLL // max(1, seq_len // GLA_CHUNK)))
    assert n_seq % seqs == 0 and row0 % (seqs * seq_len) == 0
    blk_rows = seqs * seq_len
    blk0 = row0 // blk_rows
    kern = functools.partial(_gla_kernel, seq_len=seq_len, seqs=seqs, has_s0=has_s0,
                             emit_state=emit_state, exact=exact)
    in_specs = [
        pl.BlockSpec((blk_rows, DK_HEAD), lambda b, h: (blk0 + b, COL_Q // DK_HEAD + h)),
        pl.BlockSpec((blk_rows, DK_HEAD), lambda b, h: (blk0 + b, COL_K // DK_HEAD + h)),
        pl.BlockSpec((blk_rows, DV_HEAD), lambda b, h: (blk0 + b, COL_V // DV_HEAD + h)),
        pl.BlockSpec((blk_rows, DK_HEAD), lambda b, h: (blk0 + b, h)),
        pl.BlockSpec((blk_rows, DK_HEAD), lambda b, h: (blk0 + b, GLA_HEADS + h)),
        pl.BlockSpec((1, DV_HEAD), lambda b, h: (0, 0)),
    ]
    args = [proj, proj, proj, la, la, g]
    state_spec = pl.BlockSpec((seqs, 2, 1, DK_HEAD, DV_HEAD), lambda b, h: (b, 0, h, 0, 0))
    if has_s0:
        in_specs.append(state_spec)
        args.append(s0)
    out_specs = [pl.BlockSpec((blk_rows, DV_HEAD), lambda b, h: (b, h))]
    out_shape = [jax.ShapeDtypeStruct((n_seq * seq_len, GLA_DV), F32)]
    if emit_state:
        out_specs.append(state_spec)
        out_shape.append(jax.ShapeDtypeStruct((n_seq, 2, GLA_HEADS, DK_HEAD, DV_HEAD), F32))

    res = pl.pallas_call(
        kern,
        grid=(n_seq // seqs, GLA_HEADS),
        in_specs=in_specs,
        out_specs=out_specs,
        out_shape=out_shape,
        scratch_shapes=[pltpu.VMEM((2, blk_rows, DK_HEAD), F32),
                        pltpu.VMEM((2, blk_rows, GLA_CHUNK + DK_HEAD), BF16),
                        pltpu.VMEM((2, blk_rows, DK_HEAD), BF16),
                        pltpu.VMEM((2, blk_rows, DK_HEAD), F32),
                        pltpu.VMEM((2, blk_rows, DV_HEAD), F32),
                        pltpu.VMEM((2 * seqs, DK_HEAD, DV_HEAD), F32)],
        compiler_params=pltpu.CompilerParams(
            dimension_semantics=("arbitrary", "arbitrary"), vmem_limit_bytes=VMEM_LIMIT),
        name="gla%s_seq%d" % ("_exact" if exact else "", seq_len),
    )(*args)
    return res


def _fnet_kernel(f_ref, cl_ref, sl_ref, cg_ref, sg_ref, o_ref, uc_ref, us_ref, *, seq_len):
    cg = cg_ref[...]
    sg = sg_ref[...]
    for grp in range(FNET_GROUPS):
        lo = grp * FNET_GROUP_DIM
        u = f_ref[:, lo:lo + FNET_GROUP_DIM].astype(BF16)
        uc_ref[:, lo:lo + FNET_GROUP_DIM] = _dot(u, cg).astype(BF16)
        us_ref[:, lo:lo + FNET_GROUP_DIM] = _dot(u, sg).astype(BF16)
    mixed = _dot(cl_ref[...], uc_ref[...]) - _dot(sl_ref[...], us_ref[...])
    o_ref[...] = mixed * (1.0 / math.sqrt(seq_len * FNET_GROUP_DIM))


def _dft_mats(n):
    j = np.arange(n, dtype=np.int64)
    ang = (2.0 * np.pi / n) * ((j[:, None] * j[None, :]) % n).astype(np.float64)
    return (jnp.asarray(np.cos(ang), dtype=F32).astype(BF16),
            jnp.asarray(np.sin(ang), dtype=F32).astype(BF16))


def _fnet(proj, *, n_seq, seq_len, row0):
    blk0 = row0 // seq_len
    cl, sl = _dft_mats(seq_len)
    cg, sg = _dft_mats(FNET_GROUP_DIM)
    kern = functools.partial(_fnet_kernel, seq_len=seq_len)
    return pl.pallas_call(
        kern,
        grid=(n_seq,),
        in_specs=[pl.BlockSpec((seq_len, FNET_DIM), lambda b: (blk0 + b, COL_F // FNET_DIM)),
                  pl.BlockSpec((seq_len, seq_len), lambda b: (0, 0)),
                  pl.BlockSpec((seq_len, seq_len), lambda b: (0, 0)),
                  pl.BlockSpec((FNET_GROUP_DIM, FNET_GROUP_DIM), lambda b: (0, 0)),
                  pl.BlockSpec((FNET_GROUP_DIM, FNET_GROUP_DIM), lambda b: (0, 0))],
        out_specs=pl.BlockSpec((seq_len, FNET_DIM), lambda b: (b, 0)),
        out_shape=jax.ShapeDtypeStruct((n_seq * seq_len, FNET_DIM), F32),
        scratch_shapes=[pltpu.VMEM((seq_len, FNET_DIM), BF16),
                        pltpu.VMEM((seq_len, FNET_DIM), BF16)],
        compiler_params=pltpu.CompilerParams(
            dimension_semantics=("arbitrary",), vmem_limit_bytes=VMEM_LIMIT),
        name="fnet_seq%d" % seq_len,
    )(proj, cl, sl, cg, sg)


def _merge_kernel(xc_ref, xl_ref, pos_ref, mod_ref, oc_ref, ol_ref, mc_ref, ml_ref,
                  g_ref, ga_ref, gb_ref, wbg_ref, wbf_ref, wo_ref, l1g_ref, l1b_ref, wrh_ref, wrl_ref,
                  br_ref, x1_ref, h2_ref, ridx_ref, rw_ref, *, n_ctx_tiles, alpha):
    i = pl.program_id(0)

    tm = x1_ref.shape[0]
    groups = [slice(r0, r0 + TM_MERGE_GROUP) for r0 in range(0, tm, TM_MERGE_GROUP)]
    shape = (TM_MERGE_GROUP, LANES)
    lane_i = lax.broadcasted_iota(jnp.int32, shape, 1)
    lane = lane_i.astype(F32)

    def compute(x_of, o_ref, mx_ref):
        branch = []
        for rows in groups:
            a = (o_ref[rows, :] * g_ref[rows, :]).astype(BF16)
            branch.append((_dot(a, wbg_ref[...]), _dot(mx_ref[rows, :].astype(BF16), wbf_ref[...])))
        mix = []
        for rows, (gla_out, fnet_out) in zip(groups, branch):
            merged = ga_ref[rows, :] * gla_out + gb_ref[rows, :] * fnet_out
            mix.append(_dot(merged.astype(BF16), wo_ref[...]))
        logits = []
        for rows, mix_g in zip(groups, mix):
            y = alpha * x_of(rows) + mod_ref[0, 2:3, :] * mix_g
            x1 = _layer_norm(y) * l1g_ref[...] + l1b_ref[...]
            x1_ref[rows, :] = x1
            h2 = _layer_norm(x1) * (1.0 + mod_ref[0, 4:5, :]) + mod_ref[0, 3:4, :]
            _store_row_tiles(h2_ref, h2, rows.start)
            h_hi, h_lo = _split_bf16(h2, 2)
            logits.append(_dot(h_hi, wrh_ref[...]) + _dot(h_lo, wrh_ref[...])
                          + _dot(h_hi, wrl_ref[...]) + br_ref[...])
        idx_out = [jnp.zeros(shape, F32) for _ in groups]
        val_out = [jnp.zeros(shape, F32) for _ in groups]
        top0 = [None] * len(groups)
        denom = [None] * len(groups)
        for kk in range(TOP_K):
            for gi in range(len(groups)):
                m = jnp.max(logits[gi], axis=-1, keepdims=True)
                sel = jnp.min(jnp.where(logits[gi] == m, lane, float(LANES)), axis=-1, keepdims=True)
                if kk == 0:
                    top0[gi] = m
                    p = jnp.ones_like(m)
                    denom[gi] = p
                else:
                    p = jnp.exp(m - top0[gi])
                    denom[gi] = denom[gi] + p
                idx_out[gi] = jnp.where(lane_i == kk, sel, idx_out[gi])
                val_out[gi] = jnp.where(lane_i == kk, p, val_out[gi])
                logits[gi] = jnp.where(lane == sel, -jnp.inf, logits[gi])
        for gi, rows in enumerate(groups):
            ridx_ref[rows, :] = idx_out[gi].astype(jnp.int32)
            rw_ref[rows, :] = val_out[gi] / denom[gi]

    @pl.when(i < n_ctx_tiles)
    def _():
        compute(lambda rows: xc_ref[rows, :], oc_ref, mc_ref)

    @pl.when(i >= n_ctx_tiles)
    def _():
        compute(lambda rows: xl_ref[rows, :] + pos_ref[rows, :], ol_ref, ml_ref)


def _merge(x_ctx, x_lat, pos, mod, o_ctx, o_lat, mixed_ctx, mixed_lat, proj,
           wbg, wbf, wo, l1g, l1b, wr_hi, wr_lo, br, lat_len, alpha):
    t_ctx, t_lat = x_ctx.shape[0], x_lat.shape[0]
    t_all = t_ctx + t_lat
    tm = TM_MERGE
    n_ctx_tiles = t_ctx // tm
    kern = functools.partial(_merge_kernel, n_ctx_tiles=n_ctx_tiles, alpha=alpha)
    specs = _token_specs(tm, n_ctx_tiles, lat_len // tm, mod.shape[0] - 1)
    ctx_map, lat_map = _group_maps(n_ctx_tiles)

    def const(shape):
        return pl.BlockSpec(shape, lambda i: (0,) * len(shape))

    in_specs = specs + [
        pl.BlockSpec((tm, GLA_DV), ctx_map),
        pl.BlockSpec((tm, GLA_DV), lat_map),
        pl.BlockSpec((tm, FNET_DIM), ctx_map),
        pl.BlockSpec((tm, FNET_DIM), lat_map),
        pl.BlockSpec((tm, GLA_DV), lambda i: (i, COL_G // GLA_DV)),
        pl.BlockSpec((tm, D_MODEL), lambda i: (i, COL_GATE_A // D_MODEL)),
        pl.BlockSpec((tm, D_MODEL), lambda i: (i, COL_GATE_B // D_MODEL)),
        const((GLA_DV, D_MODEL)), const((FNET_DIM, D_MODEL)), const((D_MODEL, D_MODEL)),
        const((1, D_MODEL)), const((1, D_MODEL)),
        const((D_MODEL, LANES)), const((D_MODEL, LANES)), const((1, LANES)),
    ]
    out_specs = [pl.BlockSpec((tm, D_MODEL), lambda i: (i, 0)),
                 pl.BlockSpec((tm * ROW_TILES, LANES), lambda i: (i, 0)),
                 pl.BlockSpec((tm, LANES), lambda i: (i, 0)),
                 pl.BlockSpec((tm, LANES), lambda i: (i, 0))]
    out_shape = [jax.ShapeDtypeStruct((t_all, D_MODEL), F32),
                 jax.ShapeDtypeStruct((t_all * ROW_TILES, LANES), ROW_DTYPE),
                 jax.ShapeDtypeStruct((t_all, LANES), jnp.int32),
                 jax.ShapeDtypeStruct((t_all, LANES), F32)]
    return pl.pallas_call(
        kern,
        grid=(t_all // tm,),
        in_specs=in_specs,
        out_specs=out_specs,
        out_shape=out_shape,
        compiler_params=pltpu.CompilerParams(
            dimension_semantics=("arbitrary",), vmem_limit_bytes=VMEM_LIMIT),
        name="merge_ln1_router",
    )(x_ctx, x_lat, pos, mod, o_ctx, o_lat, mixed_ctx, mixed_lat, proj, proj, proj,
      wbg, wbf, wo, l1g, l1b, wr_hi, wr_lo, br)


def _sc_mesh():
    return plsc.VectorSubcoreMesh(core_axis_name="c", subcore_axis_name="s")


def _sc_worker_id():
    return lax.axis_index("s") * SC_CORES + lax.axis_index("c")


def _sc_scatter_rows(src, idx, n_out):
    n_src = src.shape[0]
    w = SC_WINDOW
    n_chunks = n_src // (SC_WORKERS * w)
    copies = idx.shape[0]
    assert n_chunks % 2 == 0 and idx.shape == (copies, SC_WORKERS, n_chunks, w)

    @functools.partial(
        pl.kernel, mesh=_sc_mesh(),
        out_type=jax.ShapeDtypeStruct((n_out, ROW_TILES, LANES), ROW_DTYPE),
        scratch_types=[pltpu.VMEM((copies * n_chunks, w), jnp.int32),
                       pltpu.VMEM((2, w, ROW_TILES, LANES), ROW_DTYPE),
                       pltpu.SemaphoreType.DMA((2,)),
                       pltpu.SemaphoreType.DMA((2,))],
        name="moe_dispatch_scatter")
    def k(src_hbm, idx_hbm, out_hbm, idx_v, rows_v, rsem, wsem):
        wid = _sc_worker_id()
        base = wid * (n_chunks * w)
        for kk in range(copies):
            pltpu.sync_copy(idx_hbm.at[kk, wid], idx_v.at[pl.ds(kk * n_chunks, n_chunks)])

        def read(j, slot):
            return pltpu.make_async_copy(src_hbm.at[pl.ds(base + j * w, w)], rows_v.at[slot],
                                         rsem.at[slot])

        def scatter(j, kk, slot):
            return pltpu.make_async_copy(rows_v.at[slot], out_hbm.at[idx_v.at[kk * n_chunks + j]],
                                         wsem.at[slot])

        read(0, 0).start()

        @pl.loop(0, n_chunks, step=2)
        def _(jj):
            read(jj, 0).wait()

            @pl.when(jj > 0)
            def _():
                for kk in range(copies):
                    scatter(jj - 1, kk, 1).wait()

            read(jj + 1, 1).start()
            for kk in range(copies):
                scatter(jj, kk, 0).start()
            read(jj + 1, 1).wait()
            for kk in range(copies):
                scatter(jj, kk, 0).wait()

            @pl.when(jj + 2 < n_chunks)
            def _():
                read(jj + 2, 0).start()

            for kk in range(copies):
                scatter(jj + 1, kk, 1).start()

        for kk in range(copies):
            scatter(n_chunks - 1, kk, 1).wait()

    return k(src, idx)


def _sc_gather_rows(table, idx):
    _, n_chunks, w = idx.shape
    assert n_chunks % 2 == 0 and idx.shape[0] == SC_WORKERS and w == SC_WINDOW
    n_out = SC_WORKERS * n_chunks * w

    @functools.partial(
        pl.kernel, mesh=_sc_mesh(),
        out_type=jax.ShapeDtypeStruct((n_out, ROW_TILES, LANES), ROW_DTYPE),
        scratch_types=[pltpu.VMEM((n_chunks, w), jnp.int32),
                       pltpu.VMEM((2, w, ROW_TILES, LANES), ROW_DTYPE),
                       pltpu.SemaphoreType.DMA((2,)),
                       pltpu.SemaphoreType.DMA((2,))],
        name="moe_combine_gather")
    def k(table_hbm, idx_hbm, out_hbm, idx_v, rows_v, gsem, wsem):
        wid = _sc_worker_id()
        base = wid * (n_chunks * w)
        pltpu.sync_copy(idx_hbm.at[wid], idx_v)

        def gather(j, slot):
            return pltpu.make_async_copy(table_hbm.at[idx_v.at[j]], rows_v.at[slot], gsem.at[slot])

        def write(j, slot):
            return pltpu.make_async_copy(rows_v.at[slot], out_hbm.at[pl.ds(base + j * w, w)],
                                         wsem.at[slot])

        gather(0, 0).start()

        @pl.loop(0, n_chunks, step=2)
        def _(jj):
            gather(jj, 0).wait()

            @pl.when(jj > 0)
            def _():
                write(jj - 1, 1).wait()

            gather(jj + 1, 1).start()
            write(jj, 0).start()
            gather(jj + 1, 1).wait()
            write(jj, 0).wait()

            @pl.when(jj + 2 < n_chunks)
            def _():
                gather(jj + 2, 0).start()

            write(jj + 1, 1).start()

        write(n_chunks - 1, 1).wait()

    return k(table, idx)


def _moe_kernel(be_ref, nu_ref, nv_ref, slot_ref, nxt_ref, x_ref, wgu_hbm, bgu_ref, wd_hbm, bd_ref,
                o_ref, wgu_st, wd_st, wgu_bf, wd_bf, xb_ref, sem):
    b = pl.program_id(0)
    e = be_ref[b]
    prev = be_ref[jnp.maximum(b - 1, 0)]
    active = b < nu_ref[0]
    changed = (b == 0) | (e != prev)

    def weight_copies(expert, s):
        return (pltpu.make_async_copy(wgu_hbm.at[expert], wgu_st.at[s], sem.at[0, s]),
                pltpu.make_async_copy(wd_hbm.at[expert], wd_st.at[s], sem.at[1, s]))

    @pl.when(active & changed)
    def _():
        s = slot_ref[b]

        @pl.when(b == 0)
        def _():
            for cp in weight_copies(e, s):
                cp.start()

        for cp in weight_copies(e, s):
            cp.wait()
        wgu_bf[...] = wgu_st[s].astype(BF16)
        wd_bf[...] = wd_st[s].astype(BF16)
        nxt = nxt_ref[b]

        @pl.when(nxt >= 0)
        def _():
            for cp in weight_copies(nxt, 1 - s):
                cp.start()

    n_valid = nv_ref[b]

    def expert_mlp(n_rows):
        valid = lax.broadcasted_iota(jnp.int32, (n_rows, LANES), 0) < n_valid
        for j in range(ROW_TILES):
            for half, xj in enumerate(_load_row_tile(x_ref, j, n_rows)):
                c0 = half * HALF_MODEL + j * LANES
                xb_ref[0:n_rows, c0:c0 + LANES] = jnp.where(valid, xj, 0.0).astype(BF16)
        gu = _dot(xb_ref[0:n_rows, :], wgu_bf[...]) + bgu_ref[0]
        gate = jnp.minimum(gu[:, :D_EXPERT], SWIGLU_LIMIT)
        up = jnp.clip(gu[:, D_EXPERT:], -SWIGLU_LIMIT, SWIGLU_LIMIT)
        glu = gate * _sigmoid(gate * SWIGLU_ALPHA)
        act = ((up + 1.0) * glu).astype(BF16)
        _store_row_tiles(o_ref, _dot(act, wd_bf[...]) + bd_ref[0])

    for n_rows in range(MOE_STEP, MOE_ROWS + 1, MOE_STEP):
        @pl.when(active & (n_valid > n_rows - MOE_STEP) & (n_valid <= n_rows))
        def _(n_rows=n_rows):
            expert_mlp(n_rows)


def _moe(tables, xs, w_gate_up, b_gate_up, w_down, b_down):
    p_rows = xs.shape[0] // ROW_TILES
    n_blocks = p_rows // MOE_ROWS

    def blk(b, be, nu, *_):
        return jnp.minimum(b, nu[0] - 1)

    def expert(b, be, nu, *_):
        return (be[blk(b, be, nu)], 0, 0)

    def rows(b, be, nu, *_):
        return (blk(b, be, nu), 0)

    grid_spec = pltpu.PrefetchScalarGridSpec(
        num_scalar_prefetch=len(tables),
        grid=(n_blocks,),
        in_specs=[
            pl.BlockSpec((MOE_ROWS * ROW_TILES, LANES), rows),
            pl.BlockSpec(memory_space=pl.ANY),
            pl.BlockSpec((1, 1, 2 * D_EXPERT), expert),
            pl.BlockSpec(memory_space=pl.ANY),
            pl.BlockSpec((1, 1, D_MODEL), expert),
        ],
        out_specs=pl.BlockSpec((MOE_ROWS * ROW_TILES, LANES), rows),
        scratch_shapes=[pltpu.VMEM((2, D_MODEL, 2 * D_EXPERT), F32),
                        pltpu.VMEM((2, D_EXPERT, D_MODEL), F32),
                        pltpu.VMEM((D_MODEL, 2 * D_EXPERT), BF16),
                        pltpu.VMEM((D_EXPERT, D_MODEL), BF16),
                        pltpu.VMEM((MOE_ROWS, D_MODEL), BF16),
                        pltpu.SemaphoreType.DMA((2, 2))],
    )
    return pl.pallas_call(
        _moe_kernel,
        grid_spec=grid_spec,
        out_shape=jax.ShapeDtypeStruct((p_rows * ROW_TILES, LANES), ROW_DTYPE),
        compiler_params=pltpu.CompilerParams(
            dimension_semantics=("arbitrary",), vmem_limit_bytes=VMEM_LIMIT),
        name="moe_grouped_mlp",
    )(*tables, xs, w_gate_up, b_gate_up.reshape(N_EXPERTS, 1, 2 * D_EXPERT), w_down,
      b_down.reshape(N_EXPERTS, 1, D_MODEL))


def _combine_kernel(x1_ref, y0_ref, y1_ref, y2_ref, y3_ref, rw_ref, mod_ref, g_ref, b_ref, o_ref,
                    *, alpha):
    rw = rw_ref[...]
    y_refs = (y0_ref, y1_ref, y2_ref, y3_ref)
    pieces = [None] * (2 * ROW_TILES)
    for j in range(ROW_TILES):
        for kk in range(TOP_K):
            for half, yj in enumerate(_load_row_tile(y_refs[kk], j, rw.shape[0])):
                term = rw[:, kk:kk + 1] * yj
                slot = half * ROW_TILES + j
                pieces[slot] = term if kk == 0 else pieces[slot] + term
    ff = jnp.concatenate(pieces, axis=-1)
    y = alpha * x1_ref[...] + mod_ref[0, 5:6, :] * ff
    o_ref[...] = _layer_norm(y) * g_ref[...] + b_ref[...]


def _combine(x1, yg, rw, mod, l2g, l2b, *, row0, n_rows, mod_map, alpha):
    tm = TM_MIX
    t0 = row0 // tm
    tiles = n_rows // tm
    kern = functools.partial(_combine_kernel, alpha=alpha)

    def y_spec(kk):
        return pl.BlockSpec((tm * ROW_TILES, LANES), lambda i: (kk * tiles + i, 0))

    return pl.pallas_call(
        kern,
        grid=(n_rows // tm,),
        in_specs=[pl.BlockSpec((tm, D_MODEL), lambda i: (t0 + i, 0))]
        + [y_spec(kk) for kk in range(TOP_K)]
        + [pl.BlockSpec((tm, LANES), lambda i: (t0 + i, 0)),
           pl.BlockSpec((1, N_MOD, D_MODEL), mod_map),
           pl.BlockSpec((1, D_MODEL), lambda i: (0, 0)),
           pl.BlockSpec((1, D_MODEL), lambda i: (0, 0))],
        out_specs=pl.BlockSpec((tm, D_MODEL), lambda i: (i, 0)),
        out_shape=jax.ShapeDtypeStruct((n_rows, D_MODEL), F32),
        compiler_params=pltpu.CompilerParams(
            dimension_semantics=("arbitrary",), vmem_limit_bytes=VMEM_LIMIT),
        name="combine_ln2",
    )(x1, yg, yg, yg, yg, rw, mod, l2g, l2b)


def _route_kernel(ridx_ref, dest_ref, cnt_ref, run_ref, bst_ref):
    phase = pl.program_id(0)
    i = pl.program_id(1)
    tm = ridx_ref.shape[0]
    ridx = ridx_ref[...]
    lane = lax.broadcasted_iota(jnp.int32, (tm, LANES), 1)
    hits = [ridx[:, kk:kk + 1] == lane for kk in range(TOP_K)]
    chosen = jnp.where(hits[0], 1.0, 0.0)
    for kk in range(1, TOP_K):
        chosen = chosen + jnp.where(hits[kk], 1.0, 0.0)
    colsum = jnp.sum(chosen, axis=0, keepdims=True)

    @pl.when((phase == 0) & (i == 0))
    def _():
        run_ref[...] = jnp.zeros_like(run_ref)

    @pl.when(phase == 0)
    def _():
        run_ref[...] = run_ref[...] + colsum

    @pl.when((phase == 1) & (i == 0))
    def _():
        counts = run_ref[...]
        cnt_ref[...] = counts
        blocks = jnp.floor((counts + (MOE_ROWS - 1.0)) * (1.0 / MOE_ROWS))
        r = lax.broadcasted_iota(jnp.int32, (LANES, LANES), 0)
        c = lax.broadcasted_iota(jnp.int32, (LANES, LANES), 1)
        before = jnp.dot(blocks, (r < c).astype(F32), precision=HIGHEST, preferred_element_type=F32)
        bst_ref[...] = before * float(MOE_ROWS)
        run_ref[...] = jnp.zeros_like(run_ref)

    @pl.when(phase == 1)
    def _():
        rt = lax.broadcasted_iota(jnp.int32, (tm, tm), 0)
        ct = lax.broadcasted_iota(jnp.int32, (tm, tm), 1)
        earlier = _dot((ct < rt).astype(BF16), chosen.astype(BF16))
        row_of = bst_ref[0:1, :] + run_ref[0:1, :] + earlier
        out = jnp.take_along_axis(row_of, ridx, axis=1)
        dest_ref[...] = out.T[0:SUBLANES, :].astype(jnp.int32)
        run_ref[...] = run_ref[...] + colsum


def _route(ridx):
    t_all = ridx.shape[0]
    tm = TM_ROUTE
    return pl.pallas_call(
        _route_kernel,
        grid=(2, t_all // tm),
        in_specs=[pl.BlockSpec((tm, LANES), lambda p, i: (i, 0))],
        out_specs=[pl.BlockSpec((SUBLANES, tm), lambda p, i: (0, i * p)),
                   pl.BlockSpec((SUBLANES, LANES), lambda p, i: (0, 0))],
        out_shape=[jax.ShapeDtypeStruct((SUBLANES, t_all), jnp.int32),
                   jax.ShapeDtypeStruct((SUBLANES, LANES), F32)],
        scratch_shapes=[pltpu.VMEM((SUBLANES, LANES), F32),
                        pltpu.VMEM((SUBLANES, LANES), F32)],
        compiler_params=pltpu.CompilerParams(
            dimension_semantics=("arbitrary", "arbitrary"), vmem_limit_bytes=VMEM_LIMIT),
        name="moe_route",
    )(ridx)


def _routing_tables(counts, n_blocks):
    experts = jnp.arange(N_EXPERTS, dtype=jnp.int32)
    blocks_per = (counts + MOE_ROWS - 1) // MOE_ROWS
    bends = jnp.cumsum(blocks_per)
    bstarts = bends - blocks_per
    blocks = jnp.arange(n_blocks, dtype=jnp.int32)
    block_expert = jnp.minimum(
        jnp.sum((bends[None, :] <= blocks[:, None]).astype(jnp.int32), axis=1), N_EXPERTS - 1)
    n_used = bends[-1:].astype(jnp.int32)
    owner = block_expert[:, None] == experts[None, :]

    def per_block(table):
        return jnp.sum(jnp.where(owner, table[None, :], 0), axis=1)

    n_valid = jnp.clip(per_block(counts) - (blocks - per_block(bstarts)) * MOE_ROWS,
                       0, MOE_ROWS).astype(jnp.int32)
    present = blocks_per > 0
    ordinal = jnp.cumsum(present.astype(jnp.int32)) - 1
    later = lax.cummin(jnp.where(present, experts, N_EXPERTS), reverse=True)
    succ = jnp.concatenate([later[1:], jnp.full((1,), N_EXPERTS, jnp.int32)])
    succ = jnp.where(succ >= N_EXPERTS, -1, succ)
    stage_slot = (per_block(ordinal) % 2).astype(jnp.int32)
    next_expert = per_block(succ).astype(jnp.int32)
    return (block_expert.astype(jnp.int32), n_used, n_valid, stage_slot, next_expert)


def _pos_embed_2d(n_tokens):
    rows = n_tokens // GRID_W
    r = np.repeat(np.arange(rows), GRID_W).astype(np.float32)
    col = np.tile(np.arange(GRID_W), rows).astype(np.float32)
    quarter = D_MODEL // 4
    omega = (np.float32(1.0)
             / np.power(np.float32(10000.0), np.arange(quarter, dtype=np.float32) / np.float32(quarter)))
    er = (r[:, None] * omega).astype(np.float64)
    ec = (col[:, None] * omega).astype(np.float64)
    table = np.concatenate([np.sin(er), np.cos(er), np.sin(ec), np.cos(ec)], axis=-1)
    return jnp.asarray(table, dtype=F32)


def _split_w_in(w):
    wt = w.T
    o_r = 2 * GLA_DK + 2 * GLA_DV
    o_f = o_r + DECAY_RANK
    o_gate = o_f + FNET_DIM
    row_scale = jnp.where(jnp.arange(o_r) < GLA_DK, DK_HEAD ** -0.5, 1.0).astype(w.dtype)
    w_main = (wt[:o_r] * row_scale[:, None]).astype(BF16)
    pad = jnp.zeros((LANES - DECAY_RANK, w.shape[0]), BF16)
    w_tail = jnp.concatenate([wt[o_gate:].astype(BF16), wt[o_f:o_gate].astype(BF16),
                              wt[o_r:o_f].astype(BF16), pad], axis=0)
    return w_main, w_tail


def kernel(x_prompt, x_sample, state_gla, c, c_ctx, w_ada, b_ada, w_in, w_dec_fwd, b_dec_fwd,
           w_dec_bwd, b_dec_bwd, gla_norm_g, w_br_gla, w_br_fnet, w_out, ln1_g, ln1_b, w_router,
           b_router, w_gate_up, b_gate_up, w_down, b_down, ln2_g, ln2_b):
    n_req, ctx_len, _ = x_prompt.shape
    n_lat, lat_len, _ = x_sample.shape
    depth = w_in.shape[0]
    alpha = (2.0 * depth) ** 0.25
    t_ctx = n_req * ctx_len
    t_lat = n_lat * lat_len
    t_all = t_ctx + t_lat
    tile_rows = max(TM_PROJ, TM_MERGE, TM_MIX)
    assert t_ctx % tile_rows == 0 and lat_len % tile_rows == 0
    assert ctx_len % GLA_CHUNK == 0 and lat_len % GLA_CHUNK == 0 and t_all % TM_ROUTE == 0
    assert t_ctx % (SC_WORKERS * SC_WINDOW * 2) == 0 and t_lat % (SC_WORKERS * SC_WINDOW * 2) == 0
    assert MOE_ROWS & (MOE_ROWS - 1) == 0 and (t_all * TOP_K) % MOE_ROWS == 0

    x_ctx = x_prompt.reshape(t_ctx, D_MODEL)
    x_lat = x_sample.reshape(t_lat, D_MODEL)
    pos = _pos_embed_2d(lat_len)
    zero_pos = jnp.zeros_like(pos)

    cond_rows = -(-(n_lat + 1) // SUBLANES) * SUBLANES
    cond = jnp.zeros((cond_rows, D_MODEL), F32).at[:n_lat].set(c).at[cond_rows - 1].set(c_ctx)

    n_moe_blocks = (t_all * TOP_K) // MOE_ROWS + N_EXPERTS
    tok_chunks = t_all // (SC_WORKERS * SC_WINDOW)
    states = []
    for l in range(depth):
        mod = _ada(cond, w_ada[l], b_ada[l]).reshape(cond_rows, N_MOD, D_MODEL)
        layer_pos = pos if l == 0 else zero_pos
        w_dec = jnp.zeros((2, LANES, GLA_DK), F32)
        w_dec = w_dec.at[0, :DECAY_RANK].set(w_dec_fwd[l]).at[1, :DECAY_RANK].set(w_dec_bwd[l])
        b_dec = jnp.stack([b_dec_fwd[l], b_dec_bwd[l]]).reshape(2, 1, GLA_DK)
        proj, la, la_min = _inproj(x_ctx, x_lat, layer_pos, mod, *_split_w_in(w_in[l]), w_dec, b_dec,
                                   lat_len)

        norm_g = gla_norm_g[l].reshape(1, DV_HEAD)
        s0_lat = state_gla[:, l]

        def gla_both(exact):
            o_c, s_c = _gla(proj, la, norm_g, None, n_seq=n_req, seq_len=ctx_len, row0=0,
                            emit_state=True, exact=exact)
            (o_l,) = _gla(proj, la, norm_g, s0_lat, n_seq=n_lat, seq_len=lat_len, row0=t_ctx,
                          emit_state=False, exact=exact)
            return o_c, s_c, o_l

        fast_ok = jnp.min(la_min) >= -GLA_FAST_MAX_STEP_DECAY
        o_ctx, s_new, o_lat = lax.cond(fast_ok, lambda: gla_both(False), lambda: gla_both(True))
        states.append(s_new)

        mixed_ctx = _fnet(proj, n_seq=n_req, seq_len=ctx_len, row0=0)
        mixed_lat = _fnet(proj, n_seq=n_lat, seq_len=lat_len, row0=t_ctx)

        wr = jnp.zeros((D_MODEL, LANES), F32).at[:, :N_EXPERTS].set(w_router[l])
        br = jnp.full((1, LANES), -1e30, F32).at[0, :N_EXPERTS].set(b_router[l])
        wr_hi = wr.astype(BF16)
        wr_lo = (wr - wr_hi.astype(F32)).astype(BF16)
        x1, h2, ridx, rw = _merge(
            x_ctx, x_lat, layer_pos, mod, o_ctx, o_lat, mixed_ctx, mixed_lat, proj,
            w_br_gla[l].astype(BF16), w_br_fnet[l].astype(BF16), (0.5 * w_out[l]).astype(BF16),
            ln1_g[l].reshape(1, D_MODEL), ln1_b[l].reshape(1, D_MODEL), wr_hi, wr_lo, br, lat_len,
            alpha)

        dest, counts = _route(ridx)
        moe_tables = _routing_tables(counts[0, :N_EXPERTS].astype(jnp.int32), n_moe_blocks)
        dest = dest[:TOP_K]
        scatter_idx = dest.reshape(TOP_K, SC_WORKERS, tok_chunks, SC_WINDOW)
        p_rows = n_moe_blocks * MOE_ROWS
        xs = _sc_scatter_rows(h2.reshape(t_all, ROW_TILES, LANES), scatter_idx, p_rows)
        yb = _moe(moe_tables, xs.reshape(p_rows * ROW_TILES, LANES),
                  w_gate_up[l], b_gate_up[l], w_down[l], b_down[l])
        yb = yb.reshape(p_rows, ROW_TILES, LANES)

        def gathered(row0, n_rows):
            idx = dest[:, row0:row0 + n_rows].reshape(SC_WORKERS, -1, SC_WINDOW)
            return _sc_gather_rows(yb, idx).reshape(TOP_K * n_rows * ROW_TILES, LANES)

        l2g = ln2_g[l].reshape(1, D_MODEL)
        l2b = ln2_b[l].reshape(1, D_MODEL)
        tiles_per_seq = lat_len // TM_MIX
        yg_ctx = gathered(0, t_ctx)
        yg_lat = gathered(t_ctx, t_lat)
        x_ctx = _combine(x1, yg_ctx, rw, mod, l2g, l2b, row0=0, n_rows=t_ctx,
                         mod_map=lambda i: (cond_rows - 1, 0, 0), alpha=alpha)
        x_lat = _combine(x1, yg_lat, rw, mod, l2g, l2b, row0=t_ctx, n_rows=t_lat,
                         mod_map=lambda i: (i // tiles_per_seq, 0, 0), alpha=alpha)

    y_prompt = x_ctx.reshape(x_prompt.shape)
    y_sample = x_lat.reshape(x_sample.shape)
    new_state = jnp.stack(states, axis=1).astype(x_prompt.dtype)
    return (y_prompt, y_sample, new_state)
```

```python
import functools
import math

import numpy as np
import jax
import jax.numpy as jnp
from jax import lax
from jax.experimental import pallas as pl
from jax.experimental.pallas import tpu as pltpu
from jax.experimental.pallas import tpu_sc as plsc

F32 = jnp.float32
BF16 = jnp.bfloat16

D_MODEL = 1024
GRID_W = 64
GLA_HEADS = 4
DK_HEAD = 128
DV_HEAD = 256
GLA_DK = GLA_HEADS * DK_HEAD
GLA_DV = GLA_HEADS * DV_HEAD
DECAY_RANK = 16
GATE_NORMALIZER = 16.0
FNET_GROUPS = 4
FNET_GROUP_DIM = 128
FNET_DIM = FNET_GROUPS * FNET_GROUP_DIM
N_EXPERTS = 32
TOP_K = 4
D_EXPERT = 1024
SWIGLU_LIMIT = 7.0
SWIGLU_ALPHA = 1.702
LN_EPS = 1e-6
N_MOD = 6

LANES = 128
SUBLANES = 8
HALF_MODEL = D_MODEL // 2
ROW_TILES = HALF_MODEL // LANES
ROW_DTYPE = jnp.uint32
COL_Q = 0
COL_K = GLA_DK
COL_V = 2 * GLA_DK
COL_G = COL_V + GLA_DV
COL_GATE_A = COL_G + GLA_DV
COL_GATE_B = COL_GATE_A + D_MODEL
COL_F = COL_GATE_B + D_MODEL
COL_R = COL_F + FNET_DIM
PROJ_COLS = COL_R + LANES

GLA_CHUNK = 128
GLA_LEAF = 16
GLA_UNROLL = 8
FNET_ROWS = 1024
assert GLA_CHUNK == DK_HEAD
GLA_FAST_MAX_STEP_DECAY = 8.0
TM_PROJ = 512
PROJ_GROUP = 256
TM_MIX = 256
TM_MERGE = 512
TM_MERGE_GROUP = 256
TM_ROUTE = 1024
MOE_ROWS = 512
MOE_STEP = 128
VMEM_LIMIT = 56 * 1024 * 1024

SC_CORES = 2
SC_SUBCORES = 16
SC_WORKERS = SC_CORES * SC_SUBCORES
SC_WINDOW = 64

HIGHEST = lax.Precision.HIGHEST


def _layer_norm(x):
    mu = jnp.mean(x, axis=-1, keepdims=True)
    xc = x - mu
    var = jnp.mean(xc * xc, axis=-1, keepdims=True)
    return xc * lax.rsqrt(var + LN_EPS)


def _sigmoid(x):
    return 0.5 * jnp.tanh(0.5 * x) + 0.5


def _log_sigmoid(z):
    return jnp.minimum(z, 0.0) - jnp.log(1.0 + jnp.exp(-jnp.abs(z)))


def _dot(a, b):
    return jnp.dot(a, b, preferred_element_type=F32)


def _split_bf16(x, terms):
    parts = []
    for _ in range(terms):
        p = x.astype(BF16)
        parts.append(p)
        x = x - p.astype(F32)
    return parts


def _dot_nt(a, b):
    return lax.dot_general(a, b, (((1,), (1,)), ((), ())), preferred_element_type=F32)


def _dot_tn(a, b):
    return lax.dot_general(a, b, (((0,), (0,)), ((), ())), preferred_element_type=F32)


def _row_tile_slice(j, n_rows, first_row=0):
    return pl.ds(first_row * ROW_TILES + j, n_rows, stride=ROW_TILES)


def _store_row_tiles(ref, val, first_row=0):
    for j in range(ROW_TILES):
        lo = val[:, j * LANES:(j + 1) * LANES]
        hi = val[:, HALF_MODEL + j * LANES:HALF_MODEL + (j + 1) * LANES]
        ref[_row_tile_slice(j, val.shape[0], first_row), :] = pltpu.pack_elementwise(
            [lo, hi], packed_dtype=BF16)


def _load_row_tile(ref, j, n_rows, first_row=0):
    words = ref[_row_tile_slice(j, n_rows, first_row), :]
    return tuple(pltpu.unpack_elementwise(words, index=half, packed_dtype=BF16, unpacked_dtype=F32)
                 for half in range(2))


def _ada_kernel(c_ref, w_ref, b_ref, o_ref):
    c = c_ref[...]
    s = c * _sigmoid(c)
    o_ref[...] = _dot(s.astype(BF16), w_ref[...].astype(BF16)) + b_ref[...]


def _ada(cond, w_ada, b_ada):
    rows = cond.shape[0]
    n = w_ada.shape[1]
    tn = 1536
    return pl.pallas_call(
        _ada_kernel,
        grid=(n // tn,),
        in_specs=[pl.BlockSpec((rows, D_MODEL), lambda j: (0, 0)),
                  pl.BlockSpec((D_MODEL, tn), lambda j: (0, j)),
                  pl.BlockSpec((1, tn), lambda j: (0, j))],
        out_specs=pl.BlockSpec((rows, tn), lambda j: (0, j)),
        out_shape=jax.ShapeDtypeStruct((rows, n), F32),
        compiler_params=pltpu.CompilerParams(vmem_limit_bytes=VMEM_LIMIT),
        name="ada_mod",
    )(cond, w_ada, b_ada.reshape(1, n))


def _group_maps(n_ctx_tiles):
    def ctx_map(i, *_):
        return (jnp.minimum(i, n_ctx_tiles - 1), 0)

    def lat_map(i, *_):
        return (jnp.maximum(i - n_ctx_tiles, 0), 0)

    return ctx_map, lat_map


def _token_specs(tm, n_ctx_tiles, tiles_per_latent_seq, ctx_mod_row):
    ctx_map, lat_map = _group_maps(n_ctx_tiles)

    def pos_map(i, *_):
        return (jnp.maximum(i - n_ctx_tiles, 0) % tiles_per_latent_seq, 0)

    def mod_map(i, *_):
        return (jnp.where(i < n_ctx_tiles, ctx_mod_row,
                          jnp.maximum(i - n_ctx_tiles, 0) // tiles_per_latent_seq), 0, 0)

    return [pl.BlockSpec((tm, D_MODEL), ctx_map),
            pl.BlockSpec((tm, D_MODEL), lat_map),
            pl.BlockSpec((tm, D_MODEL), pos_map),
            pl.BlockSpec((1, N_MOD, D_MODEL), mod_map)]


def _inproj_kernel(xc_ref, xl_ref, pos_ref, mod_ref, wm_ref, wt_ref, wdh_ref, wdl_ref, bd_ref,
                   o_ref, la_ref, lamin_ref, *, n_ctx_tiles):
    tail = lambda lo, hi: wt_ref[lo - COL_GATE_A:hi - COL_GATE_A, :]
    i = pl.program_id(0)

    def project(x_of):
        groups = [slice(r0, r0 + PROJ_GROUP) for r0 in range(0, TM_PROJ, PROJ_GROUP)]
        hs = [(_layer_norm(x_of(rows)) * (1.0 + mod_ref[0, 1:2, :]) + mod_ref[0, 0:1, :]).astype(BF16)
              for rows in groups]
        rs = [_split_bf16(_dot_nt(h, tail(COL_R, PROJ_COLS)), 2) for h in hs]
        steepest = None
        for rows, (r_hi, r_lo) in zip(groups, rs):
            for d in range(2):
                z = (_dot(r_hi, wdh_ref[d]) + _dot(r_lo, wdh_ref[d]) + _dot(r_hi, wdl_ref[d])
                     + bd_ref[d])
                la = _log_sigmoid(z) * (1.0 / GATE_NORMALIZER)
                la_ref[rows, d * GLA_DK:(d + 1) * GLA_DK] = la
                low = jnp.min(la, axis=0, keepdims=True)
                for c0 in range(0, GLA_DK, LANES):
                    piece = low[:, c0:c0 + LANES]
                    steepest = piece if steepest is None else jnp.minimum(steepest, piece)
        lamin_ref[...] = jnp.broadcast_to(steepest, (SUBLANES, LANES))
        for rows, h in zip(groups, hs):
            hg = 0.5 * _dot_nt(h, wm_ref[COL_G:COL_GATE_A, :])
            o_ref[rows, COL_G:COL_GATE_A] = hg * (jnp.tanh(hg) + 1.0)
        for rows, h in zip(groups, hs):
            gates = _dot_nt(h, tail(COL_GATE_A, COL_F))
            o_ref[rows, COL_GATE_A:COL_F] = jnp.tanh(0.5 * gates) + 1.0
        for rows, h in zip(groups, hs):
            o_ref[rows, :COL_G] = _dot_nt(h, wm_ref[:COL_G, :])
        for rows, h in zip(groups, hs):
            o_ref[rows, COL_F:COL_R] = _dot_nt(h, tail(COL_F, COL_R))

    @pl.when(i < n_ctx_tiles)
    def _():
        project(lambda rows: xc_ref[rows, :])

    @pl.when(i >= n_ctx_tiles)
    def _():
        project(lambda rows: xl_ref[rows, :] + pos_ref[rows, :])


def _inproj(x_ctx, x_lat, pos, mod, w_main, w_tail, w_dec, b_dec, lat_len):
    t_ctx, t_lat = x_ctx.shape[0], x_lat.shape[0]
    t_all = t_ctx + t_lat
    n_ctx_tiles = t_ctx // TM_PROJ
    kern = functools.partial(_inproj_kernel, n_ctx_tiles=n_ctx_tiles)
    specs = _token_specs(TM_PROJ, n_ctx_tiles, lat_len // TM_PROJ, mod.shape[0] - 1)
    w_dec_hi = w_dec.astype(BF16)
    w_dec_lo = (w_dec - w_dec_hi.astype(F32)).astype(BF16)

    def const(shape):
        return pl.BlockSpec(shape, lambda i: (0,) * len(shape))

    return pl.pallas_call(
        kern,
        grid=(t_all // TM_PROJ,),
        in_specs=specs + [pl.BlockSpec(w_main.shape, lambda i: (0, 0), pipeline_mode=pl.Buffered(1)),
                          pl.BlockSpec(w_tail.shape, lambda i: (0, 0), pipeline_mode=pl.Buffered(1)),
                          const((2, LANES, GLA_DK)), const((2, LANES, GLA_DK)), const((2, 1, GLA_DK))],
        out_specs=[pl.BlockSpec((TM_PROJ, COL_R), lambda i: (i, 0)),
                   pl.BlockSpec((TM_PROJ, 2 * GLA_DK), lambda i: (i, 0)),
                   pl.BlockSpec((SUBLANES, LANES), lambda i: (i, 0))],
        out_shape=[jax.ShapeDtypeStruct((t_all, COL_R), F32),
                   jax.ShapeDtypeStruct((t_all, 2 * GLA_DK), F32),
                   jax.ShapeDtypeStruct((t_all // TM_PROJ * SUBLANES, LANES), F32)],
        compiler_params=pltpu.CompilerParams(
            dimension_semantics=("arbitrary",), vmem_limit_bytes=VMEM_LIMIT),
        name="ln_inproj",
    )(x_ctx, x_lat, pos, mod, w_main, w_tail, w_dec_hi, w_dec_lo, b_dec)


def _gla_kernel(*refs, seq_len, seqs, has_s0, emit_state, exact):
    it = iter(refs)
    q_ref, k_ref, v_ref, laf_ref, lab_ref, g_ref = (next(it) for _ in range(6))
    s0_ref = next(it) if has_s0 else None
    o_ref = next(it)
    sout_ref = next(it) if emit_state else None
    cum_ref, aq_ref, ko_ref, dec_ref, op_ref, st_ref = (next(it) for _ in range(6))

    C = GLA_CHUNK
    n_chunks = seq_len // C
    all_chunks = seqs * n_chunks
    assert all_chunks <= GLA_UNROLL or (seqs == 1 and n_chunks % GLA_UNROLL == 0)

    def rows(n):
        if isinstance(n, int):
            return pl.ds(n * C, C)
        return pl.ds(pl.multiple_of(n * C, C), C)

    def loop(body, count=all_chunks):
        if count <= GLA_UNROLL:
            for n in range(count):
                body(n)
        else:
            def step(m, carry):
                for u in range(GLA_UNROLL):
                    body(GLA_UNROLL * m + u)
                return carry
            lax.fori_loop(0, count // GLA_UNROLL, step, 0)

    rt = lax.broadcasted_iota(jnp.int32, (C, C), 0)
    ct = lax.broadcasted_iota(jnp.int32, (C, C), 1)
    tri = ((rt >= ct).astype(BF16), (ct >= rt).astype(BF16))
    row_id = lax.broadcasted_iota(jnp.int32, (C, DK_HEAD), 0)

    def cumsum_chunk(n):
        for d, la_ref in enumerate((laf_ref, lab_ref)):
            la_hi, la_lo = _split_bf16(la_ref[rows(n), :], 2)
            cum_ref[d, rows(n), :] = _dot(tri[d], la_hi) + _dot(tri[d], la_lo)

    loop(cumsum_chunk)

    query_rows = ({}, {})
    keep = ({}, {})
    for d in range(2):
        blk = C // 2
        while blk >= GLA_LEAF:
            q_parity = 1 if d == 0 else 0
            query_rows[d][blk] = ((row_id // blk) % 2) == q_parity
            qb, kb = rt // blk, ct // blk
            keep[d][blk] = ((qb % 2) == q_parity) & ((qb == kb + 1) if d == 0 else (kb == qb + 1))
            blk //= 2
        order = (rt >= ct) if d == 0 else (ct >= rt)
        keep[d][0] = ((rt // GLA_LEAF) == (ct // GLA_LEAF)) & order

    n_leaves = C // GLA_LEAF

    def per_leaf(rows_of_cum):
        return jnp.concatenate(rows_of_cum, axis=0)

    def expand(per_leaf_rows):
        return jnp.concatenate(
            [jnp.broadcast_to(per_leaf_rows[j:j + 1, :], (GLA_LEAF, DK_HEAD)) for j in range(n_leaves)],
            axis=0)

    def store_scores(n, d, acc, q_in, k_out, end):
        aq_ref[d, rows(n), 0:C] = acc.astype(BF16)
        aq_ref[d, rows(n), C:C + DK_HEAD] = q_in.astype(BF16)
        ko_ref[d, rows(n), :] = k_out.astype(BF16)
        dec_ref[d, rows(n), :] = jnp.broadcast_to(jnp.exp(end), (DK_HEAD, DK_HEAD)).T

    def block_level_scores(cum, q, k, d, blk, at_bnd_scale):
        x = (jnp.where(query_rows[d][blk], q, k) * at_bnd_scale).astype(BF16)
        return jnp.where(keep[d][blk], _dot_nt(x, x), 0.0)

    def boundary_rows(cum, d, blk):
        bnd = blk - 1 if d == 0 else blk
        return per_leaf([cum[bnd + (j * GLA_LEAF) // (2 * blk) * (2 * blk):
                             bnd + (j * GLA_LEAF) // (2 * blk) * (2 * blk) + 1, :]
                         for j in range(n_leaves)])

    def scores_exact(n, d):
        cum = cum_ref[d, rows(n), :]
        q = q_ref[rows(n), :]
        k = k_ref[rows(n), :]
        in_leaf_pos = row_id % GLA_LEAF
        acc = jnp.zeros((C, C), F32)
        for lag in range(GLA_LEAF):
            shift = lag if d == 0 else (C - lag) % C
            k_s = pltpu.roll(k, shift, 0) if lag else k
            cum_s = pltpu.roll(cum, shift, 0) if lag else cum
            paired = (in_leaf_pos >= lag) if d == 0 else (in_leaf_pos < GLA_LEAF - lag)
            term = q * k_s * jnp.exp(jnp.where(paired, cum - cum_s, 0.0))
            s = jnp.sum(jnp.where(paired, term, 0.0), axis=-1, keepdims=True)
            diagonal = (rt - ct == lag) if d == 0 else (ct - rt == lag)
            acc = acc + jnp.where(diagonal, s, 0.0)
        blk = C // 2
        while blk >= GLA_LEAF:
            w = jnp.exp(-jnp.abs(cum - expand(boundary_rows(cum, d, blk))))
            acc = acc + block_level_scores(cum, q, k, d, blk, w)
            blk //= 2
        end = cum[C - 1:C, :] if d == 0 else cum[0:1, :]
        store_scores(n, d, acc, q * jnp.exp(cum), k * jnp.exp(end - cum), end)

    def scores(n, d):
        cum = cum_ref[d, rows(n), :]
        mid = GLA_LEAF // 2 - 1 if d == 0 else GLA_LEAF // 2
        at_mid = per_leaf([cum[mid + j * GLA_LEAF:mid + j * GLA_LEAF + 1, :] for j in range(n_leaves)])
        e = cum - expand(at_mid)
        qe = q_ref[rows(n), :] * jnp.exp(e)
        ke = k_ref[rows(n), :] * jnp.exp(-e)
        acc = jnp.where(keep[d][0], _dot_nt(qe.astype(BF16), ke.astype(BF16)), 0.0)
        blk = C // 2
        while blk >= GLA_LEAF:
            through = expand(jnp.exp(-jnp.abs(at_mid - boundary_rows(cum, d, blk))))
            acc = acc + block_level_scores(cum, qe, ke, d, blk, through)
            blk //= 2
        end = cum[C - 1:C, :] if d == 0 else cum[0:1, :]
        store_scores(n, d, acc, qe * expand(jnp.exp(at_mid)), ke * expand(jnp.exp(end - at_mid)), end)

    def scores_chunk(n):
        for d in range(2):
            (scores_exact if exact else scores)(n, d)

    loop(scores_chunk)

    for s in range(seqs):
        for d in range(2):
            if has_s0:
                st_ref[2 * s + d] = s0_ref[s, d, 0]
            else:
                st_ref[2 * s + d] = jnp.zeros((DK_HEAD, DV_HEAD), F32)

    def scan(s, m, d):
        n = s * n_chunks + (m if d == 0 else n_chunks - 1 - m)
        v = v_ref[rows(n), :].astype(BF16)
        st = st_ref[2 * s + d]
        op_ref[d, rows(n), :] = _dot(aq_ref[d, rows(n), :],
                                     jnp.concatenate([v, st.astype(BF16)], axis=0))
        dec = dec_ref[d, rows(n), :]
        st_ref[2 * s + d] = (st * jnp.concatenate([dec] * (DV_HEAD // DK_HEAD), axis=1)
                             + _dot_tn(ko_ref[d, rows(n), :], v))

    def scan_step(m):
        for s in range(seqs):
            for d in range(2):
                scan(s, m, d)

    loop(scan_step, n_chunks)
    if emit_state:
        for s in range(seqs):
            for d in range(2):
                sout_ref[s, d, 0] = st_ref[2 * s + d]

    g = g_ref[...]

    def finish_chunk(n):
        o = op_ref[0, rows(n), :] + op_ref[1, rows(n), :]
        ms = jnp.mean(o * o, axis=-1, keepdims=True)
        o_ref[rows(n), :] = o * lax.rsqrt(ms + LN_EPS) * g

    loop(finish_chunk)


def _gla(proj, la, g, s0, *, n_seq, seq_len, row0, emit_state, exact):
    has_s0 = s0 is not None
    seqs = max(1, min(n_seq, GLA_UNROLL // max(1, seq_len // GLA_CHUNK)))
    assert n_seq % seqs == 0 and row0 % (seqs * seq_len) == 0
    blk_rows = seqs * seq_len
    blk0 = row0 // blk_rows
    kern = functools.partial(_gla_kernel, seq_len=seq_len, seqs=seqs, has_s0=has_s0,
                             emit_state=emit_state, exact=exact)
    in_specs = [
        pl.BlockSpec((blk_rows, DK_HEAD), lambda b, h: (blk0 + b, COL_Q // DK_HEAD + h)),
        pl.BlockSpec((blk_rows, DK_HEAD), lambda b, h: (blk0 + b, COL_K // DK_HEAD + h)),
        pl.BlockSpec((blk_rows, DV_HEAD), lambda b, h: (blk0 + b, COL_V // DV_HEAD + h)),
        pl.BlockSpec((blk_rows, DK_HEAD), lambda b, h: (blk0 + b, h)),
        pl.BlockSpec((blk_rows, DK_HEAD), lambda b, h: (blk0 + b, GLA_HEADS + h)),
        pl.BlockSpec((1, DV_HEAD), lambda b, h: (0, 0)),
    ]
    args = [proj, proj, proj, la, la, g]
    state_spec = pl.BlockSpec((seqs, 2, 1, DK_HEAD, DV_HEAD), lambda b, h: (b, 0, h, 0, 0))
    if has_s0:
        in_specs.append(state_spec)
        args.append(s0)
    out_specs = [pl.BlockSpec((blk_rows, DV_HEAD), lambda b, h: (b, h))]
    out_shape = [jax.ShapeDtypeStruct((n_seq * seq_len, GLA_DV), F32)]
    if emit_state:
        out_specs.append(state_spec)
        out_shape.append(jax.ShapeDtypeStruct((n_seq, 2, GLA_HEADS, DK_HEAD, DV_HEAD), F32))

    res = pl.pallas_call(
        kern,
        grid=(n_seq // seqs, GLA_HEADS),
        in_specs=in_specs,
        out_specs=out_specs,
        out_shape=out_shape,
        scratch_shapes=[pltpu.VMEM((2, blk_rows, DK_HEAD), F32),
                        pltpu.VMEM((2, blk_rows, GLA_CHUNK + DK_HEAD), BF16),
                        pltpu.VMEM((2, blk_rows, DK_HEAD), BF16),
                        pltpu.VMEM((2, blk_rows, DK_HEAD), F32),
                        pltpu.VMEM((2, blk_rows, DV_HEAD), F32),
                        pltpu.VMEM((2 * seqs, DK_HEAD, DV_HEAD), F32)],
        compiler_params=pltpu.CompilerParams(
            dimension_semantics=("arbitrary", "arbitrary"), vmem_limit_bytes=VMEM_LIMIT),
        name="gla%s_seq%d" % ("_exact" if exact else "", seq_len),
    )(*args)
    return res


def _fnet_kernel(f_ref, cl_ref, sl_ref, cg_ref, sg_ref, o_ref, uc_ref, us_ref, *, seq_len, seqs):
    cg = cg_ref[...]
    sg = sg_ref[...]
    for grp in range(FNET_GROUPS):
        lo = grp * FNET_GROUP_DIM
        u = f_ref[:, lo:lo + FNET_GROUP_DIM].astype(BF16)
        uc_ref[:, lo:lo + FNET_GROUP_DIM] = _dot(u, cg).astype(BF16)
        us_ref[:, lo:lo + FNET_GROUP_DIM] = _dot(u, sg).astype(BF16)
    scale = 1.0 / math.sqrt(seq_len * FNET_GROUP_DIM)
    for s in range(seqs):
        r = slice(s * seq_len, (s + 1) * seq_len)
        o_ref[r, :] = (_dot(cl_ref[...], uc_ref[r, :]) - _dot(sl_ref[...], us_ref[r, :])) * scale


def _dft_mats(n):
    j = np.arange(n, dtype=np.int64)
    ang = (2.0 * np.pi / n) * ((j[:, None] * j[None, :]) % n).astype(np.float64)
    return (jnp.asarray(np.cos(ang), dtype=F32).astype(BF16),
            jnp.asarray(np.sin(ang), dtype=F32).astype(BF16))


def _fnet(proj, *, n_seq, seq_len, row0):
    seqs = max(1, min(n_seq, FNET_ROWS // seq_len))
    assert n_seq % seqs == 0 and row0 % (seqs * seq_len) == 0
    blk_rows = seqs * seq_len
    blk0 = row0 // blk_rows
    cl, sl = _dft_mats(seq_len)
    cg, sg = _dft_mats(FNET_GROUP_DIM)
    kern = functools.partial(_fnet_kernel, seq_len=seq_len, seqs=seqs)
    return pl.pallas_call(
        kern,
        grid=(n_seq // seqs,),
        in_specs=[pl.BlockSpec((blk_rows, FNET_DIM), lambda b: (blk0 + b, COL_F // FNET_DIM)),
                  pl.BlockSpec((seq_len, seq_len), lambda b: (0, 0)),
                  pl.BlockSpec((seq_len, seq_len), lambda b: (0, 0)),
                  pl.BlockSpec((FNET_GROUP_DIM, FNET_GROUP_DIM), lambda b: (0, 0)),
                  pl.BlockSpec((FNET_GROUP_DIM, FNET_GROUP_DIM), lambda b: (0, 0))],
        out_specs=pl.BlockSpec((blk_rows, FNET_DIM), lambda b: (b, 0)),
        out_shape=jax.ShapeDtypeStruct((n_seq * seq_len, FNET_DIM), F32),
        scratch_shapes=[pltpu.VMEM((blk_rows, FNET_DIM), BF16),
                        pltpu.VMEM((blk_rows, FNET_DIM), BF16)],
        compiler_params=pltpu.CompilerParams(
            dimension_semantics=("arbitrary",), vmem_limit_bytes=VMEM_LIMIT),
        name="fnet_seq%d" % seq_len,
    )(proj, cl, sl, cg, sg)


def _merge_kernel(xc_ref, xl_ref, pos_ref, mod_ref, oc_ref, ol_ref, mc_ref, ml_ref,
                  g_ref, ga_ref, gb_ref, wbg_ref, wbf_ref, wo_ref, l1g_ref, l1b_ref, wrh_ref, wrl_ref,
                  br_ref, x1_ref, h2_ref, ridx_ref, rw_ref, *, n_ctx_tiles, alpha):
    i = pl.program_id(0)

    tm = x1_ref.shape[0]
    groups = [slice(r0, r0 + TM_MERGE_GROUP) for r0 in range(0, tm, TM_MERGE_GROUP)]
    shape = (TM_MERGE_GROUP, LANES)
    lane_i = lax.broadcasted_iota(jnp.int32, shape, 1)
    lane = lane_i.astype(F32)

    def compute(x_of, o_ref, mx_ref):
        branch = []
        for rows in groups:
            a = (o_ref[rows, :] * g_ref[rows, :]).astype(BF16)
            branch.append((_dot(a, wbg_ref[...]), _dot(mx_ref[rows, :].astype(BF16), wbf_ref[...])))
        mix = []
        for rows, (gla_out, fnet_out) in zip(groups, branch):
            merged = ga_ref[rows, :] * gla_out + gb_ref[rows, :] * fnet_out
            mix.append(_dot(merged.astype(BF16), wo_ref[...]))
        logits = []
        for rows, mix_g in zip(groups, mix):
            y = alpha * x_of(rows) + mod_ref[0, 2:3, :] * mix_g
            x1 = _layer_norm(y) * l1g_ref[...] + l1b_ref[...]
            x1_ref[rows, :] = x1
            h2 = _layer_norm(x1) * (1.0 + mod_ref[0, 4:5, :]) + mod_ref[0, 3:4, :]
            _store_row_tiles(h2_ref, h2, rows.start)
            h_hi, h_lo = _split_bf16(h2, 2)
            logits.append(_dot(h_hi, wrh_ref[...]) + _dot(h_lo, wrh_ref[...])
                          + _dot(h_hi, wrl_ref[...]) + br_ref[...])
        idx_out = [jnp.zeros(shape, F32) for _ in groups]
        val_out = [jnp.zeros(shape, F32) for _ in groups]
        top0 = [None] * len(groups)
        denom = [None] * len(groups)
        for kk in range(TOP_K):
            for gi in range(len(groups)):
                m = jnp.max(logits[gi], axis=-1, keepdims=True)
                sel = jnp.min(jnp.where(logits[gi] == m, lane, float(LANES)), axis=-1, keepdims=True)
                if kk == 0:
                    top0[gi] = m
                    p = jnp.ones_like(m)
                    denom[gi] = p
                else:
                    p = jnp.exp(m - top0[gi])
                    denom[gi] = denom[gi] + p
                idx_out[gi] = jnp.where(lane_i == kk, sel, idx_out[gi])
                val_out[gi] = jnp.where(lane_i == kk, p, val_out[gi])
                logits[gi] = jnp.where(lane == sel, -jnp.inf, logits[gi])
        for gi, rows in enumerate(groups):
            ridx_ref[rows, :] = idx_out[gi].astype(jnp.int32)
            rw_ref[rows, :] = val_out[gi] / denom[gi]

    @pl.when(i < n_ctx_tiles)
    def _():
        compute(lambda rows: xc_ref[rows, :], oc_ref, mc_ref)

    @pl.when(i >= n_ctx_tiles)
    def _():
        compute(lambda rows: xl_ref[rows, :] + pos_ref[rows, :], ol_ref, ml_ref)


def _merge(x_ctx, x_lat, pos, mod, o_ctx, o_lat, mixed_ctx, mixed_lat, proj,
           wbg, wbf, wo, l1g, l1b, wr_hi, wr_lo, br, lat_len, alpha):
    t_ctx, t_lat = x_ctx.shape[0], x_lat.shape[0]
    t_all = t_ctx + t_lat
    tm = TM_MERGE
    n_ctx_tiles = t_ctx // tm
    kern = functools.partial(_merge_kernel, n_ctx_tiles=n_ctx_tiles, alpha=alpha)
    specs = _token_specs(tm, n_ctx_tiles, lat_len // tm, mod.shape[0] - 1)
    ctx_map, lat_map = _group_maps(n_ctx_tiles)

    def const(shape):
        return pl.BlockSpec(shape, lambda i: (0,) * len(shape))

    in_specs = specs + [
        pl.BlockSpec((tm, GLA_DV), ctx_map),
        pl.BlockSpec((tm, GLA_DV), lat_map),
        pl.BlockSpec((tm, FNET_DIM), ctx_map),
        pl.BlockSpec((tm, FNET_DIM), lat_map),
        pl.BlockSpec((tm, GLA_DV), lambda i: (i, COL_G // GLA_DV)),
        pl.BlockSpec((tm, D_MODEL), lambda i: (i, COL_GATE_A // D_MODEL)),
        pl.BlockSpec((tm, D_MODEL), lambda i: (i, COL_GATE_B // D_MODEL)),
        const((GLA_DV, D_MODEL)), const((FNET_DIM, D_MODEL)), const((D_MODEL, D_MODEL)),
        const((1, D_MODEL)), const((1, D_MODEL)),
        const((D_MODEL, LANES)), const((D_MODEL, LANES)), const((1, LANES)),
    ]
    out_specs = [pl.BlockSpec((tm, D_MODEL), lambda i: (i, 0)),
                 pl.BlockSpec((tm * ROW_TILES, LANES), lambda i: (i, 0)),
                 pl.BlockSpec((tm, LANES), lambda i: (i, 0)),
                 pl.BlockSpec((tm, LANES), lambda i: (i, 0))]
    out_shape = [jax.ShapeDtypeStruct((t_all, D_MODEL), F32),
                 jax.ShapeDtypeStruct((t_all * ROW_TILES, LANES), ROW_DTYPE),
                 jax.ShapeDtypeStruct((t_all, LANES), jnp.int32),
                 jax.ShapeDtypeStruct((t_all, LANES), F32)]
    return pl.pallas_call(
        kern,
        grid=(t_all // tm,),
        in_specs=in_specs,
        out_specs=out_specs,
        out_shape=out_shape,
        compiler_params=pltpu.CompilerParams(
            dimension_semantics=("arbitrary",), vmem_limit_bytes=VMEM_LIMIT),
        name="merge_ln1_router",
    )(x_ctx, x_lat, pos, mod, o_ctx, o_lat, mixed_ctx, mixed_lat, proj, proj, proj,
      wbg, wbf, wo, l1g, l1b, wr_hi, wr_lo, br)


def _sc_mesh():
    return plsc.VectorSubcoreMesh(core_axis_name="c", subcore_axis_name="s")


def _sc_worker_id():
    return lax.axis_index("s") * SC_CORES + lax.axis_index("c")


def _sc_scatter_rows(src, idx, n_out):
    n_src = src.shape[0]
    w = SC_WINDOW
    n_chunks = n_src // (SC_WORKERS * w)
    copies = idx.shape[0]
    assert n_chunks % 2 == 0 and idx.shape == (copies, SC_WORKERS, n_chunks, w)

    @functools.partial(
        pl.kernel, mesh=_sc_mesh(),
        out_type=jax.ShapeDtypeStruct((n_out, ROW_TILES, LANES), ROW_DTYPE),
        scratch_types=[pltpu.VMEM((copies * n_chunks, w), jnp.int32),
                       pltpu.VMEM((2, w, ROW_TILES, LANES), ROW_DTYPE),
                       pltpu.SemaphoreType.DMA((2,)),
                       pltpu.SemaphoreType.DMA((2,))],
        name="moe_dispatch_scatter")
    def k(src_hbm, idx_hbm, out_hbm, idx_v, rows_v, rsem, wsem):
        wid = _sc_worker_id()
        base = wid * (n_chunks * w)
        for kk in range(copies):
            pltpu.sync_copy(idx_hbm.at[kk, wid], idx_v.at[pl.ds(kk * n_chunks, n_chunks)])

        def read(j, slot):
            return pltpu.make_async_copy(src_hbm.at[pl.ds(base + j * w, w)], rows_v.at[slot],
                                         rsem.at[slot])

        def scatter(j, kk, slot):
            return pltpu.make_async_copy(rows_v.at[slot], out_hbm.at[idx_v.at[kk * n_chunks + j]],
                                         wsem.at[slot])

        read(0, 0).start()

        @pl.loop(0, n_chunks, step=2)
        def _(jj):
            read(jj, 0).wait()

            @pl.when(jj > 0)
            def _():
                for kk in range(copies):
                    scatter(jj - 1, kk, 1).wait()

            read(jj + 1, 1).start()
            for kk in range(copies):
                scatter(jj, kk, 0).start()
            read(jj + 1, 1).wait()
            for kk in range(copies):
                scatter(jj, kk, 0).wait()

            @pl.when(jj + 2 < n_chunks)
            def _():
                read(jj + 2, 0).start()

            for kk in range(copies):
                scatter(jj + 1, kk, 1).start()

        for kk in range(copies):
            scatter(n_chunks - 1, kk, 1).wait()

    return k(src, idx)


def _sc_gather_rows(table, idx):
    _, n_chunks, w = idx.shape
    assert n_chunks % 2 == 0 and idx.shape[0] == SC_WORKERS and w == SC_WINDOW
    n_out = SC_WORKERS * n_chunks * w

    @functools.partial(
        pl.kernel, mesh=_sc_mesh(),
        out_type=jax.ShapeDtypeStruct((n_out, ROW_TILES, LANES), ROW_DTYPE),
        scratch_types=[pltpu.VMEM((n_chunks, w), jnp.int32),
                       pltpu.VMEM((2, w, ROW_TILES, LANES), ROW_DTYPE),
                       pltpu.SemaphoreType.DMA((2,)),
                       pltpu.SemaphoreType.DMA((2,))],
        name="moe_combine_gather")
    def k(table_hbm, idx_hbm, out_hbm, idx_v, rows_v, gsem, wsem):
        wid = _sc_worker_id()
        base = wid * (n_chunks * w)
        pltpu.sync_copy(idx_hbm.at[wid], idx_v)

        def gather(j, slot):
            return pltpu.make_async_copy(table_hbm.at[idx_v.at[j]], rows_v.at[slot], gsem.at[slot])

        def write(j, slot):
            return pltpu.make_async_copy(rows_v.at[slot], out_hbm.at[pl.ds(base + j * w, w)],
                                         wsem.at[slot])

        gather(0, 0).start()

        @pl.loop(0, n_chunks, step=2)
        def _(jj):
            gather(jj, 0).wait()

            @pl.when(jj > 0)
            def _():
                write(jj - 1, 1).wait()

            gather(jj + 1, 1).start()
            write(jj, 0).start()
            gather(jj + 1, 1).wait()
            write(jj, 0).wait()

            @pl.when(jj + 2 < n_chunks)
            def _():
                gather(jj + 2, 0).start()

            write(jj + 1, 1).start()

        write(n_chunks - 1, 1).wait()

    return k(table, idx)


def _moe_kernel(be_ref, nu_ref, nv_ref, slot_ref, nxt_ref, x_ref, wgu_hbm, bgu_ref, wd_hbm, bd_ref,
                o_ref, wgu_st, wd_st, wgu_bf, wd_bf, xb_ref, sem):
    b = pl.program_id(0)
    e = be_ref[b]
    prev = be_ref[jnp.maximum(b - 1, 0)]
    active = b < nu_ref[0]
    changed = (b == 0) | (e != prev)

    def weight_copies(expert, s):
        return (pltpu.make_async_copy(wgu_hbm.at[expert], wgu_st.at[s], sem.at[0, s]),
                pltpu.make_async_copy(wd_hbm.at[expert], wd_st.at[s], sem.at[1, s]))

    @pl.when(active & changed)
    def _():
        s = slot_ref[b]

        @pl.when(b == 0)
        def _():
            for cp in weight_copies(e, s):
                cp.start()

        for cp in weight_copies(e, s):
            cp.wait()
        wgu_bf[...] = wgu_st[s].astype(BF16)
        wd_bf[...] = wd_st[s].astype(BF16)
        nxt = nxt_ref[b]

        @pl.when(nxt >= 0)
        def _():
            for cp in weight_copies(nxt, 1 - s):
                cp.start()

    n_valid = nv_ref[b]

    def expert_mlp(n_rows):
        valid = lax.broadcasted_iota(jnp.int32, (n_rows, LANES), 0) < n_valid
        for j in range(ROW_TILES):
            for half, xj in enumerate(_load_row_tile(x_ref, j, n_rows)):
                c0 = half * HALF_MODEL + j * LANES
                xb_ref[0:n_rows, c0:c0 + LANES] = jnp.where(valid, xj, 0.0).astype(BF16)
        gu = _dot(xb_ref[0:n_rows, :], wgu_bf[...]) + bgu_ref[0]
        gate = jnp.minimum(gu[:, :D_EXPERT], SWIGLU_LIMIT)
        up = jnp.clip(gu[:, D_EXPERT:], -SWIGLU_LIMIT, SWIGLU_LIMIT)
        glu = gate * _sigmoid(gate * SWIGLU_ALPHA)
        act = ((up + 1.0) * glu).astype(BF16)
        _store_row_tiles(o_ref, _dot(act, wd_bf[...]) + bd_ref[0])

    for n_rows in range(MOE_STEP, MOE_ROWS + 1, MOE_STEP):
        @pl.when(active & (n_valid > n_rows - MOE_STEP) & (n_valid <= n_rows))
        def _(n_rows=n_rows):
            expert_mlp(n_rows)


def _moe(tables, xs, w_gate_up, b_gate_up, w_down, b_down):
    p_rows = xs.shape[0] // ROW_TILES
    n_blocks = p_rows // MOE_ROWS

    def blk(b, be, nu, *_):
        return jnp.minimum(b, nu[0] - 1)

    def expert(b, be, nu, *_):
        return (be[blk(b, be, nu)], 0, 0)

    def rows(b, be, nu, *_):
        return (blk(b, be, nu), 0)

    grid_spec = pltpu.PrefetchScalarGridSpec(
        num_scalar_prefetch=len(tables),
        grid=(n_blocks,),
        in_specs=[
            pl.BlockSpec((MOE_ROWS * ROW_TILES, LANES), rows),
            pl.BlockSpec(memory_space=pl.ANY),
            pl.BlockSpec((1, 1, 2 * D_EXPERT), expert),
            pl.BlockSpec(memory_space=pl.ANY),
            pl.BlockSpec((1, 1, D_MODEL), expert),
        ],
        out_specs=pl.BlockSpec((MOE_ROWS * ROW_TILES, LANES), rows),
        scratch_shapes=[pltpu.VMEM((2, D_MODEL, 2 * D_EXPERT), F32),
                        pltpu.VMEM((2, D_EXPERT, D_MODEL), F32),
                        pltpu.VMEM((D_MODEL, 2 * D_EXPERT), BF16),
                        pltpu.VMEM((D_EXPERT, D_MODEL), BF16),
                        pltpu.VMEM((MOE_ROWS, D_MODEL), BF16),
                        pltpu.SemaphoreType.DMA((2, 2))],
    )
    return pl.pallas_call(
        _moe_kernel,
        grid_spec=grid_spec,
        out_shape=jax.ShapeDtypeStruct((p_rows * ROW_TILES, LANES), ROW_DTYPE),
        compiler_params=pltpu.CompilerParams(
            dimension_semantics=("arbitrary",), vmem_limit_bytes=VMEM_LIMIT),
        name="moe_grouped_mlp",
    )(*tables, xs, w_gate_up, b_gate_up.reshape(N_EXPERTS, 1, 2 * D_EXPERT), w_down,
      b_down.reshape(N_EXPERTS, 1, D_MODEL))


def _combine_kernel(x1_ref, y0_ref, y1_ref, y2_ref, y3_ref, rw_ref, mod_ref, g_ref, b_ref, o_ref,
                    *, alpha):
    rw = rw_ref[...]
    y_refs = (y0_ref, y1_ref, y2_ref, y3_ref)
    pieces = [None] * (2 * ROW_TILES)
    for j in range(ROW_TILES):
        for kk in range(TOP_K):
            for half, yj in enumerate(_load_row_tile(y_refs[kk], j, rw.shape[0])):
                term = rw[:, kk:kk + 1] * yj
                slot = half * ROW_TILES + j
                pieces[slot] = term if kk == 0 else pieces[slot] + term
    ff = jnp.concatenate(pieces, axis=-1)
    y = alpha * x1_ref[...] + mod_ref[0, 5:6, :] * ff
    o_ref[...] = _layer_norm(y) * g_ref[...] + b_ref[...]


def _combine(x1, yg, rw, mod, l2g, l2b, *, row0, n_rows, mod_map, alpha):
    tm = TM_MIX
    t0 = row0 // tm
    tiles = n_rows // tm
    kern = functools.partial(_combine_kernel, alpha=alpha)

    def y_spec(kk):
        return pl.BlockSpec((tm * ROW_TILES, LANES), lambda i: (kk * tiles + i, 0))

    return pl.pallas_call(
        kern,
        grid=(n_rows // tm,),
        in_specs=[pl.BlockSpec((tm, D_MODEL), lambda i: (t0 + i, 0))]
        + [y_spec(kk) for kk in range(TOP_K)]
        + [pl.BlockSpec((tm, LANES), lambda i: (t0 + i, 0)),
           pl.BlockSpec((1, N_MOD, D_MODEL), mod_map),
           pl.BlockSpec((1, D_MODEL), lambda i: (0, 0)),
           pl.BlockSpec((1, D_MODEL), lambda i: (0, 0))],
        out_specs=pl.BlockSpec((tm, D_MODEL), lambda i: (i, 0)),
        out_shape=jax.ShapeDtypeStruct((n_rows, D_MODEL), F32),
        compiler_params=pltpu.CompilerParams(
            dimension_semantics=("arbitrary",), vmem_limit_bytes=VMEM_LIMIT),
        name="combine_ln2",
    )(x1, yg, yg, yg, yg, rw, mod, l2g, l2b)


def _route_kernel(ridx_ref, dest_ref, cnt_ref, run_ref, bst_ref):
    phase = pl.program_id(0)
    i = pl.program_id(1)
    tm = ridx_ref.shape[0]
    ridx = ridx_ref[...]
    lane = lax.broadcasted_iota(jnp.int32, (tm, LANES), 1)
    hits = [ridx[:, kk:kk + 1] == lane for kk in range(TOP_K)]
    chosen = jnp.where(hits[0], 1.0, 0.0)
    for kk in range(1, TOP_K):
        chosen = chosen + jnp.where(hits[kk], 1.0, 0.0)
    colsum = jnp.sum(chosen, axis=0, keepdims=True)

    @pl.when((phase == 0) & (i == 0))
    def _():
        run_ref[...] = jnp.zeros_like(run_ref)

    @pl.when(phase == 0)
    def _():
        run_ref[...] = run_ref[...] + colsum

    @pl.when((phase == 1) & (i == 0))
    def _():
        counts = run_ref[...]
        cnt_ref[...] = counts
        blocks = jnp.floor((counts + (MOE_ROWS - 1.0)) * (1.0 / MOE_ROWS))
        r = lax.broadcasted_iota(jnp.int32, (LANES, LANES), 0)
        c = lax.broadcasted_iota(jnp.int32, (LANES, LANES), 1)
        before = jnp.dot(blocks, (r < c).astype(F32), precision=HIGHEST, preferred_element_type=F32)
        bst_ref[...] = before * float(MOE_ROWS)
        run_ref[...] = jnp.zeros_like(run_ref)

    @pl.when(phase == 1)
    def _():
        rt = lax.broadcasted_iota(jnp.int32, (tm, tm), 0)
        ct = lax.broadcasted_iota(jnp.int32, (tm, tm), 1)
        earlier = _dot((ct < rt).astype(BF16), chosen.astype(BF16))
        row_of = bst_ref[0:1, :] + run_ref[0:1, :] + earlier
        out = jnp.take_along_axis(row_of, ridx, axis=1)
        dest_ref[...] = out.T[0:SUBLANES, :].astype(jnp.int32)
        run_ref[...] = run_ref[...] + colsum


def _route(ridx):
    t_all = ridx.shape[0]
    tm = TM_ROUTE
    return pl.pallas_call(
        _route_kernel,
        grid=(2, t_all // tm),
        in_specs=[pl.BlockSpec((tm, LANES), lambda p, i: (i, 0))],
        out_specs=[pl.BlockSpec((SUBLANES, tm), lambda p, i: (0, i * p)),
                   pl.BlockSpec((SUBLANES, LANES), lambda p, i: (0, 0))],
        out_shape=[jax.ShapeDtypeStruct((SUBLANES, t_all), jnp.int32),
                   jax.ShapeDtypeStruct((SUBLANES, LANES), F32)],
        scratch_shapes=[pltpu.VMEM((SUBLANES, LANES), F32),
                        pltpu.VMEM((SUBLANES, LANES), F32)],
        compiler_params=pltpu.CompilerParams(
            dimension_semantics=("arbitrary", "arbitrary"), vmem_limit_bytes=VMEM_LIMIT),
        name="moe_route",
    )(ridx)


def _routing_tables(counts, n_blocks):
    experts = jnp.arange(N_EXPERTS, dtype=jnp.int32)
    blocks_per = (counts + MOE_ROWS - 1) // MOE_ROWS
    bends = jnp.cumsum(blocks_per)
    bstarts = bends - blocks_per
    blocks = jnp.arange(n_blocks, dtype=jnp.int32)
    block_expert = jnp.minimum(
        jnp.sum((bends[None, :] <= blocks[:, None]).astype(jnp.int32), axis=1), N_EXPERTS - 1)
    n_used = bends[-1:].astype(jnp.int32)
    owner = block_expert[:, None] == experts[None, :]

    def per_block(table):
        return jnp.sum(jnp.where(owner, table[None, :], 0), axis=1)

    n_valid = jnp.clip(per_block(counts) - (blocks - per_block(bstarts)) * MOE_ROWS,
                       0, MOE_ROWS).astype(jnp.int32)
    present = blocks_per > 0
    ordinal = jnp.cumsum(present.astype(jnp.int32)) - 1
    later = lax.cummin(jnp.where(present, experts, N_EXPERTS), reverse=True)
    succ = jnp.concatenate([later[1:], jnp.full((1,), N_EXPERTS, jnp.int32)])
    succ = jnp.where(succ >= N_EXPERTS, -1, succ)
    stage_slot = (per_block(ordinal) % 2).astype(jnp.int32)
    next_expert = per_block(succ).astype(jnp.int32)
    return (block_expert.astype(jnp.int32), n_used, n_valid, stage_slot, next_expert)


def _pos_embed_2d(n_tokens):
    rows = n_tokens // GRID_W
    r = np.repeat(np.arange(rows), GRID_W).astype(np.float32)
    col = np.tile(np.arange(GRID_W), rows).astype(np.float32)
    quarter = D_MODEL // 4
    omega = (np.float32(1.0)
             / np.power(np.float32(10000.0), np.arange(quarter, dtype=np.float32) / np.float32(quarter)))
    er = (r[:, None] * omega).astype(np.float64)
    ec = (col[:, None] * omega).astype(np.float64)
    table = np.concatenate([np.sin(er), np.cos(er), np.sin(ec), np.cos(ec)], axis=-1)
    return jnp.asarray(table, dtype=F32)


def _split_w_in(w):
    wt = w.T
    o_r = 2 * GLA_DK + 2 * GLA_DV
    o_f = o_r + DECAY_RANK
    o_gate = o_f + FNET_DIM
    row_scale = jnp.where(jnp.arange(o_r) < GLA_DK, DK_HEAD ** -0.5, 1.0).astype(w.dtype)
    w_main = (wt[:o_r] * row_scale[:, None]).astype(BF16)
    pad = jnp.zeros((LANES - DECAY_RANK, w.shape[0]), BF16)
    w_tail = jnp.concatenate([wt[o_gate:].astype(BF16), wt[o_f:o_gate].astype(BF16),
                              wt[o_r:o_f].astype(BF16), pad], axis=0)
    return w_main, w_tail


def kernel(x_prompt, x_sample, state_gla, c, c_ctx, w_ada, b_ada, w_in, w_dec_fwd, b_dec_fwd,
           w_dec_bwd, b_dec_bwd, gla_norm_g, w_br_gla, w_br_fnet, w_out, ln1_g, ln1_b, w_router,
           b_router, w_gate_up, b_gate_up, w_down, b_down, ln2_g, ln2_b):
    n_req, ctx_len, _ = x_prompt.shape
    n_lat, lat_len, _ = x_sample.shape
    depth = w_in.shape[0]
    alpha = (2.0 * depth) ** 0.25
    t_ctx = n_req * ctx_len
    t_lat = n_lat * lat_len
    t_all = t_ctx + t_lat
    tile_rows = max(TM_PROJ, TM_MERGE, TM_MIX)
    assert t_ctx % tile_rows == 0 and lat_len % tile_rows == 0
    assert ctx_len % GLA_CHUNK == 0 and lat_len % GLA_CHUNK == 0 and t_all % TM_ROUTE == 0
    assert t_ctx % (SC_WORKERS * SC_WINDOW * 2) == 0 and t_lat % (SC_WORKERS * SC_WINDOW * 2) == 0
    assert MOE_ROWS & (MOE_ROWS - 1) == 0 and (t_all * TOP_K) % MOE_ROWS == 0

    x_ctx = x_prompt.reshape(t_ctx, D_MODEL)
    x_lat = x_sample.reshape(t_lat, D_MODEL)
    pos = _pos_embed_2d(lat_len)
    zero_pos = jnp.zeros_like(pos)

    cond_rows = -(-(n_lat + 1) // SUBLANES) * SUBLANES
    cond = jnp.zeros((cond_rows, D_MODEL), F32).at[:n_lat].set(c).at[cond_rows - 1].set(c_ctx)

    n_moe_blocks = (t_all * TOP_K) // MOE_ROWS + N_EXPERTS
    tok_chunks = t_all // (SC_WORKERS * SC_WINDOW)
    states = []
    for l in range(depth):
        mod = _ada(cond, w_ada[l], b_ada[l]).reshape(cond_rows, N_MOD, D_MODEL)
        layer_pos = pos if l == 0 else zero_pos
        w_dec = jnp.zeros((2, LANES, GLA_DK), F32)
        w_dec = w_dec.at[0, :DECAY_RANK].set(w_dec_fwd[l]).at[1, :DECAY_RANK].set(w_dec_bwd[l])
        b_dec = jnp.stack([b_dec_fwd[l], b_dec_bwd[l]]).reshape(2, 1, GLA_DK)
        proj, la, la_min = _inproj(x_ctx, x_lat, layer_pos, mod, *_split_w_in(w_in[l]), w_dec, b_dec,
                                   lat_len)

        norm_g = gla_norm_g[l].reshape(1, DV_HEAD)
        s0_lat = state_gla[:, l]

        def gla_both(exact):
            o_c, s_c = _gla(proj, la, norm_g, None, n_seq=n_req, seq_len=ctx_len, row0=0,
                            emit_state=True, exact=exact)
            (o_l,) = _gla(proj, la, norm_g, s0_lat, n_seq=n_lat, seq_len=lat_len, row0=t_ctx,
                          emit_state=False, exact=exact)
            return o_c, s_c, o_l

        fast_ok = jnp.min(la_min) >= -GLA_FAST_MAX_STEP_DECAY
        o_ctx, s_new, o_lat = lax.cond(fast_ok, lambda: gla_both(False), lambda: gla_both(True))
        states.append(s_new)

        mixed_ctx = _fnet(proj, n_seq=n_req, seq_len=ctx_len, row0=0)
        mixed_lat = _fnet(proj, n_seq=n_lat, seq_len=lat_len, row0=t_ctx)

        wr = jnp.zeros((D_MODEL, LANES), F32).at[:, :N_EXPERTS].set(w_router[l])
        br = jnp.full((1, LANES), -1e30, F32).at[0, :N_EXPERTS].set(b_router[l])
        wr_hi = wr.astype(BF16)
        wr_lo = (wr - wr_hi.astype(F32)).astype(BF16)
        x1, h2, ridx, rw = _merge(
            x_ctx, x_lat, layer_pos, mod, o_ctx, o_lat, mixed_ctx, mixed_lat, proj,
            w_br_gla[l].astype(BF16), w_br_fnet[l].astype(BF16), (0.5 * w_out[l]).astype(BF16),
            ln1_g[l].reshape(1, D_MODEL), ln1_b[l].reshape(1, D_MODEL), wr_hi, wr_lo, br, lat_len,
            alpha)

        dest, counts = _route(ridx)
        moe_tables = _routing_tables(counts[0, :N_EXPERTS].astype(jnp.int32), n_moe_blocks)
        dest = dest[:TOP_K]
        scatter_idx = dest.reshape(TOP_K, SC_WORKERS, tok_chunks, SC_WINDOW)
        p_rows = n_moe_blocks * MOE_ROWS
        xs = _sc_scatter_rows(h2.reshape(t_all, ROW_TILES, LANES), scatter_idx, p_rows)
        yb = _moe(moe_tables, xs.reshape(p_rows * ROW_TILES, LANES),
                  w_gate_up[l], b_gate_up[l], w_down[l], b_down[l])
        yb = yb.reshape(p_rows, ROW_TILES, LANES)

        def gathered(row0, n_rows):
            idx = dest[:, row0:row0 + n_rows].reshape(SC_WORKERS, -1, SC_WINDOW)
            return _sc_gather_rows(yb, idx).reshape(TOP_K * n_rows * ROW_TILES, LANES)

        l2g = ln2_g[l].reshape(1, D_MODEL)
        l2b = ln2_b[l].reshape(1, D_MODEL)
        tiles_per_seq = lat_len // TM_MIX
        yg_ctx = gathered(0, t_ctx)
        yg_lat = gathered(t_ctx, t_lat)
        x_ctx = _combine(x1, yg_ctx, rw, mod, l2g, l2b, row0=0, n_rows=t_ctx,
                         mod_map=lambda i: (cond_rows - 1, 0, 0), alpha=alpha)
        x_lat = _combine(x1, yg_lat, rw, mod, l2g, l2b, row0=t_ctx, n_rows=t_lat,
                         mod_map=lambda i: (i // tiles_per_seq, 0, 0), alpha=alpha)

    y_prompt = x_ctx.reshape(x_prompt.shape)
    y_sample = x_lat.reshape(x_sample.shape)
    new_state = jnp.stack(states, axis=1).astype(x_prompt.dtype)
    return (y_prompt, y_sample, new_state)
```

```python
import functools
import math

import numpy as np
import jax
import jax.numpy as jnp
from jax import lax
from jax.experimental import pallas as pl
from jax.experimental.pallas import tpu as pltpu
from jax.experimental.pallas import tpu_sc as plsc

F32 = jnp.float32
BF16 = jnp.bfloat16

D_MODEL = 1024
GRID_W = 64
GLA_HEADS = 4
DK_HEAD = 128
DV_HEAD = 256
GLA_DK = GLA_HEADS * DK_HEAD
GLA_DV = GLA_HEADS * DV_HEAD
DECAY_RANK = 16
GATE_NORMALIZER = 16.0
FNET_GROUPS = 4
FNET_GROUP_DIM = 128
FNET_DIM = FNET_GROUPS * FNET_GROUP_DIM
N_EXPERTS = 32
TOP_K = 4
D_EXPERT = 1024
SWIGLU_LIMIT = 7.0
SWIGLU_ALPHA = 1.702
LN_EPS = 1e-6
N_MOD = 6

LANES = 128
SUBLANES = 8
HALF_MODEL = D_MODEL // 2
ROW_TILES = HALF_MODEL // LANES
ROW_DTYPE = jnp.uint32
COL_Q = 0
COL_K = GLA_DK
COL_V = 2 * GLA_DK
COL_G = COL_V + GLA_DV
COL_GATE_A = COL_G + GLA_DV
COL_GATE_B = COL_GATE_A + D_MODEL
COL_F = COL_GATE_B + D_MODEL
COL_R = COL_F + FNET_DIM
PROJ_COLS = COL_R + LANES

GLA_CHUNK = 128
GLA_LEAF = 16
GLA_UNROLL = 8
FNET_ROWS = 1024
assert GLA_CHUNK == DK_HEAD
GLA_FAST_MAX_STEP_DECAY = 8.0
TM_PROJ = 512
PROJ_GROUP = 256
TM_MIX = 512
TM_MERGE = 512
TM_MERGE_GROUP = 256
TM_ROUTE = 1024
MOE_ROWS = 512
MOE_STEP = 128
VMEM_LIMIT = 56 * 1024 * 1024

SC_CORES = 2
SC_SUBCORES = 16
SC_WORKERS = SC_CORES * SC_SUBCORES
SC_WINDOW = 64

HIGHEST = lax.Precision.HIGHEST


def _layer_norm(x):
    mu = jnp.mean(x, axis=-1, keepdims=True)
    xc = x - mu
    var = jnp.mean(xc * xc, axis=-1, keepdims=True)
    return xc * lax.rsqrt(var + LN_EPS)


def _sigmoid(x):
    return 0.5 * jnp.tanh(0.5 * x) + 0.5


def _log_sigmoid(z):
    return jnp.minimum(z, 0.0) - jnp.log(1.0 + jnp.exp(-jnp.abs(z)))


def _dot(a, b):
    return jnp.dot(a, b, preferred_element_type=F32)


def _split_bf16(x, terms):
    parts = []
    for _ in range(terms):
        p = x.astype(BF16)
        parts.append(p)
        x = x - p.astype(F32)
    return parts


def _dot_nt(a, b):
    return lax.dot_general(a, b, (((1,), (1,)), ((), ())), preferred_element_type=F32)


def _dot_tn(a, b):
    return lax.dot_general(a, b, (((0,), (0,)), ((), ())), preferred_element_type=F32)


def _row_tile_slice(j, n_rows, first_row=0):
    return pl.ds(first_row * ROW_TILES + j, n_rows, stride=ROW_TILES)


def _store_row_tiles(ref, val, first_row=0):
    for j in range(ROW_TILES):
        lo = val[:, j * LANES:(j + 1) * LANES]
        hi = val[:, HALF_MODEL + j * LANES:HALF_MODEL + (j + 1) * LANES]
        ref[_row_tile_slice(j, val.shape[0], first_row), :] = pltpu.pack_elementwise(
            [lo, hi], packed_dtype=BF16)


def _load_row_tile(ref, j, n_rows, first_row=0):
    words = ref[_row_tile_slice(j, n_rows, first_row), :]
    return tuple(pltpu.unpack_elementwise(words, index=half, packed_dtype=BF16, unpacked_dtype=F32)
                 for half in range(2))


def _ada_kernel(c_ref, w_ref, b_ref, o_ref):
    c = c_ref[...]
    s = c * _sigmoid(c)
    o_ref[...] = _dot(s.astype(BF16), w_ref[...].astype(BF16)) + b_ref[...]


def _ada(cond, w_ada, b_ada):
    rows = cond.shape[0]
    n = w_ada.shape[1]
    tn = 1536
    return pl.pallas_call(
        _ada_kernel,
        grid=(n // tn,),
        in_specs=[pl.BlockSpec((rows, D_MODEL), lambda j: (0, 0)),
                  pl.BlockSpec((D_MODEL, tn), lambda j: (0, j)),
                  pl.BlockSpec((1, tn), lambda j: (0, j))],
        out_specs=pl.BlockSpec((rows, tn), lambda j: (0, j)),
        out_shape=jax.ShapeDtypeStruct((rows, n), F32),
        compiler_params=pltpu.CompilerParams(vmem_limit_bytes=VMEM_LIMIT),
        name="ada_mod",
    )(cond, w_ada, b_ada.reshape(1, n))


def _group_maps(n_ctx_tiles):
    def ctx_map(i, *_):
        return (jnp.minimum(i, n_ctx_tiles - 1), 0)

    def lat_map(i, *_):
        return (jnp.maximum(i - n_ctx_tiles, 0), 0)

    return ctx_map, lat_map


def _token_specs(tm, n_ctx_tiles, tiles_per_latent_seq, ctx_mod_row):
    ctx_map, lat_map = _group_maps(n_ctx_tiles)

    def pos_map(i, *_):
        return (jnp.maximum(i - n_ctx_tiles, 0) % tiles_per_latent_seq, 0)

    def mod_map(i, *_):
        return (jnp.where(i < n_ctx_tiles, ctx_mod_row,
                          jnp.maximum(i - n_ctx_tiles, 0) // tiles_per_latent_seq), 0, 0)

    return [pl.BlockSpec((tm, D_MODEL), ctx_map),
            pl.BlockSpec((tm, D_MODEL), lat_map),
            pl.BlockSpec((tm, D_MODEL), pos_map),
            pl.BlockSpec((1, N_MOD, D_MODEL), mod_map)]


def _inproj_kernel(xc_ref, xl_ref, pos_ref, mod_ref, wm_ref, wt_ref, wdh_ref, wdl_ref, bd_ref,
                   o_ref, la_ref, lamin_ref, *, n_ctx_tiles):
    tail = lambda lo, hi: wt_ref[lo - COL_GATE_A:hi - COL_GATE_A, :]
    i = pl.program_id(0)

    def project(x_of):
        groups = [slice(r0, r0 + PROJ_GROUP) for r0 in range(0, TM_PROJ, PROJ_GROUP)]
        hs = [(_layer_norm(x_of(rows)) * (1.0 + mod_ref[0, 1:2, :]) + mod_ref[0, 0:1, :]).astype(BF16)
              for rows in groups]
        rs = [_split_bf16(_dot_nt(h, tail(COL_R, PROJ_COLS)), 2) for h in hs]
        steepest = None
        for rows, (r_hi, r_lo) in zip(groups, rs):
            for d in range(2):
                z = (_dot(r_hi, wdh_ref[d]) + _dot(r_lo, wdh_ref[d]) + _dot(r_hi, wdl_ref[d])
                     + bd_ref[d])
                la = _log_sigmoid(z) * (1.0 / GATE_NORMALIZER)
                la_ref[rows, d * GLA_DK:(d + 1) * GLA_DK] = la
                low = jnp.min(la, axis=0, keepdims=True)
                for c0 in range(0, GLA_DK, LANES):
                    piece = low[:, c0:c0 + LANES]
                    steepest = piece if steepest is None else jnp.minimum(steepest, piece)
        lamin_ref[...] = jnp.broadcast_to(steepest, (SUBLANES, LANES))
        for rows, h in zip(groups, hs):
            hg = 0.5 * _dot_nt(h, wm_ref[COL_G:COL_GATE_A, :])
            o_ref[rows, COL_G:COL_GATE_A] = hg * (jnp.tanh(hg) + 1.0)
        for rows, h in zip(groups, hs):
            gates = _dot_nt(h, tail(COL_GATE_A, COL_F))
            o_ref[rows, COL_GATE_A:COL_F] = jnp.tanh(0.5 * gates) + 1.0
        for rows, h in zip(groups, hs):
            o_ref[rows, :COL_G] = _dot_nt(h, wm_ref[:COL_G, :])
        for rows, h in zip(groups, hs):
            o_ref[rows, COL_F:COL_R] = _dot_nt(h, tail(COL_F, COL_R))

    @pl.when(i < n_ctx_tiles)
    def _():
        project(lambda rows: xc_ref[rows, :])

    @pl.when(i >= n_ctx_tiles)
    def _():
        project(lambda rows: xl_ref[rows, :] + pos_ref[rows, :])


def _inproj(x_ctx, x_lat, pos, mod, w_main, w_tail, w_dec, b_dec, lat_len):
    t_ctx, t_lat = x_ctx.shape[0], x_lat.shape[0]
    t_all = t_ctx + t_lat
    n_ctx_tiles = t_ctx // TM_PROJ
    kern = functools.partial(_inproj_kernel, n_ctx_tiles=n_ctx_tiles)
    specs = _token_specs(TM_PROJ, n_ctx_tiles, lat_len // TM_PROJ, mod.shape[0] - 1)
    w_dec_hi = w_dec.astype(BF16)
    w_dec_lo = (w_dec - w_dec_hi.astype(F32)).astype(BF16)

    def const(shape):
        return pl.BlockSpec(shape, lambda i: (0,) * len(shape))

    return pl.pallas_call(
        kern,
        grid=(t_all // TM_PROJ,),
        in_specs=specs + [pl.BlockSpec(w_main.shape, lambda i: (0, 0), pipeline_mode=pl.Buffered(1)),
                          pl.BlockSpec(w_tail.shape, lambda i: (0, 0), pipeline_mode=pl.Buffered(1)),
                          const((2, LANES, GLA_DK)), const((2, LANES, GLA_DK)), const((2, 1, GLA_DK))],
        out_specs=[pl.BlockSpec((TM_PROJ, COL_R), lambda i: (i, 0)),
                   pl.BlockSpec((TM_PROJ, 2 * GLA_DK), lambda i: (i, 0)),
                   pl.BlockSpec((SUBLANES, LANES), lambda i: (i, 0))],
        out_shape=[jax.ShapeDtypeStruct((t_all, COL_R), F32),
                   jax.ShapeDtypeStruct((t_all, 2 * GLA_DK), F32),
                   jax.ShapeDtypeStruct((t_all // TM_PROJ * SUBLANES, LANES), F32)],
        compiler_params=pltpu.CompilerParams(
            dimension_semantics=("arbitrary",), vmem_limit_bytes=VMEM_LIMIT),
        name="ln_inproj",
    )(x_ctx, x_lat, pos, mod, w_main, w_tail, w_dec_hi, w_dec_lo, b_dec)


def _gla_kernel(*refs, seq_len, seqs, has_s0, emit_state, exact):
    it = iter(refs)
    q_ref, k_ref, v_ref, laf_ref, lab_ref, g_ref = (next(it) for _ in range(6))
    s0_ref = next(it) if has_s0 else None
    o_ref = next(it)
    sout_ref = next(it) if emit_state else None
    cum_ref, aq_ref, ko_ref, dec_ref, op_ref, st_ref = (next(it) for _ in range(6))

    C = GLA_CHUNK
    n_chunks = seq_len // C
    all_chunks = seqs * n_chunks
    assert all_chunks <= GLA_UNROLL or (seqs == 1 and n_chunks % GLA_UNROLL == 0)

    def rows(n):
        if isinstance(n, int):
            return pl.ds(n * C, C)
        return pl.ds(pl.multiple_of(n * C, C), C)

    def loop(body, count=all_chunks):
        if count <= GLA_UNROLL:
            for n in range(count):
                body(n)
        else:
            def step(m, carry):
                for u in range(GLA_UNROLL):
                    body(GLA_UNROLL * m + u)
                return carry
            lax.fori_loop(0, count // GLA_UNROLL, step, 0)

    rt = lax.broadcasted_iota(jnp.int32, (C, C), 0)
    ct = lax.broadcasted_iota(jnp.int32, (C, C), 1)
    tri = ((rt >= ct).astype(BF16), (ct >= rt).astype(BF16))
    row_id = lax.broadcasted_iota(jnp.int32, (C, DK_HEAD), 0)

    def cumsum_chunk(n):
        for d, la_ref in enumerate((laf_ref, lab_ref)):
            la_hi, la_lo = _split_bf16(la_ref[rows(n), :], 2)
            cum_ref[d, rows(n), :] = _dot(tri[d], la_hi) + _dot(tri[d], la_lo)

    loop(cumsum_chunk)

    query_rows = ({}, {})
    keep = ({}, {})
    for d in range(2):
        blk = C // 2
        while blk >= GLA_LEAF:
            q_parity = 1 if d == 0 else 0
            query_rows[d][blk] = ((row_id // blk) % 2) == q_parity
            qb, kb = rt // blk, ct // blk
            keep[d][blk] = ((qb % 2) == q_parity) & ((qb == kb + 1) if d == 0 else (kb == qb + 1))
            blk //= 2
        order = (rt >= ct) if d == 0 else (ct >= rt)
        keep[d][0] = ((rt // GLA_LEAF) == (ct // GLA_LEAF)) & order

    n_leaves = C // GLA_LEAF

    def per_leaf(rows_of_cum):
        return jnp.concatenate(rows_of_cum, axis=0)

    def expand(per_leaf_rows):
        return jnp.concatenate(
            [jnp.broadcast_to(per_leaf_rows[j:j + 1, :], (GLA_LEAF, DK_HEAD)) for j in range(n_leaves)],
            axis=0)

    def store_scores(n, d, acc, q_in, k_out, end):
        aq_ref[d, rows(n), 0:C] = acc.astype(BF16)
        aq_ref[d, rows(n), C:C + DK_HEAD] = q_in.astype(BF16)
        ko_ref[d, rows(n), :] = k_out.astype(BF16)
        dec_ref[d, rows(n), :] = jnp.broadcast_to(jnp.exp(end), (DK_HEAD, DK_HEAD)).T

    def block_level_scores(cum, q, k, d, blk, at_bnd_scale):
        x = (jnp.where(query_rows[d][blk], q, k) * at_bnd_scale).astype(BF16)
        return jnp.where(keep[d][blk], _dot_nt(x, x), 0.0)

    def boundary_rows(cum, d, blk):
        bnd = blk - 1 if d == 0 else blk
        return per_leaf([cum[bnd + (j * GLA_LEAF) // (2 * blk) * (2 * blk):
                             bnd + (j * GLA_LEAF) // (2 * blk) * (2 * blk) + 1, :]
                         for j in range(n_leaves)])

    def scores_exact(n, d):
        cum = cum_ref[d, rows(n), :]
        q = q_ref[rows(n), :]
        k = k_ref[rows(n), :]
        in_leaf_pos = row_id % GLA_LEAF
        acc = jnp.zeros((C, C), F32)
        for lag in range(GLA_LEAF):
            shift = lag if d == 0 else (C - lag) % C
            k_s = pltpu.roll(k, shift, 0) if lag else k
            cum_s = pltpu.roll(cum, shift, 0) if lag else cum
            paired = (in_leaf_pos >= lag) if d == 0 else (in_leaf_pos < GLA_LEAF - lag)
            term = q * k_s * jnp.exp(jnp.where(paired, cum - cum_s, 0.0))
            s = jnp.sum(jnp.where(paired, term, 0.0), axis=-1, keepdims=True)
            diagonal = (rt - ct == lag) if d == 0 else (ct - rt == lag)
            acc = acc + jnp.where(diagonal, s, 0.0)
        blk = C // 2
        while blk >= GLA_LEAF:
            w = jnp.exp(-jnp.abs(cum - expand(boundary_rows(cum, d, blk))))
            acc = acc + block_level_scores(cum, q, k, d, blk, w)
            blk //= 2
        end = cum[C - 1:C, :] if d == 0 else cum[0:1, :]
        store_scores(n, d, acc, q * jnp.exp(cum), k * jnp.exp(end - cum), end)

    def scores(n, d):
        cum = cum_ref[d, rows(n), :]
        mid = GLA_LEAF // 2 - 1 if d == 0 else GLA_LEAF // 2
        at_mid = per_leaf([cum[mid + j * GLA_LEAF:mid + j * GLA_LEAF + 1, :] for j in range(n_leaves)])
        e = cum - expand(at_mid)
        qe = q_ref[rows(n), :] * jnp.exp(e)
        ke = k_ref[rows(n), :] * jnp.exp(-e)
        acc = jnp.where(keep[d][0], _dot_nt(qe.astype(BF16), ke.astype(BF16)), 0.0)
        blk = C // 2
        while blk >= GLA_LEAF:
            through = expand(jnp.exp(-jnp.abs(at_mid - boundary_rows(cum, d, blk))))
            acc = acc + block_level_scores(cum, qe, ke, d, blk, through)
            blk //= 2
        end = cum[C - 1:C, :] if d == 0 else cum[0:1, :]
        store_scores(n, d, acc, qe * expand(jnp.exp(at_mid)), ke * expand(jnp.exp(end - at_mid)), end)

    def scores_chunk(n):
        for d in range(2):
            (scores_exact if exact else scores)(n, d)

    loop(scores_chunk)

    for s in range(seqs):
        for d in range(2):
            if has_s0:
                st_ref[2 * s + d] = s0_ref[s, d, 0]
            else:
                st_ref[2 * s + d] = jnp.zeros((DK_HEAD, DV_HEAD), F32)

    def scan(s, m, d):
        n = s * n_chunks + (m if d == 0 else n_chunks - 1 - m)
        v = v_ref[rows(n), :].astype(BF16)
        st = st_ref[2 * s + d]
        op_ref[d, rows(n), :] = _dot(aq_ref[d, rows(n), :],
                                     jnp.concatenate([v, st.astype(BF16)], axis=0))
        dec = dec_ref[d, rows(n), :]
        st_ref[2 * s + d] = (st * jnp.concatenate([dec] * (DV_HEAD // DK_HEAD), axis=1)
                             + _dot_tn(ko_ref[d, rows(n), :], v))

    def scan_step(m):
        for s in range(seqs):
            for d in range(2):
                scan(s, m, d)

    loop(scan_step, n_chunks)
    if emit_state:
        for s in range(seqs):
            for d in range(2):
                sout_ref[s, d, 0] = st_ref[2 * s + d]

    g = g_ref[...]

    def finish_chunk(n):
        o = op_ref[0, rows(n), :] + op_ref[1, rows(n), :]
        ms = jnp.mean(o * o, axis=-1, keepdims=True)
        o_ref[rows(n), :] = o * lax.rsqrt(ms + LN_EPS) * g

    loop(finish_chunk)


def _gla(proj, la, g, s0, *, n_seq, seq_len, row0, emit_state, exact):
    has_s0 = s0 is not None
    seqs = max(1, min(n_seq, GLA_UNROLL // max(1, seq_len // GLA_CHUNK)))
    assert n_seq % seqs == 0 and row0 % (seqs * seq_len) == 0
    blk_rows = seqs * seq_len
    blk0 = row0 // blk_rows
    kern = functools.partial(_gla_kernel, seq_len=seq_len, seqs=seqs, has_s0=has_s0,
                             emit_state=emit_state, exact=exact)
    in_specs = [
        pl.BlockSpec((blk_rows, DK_HEAD), lambda b, h: (blk0 + b, COL_Q // DK_HEAD + h)),
        pl.BlockSpec((blk_rows, DK_HEAD), lambda b, h: (blk0 + b, COL_K // DK_HEAD + h)),
        pl.BlockSpec((blk_rows, DV_HEAD), lambda b, h: (blk0 + b, COL_V // DV_HEAD + h)),
        pl.BlockSpec((blk_rows, DK_HEAD), lambda b, h: (blk0 + b, h)),
        pl.BlockSpec((blk_rows, DK_HEAD), lambda b, h: (blk0 + b, GLA_HEADS + h)),
        pl.BlockSpec((1, DV_HEAD), lambda b, h: (0, 0)),
    ]
    args = [proj, proj, proj, la, la, g]
    state_spec = pl.BlockSpec((seqs, 2, 1, DK_HEAD, DV_HEAD), lambda b, h: (b, 0, h, 0, 0))
    if has_s0:
        in_specs.append(state_spec)
        args.append(s0)
    out_specs = [pl.BlockSpec((blk_rows, DV_HEAD), lambda b, h: (b, h))]
    out_shape = [jax.ShapeDtypeStruct((n_seq * seq_len, GLA_DV), F32)]
    if emit_state:
        out_specs.append(state_spec)
        out_shape.append(jax.ShapeDtypeStruct((n_seq, 2, GLA_HEADS, DK_HEAD, DV_HEAD), F32))

    res = pl.pallas_call(
        kern,
        grid=(n_seq // seqs, GLA_HEADS),
        in_specs=in_specs,
        out_specs=out_specs,
        out_shape=out_shape,
        scratch_shapes=[pltpu.VMEM((2, blk_rows, DK_HEAD), F32),
                        pltpu.VMEM((2, blk_rows, GLA_CHUNK + DK_HEAD), BF16),
                        pltpu.VMEM((2, blk_rows, DK_HEAD), BF16),
                        pltpu.VMEM((2, blk_rows, DK_HEAD), F32),
                        pltpu.VMEM((2, blk_rows, DV_HEAD), F32),
                        pltpu.VMEM((2 * seqs, DK_HEAD, DV_HEAD), F32)],
        compiler_params=pltpu.CompilerParams(
            dimension_semantics=("arbitrary", "arbitrary"), vmem_limit_bytes=VMEM_LIMIT),
        name="gla%s_seq%d" % ("_exact" if exact else "", seq_len),
    )(*args)
    return res


def _fnet_kernel(f_ref, cl_ref, sl_ref, cg_ref, sg_ref, o_ref, uc_ref, us_ref, *, seq_len, seqs):
    cg = cg_ref[...]
    sg = sg_ref[...]
    for grp in range(FNET_GROUPS):
        lo = grp * FNET_GROUP_DIM
        u = f_ref[:, lo:lo + FNET_GROUP_DIM].astype(BF16)
        uc_ref[:, lo:lo + FNET_GROUP_DIM] = _dot(u, cg).astype(BF16)
        us_ref[:, lo:lo + FNET_GROUP_DIM] = _dot(u, sg).astype(BF16)
    scale = 1.0 / math.sqrt(seq_len * FNET_GROUP_DIM)
    for s in range(seqs):
        r = slice(s * seq_len, (s + 1) * seq_len)
        o_ref[r, :] = (_dot(cl_ref[...], uc_ref[r, :]) - _dot(sl_ref[...], us_ref[r, :])) * scale


def _dft_mats(n):
    j = np.arange(n, dtype=np.int64)
    ang = (2.0 * np.pi / n) * ((j[:, None] * j[None, :]) % n).astype(np.float64)
    return (jnp.asarray(np.cos(ang), dtype=F32).astype(BF16),
            jnp.asarray(np.sin(ang), dtype=F32).astype(BF16))


def _fnet(proj, *, n_seq, seq_len, row0):
    seqs = max(1, min(n_seq, FNET_ROWS // seq_len))
    assert n_seq % seqs == 0 and row0 % (seqs * seq_len) == 0
    blk_rows = seqs * seq_len
    blk0 = row0 // blk_rows
    cl, sl = _dft_mats(seq_len)
    cg, sg = _dft_mats(FNET_GROUP_DIM)
    kern = functools.partial(_fnet_kernel, seq_len=seq_len, seqs=seqs)
    return pl.pallas_call(
        kern,
        grid=(n_seq // seqs,),
        in_specs=[pl.BlockSpec((blk_rows, FNET_DIM), lambda b: (blk0 + b, COL_F // FNET_DIM)),
                  pl.BlockSpec((seq_len, seq_len), lambda b: (0, 0)),
                  pl.BlockSpec((seq_len, seq_len), lambda b: (0, 0)),
                  pl.BlockSpec((FNET_GROUP_DIM, FNET_GROUP_DIM), lambda b: (0, 0)),
                  pl.BlockSpec((FNET_GROUP_DIM, FNET_GROUP_DIM), lambda b: (0, 0))],
        out_specs=pl.BlockSpec((blk_rows, FNET_DIM), lambda b: (b, 0)),
        out_shape=jax.ShapeDtypeStruct((n_seq * seq_len, FNET_DIM), F32),
        scratch_shapes=[pltpu.VMEM((blk_rows, FNET_DIM), BF16),
                        pltpu.VMEM((blk_rows, FNET_DIM), BF16)],
        compiler_params=pltpu.CompilerParams(
            dimension_semantics=("arbitrary",), vmem_limit_bytes=VMEM_LIMIT),
        name="fnet_seq%d" % seq_len,
    )(proj, cl, sl, cg, sg)


def _merge_kernel(xc_ref, xl_ref, pos_ref, mod_ref, oc_ref, ol_ref, mc_ref, ml_ref,
                  g_ref, ga_ref, gb_ref, wbg_ref, wbf_ref, wo_ref, l1g_ref, l1b_ref, wrh_ref, wrl_ref,
                  br_ref, x1_ref, h2_ref, ridx_ref, rw_ref, *, n_ctx_tiles, alpha):
    i = pl.program_id(0)

    tm = x1_ref.shape[0]
    groups = [slice(r0, r0 + TM_MERGE_GROUP) for r0 in range(0, tm, TM_MERGE_GROUP)]
    shape = (TM_MERGE_GROUP, LANES)
    lane_i = lax.broadcasted_iota(jnp.int32, shape, 1)
    lane = lane_i.astype(F32)

    def compute(x_of, o_ref, mx_ref):
        branch = []
        for rows in groups:
            a = (o_ref[rows, :] * g_ref[rows, :]).astype(BF16)
            branch.append((_dot(a, wbg_ref[...]), _dot(mx_ref[rows, :].astype(BF16), wbf_ref[...])))
        mix = []
        for rows, (gla_out, fnet_out) in zip(groups, branch):
            merged = ga_ref[rows, :] * gla_out + gb_ref[rows, :] * fnet_out
            mix.append(_dot(merged.astype(BF16), wo_ref[...]))
        logits = []
        for rows, mix_g in zip(groups, mix):
            y = alpha * x_of(rows) + mod_ref[0, 2:3, :] * mix_g
            x1 = _layer_norm(y) * l1g_ref[...] + l1b_ref[...]
            x1_ref[rows, :] = x1
            h2 = _layer_norm(x1) * (1.0 + mod_ref[0, 4:5, :]) + mod_ref[0, 3:4, :]
            _store_row_tiles(h2_ref, h2, rows.start)
            h_hi, h_lo = _split_bf16(h2, 2)
            logits.append(_dot(h_hi, wrh_ref[...]) + _dot(h_lo, wrh_ref[...])
                          + _dot(h_hi, wrl_ref[...]) + br_ref[...])
        idx_out = [jnp.zeros(shape, F32) for _ in groups]
        val_out = [jnp.zeros(shape, F32) for _ in groups]
        top0 = [None] * len(groups)
        denom = [None] * len(groups)
        for kk in range(TOP_K):
            for gi in range(len(groups)):
                m = jnp.max(logits[gi], axis=-1, keepdims=True)
                sel = jnp.min(jnp.where(logits[gi] == m, lane, float(LANES)), axis=-1, keepdims=True)
                if kk == 0:
                    top0[gi] = m
                    p = jnp.ones_like(m)
                    denom[gi] = p
                else:
                    p = jnp.exp(m - top0[gi])
                    denom[gi] = denom[gi] + p
                idx_out[gi] = jnp.where(lane_i == kk, sel, idx_out[gi])
                val_out[gi] = jnp.where(lane_i == kk, p, val_out[gi])
                logits[gi] = jnp.where(lane == sel, -jnp.inf, logits[gi])
        for gi, rows in enumerate(groups):
            ridx_ref[rows, :] = idx_out[gi].astype(jnp.int32)
            rw_ref[rows, :] = val_out[gi] / denom[gi]

    @pl.when(i < n_ctx_tiles)
    def _():
        compute(lambda rows: xc_ref[rows, :], oc_ref, mc_ref)

    @pl.when(i >= n_ctx_tiles)
    def _():
        compute(lambda rows: xl_ref[rows, :] + pos_ref[rows, :], ol_ref, ml_ref)


def _merge(x_ctx, x_lat, pos, mod, o_ctx, o_lat, mixed_ctx, mixed_lat, proj,
           wbg, wbf, wo, l1g, l1b, wr_hi, wr_lo, br, lat_len, alpha):
    t_ctx, t_lat = x_ctx.shape[0], x_lat.shape[0]
    t_all = t_ctx + t_lat
    tm = TM_MERGE
    n_ctx_tiles = t_ctx // tm
    kern = functools.partial(_merge_kernel, n_ctx_tiles=n_ctx_tiles, alpha=alpha)
    specs = _token_specs(tm, n_ctx_tiles, lat_len // tm, mod.shape[0] - 1)
    ctx_map, lat_map = _group_maps(n_ctx_tiles)

    def const(shape):
        return pl.BlockSpec(shape, lambda i: (0,) * len(shape))

    in_specs = specs + [
        pl.BlockSpec((tm, GLA_DV), ctx_map),
        pl.BlockSpec((tm, GLA_DV), lat_map),
        pl.BlockSpec((tm, FNET_DIM), ctx_map),
        pl.BlockSpec((tm, FNET_DIM), lat_map),
        pl.BlockSpec((tm, GLA_DV), lambda i: (i, COL_G // GLA_DV)),
        pl.BlockSpec((tm, D_MODEL), lambda i: (i, COL_GATE_A // D_MODEL)),
        pl.BlockSpec((tm, D_MODEL), lambda i: (i, COL_GATE_B // D_MODEL)),
        const((GLA_DV, D_MODEL)), const((FNET_DIM, D_MODEL)), const((D_MODEL, D_MODEL)),
        const((1, D_MODEL)), const((1, D_MODEL)),
        const((D_MODEL, LANES)), const((D_MODEL, LANES)), const((1, LANES)),
    ]
    out_specs = [pl.BlockSpec((tm, D_MODEL), lambda i: (i, 0)),
                 pl.BlockSpec((tm * ROW_TILES, LANES), lambda i: (i, 0)),
                 pl.BlockSpec((tm, LANES), lambda i: (i, 0)),
                 pl.BlockSpec((tm, LANES), lambda i: (i, 0))]
    out_shape = [jax.ShapeDtypeStruct((t_all, D_MODEL), F32),
                 jax.ShapeDtypeStruct((t_all * ROW_TILES, LANES), ROW_DTYPE),
                 jax.ShapeDtypeStruct((t_all, LANES), jnp.int32),
                 jax.ShapeDtypeStruct((t_all, LANES), F32)]
    return pl.pallas_call(
        kern,
        grid=(t_all // tm,),
        in_specs=in_specs,
        out_specs=out_specs,
        out_shape=out_shape,
        compiler_params=pltpu.CompilerParams(
            dimension_semantics=("arbitrary",), vmem_limit_bytes=VMEM_LIMIT),
        name="merge_ln1_router",
    )(x_ctx, x_lat, pos, mod, o_ctx, o_lat, mixed_ctx, mixed_lat, proj, proj, proj,
      wbg, wbf, wo, l1g, l1b, wr_hi, wr_lo, br)


def _sc_mesh():
    return plsc.VectorSubcoreMesh(core_axis_name="c", subcore_axis_name="s")


def _sc_worker_id():
    return lax.axis_index("s") * SC_CORES + lax.axis_index("c")


def _sc_scatter_rows(src, idx, n_out):
    n_src = src.shape[0]
    w = SC_WINDOW
    n_chunks = n_src // (SC_WORKERS * w)
    copies = idx.shape[0]
    assert n_chunks % 2 == 0 and idx.shape == (copies, SC_WORKERS, n_chunks, w)

    @functools.partial(
        pl.kernel, mesh=_sc_mesh(),
        out_type=jax.ShapeDtypeStruct((n_out, ROW_TILES, LANES), ROW_DTYPE),
        scratch_types=[pltpu.VMEM((copies * n_chunks, w), jnp.int32),
                       pltpu.VMEM((2, w, ROW_TILES, LANES), ROW_DTYPE),
                       pltpu.SemaphoreType.DMA((2,)),
                       pltpu.SemaphoreType.DMA((2,))],
        name="moe_dispatch_scatter")
    def k(src_hbm, idx_hbm, out_hbm, idx_v, rows_v, rsem, wsem):
        wid = _sc_worker_id()
        base = wid * (n_chunks * w)
        for kk in range(copies):
            pltpu.sync_copy(idx_hbm.at[kk, wid], idx_v.at[pl.ds(kk * n_chunks, n_chunks)])

        def read(j, slot):
            return pltpu.make_async_copy(src_hbm.at[pl.ds(base + j * w, w)], rows_v.at[slot],
                                         rsem.at[slot])

        def scatter(j, kk, slot):
            return pltpu.make_async_copy(rows_v.at[slot], out_hbm.at[idx_v.at[kk * n_chunks + j]],
                                         wsem.at[slot])

        read(0, 0).start()

        @pl.loop(0, n_chunks, step=2)
        def _(jj):
            read(jj, 0).wait()

            @pl.when(jj > 0)
            def _():
                for kk in range(copies):
                    scatter(jj - 1, kk, 1).wait()

            read(jj + 1, 1).start()
            for kk in range(copies):
                scatter(jj, kk, 0).start()
            read(jj + 1, 1).wait()
            for kk in range(copies):
                scatter(jj, kk, 0).wait()

            @pl.when(jj + 2 < n_chunks)
            def _():
                read(jj + 2, 0).start()

            for kk in range(copies):
                scatter(jj + 1, kk, 1).start()

        for kk in range(copies):
            scatter(n_chunks - 1, kk, 1).wait()

    return k(src, idx)


def _sc_gather_rows(table, idx):
    _, n_chunks, w = idx.shape
    assert n_chunks % 2 == 0 and idx.shape[0] == SC_WORKERS and w == SC_WINDOW
    n_out = SC_WORKERS * n_chunks * w

    @functools.partial(
        pl.kernel, mesh=_sc_mesh(),
        out_type=jax.ShapeDtypeStruct((n_out, ROW_TILES, LANES), ROW_DTYPE),
        scratch_types=[pltpu.VMEM((n_chunks, w), jnp.int32),
                       pltpu.VMEM((2, w, ROW_TILES, LANES), ROW_DTYPE),
                       pltpu.SemaphoreType.DMA((2,)),
                       pltpu.SemaphoreType.DMA((2,))],
        name="moe_combine_gather")
    def k(table_hbm, idx_hbm, out_hbm, idx_v, rows_v, gsem, wsem):
        wid = _sc_worker_id()
        base = wid * (n_chunks * w)
        pltpu.sync_copy(idx_hbm.at[wid], idx_v)

        def gather(j, slot):
            return pltpu.make_async_copy(table_hbm.at[idx_v.at[j]], rows_v.at[slot], gsem.at[slot])

        def write(j, slot):
            return pltpu.make_async_copy(rows_v.at[slot], out_hbm.at[pl.ds(base + j * w, w)],
                                         wsem.at[slot])

        gather(0, 0).start()

        @pl.loop(0, n_chunks, step=2)
        def _(jj):
            gather(jj, 0).wait()

            @pl.when(jj > 0)
            def _():
                write(jj - 1, 1).wait()

            gather(jj + 1, 1).start()
            write(jj, 0).start()
            gather(jj + 1, 1).wait()
            write(jj, 0).wait()

            @pl.when(jj + 2 < n_chunks)
            def _():
                gather(jj + 2, 0).start()

            write(jj + 1, 1).start()

        write(n_chunks - 1, 1).wait()

    return k(table, idx)


def _moe_kernel(be_ref, nu_ref, nv_ref, slot_ref, nxt_ref, x_ref, wgu_hbm, bgu_ref, wd_hbm, bd_ref,
                o_ref, wgu_st, wd_st, wgu_bf, wd_bf, xb_ref, sem):
    b = pl.program_id(0)
    e = be_ref[b]
    prev = be_ref[jnp.maximum(b - 1, 0)]
    active = b < nu_ref[0]
    changed = (b == 0) | (e != prev)

    def weight_copies(expert, s):
        return (pltpu.make_async_copy(wgu_hbm.at[expert], wgu_st.at[s], sem.at[0, s]),
                pltpu.make_async_copy(wd_hbm.at[expert], wd_st.at[s], sem.at[1, s]))

    @pl.when(active & changed)
    def _():
        s = slot_ref[b]

        @pl.when(b == 0)
        def _():
            for cp in weight_copies(e, s):
                cp.start()

        for cp in weight_copies(e, s):
            cp.wait()
        wgu_bf[...] = wgu_st[s].astype(BF16)
        wd_bf[...] = wd_st[s].astype(BF16)
        nxt = nxt_ref[b]

        @pl.when(nxt >= 0)
        def _():
            for cp in weight_copies(nxt, 1 - s):
                cp.start()

    n_valid = nv_ref[b]

    def expert_mlp(n_rows):
        valid = lax.broadcasted_iota(jnp.int32, (n_rows, LANES), 0) < n_valid
        for j in range(ROW_TILES):
            for half, xj in enumerate(_load_row_tile(x_ref, j, n_rows)):
                c0 = half * HALF_MODEL + j * LANES
                xb_ref[0:n_rows, c0:c0 + LANES] = jnp.where(valid, xj, 0.0).astype(BF16)
        gu = _dot(xb_ref[0:n_rows, :], wgu_bf[...]) + bgu_ref[0]
        gate = jnp.minimum(gu[:, :D_EXPERT], SWIGLU_LIMIT)
        up = jnp.clip(gu[:, D_EXPERT:], -SWIGLU_LIMIT, SWIGLU_LIMIT)
        glu = gate * _sigmoid(gate * SWIGLU_ALPHA)
        act = ((up + 1.0) * glu).astype(BF16)
        _store_row_tiles(o_ref, _dot(act, wd_bf[...]) + bd_ref[0])

    for n_rows in range(MOE_STEP, MOE_ROWS + 1, MOE_STEP):
        @pl.when(active & (n_valid > n_rows - MOE_STEP) & (n_valid <= n_rows))
        def _(n_rows=n_rows):
            expert_mlp(n_rows)


def _moe(tables, xs, w_gate_up, b_gate_up, w_down, b_down):
    p_rows = xs.shape[0] // ROW_TILES
    n_blocks = p_rows // MOE_ROWS

    def blk(b, be, nu, *_):
        return jnp.minimum(b, nu[0] - 1)

    def expert(b, be, nu, *_):
        return (be[blk(b, be, nu)], 0, 0)

    def rows(b, be, nu, *_):
        return (blk(b, be, nu), 0)

    grid_spec = pltpu.PrefetchScalarGridSpec(
        num_scalar_prefetch=len(tables),
        grid=(n_blocks,),
        in_specs=[
            pl.BlockSpec((MOE_ROWS * ROW_TILES, LANES), rows),
            pl.BlockSpec(memory_space=pl.ANY),
            pl.BlockSpec((1, 1, 2 * D_EXPERT), expert),
            pl.BlockSpec(memory_space=pl.ANY),
            pl.BlockSpec((1, 1, D_MODEL), expert),
        ],
        out_specs=pl.BlockSpec((MOE_ROWS * ROW_TILES, LANES), rows),
        scratch_shapes=[pltpu.VMEM((2, D_MODEL, 2 * D_EXPERT), F32),
                        pltpu.VMEM((2, D_EXPERT, D_MODEL), F32),
                        pltpu.VMEM((D_MODEL, 2 * D_EXPERT), BF16),
                        pltpu.VMEM((D_EXPERT, D_MODEL), BF16),
                        pltpu.VMEM((MOE_ROWS, D_MODEL), BF16),
                        pltpu.SemaphoreType.DMA((2, 2))],
    )
    return pl.pallas_call(
        _moe_kernel,
        grid_spec=grid_spec,
        out_shape=jax.ShapeDtypeStruct((p_rows * ROW_TILES, LANES), ROW_DTYPE),
        compiler_params=pltpu.CompilerParams(
            dimension_semantics=("arbitrary",), vmem_limit_bytes=VMEM_LIMIT),
        name="moe_grouped_mlp",
    )(*tables, xs, w_gate_up, b_gate_up.reshape(N_EXPERTS, 1, 2 * D_EXPERT), w_down,
      b_down.reshape(N_EXPERTS, 1, D_MODEL))


def _combine_kernel(x1_ref, y0_ref, y1_ref, y2_ref, y3_ref, rw_ref, mod_ref, g_ref, b_ref, o_ref,
                    *, alpha):
    rw = rw_ref[...]
    y_refs = (y0_ref, y1_ref, y2_ref, y3_ref)
    pieces = [None] * (2 * ROW_TILES)
    for j in range(ROW_TILES):
        for kk in range(TOP_K):
            for half, yj in enumerate(_load_row_tile(y_refs[kk], j, rw.shape[0])):
                term = rw[:, kk:kk + 1] * yj
                slot = half * ROW_TILES + j
                pieces[slot] = term if kk == 0 else pieces[slot] + term
    ff = jnp.concatenate(pieces, axis=-1)
    y = alpha * x1_ref[...] + mod_ref[0, 5:6, :] * ff
    o_ref[...] = _layer_norm(y) * g_ref[...] + b_ref[...]


def _combine(x1, yg, rw, mod, l2g, l2b, *, row0, n_rows, mod_map, alpha):
    tm = TM_MIX
    t0 = row0 // tm
    tiles = n_rows // tm
    kern = functools.partial(_combine_kernel, alpha=alpha)

    def y_spec(kk):
        return pl.BlockSpec((tm * ROW_TILES, LANES), lambda i: (kk * tiles + i, 0))

    return pl.pallas_call(
        kern,
        grid=(n_rows // tm,),
        in_specs=[pl.BlockSpec((tm, D_MODEL), lambda i: (t0 + i, 0))]
        + [y_spec(kk) for kk in range(TOP_K)]
        + [pl.BlockSpec((tm, LANES), lambda i: (t0 + i, 0)),
           pl.BlockSpec((1, N_MOD, D_MODEL), mod_map),
           pl.BlockSpec((1, D_MODEL), lambda i: (0, 0)),
           pl.BlockSpec((1, D_MODEL), lambda i: (0, 0))],
        out_specs=pl.BlockSpec((tm, D_MODEL), lambda i: (i, 0)),
        out_shape=jax.ShapeDtypeStruct((n_rows, D_MODEL), F32),
        compiler_params=pltpu.CompilerParams(
            dimension_semantics=("arbitrary",), vmem_limit_bytes=VMEM_LIMIT),
        name="combine_ln2",
    )(x1, yg, yg, yg, yg, rw, mod, l2g, l2b)


def _route_kernel(ridx_ref, dest_ref, cnt_ref, run_ref, bst_ref):
    phase = pl.program_id(0)
    i = pl.program_id(1)
    tm = ridx_ref.shape[0]
    ridx = ridx_ref[...]
    lane = lax.broadcasted_iota(jnp.int32, (tm, LANES), 1)
    hits = [ridx[:, kk:kk + 1] == lane for kk in range(TOP_K)]
    chosen = jnp.where(hits[0], 1.0, 0.0)
    for kk in range(1, TOP_K):
        chosen = chosen + jnp.where(hits[kk], 1.0, 0.0)
    colsum = jnp.sum(chosen, axis=0, keepdims=True)

    @pl.when((phase == 0) & (i == 0))
    def _():
        run_ref[...] = jnp.zeros_like(run_ref)

    @pl.when(phase == 0)
    def _():
        run_ref[...] = run_ref[...] + colsum

    @pl.when((phase == 1) & (i == 0))
    def _():
        counts = run_ref[...]
        cnt_ref[...] = counts
        blocks = jnp.floor((counts + (MOE_ROWS - 1.0)) * (1.0 / MOE_ROWS))
        r = lax.broadcasted_iota(jnp.int32, (LANES, LANES), 0)
        c = lax.broadcasted_iota(jnp.int32, (LANES, LANES), 1)
        before = jnp.dot(blocks, (r < c).astype(F32), precision=HIGHEST, preferred_element_type=F32)
        bst_ref[...] = before * float(MOE_ROWS)
        run_ref[...] = jnp.zeros_like(run_ref)

    @pl.when(phase == 1)
    def _():
        rt = lax.broadcasted_iota(jnp.int32, (tm, tm), 0)
        ct = lax.broadcasted_iota(jnp.int32, (tm, tm), 1)
        earlier = _dot((ct < rt).astype(BF16), chosen.astype(BF16))
        row_of = bst_ref[0:1, :] + run_ref[0:1, :] + earlier
        out = jnp.take_along_axis(row_of, ridx, axis=1)
        dest_ref[...] = out.T[0:SUBLANES, :].astype(jnp.int32)
        run_ref[...] = run_ref[...] + colsum


def _route(ridx):
    t_all = ridx.shape[0]
    tm = TM_ROUTE
    return pl.pallas_call(
        _route_kernel,
        grid=(2, t_all // tm),
        in_specs=[pl.BlockSpec((tm, LANES), lambda p, i: (i, 0))],
        out_specs=[pl.BlockSpec((SUBLANES, tm), lambda p, i: (0, i * p)),
                   pl.BlockSpec((SUBLANES, LANES), lambda p, i: (0, 0))],
        out_shape=[jax.ShapeDtypeStruct((SUBLANES, t_all), jnp.int32),
                   jax.ShapeDtypeStruct((SUBLANES, LANES), F32)],
        scratch_shapes=[pltpu.VMEM((SUBLANES, LANES), F32),
                        pltpu.VMEM((SUBLANES, LANES), F32)],
        compiler_params=pltpu.CompilerParams(
            dimension_semantics=("arbitrary", "arbitrary"), vmem_limit_bytes=VMEM_LIMIT),
        name="moe_route",
    )(ridx)


def _routing_tables(counts, n_blocks):
    experts = jnp.arange(N_EXPERTS, dtype=jnp.int32)
    blocks_per = (counts + MOE_ROWS - 1) // MOE_ROWS
    bends = jnp.cumsum(blocks_per)
    bstarts = bends - blocks_per
    blocks = jnp.arange(n_blocks, dtype=jnp.int32)
    block_expert = jnp.minimum(
        jnp.sum((bends[None, :] <= blocks[:, None]).astype(jnp.int32), axis=1), N_EXPERTS - 1)
    n_used = bends[-1:].astype(jnp.int32)
    owner = block_expert[:, None] == experts[None, :]

    def per_block(table):
        return jnp.sum(jnp.where(owner, table[None, :], 0), axis=1)

    n_valid = jnp.clip(per_block(counts) - (blocks - per_block(bstarts)) * MOE_ROWS,
                       0, MOE_ROWS).astype(jnp.int32)
    present = blocks_per > 0
    ordinal = jnp.cumsum(present.astype(jnp.int32)) - 1
    later = lax.cummin(jnp.where(present, experts, N_EXPERTS), reverse=True)
    succ = jnp.concatenate([later[1:], jnp.full((1,), N_EXPERTS, jnp.int32)])
    succ = jnp.where(succ >= N_EXPERTS, -1, succ)
    stage_slot = (per_block(ordinal) % 2).astype(jnp.int32)
    next_expert = per_block(succ).astype(jnp.int32)
    return (block_expert.astype(jnp.int32), n_used, n_valid, stage_slot, next_expert)


def _pos_embed_2d(n_tokens):
    rows = n_tokens // GRID_W
    r = np.repeat(np.arange(rows), GRID_W).astype(np.float32)
    col = np.tile(np.arange(GRID_W), rows).astype(np.float32)
    quarter = D_MODEL // 4
    omega = (np.float32(1.0)
             / np.power(np.float32(10000.0), np.arange(quarter, dtype=np.float32) / np.float32(quarter)))
    er = (r[:, None] * omega).astype(np.float64)
    ec = (col[:, None] * omega).astype(np.float64)
    table = np.concatenate([np.sin(er), np.cos(er), np.sin(ec), np.cos(ec)], axis=-1)
    return jnp.asarray(table, dtype=F32)


def _split_w_in(w):
    wt = w.T
    o_r = 2 * GLA_DK + 2 * GLA_DV
    o_f = o_r + DECAY_RANK
    o_gate = o_f + FNET_DIM
    row_scale = jnp.where(jnp.arange(o_r) < GLA_DK, DK_HEAD ** -0.5, 1.0).astype(w.dtype)
    w_main = (wt[:o_r] * row_scale[:, None]).astype(BF16)
    pad = jnp.zeros((LANES - DECAY_RANK, w.shape[0]), BF16)
    w_tail = jnp.concatenate([wt[o_gate:].astype(BF16), wt[o_f:o_gate].astype(BF16),
                              wt[o_r:o_f].astype(BF16), pad], axis=0)
    return w_main, w_tail


def kernel(x_prompt, x_sample, state_gla, c, c_ctx, w_ada, b_ada, w_in, w_dec_fwd, b_dec_fwd,
           w_dec_bwd, b_dec_bwd, gla_norm_g, w_br_gla, w_br_fnet, w_out, ln1_g, ln1_b, w_router,
           b_router, w_gate_up, b_gate_up, w_down, b_down, ln2_g, ln2_b):
    n_req, ctx_len, _ = x_prompt.shape
    n_lat, lat_len, _ = x_sample.shape
    depth = w_in.shape[0]
    alpha = (2.0 * depth) ** 0.25
    t_ctx = n_req * ctx_len
    t_lat = n_lat * lat_len
    t_all = t_ctx + t_lat
    tile_rows = max(TM_PROJ, TM_MERGE, TM_MIX)
    assert t_ctx % tile_rows == 0 and lat_len % tile_rows == 0
    assert ctx_len % GLA_CHUNK == 0 and lat_len % GLA_CHUNK == 0 and t_all % TM_ROUTE == 0
    assert t_ctx % (SC_WORKERS * SC_WINDOW * 2) == 0 and t_lat % (SC_WORKERS * SC_WINDOW * 2) == 0
    assert MOE_ROWS & (MOE_ROWS - 1) == 0 and (t_all * TOP_K) % MOE_ROWS == 0

    x_ctx = x_prompt.reshape(t_ctx, D_MODEL)
    x_lat = x_sample.reshape(t_lat, D_MODEL)
    pos = _pos_embed_2d(lat_len)
    zero_pos = jnp.zeros_like(pos)

    cond_rows = -(-(n_lat + 1) // SUBLANES) * SUBLANES
    cond = jnp.zeros((cond_rows, D_MODEL), F32).at[:n_lat].set(c).at[cond_rows - 1].set(c_ctx)

    n_moe_blocks = (t_all * TOP_K) // MOE_ROWS + N_EXPERTS
    tok_chunks = t_all // (SC_WORKERS * SC_WINDOW)
    states = []
    for l in range(depth):
        mod = _ada(cond, w_ada[l], b_ada[l]).reshape(cond_rows, N_MOD, D_MODEL)
        layer_pos = pos if l == 0 else zero_pos
        w_dec = jnp.zeros((2, LANES, GLA_DK), F32)
        w_dec = w_dec.at[0, :DECAY_RANK].set(w_dec_fwd[l]).at[1, :DECAY_RANK].set(w_dec_bwd[l])
        b_dec = jnp.stack([b_dec_fwd[l], b_dec_bwd[l]]).reshape(2, 1, GLA_DK)
        proj, la, la_min = _inproj(x_ctx, x_lat, layer_pos, mod, *_split_w_in(w_in[l]), w_dec, b_dec,
                                   lat_len)

        norm_g = gla_norm_g[l].reshape(1, DV_HEAD)
        s0_lat = state_gla[:, l]

        def gla_both(exact):
            o_c, s_c = _gla(proj, la, norm_g, None, n_seq=n_req, seq_len=ctx_len, row0=0,
                            emit_state=True, exact=exact)
            (o_l,) = _gla(proj, la, norm_g, s0_lat, n_seq=n_lat, seq_len=lat_len, row0=t_ctx,
                          emit_state=False, exact=exact)
            return o_c, s_c, o_l

        fast_ok = jnp.min(la_min) >= -GLA_FAST_MAX_STEP_DECAY
        o_ctx, s_new, o_lat = lax.cond(fast_ok, lambda: gla_both(False), lambda: gla_both(True))
        states.append(s_new)

        mixed_ctx = _fnet(proj, n_seq=n_req, seq_len=ctx_len, row0=0)
        mixed_lat = _fnet(proj, n_seq=n_lat, seq_len=lat_len, row0=t_ctx)

        wr = jnp.zeros((D_MODEL, LANES), F32).at[:, :N_EXPERTS].set(w_router[l])
        br = jnp.full((1, LANES), -1e30, F32).at[0, :N_EXPERTS].set(b_router[l])
        wr_hi = wr.astype(BF16)
        wr_lo = (wr - wr_hi.astype(F32)).astype(BF16)
        x1, h2, ridx, rw = _merge(
            x_ctx, x_lat, layer_pos, mod, o_ctx, o_lat, mixed_ctx, mixed_lat, proj,
            w_br_gla[l].astype(BF16), w_br_fnet[l].astype(BF16), (0.5 * w_out[l]).astype(BF16),
            ln1_g[l].reshape(1, D_MODEL), ln1_b[l].reshape(1, D_MODEL), wr_hi, wr_lo, br, lat_len,
            alpha)

        dest, counts = _route(ridx)
        moe_tables = _routing_tables(counts[0, :N_EXPERTS].astype(jnp.int32), n_moe_blocks)
        dest = dest[:TOP_K]
        scatter_idx = dest.reshape(TOP_K, SC_WORKERS, tok_chunks, SC_WINDOW)
        p_rows = n_moe_blocks * MOE_ROWS
        xs = _sc_scatter_rows(h2.reshape(t_all, ROW_TILES, LANES), scatter_idx, p_rows)
        yb = _moe(moe_tables, xs.reshape(p_rows * ROW_TILES, LANES),
                  w_gate_up[l], b_gate_up[l], w_down[l], b_down[l])
        yb = yb.reshape(p_rows, ROW_TILES, LANES)

        def gathered(row0, n_rows):
            idx = dest[:, row0:row0 + n_rows].reshape(SC_WORKERS, -1, SC_WINDOW)
            return _sc_gather_rows(yb, idx).reshape(TOP_K * n_rows * ROW_TILES, LANES)

        l2g = ln2_g[l].reshape(1, D_MODEL)
        l2b = ln2_b[l].reshape(1, D_MODEL)
        tiles_per_seq = lat_len // TM_MIX
        yg_ctx = gathered(0, t_ctx)
        yg_lat = gathered(t_ctx, t_lat)
        x_ctx = _combine(x1, yg_ctx, rw, mod, l2g, l2b, row0=0, n_rows=t_ctx,
                         mod_map=lambda i: (cond_rows - 1, 0, 0), alpha=alpha)
        x_lat = _combine(x1, yg_lat, rw, mod, l2g, l2b, row0=t_ctx, n_rows=t_lat,
                         mod_map=lambda i: (i // tiles_per_seq, 0, 0), alpha=alpha)

    y_prompt = x_ctx.reshape(x_prompt.shape)
    y_sample = x_lat.reshape(x_sample.shape)
    new_state = jnp.stack(states, axis=1).astype(x_prompt.dtype)
    return (y_prompt, y_sample, new_state)
```

```python
import functools
import math

import numpy as np
import jax
import jax.numpy as jnp
from jax import lax
from jax.experimental import pallas as pl
from jax.experimental.pallas import tpu as pltpu
from jax.experimental.pallas import tpu_sc as plsc

F32 = jnp.float32
BF16 = jnp.bfloat16

D_MODEL = 1024
GRID_W = 64
GLA_HEADS = 4
DK_HEAD = 128
DV_HEAD = 256
GLA_DK = GLA_HEADS * DK_HEAD
GLA_DV = GLA_HEADS * DV_HEAD
DECAY_RANK = 16
GATE_NORMALIZER = 16.0
FNET_GROUPS = 4
FNET_GROUP_DIM = 128
FNET_DIM = FNET_GROUPS * FNET_GROUP_DIM
N_EXPERTS = 32
TOP_K = 4
D_EXPERT = 1024
SWIGLU_LIMIT = 7.0
SWIGLU_ALPHA = 1.702
LN_EPS = 1e-6
N_MOD = 6

LANES = 128
SUBLANES = 8
HALF_MODEL = D_MODEL // 2
ROW_TILES = HALF_MODEL // LANES
ROW_DTYPE = jnp.uint32
COL_Q = 0
COL_K = GLA_DK
COL_V = 2 * GLA_DK
COL_G = COL_V + GLA_DV
COL_GATE_A = COL_G + GLA_DV
COL_GATE_B = COL_GATE_A + D_MODEL
COL_F = COL_GATE_B + D_MODEL
COL_R = COL_F + FNET_DIM
PROJ_COLS = COL_R + LANES

GLA_CHUNK = 128
GLA_LEAF = 16
GLA_UNROLL = 8
FNET_ROWS = 1024
assert GLA_CHUNK == DK_HEAD
GLA_FAST_MAX_STEP_DECAY = 8.0
TM_PROJ = 512
PROJ_GROUP = 256
TM_MIX = 512
TM_MERGE = 512
TM_MERGE_GROUP = 256
TM_ROUTE = 1024
MOE_ROWS = 512
MOE_STEP = 64
VMEM_LIMIT = 56 * 1024 * 1024

SC_CORES = 2
SC_SUBCORES = 16
SC_WORKERS = SC_CORES * SC_SUBCORES
SC_WINDOW = 64

HIGHEST = lax.Precision.HIGHEST


def _layer_norm(x):
    mu = jnp.mean(x, axis=-1, keepdims=True)
    xc = x - mu
    var = jnp.mean(xc * xc, axis=-1, keepdims=True)
    return xc * lax.rsqrt(var + LN_EPS)


def _sigmoid(x):
    return 0.5 * jnp.tanh(0.5 * x) + 0.5


def _log_sigmoid(z):
    return jnp.minimum(z, 0.0) - jnp.log(1.0 + jnp.exp(-jnp.abs(z)))


def _dot(a, b):
    return jnp.dot(a, b, preferred_element_type=F32)


def _split_bf16(x, terms):
    parts = []
    for _ in range(terms):
        p = x.astype(BF16)
        parts.append(p)
        x = x - p.astype(F32)
    return parts


def _dot_nt(a, b):
    return lax.dot_general(a, b, (((1,), (1,)), ((), ())), preferred_element_type=F32)


def _dot_tn(a, b):
    return lax.dot_general(a, b, (((0,), (0,)), ((), ())), preferred_element_type=F32)


def _row_tile_slice(j, n_rows, first_row=0):
    return pl.ds(first_row * ROW_TILES + j, n_rows, stride=ROW_TILES)


def _store_row_tiles(ref, val, first_row=0):
    for j in range(ROW_TILES):
        lo = val[:, j * LANES:(j + 1) * LANES]
        hi = val[:, HALF_MODEL + j * LANES:HALF_MODEL + (j + 1) * LANES]
        ref[_row_tile_slice(j, val.shape[0], first_row), :] = pltpu.pack_elementwise(
            [lo, hi], packed_dtype=BF16)


def _load_row_tile(ref, j, n_rows, first_row=0):
    words = ref[_row_tile_slice(j, n_rows, first_row), :]
    return tuple(pltpu.unpack_elementwise(words, index=half, packed_dtype=BF16, unpacked_dtype=F32)
                 for half in range(2))


def _ada_kernel(c_ref, w_ref, b_ref, o_ref):
    c = c_ref[...]
    s = c * _sigmoid(c)
    o_ref[...] = _dot(s.astype(BF16), w_ref[...].astype(BF16)) + b_ref[...]


def _ada(cond, w_ada, b_ada):
    rows = cond.shape[0]
    n = w_ada.shape[1]
    tn = 1536
    return pl.pallas_call(
        _ada_kernel,
        grid=(n // tn,),
        in_specs=[pl.BlockSpec((rows, D_MODEL), lambda j: (0, 0)),
                  pl.BlockSpec((D_MODEL, tn), lambda j: (0, j)),
                  pl.BlockSpec((1, tn), lambda j: (0, j))],
        out_specs=pl.BlockSpec((rows, tn), lambda j: (0, j)),
        out_shape=jax.ShapeDtypeStruct((rows, n), F32),
        compiler_params=pltpu.CompilerParams(vmem_limit_bytes=VMEM_LIMIT),
        name="ada_mod",
    )(cond, w_ada, b_ada.reshape(1, n))


def _group_maps(n_ctx_tiles):
    def ctx_map(i, *_):
        return (jnp.minimum(i, n_ctx_tiles - 1), 0)

    def lat_map(i, *_):
        return (jnp.maximum(i - n_ctx_tiles, 0), 0)

    return ctx_map, lat_map


def _token_specs(tm, n_ctx_tiles, tiles_per_latent_seq, ctx_mod_row):
    ctx_map, lat_map = _group_maps(n_ctx_tiles)

    def pos_map(i, *_):
        return (jnp.maximum(i - n_ctx_tiles, 0) % tiles_per_latent_seq, 0)

    def mod_map(i, *_):
        return (jnp.where(i < n_ctx_tiles, ctx_mod_row,
                          jnp.maximum(i - n_ctx_tiles, 0) // tiles_per_latent_seq), 0, 0)

    return [pl.BlockSpec((tm, D_MODEL), ctx_map),
            pl.BlockSpec((tm, D_MODEL), lat_map),
            pl.BlockSpec((tm, D_MODEL), pos_map),
            pl.BlockSpec((1, N_MOD, D_MODEL), mod_map)]


def _inproj_kernel(xc_ref, xl_ref, pos_ref, mod_ref, wm_ref, wt_ref, wdh_ref, wdl_ref, bd_ref,
                   o_ref, la_ref, lamin_ref, *, n_ctx_tiles):
    tail = lambda lo, hi: wt_ref[lo - COL_GATE_A:hi - COL_GATE_A, :]
    i = pl.program_id(0)

    def project(x_of):
        groups = [slice(r0, r0 + PROJ_GROUP) for r0 in range(0, TM_PROJ, PROJ_GROUP)]
        hs = [(_layer_norm(x_of(rows)) * (1.0 + mod_ref[0, 1:2, :]) + mod_ref[0, 0:1, :]).astype(BF16)
              for rows in groups]
        rs = [_split_bf16(_dot_nt(h, tail(COL_R, PROJ_COLS)), 2) for h in hs]
        steepest = None
        for rows, (r_hi, r_lo) in zip(groups, rs):
            for d in range(2):
                z = (_dot(r_hi, wdh_ref[d]) + _dot(r_lo, wdh_ref[d]) + _dot(r_hi, wdl_ref[d])
                     + bd_ref[d])
                la = _log_sigmoid(z) * (1.0 / GATE_NORMALIZER)
                la_ref[rows, d * GLA_DK:(d + 1) * GLA_DK] = la
                low = jnp.min(la, axis=0, keepdims=True)
                for c0 in range(0, GLA_DK, LANES):
                    piece = low[:, c0:c0 + LANES]
                    steepest = piece if steepest is None else jnp.minimum(steepest, piece)
        lamin_ref[...] = jnp.broadcast_to(steepest, (SUBLANES, LANES))
        for rows, h in zip(groups, hs):
            hg = 0.5 * _dot_nt(h, wm_ref[COL_G:COL_GATE_A, :])
            o_ref[rows, COL_G:COL_GATE_A] = hg * (jnp.tanh(hg) + 1.0)
        for rows, h in zip(groups, hs):
            gates = _dot_nt(h, tail(COL_GATE_A, COL_F))
            o_ref[rows, COL_GATE_A:COL_F] = jnp.tanh(0.5 * gates) + 1.0
        for rows, h in zip(groups, hs):
            o_ref[rows, :COL_G] = _dot_nt(h, wm_ref[:COL_G, :])
        for rows, h in zip(groups, hs):
            o_ref[rows, COL_F:COL_R] = _dot_nt(h, tail(COL_F, COL_R))

    @pl.when(i < n_ctx_tiles)
    def _():
        project(lambda rows: xc_ref[rows, :])

    @pl.when(i >= n_ctx_tiles)
    def _():
        project(lambda rows: xl_ref[rows, :] + pos_ref[rows, :])


def _inproj(x_ctx, x_lat, pos, mod, w_main, w_tail, w_dec, b_dec, lat_len):
    t_ctx, t_lat = x_ctx.shape[0], x_lat.shape[0]
    t_all = t_ctx + t_lat
    n_ctx_tiles = t_ctx // TM_PROJ
    kern = functools.partial(_inproj_kernel, n_ctx_tiles=n_ctx_tiles)
    specs = _token_specs(TM_PROJ, n_ctx_tiles, lat_len // TM_PROJ, mod.shape[0] - 1)
    w_dec_hi = w_dec.astype(BF16)
    w_dec_lo = (w_dec - w_dec_hi.astype(F32)).astype(BF16)

    def const(shape):
        return pl.BlockSpec(shape, lambda i: (0,) * len(shape))

    return pl.pallas_call(
        kern,
        grid=(t_all // TM_PROJ,),
        in_specs=specs + [pl.BlockSpec(w_main.shape, lambda i: (0, 0), pipeline_mode=pl.Buffered(1)),
                          pl.BlockSpec(w_tail.shape, lambda i: (0, 0), pipeline_mode=pl.Buffered(1)),
                          const((2, LANES, GLA_DK)), const((2, LANES, GLA_DK)), const((2, 1, GLA_DK))],
        out_specs=[pl.BlockSpec((TM_PROJ, COL_R), lambda i: (i, 0)),
                   pl.BlockSpec((TM_PROJ, 2 * GLA_DK), lambda i: (i, 0)),
                   pl.BlockSpec((SUBLANES, LANES), lambda i: (i, 0))],
        out_shape=[jax.ShapeDtypeStruct((t_all, COL_R), F32),
                   jax.ShapeDtypeStruct((t_all, 2 * GLA_DK), F32),
                   jax.ShapeDtypeStruct((t_all // TM_PROJ * SUBLANES, LANES), F32)],
        compiler_params=pltpu.CompilerParams(
            dimension_semantics=("arbitrary",), vmem_limit_bytes=VMEM_LIMIT),
        name="ln_inproj",
    )(x_ctx, x_lat, pos, mod, w_main, w_tail, w_dec_hi, w_dec_lo, b_dec)


def _gla_kernel(*refs, seq_len, seqs, has_s0, emit_state, exact):
    it = iter(refs)
    q_ref, k_ref, v_ref, laf_ref, lab_ref, g_ref = (next(it) for _ in range(6))
    s0_ref = next(it) if has_s0 else None
    o_ref = next(it)
    sout_ref = next(it) if emit_state else None
    cum_ref, aq_ref, ko_ref, dec_ref, op_ref, st_ref = (next(it) for _ in range(6))

    C = GLA_CHUNK
    n_chunks = seq_len // C
    all_chunks = seqs * n_chunks
    assert all_chunks <= GLA_UNROLL or (seqs == 1 and n_chunks % GLA_UNROLL == 0)

    def rows(n):
        if isinstance(n, int):
            return pl.ds(n * C, C)
        return pl.ds(pl.multiple_of(n * C, C), C)

    def loop(body, count=all_chunks):
        if count <= GLA_UNROLL:
            for n in range(count):
                body(n)
        else:
            def step(m, carry):
                for u in range(GLA_UNROLL):
                    body(GLA_UNROLL * m + u)
                return carry
            lax.fori_loop(0, count // GLA_UNROLL, step, 0)

    rt = lax.broadcasted_iota(jnp.int32, (C, C), 0)
    ct = lax.broadcasted_iota(jnp.int32, (C, C), 1)
    tri = ((rt >= ct).astype(BF16), (ct >= rt).astype(BF16))
    row_id = lax.broadcasted_iota(jnp.int32, (C, DK_HEAD), 0)

    def cumsum_chunk(n):
        for d, la_ref in enumerate((laf_ref, lab_ref)):
            la_hi, la_lo = _split_bf16(la_ref[rows(n), :], 2)
            cum_ref[d, rows(n), :] = _dot(tri[d], la_hi) + _dot(tri[d], la_lo)

    loop(cumsum_chunk)

    query_rows = ({}, {})
    keep = ({}, {})
    for d in range(2):
        blk = C // 2
        while blk >= GLA_LEAF:
            q_parity = 1 if d == 0 else 0
            query_rows[d][blk] = ((row_id // blk) % 2) == q_parity
            qb, kb = rt // blk, ct // blk
            keep[d][blk] = ((qb % 2) == q_parity) & ((qb == kb + 1) if d == 0 else (kb == qb + 1))
            blk //= 2
        order = (rt >= ct) if d == 0 else (ct >= rt)
        keep[d][0] = ((rt // GLA_LEAF) == (ct // GLA_LEAF)) & order

    n_leaves = C // GLA_LEAF

    def per_leaf(rows_of_cum):
        return jnp.concatenate(rows_of_cum, axis=0)

    def expand(per_leaf_rows):
        return jnp.concatenate(
            [jnp.broadcast_to(per_leaf_rows[j:j + 1, :], (GLA_LEAF, DK_HEAD)) for j in range(n_leaves)],
            axis=0)

    def store_scores(n, d, acc, q_in, k_out, end):
        aq_ref[d, rows(n), 0:C] = acc.astype(BF16)
        aq_ref[d, rows(n), C:C + DK_HEAD] = q_in.astype(BF16)
        ko_ref[d, rows(n), :] = k_out.astype(BF16)
        dec_ref[d, rows(n), :] = jnp.broadcast_to(jnp.exp(end), (DK_HEAD, DK_HEAD)).T

    def block_level_scores(cum, q, k, d, blk, at_bnd_scale):
        x = (jnp.where(query_rows[d][blk], q, k) * at_bnd_scale).astype(BF16)
        return jnp.where(keep[d][blk], _dot_nt(x, x), 0.0)

    def boundary_rows(cum, d, blk):
        bnd = blk - 1 if d == 0 else blk
        return per_leaf([cum[bnd + (j * GLA_LEAF) // (2 * blk) * (2 * blk):
                             bnd + (j * GLA_LEAF) // (2 * blk) * (2 * blk) + 1, :]
                         for j in range(n_leaves)])

    def scores_exact(n, d):
        cum = cum_ref[d, rows(n), :]
        q = q_ref[rows(n), :]
        k = k_ref[rows(n), :]
        in_leaf_pos = row_id % GLA_LEAF
        acc = jnp.zeros((C, C), F32)
        for lag in range(GLA_LEAF):
            shift = lag if d == 0 else (C - lag) % C
            k_s = pltpu.roll(k, shift, 0) if lag else k
            cum_s = pltpu.roll(cum, shift, 0) if lag else cum
            paired = (in_leaf_pos >= lag) if d == 0 else (in_leaf_pos < GLA_LEAF - lag)
            term = q * k_s * jnp.exp(jnp.where(paired, cum - cum_s, 0.0))
            s = jnp.sum(jnp.where(paired, term, 0.0), axis=-1, keepdims=True)
            diagonal = (rt - ct == lag) if d == 0 else (ct - rt == lag)
            acc = acc + jnp.where(diagonal, s, 0.0)
        blk = C // 2
        while blk >= GLA_LEAF:
            w = jnp.exp(-jnp.abs(cum - expand(boundary_rows(cum, d, blk))))
            acc = acc + block_level_scores(cum, q, k, d, blk, w)
            blk //= 2
        end = cum[C - 1:C, :] if d == 0 else cum[0:1, :]
        store_scores(n, d, acc, q * jnp.exp(cum), k * jnp.exp(end - cum), end)

    def scores(n, d):
        cum = cum_ref[d, rows(n), :]
        mid = GLA_LEAF // 2 - 1 if d == 0 else GLA_LEAF // 2
        at_mid = per_leaf([cum[mid + j * GLA_LEAF:mid + j * GLA_LEAF + 1, :] for j in range(n_leaves)])
        e = cum - expand(at_mid)
        qe = q_ref[rows(n), :] * jnp.exp(e)
        ke = k_ref[rows(n), :] * jnp.exp(-e)
        acc = jnp.where(keep[d][0], _dot_nt(qe.astype(BF16), ke.astype(BF16)), 0.0)
        blk = C // 2
        while blk >= GLA_LEAF:
            through = expand(jnp.exp(-jnp.abs(at_mid - boundary_rows(cum, d, blk))))
            acc = acc + block_level_scores(cum, qe, ke, d, blk, through)
            blk //= 2
        end = cum[C - 1:C, :] if d == 0 else cum[0:1, :]
        store_scores(n, d, acc, qe * expand(jnp.exp(at_mid)), ke * expand(jnp.exp(end - at_mid)), end)

    def scores_chunk(n):
        for d in range(2):
            (scores_exact if exact else scores)(n, d)

    loop(scores_chunk)

    for s in range(seqs):
        for d in range(2):
            if has_s0:
                st_ref[2 * s + d] = s0_ref[s, d, 0]
            else:
                st_ref[2 * s + d] = jnp.zeros((DK_HEAD, DV_HEAD), F32)

    def scan(s, m, d):
        n = s * n_chunks + (m if d == 0 else n_chunks - 1 - m)
        v = v_ref[rows(n), :].astype(BF16)
        st = st_ref[2 * s + d]
        op_ref[d, rows(n), :] = _dot(aq_ref[d, rows(n), :],
                                     jnp.concatenate([v, st.astype(BF16)], axis=0))
        dec = dec_ref[d, rows(n), :]
        st_ref[2 * s + d] = (st * jnp.concatenate([dec] * (DV_HEAD // DK_HEAD), axis=1)
                             + _dot_tn(ko_ref[d, rows(n), :], v))

    def scan_step(m):
        for s in range(seqs):
            for d in range(2):
                scan(s, m, d)

    loop(scan_step, n_chunks)
    if emit_state:
        for s in range(seqs):
            for d in range(2):
                sout_ref[s, d, 0] = st_ref[2 * s + d]

    g = g_ref[...]

    def finish_chunk(n):
        o = op_ref[0, rows(n), :] + op_ref[1, rows(n), :]
        ms = jnp.mean(o * o, axis=-1, keepdims=True)
        o_ref[rows(n), :] = o * lax.rsqrt(ms + LN_EPS) * g

    loop(finish_chunk)


def _gla(proj, la, g, s0, *, n_seq, seq_len, row0, emit_state, exact):
    has_s0 = s0 is not None
    seqs = max(1, min(n_seq, GLA_UNROLL // max(1, seq_len // GLA_CHUNK)))
    assert n_seq % seqs == 0 and row0 % (seqs * seq_len) == 0
    blk_rows = seqs * seq_len
    blk0 = row0 // blk_rows
    kern = functools.partial(_gla_kernel, seq_len=seq_len, seqs=seqs, has_s0=has_s0,
                             emit_state=emit_state, exact=exact)
    in_specs = [
        pl.BlockSpec((blk_rows, DK_HEAD), lambda b, h: (blk0 + b, COL_Q // DK_HEAD + h)),
        pl.BlockSpec((blk_rows, DK_HEAD), lambda b, h: (blk0 + b, COL_K // DK_HEAD + h)),
        pl.BlockSpec((blk_rows, DV_HEAD), lambda b, h: (blk0 + b, COL_V // DV_HEAD + h)),
        pl.BlockSpec((blk_rows, DK_HEAD), lambda b, h: (blk0 + b, h)),
        pl.BlockSpec((blk_rows, DK_HEAD), lambda b, h: (blk0 + b, GLA_HEADS + h)),
        pl.BlockSpec((1, DV_HEAD), lambda b, h: (0, 0)),
    ]
    args = [proj, proj, proj, la, la, g]
    state_spec = pl.BlockSpec((seqs, 2, 1, DK_HEAD, DV_HEAD), lambda b, h: (b, 0, h, 0, 0))
    if has_s0:
        in_specs.append(state_spec)
        args.append(s0)
    out_specs = [pl.BlockSpec((blk_rows, DV_HEAD), lambda b, h: (b, h))]
    out_shape = [jax.ShapeDtypeStruct((n_seq * seq_len, GLA_DV), F32)]
    if emit_state:
        out_specs.append(state_spec)
        out_shape.append(jax.ShapeDtypeStruct((n_seq, 2, GLA_HEADS, DK_HEAD, DV_HEAD), F32))

    res = pl.pallas_call(
        kern,
        grid=(n_seq // seqs, GLA_HEADS),
        in_specs=in_specs,
        out_specs=out_specs,
        out_shape=out_shape,
        scratch_shapes=[pltpu.VMEM((2, blk_rows, DK_HEAD), F32),
                        pltpu.VMEM((2, blk_rows, GLA_CHUNK + DK_HEAD), BF16),
                        pltpu.VMEM((2, blk_rows, DK_HEAD), BF16),
                        pltpu.VMEM((2, blk_rows, DK_HEAD), F32),
                        pltpu.VMEM((2, blk_rows, DV_HEAD), F32),
                        pltpu.VMEM((2 * seqs, DK_HEAD, DV_HEAD), F32)],
        compiler_params=pltpu.CompilerParams(
            dimension_semantics=("arbitrary", "arbitrary"), vmem_limit_bytes=VMEM_LIMIT),
        name="gla%s_seq%d" % ("_exact" if exact else "", seq_len),
    )(*args)
    return res


def _fnet_kernel(f_ref, cl_ref, sl_ref, cg_ref, sg_ref, o_ref, uc_ref, us_ref, *, seq_len, seqs):
    cg = cg_ref[...]
    sg = sg_ref[...]
    for grp in range(FNET_GROUPS):
        lo = grp * FNET_GROUP_DIM
        u = f_ref[:, lo:lo + FNET_GROUP_DIM].astype(BF16)
        uc_ref[:, lo:lo + FNET_GROUP_DIM] = _dot(u, cg).astype(BF16)
        us_ref[:, lo:lo + FNET_GROUP_DIM] = _dot(u, sg).astype(BF16)
    scale = 1.0 / math.sqrt(seq_len * FNET_GROUP_DIM)
    for s in range(seqs):
        r = slice(s * seq_len, (s + 1) * seq_len)
        o_ref[r, :] = (_dot(cl_ref[...], uc_ref[r, :]) - _dot(sl_ref[...], us_ref[r, :])) * scale


def _dft_mats(n):
    j = np.arange(n, dtype=np.int64)
    ang = (2.0 * np.pi / n) * ((j[:, None] * j[None, :]) % n).astype(np.float64)
    return (jnp.asarray(np.cos(ang), dtype=F32).astype(BF16),
            jnp.asarray(np.sin(ang), dtype=F32).astype(BF16))


def _fnet(proj, *, n_seq, seq_len, row0):
    seqs = max(1, min(n_seq, FNET_ROWS // seq_len))
    assert n_seq % seqs == 0 and row0 % (seqs * seq_len) == 0
    blk_rows = seqs * seq_len
    blk0 = row0 // blk_rows
    cl, sl = _dft_mats(seq_len)
    cg, sg = _dft_mats(FNET_GROUP_DIM)
    kern = functools.partial(_fnet_kernel, seq_len=seq_len, seqs=seqs)
    return pl.pallas_call(
        kern,
        grid=(n_seq // seqs,),
        in_specs=[pl.BlockSpec((blk_rows, FNET_DIM), lambda b: (blk0 + b, COL_F // FNET_DIM)),
                  pl.BlockSpec((seq_len, seq_len), lambda b: (0, 0)),
                  pl.BlockSpec((seq_len, seq_len), lambda b: (0, 0)),
                  pl.BlockSpec((FNET_GROUP_DIM, FNET_GROUP_DIM), lambda b: (0, 0)),
                  pl.BlockSpec((FNET_GROUP_DIM, FNET_GROUP_DIM), lambda b: (0, 0))],
        out_specs=pl.BlockSpec((blk_rows, FNET_DIM), lambda b: (b, 0)),
        out_shape=jax.ShapeDtypeStruct((n_seq * seq_len, FNET_DIM), F32),
        scratch_shapes=[pltpu.VMEM((blk_rows, FNET_DIM), BF16),
                        pltpu.VMEM((blk_rows, FNET_DIM), BF16)],
        compiler_params=pltpu.CompilerParams(
            dimension_semantics=("arbitrary",), vmem_limit_bytes=VMEM_LIMIT),
        name="fnet_seq%d" % seq_len,
    )(proj, cl, sl, cg, sg)


def _merge_kernel(xc_ref, xl_ref, pos_ref, mod_ref, oc_ref, ol_ref, mc_ref, ml_ref,
                  g_ref, ga_ref, gb_ref, wbg_ref, wbf_ref, wo_ref, l1g_ref, l1b_ref, wrh_ref, wrl_ref,
                  br_ref, x1_ref, h2_ref, ridx_ref, rw_ref, *, n_ctx_tiles, alpha):
    i = pl.program_id(0)

    tm = x1_ref.shape[0]
    groups = [slice(r0, r0 + TM_MERGE_GROUP) for r0 in range(0, tm, TM_MERGE_GROUP)]
    shape = (TM_MERGE_GROUP, LANES)
    lane_i = lax.broadcasted_iota(jnp.int32, shape, 1)
    lane = lane_i.astype(F32)

    def compute(x_of, o_ref, mx_ref):
        branch = []
        for rows in groups:
            a = (o_ref[rows, :] * g_ref[rows, :]).astype(BF16)
            branch.append((_dot(a, wbg_ref[...]), _dot(mx_ref[rows, :].astype(BF16), wbf_ref[...])))
        mix = []
        for rows, (gla_out, fnet_out) in zip(groups, branch):
            merged = ga_ref[rows, :] * gla_out + gb_ref[rows, :] * fnet_out
            mix.append(_dot(merged.astype(BF16), wo_ref[...]))
        logits = []
        for rows, mix_g in zip(groups, mix):
            y = alpha * x_of(rows) + mod_ref[0, 2:3, :] * mix_g
            x1 = _layer_norm(y) * l1g_ref[...] + l1b_ref[...]
            x1_ref[rows, :] = x1
            h2 = _layer_norm(x1) * (1.0 + mod_ref[0, 4:5, :]) + mod_ref[0, 3:4, :]
            _store_row_tiles(h2_ref, h2, rows.start)
            h_hi, h_lo = _split_bf16(h2, 2)
            logits.append(_dot(h_hi, wrh_ref[...]) + _dot(h_lo, wrh_ref[...])
                          + _dot(h_hi, wrl_ref[...]) + br_ref[...])
        idx_out = [jnp.zeros(shape, F32) for _ in groups]
        val_out = [jnp.zeros(shape, F32) for _ in groups]
        top0 = [None] * len(groups)
        denom = [None] * len(groups)
        for kk in range(TOP_K):
            for gi in range(len(groups)):
                m = jnp.max(logits[gi], axis=-1, keepdims=True)
                sel = jnp.min(jnp.where(logits[gi] == m, lane, float(LANES)), axis=-1, keepdims=True)
                if kk == 0:
                    top0[gi] = m
                    p = jnp.ones_like(m)
                    denom[gi] = p
                else:
                    p = jnp.exp(m - top0[gi])
                    denom[gi] = denom[gi] + p
                idx_out[gi] = jnp.where(lane_i == kk, sel, idx_out[gi])
                val_out[gi] = jnp.where(lane_i == kk, p, val_out[gi])
                logits[gi] = jnp.where(lane == sel, -jnp.inf, logits[gi])
        for gi, rows in enumerate(groups):
            ridx_ref[rows, :] = idx_out[gi].astype(jnp.int32)
            rw_ref[rows, :] = val_out[gi] / denom[gi]

    @pl.when(i < n_ctx_tiles)
    def _():
        compute(lambda rows: xc_ref[rows, :], oc_ref, mc_ref)

    @pl.when(i >= n_ctx_tiles)
    def _():
        compute(lambda rows: xl_ref[rows, :] + pos_ref[rows, :], ol_ref, ml_ref)


def _merge(x_ctx, x_lat, pos, mod, o_ctx, o_lat, mixed_ctx, mixed_lat, proj,
           wbg, wbf, wo, l1g, l1b, wr_hi, wr_lo, br, lat_len, alpha):
    t_ctx, t_lat = x_ctx.shape[0], x_lat.shape[0]
    t_all = t_ctx + t_lat
    tm = TM_MERGE
    n_ctx_tiles = t_ctx // tm
    kern = functools.partial(_merge_kernel, n_ctx_tiles=n_ctx_tiles, alpha=alpha)
    specs = _token_specs(tm, n_ctx_tiles, lat_len // tm, mod.shape[0] - 1)
    ctx_map, lat_map = _group_maps(n_ctx_tiles)

    def const(shape):
        return pl.BlockSpec(shape, lambda i: (0,) * len(shape))

    in_specs = specs + [
        pl.BlockSpec((tm, GLA_DV), ctx_map),
        pl.BlockSpec((tm, GLA_DV), lat_map),
        pl.BlockSpec((tm, FNET_DIM), ctx_map),
        pl.BlockSpec((tm, FNET_DIM), lat_map),
        pl.BlockSpec((tm, GLA_DV), lambda i: (i, COL_G // GLA_DV)),
        pl.BlockSpec((tm, D_MODEL), lambda i: (i, COL_GATE_A // D_MODEL)),
        pl.BlockSpec((tm, D_MODEL), lambda i: (i, COL_GATE_B // D_MODEL)),
        const((GLA_DV, D_MODEL)), const((FNET_DIM, D_MODEL)), const((D_MODEL, D_MODEL)),
        const((1, D_MODEL)), const((1, D_MODEL)),
        const((D_MODEL, LANES)), const((D_MODEL, LANES)), const((1, LANES)),
    ]
    out_specs = [pl.BlockSpec((tm, D_MODEL), lambda i: (i, 0)),
                 pl.BlockSpec((tm * ROW_TILES, LANES), lambda i: (i, 0)),
                 pl.BlockSpec((tm, LANES), lambda i: (i, 0)),
                 pl.BlockSpec((tm, LANES), lambda i: (i, 0))]
    out_shape = [jax.ShapeDtypeStruct((t_all, D_MODEL), F32),
                 jax.ShapeDtypeStruct((t_all * ROW_TILES, LANES), ROW_DTYPE),
                 jax.ShapeDtypeStruct((t_all, LANES), jnp.int32),
                 jax.ShapeDtypeStruct((t_all, LANES), F32)]
    return pl.pallas_call(
        kern,
        grid=(t_all // tm,),
        in_specs=in_specs,
        out_specs=out_specs,
        out_shape=out_shape,
        compiler_params=pltpu.CompilerParams(
            dimension_semantics=("arbitrary",), vmem_limit_bytes=VMEM_LIMIT),
        name="merge_ln1_router",
    )(x_ctx, x_lat, pos, mod, o_ctx, o_lat, mixed_ctx, mixed_lat, proj, proj, proj,
      wbg, wbf, wo, l1g, l1b, wr_hi, wr_lo, br)


def _sc_mesh():
    return plsc.VectorSubcoreMesh(core_axis_name="c", subcore_axis_name="s")


def _sc_worker_id():
    return lax.axis_index("s") * SC_CORES + lax.axis_index("c")


def _sc_scatter_rows(src, idx, n_out):
    n_src = src.shape[0]
    w = SC_WINDOW
    n_chunks = n_src // (SC_WORKERS * w)
    copies = idx.shape[0]
    assert n_chunks % 2 == 0 and idx.shape == (copies, SC_WORKERS, n_chunks, w)

    @functools.partial(
        pl.kernel, mesh=_sc_mesh(),
        out_type=jax.ShapeDtypeStruct((n_out, ROW_TILES, LANES), ROW_DTYPE),
        scratch_types=[pltpu.VMEM((copies * n_chunks, w), jnp.int32),
                       pltpu.VMEM((2, w, ROW_TILES, LANES), ROW_DTYPE),
                       pltpu.SemaphoreType.DMA((2,)),
                       pltpu.SemaphoreType.DMA((2,))],
        name="moe_dispatch_scatter")
    def k(src_hbm, idx_hbm, out_hbm, idx_v, rows_v, rsem, wsem):
        wid = _sc_worker_id()
        base = wid * (n_chunks * w)
        for kk in range(copies):
            pltpu.sync_copy(idx_hbm.at[kk, wid], idx_v.at[pl.ds(kk * n_chunks, n_chunks)])

        def read(j, slot):
            return pltpu.make_async_copy(src_hbm.at[pl.ds(base + j * w, w)], rows_v.at[slot],
                                         rsem.at[slot])

        def scatter(j, kk, slot):
            return pltpu.make_async_copy(rows_v.at[slot], out_hbm.at[idx_v.at[kk * n_chunks + j]],
                                         wsem.at[slot])

        read(0, 0).start()

        @pl.loop(0, n_chunks, step=2)
        def _(jj):
            read(jj, 0).wait()

            @pl.when(jj > 0)
            def _():
                for kk in range(copies):
                    scatter(jj - 1, kk, 1).wait()

            read(jj + 1, 1).start()
            for kk in range(copies):
                scatter(jj, kk, 0).start()
            read(jj + 1, 1).wait()
            for kk in range(copies):
                scatter(jj, kk, 0).wait()

            @pl.when(jj + 2 < n_chunks)
            def _():
                read(jj + 2, 0).start()

            for kk in range(copies):
                scatter(jj + 1, kk, 1).start()

        for kk in range(copies):
            scatter(n_chunks - 1, kk, 1).wait()

    return k(src, idx)


def _sc_gather_rows(table, idx):
    _, n_chunks, w = idx.shape
    assert n_chunks % 2 == 0 and idx.shape[0] == SC_WORKERS and w == SC_WINDOW
    n_out = SC_WORKERS * n_chunks * w

    @functools.partial(
        pl.kernel, mesh=_sc_mesh(),
        out_type=jax.ShapeDtypeStruct((n_out, ROW_TILES, LANES), ROW_DTYPE),
        scratch_types=[pltpu.VMEM((n_chunks, w), jnp.int32),
                       pltpu.VMEM((2, w, ROW_TILES, LANES), ROW_DTYPE),
                       pltpu.SemaphoreType.DMA((2,)),
                       pltpu.SemaphoreType.DMA((2,))],
        name="moe_combine_gather")
    def k(table_hbm, idx_hbm, out_hbm, idx_v, rows_v, gsem, wsem):
        wid = _sc_worker_id()
        base = wid * (n_chunks * w)
        pltpu.sync_copy(idx_hbm.at[wid], idx_v)

        def gather(j, slot):
            return pltpu.make_async_copy(table_hbm.at[idx_v.at[j]], rows_v.at[slot], gsem.at[slot])

        def write(j, slot):
            return pltpu.make_async_copy(rows_v.at[slot], out_hbm.at[pl.ds(base + j * w, w)],
                                         wsem.at[slot])

        gather(0, 0).start()

        @pl.loop(0, n_chunks, step=2)
        def _(jj):
            gather(jj, 0).wait()

            @pl.when(jj > 0)
            def _():
                write(jj - 1, 1).wait()

            gather(jj + 1, 1).start()
            write(jj, 0).start()
            gather(jj + 1, 1).wait()
            write(jj, 0).wait()

            @pl.when(jj + 2 < n_chunks)
            def _():
                gather(jj + 2, 0).start()

            write(jj + 1, 1).start()

        write(n_chunks - 1, 1).wait()

    return k(table, idx)


def _moe_kernel(be_ref, nu_ref, nv_ref, slot_ref, nxt_ref, x_ref, wgu_hbm, bgu_ref, wd_hbm, bd_ref,
                o_ref, wgu_st, wd_st, wgu_bf, wd_bf, xb_ref, sem):
    b = pl.program_id(0)
    e = be_ref[b]
    prev = be_ref[jnp.maximum(b - 1, 0)]
    active = b < nu_ref[0]
    changed = (b == 0) | (e != prev)

    def weight_copies(expert, s):
        return (pltpu.make_async_copy(wgu_hbm.at[expert], wgu_st.at[s], sem.at[0, s]),
                pltpu.make_async_copy(wd_hbm.at[expert], wd_st.at[s], sem.at[1, s]))

    @pl.when(active & changed)
    def _():
        s = slot_ref[b]

        @pl.when(b == 0)
        def _():
            for cp in weight_copies(e, s):
                cp.start()

        for cp in weight_copies(e, s):
            cp.wait()
        wgu_bf[...] = wgu_st[s].astype(BF16)
        wd_bf[...] = wd_st[s].astype(BF16)
        nxt = nxt_ref[b]

        @pl.when(nxt >= 0)
        def _():
            for cp in weight_copies(nxt, 1 - s):
                cp.start()

    n_valid = nv_ref[b]

    def expert_mlp(n_rows):
        valid = lax.broadcasted_iota(jnp.int32, (n_rows, LANES), 0) < n_valid
        for j in range(ROW_TILES):
            for half, xj in enumerate(_load_row_tile(x_ref, j, n_rows)):
                c0 = half * HALF_MODEL + j * LANES
                xb_ref[0:n_rows, c0:c0 + LANES] = jnp.where(valid, xj, 0.0).astype(BF16)
        gu = _dot(xb_ref[0:n_rows, :], wgu_bf[...]) + bgu_ref[0]
        gate = jnp.minimum(gu[:, :D_EXPERT], SWIGLU_LIMIT)
        up = jnp.clip(gu[:, D_EXPERT:], -SWIGLU_LIMIT, SWIGLU_LIMIT)
        glu = gate * _sigmoid(gate * SWIGLU_ALPHA)
        act = ((up + 1.0) * glu).astype(BF16)
        _store_row_tiles(o_ref, _dot(act, wd_bf[...]) + bd_ref[0])

    for n_rows in range(MOE_STEP, MOE_ROWS + 1, MOE_STEP):
        @pl.when(active & (n_valid > n_rows - MOE_STEP) & (n_valid <= n_rows))
        def _(n_rows=n_rows):
            expert_mlp(n_rows)


def _moe(tables, xs, w_gate_up, b_gate_up, w_down, b_down):
    p_rows = xs.shape[0] // ROW_TILES
    n_blocks = p_rows // MOE_ROWS

    def blk(b, be, nu, *_):
        return jnp.minimum(b, nu[0] - 1)

    def expert(b, be, nu, *_):
        return (be[blk(b, be, nu)], 0, 0)

    def rows(b, be, nu, *_):
        return (blk(b, be, nu), 0)

    grid_spec = pltpu.PrefetchScalarGridSpec(
        num_scalar_prefetch=len(tables),
        grid=(n_blocks,),
        in_specs=[
            pl.BlockSpec((MOE_ROWS * ROW_TILES, LANES), rows),
            pl.BlockSpec(memory_space=pl.ANY),
            pl.BlockSpec((1, 1, 2 * D_EXPERT), expert),
            pl.BlockSpec(memory_space=pl.ANY),
            pl.BlockSpec((1, 1, D_MODEL), expert),
        ],
        out_specs=pl.BlockSpec((MOE_ROWS * ROW_TILES, LANES), rows),
        scratch_shapes=[pltpu.VMEM((2, D_MODEL, 2 * D_EXPERT), F32),
                        pltpu.VMEM((2, D_EXPERT, D_MODEL), F32),
                        pltpu.VMEM((D_MODEL, 2 * D_EXPERT), BF16),
                        pltpu.VMEM((D_EXPERT, D_MODEL), BF16),
                        pltpu.VMEM((MOE_ROWS, D_MODEL), BF16),
                        pltpu.SemaphoreType.DMA((2, 2))],
    )
    return pl.pallas_call(
        _moe_kernel,
        grid_spec=grid_spec,
        out_shape=jax.ShapeDtypeStruct((p_rows * ROW_TILES, LANES), ROW_DTYPE),
        compiler_params=pltpu.CompilerParams(
            dimension_semantics=("arbitrary",), vmem_limit_bytes=VMEM_LIMIT),
        name="moe_grouped_mlp",
    )(*tables, xs, w_gate_up, b_gate_up.reshape(N_EXPERTS, 1, 2 * D_EXPERT), w_down,
      b_down.reshape(N_EXPERTS, 1, D_MODEL))


def _combine_kernel(x1_ref, y0_ref, y1_ref, y2_ref, y3_ref, rw_ref, mod_ref, g_ref, b_ref, o_ref,
                    *, alpha):
    rw = rw_ref[...]
    y_refs = (y0_ref, y1_ref, y2_ref, y3_ref)
    pieces = [None] * (2 * ROW_TILES)
    for j in range(ROW_TILES):
        for kk in range(TOP_K):
            for half, yj in enumerate(_load_row_tile(y_refs[kk], j, rw.shape[0])):
                term = rw[:, kk:kk + 1] * yj
                slot = half * ROW_TILES + j
                pieces[slot] = term if kk == 0 else pieces[slot] + term
    ff = jnp.concatenate(pieces, axis=-1)
    y = alpha * x1_ref[...] + mod_ref[0, 5:6, :] * ff
    o_ref[...] = _layer_norm(y) * g_ref[...] + b_ref[...]


def _combine(x1, yg, rw, mod, l2g, l2b, *, row0, n_rows, mod_map, alpha):
    tm = TM_MIX
    t0 = row0 // tm
    tiles = n_rows // tm
    kern = functools.partial(_combine_kernel, alpha=alpha)

    def y_spec(kk):
        return pl.BlockSpec((tm * ROW_TILES, LANES), lambda i: (kk * tiles + i, 0))

    return pl.pallas_call(
        kern,
        grid=(n_rows // tm,),
        in_specs=[pl.BlockSpec((tm, D_MODEL), lambda i: (t0 + i, 0))]
        + [y_spec(kk) for kk in range(TOP_K)]
        + [pl.BlockSpec((tm, LANES), lambda i: (t0 + i, 0)),
           pl.BlockSpec((1, N_MOD, D_MODEL), mod_map),
           pl.BlockSpec((1, D_MODEL), lambda i: (0, 0)),
           pl.BlockSpec((1, D_MODEL), lambda i: (0, 0))],
        out_specs=pl.BlockSpec((tm, D_MODEL), lambda i: (i, 0)),
        out_shape=jax.ShapeDtypeStruct((n_rows, D_MODEL), F32),
        compiler_params=pltpu.CompilerParams(
            dimension_semantics=("arbitrary",), vmem_limit_bytes=VMEM_LIMIT),
        name="combine_ln2",
    )(x1, yg, yg, yg, yg, rw, mod, l2g, l2b)


def _route_kernel(ridx_ref, dest_ref, cnt_ref, run_ref, bst_ref):
    phase = pl.program_id(0)
    i = pl.program_id(1)
    tm = ridx_ref.shape[0]
    ridx = ridx_ref[...]
    lane = lax.broadcasted_iota(jnp.int32, (tm, LANES), 1)
    hits = [ridx[:, kk:kk + 1] == lane for kk in range(TOP_K)]
    chosen = jnp.where(hits[0], 1.0, 0.0)
    for kk in range(1, TOP_K):
        chosen = chosen + jnp.where(hits[kk], 1.0, 0.0)
    colsum = jnp.sum(chosen, axis=0, keepdims=True)

    @pl.when((phase == 0) & (i == 0))
    def _():
        run_ref[...] = jnp.zeros_like(run_ref)

    @pl.when(phase == 0)
    def _():
        run_ref[...] = run_ref[...] + colsum

    @pl.when((phase == 1) & (i == 0))
    def _():
        counts = run_ref[...]
        cnt_ref[...] = counts
        blocks = jnp.floor((counts + (MOE_ROWS - 1.0)) * (1.0 / MOE_ROWS))
        r = lax.broadcasted_iota(jnp.int32, (LANES, LANES), 0)
        c = lax.broadcasted_iota(jnp.int32, (LANES, LANES), 1)
        before = jnp.dot(blocks, (r < c).astype(F32), precision=HIGHEST, preferred_element_type=F32)
        bst_ref[...] = before * float(MOE_ROWS)
        run_ref[...] = jnp.zeros_like(run_ref)

    @pl.when(phase == 1)
    def _():
        rt = lax.broadcasted_iota(jnp.int32, (tm, tm), 0)
        ct = lax.broadcasted_iota(jnp.int32, (tm, tm), 1)
        earlier = _dot((ct < rt).astype(BF16), chosen.astype(BF16))
        row_of = bst_ref[0:1, :] + run_ref[0:1, :] + earlier
        out = jnp.take_along_axis(row_of, ridx, axis=1)
        dest_ref[...] = out.T[0:SUBLANES, :].astype(jnp.int32)
        run_ref[...] = run_ref[...] + colsum


def _route(ridx):
    t_all = ridx.shape[0]
    tm = TM_ROUTE
    return pl.pallas_call(
        _route_kernel,
        grid=(2, t_all // tm),
        in_specs=[pl.BlockSpec((tm, LANES), lambda p, i: (i, 0))],
        out_specs=[pl.BlockSpec((SUBLANES, tm), lambda p, i: (0, i * p)),
                   pl.BlockSpec((SUBLANES, LANES), lambda p, i: (0, 0))],
        out_shape=[jax.ShapeDtypeStruct((SUBLANES, t_all), jnp.int32),
                   jax.ShapeDtypeStruct((SUBLANES, LANES), F32)],
        scratch_shapes=[pltpu.VMEM((SUBLANES, LANES), F32),
                        pltpu.VMEM((SUBLANES, LANES), F32)],
        compiler_params=pltpu.CompilerParams(
            dimension_semantics=("arbitrary", "arbitrary"), vmem_limit_bytes=VMEM_LIMIT),
        name="moe_route",
    )(ridx)


def _routing_tables(counts, n_blocks):
    experts = jnp.arange(N_EXPERTS, dtype=jnp.int32)
    blocks_per = (counts + MOE_ROWS - 1) // MOE_ROWS
    bends = jnp.cumsum(blocks_per)
    bstarts = bends - blocks_per
    blocks = jnp.arange(n_blocks, dtype=jnp.int32)
    block_expert = jnp.minimum(
        jnp.sum((bends[None, :] <= blocks[:, None]).astype(jnp.int32), axis=1), N_EXPERTS - 1)
    n_used = bends[-1:].astype(jnp.int32)
    owner = block_expert[:, None] == experts[None, :]

    def per_block(table):
        return jnp.sum(jnp.where(owner, table[None, :], 0), axis=1)

    n_valid = jnp.clip(per_block(counts) - (blocks - per_block(bstarts)) * MOE_ROWS,
                       0, MOE_ROWS).astype(jnp.int32)
    present = blocks_per > 0
    ordinal = jnp.cumsum(present.astype(jnp.int32)) - 1
    later = lax.cummin(jnp.where(present, experts, N_EXPERTS), reverse=True)
    succ = jnp.concatenate([later[1:], jnp.full((1,), N_EXPERTS, jnp.int32)])
    succ = jnp.where(succ >= N_EXPERTS, -1, succ)
    stage_slot = (per_block(ordinal) % 2).astype(jnp.int32)
    next_expert = per_block(succ).astype(jnp.int32)
    return (block_expert.astype(jnp.int32), n_used, n_valid, stage_slot, next_expert)


def _pos_embed_2d(n_tokens):
    rows = n_tokens // GRID_W
    r = np.repeat(np.arange(rows), GRID_W).astype(np.float32)
    col = np.tile(np.arange(GRID_W), rows).astype(np.float32)
    quarter = D_MODEL // 4
    omega = (np.float32(1.0)
             / np.power(np.float32(10000.0), np.arange(quarter, dtype=np.float32) / np.float32(quarter)))
    er = (r[:, None] * omega).astype(np.float64)
    ec = (col[:, None] * omega).astype(np.float64)
    table = np.concatenate([np.sin(er), np.cos(er), np.sin(ec), np.cos(ec)], axis=-1)
    return jnp.asarray(table, dtype=F32)


def _split_w_in(w):
    wt = w.T
    o_r = 2 * GLA_DK + 2 * GLA_DV
    o_f = o_r + DECAY_RANK
    o_gate = o_f + FNET_DIM
    row_scale = jnp.where(jnp.arange(o_r) < GLA_DK, DK_HEAD ** -0.5, 1.0).astype(w.dtype)
    w_main = (wt[:o_r] * row_scale[:, None]).astype(BF16)
    pad = jnp.zeros((LANES - DECAY_RANK, w.shape[0]), BF16)
    w_tail = jnp.concatenate([wt[o_gate:].astype(BF16), wt[o_f:o_gate].astype(BF16),
                              wt[o_r:o_f].astype(BF16), pad], axis=0)
    return w_main, w_tail


def kernel(x_prompt, x_sample, state_gla, c, c_ctx, w_ada, b_ada, w_in, w_dec_fwd, b_dec_fwd,
           w_dec_bwd, b_dec_bwd, gla_norm_g, w_br_gla, w_br_fnet, w_out, ln1_g, ln1_b, w_router,
           b_router, w_gate_up, b_gate_up, w_down, b_down, ln2_g, ln2_b):
    n_req, ctx_len, _ = x_prompt.shape
    n_lat, lat_len, _ = x_sample.shape
    depth = w_in.shape[0]
    alpha = (2.0 * depth) ** 0.25
    t_ctx = n_req * ctx_len
    t_lat = n_lat * lat_len
    t_all = t_ctx + t_lat
    tile_rows = max(TM_PROJ, TM_MERGE, TM_MIX)
    assert t_ctx % tile_rows == 0 and lat_len % tile_rows == 0
    assert ctx_len % GLA_CHUNK == 0 and lat_len % GLA_CHUNK == 0 and t_all % TM_ROUTE == 0
    assert t_ctx % (SC_WORKERS * SC_WINDOW * 2) == 0 and t_lat % (SC_WORKERS * SC_WINDOW * 2) == 0
    assert MOE_ROWS & (MOE_ROWS - 1) == 0 and (t_all * TOP_K) % MOE_ROWS == 0

    x_ctx = x_prompt.reshape(t_ctx, D_MODEL)
    x_lat = x_sample.reshape(t_lat, D_MODEL)
    pos = _pos_embed_2d(lat_len)
    zero_pos = jnp.zeros_like(pos)

    cond_rows = -(-(n_lat + 1) // SUBLANES) * SUBLANES
    cond = jnp.zeros((cond_rows, D_MODEL), F32).at[:n_lat].set(c).at[cond_rows - 1].set(c_ctx)

    n_moe_blocks = (t_all * TOP_K) // MOE_ROWS + N_EXPERTS
    tok_chunks = t_all // (SC_WORKERS * SC_WINDOW)
    states = []
    for l in range(depth):
        mod = _ada(cond, w_ada[l], b_ada[l]).reshape(cond_rows, N_MOD, D_MODEL)
        layer_pos = pos if l == 0 else zero_pos
        w_dec = jnp.zeros((2, LANES, GLA_DK), F32)
        w_dec = w_dec.at[0, :DECAY_RANK].set(w_dec_fwd[l]).at[1, :DECAY_RANK].set(w_dec_bwd[l])
        b_dec = jnp.stack([b_dec_fwd[l], b_dec_bwd[l]]).reshape(2, 1, GLA_DK)
        proj, la, la_min = _inproj(x_ctx, x_lat, layer_pos, mod, *_split_w_in(w_in[l]), w_dec, b_dec,
                                   lat_len)

        norm_g = gla_norm_g[l].reshape(1, DV_HEAD)
        s0_lat = state_gla[:, l]

        def gla_both(exact):
            o_c, s_c = _gla(proj, la, norm_g, None, n_seq=n_req, seq_len=ctx_len, row0=0,
                            emit_state=True, exact=exact)
            (o_l,) = _gla(proj, la, norm_g, s0_lat, n_seq=n_lat, seq_len=lat_len, row0=t_ctx,
                          emit_state=False, exact=exact)
            return o_c, s_c, o_l

        fast_ok = jnp.min(la_min) >= -GLA_FAST_MAX_STEP_DECAY
        o_ctx, s_new, o_lat = lax.cond(fast_ok, lambda: gla_both(False), lambda: gla_both(True))
        states.append(s_new)

        mixed_ctx = _fnet(proj, n_seq=n_req, seq_len=ctx_len, row0=0)
        mixed_lat = _fnet(proj, n_seq=n_lat, seq_len=lat_len, row0=t_ctx)

        wr = jnp.zeros((D_MODEL, LANES), F32).at[:, :N_EXPERTS].set(w_router[l])
        br = jnp.full((1, LANES), -1e30, F32).at[0, :N_EXPERTS].set(b_router[l])
        wr_hi = wr.astype(BF16)
        wr_lo = (wr - wr_hi.astype(F32)).astype(BF16)
        x1, h2, ridx, rw = _merge(
            x_ctx, x_lat, layer_pos, mod, o_ctx, o_lat, mixed_ctx, mixed_lat, proj,
            w_br_gla[l].astype(BF16), w_br_fnet[l].astype(BF16), (0.5 * w_out[l]).astype(BF16),
            ln1_g[l].reshape(1, D_MODEL), ln1_b[l].reshape(1, D_MODEL), wr_hi, wr_lo, br, lat_len,
            alpha)

        dest, counts = _route(ridx)
        moe_tables = _routing_tables(counts[0, :N_EXPERTS].astype(jnp.int32), n_moe_blocks)
        dest = dest[:TOP_K]
        scatter_idx = dest.reshape(TOP_K, SC_WORKERS, tok_chunks, SC_WINDOW)
        p_rows = n_moe_blocks * MOE_ROWS
        xs = _sc_scatter_rows(h2.reshape(t_all, ROW_TILES, LANES), scatter_idx, p_rows)
        yb = _moe(moe_tables, xs.reshape(p_rows * ROW_TILES, LANES),
                  w_gate_up[l], b_gate_up[l], w_down[l], b_down[l])
        yb = yb.reshape(p_rows, ROW_TILES, LANES)

        def gathered(row0, n_rows):
            idx = dest[:, row0:row0 + n_rows].reshape(SC_WORKERS, -1, SC_WINDOW)
            return _sc_gather_rows(yb, idx).reshape(TOP_K * n_rows * ROW_TILES, LANES)

        l2g = ln2_g[l].reshape(1, D_MODEL)
        l2b = ln2_b[l].reshape(1, D_MODEL)
        tiles_per_seq = lat_len // TM_MIX
        yg_ctx = gathered(0, t_ctx)
        yg_lat = gathered(t_ctx, t_lat)
        x_ctx = _combine(x1, yg_ctx, rw, mod, l2g, l2b, row0=0, n_rows=t_ctx,
                         mod_map=lambda i: (cond_rows - 1, 0, 0), alpha=alpha)
        x_lat = _combine(x1, yg_lat, rw, mod, l2g, l2b, row0=t_ctx, n_rows=t_lat,
                         mod_map=lambda i: (i // tiles_per_seq, 0, 0), alpha=alpha)

    y_prompt = x_ctx.reshape(x_prompt.shape)
    y_sample = x_lat.reshape(x_sample.shape)
    new_state = jnp.stack(states, axis=1).astype(x_prompt.dtype)
    return (y_prompt, y_sample, new_state)
```
